```python
import jax, jax.numpy as jnp
from jax import lax
import numpy as np

D_MODEL = 1024
BATCH = 8
SEQ = 4096
DEPTH = 2

D_FF = 2816
GDN_HEADS = 8
GDN_DK = 128
GDN_DV = 128
GDN_CONV = 4
CHUNK = 64
CNV_CH = 1024
CNV_K = 31
W_Q = GDN_HEADS * GDN_DK
W_K = GDN_HEADS * GDN_DK
W_V = GDN_HEADS * GDN_DV
W_Z = GDN_HEADS * GDN_DV
W_BETA = GDN_HEADS
W_A = GDN_HEADS
W_GLU = 2 * CNV_CH
W_GATE = 2 * D_MODEL
SPLITS = [int(s) for s in np.cumsum([W_Q + W_K + W_V, W_Z, W_BETA, W_A, W_GLU])]
P_IN = W_Q + W_K + W_V + W_Z + W_BETA + W_A + W_GLU + W_GATE
RMS_EPS = 1e-6
LN_EPS = 1e-5

kernel_name = "hybrid_gdn_conformer_macaron_sandwich"


def rmsnorm(x, w):
    xf = x.astype(jnp.float32)
    y = xf * lax.rsqrt(jnp.mean(xf * xf, axis=-1, keepdims=True) + RMS_EPS)
    return (y * w.astype(jnp.float32)).astype(x.dtype)


def layernorm(x, g, b):
    xf = x.astype(jnp.float32)
    mu = jnp.mean(xf, axis=-1, keepdims=True)
    var = jnp.mean(jnp.square(xf - mu), axis=-1, keepdims=True)
    y = (xf - mu) * lax.rsqrt(var + LN_EPS)
    return (y * g.astype(jnp.float32) + b.astype(jnp.float32)).astype(x.dtype)


def l2norm(x):
    return x * lax.rsqrt(jnp.sum(x * x, axis=-1, keepdims=True) + 1e-6)


def causal_dwconv(x, w):
    k, c = w.shape
    return lax.conv_general_dilated(
        x, w[:, None, :].astype(x.dtype), window_strides=(1,), padding=[(k - 1, 0)],
        dimension_numbers=("NWC", "WIO", "NWC"), feature_group_count=c)


def swiglu_ffn(h, w_in, w_out):
    gate, up = jnp.split(h @ w_in, 2, axis=-1)
    return (jax.nn.silu(gate) * up) @ w_out


def chunk_gated_delta_rule(q, k, v, g, beta):
    b, l, h, dk = q.shape
    dv = v.shape[-1]
    n = l // CHUNK

    def to_chunks(t):
        return t.reshape(b, n, CHUNK, h, -1).transpose(0, 3, 1, 2, 4)

    q, k, v = to_chunks(q), to_chunks(k), to_chunks(v)
    g = g.reshape(b, n, CHUNK, h).transpose(0, 3, 1, 2)
    beta = beta.reshape(b, n, CHUNK, h).transpose(0, 3, 1, 2)
    G = jnp.cumsum(g, axis=-1)
    causal = jnp.tril(jnp.ones((CHUNK, CHUNK), dtype=bool))
    strict = jnp.tril(jnp.ones((CHUNK, CHUNK), dtype=bool), k=-1)
    diff = G[..., :, None] - G[..., None, :]
    decay = jnp.where(causal, jnp.exp(jnp.where(causal, diff, 0.0)), 0.0)
    kk = jnp.einsum("bhncd,bhnsd->bhncs", k, k)
    a_mat = jnp.where(strict, kk * decay * beta[..., :, None], 0.0)
    lhs = jnp.eye(CHUNK, dtype=jnp.float32) + a_mat
    rhs = jnp.concatenate([v * beta[..., None], k * (beta * jnp.exp(G))[..., None]], axis=-1)
    sol = lax.linalg.triangular_solve(lhs, rhs, left_side=True, lower=True, unit_diagonal=True)
    u, w = sol[..., :dv], sol[..., dv:]
    qk = jnp.einsum("bhncd,bhnsd->bhncs", q, k) * decay
    q_dec = q * jnp.exp(G)[..., None]
    k_dec = k * jnp.exp(G[..., -1:] - G)[..., None]
    chunk_decay = jnp.exp(G[..., -1])

    xs = tuple(jnp.moveaxis(t, 2, 0) for t in (q_dec, k_dec, u, w, qk, chunk_decay))

    def step(state, inp):
        q_c, k_c, u_c, w_c, qk_c, d_c = inp
        v_new = u_c - jnp.einsum("bhck,bhkv->bhcv", w_c, state)
        o_c = jnp.einsum("bhck,bhkv->bhcv", q_c, state) + jnp.einsum("bhcs,bhsv->bhcv", qk_c, v_new)
        state = state * d_c[..., None, None] + jnp.einsum("bhck,bhcv->bhkv", k_c, v_new)
        return state, o_c

    s0 = jnp.zeros((b, h, dk, dv), jnp.float32)
    _, o = lax.scan(step, s0, xs)
    return o.transpose(1, 0, 3, 2, 4).reshape(b, l, h, dv)


def gated_deltanet(qkv, z, beta_logit, a_logit, conv_w, a_log, dt_bias, norm_w, w_o):
    b, l, _ = qkv.shape
    qkv = jax.nn.silu(causal_dwconv(qkv, conv_w))
    q, k, v = jnp.split(qkv.astype(jnp.float32), [W_Q, W_Q + W_K], axis=-1)
    q = l2norm(q.reshape(b, l, GDN_HEADS, GDN_DK)) * (GDN_DK ** -0.5)
    k = l2norm(k.reshape(b, l, GDN_HEADS, GDN_DK))
    v = v.reshape(b, l, GDN_HEADS, GDN_DV)
    beta = jax.nn.sigmoid(beta_logit.astype(jnp.float32))
    g = -jnp.exp(a_log.astype(jnp.float32)) * jax.nn.softplus(
        a_logit.astype(jnp.float32) + dt_bias.astype(jnp.float32))
    o = chunk_gated_delta_rule(q, k, v, g, beta)
    zf = z.astype(jnp.float32).reshape(b, l, GDN_HEADS, GDN_DV)
    o = rmsnorm(o, norm_w) * jax.nn.silu(zf)
    return o.reshape(b, l, GDN_HEADS * GDN_DV).astype(qkv.dtype) @ w_o


def conformer_conv(glu_in, pw1_b, dw_w, dw_b, ln_g, ln_b, w_o, b_o):
    a, gate = jnp.split(glu_in + pw1_b, 2, axis=-1)
    h = a * jax.nn.sigmoid(gate)
    h = causal_dwconv(h, dw_w) + dw_b
    h = jax.nn.silu(layernorm(h, ln_g, ln_b))
    return h @ w_o + b_o


def _fwd_setup_inputs(seed: int = 0) -> dict:
    key = jax.random.key(seed)
    ks = iter(jax.random.split(key, 40))

    def nrm(shape, scale):
        return jax.random.normal(next(ks), shape, jnp.float32) * scale

    def gain(shape):
        return 1.0 + 0.02 * jax.random.normal(next(ks), shape, jnp.float32)

    L = DEPTH
    inp = {}
    inp["x"] = nrm((BATCH, SEQ, D_MODEL), 1.0)
    inp["ffn1_norm_pre"] = gain((L, D_MODEL))
    inp["ffn1_norm_post"] = gain((L, D_MODEL))
    inp["ffn1_w_in"] = nrm((L, D_MODEL, 2 * D_FF), D_MODEL ** -0.5)
    inp["ffn1_w_out"] = nrm((L, D_FF, D_MODEL), D_FF ** -0.5)
    inp["mix_norm_pre"] = gain((L, D_MODEL))
    inp["mix_norm_post"] = gain((L, D_MODEL))
    inp["mix_w_in"] = nrm((L, D_MODEL, P_IN), D_MODEL ** -0.5)
    inp["gdn_conv_w"] = nrm((L, GDN_CONV, W_Q + W_K + W_V), GDN_CONV ** -0.5)
    inp["gdn_a_log"] = jnp.log(jax.random.uniform(next(ks), (L, GDN_HEADS), jnp.float32, 1.0, 16.0))
    dt = jnp.exp(jax.random.uniform(next(ks), (L, GDN_HEADS), jnp.float32, np.log(1e-3), np.log(1e-1)))
    inp["gdn_dt_bias"] = dt + jnp.log(-jnp.expm1(-dt))
    inp["gdn_norm_w"] = gain((L, GDN_DV))
    inp["gdn_w_o"] = nrm((L, GDN_HEADS * GDN_DV, D_MODEL), (GDN_HEADS * GDN_DV) ** -0.5)
    inp["cnv_pw1_b"] = nrm((L, W_GLU), 0.02)
    inp["cnv_dw_w"] = nrm((L, CNV_K, CNV_CH), CNV_K ** -0.5)
    inp["cnv_dw_b"] = nrm((L, CNV_CH), 0.02)
    inp["cnv_ln_g"] = gain((L, CNV_CH))
    inp["cnv_ln_b"] = nrm((L, CNV_CH), 0.02)
    inp["cnv_w_o"] = nrm((L, CNV_CH, D_MODEL), CNV_CH ** -0.5)
    inp["cnv_b_o"] = nrm((L, D_MODEL), 0.02)
    inp["mix_w_out"] = nrm((L, D_MODEL, D_MODEL), D_MODEL ** -0.5)
    inp["ffn2_norm_pre"] = gain((L, D_MODEL))
    inp["ffn2_norm_post"] = gain((L, D_MODEL))
    inp["ffn2_w_in"] = nrm((L, D_MODEL, 2 * D_FF), D_MODEL ** -0.5)
    inp["ffn2_w_out"] = nrm((L, D_FF, D_MODEL), D_FF ** -0.5)
    return inp


def _fwd_reference(x, ffn1_norm_pre, ffn1_norm_post, ffn1_w_in, ffn1_w_out,
              mix_norm_pre, mix_norm_post, mix_w_in,
              gdn_conv_w, gdn_a_log, gdn_dt_bias, gdn_norm_w, gdn_w_o,
              cnv_pw1_b, cnv_dw_w, cnv_dw_b, cnv_ln_g, cnv_ln_b, cnv_w_o, cnv_b_o,
              mix_w_out,
              ffn2_norm_pre, ffn2_norm_post, ffn2_w_in, ffn2_w_out):
    for i in range(DEPTH):
        f = swiglu_ffn(rmsnorm(x, ffn1_norm_pre[i]), ffn1_w_in[i], ffn1_w_out[i])
        x = x + 0.5 * rmsnorm(f, ffn1_norm_post[i])

        h = rmsnorm(x, mix_norm_pre[i])
        p = h @ mix_w_in[i]
        qkv, z, beta_logit, a_logit, glu_in, gates = jnp.split(p, SPLITS, axis=-1)
        y_a = gated_deltanet(qkv, z, beta_logit, a_logit, gdn_conv_w[i], gdn_a_log[i],
                             gdn_dt_bias[i], gdn_norm_w[i], gdn_w_o[i])
        y_b = conformer_conv(glu_in, cnv_pw1_b[i], cnv_dw_w[i], cnv_dw_b[i],
                             cnv_ln_g[i], cnv_ln_b[i], cnv_w_o[i], cnv_b_o[i])
        g_a, g_b = jnp.split(jax.nn.sigmoid(gates), 2, axis=-1)
        y = (g_a * y_a + g_b * y_b) @ mix_w_out[i]
        x = x + rmsnorm(y, mix_norm_post[i])

        f = swiglu_ffn(rmsnorm(x, ffn2_norm_pre[i]), ffn2_w_in[i], ffn2_w_out[i])
        x = x + 0.5 * rmsnorm(f, ffn2_norm_post[i])
    return x


import jax as _jax
import jax.numpy as _jnp

TWIN_FORMAT = 'train_step'
FWD_PARAMS = ['x', 'ffn1_norm_pre', 'ffn1_norm_post', 'ffn1_w_in', 'ffn1_w_out', 'mix_norm_pre', 'mix_norm_post', 'mix_w_in', 'gdn_conv_w', 'gdn_a_log', 'gdn_dt_bias', 'gdn_norm_w', 'gdn_w_o', 'cnv_pw1_b', 'cnv_dw_w', 'cnv_dw_b', 'cnv_ln_g', 'cnv_ln_b', 'cnv_w_o', 'cnv_b_o', 'mix_w_out', 'ffn2_norm_pre', 'ffn2_norm_post', 'ffn2_w_in', 'ffn2_w_out']
TWIN_WEIGHTS = ['ffn1_norm_pre', 'ffn1_norm_post', 'ffn1_w_in', 'ffn1_w_out', 'mix_norm_pre', 'mix_norm_post', 'mix_w_in', 'gdn_conv_w', 'gdn_a_log', 'gdn_dt_bias', 'gdn_norm_w', 'gdn_w_o', 'cnv_pw1_b', 'cnv_dw_w', 'cnv_dw_b', 'cnv_ln_g', 'cnv_ln_b', 'cnv_w_o', 'cnv_b_o', 'mix_w_out', 'ffn2_norm_pre', 'ffn2_norm_post', 'ffn2_w_in', 'ffn2_w_out']
TWIN_DIFF_INPUT = 'x'
TWIN_INPUTS = ['x', 'ffn1_norm_pre', 'ffn1_norm_post', 'ffn1_w_in', 'ffn1_w_out', 'mix_norm_pre', 'mix_norm_post', 'mix_w_in', 'gdn_conv_w', 'gdn_a_log', 'gdn_dt_bias', 'gdn_norm_w', 'gdn_w_o', 'cnv_pw1_b', 'cnv_dw_w', 'cnv_dw_b', 'cnv_ln_g', 'cnv_ln_b', 'cnv_w_o', 'cnv_b_o', 'mix_w_out', 'ffn2_norm_pre', 'ffn2_norm_post', 'ffn2_w_in', 'ffn2_w_out', 'loss_target', 'm_ffn1_norm_pre', 'm_ffn1_norm_post', 'm_ffn1_w_in', 'm_ffn1_w_out', 'm_mix_norm_pre', 'm_mix_norm_post', 'm_mix_w_in', 'm_gdn_conv_w', 'm_gdn_a_log', 'm_gdn_dt_bias', 'm_gdn_norm_w', 'm_gdn_w_o', 'm_cnv_pw1_b', 'm_cnv_dw_w', 'm_cnv_dw_b', 'm_cnv_ln_g', 'm_cnv_ln_b', 'm_cnv_w_o', 'm_cnv_b_o', 'm_mix_w_out', 'm_ffn2_norm_pre', 'm_ffn2_norm_post', 'm_ffn2_w_in', 'm_ffn2_w_out', 'v_ffn1_norm_pre', 'v_ffn1_norm_post', 'v_ffn1_w_in', 'v_ffn1_w_out', 'v_mix_norm_pre', 'v_mix_norm_post', 'v_mix_w_in', 'v_gdn_conv_w', 'v_gdn_a_log', 'v_gdn_dt_bias', 'v_gdn_norm_w', 'v_gdn_w_o', 'v_cnv_pw1_b', 'v_cnv_dw_w', 'v_cnv_dw_b', 'v_cnv_ln_g', 'v_cnv_ln_b', 'v_cnv_w_o', 'v_cnv_b_o', 'v_mix_w_out', 'v_ffn2_norm_pre', 'v_ffn2_norm_post', 'v_ffn2_w_in', 'v_ffn2_w_out']
TWIN_OUTPUTS = ['loss', 'grad_x', 'grad_ffn1_norm_pre', 'grad_ffn1_norm_post', 'grad_ffn1_w_in', 'grad_ffn1_w_out', 'grad_mix_norm_pre', 'grad_mix_norm_post', 'grad_mix_w_in', 'grad_gdn_conv_w', 'grad_gdn_a_log', 'grad_gdn_dt_bias', 'grad_gdn_norm_w', 'grad_gdn_w_o', 'grad_cnv_pw1_b', 'grad_cnv_dw_w', 'grad_cnv_dw_b', 'grad_cnv_ln_g', 'grad_cnv_ln_b', 'grad_cnv_w_o', 'grad_cnv_b_o', 'grad_mix_w_out', 'grad_ffn2_norm_pre', 'grad_ffn2_norm_post', 'grad_ffn2_w_in', 'grad_ffn2_w_out', 'delta_ffn1_norm_pre', 'delta_ffn1_norm_post', 'delta_ffn1_w_in', 'delta_ffn1_w_out', 'delta_mix_norm_pre', 'delta_mix_norm_post', 'delta_mix_w_in', 'delta_gdn_conv_w', 'delta_gdn_a_log', 'delta_gdn_dt_bias', 'delta_gdn_norm_w', 'delta_gdn_w_o', 'delta_cnv_pw1_b', 'delta_cnv_dw_w', 'delta_cnv_dw_b', 'delta_cnv_ln_g', 'delta_cnv_ln_b', 'delta_cnv_w_o', 'delta_cnv_b_o', 'delta_mix_w_out', 'delta_ffn2_norm_pre', 'delta_ffn2_norm_post', 'delta_ffn2_w_in', 'delta_ffn2_w_out', 'new_m_ffn1_norm_pre', 'new_m_ffn1_norm_post', 'new_m_ffn1_w_in', 'new_m_ffn1_w_out', 'new_m_mix_norm_pre', 'new_m_mix_norm_post', 'new_m_mix_w_in', 'new_m_gdn_conv_w', 'new_m_gdn_a_log', 'new_m_gdn_dt_bias', 'new_m_gdn_norm_w', 'new_m_gdn_w_o', 'new_m_cnv_pw1_b', 'new_m_cnv_dw_w', 'new_m_cnv_dw_b', 'new_m_cnv_ln_g', 'new_m_cnv_ln_b', 'new_m_cnv_w_o', 'new_m_cnv_b_o', 'new_m_mix_w_out', 'new_m_ffn2_norm_pre', 'new_m_ffn2_norm_post', 'new_m_ffn2_w_in', 'new_m_ffn2_w_out', 'new_v_ffn1_norm_pre', 'new_v_ffn1_norm_post', 'new_v_ffn1_w_in', 'new_v_ffn1_w_out', 'new_v_mix_norm_pre', 'new_v_mix_norm_post', 'new_v_mix_w_in', 'new_v_gdn_conv_w', 'new_v_gdn_a_log', 'new_v_gdn_dt_bias', 'new_v_gdn_norm_w', 'new_v_gdn_w_o', 'new_v_cnv_pw1_b', 'new_v_cnv_dw_w', 'new_v_cnv_dw_b', 'new_v_cnv_ln_g', 'new_v_cnv_ln_b', 'new_v_cnv_w_o', 'new_v_cnv_b_o', 'new_v_mix_w_out', 'new_v_ffn2_norm_pre', 'new_v_ffn2_norm_post', 'new_v_ffn2_w_in', 'new_v_ffn2_w_out']
TWIN_LEAF_KINDS = {'loss': 'loss', 'grad_x': 'grad_x', 'grad_ffn1_norm_pre': 'grad_w', 'grad_ffn1_norm_post': 'grad_w', 'grad_ffn1_w_in': 'grad_w', 'grad_ffn1_w_out': 'grad_w', 'grad_mix_norm_pre': 'grad_w', 'grad_mix_norm_post': 'grad_w', 'grad_mix_w_in': 'grad_w', 'grad_gdn_conv_w': 'grad_w', 'grad_gdn_a_log': 'grad_w', 'grad_gdn_dt_bias': 'grad_w', 'grad_gdn_norm_w': 'grad_w', 'grad_gdn_w_o': 'grad_w', 'grad_cnv_pw1_b': 'grad_w', 'grad_cnv_dw_w': 'grad_w', 'grad_cnv_dw_b': 'grad_w', 'grad_cnv_ln_g': 'grad_w', 'grad_cnv_ln_b': 'grad_w', 'grad_cnv_w_o': 'grad_w', 'grad_cnv_b_o': 'grad_w', 'grad_mix_w_out': 'grad_w', 'grad_ffn2_norm_pre': 'grad_w', 'grad_ffn2_norm_post': 'grad_w', 'grad_ffn2_w_in': 'grad_w', 'grad_ffn2_w_out': 'grad_w', 'delta_ffn1_norm_pre': 'delta_w', 'delta_ffn1_norm_post': 'delta_w', 'delta_ffn1_w_in': 'delta_w', 'delta_ffn1_w_out': 'delta_w', 'delta_mix_norm_pre': 'delta_w', 'delta_mix_norm_post': 'delta_w', 'delta_mix_w_in': 'delta_w', 'delta_gdn_conv_w': 'delta_w', 'delta_gdn_a_log': 'delta_w', 'delta_gdn_dt_bias': 'delta_w', 'delta_gdn_norm_w': 'delta_w', 'delta_gdn_w_o': 'delta_w', 'delta_cnv_pw1_b': 'delta_w', 'delta_cnv_dw_w': 'delta_w', 'delta_cnv_dw_b': 'delta_w', 'delta_cnv_ln_g': 'delta_w', 'delta_cnv_ln_b': 'delta_w', 'delta_cnv_w_o': 'delta_w', 'delta_cnv_b_o': 'delta_w', 'delta_mix_w_out': 'delta_w', 'delta_ffn2_norm_pre': 'delta_w', 'delta_ffn2_norm_post': 'delta_w', 'delta_ffn2_w_in': 'delta_w', 'delta_ffn2_w_out': 'delta_w', 'new_m_ffn1_norm_pre': 'new_m', 'new_m_ffn1_norm_post': 'new_m', 'new_m_ffn1_w_in': 'new_m', 'new_m_ffn1_w_out': 'new_m', 'new_m_mix_norm_pre': 'new_m', 'new_m_mix_norm_post': 'new_m', 'new_m_mix_w_in': 'new_m', 'new_m_gdn_conv_w': 'new_m', 'new_m_gdn_a_log': 'new_m', 'new_m_gdn_dt_bias': 'new_m', 'new_m_gdn_norm_w': 'new_m', 'new_m_gdn_w_o': 'new_m', 'new_m_cnv_pw1_b': 'new_m', 'new_m_cnv_dw_w': 'new_m', 'new_m_cnv_dw_b': 'new_m', 'new_m_cnv_ln_g': 'new_m', 'new_m_cnv_ln_b': 'new_m', 'new_m_cnv_w_o': 'new_m', 'new_m_cnv_b_o': 'new_m', 'new_m_mix_w_out': 'new_m', 'new_m_ffn2_norm_pre': 'new_m', 'new_m_ffn2_norm_post': 'new_m', 'new_m_ffn2_w_in': 'new_m', 'new_m_ffn2_w_out': 'new_m', 'new_v_ffn1_norm_pre': 'new_v', 'new_v_ffn1_norm_post': 'new_v', 'new_v_ffn1_w_in': 'new_v', 'new_v_ffn1_w_out': 'new_v', 'new_v_mix_norm_pre': 'new_v', 'new_v_mix_norm_post': 'new_v', 'new_v_mix_w_in': 'new_v', 'new_v_gdn_conv_w': 'new_v', 'new_v_gdn_a_log': 'new_v', 'new_v_gdn_dt_bias': 'new_v', 'new_v_gdn_norm_w': 'new_v', 'new_v_gdn_w_o': 'new_v', 'new_v_cnv_pw1_b': 'new_v', 'new_v_cnv_dw_w': 'new_v', 'new_v_cnv_dw_b': 'new_v', 'new_v_cnv_ln_g': 'new_v', 'new_v_cnv_ln_b': 'new_v', 'new_v_cnv_w_o': 'new_v', 'new_v_cnv_b_o': 'new_v', 'new_v_mix_w_out': 'new_v', 'new_v_ffn2_norm_pre': 'new_v', 'new_v_ffn2_norm_post': 'new_v', 'new_v_ffn2_w_in': 'new_v', 'new_v_ffn2_w_out': 'new_v'}


def _forward(args):
    return _fwd_reference(*[args[k] for k in FWD_PARAMS])


def _output_shape():
    def fwd():
        inp = _fwd_setup_inputs(0)
        return _fwd_reference(*[inp[k] for k in FWD_PARAMS])
    out = _jax.eval_shape(fwd)
    return out.shape, out.dtype

N_MICROBATCH = 1
ADAM_LR = 0.001
ADAM_B1 = 0.9
ADAM_B2 = 0.999
ADAM_EPS = 1e-08
ADAM_WD = 0.01
ADAM_STEP = 10
PER_EXAMPLE_BATCH_AXIS = {'x': 0, 'loss_target': 0}
SHARED_INPUTS = []
_WEIGHT_DTYPES = {'ffn1_norm_pre': _jnp.float32, 'ffn1_norm_post': _jnp.float32, 'ffn1_w_in': _jnp.float32, 'ffn1_w_out': _jnp.float32, 'mix_norm_pre': _jnp.float32, 'mix_norm_post': _jnp.float32, 'mix_w_in': _jnp.float32, 'gdn_conv_w': _jnp.float32, 'gdn_a_log': _jnp.float32, 'gdn_dt_bias': _jnp.float32, 'gdn_norm_w': _jnp.float32, 'gdn_w_o': _jnp.float32, 'cnv_pw1_b': _jnp.float32, 'cnv_dw_w': _jnp.float32, 'cnv_dw_b': _jnp.float32, 'cnv_ln_g': _jnp.float32, 'cnv_ln_b': _jnp.float32, 'cnv_w_o': _jnp.float32, 'cnv_b_o': _jnp.float32, 'mix_w_out': _jnp.float32, 'ffn2_norm_pre': _jnp.float32, 'ffn2_norm_post': _jnp.float32, 'ffn2_w_in': _jnp.float32, 'ffn2_w_out': _jnp.float32}
MOMENT_SCALE = {'ffn1_norm_pre': 8.183380e-01, 'ffn1_norm_post': 7.956737e+00, 'ffn1_w_in': 3.056252e-01, 'ffn1_w_out': 5.481581e-01, 'mix_norm_pre': 1.041229e+00, 'mix_norm_post': 3.238952e+01, 'mix_w_in': 3.767221e-01, 'gdn_conv_w': 5.733611e-01, 'gdn_a_log': 2.592002e+00, 'gdn_dt_bias': 2.469824e+00, 'gdn_norm_w': 4.141781e+00, 'gdn_w_o': 1.401788e+00, 'cnv_pw1_b': 3.899197e+00, 'cnv_dw_w': 8.116051e-01, 'cnv_dw_b': 9.927374e+00, 'cnv_ln_g': 3.867416e+00, 'cnv_ln_b': 5.623856e+00, 'cnv_w_o': 2.303940e+00, 'cnv_b_o': 1.125695e+01, 'mix_w_out': 2.762107e+00, 'ffn2_norm_pre': 7.654767e-01, 'ffn2_norm_post': 8.017476e+00, 'ffn2_w_in': 3.061489e-01, 'ffn2_w_out': 6.149579e-01}


def _to_microbatches(a, axis):
    t = _jnp.moveaxis(a, axis, 0)
    t = t.reshape((N_MICROBATCH, t.shape[0] // N_MICROBATCH) + t.shape[1:])
    return _jnp.moveaxis(t, 1, axis + 1)


def setup_inputs(seed: int = 0) -> dict:
    inp = _fwd_setup_inputs(seed)
    key = _jax.random.fold_in(_jax.random.key(seed), 7919)
    shape, _ = _output_shape()
    out = dict(inp)
    out["loss_target"] = _jax.random.normal(_jax.random.fold_in(key, 0), shape, _jnp.float32)
    for i, name in enumerate(TWIN_WEIGHTS):
        w = inp[name].astype(_jnp.float32)
        if MOMENT_SCALE is None:
            s = _jnp.sqrt(_jnp.mean(_jnp.square(w)) + 1e-30)
        else:
            s = MOMENT_SCALE[name]
        km, kv = _jax.random.split(_jax.random.fold_in(key, i + 1))
        out[name] = w
        out["m_" + name] = s * _jax.random.normal(km, w.shape, _jnp.float32)
        out["v_" + name] = (s * s) * _jax.random.uniform(kv, w.shape, _jnp.float32, 0.5, 1.5)
    if N_MICROBATCH > 1:
        for name, axis in PER_EXAMPLE_BATCH_AXIS.items():
            out[name] = _to_microbatches(out[name], axis)
    return {'x': out['x'], 'ffn1_norm_pre': out['ffn1_norm_pre'], 'ffn1_norm_post': out['ffn1_norm_post'], 'ffn1_w_in': out['ffn1_w_in'], 'ffn1_w_out': out['ffn1_w_out'], 'mix_norm_pre': out['mix_norm_pre'], 'mix_norm_post': out['mix_norm_post'], 'mix_w_in': out['mix_w_in'], 'gdn_conv_w': out['gdn_conv_w'], 'gdn_a_log': out['gdn_a_log'], 'gdn_dt_bias': out['gdn_dt_bias'], 'gdn_norm_w': out['gdn_norm_w'], 'gdn_w_o': out['gdn_w_o'], 'cnv_pw1_b': out['cnv_pw1_b'], 'cnv_dw_w': out['cnv_dw_w'], 'cnv_dw_b': out['cnv_dw_b'], 'cnv_ln_g': out['cnv_ln_g'], 'cnv_ln_b': out['cnv_ln_b'], 'cnv_w_o': out['cnv_w_o'], 'cnv_b_o': out['cnv_b_o'], 'mix_w_out': out['mix_w_out'], 'ffn2_norm_pre': out['ffn2_norm_pre'], 'ffn2_norm_post': out['ffn2_norm_post'], 'ffn2_w_in': out['ffn2_w_in'], 'ffn2_w_out': out['ffn2_w_out'], 'loss_target': out['loss_target'], 'm_ffn1_norm_pre': out['m_ffn1_norm_pre'], 'm_ffn1_norm_post': out['m_ffn1_norm_post'], 'm_ffn1_w_in': out['m_ffn1_w_in'], 'm_ffn1_w_out': out['m_ffn1_w_out'], 'm_mix_norm_pre': out['m_mix_norm_pre'], 'm_mix_norm_post': out['m_mix_norm_post'], 'm_mix_w_in': out['m_mix_w_in'], 'm_gdn_conv_w': out['m_gdn_conv_w'], 'm_gdn_a_log': out['m_gdn_a_log'], 'm_gdn_dt_bias': out['m_gdn_dt_bias'], 'm_gdn_norm_w': out['m_gdn_norm_w'], 'm_gdn_w_o': out['m_gdn_w_o'], 'm_cnv_pw1_b': out['m_cnv_pw1_b'], 'm_cnv_dw_w': out['m_cnv_dw_w'], 'm_cnv_dw_b': out['m_cnv_dw_b'], 'm_cnv_ln_g': out['m_cnv_ln_g'], 'm_cnv_ln_b': out['m_cnv_ln_b'], 'm_cnv_w_o': out['m_cnv_w_o'], 'm_cnv_b_o': out['m_cnv_b_o'], 'm_mix_w_out': out['m_mix_w_out'], 'm_ffn2_norm_pre': out['m_ffn2_norm_pre'], 'm_ffn2_norm_post': out['m_ffn2_norm_post'], 'm_ffn2_w_in': out['m_ffn2_w_in'], 'm_ffn2_w_out': out['m_ffn2_w_out'], 'v_ffn1_norm_pre': out['v_ffn1_norm_pre'], 'v_ffn1_norm_post': out['v_ffn1_norm_post'], 'v_ffn1_w_in': out['v_ffn1_w_in'], 'v_ffn1_w_out': out['v_ffn1_w_out'], 'v_mix_norm_pre': out['v_mix_norm_pre'], 'v_mix_norm_post': out['v_mix_norm_post'], 'v_mix_w_in': out['v_mix_w_in'], 'v_gdn_conv_w': out['v_gdn_conv_w'], 'v_gdn_a_log': out['v_gdn_a_log'], 'v_gdn_dt_bias': out['v_gdn_dt_bias'], 'v_gdn_norm_w': out['v_gdn_norm_w'], 'v_gdn_w_o': out['v_gdn_w_o'], 'v_cnv_pw1_b': out['v_cnv_pw1_b'], 'v_cnv_dw_w': out['v_cnv_dw_w'], 'v_cnv_dw_b': out['v_cnv_dw_b'], 'v_cnv_ln_g': out['v_cnv_ln_g'], 'v_cnv_ln_b': out['v_cnv_ln_b'], 'v_cnv_w_o': out['v_cnv_w_o'], 'v_cnv_b_o': out['v_cnv_b_o'], 'v_mix_w_out': out['v_mix_w_out'], 'v_ffn2_norm_pre': out['v_ffn2_norm_pre'], 'v_ffn2_norm_post': out['v_ffn2_norm_post'], 'v_ffn2_w_in': out['v_ffn2_w_in'], 'v_ffn2_w_out': out['v_ffn2_w_out']}


def _loss(weights, diff, rest, loss_target):
    with _jax.named_scope("forward"):
        args = {**rest, TWIN_DIFF_INPUT: diff, **{k: w.astype(_WEIGHT_DTYPES[k]) for k, w in weights.items()}}
        y = _forward(args)
    with _jax.named_scope("loss_head"):
        err = _jnp.square(y.astype(_jnp.float32) - loss_target)
        return 0.5 * _jnp.sum(_jnp.mean(err, axis=-1)) if err.ndim else 0.5 * err


def _adamw(w, g, m, v):
    m = ADAM_B1 * m + (1.0 - ADAM_B1) * g
    v = ADAM_B2 * v + (1.0 - ADAM_B2) * _jnp.square(g)
    m_hat = m / (1.0 - ADAM_B1 ** ADAM_STEP)
    v_hat = v / (1.0 - ADAM_B2 ** ADAM_STEP)
    delta = -ADAM_LR * (m_hat / (_jnp.sqrt(v_hat) + ADAM_EPS) + ADAM_WD * w)
    return delta, m, v


def reference(x, ffn1_norm_pre, ffn1_norm_post, ffn1_w_in, ffn1_w_out, mix_norm_pre, mix_norm_post, mix_w_in, gdn_conv_w, gdn_a_log, gdn_dt_bias, gdn_norm_w, gdn_w_o, cnv_pw1_b, cnv_dw_w, cnv_dw_b, cnv_ln_g, cnv_ln_b, cnv_w_o, cnv_b_o, mix_w_out, ffn2_norm_pre, ffn2_norm_post, ffn2_w_in, ffn2_w_out, loss_target, m_ffn1_norm_pre, m_ffn1_norm_post, m_ffn1_w_in, m_ffn1_w_out, m_mix_norm_pre, m_mix_norm_post, m_mix_w_in, m_gdn_conv_w, m_gdn_a_log, m_gdn_dt_bias, m_gdn_norm_w, m_gdn_w_o, m_cnv_pw1_b, m_cnv_dw_w, m_cnv_dw_b, m_cnv_ln_g, m_cnv_ln_b, m_cnv_w_o, m_cnv_b_o, m_mix_w_out, m_ffn2_norm_pre, m_ffn2_norm_post, m_ffn2_w_in, m_ffn2_w_out, v_ffn1_norm_pre, v_ffn1_norm_post, v_ffn1_w_in, v_ffn1_w_out, v_mix_norm_pre, v_mix_norm_post, v_mix_w_in, v_gdn_conv_w, v_gdn_a_log, v_gdn_dt_bias, v_gdn_norm_w, v_gdn_w_o, v_cnv_pw1_b, v_cnv_dw_w, v_cnv_dw_b, v_cnv_ln_g, v_cnv_ln_b, v_cnv_w_o, v_cnv_b_o, v_mix_w_out, v_ffn2_norm_pre, v_ffn2_norm_post, v_ffn2_w_in, v_ffn2_w_out):
    given = dict(x=x, ffn1_norm_pre=ffn1_norm_pre, ffn1_norm_post=ffn1_norm_post, ffn1_w_in=ffn1_w_in, ffn1_w_out=ffn1_w_out, mix_norm_pre=mix_norm_pre, mix_norm_post=mix_norm_post, mix_w_in=mix_w_in, gdn_conv_w=gdn_conv_w, gdn_a_log=gdn_a_log, gdn_dt_bias=gdn_dt_bias, gdn_norm_w=gdn_norm_w, gdn_w_o=gdn_w_o, cnv_pw1_b=cnv_pw1_b, cnv_dw_w=cnv_dw_w, cnv_dw_b=cnv_dw_b, cnv_ln_g=cnv_ln_g, cnv_ln_b=cnv_ln_b, cnv_w_o=cnv_w_o, cnv_b_o=cnv_b_o, mix_w_out=mix_w_out, ffn2_norm_pre=ffn2_norm_pre, ffn2_norm_post=ffn2_norm_post, ffn2_w_in=ffn2_w_in, ffn2_w_out=ffn2_w_out, loss_target=loss_target, m_ffn1_norm_pre=m_ffn1_norm_pre, m_ffn1_norm_post=m_ffn1_norm_post, m_ffn1_w_in=m_ffn1_w_in, m_ffn1_w_out=m_ffn1_w_out, m_mix_norm_pre=m_mix_norm_pre, m_mix_norm_post=m_mix_norm_post, m_mix_w_in=m_mix_w_in, m_gdn_conv_w=m_gdn_conv_w, m_gdn_a_log=m_gdn_a_log, m_gdn_dt_bias=m_gdn_dt_bias, m_gdn_norm_w=m_gdn_norm_w, m_gdn_w_o=m_gdn_w_o, m_cnv_pw1_b=m_cnv_pw1_b, m_cnv_dw_w=m_cnv_dw_w, m_cnv_dw_b=m_cnv_dw_b, m_cnv_ln_g=m_cnv_ln_g, m_cnv_ln_b=m_cnv_ln_b, m_cnv_w_o=m_cnv_w_o, m_cnv_b_o=m_cnv_b_o, m_mix_w_out=m_mix_w_out, m_ffn2_norm_pre=m_ffn2_norm_pre, m_ffn2_norm_post=m_ffn2_norm_post, m_ffn2_w_in=m_ffn2_w_in, m_ffn2_w_out=m_ffn2_w_out, v_ffn1_norm_pre=v_ffn1_norm_pre, v_ffn1_norm_post=v_ffn1_norm_post, v_ffn1_w_in=v_ffn1_w_in, v_ffn1_w_out=v_ffn1_w_out, v_mix_norm_pre=v_mix_norm_pre, v_mix_norm_post=v_mix_norm_post, v_mix_w_in=v_mix_w_in, v_gdn_conv_w=v_gdn_conv_w, v_gdn_a_log=v_gdn_a_log, v_gdn_dt_bias=v_gdn_dt_bias, v_gdn_norm_w=v_gdn_norm_w, v_gdn_w_o=v_gdn_w_o, v_cnv_pw1_b=v_cnv_pw1_b, v_cnv_dw_w=v_cnv_dw_w, v_cnv_dw_b=v_cnv_dw_b, v_cnv_ln_g=v_cnv_ln_g, v_cnv_ln_b=v_cnv_ln_b, v_cnv_w_o=v_cnv_w_o, v_cnv_b_o=v_cnv_b_o, v_mix_w_out=v_mix_w_out, v_ffn2_norm_pre=v_ffn2_norm_pre, v_ffn2_norm_post=v_ffn2_norm_post, v_ffn2_w_in=v_ffn2_w_in, v_ffn2_w_out=v_ffn2_w_out)
    weights = {n: given[n] for n in TWIN_WEIGHTS}
    shared = {n: given[n] for n in SHARED_INPUTS}
    per_example = {n: given[n] for n in ['x']}
    grad_fn = _jax.value_and_grad(_loss, argnums=(0, 1))

    def one_microbatch(ex, loss_target):
        ex = dict(ex)
        diff = ex.pop(TWIN_DIFF_INPUT)
        return grad_fn(weights, diff, {**shared, **ex}, loss_target)

    if N_MICROBATCH == 1:
        loss, (grad_w, grad_x) = one_microbatch(per_example, given["loss_target"])
    else:
        def body(carry, xs):
            loss_sum, grad_sum = carry
            l_k, (gw_k, gx_k) = one_microbatch(xs[0], xs[1])
            with _jax.named_scope("update"):
                return (loss_sum + l_k, _jax.tree.map(_jnp.add, grad_sum, gw_k)), gx_k

        init = (_jnp.zeros((), _jnp.float32), _jax.tree.map(_jnp.zeros_like, weights))
        (loss, grad_w), grad_x = _jax.lax.scan(body, init, (per_example, given["loss_target"]))
    with _jax.named_scope("update"):
        delta_w, new_m, new_v = {}, {}, {}
        for n in TWIN_WEIGHTS:
            delta_w[n], new_m[n], new_v[n] = _adamw(weights[n], grad_w[n], given["m_" + n], given["v_" + n])
    return (loss, grad_x, *[grad_w[n] for n in TWIN_WEIGHTS], *[delta_w[n] for n in TWIN_WEIGHTS],
            *[new_m[n] for n in TWIN_WEIGHTS], *[new_v[n] for n in TWIN_WEIGHTS])
```

```python
import functools
import math

import jax
import jax.numpy as jnp
from jax import lax
from jax.experimental import pallas as pl
from jax.experimental.pallas import tpu as pltpu

F32, BF16 = jnp.float32, jnp.bfloat16
S = jax.ShapeDtypeStruct

D_MODEL = 1024
D_FF = 2816
HEADS = 8
DK = 128
CHUNK = 64
GDN_CONV = 4
CNV_K = 31
W_QKV = 3 * HEADS * DK
W_Z = HEADS * DK
W_GLU = 2 * D_MODEL
W_GATE = 2 * D_MODEL
P_IN = W_QKV + W_Z + 2 * HEADS + W_GLU + W_GATE
LANES = 128
P_ALL = W_QKV + W_Z + W_GLU + W_GATE + LANES
COL_Z = W_QKV // LANES
COL_GLU = (W_QKV + W_Z) // LANES
COL_GATE = (W_QKV + W_Z + W_GLU) // LANES
COL_BA = (W_QKV + W_Z + W_GLU + W_GATE) // LANES
RMS_EPS = 1e-6
LN_EPS = 1e-5
DEPTH = 2
N_BLK = 4
VMEM_LIMIT = 56 * 1024 * 1024

ADAM_LR, ADAM_B1, ADAM_B2, ADAM_EPS, ADAM_WD, ADAM_STEP = 0.001, 0.9, 0.999, 1e-08, 0.01, 10


def _cp(sem):
    return pltpu.CompilerParams(dimension_semantics=sem, vmem_limit_bytes=VMEM_LIMIT)


def _tile(n, prefs=(512, 640, 256, 384, 128)):
    for t in prefs:
        if n % t == 0:
            return t
    return n


def mm(name, a, b, ta=False, tb=False, out_dtype=F32):
    k = a.shape[0] if ta else a.shape[1]
    m = a.shape[1] if ta else a.shape[0]
    n = b.shape[0] if tb else b.shape[1]
    assert k == (b.shape[1] if tb else b.shape[0]), (name, a.shape, b.shape)
    tm, tn = _tile(m), _tile(n)
    a_spec = pl.BlockSpec((k, tm), lambda i, j: (0, i)) if ta else pl.BlockSpec((tm, k), lambda i, j: (i, 0))
    b_spec = pl.BlockSpec((tn, k), lambda i, j: (j, 0)) if tb else pl.BlockSpec((k, tn), lambda i, j: (0, j))
    dims = (((0 if ta else 1,), (1 if tb else 0,)), ((), ()))

    def body(a_ref, b_ref, o_ref):
        o_ref[...] = lax.dot_general(a_ref[...], b_ref[...], dims, preferred_element_type=F32).astype(o_ref.dtype)

    return pl.pallas_call(
        body, name=name, grid=(m // tm, n // tn), in_specs=[a_spec, b_spec],
        out_specs=pl.BlockSpec((tm, tn), lambda i, j: (i, j)), out_shape=S((m, n), out_dtype),
        compiler_params=_cp(("parallel", "parallel")))(a, b)


def ew_fwd(name, fn, grid, ins, outs):
    n_in = len(ins)

    def body(*refs):
        vals = [r[...] for r in refs[:n_in]]
        res = fn(pl.program_id(0), *vals)
        for r, v in zip(refs[n_in:], res):
            r[...] = v.astype(r.dtype)

    out = pl.pallas_call(
        body, name=name, grid=grid, in_specs=[s for _, s in ins], out_specs=[s for _, s in outs],
        out_shape=[sd for sd, _ in outs], compiler_params=_cp(("parallel", "parallel")))(*[a for a, _ in ins])
    return out


def ew_bwd(name, fn, grid, ins, cts, wrt, acc, add=None):
    n_in, n_ct, n_wrt, n_acc = len(ins), len(cts), len(wrt), len(acc)
    has_add = add is not None

    def body(*refs):
        in_refs = refs[:n_in]
        ct_refs = refs[n_in:n_in + n_ct]
        pos = n_in + n_ct
        add_ref = refs[pos] if has_add else None
        pos += 1 if has_add else 0
        wrt_refs = refs[pos:pos + n_wrt]
        acc_refs = refs[pos + n_wrt:pos + n_wrt + n_acc]
        col, tok = pl.program_id(0), pl.program_id(1)
        vals = [r[...] for r in in_refs]
        _, vjp = jax.vjp(lambda *a: fn(col, *a), *vals)
        grads = vjp(tuple(c[...].astype(F32) for c in ct_refs))
        for pos_w, ((idx, _, _), r) in enumerate(zip(wrt, wrt_refs)):
            g = grads[idx]
            if has_add and pos_w == 0:
                g = g + add_ref[...]
            r[...] = g.astype(r.dtype)
        for (idx, _, _, over_cols), r in zip(acc, acc_refs):
            first = (tok == 0) & (col == 0) if over_cols else tok == 0

            @pl.when(first)
            def _():
                r[...] = jnp.zeros_like(r)

            r[...] += grads[idx]

    arrays = [a for a, _ in ins] + [a for a, _ in cts] + ([add[0]] if has_add else [])
    in_specs = [s for _, s in ins] + [s for _, s in cts] + ([add[1]] if has_add else [])
    over_any = any(o for *_, o in acc)
    out = pl.pallas_call(
        body, name=name, grid=grid, in_specs=in_specs,
        out_specs=[s for _, _, s in wrt] + [s for _, _, s, _ in acc],
        out_shape=[sd for _, sd, _ in wrt] + [sd for _, sd, _, _ in acc],
        compiler_params=_cp(("arbitrary" if over_any else "parallel", "arbitrary")))(*arrays)
    return out


def _tok(width, col=0):
    return lambda tm: pl.BlockSpec((tm, width), lambda j, i: (i, col))


def _tokcol(off=0):
    return lambda tm: pl.BlockSpec((tm, LANES), lambda j, i: (i, off + j))


def _par(width, col=0):
    return pl.BlockSpec((1, width), lambda j, i: (0, col))


def _parcol(off=0):
    return pl.BlockSpec((1, LANES), lambda j, i: (0, off + j))


def _rms(x, w, eps=RMS_EPS):
    return x * lax.rsqrt(jnp.mean(x * x, axis=-1, keepdims=True) + eps) * w


def _silu(x):
    return x * jax.nn.sigmoid(x)


def fn_rms(col, x, w):
    return (_rms(x, w),)


def fn_swiglu(col, gate, up):
    return (_silu(gate) * up,)


def fn_gdnpost(col, c):
    typ = col // HEADS
    y = _silu(c)
    n = y * lax.rsqrt(jnp.sum(y * y, axis=-1, keepdims=True) + 1e-6)
    n = n * jnp.where(typ == 0, DK ** -0.5, 1.0)
    return (jnp.where(typ < 2, n, y),)


def fn_gdnout(col, o, z, nw):
    return (_rms(o, nw) * _silu(z),)


def fn_glu(col, a, g, ba, bg):
    return ((a + ba) * jax.nn.sigmoid(g + bg),)


def fn_lnsilu(col, h, g, b):
    mu = jnp.mean(h, axis=-1, keepdims=True)
    var = jnp.mean(jnp.square(h - mu), axis=-1, keepdims=True)
    return (_silu((h - mu) * lax.rsqrt(var + LN_EPS) * g + b),)


def fn_merge(col, ya, yb, ga, gb, bo):
    return (jax.nn.sigmoid(ga) * ya + jax.nn.sigmoid(gb) * (yb + bo),)


HALO = 32


def conv_fwd(name, x, col_off, n_ch, w, bias, tb):
    n_tok = x.shape[0]
    k = w.shape[0]
    nt = n_tok // tb

    def body(xp_ref, xc_ref, w_ref, *rest):
        if bias is not None:
            b_ref, o_ref, xs = rest
        else:
            o_ref, xs = rest
        i = pl.program_id(1)
        xs[0:HALO, :] = jnp.where(i == 0, 0.0, xp_ref[tb - HALO:tb, :])
        xs[HALO:HALO + tb, :] = xc_ref[...]
        acc = jnp.zeros((tb, LANES), F32)
        for j in range(k):
            s = k - 1 - j
            acc = acc + w_ref[j:j + 1, :] * xs[HALO - s:HALO - s + tb, :]
        if bias is not None:
            acc = acc + b_ref[...]
        o_ref[...] = acc

    in_specs = [pl.BlockSpec((tb, LANES), lambda j, i: (jnp.maximum(i - 1, 0), col_off + j)),
                pl.BlockSpec((tb, LANES), lambda j, i: (i, col_off + j)),
                pl.BlockSpec((k, LANES), lambda j, i: (0, j))]
    args = [x, x, w]
    if bias is not None:
        in_specs.append(pl.BlockSpec((1, LANES), lambda j, i: (0, j)))
        args.append(bias)
    return pl.pallas_call(
        body, name=name, grid=(n_ch // LANES, nt), in_specs=in_specs,
        out_specs=pl.BlockSpec((tb, LANES), lambda j, i: (i, j)), out_shape=S((n_tok, n_ch), F32),
        scratch_shapes=[pltpu.VMEM((HALO + tb, LANES), F32)],
        compiler_params=_cp(("parallel", "parallel")))(*args)


def conv_bwd(name, x, col_off, n_ch, w, dy, dx_dtype, tb):
    n_tok = x.shape[0]
    k = w.shape[0]
    nt = n_tok // tb

    def body(xp_ref, xc_ref, w_ref, dyc_ref, dyn_ref, dx_ref, dw_ref, db_ref, xs, dys):
        i = pl.program_id(1)
        xs[0:HALO, :] = jnp.where(i == 0, 0.0, xp_ref[tb - HALO:tb, :])
        xs[HALO:HALO + tb, :] = xc_ref[...]
        dyc = dyc_ref[...]
        dys[0:tb, :] = dyc
        dys[tb:tb + HALO, :] = jnp.where(i == nt - 1, 0.0, dyn_ref[0:HALO, :])

        @pl.when(i == 0)
        def _():
            dw_ref[...] = jnp.zeros_like(dw_ref)
            db_ref[...] = jnp.zeros_like(db_ref)

        acc = jnp.zeros((tb, LANES), F32)
        for j in range(k):
            s = k - 1 - j
            acc = acc + w_ref[j:j + 1, :] * dys[s:s + tb, :]
            dw_ref[j:j + 1, :] += jnp.sum(dyc * xs[HALO - s:HALO - s + tb, :], axis=0, keepdims=True)
        dx_ref[...] = acc.astype(dx_ref.dtype)
        db_ref[...] += jnp.sum(dyc, axis=0, keepdims=True)

    in_specs = [pl.BlockSpec((tb, LANES), lambda j, i: (jnp.maximum(i - 1, 0), col_off + j)),
                pl.BlockSpec((tb, LANES), lambda j, i: (i, col_off + j)),
                pl.BlockSpec((k, LANES), lambda j, i: (0, j)),
                pl.BlockSpec((tb, LANES), lambda j, i: (i, j)),
                pl.BlockSpec((tb, LANES), lambda j, i: (jnp.minimum(i + 1, nt - 1), j))]
    return pl.pallas_call(
        body, name=name, grid=(n_ch // LANES, nt), in_specs=in_specs,
        out_specs=[pl.BlockSpec((tb, LANES), lambda j, i: (i, j)),
                   pl.BlockSpec((k, LANES), lambda j, i: (0, j)),
                   pl.BlockSpec((1, LANES), lambda j, i: (0, j))],
        out_shape=[S((n_tok, n_ch), dx_dtype), S((k, n_ch), F32), S((1, n_ch), F32)],
        scratch_shapes=[pltpu.VMEM((HALO + tb, LANES), F32), pltpu.VMEM((tb + HALO, LANES), F32)],
        compiler_params=_cp(("parallel", "arbitrary")))(x, x, w, dy, dy)


def _dotb(a, b, ca, cb):
    return lax.dot_general(a.astype(BF16), b.astype(BF16), (((ca,), (cb,)), ((), ())), preferred_element_type=F32)


def _dot32(a, b, ca, cb):
    return lax.dot_general(a, b, (((ca,), (cb,)), ((), ())), preferred_element_type=F32,
                           precision=lax.Precision.HIGHEST)


@jax.custom_vjp
def _inv_unit_lower(a):
    n = a.shape[0]
    eye = (lax.broadcasted_iota(jnp.int32, (n, n), 0) == lax.broadcasted_iota(jnp.int32, (n, n), 1)).astype(F32)
    inv = eye - a
    p = a
    for _ in range(int(math.log2(n)) - 1):
        p = _dot32(p, p, 1, 0)
        inv = inv + _dot32(inv, p, 1, 0)
    return inv


def _inv_fwd(a):
    t = _inv_unit_lower(a)
    return t, t


def _inv_bwd(t, dt):
    x = _dot32(t, dt, 0, 0)
    return (-_dot32(x, t, 1, 1),)


_inv_unit_lower.defvjp(_inv_fwd, _inv_bwd)


def _softplus(x):
    return jnp.maximum(x, 0.0) + jnp.log(1.0 + jnp.exp(-jnp.abs(x)))


def _gdn_chunk(qs, ks, vs, pba, alog, dtb, states):
    c = pba.shape[0]
    row = lax.broadcasted_iota(jnp.int32, (c, c), 0)
    colm = lax.broadcasted_iota(jnp.int32, (c, c), 1)
    causal, strict = row >= colm, row > colm
    tril = causal.astype(F32)
    lane = lax.broadcasted_iota(jnp.int32, (1, LANES), 1)
    sub = lax.broadcasted_iota(jnp.int32, (LANES, 1), 0)
    last = (lax.broadcasted_iota(jnp.int32, (c, 1), 0) == c - 1).astype(F32)
    beta_all = jax.nn.sigmoid(pba)
    g_all = -jnp.exp(alog) * _softplus(pba + dtb)
    gc_all = _dot32(tril, g_all, 1, 0)
    gr_all = _dot32(g_all, tril, 0, 1)
    outs, news = [], []
    for h in range(HEADS):
        q, k, v, st = qs[h], ks[h], vs[h], states[h]
        beta = jnp.sum(beta_all * (lane == h).astype(F32), axis=1, keepdims=True)
        gc = jnp.sum(gc_all * (lane == HEADS + h).astype(F32), axis=1, keepdims=True)
        gr = jnp.sum(gr_all * (sub == HEADS + h).astype(F32), axis=0, keepdims=True)
        diff = gc - gr
        decay = jnp.where(causal, jnp.exp(jnp.where(causal, diff, 0.0)), 0.0)
        kk = _dotb(k, k, 1, 1)
        a_mat = jnp.where(strict, kk * decay * beta, 0.0)
        tinv = _inv_unit_lower(a_mat)
        eg = jnp.exp(gc)
        u = _dot32(tinv, v * beta, 1, 0)
        w = _dot32(tinv, k * (beta * eg), 1, 0)
        qk = _dotb(q, k, 1, 1) * decay
        g_last = jnp.sum(gc * last, axis=0, keepdims=True)
        q_dec = q * eg
        k_dec = k * jnp.exp(g_last - gc)
        v_new = u - _dotb(w, st, 1, 0)
        o = _dotb(q_dec, st, 1, 0) + _dotb(qk, v_new, 1, 0)
        st_new = st * jnp.exp(g_last) + _dotb(k_dec, v_new, 0, 0)
        outs.append(o)
        news.append(st_new)
    return outs, news


def _heads(ref, base=0):
    return [ref[:, (base + h) * DK:(base + h + 1) * DK] for h in range(HEADS)]


def gdn_fwd(name, qkvn, p, alog, dtb):
    n_tok = qkvn.shape[0]
    n = n_tok // CHUNK
    hd = HEADS * DK

    def body(q_ref, k_ref, v_ref, pba_ref, al_ref, dt_ref, o_ref, s_ref, st):
        @pl.when(pl.program_id(0) == 0)
        def _():
            st[...] = jnp.zeros_like(st)

        s_ref[...] = st[...]
        states = [st[h * DK:(h + 1) * DK, :] for h in range(HEADS)]
        outs, news = _gdn_chunk(_heads(q_ref), _heads(k_ref), _heads(v_ref), pba_ref[...], al_ref[...], dt_ref[...],
                                states)
        for h in range(HEADS):
            o_ref[:, h * DK:(h + 1) * DK] = outs[h]
            st[h * DK:(h + 1) * DK, :] = news[h]

    blk = lambda c: pl.BlockSpec((CHUNK, hd), lambda i: (i, c))
    return pl.pallas_call(
        body, name=name, grid=(n,),
        in_specs=[blk(0), blk(1), blk(2), pl.BlockSpec((CHUNK, LANES), lambda i: (i, COL_BA)),
                  pl.BlockSpec((1, LANES), lambda i: (0, 0)), pl.BlockSpec((1, LANES), lambda i: (0, 0))],
        out_specs=[pl.BlockSpec((CHUNK, hd), lambda i: (i, 0)), pl.BlockSpec((None, hd, DK), lambda i: (i, 0, 0))],
        out_shape=[S((n_tok, hd), F32), S((n, hd, DK), F32)],
        scratch_shapes=[pltpu.VMEM((hd, DK), F32)],
        compiler_params=_cp(("arbitrary",)))(qkvn, qkvn, qkvn, p, alog, dtb)


def gdn_bwd(name, qkvn, p, alog, dtb, states, do):
    n_tok = qkvn.shape[0]
    n = n_tok // CHUNK
    hd = HEADS * DK

    def body(q_ref, k_ref, v_ref, pba_ref, al_ref, dt_ref, s_ref, do_ref, dqkv_ref, dpba_ref, dal_ref, ddt_ref, dst):
        @pl.when(pl.program_id(0) == 0)
        def _():
            dst[...] = jnp.zeros_like(dst)
            dal_ref[...] = jnp.zeros_like(dal_ref)
            ddt_ref[...] = jnp.zeros_like(ddt_ref)

        states = [s_ref[h * DK:(h + 1) * DK, :] for h in range(HEADS)]
        _, vjp = jax.vjp(_gdn_chunk, _heads(q_ref), _heads(k_ref), _heads(v_ref), pba_ref[...], al_ref[...],
                         dt_ref[...], states)
        d_outs = [do_ref[:, h * DK:(h + 1) * DK] for h in range(HEADS)]
        d_news = [dst[h * DK:(h + 1) * DK, :] for h in range(HEADS)]
        dq, dk, dv, dpba, dal, ddt, dstates = vjp((d_outs, d_news))
        for h in range(HEADS):
            dqkv_ref[:, h * DK:(h + 1) * DK] = dq[h]
            dqkv_ref[:, (HEADS + h) * DK:(HEADS + h + 1) * DK] = dk[h]
            dqkv_ref[:, (2 * HEADS + h) * DK:(2 * HEADS + h + 1) * DK] = dv[h]
            dst[h * DK:(h + 1) * DK, :] = dstates[h]
        dpba_ref[...] = dpba.astype(dpba_ref.dtype)
        dal_ref[...] += dal
        ddt_ref[...] += ddt

    blk = lambda c: pl.BlockSpec((CHUNK, hd), lambda i: (n - 1 - i, c))
    return pl.pallas_call(
        body, name=name, grid=(n,),
        in_specs=[blk(0), blk(1), blk(2), pl.BlockSpec((CHUNK, LANES), lambda i: (n - 1 - i, COL_BA)),
                  pl.BlockSpec((1, LANES), lambda i: (0, 0)), pl.BlockSpec((1, LANES), lambda i: (0, 0)),
                  pl.BlockSpec((None, hd, DK), lambda i: (n - 1 - i, 0, 0)), blk(0)],
        out_specs=[pl.BlockSpec((CHUNK, 3 * hd), lambda i: (n - 1 - i, 0)),
                   pl.BlockSpec((CHUNK, LANES), lambda i: (n - 1 - i, 0)),
                   pl.BlockSpec((1, LANES), lambda i: (0, 0)), pl.BlockSpec((1, LANES), lambda i: (0, 0))],
        out_shape=[S((n_tok, 3 * hd), F32), S((n_tok, LANES), BF16), S((1, LANES), F32), S((1, LANES), F32)],
        scratch_shapes=[pltpu.VMEM((hd, DK), F32)],
        compiler_params=_cp(("arbitrary",)))(qkvn, qkvn, qkvn, p, alog, dtb, states, do)


def loss_head(name, y, target, tm):
    n_tok, d = y.shape

    def body(y_ref, t_ref, dy_ref, l_ref):
        @pl.when(pl.program_id(0) == 0)
        def _():
            l_ref[...] = jnp.zeros_like(l_ref)

        e = y_ref[...] - t_ref[...]
        dy_ref[...] = e * (1.0 / d)
        l_ref[...] += jnp.sum(e * e, keepdims=True) * (0.5 / d)

    spec = pl.BlockSpec((tm, d), lambda i: (i, 0))
    return pl.pallas_call(
        body, name=name, grid=(n_tok // tm,), in_specs=[spec, spec],
        out_specs=[spec, pl.BlockSpec((1, 1), lambda i: (0, 0))], out_shape=[S((n_tok, d), F32), S((1, 1), F32)],
        compiler_params=_cp(("arbitrary",)))(y, target)


def _tm(n_tok):
    return min(512, n_tok)


def ffn_fwd(tag, x, w):
    n_tok = x.shape[0]
    tm = _tm(n_tok)
    g1 = (1, n_tok // tm)
    tD = _tok(D_MODEL)(tm)
    (h,) = ew_fwd(tag + "_rms", fn_rms, g1, [(x, tD), (w["norm_pre"], _par(D_MODEL))], [(S((n_tok, D_MODEL), BF16), tD)])
    u = mm(tag + "_in", h, w["w_in"])
    tF = lambda c: _tok(D_FF, c)(tm)
    (a,) = ew_fwd(tag + "_swiglu", fn_swiglu, g1, [(u, tF(0)), (u, tF(1))], [(S((n_tok, D_FF), BF16), tF(0))])
    f = mm(tag + "_out", a, w["w_out"])
    fn_res = lambda col, x_, f_, w_: (x_ + 0.5 * _rms(f_, w_),)
    (xo,) = ew_fwd(tag + "_res", fn_res, g1, [(x, tD), (f, tD), (w["norm_post"], _par(D_MODEL))],
                   [(S((n_tok, D_MODEL), F32), tD)])
    return xo, dict(x=x, h=h, u=u, a=a, f=f)


def ffn_bwd(tag, dxo, sv, w):
    n_tok = dxo.shape[0]
    tm = _tm(n_tok)
    g1 = (1, n_tok // tm)
    tD = _tok(D_MODEL)(tm)
    pD = _par(D_MODEL)
    fn_post = lambda col, f_, w_: (0.5 * _rms(f_, w_),)
    df, d_post = ew_bwd(tag + "_res_b", fn_post, g1, [(sv["f"], tD), (w["norm_post"], pD)], [(dxo, tD)],
                        [(0, S((n_tok, D_MODEL), BF16), tD)], [(1, S((1, D_MODEL), F32), pD, False)])
    da = mm(tag + "_out_bx", df, w["w_out"], tb=True)
    d_wout = mm(tag + "_out_bw", sv["a"], df, ta=True)
    tF = lambda c: _tok(D_FF, c)(tm)
    dgate, dup = ew_bwd(tag + "_swiglu_b", fn_swiglu, g1, [(sv["u"], tF(0)), (sv["u"], tF(1))], [(da, tF(0))],
                        [(0, S((n_tok, D_FF), BF16), tF(0)), (1, S((n_tok, D_FF), BF16), tF(0))], [])
    du = jnp.concatenate([dgate, dup], axis=1)
    dh = mm(tag + "_in_bx", du, w["w_in"], tb=True)
    d_win = mm(tag + "_in_bw", sv["h"], du, ta=True)
    dx, d_pre = ew_bwd(tag + "_rms_b", fn_rms, g1, [(sv["x"], tD), (w["norm_pre"], pD)], [(dh, tD)],
                       [(0, S((n_tok, D_MODEL), F32), tD)], [(1, S((1, D_MODEL), F32), pD, False)], add=(dxo, tD))
    return dx, dict(norm_pre=d_pre, norm_post=d_post, w_in=d_win, w_out=d_wout)


def mix_fwd(tag, x, w):
    n_tok = x.shape[0]
    tm = _tm(n_tok)
    nt = n_tok // tm
    g1 = (1, nt)
    tD = _tok(D_MODEL)(tm)
    pD = _par(D_MODEL)
    (h,) = ew_fwd(tag + "_rms", fn_rms, g1, [(x, tD), (w["norm_pre"], pD)], [(S((n_tok, D_MODEL), BF16), tD)])
    p = mm(tag + "_in", h, w["w_all"])
    c = conv_fwd(tag + "_gconv", p, 0, W_QKV, w["conv_w"], None, tm)
    tC = _tokcol()(tm)
    (qkvn,) = ew_fwd(tag + "_gpost", fn_gdnpost, (W_QKV // LANES, nt), [(c, tC)], [(S((n_tok, W_QKV), F32), tC)])
    o, states = gdn_fwd(tag + "_gdn", qkvn, p, w["alog"], w["dtb"])
    (on,) = ew_fwd(tag + "_gout", fn_gdnout, (HEADS, nt),
                   [(o, tC), (p, _tokcol(COL_Z)(tm)), (w["gdn_norm_w"], _par(LANES))],
                   [(S((n_tok, D_MODEL), BF16), tC)])
    ya = mm(tag + "_go", on, w["gdn_w_o"])
    (hglu,) = ew_fwd(tag + "_glu", fn_glu, (D_MODEL // LANES, nt),
                     [(p, _tokcol(COL_GLU)(tm)), (p, _tokcol(COL_GLU + D_MODEL // LANES)(tm)),
                      (w["pw1_b"], _parcol(0)), (w["pw1_b"], _parcol(D_MODEL // LANES))],
                     [(S((n_tok, D_MODEL), F32), tC)])
    hc = conv_fwd(tag + "_cconv", hglu, 0, D_MODEL, w["dw_w"], w["dw_b"], tm)
    (hs,) = ew_fwd(tag + "_ln", fn_lnsilu, g1, [(hc, tD), (w["ln_g"], pD), (w["ln_b"], pD)],
                   [(S((n_tok, D_MODEL), BF16), tD)])
    yb = mm(tag + "_co", hs, w["cnv_w_o"])
    tG = lambda cb: pl.BlockSpec((tm, D_MODEL), lambda j, i: (i, cb))
    gcol = (W_QKV + W_Z + W_GLU) // D_MODEL
    (ym,) = ew_fwd(tag + "_merge", fn_merge, g1, [(ya, tD), (yb, tD), (p, tG(gcol)), (p, tG(gcol + 1)), (w["b_o"], pD)],
                   [(S((n_tok, D_MODEL), BF16), tD)])
    y = mm(tag + "_wo", ym, w["w_out"])
    fn_res = lambda col, x_, f_, w_: (x_ + _rms(f_, w_),)
    (xo,) = ew_fwd(tag + "_res", fn_res, g1, [(x, tD), (y, tD), (w["norm_post"], pD)], [(S((n_tok, D_MODEL), F32), tD)])
    sv = dict(x=x, h=h, p=p, c=c, qkvn=qkvn, states=states, o=o, on=on, ya=ya, hglu=hglu, hc=hc, hs=hs, yb=yb, ym=ym, y=y)
    return xo, sv


def mix_bwd(tag, dxo, sv, w):
    n_tok = dxo.shape[0]
    tm = _tm(n_tok)
    nt = n_tok // tm
    g1 = (1, nt)
    tD = _tok(D_MODEL)(tm)
    pD = _par(D_MODEL)
    tC = _tokcol()(tm)
    p = sv["p"]
    sD = lambda dt: S((n_tok, D_MODEL), dt)
    fn_post = lambda col, f_, w_: (_rms(f_, w_),)
    dy, d_post = ew_bwd(tag + "_res_b", fn_post, g1, [(sv["y"], tD), (w["norm_post"], pD)], [(dxo, tD)],
                        [(0, sD(BF16), tD)], [(1, S((1, D_MODEL), F32), pD, False)])
    dym = mm(tag + "_wo_bx", dy, w["w_out"], tb=True)
    d_wout = mm(tag + "_wo_bw", sv["ym"], dy, ta=True)
    tG = lambda cb: pl.BlockSpec((tm, D_MODEL), lambda j, i: (i, cb))
    gcol = (W_QKV + W_Z + W_GLU) // D_MODEL
    dya, dyb, dga, dgb, d_bo = ew_bwd(
        tag + "_merge_b", fn_merge, g1, [(sv["ya"], tD), (sv["yb"], tD), (p, tG(gcol)), (p, tG(gcol + 1)), (w["b_o"], pD)],
        [(dym, tD)], [(0, sD(BF16), tD), (1, sD(BF16), tD), (2, sD(BF16), tD), (3, sD(BF16), tD)],
        [(4, S((1, D_MODEL), F32), pD, False)])
    dhs = mm(tag + "_co_bx", dyb, w["cnv_w_o"], tb=True)
    d_cwo = mm(tag + "_co_bw", sv["hs"], dyb, ta=True)
    dhc, d_lng, d_lnb = ew_bwd(tag + "_ln_b", fn_lnsilu, g1, [(sv["hc"], tD), (w["ln_g"], pD), (w["ln_b"], pD)], [(dhs, tD)],
                               [(0, sD(F32), tD)], [(1, S((1, D_MODEL), F32), pD, False), (2, S((1, D_MODEL), F32), pD, False)])
    dhglu, d_dww, d_dwb = conv_bwd(tag + "_cconv_b", sv["hglu"], 0, D_MODEL, w["dw_w"], dhc, F32, tm)
    nc = D_MODEL // LANES
    dpa, dpg, d_ba, d_bg = ew_bwd(
        tag + "_glu_b", fn_glu, (nc, nt),
        [(p, _tokcol(COL_GLU)(tm)), (p, _tokcol(COL_GLU + nc)(tm)), (w["pw1_b"], _parcol(0)), (w["pw1_b"], _parcol(nc))],
        [(dhglu, tC)], [(0, sD(BF16), tC), (1, sD(BF16), tC)],
        [(2, S((1, D_MODEL), F32), _parcol(0), False), (3, S((1, D_MODEL), F32), _parcol(0), False)])
    don = mm(tag + "_go_bx", dya, w["gdn_w_o"], tb=True)
    d_gwo = mm(tag + "_go_bw", sv["on"], dya, ta=True)
    do, dz, d_gnw = ew_bwd(tag + "_gout_b", fn_gdnout, (HEADS, nt),
                           [(sv["o"], tC), (p, _tokcol(COL_Z)(tm)), (w["gdn_norm_w"], _par(LANES))], [(don, tC)],
                           [(0, sD(F32), tC), (1, sD(BF16), tC)], [(2, S((1, LANES), F32), _par(LANES), True)])
    dqkvn, dpba, d_alog, d_dtb = gdn_bwd(tag + "_gdn_b", sv["qkvn"], p, w["alog"], w["dtb"], sv["states"], do)
    (dc,) = ew_bwd(tag + "_gpost_b", fn_gdnpost, (W_QKV // LANES, nt), [(sv["c"], tC)], [(dqkvn, tC)],
                   [(0, S((n_tok, W_QKV), F32), tC)], [])
    dqkv, d_convw, _ = conv_bwd(tag + "_gconv_b", p, 0, W_QKV, w["conv_w"], dc, BF16, tm)
    dp = jnp.concatenate([dqkv, dz, dpa, dpg, dga, dgb, dpba], axis=1)
    dh = mm(tag + "_in_bx", dp, w["w_all"], tb=True)
    d_wall = mm(tag + "_in_bw", sv["h"], dp, ta=True)
    dx, d_pre = ew_bwd(tag + "_rms_b", fn_rms, g1, [(sv["x"], tD), (w["norm_pre"], pD)], [(dh, tD)],
                       [(0, sD(F32), tD)], [(1, S((1, D_MODEL), F32), pD, False)], add=(dxo, tD))
    grads = dict(norm_pre=d_pre, norm_post=d_post, w_all=d_wall, conv_w=d_convw, alog=d_alog, dtb=d_dtb,
                 gdn_norm_w=d_gnw, gdn_w_o=d_gwo, pw1_b=jnp.concatenate([d_ba, d_bg], axis=1), dw_w=d_dww,
                 dw_b=d_dwb, ln_g=d_lng, ln_b=d_lnb, cnv_w_o=d_cwo, b_o=d_bo, w_out=d_wout)
    return dx, grads


def local_step(x, target, layers):
    saved = []
    for i, lw in enumerate(layers):
        x, s1 = ffn_fwd("ffn", x, lw["ffn1"])
        x, s2 = mix_fwd("mix", x, lw["mix"])
        x, s3 = ffn_fwd("ffn", x, lw["ffn2"])
        saved.append((s1, s2, s3))
    dx, loss = loss_head("loss", x, target, _tm(x.shape[0]))
    grads = [None] * len(layers)
    for i in reversed(range(len(layers))):
        lw = layers[i]
        s1, s2, s3 = saved[i]
        dx, g3 = ffn_bwd("ffn", dx, s3, lw["ffn2"])
        dx, g2 = mix_bwd("mix", dx, s2, lw["mix"])
        dx, g1 = ffn_bwd("ffn", dx, s1, lw["ffn1"])
        grads[i] = dict(ffn1=g1, mix=g2, ffn2=g3)
    return loss, dx, grads


_O_BA = W_QKV + W_Z
_O_GLU = _O_BA + 2 * HEADS


def prep_layer(wl):
    row = lambda v: v.reshape(1, -1).astype(F32)
    bf = lambda v: v.astype(BF16)
    lanes8 = lambda v: jnp.zeros((1, LANES), F32).at[0, HEADS:2 * HEADS].set(v.astype(F32))
    mw = bf(wl["mix_w_in"])
    w_all = jnp.concatenate([mw[:, :_O_BA], mw[:, _O_GLU:], mw[:, _O_BA:_O_GLU],
                             jnp.zeros((D_MODEL, LANES - 2 * HEADS), BF16)], axis=1)
    ffn = lambda k: dict(norm_pre=row(wl[k + "_norm_pre"]), norm_post=row(wl[k + "_norm_post"]),
                         w_in=bf(wl[k + "_w_in"]), w_out=bf(wl[k + "_w_out"]))
    mix = dict(norm_pre=row(wl["mix_norm_pre"]), norm_post=row(wl["mix_norm_post"]), w_all=w_all,
               conv_w=wl["gdn_conv_w"].astype(F32), alog=lanes8(wl["gdn_a_log"]), dtb=lanes8(wl["gdn_dt_bias"]),
               gdn_norm_w=row(wl["gdn_norm_w"]), gdn_w_o=bf(wl["gdn_w_o"]), pw1_b=row(wl["cnv_pw1_b"]),
               dw_w=wl["cnv_dw_w"].astype(F32), dw_b=row(wl["cnv_dw_b"]), ln_g=row(wl["cnv_ln_g"]),
               ln_b=row(wl["cnv_ln_b"]), cnv_w_o=bf(wl["cnv_w_o"]), b_o=row(wl["cnv_b_o"]), w_out=bf(wl["mix_w_out"]))
    return dict(ffn1=ffn("ffn1"), mix=mix, ffn2=ffn("ffn2"))


def unprep_grads(g):
    m = g["mix"]
    wa = m["w_all"]
    out = {}
    for k in ("ffn1", "ffn2"):
        out[k + "_norm_pre"] = g[k]["norm_pre"][0]
        out[k + "_norm_post"] = g[k]["norm_post"][0]
        out[k + "_w_in"] = g[k]["w_in"]
        out[k + "_w_out"] = g[k]["w_out"]
    p_ba = P_ALL - LANES
    out.update(
        mix_norm_pre=m["norm_pre"][0], mix_norm_post=m["norm_post"][0],
        mix_w_in=jnp.concatenate([wa[:, :_O_BA], wa[:, p_ba:p_ba + 2 * HEADS], wa[:, _O_BA:p_ba]], axis=1),
        gdn_conv_w=m["conv_w"], gdn_a_log=m["alog"][0, HEADS:2 * HEADS], gdn_dt_bias=m["dtb"][0, HEADS:2 * HEADS],
        gdn_norm_w=m["gdn_norm_w"][0], gdn_w_o=m["gdn_w_o"], cnv_pw1_b=m["pw1_b"][0], cnv_dw_w=m["dw_w"],
        cnv_dw_b=m["dw_b"][0], cnv_ln_g=m["ln_g"][0], cnv_ln_b=m["ln_b"][0], cnv_w_o=m["cnv_w_o"], cnv_b_o=m["b_o"][0],
        mix_w_out=m["w_out"])
    return out


MESH = pl.DeviceIdType.MESH
ANY = pl.BlockSpec(memory_space=pl.ANY)
N_DEV = 8


def _pos():
    return lax.axis_index("x"), lax.axis_index("y"), lax.axis_index("c")


def _other_chips(x, y):
    return [(1 - x, y), (x, 1 - y), (1 - x, 1 - y)]


def gather_weights(ws):
    n = len(ws)

    def body(*refs):
        w_refs, o_refs = refs[:n], refs[n:2 * n]
        send_sems, recv_sems, loc_sems = refs[2 * n:]
        x, y, c = _pos()
        b = 2 * x + y
        chips = _other_chips(x, y)

        def ici(a, j, blk):
            return pltpu.make_async_remote_copy(
                src_ref=w_refs[a].at[c], dst_ref=o_refs[a].at[blk, c], send_sem=send_sems.at[4 * a + j],
                recv_sem=recv_sems.at[4 * a + j], device_id=(chips[j][0], chips[j][1], c), device_id_type=MESH)

        def d2d(a, layer):
            part = o_refs[a].at[:, pl.ds(layer, 1)]
            return pltpu.make_async_remote_copy(
                src_ref=part, dst_ref=part, send_sem=send_sems.at[4 * a + 3], recv_sem=recv_sems.at[4 * a + 3],
                device_id=(x, y, 1 - c), device_id_type=MESH)

        locs = [pltpu.make_async_copy(w_refs[a].at[c], o_refs[a].at[b, c], loc_sems.at[a]) for a in range(n)]
        for cp in locs:
            cp.start()
        sends = [ici(a, j, b) for a in range(n) for j in range(3)]
        for cp in sends:
            cp.start()
        for a in range(n):
            for j in range(3):
                ici(a, j, 2 * chips[j][0] + chips[j][1]).wait_recv()
        for cp in locs:
            cp.wait()
        passed = [d2d(a, c) for a in range(n)]
        for cp in passed:
            cp.start()
        for a in range(n):
            d2d(a, 1 - c).wait_recv()
        for cp in sends + passed:
            cp.wait_send()

    return pl.pallas_call(
        body, name="gather_weights", in_specs=[ANY] * n, out_specs=[ANY] * n,
        out_shape=[S((N_BLK,) + w.shape, w.dtype) for w in ws],
        scratch_shapes=[pltpu.SemaphoreType.DMA((4 * n,)), pltpu.SemaphoreType.DMA((4 * n,)),
                        pltpu.SemaphoreType.DMA((n,))])(*ws)


def rs_sibling(gs):
    n = len(gs)

    def body(*refs):
        g_refs, r_refs = refs[:n], refs[n:2 * n]
        send_sems, recv_sems = refs[2 * n:]
        x, y, c = _pos()

        def cp(k):
            hr = gs[k].shape[1] // 2
            return pltpu.make_async_remote_copy(
                src_ref=g_refs[k].at[:, pl.ds((1 - c) * hr, hr)], dst_ref=r_refs[k], send_sem=send_sems.at[k],
                recv_sem=recv_sems.at[k], device_id=(x, y, 1 - c), device_id_type=MESH)

        cps = [cp(k) for k in range(n)]
        for d in cps:
            d.start()
        for d in cps:
            d.wait_recv()
        for d in cps:
            d.wait_send()

    return pl.pallas_call(
        body, name="rs_sibling", in_specs=[ANY] * n, out_specs=[ANY] * n,
        out_shape=[S((N_BLK, g.shape[1] // 2, g.shape[2]), g.dtype) for g in gs],
        scratch_shapes=[pltpu.SemaphoreType.DMA((n,)), pltpu.SemaphoreType.DMA((n,))])(*gs)


def rs_chips(ps):
    n = len(ps)

    def body(*refs):
        p_refs, r_refs = refs[:n], refs[n:2 * n]
        send_sems, recv_sems, loc_sems = refs[2 * n:]
        x, y, c = _pos()
        b = 2 * x + y
        chips = _other_chips(x, y)

        def ici(k, j, src_blk, dst_slot):
            return pltpu.make_async_remote_copy(
                src_ref=p_refs[k].at[src_blk], dst_ref=r_refs[k].at[dst_slot], send_sem=send_sems.at[3 * k + j],
                recv_sem=recv_sems.at[3 * k + j], device_id=(chips[j][0], chips[j][1], c), device_id_type=MESH)

        locs = [pltpu.make_async_copy(p_refs[k].at[b], r_refs[k].at[b], loc_sems.at[k]) for k in range(n)]
        for d in locs:
            d.start()
        sends = [ici(k, j, 2 * chips[j][0] + chips[j][1], b) for k in range(n) for j in range(3)]
        for d in sends:
            d.start()
        for k in range(n):
            for j in range(3):
                ici(k, j, b, 2 * chips[j][0] + chips[j][1]).wait_recv()
        for d in locs:
            d.wait()
        for d in sends:
            d.wait_send()

    return pl.pallas_call(
        body, name="rs_chips", in_specs=[ANY] * n, out_specs=[ANY] * n,
        out_shape=[S(p.shape, p.dtype) for p in ps],
        scratch_shapes=[pltpu.SemaphoreType.DMA((3 * n,)), pltpu.SemaphoreType.DMA((3 * n,)),
                        pltpu.SemaphoreType.DMA((n,))])(*ps)


def ag_sibling(hs):
    n = len(hs)

    def body(*refs):
        h_refs, o_refs = refs[:n], refs[n:2 * n]
        send_sems, recv_sems, loc_sems = refs[2 * n:]
        x, y, c = _pos()

        def cp(k, half):
            hr = hs[k].shape[0]
            return pltpu.make_async_remote_copy(
                src_ref=h_refs[k], dst_ref=o_refs[k].at[pl.ds(half * hr, hr)], send_sem=send_sems.at[k],
                recv_sem=recv_sems.at[k], device_id=(x, y, 1 - c), device_id_type=MESH)

        locs = [pltpu.make_async_copy(h_refs[k], o_refs[k].at[pl.ds(c * hs[k].shape[0], hs[k].shape[0])], loc_sems.at[k])
                for k in range(n)]
        for d in locs:
            d.start()
        cps = [cp(k, c) for k in range(n)]
        for d in cps:
            d.start()
        for k in range(n):
            cp(k, 1 - c).wait_recv()
        for d in locs:
            d.wait()
        for d in cps:
            d.wait_send()

    return pl.pallas_call(
        body, name="ag_sibling", in_specs=[ANY] * n, out_specs=[ANY] * n,
        out_shape=[S((2 * h.shape[0], h.shape[1]), h.dtype) for h in hs],
        scratch_shapes=[pltpu.SemaphoreType.DMA((n,)), pltpu.SemaphoreType.DMA((n,)), pltpu.SemaphoreType.DMA((n,))])(*hs)


def allreduce_small(v):
    rows = v.shape[0]

    def body(v_ref, o_ref, buf, send_sems, recv_sems):
        x, y, c = _pos()
        me = 4 * x + 2 * y + c
        buf[me] = v_ref[...]

        def cp(d, slot):
            dx, dy, dc = (d >> 2) & 1, (d >> 1) & 1, d & 1
            peer = (1 - x if dx else x, 1 - y if dy else y, 1 - c if dc else c)
            return pltpu.make_async_remote_copy(
                src_ref=v_ref, dst_ref=buf.at[slot], send_sem=send_sems.at[d - 1], recv_sem=recv_sems.at[d - 1],
                device_id=peer, device_id_type=MESH)

        cps = [cp(d, me) for d in range(1, N_DEV)]
        for d in cps:
            d.start()
        for d in range(1, N_DEV):
            dx, dy, dc = (d >> 2) & 1, (d >> 1) & 1, d & 1
            src = 4 * (1 - x if dx else x) + 2 * (1 - y if dy else y) + (1 - c if dc else c)
            cp(d, src).wait_recv()
        for d in cps:
            d.wait_send()
        acc = buf[0]
        for s in range(1, N_DEV):
            acc = acc + buf[s]
        o_ref[...] = acc

    vm = pl.BlockSpec(memory_space=pltpu.VMEM)
    return pl.pallas_call(
        body, name="allreduce_small", in_specs=[vm], out_specs=vm, out_shape=S(v.shape, v.dtype),
        scratch_shapes=[pltpu.VMEM((N_DEV, rows, LANES), F32), pltpu.SemaphoreType.DMA((N_DEV - 1,)),
                        pltpu.SemaphoreType.DMA((N_DEV - 1,))])(v)


def _rows_tile(rows, cols, cap_bytes=1 << 20, mult=8):
    best = None
    for t in range(mult, rows + 1, mult):
        if rows % t == 0 and t * cols * 4 <= cap_bytes:
            best = t
    return best if best is not None else rows


def add_half(name, g, r, c_arr):
    _, hr, cols = r.shape
    tr = _rows_tile(hr, cols, mult=16)
    nb = hr // tr

    def body(c_ref, g_ref, r_ref, o_ref):
        o_ref[...] = (g_ref[...] + r_ref[...]).astype(o_ref.dtype)

    gs = pltpu.PrefetchScalarGridSpec(
        num_scalar_prefetch=1, grid=(N_BLK, nb),
        in_specs=[pl.BlockSpec((None, tr, cols), lambda b, i, cr: (b, cr[0] * nb + i, 0)),
                  pl.BlockSpec((None, tr, cols), lambda b, i, cr: (b, i, 0))],
        out_specs=pl.BlockSpec((None, tr, cols), lambda b, i, cr: (b, i, 0)))
    return pl.pallas_call(body, name=name, grid_spec=gs, out_shape=S(r.shape, BF16),
                          compiler_params=_cp(("parallel", "parallel")))(c_arr, g, r)


def sum_chips(name, r):
    _, hr, cols = r.shape
    tr = _rows_tile(hr, cols, mult=16)

    def body(r_ref, o_ref):
        acc = r_ref[0].astype(F32)
        for s in range(1, N_BLK):
            acc = acc + r_ref[s].astype(F32)
        o_ref[...] = acc

    return pl.pallas_call(
        body, name=name, grid=(hr // tr,), in_specs=[pl.BlockSpec((N_BLK, tr, cols), lambda i: (0, i, 0))],
        out_specs=pl.BlockSpec((tr, cols), lambda i: (i, 0)), out_shape=S((hr, cols), F32),
        compiler_params=_cp(("parallel",)))(r)


def adamw(name, w, m, v, gs):
    rows, cols = w.shape
    two = len(gs) == 2
    span = rows // 2 if two else rows
    tr = _rows_tile(span, cols, 1 << 19)
    nb = span // tr

    def body(w_ref, m_ref, v_ref, *rest):
        g_refs, (go_ref, d_ref, mo_ref, vo_ref) = rest[:len(gs)], rest[len(gs):]
        if two:
            g = jnp.where(pl.program_id(0) < nb, g_refs[0][...], g_refs[1][...])
        else:
            g = g_refs[0][...]
        mn = ADAM_B1 * m_ref[...] + (1.0 - ADAM_B1) * g
        vn = ADAM_B2 * v_ref[...] + (1.0 - ADAM_B2) * jnp.square(g)
        m_hat = mn / (1.0 - ADAM_B1 ** ADAM_STEP)
        v_hat = vn / (1.0 - ADAM_B2 ** ADAM_STEP)
        go_ref[...] = g
        d_ref[...] = -ADAM_LR * (m_hat / (jnp.sqrt(v_hat) + ADAM_EPS) + ADAM_WD * w_ref[...])
        mo_ref[...] = mn
        vo_ref[...] = vn

    full = pl.BlockSpec((tr, cols), lambda i: (i, 0))
    if two:
        g_specs = [pl.BlockSpec((tr, cols), lambda i: (jnp.minimum(i, nb - 1), 0)),
                   pl.BlockSpec((tr, cols), lambda i: (jnp.maximum(i - nb, 0), 0))]
    else:
        g_specs = [full]
    return pl.pallas_call(
        body, name=name, grid=(2 * nb if two else nb,), in_specs=[full, full, full] + g_specs, out_specs=[full] * 4,
        out_shape=[S((rows, cols), F32)] * 4, compiler_params=_cp(("parallel",)))(w, m, v, *gs)


WEIGHTS = ["ffn1_norm_pre", "ffn1_norm_post", "ffn1_w_in", "ffn1_w_out", "mix_norm_pre", "mix_norm_post", "mix_w_in",
           "gdn_conv_w", "gdn_a_log", "gdn_dt_bias", "gdn_norm_w", "gdn_w_o", "cnv_pw1_b", "cnv_dw_w", "cnv_dw_b",
           "cnv_ln_g", "cnv_ln_b", "cnv_w_o", "cnv_b_o", "mix_w_out", "ffn2_norm_pre", "ffn2_norm_post", "ffn2_w_in",
           "ffn2_w_out"]
BIG = {"ffn1_w_in": True, "ffn1_w_out": False, "mix_w_in": True, "gdn_conv_w": True, "gdn_w_o": False,
       "cnv_dw_w": True, "cnv_w_o": False, "mix_w_out": False, "ffn2_w_in": True, "ffn2_w_out": False}
TINY = {"gdn_conv_w": (32, LANES), "cnv_dw_w": (64, LANES)}
SMALL = [n for n in WEIGHTS if n not in BIG]


def _whole(name, blocks):
    if BIG[name]:
        return jnp.transpose(blocks, (1, 0, 2)).reshape(blocks.shape[1], N_BLK * blocks.shape[2])
    return blocks.reshape(N_BLK * blocks.shape[1], blocks.shape[2])


def _blocks(name, whole):
    r, cfull = whole.shape
    if BIG[name]:
        blk = jnp.transpose(whole.reshape(r, N_BLK, cfull // N_BLK), (1, 0, 2))
    else:
        blk = whole.reshape(N_BLK, r // N_BLK, cfull)
    if name in TINY:
        tr, tc = TINY[name]
        flat = blk.reshape(N_BLK, -1)
        blk = jnp.pad(flat, ((0, 0), (0, tr * tc - flat.shape[1]))).reshape(N_BLK, tr, tc)
    return blk


def _pack(parts):
    rows = []
    for p in parts:
        flat = p.reshape(-1).astype(F32)
        rows.append(jnp.pad(flat, (0, (-flat.shape[0]) % LANES)).reshape(-1, LANES))
    out = jnp.concatenate(rows, axis=0)
    return jnp.pad(out, ((0, (-out.shape[0]) % 8), (0, 0)))


def _unpack(packed, shapes):
    out, r = [], 0
    for shp in shapes:
        size = math.prod(shp)
        nr = -(-size // LANES)
        out.append(packed[r:r + nr].reshape(-1)[:size].reshape(shp))
        r += nr
    return out


def kernel(x, ffn1_norm_pre, ffn1_norm_post, ffn1_w_in, ffn1_w_out, mix_norm_pre, mix_norm_post, mix_w_in, gdn_conv_w, gdn_a_log, gdn_dt_bias, gdn_norm_w, gdn_w_o, cnv_pw1_b, cnv_dw_w, cnv_dw_b, cnv_ln_g, cnv_ln_b, cnv_w_o, cnv_b_o, mix_w_out, ffn2_norm_pre, ffn2_norm_post, ffn2_w_in, ffn2_w_out, loss_target, m_ffn1_norm_pre, m_ffn1_norm_post, m_ffn1_w_in, m_ffn1_w_out, m_mix_norm_pre, m_mix_norm_post, m_mix_w_in, m_gdn_conv_w, m_gdn_a_log, m_gdn_dt_bias, m_gdn_norm_w, m_gdn_w_o, m_cnv_pw1_b, m_cnv_dw_w, m_cnv_dw_b, m_cnv_ln_g, m_cnv_ln_b, m_cnv_w_o, m_cnv_b_o, m_mix_w_out, m_ffn2_norm_pre, m_ffn2_norm_post, m_ffn2_w_in, m_ffn2_w_out, v_ffn1_norm_pre, v_ffn1_norm_post, v_ffn1_w_in, v_ffn1_w_out, v_mix_norm_pre, v_mix_norm_post, v_mix_w_in, v_gdn_conv_w, v_gdn_a_log, v_gdn_dt_bias, v_gdn_norm_w, v_gdn_w_o, v_cnv_pw1_b, v_cnv_dw_w, v_cnv_dw_b, v_cnv_ln_g, v_cnv_ln_b, v_cnv_w_o, v_cnv_b_o, v_mix_w_out, v_ffn2_norm_pre, v_ffn2_norm_post, v_ffn2_w_in, v_ffn2_w_out):
    args = locals()
    wts = {n: args[n] for n in WEIGHTS}
    mom = {n: args["m_" + n] for n in WEIGHTS}
    var = {n: args["v_" + n] for n in WEIGHTS}
    big = list(BIG)

    gathered = dict(zip(big, gather_weights([wts[n].astype(BF16) for n in big])))
    layers = []
    for l in range(DEPTH):
        wl = {n: _whole(n, gathered[n][:, l]) for n in big}
        wl.update({n: wts[n][l] for n in SMALL})
        layers.append(prep_layer(wl))

    loss, dx, grads = local_step(x[0], loss_target[0], layers)
    gw = [unprep_grads(g) for g in grads]

    small_shapes = [wts[n].shape for n in SMALL]
    packed = _pack([jnp.stack([gw[l][n] for l in range(DEPTH)]) for n in SMALL] + [loss])
    total = allreduce_small(packed)
    small_g = dict(zip(SMALL, _unpack(total, small_shapes)))
    loss_sum = total[sum(-(-math.prod(s) // LANES) for s in small_shapes), 0]

    c_arr = lax.axis_index("c").astype(jnp.int32).reshape(1)
    keys = [(n, l) for n in big for l in range(DEPTH)]
    blocks = [_blocks(n, gw[l][n]) for n, l in keys]
    from_sib = rs_sibling(blocks)
    partial = [add_half("add_half", g, r, c_arr) for g, r in zip(blocks, from_sib)]
    by_chip = rs_chips(partial)
    halves = [sum_chips("sum_chips", r) for r in by_chip]
    summed = dict(zip(keys, ag_sibling(halves)))

    out_g, out_d, out_m, out_v = {}, {}, {}, {}
    for n in big:
        shp = wts[n].shape
        gs = [summed[(n, l)] for l in range(DEPTH)]
        if n in TINY:
            gs = [jnp.concatenate([g.reshape(-1)[:shp[1] * shp[2]].reshape(shp[1], shp[2]) for g in gs], axis=0)]
        two_d = lambda a: a.reshape(DEPTH * shp[1], shp[2])
        res = adamw("adamw", two_d(wts[n]), two_d(mom[n]), two_d(var[n]), gs)
        out_g[n], out_d[n], out_m[n], out_v[n] = [r.reshape(shp) for r in res]

    pk = lambda d: _pack([d[n] for n in SMALL])
    res = adamw("adamw_small", pk(wts), pk(mom), pk(var), [pk(small_g)])
    for d, r in zip((out_g, out_d, out_m, out_v), res):
        d.update(dict(zip(SMALL, _unpack(r, small_shapes))))

    return (loss_sum, dx[None], *[out_g[n] for n in WEIGHTS], *[out_d[n] for n in WEIGHTS],
            *[out_m[n] for n in WEIGHTS], *[out_v[n] for n in WEIGHTS])
```

```python
import functools
import math

import jax
import jax.numpy as jnp
from jax import lax
from jax.experimental import pallas as pl
from jax.experimental.pallas import tpu as pltpu

F32, BF16 = jnp.float32, jnp.bfloat16
S = jax.ShapeDtypeStruct

D_MODEL = 1024
D_FF = 2816
HEADS = 8
DK = 128
CHUNK = 64
GDN_CONV = 4
CNV_K = 31
W_QKV = 3 * HEADS * DK
W_Z = HEADS * DK
W_GLU = 2 * D_MODEL
W_GATE = 2 * D_MODEL
P_IN = W_QKV + W_Z + 2 * HEADS + W_GLU + W_GATE
LANES = 128
P_ALL = W_QKV + W_Z + W_GLU + W_GATE + LANES
COL_Z = W_QKV // LANES
COL_GLU = (W_QKV + W_Z) // LANES
COL_GATE = (W_QKV + W_Z + W_GLU) // LANES
COL_BA = (W_QKV + W_Z + W_GLU + W_GATE) // LANES
RMS_EPS = 1e-6
LN_EPS = 1e-5
DEPTH = 2
N_BLK = 4
VMEM_LIMIT = 56 * 1024 * 1024

ADAM_LR, ADAM_B1, ADAM_B2, ADAM_EPS, ADAM_WD, ADAM_STEP = 0.001, 0.9, 0.999, 1e-08, 0.01, 10


def _cp(sem):
    return pltpu.CompilerParams(dimension_semantics=sem, vmem_limit_bytes=VMEM_LIMIT)


MM_VMEM_BUDGET = 36 * 1024 * 1024


def _mm_tiles(m, n, k_bytes_a, k_bytes_b, out_bytes):
    best = None
    for tm in (1024, 512, 256, 128):
        if m % tm:
            continue
        for tn in (1024, 512, 640, 256, 384, 128):
            if n % tn:
                continue
            need = 2 * (tm * k_bytes_a + tn * k_bytes_b + tm * tn * out_bytes)
            if need <= MM_VMEM_BUDGET and (best is None or tm * tn > best[0] * best[1]):
                best = (tm, tn)
    if best is None:
        raise ValueError((m, n, k_bytes_a, k_bytes_b))
    return best


def mm_nt_sum(name, parts, b):
    m, n = parts[0][0].shape[0], b.shape[0]
    k_total = sum(a.shape[1] for a, _ in parts)
    tm, tn = _mm_tiles(m, n, k_total * 2, k_total * 2, 4)
    n_p = len(parts)

    def body(*refs):
        o_ref = refs[2 * n_p]
        acc = None
        for a_ref, b_ref in zip(refs[:n_p], refs[n_p:2 * n_p]):
            t = lax.dot_general(a_ref[...], b_ref[...], (((1,), (1,)), ((), ())), preferred_element_type=F32)
            acc = t if acc is None else acc + t
        o_ref[...] = acc

    a_specs = [pl.BlockSpec((tm, a.shape[1]), lambda i, j: (i, 0)) for a, _ in parts]
    b_specs = [pl.BlockSpec((tn, a.shape[1]), functools.partial(lambda i, j, c: (j, c), c=col)) for a, col in parts]
    return pl.pallas_call(
        body, name=name, grid=(m // tm, n // tn), in_specs=a_specs + b_specs,
        out_specs=pl.BlockSpec((tm, tn), lambda i, j: (i, j)), out_shape=S((m, n), F32),
        compiler_params=_cp(("parallel", "parallel")))(*[a for a, _ in parts], *([b] * n_p))


def mm(name, a, b, ta=False, tb=False, out_dtype=F32):
    k = a.shape[0] if ta else a.shape[1]
    m = a.shape[1] if ta else a.shape[0]
    n = b.shape[0] if tb else b.shape[1]
    assert k == (b.shape[1] if tb else b.shape[0]), (name, a.shape, b.shape)
    tm, tn = _mm_tiles(m, n, k * a.dtype.itemsize, k * b.dtype.itemsize, jnp.dtype(out_dtype).itemsize)
    a_spec = pl.BlockSpec((k, tm), lambda i, j: (0, i)) if ta else pl.BlockSpec((tm, k), lambda i, j: (i, 0))
    b_spec = pl.BlockSpec((tn, k), lambda i, j: (j, 0)) if tb else pl.BlockSpec((k, tn), lambda i, j: (0, j))
    dims = (((0 if ta else 1,), (1 if tb else 0,)), ((), ()))

    def body(a_ref, b_ref, o_ref):
        o_ref[...] = lax.dot_general(a_ref[...], b_ref[...], dims, preferred_element_type=F32).astype(o_ref.dtype)

    return pl.pallas_call(
        body, name=name, grid=(m // tm, n // tn), in_specs=[a_spec, b_spec],
        out_specs=pl.BlockSpec((tm, tn), lambda i, j: (i, j)), out_shape=S((m, n), out_dtype),
        compiler_params=_cp(("parallel", "parallel")))(a, b)


def ew_fwd(name, fn, grid, ins, outs):
    n_in = len(ins)

    def body(*refs):
        vals = [r[...] for r in refs[:n_in]]
        res = fn(pl.program_id(0), *vals)
        for r, v in zip(refs[n_in:], res):
            r[...] = v.astype(r.dtype)

    out = pl.pallas_call(
        body, name=name, grid=grid, in_specs=[s for _, s in ins], out_specs=[s for _, s in outs],
        out_shape=[sd for sd, _ in outs], compiler_params=_cp(("parallel", "parallel")))(*[a for a, _ in ins])
    return out


def ew_bwd(name, fn, grid, ins, cts, wrt, acc, add=None):
    n_in, n_ct, n_wrt, n_acc = len(ins), len(cts), len(wrt), len(acc)
    has_add = add is not None

    def body(*refs):
        in_refs = refs[:n_in]
        ct_refs = refs[n_in:n_in + n_ct]
        pos = n_in + n_ct
        add_ref = refs[pos] if has_add else None
        pos += 1 if has_add else 0
        wrt_refs = refs[pos:pos + n_wrt]
        acc_refs = refs[pos + n_wrt:pos + n_wrt + n_acc]
        col, tok = pl.program_id(0), pl.program_id(1)
        vals = [r[...] for r in in_refs]
        _, vjp = jax.vjp(lambda *a: fn(col, *a), *vals)
        grads = vjp(tuple(c[...].astype(F32) for c in ct_refs))
        for pos_w, ((idx, _, _), r) in enumerate(zip(wrt, wrt_refs)):
            g = grads[idx]
            if has_add and pos_w == 0:
                g = g + add_ref[...]
            r[...] = g.astype(r.dtype)
        for (idx, _, _, over_cols), r in zip(acc, acc_refs):
            first = (tok == 0) & (col == 0) if over_cols else tok == 0

            @pl.when(first)
            def _():
                r[...] = jnp.zeros_like(r)

            r[...] += grads[idx]

    arrays = [a for a, _ in ins] + [a for a, _ in cts] + ([add[0]] if has_add else [])
    in_specs = [s for _, s in ins] + [s for _, s in cts] + ([add[1]] if has_add else [])
    over_any = any(o for *_, o in acc)
    out = pl.pallas_call(
        body, name=name, grid=grid, in_specs=in_specs,
        out_specs=[s for _, _, s in wrt] + [s for _, _, s, _ in acc],
        out_shape=[sd for _, sd, _ in wrt] + [sd for _, sd, _, _ in acc],
        compiler_params=_cp(("arbitrary" if over_any else "parallel", "arbitrary")))(*arrays)
    return out


def _tok(width, col=0):
    return lambda tm: pl.BlockSpec((tm, width), lambda j, i: (i, col))


def _tokcol(off=0):
    return lambda tm: pl.BlockSpec((tm, LANES), lambda j, i: (i, off + j))


def _par(width, col=0):
    return pl.BlockSpec((1, width), lambda j, i: (0, col))


def _parcol(off=0):
    return pl.BlockSpec((1, LANES), lambda j, i: (0, off + j))


def _rms(x, w, eps=RMS_EPS):
    return x * lax.rsqrt(jnp.mean(x * x, axis=-1, keepdims=True) + eps) * w


def _silu(x):
    return x * jax.nn.sigmoid(x)


def fn_rms(col, x, w):
    return (_rms(x, w),)


def fn_swiglu(col, gate, up):
    return (_silu(gate) * up,)


def swiglu_bwd(name, u, da, tm):
    n_tok, f2 = u.shape
    f = f2 // 2

    def body(g_ref, up_ref, da_ref, o_ref):
        g, d = g_ref[...], da_ref[...]
        s = jax.nn.sigmoid(g)
        o_ref[:, :f] = (d * up_ref[...] * (s * (1.0 + g * (1.0 - s)))).astype(o_ref.dtype)
        o_ref[:, f:] = (d * (g * s)).astype(o_ref.dtype)

    half = lambda c: pl.BlockSpec((tm, f), lambda i: (i, c))
    return pl.pallas_call(
        body, name=name, grid=(n_tok // tm,), in_specs=[half(0), half(1), half(0)],
        out_specs=pl.BlockSpec((tm, f2), lambda i: (i, 0)), out_shape=S((n_tok, f2), BF16),
        compiler_params=_cp(("parallel",)))(u, u, da)


def fn_gdnpost(col, c):
    typ = col // HEADS
    y = _silu(c)
    n = y * lax.rsqrt(jnp.sum(y * y, axis=-1, keepdims=True) + 1e-6)
    n = n * jnp.where(typ == 0, DK ** -0.5, 1.0)
    return (jnp.where(typ < 2, n, y),)


def fn_gdnout(col, o, z, nw):
    return (_rms(o, nw) * _silu(z),)


def fn_glu(col, a, g, ba, bg):
    return ((a + ba) * jax.nn.sigmoid(g + bg),)


def fn_lnsilu(col, h, g, b):
    mu = jnp.mean(h, axis=-1, keepdims=True)
    var = jnp.mean(jnp.square(h - mu), axis=-1, keepdims=True)
    return (_silu((h - mu) * lax.rsqrt(var + LN_EPS) * g + b),)


def fn_merge(col, ya, yb, ga, gb, bo):
    return (jax.nn.sigmoid(ga) * ya + jax.nn.sigmoid(gb) * (yb + bo),)


HALO = 32


def conv_fwd(name, x, col_off, n_ch, w, bias, tb):
    n_tok = x.shape[0]
    k = w.shape[0]
    nt = n_tok // tb

    def body(xp_ref, xc_ref, w_ref, *rest):
        if bias is not None:
            b_ref, o_ref, xs = rest
        else:
            o_ref, xs = rest
        i = pl.program_id(1)
        xs[0:HALO, :] = jnp.where(i == 0, 0.0, xp_ref[tb - HALO:tb, :])
        xs[HALO:HALO + tb, :] = xc_ref[...]
        acc = jnp.zeros((tb, LANES), F32)
        for j in range(k):
            s = k - 1 - j
            acc = acc + w_ref[j:j + 1, :] * xs[HALO - s:HALO - s + tb, :]
        if bias is not None:
            acc = acc + b_ref[...]
        o_ref[...] = acc

    in_specs = [pl.BlockSpec((tb, LANES), lambda j, i: (jnp.maximum(i - 1, 0), col_off + j)),
                pl.BlockSpec((tb, LANES), lambda j, i: (i, col_off + j)),
                pl.BlockSpec((k, LANES), lambda j, i: (0, j))]
    args = [x, x, w]
    if bias is not None:
        in_specs.append(pl.BlockSpec((1, LANES), lambda j, i: (0, j)))
        args.append(bias)
    return pl.pallas_call(
        body, name=name, grid=(n_ch // LANES, nt), in_specs=in_specs,
        out_specs=pl.BlockSpec((tb, LANES), lambda j, i: (i, j)), out_shape=S((n_tok, n_ch), F32),
        scratch_shapes=[pltpu.VMEM((HALO + tb, LANES), F32)],
        compiler_params=_cp(("parallel", "parallel")))(*args)


def conv_bwd(name, x, col_off, n_ch, w, dy, dx_dtype, tb):
    n_tok = x.shape[0]
    k = w.shape[0]
    nt = n_tok // tb

    def body(xp_ref, xc_ref, w_ref, dyc_ref, dyn_ref, dx_ref, dw_ref, db_ref, xs, dys):
        i = pl.program_id(1)
        xs[0:HALO, :] = jnp.where(i == 0, 0.0, xp_ref[tb - HALO:tb, :])
        xs[HALO:HALO + tb, :] = xc_ref[...]
        dyc = dyc_ref[...]
        dys[0:tb, :] = dyc
        dys[tb:tb + HALO, :] = jnp.where(i == nt - 1, 0.0, dyn_ref[0:HALO, :])

        @pl.when(i == 0)
        def _():
            dw_ref[...] = jnp.zeros_like(dw_ref)
            db_ref[...] = jnp.zeros_like(db_ref)

        acc = jnp.zeros((tb, LANES), F32)
        for j in range(k):
            s = k - 1 - j
            acc = acc + w_ref[j:j + 1, :] * dys[s:s + tb, :]
            dw_ref[j:j + 1, :] += jnp.sum(dyc * xs[HALO - s:HALO - s + tb, :], axis=0, keepdims=True)
        dx_ref[...] = acc.astype(dx_ref.dtype)
        db_ref[...] += jnp.sum(dyc, axis=0, keepdims=True)

    in_specs = [pl.BlockSpec((tb, LANES), lambda j, i: (jnp.maximum(i - 1, 0), col_off + j)),
                pl.BlockSpec((tb, LANES), lambda j, i: (i, col_off + j)),
                pl.BlockSpec((k, LANES), lambda j, i: (0, j)),
                pl.BlockSpec((tb, LANES), lambda j, i: (i, j)),
                pl.BlockSpec((tb, LANES), lambda j, i: (jnp.minimum(i + 1, nt - 1), j))]
    return pl.pallas_call(
        body, name=name, grid=(n_ch // LANES, nt), in_specs=in_specs,
        out_specs=[pl.BlockSpec((tb, LANES), lambda j, i: (i, j)),
                   pl.BlockSpec((k, LANES), lambda j, i: (0, j)),
                   pl.BlockSpec((1, LANES), lambda j, i: (0, j))],
        out_shape=[S((n_tok, n_ch), dx_dtype), S((k, n_ch), F32), S((1, n_ch), F32)],
        scratch_shapes=[pltpu.VMEM((HALO + tb, LANES), F32), pltpu.VMEM((tb + HALO, LANES), F32)],
        compiler_params=_cp(("parallel", "arbitrary")))(x, x, w, dy, dy)


GDN_GROUP = 2


def _dotb(a, b, ca, cb):
    return lax.dot_general(a.astype(BF16), b.astype(BF16), (((ca,), (cb,)), ((), ())), preferred_element_type=F32)


def _dot32(a, b, ca, cb):
    return lax.dot_general(a, b, (((ca,), (cb,)), ((), ())), preferred_element_type=F32,
                           precision=lax.Precision.HIGHEST)


def _dot3_many(xs, ys, ca, cb):
    xh = [x.astype(BF16) for x in xs]
    xl = [(x - h.astype(F32)).astype(BF16) for x, h in zip(xs, xh)]
    yh = [y.astype(BF16) for y in ys]
    yl = [(y - h.astype(F32)).astype(BF16) for y, h in zip(ys, yh)]
    dg = lambda p, q: lax.dot_general(p, q, (((ca,), (cb,)), ((), ())), preferred_element_type=F32)
    hh = [dg(p, q) for p, q in zip(xh, yh)]
    hl = [dg(p, q) for p, q in zip(xh, yl)]
    lh = [dg(p, q) for p, q in zip(xl, yh)]
    return [a + (b + c) for a, b, c in zip(hh, hl, lh)]


@jax.custom_vjp
def _mm3_many(xs, ys):
    return _dot3_many(xs, ys, 1, 0)


def _mm3_fwd(xs, ys):
    return _dot3_many(xs, ys, 1, 0), (xs, ys)


def _mm3_bwd(res, cts):
    xs, ys = res
    return _dot3_many(cts, ys, 1, 1), _dot3_many(xs, cts, 0, 0)


_mm3_many.defvjp(_mm3_fwd, _mm3_bwd)


@jax.custom_vjp
def _inv_unit_lower_many(mats):
    n = mats[0].shape[0]
    eye = (lax.broadcasted_iota(jnp.int32, (n, n), 0) == lax.broadcasted_iota(jnp.int32, (n, n), 1)).astype(F32)
    inv = [eye - a for a in mats]
    p = list(mats)
    for _ in range(int(math.log2(n)) - 1):
        p = _dot3_many(p, p, 1, 0)
        upd = _dot3_many(inv, p, 1, 0)
        inv = [i + u for i, u in zip(inv, upd)]
    return inv


def _inv_fwd(mats):
    t = _inv_unit_lower_many(mats)
    return t, t


def _inv_bwd(t, dt):
    x = _dot3_many(t, dt, 0, 0)
    return ([-y for y in _dot3_many(x, t, 1, 1)],)


_inv_unit_lower_many.defvjp(_inv_fwd, _inv_bwd)


def _softplus(x):
    return jnp.maximum(x, 0.0) + jnp.log(1.0 + jnp.exp(-jnp.abs(x)))


def _gdn_intra(qs, ks, vs, pbas, alog, dtb):
    c = pbas[0].shape[0]
    row = lax.broadcasted_iota(jnp.int32, (c, c), 0)
    colm = lax.broadcasted_iota(jnp.int32, (c, c), 1)
    causal, strict = row >= colm, row > colm
    tril = causal.astype(F32)
    lane = lax.broadcasted_iota(jnp.int32, (1, LANES), 1)
    sub = lax.broadcasted_iota(jnp.int32, (LANES, 1), 0)
    last = (lax.broadcasted_iota(jnp.int32, (c, 1), 0) == c - 1).astype(F32)
    beta_all = [jax.nn.sigmoid(pb) for pb in pbas]
    g_all = [-jnp.exp(alog) * _softplus(pb + dtb) for pb in pbas]
    gc_all = [_dot32(tril, ga, 1, 0) for ga in g_all]
    gr_all = [_dot32(ga, tril, 0, 1) for ga in g_all]
    idx = [(g, h) for g in range(len(pbas)) for h in range(HEADS)]
    beta = [jnp.sum(beta_all[g] * (lane == h).astype(F32), axis=1, keepdims=True) for g, h in idx]
    gc = [jnp.sum(gc_all[g] * (lane == HEADS + h).astype(F32), axis=1, keepdims=True) for g, h in idx]
    gr = [jnp.sum(gr_all[g] * (sub == HEADS + h).astype(F32), axis=0, keepdims=True) for g, h in idx]
    decay = [jnp.where(causal, jnp.exp(jnp.where(causal, a - b, 0.0)), 0.0) for a, b in zip(gc, gr)]
    kk = [_dotb(k, k, 1, 1) for k in ks]
    tinv = _inv_unit_lower_many([jnp.where(strict, x * d * b, 0.0) for x, d, b in zip(kk, decay, beta)])
    eg = [jnp.exp(a) for a in gc]
    g_last = [jnp.sum(a * last, axis=0, keepdims=True) for a in gc]
    us = _mm3_many(tinv, [v * b for v, b in zip(vs, beta)])
    ws = _mm3_many(tinv, [k * (b * e) for k, b, e in zip(ks, beta, eg)])
    qds = [q * e for q, e in zip(qs, eg)]
    kds = [k * jnp.exp(gl - a) for k, gl, a in zip(ks, g_last, gc)]
    qks = [_dotb(q, k, 1, 1) * d for q, k, d in zip(qs, ks, decay)]
    decs = [jnp.exp(gl) for gl in g_last]
    return us, ws, qds, kds, qks, decs


def _gdn_seq(us, ws, qds, kds, qks, decs, states):
    corr = [_dotb(w, st, 1, 0) for w, st in zip(ws, states)]
    from_state = [_dotb(qd, st, 1, 0) for qd, st in zip(qds, states)]
    v_new = [u - x for u, x in zip(us, corr)]
    intra = [_dotb(qk, vn, 1, 0) for qk, vn in zip(qks, v_new)]
    upd = [_dotb(kd, vn, 0, 0) for kd, vn in zip(kds, v_new)]
    outs = [a + b for a, b in zip(from_state, intra)]
    news = [st * d + x for st, d, x in zip(states, decs, upd)]
    return outs, news


def _heads(ref, rows=slice(None), base=0):
    return [ref[rows, (base + h) * DK:(base + h + 1) * DK] for h in range(HEADS)]


def _qk_heads(ref, rows=slice(None)):
    return [ref[rows, h * DK:h * DK + CHUNK] for h in range(HEADS)]


def _put_heads(ref, vals, rows=slice(None), base=0):
    for h in range(HEADS):
        ref[rows, (base + h) * DK:(base + h + 1) * DK] = vals[h]


def _put_qk(ref, vals, rows=slice(None)):
    for h in range(HEADS):
        ref[rows, h * DK:h * DK + CHUNK] = vals[h]
        ref[rows, h * DK + CHUNK:(h + 1) * DK] = jnp.zeros_like(vals[h])


def _group(n_chunks):
    return GDN_GROUP if n_chunks % GDN_GROUP == 0 else 1


def _decs(ref):
    return [ref[h:h + 1, 0:1] for h in range(HEADS)]


def gdn_intra_fwd(name, qkvn, p, alog, dtb):
    n_tok = qkvn.shape[0]
    n = n_tok // CHUNK
    grp = _group(n)
    hd = HEADS * DK
    rb = grp * CHUNK

    def body(q_ref, k_ref, v_ref, pba_ref, al_ref, dt_ref, u_ref, w_ref, qd_ref, kd_ref, qk_ref, dec_ref):
        rows = [slice(g * CHUNK, (g + 1) * CHUNK) for g in range(grp)]
        cat = lambda ref: [t for r in rows for t in _heads(ref, r)]
        us, ws, qds, kds, qks, decs = _gdn_intra(cat(q_ref), cat(k_ref), cat(v_ref), [pba_ref[r, :] for r in rows],
                                                 al_ref[...], dt_ref[...])
        for g, r in enumerate(rows):
            part = slice(g * HEADS, (g + 1) * HEADS)
            _put_heads(u_ref, us[part], r)
            _put_heads(w_ref, ws[part], r)
            _put_heads(qd_ref, qds[part], r)
            _put_heads(kd_ref, kds[part], r)
            _put_qk(qk_ref, qks[part], r)
            for h in range(HEADS):
                dec_ref[g * HEADS + h:g * HEADS + h + 1, :] = jnp.broadcast_to(decs[g * HEADS + h], (1, LANES))

    blk = lambda c: pl.BlockSpec((rb, hd), lambda i: (i, c))
    par = pl.BlockSpec((1, LANES), lambda i: (0, 0))
    return pl.pallas_call(
        body, name=name, grid=(n // grp,),
        in_specs=[blk(0), blk(1), blk(2), pl.BlockSpec((rb, LANES), lambda i: (i, COL_BA)), par, par],
        out_specs=[blk(0)] * 5 + [pl.BlockSpec((grp * HEADS, LANES), lambda i: (i, 0))],
        out_shape=[S((n_tok, hd), F32)] * 5 + [S((n * HEADS, LANES), F32)],
        compiler_params=_cp(("parallel",)))(qkvn, qkvn, qkvn, p, alog, dtb)


def gdn_seq_fwd(name, u, w, qd, kd, qk, dec):
    n_tok = u.shape[0]
    n = n_tok // CHUNK
    hd = HEADS * DK

    def body(u_ref, w_ref, qd_ref, kd_ref, qk_ref, dec_ref, o_ref, s_ref, st):
        @pl.when(pl.program_id(0) == 0)
        def _():
            st[...] = jnp.zeros_like(st)

        s_ref[...] = st[...]
        states = [st[h * DK:(h + 1) * DK, :] for h in range(HEADS)]
        outs, news = _gdn_seq(_heads(u_ref), _heads(w_ref), _heads(qd_ref), _heads(kd_ref), _qk_heads(qk_ref),
                              _decs(dec_ref), states)
        _put_heads(o_ref, outs)
        for h in range(HEADS):
            st[h * DK:(h + 1) * DK, :] = news[h]

    blk = pl.BlockSpec((CHUNK, hd), lambda i: (i, 0))
    return pl.pallas_call(
        body, name=name, grid=(n,),
        in_specs=[blk] * 5 + [pl.BlockSpec((HEADS, LANES), lambda i: (i, 0))],
        out_specs=[blk, pl.BlockSpec((None, hd, DK), lambda i: (i, 0, 0))],
        out_shape=[S((n_tok, hd), F32), S((n, hd, DK), F32)],
        scratch_shapes=[pltpu.VMEM((hd, DK), F32)],
        compiler_params=_cp(("arbitrary",)))(u, w, qd, kd, qk, dec)


def gdn_seq_bwd(name, u, w, qd, kd, qk, dec, states, do):
    n_tok = u.shape[0]
    n = n_tok // CHUNK
    hd = HEADS * DK

    def body(u_ref, w_ref, qd_ref, kd_ref, qk_ref, dec_ref, s_ref, do_ref,
             du_ref, dw_ref, dqd_ref, dkd_ref, dqk_ref, ddec_ref, dst):
        @pl.when(pl.program_id(0) == 0)
        def _():
            dst[...] = jnp.zeros_like(dst)

        states = [s_ref[h * DK:(h + 1) * DK, :] for h in range(HEADS)]
        _, vjp = jax.vjp(_gdn_seq, _heads(u_ref), _heads(w_ref), _heads(qd_ref), _heads(kd_ref), _qk_heads(qk_ref),
                         _decs(dec_ref), states)
        d_news = [dst[h * DK:(h + 1) * DK, :] for h in range(HEADS)]
        du, dw, dqd, dkd, dqk, ddec, dstates = vjp((_heads(do_ref), d_news))
        _put_heads(du_ref, du)
        _put_heads(dw_ref, dw)
        _put_heads(dqd_ref, dqd)
        _put_heads(dkd_ref, dkd)
        _put_qk(dqk_ref, dqk)
        for h in range(HEADS):
            ddec_ref[h:h + 1, :] = jnp.broadcast_to(ddec[h], (1, LANES))
            dst[h * DK:(h + 1) * DK, :] = dstates[h]

    blk = pl.BlockSpec((CHUNK, hd), lambda i: (n - 1 - i, 0))
    dspec = pl.BlockSpec((HEADS, LANES), lambda i: (n - 1 - i, 0))
    return pl.pallas_call(
        body, name=name, grid=(n,),
        in_specs=[blk] * 5 + [dspec, pl.BlockSpec((None, hd, DK), lambda i: (n - 1 - i, 0, 0)), blk],
        out_specs=[blk] * 5 + [dspec],
        out_shape=[S((n_tok, hd), F32)] * 5 + [S((n * HEADS, LANES), F32)],
        scratch_shapes=[pltpu.VMEM((hd, DK), F32)],
        compiler_params=_cp(("arbitrary",)))(u, w, qd, kd, qk, dec, states, do)


def gdn_intra_bwd(name, qkvn, p, alog, dtb, du, dw, dqd, dkd, dqk, ddec):
    n_tok = qkvn.shape[0]
    n = n_tok // CHUNK
    grp = _group(n)
    hd = HEADS * DK
    rb = grp * CHUNK

    def body(q_ref, k_ref, v_ref, pba_ref, al_ref, dt_ref, du_ref, dw_ref, dqd_ref, dkd_ref, dqk_ref, ddec_ref,
             dqkv_ref, dpba_ref, dal_ref, ddt_ref):
        @pl.when(pl.program_id(0) == 0)
        def _():
            dal_ref[...] = jnp.zeros_like(dal_ref)
            ddt_ref[...] = jnp.zeros_like(ddt_ref)

        rows = [slice(g * CHUNK, (g + 1) * CHUNK) for g in range(grp)]
        cat = lambda ref: [t for r in rows for t in _heads(ref, r)]
        _, vjp = jax.vjp(_gdn_intra, cat(q_ref), cat(k_ref), cat(v_ref), [pba_ref[r, :] for r in rows],
                         al_ref[...], dt_ref[...])
        cts = (cat(du_ref), cat(dw_ref), cat(dqd_ref), cat(dkd_ref), [t for r in rows for t in _qk_heads(dqk_ref, r)],
               [ddec_ref[i:i + 1, 0:1] for i in range(grp * HEADS)])
        dq, dk, dv, dpba, dal, ddt = vjp(cts)
        for g, r in enumerate(rows):
            part = slice(g * HEADS, (g + 1) * HEADS)
            _put_heads(dqkv_ref, dq[part], r, 0)
            _put_heads(dqkv_ref, dk[part], r, HEADS)
            _put_heads(dqkv_ref, dv[part], r, 2 * HEADS)
            dpba_ref[r, :] = dpba[g].astype(dpba_ref.dtype)
        dal_ref[...] += dal
        ddt_ref[...] += ddt

    blk = lambda c: pl.BlockSpec((rb, hd), lambda i: (i, c))
    par = pl.BlockSpec((1, LANES), lambda i: (0, 0))
    return pl.pallas_call(
        body, name=name, grid=(n // grp,),
        in_specs=[blk(0), blk(1), blk(2), pl.BlockSpec((rb, LANES), lambda i: (i, COL_BA)), par, par]
        + [blk(0)] * 5 + [pl.BlockSpec((grp * HEADS, LANES), lambda i: (i, 0))],
        out_specs=[pl.BlockSpec((rb, 3 * hd), lambda i: (i, 0)), pl.BlockSpec((rb, LANES), lambda i: (i, 0)), par, par],
        out_shape=[S((n_tok, 3 * hd), F32), S((n_tok, LANES), BF16), S((1, LANES), F32), S((1, LANES), F32)],
        compiler_params=_cp(("arbitrary",)))(qkvn, qkvn, qkvn, p, alog, dtb, du, dw, dqd, dkd, dqk, ddec)


def loss_head(name, y, target, tm):
    n_tok, d = y.shape

    def body(y_ref, t_ref, dy_ref, l_ref):
        @pl.when(pl.program_id(0) == 0)
        def _():
            l_ref[...] = jnp.zeros_like(l_ref)

        e = y_ref[...] - t_ref[...]
        dy_ref[...] = e * (1.0 / d)
        l_ref[...] += jnp.sum(e * e, keepdims=True) * (0.5 / d)

    spec = pl.BlockSpec((tm, d), lambda i: (i, 0))
    return pl.pallas_call(
        body, name=name, grid=(n_tok // tm,), in_specs=[spec, spec],
        out_specs=[spec, pl.BlockSpec((1, 1), lambda i: (0, 0))], out_shape=[S((n_tok, d), F32), S((1, 1), F32)],
        compiler_params=_cp(("arbitrary",)))(y, target)


def _tm(n_tok):
    return min(512, n_tok)


def ffn_fwd(tag, x, w):
    n_tok = x.shape[0]
    tm = _tm(n_tok)
    g1 = (1, n_tok // tm)
    tD = _tok(D_MODEL)(tm)
    (h,) = ew_fwd(tag + "_rms", fn_rms, g1, [(x, tD), (w["norm_pre"], _par(D_MODEL))], [(S((n_tok, D_MODEL), BF16), tD)])
    u = mm(tag + "_in", h, w["w_in"])
    tF = lambda c: _tok(D_FF, c)(tm)
    (a,) = ew_fwd(tag + "_swiglu", fn_swiglu, g1, [(u, tF(0)), (u, tF(1))], [(S((n_tok, D_FF), BF16), tF(0))])
    f = mm(tag + "_out", a, w["w_out"])
    fn_res = lambda col, x_, f_, w_: (x_ + 0.5 * _rms(f_, w_),)
    (xo,) = ew_fwd(tag + "_res", fn_res, g1, [(x, tD), (f, tD), (w["norm_post"], _par(D_MODEL))],
                   [(S((n_tok, D_MODEL), F32), tD)])
    return xo, dict(x=x, h=h, u=u, a=a, f=f)


def ffn_bwd(tag, dxo, sv, w):
    n_tok = dxo.shape[0]
    tm = _tm(n_tok)
    g1 = (1, n_tok // tm)
    tD = _tok(D_MODEL)(tm)
    pD = _par(D_MODEL)
    fn_post = lambda col, f_, w_: (0.5 * _rms(f_, w_),)
    df, d_post = ew_bwd(tag + "_res_b", fn_post, g1, [(sv["f"], tD), (w["norm_post"], pD)], [(dxo, tD)],
                        [(0, S((n_tok, D_MODEL), BF16), tD)], [(1, S((1, D_MODEL), F32), pD, False)])
    da = mm(tag + "_out_bx", df, w["w_out"], tb=True)
    d_wout = mm(tag + "_out_bw", sv["a"], df, ta=True)
    tF = lambda c: _tok(D_FF, c)(tm)
    du = swiglu_bwd(tag + "_swiglu_b", sv["u"], da, tm)
    dh = mm(tag + "_in_bx", du, w["w_in"], tb=True)
    d_win = mm(tag + "_in_bw", sv["h"], du, ta=True)
    dx, d_pre = ew_bwd(tag + "_rms_b", fn_rms, g1, [(sv["x"], tD), (w["norm_pre"], pD)], [(dh, tD)],
                       [(0, S((n_tok, D_MODEL), F32), tD)], [(1, S((1, D_MODEL), F32), pD, False)], add=(dxo, tD))
    return dx, dict(norm_pre=d_pre, norm_post=d_post, w_in=d_win, w_out=d_wout)


def mix_fwd(tag, x, w):
    n_tok = x.shape[0]
    tm = _tm(n_tok)
    nt = n_tok // tm
    g1 = (1, nt)
    tD = _tok(D_MODEL)(tm)
    pD = _par(D_MODEL)
    (h,) = ew_fwd(tag + "_rms", fn_rms, g1, [(x, tD), (w["norm_pre"], pD)], [(S((n_tok, D_MODEL), BF16), tD)])
    p = mm(tag + "_in", h, w["w_all"])
    c = conv_fwd(tag + "_gconv", p, 0, W_QKV, w["conv_w"], None, tm)
    tC = _tokcol()(tm)
    (qkvn,) = ew_fwd(tag + "_gpost", fn_gdnpost, (W_QKV // LANES, nt), [(c, tC)], [(S((n_tok, W_QKV), F32), tC)])
    intra = gdn_intra_fwd(tag + "_gintra", qkvn, p, w["alog"], w["dtb"])
    o, states = gdn_seq_fwd(tag + "_gseq", *intra)
    (on,) = ew_fwd(tag + "_gout", fn_gdnout, (HEADS, nt),
                   [(o, tC), (p, _tokcol(COL_Z)(tm)), (w["gdn_norm_w"], _par(LANES))],
                   [(S((n_tok, D_MODEL), BF16), tC)])
    ya = mm(tag + "_go", on, w["gdn_w_o"])
    (hglu,) = ew_fwd(tag + "_glu", fn_glu, (D_MODEL // LANES, nt),
                     [(p, _tokcol(COL_GLU)(tm)), (p, _tokcol(COL_GLU + D_MODEL // LANES)(tm)),
                      (w["pw1_b"], _parcol(0)), (w["pw1_b"], _parcol(D_MODEL // LANES))],
                     [(S((n_tok, D_MODEL), F32), tC)])
    hc = conv_fwd(tag + "_cconv", hglu, 0, D_MODEL, w["dw_w"], w["dw_b"], tm)
    (hs,) = ew_fwd(tag + "_ln", fn_lnsilu, g1, [(hc, tD), (w["ln_g"], pD), (w["ln_b"], pD)],
                   [(S((n_tok, D_MODEL), BF16), tD)])
    yb = mm(tag + "_co", hs, w["cnv_w_o"])
    tG = lambda cb: pl.BlockSpec((tm, D_MODEL), lambda j, i: (i, cb))
    gcol = (W_QKV + W_Z + W_GLU) // D_MODEL
    (ym,) = ew_fwd(tag + "_merge", fn_merge, g1, [(ya, tD), (yb, tD), (p, tG(gcol)), (p, tG(gcol + 1)), (w["b_o"], pD)],
                   [(S((n_tok, D_MODEL), BF16), tD)])
    y = mm(tag + "_wo", ym, w["w_out"])
    fn_res = lambda col, x_, f_, w_: (x_ + _rms(f_, w_),)
    (xo,) = ew_fwd(tag + "_res", fn_res, g1, [(x, tD), (y, tD), (w["norm_post"], pD)], [(S((n_tok, D_MODEL), F32), tD)])
    sv = dict(x=x, h=h, p=p, c=c, qkvn=qkvn, intra=intra, states=states, o=o, on=on, ya=ya, hglu=hglu, hc=hc, hs=hs, yb=yb, ym=ym, y=y)
    return xo, sv


def mix_bwd(tag, dxo, sv, w):
    n_tok = dxo.shape[0]
    tm = _tm(n_tok)
    nt = n_tok // tm
    g1 = (1, nt)
    tD = _tok(D_MODEL)(tm)
    pD = _par(D_MODEL)
    tC = _tokcol()(tm)
    p = sv["p"]
    sD = lambda dt: S((n_tok, D_MODEL), dt)
    fn_post = lambda col, f_, w_: (_rms(f_, w_),)
    dy, d_post = ew_bwd(tag + "_res_b", fn_post, g1, [(sv["y"], tD), (w["norm_post"], pD)], [(dxo, tD)],
                        [(0, sD(BF16), tD)], [(1, S((1, D_MODEL), F32), pD, False)])
    dym = mm(tag + "_wo_bx", dy, w["w_out"], tb=True)
    d_wout = mm(tag + "_wo_bw", sv["ym"], dy, ta=True)
    tG = lambda cb: pl.BlockSpec((tm, D_MODEL), lambda j, i: (i, cb))
    gcol = (W_QKV + W_Z + W_GLU) // D_MODEL
    dya, dyb, dga, dgb, d_bo = ew_bwd(
        tag + "_merge_b", fn_merge, g1, [(sv["ya"], tD), (sv["yb"], tD), (p, tG(gcol)), (p, tG(gcol + 1)), (w["b_o"], pD)],
        [(dym, tD)], [(0, sD(BF16), tD), (1, sD(BF16), tD), (2, sD(BF16), tD), (3, sD(BF16), tD)],
        [(4, S((1, D_MODEL), F32), pD, False)])
    dhs = mm(tag + "_co_bx", dyb, w["cnv_w_o"], tb=True)
    d_cwo = mm(tag + "_co_bw", sv["hs"], dyb, ta=True)
    dhc, d_lng, d_lnb = ew_bwd(tag + "_ln_b", fn_lnsilu, g1, [(sv["hc"], tD), (w["ln_g"], pD), (w["ln_b"], pD)], [(dhs, tD)],
                               [(0, sD(F32), tD)], [(1, S((1, D_MODEL), F32), pD, False), (2, S((1, D_MODEL), F32), pD, False)])
    dhglu, d_dww, d_dwb = conv_bwd(tag + "_cconv_b", sv["hglu"], 0, D_MODEL, w["dw_w"], dhc, F32, tm)
    nc = D_MODEL // LANES
    dpa, dpg, d_ba, d_bg = ew_bwd(
        tag + "_glu_b", fn_glu, (nc, nt),
        [(p, _tokcol(COL_GLU)(tm)), (p, _tokcol(COL_GLU + nc)(tm)), (w["pw1_b"], _parcol(0)), (w["pw1_b"], _parcol(nc))],
        [(dhglu, tC)], [(0, sD(BF16), tC), (1, sD(BF16), tC)],
        [(2, S((1, D_MODEL), F32), _parcol(0), False), (3, S((1, D_MODEL), F32), _parcol(0), False)])
    don = mm(tag + "_go_bx", dya, w["gdn_w_o"], tb=True)
    d_gwo = mm(tag + "_go_bw", sv["on"], dya, ta=True)
    do, dz, d_gnw = ew_bwd(tag + "_gout_b", fn_gdnout, (HEADS, nt),
                           [(sv["o"], tC), (p, _tokcol(COL_Z)(tm)), (w["gdn_norm_w"], _par(LANES))], [(don, tC)],
                           [(0, sD(F32), tC), (1, sD(BF16), tC)], [(2, S((1, LANES), F32), _par(LANES), True)])
    d_intra = gdn_seq_bwd(tag + "_gseq_b", *sv["intra"], sv["states"], do)
    dqkvn, dpba, d_alog, d_dtb = gdn_intra_bwd(tag + "_gintra_b", sv["qkvn"], p, w["alog"], w["dtb"], *d_intra)
    (dc,) = ew_bwd(tag + "_gpost_b", fn_gdnpost, (W_QKV // LANES, nt), [(sv["c"], tC)], [(dqkvn, tC)],
                   [(0, S((n_tok, W_QKV), F32), tC)], [])
    dqkv, d_convw, _ = conv_bwd(tag + "_gconv_b", p, 0, W_QKV, w["conv_w"], dc, BF16, tm)
    nd = D_MODEL // LANES
    pieces = [(dqkv, 0, 0), (dz, W_QKV // D_MODEL, COL_Z), (dpa, COL_GLU // nd, COL_GLU), (dpg, COL_GLU // nd + 1, COL_GLU + nd),
              (dga, COL_GATE // nd, COL_GATE), (dgb, COL_GATE // nd + 1, COL_GATE + nd), (dpba, COL_BA, COL_BA)]
    dh = mm_nt_sum(tag + "_in_bx", [(a, blk) for a, blk, _ in pieces], w["w_all"])
    d_wall = [mm(tag + "_in_bw", sv["h"], a, ta=True) for a, _, _ in pieces]
    dx, d_pre = ew_bwd(tag + "_rms_b", fn_rms, g1, [(sv["x"], tD), (w["norm_pre"], pD)], [(dh, tD)],
                       [(0, sD(F32), tD)], [(1, S((1, D_MODEL), F32), pD, False)], add=(dxo, tD))
    grads = dict(norm_pre=d_pre, norm_post=d_post, w_all=d_wall, conv_w=d_convw, alog=d_alog, dtb=d_dtb,
                 gdn_norm_w=d_gnw, gdn_w_o=d_gwo, pw1_b=jnp.concatenate([d_ba, d_bg], axis=1), dw_w=d_dww,
                 dw_b=d_dwb, ln_g=d_lng, ln_b=d_lnb, cnv_w_o=d_cwo, b_o=d_bo, w_out=d_wout)
    return dx, grads


def local_step(x, target, layers):
    saved = []
    for i, lw in enumerate(layers):
        x, s1 = ffn_fwd("ffn", x, lw["ffn1"])
        x, s2 = mix_fwd("mix", x, lw["mix"])
        x, s3 = ffn_fwd("ffn", x, lw["ffn2"])
        saved.append((s1, s2, s3))
    dx, loss = loss_head("loss", x, target, _tm(x.shape[0]))
    grads = [None] * len(layers)
    for i in reversed(range(len(layers))):
        lw = layers[i]
        s1, s2, s3 = saved[i]
        dx, g3 = ffn_bwd("ffn", dx, s3, lw["ffn2"])
        dx, g2 = mix_bwd("mix", dx, s2, lw["mix"])
        dx, g1 = ffn_bwd("ffn", dx, s1, lw["ffn1"])
        grads[i] = dict(ffn1=g1, mix=g2, ffn2=g3)
    return loss, dx, grads


_O_BA = W_QKV + W_Z
_O_GLU = _O_BA + 2 * HEADS


def prep_layer(wl):
    row = lambda v: v.reshape(1, -1).astype(F32)
    bf = lambda v: v.astype(BF16)
    lanes8 = lambda v: jnp.zeros((1, LANES), F32).at[0, HEADS:2 * HEADS].set(v.astype(F32))
    mw = bf(wl["mix_w_in"])
    w_all = jnp.concatenate([mw[:, :_O_BA], mw[:, _O_GLU:], mw[:, _O_BA:_O_GLU],
                             jnp.zeros((D_MODEL, LANES - 2 * HEADS), BF16)], axis=1)
    ffn = lambda k: dict(norm_pre=row(wl[k + "_norm_pre"]), norm_post=row(wl[k + "_norm_post"]),
                         w_in=bf(wl[k + "_w_in"]), w_out=bf(wl[k + "_w_out"]))
    mix = dict(norm_pre=row(wl["mix_norm_pre"]), norm_post=row(wl["mix_norm_post"]), w_all=w_all,
               conv_w=wl["gdn_conv_w"].astype(F32), alog=lanes8(wl["gdn_a_log"]), dtb=lanes8(wl["gdn_dt_bias"]),
               gdn_norm_w=row(wl["gdn_norm_w"]), gdn_w_o=bf(wl["gdn_w_o"]), pw1_b=row(wl["cnv_pw1_b"]),
               dw_w=wl["cnv_dw_w"].astype(F32), dw_b=row(wl["cnv_dw_b"]), ln_g=row(wl["cnv_ln_g"]),
               ln_b=row(wl["cnv_ln_b"]), cnv_w_o=bf(wl["cnv_w_o"]), b_o=row(wl["cnv_b_o"]), w_out=bf(wl["mix_w_out"]))
    return dict(ffn1=ffn("ffn1"), mix=mix, ffn2=ffn("ffn2"))


def unprep_grads(g):
    m = g["mix"]
    dqkv, dz, dpa, dpg, dga, dgb, dba = m["w_all"]
    out = {}
    for k in ("ffn1", "ffn2"):
        out[k + "_norm_pre"] = g[k]["norm_pre"][0]
        out[k + "_norm_post"] = g[k]["norm_post"][0]
        out[k + "_w_in"] = g[k]["w_in"]
        out[k + "_w_out"] = g[k]["w_out"]
    out.update(
        mix_norm_pre=m["norm_pre"][0], mix_norm_post=m["norm_post"][0],
        mix_w_in=jnp.concatenate([dqkv, dz, dba[:, :2 * HEADS], dpa, dpg, dga, dgb], axis=1),
        gdn_conv_w=m["conv_w"], gdn_a_log=m["alog"][0, HEADS:2 * HEADS], gdn_dt_bias=m["dtb"][0, HEADS:2 * HEADS],
        gdn_norm_w=m["gdn_norm_w"][0], gdn_w_o=m["gdn_w_o"], cnv_pw1_b=m["pw1_b"][0], cnv_dw_w=m["dw_w"],
        cnv_dw_b=m["dw_b"][0], cnv_ln_g=m["ln_g"][0], cnv_ln_b=m["ln_b"][0], cnv_w_o=m["cnv_w_o"], cnv_b_o=m["b_o"][0],
        mix_w_out=m["w_out"])
    return out


MESH = pl.DeviceIdType.MESH
ANY = pl.BlockSpec(memory_space=pl.ANY)
N_DEV = 8


def _pos():
    return lax.axis_index("x"), lax.axis_index("y"), lax.axis_index("c")


def _other_chips(x, y):
    return [(1 - x, y), (x, 1 - y), (1 - x, 1 - y)]


def gather_weights(ws):
    n = len(ws)

    def body(*refs):
        w_refs, o_refs = refs[:n], refs[n:2 * n]
        send_sems, recv_sems, loc_sems = refs[2 * n:]
        x, y, c = _pos()
        b = 2 * x + y
        chips = _other_chips(x, y)

        def ici(a, j, blk):
            return pltpu.make_async_remote_copy(
                src_ref=w_refs[a].at[c], dst_ref=o_refs[a].at[blk, c], send_sem=send_sems.at[4 * a + j],
                recv_sem=recv_sems.at[4 * a + j], device_id=(chips[j][0], chips[j][1], c), device_id_type=MESH)

        def d2d(a, layer):
            part = o_refs[a].at[:, pl.ds(layer, 1)]
            return pltpu.make_async_remote_copy(
                src_ref=part, dst_ref=part, send_sem=send_sems.at[4 * a + 3], recv_sem=recv_sems.at[4 * a + 3],
                device_id=(x, y, 1 - c), device_id_type=MESH)

        locs = [pltpu.make_async_copy(w_refs[a].at[c], o_refs[a].at[b, c], loc_sems.at[a]) for a in range(n)]
        for cp in locs:
            cp.start()
        sends = [ici(a, j, b) for a in range(n) for j in range(3)]
        for cp in sends:
            cp.start()
        for a in range(n):
            for j in range(3):
                ici(a, j, 2 * chips[j][0] + chips[j][1]).wait_recv()
        for cp in locs:
            cp.wait()
        passed = [d2d(a, c) for a in range(n)]
        for cp in passed:
            cp.start()
        for a in range(n):
            d2d(a, 1 - c).wait_recv()
        for cp in sends + passed:
            cp.wait_send()

    return pl.pallas_call(
        body, name="gather_weights", in_specs=[ANY] * n, out_specs=[ANY] * n,
        out_shape=[S((N_BLK,) + w.shape, w.dtype) for w in ws],
        scratch_shapes=[pltpu.SemaphoreType.DMA((4 * n,)), pltpu.SemaphoreType.DMA((4 * n,)),
                        pltpu.SemaphoreType.DMA((n,))])(*ws)


def rs_sibling(gs):
    n = len(gs)

    def body(*refs):
        g_refs, r_refs = refs[:n], refs[n:2 * n]
        send_sems, recv_sems = refs[2 * n:]
        x, y, c = _pos()

        def cp(k):
            hr = gs[k].shape[1] // 2
            return pltpu.make_async_remote_copy(
                src_ref=g_refs[k].at[:, pl.ds((1 - c) * hr, hr)], dst_ref=r_refs[k], send_sem=send_sems.at[k],
                recv_sem=recv_sems.at[k], device_id=(x, y, 1 - c), device_id_type=MESH)

        cps = [cp(k) for k in range(n)]
        for d in cps:
            d.start()
        for d in cps:
            d.wait_recv()
        for d in cps:
            d.wait_send()

    return pl.pallas_call(
        body, name="rs_sibling", in_specs=[ANY] * n, out_specs=[ANY] * n,
        out_shape=[S((N_BLK, g.shape[1] // 2, g.shape[2]), g.dtype) for g in gs],
        scratch_shapes=[pltpu.SemaphoreType.DMA((n,)), pltpu.SemaphoreType.DMA((n,))])(*gs)


def rs_chips(ps):
    n = len(ps)

    def body(*refs):
        p_refs, r_refs = refs[:n], refs[n:2 * n]
        send_sems, recv_sems, loc_sems = refs[2 * n:]
        x, y, c = _pos()
        b = 2 * x + y
        chips = _other_chips(x, y)

        def ici(k, j, src_blk, dst_slot):
            return pltpu.make_async_remote_copy(
                src_ref=p_refs[k].at[src_blk], dst_ref=r_refs[k].at[dst_slot], send_sem=send_sems.at[3 * k + j],
                recv_sem=recv_sems.at[3 * k + j], device_id=(chips[j][0], chips[j][1], c), device_id_type=MESH)

        locs = [pltpu.make_async_copy(p_refs[k].at[b], r_refs[k].at[b], loc_sems.at[k]) for k in range(n)]
        for d in locs:
            d.start()
        sends = [ici(k, j, 2 * chips[j][0] + chips[j][1], b) for k in range(n) for j in range(3)]
        for d in sends:
            d.start()
        for k in range(n):
            for j in range(3):
                ici(k, j, b, 2 * chips[j][0] + chips[j][1]).wait_recv()
        for d in locs:
            d.wait()
        for d in sends:
            d.wait_send()

    return pl.pallas_call(
        body, name="rs_chips", in_specs=[ANY] * n, out_specs=[ANY] * n,
        out_shape=[S(p.shape, p.dtype) for p in ps],
        scratch_shapes=[pltpu.SemaphoreType.DMA((3 * n,)), pltpu.SemaphoreType.DMA((3 * n,)),
                        pltpu.SemaphoreType.DMA((n,))])(*ps)


def ag_sibling(fs):
    n = len(fs)

    def body(*refs):
        o_refs = refs[n:2 * n]
        send_sems, recv_sems = refs[2 * n:]
        x, y, c = _pos()

        def cp(k, half):
            hr = fs[k].shape[0] // 2
            rows = o_refs[k].at[pl.ds(half * hr, hr)]
            return pltpu.make_async_remote_copy(
                src_ref=rows, dst_ref=rows, send_sem=send_sems.at[k], recv_sem=recv_sems.at[k],
                device_id=(x, y, 1 - c), device_id_type=MESH)

        cps = [cp(k, c) for k in range(n)]
        for d in cps:
            d.start()
        for k in range(n):
            cp(k, 1 - c).wait_recv()
        for d in cps:
            d.wait_send()

    return pl.pallas_call(
        body, name="ag_sibling", in_specs=[ANY] * n, out_specs=[ANY] * n,
        out_shape=[S(f.shape, f.dtype) for f in fs], input_output_aliases={k: k for k in range(n)},
        scratch_shapes=[pltpu.SemaphoreType.DMA((n,)), pltpu.SemaphoreType.DMA((n,))])(*fs)


def allreduce_small(v):
    rows = v.shape[0]

    def body(v_ref, o_ref, buf, send_sems, recv_sems):
        x, y, c = _pos()
        me = 4 * x + 2 * y + c
        buf[me] = v_ref[...]

        def cp(d, slot):
            dx, dy, dc = (d >> 2) & 1, (d >> 1) & 1, d & 1
            peer = (1 - x if dx else x, 1 - y if dy else y, 1 - c if dc else c)
            return pltpu.make_async_remote_copy(
                src_ref=v_ref, dst_ref=buf.at[slot], send_sem=send_sems.at[d - 1], recv_sem=recv_sems.at[d - 1],
                device_id=peer, device_id_type=MESH)

        cps = [cp(d, me) for d in range(1, N_DEV)]
        for d in cps:
            d.start()
        for d in range(1, N_DEV):
            dx, dy, dc = (d >> 2) & 1, (d >> 1) & 1, d & 1
            src = 4 * (1 - x if dx else x) + 2 * (1 - y if dy else y) + (1 - c if dc else c)
            cp(d, src).wait_recv()
        for d in cps:
            d.wait_send()
        acc = buf[0]
        for s in range(1, N_DEV):
            acc = acc + buf[s]
        o_ref[...] = acc

    vm = pl.BlockSpec(memory_space=pltpu.VMEM)
    return pl.pallas_call(
        body, name="allreduce_small", in_specs=[vm], out_specs=vm, out_shape=S(v.shape, v.dtype),
        scratch_shapes=[pltpu.VMEM((N_DEV, rows, LANES), F32), pltpu.SemaphoreType.DMA((N_DEV - 1,)),
                        pltpu.SemaphoreType.DMA((N_DEV - 1,))])(v)


def _rows_tile(rows, cols, cap_bytes=1 << 20, mult=8):
    best = None
    for t in range(mult, rows + 1, mult):
        if rows % t == 0 and t * cols * 4 <= cap_bytes:
            best = t
    return best if best is not None else rows


def add_half(name, g, r, c_arr):
    _, hr, cols = r.shape
    tr = _rows_tile(hr, cols, mult=16)
    nb = hr // tr

    def body(c_ref, g_ref, r_ref, o_ref):
        o_ref[...] = (g_ref[...] + r_ref[...]).astype(o_ref.dtype)

    gs = pltpu.PrefetchScalarGridSpec(
        num_scalar_prefetch=1, grid=(N_BLK, nb),
        in_specs=[pl.BlockSpec((None, tr, cols), lambda b, i, cr: (b, cr[0] * nb + i, 0)),
                  pl.BlockSpec((None, tr, cols), lambda b, i, cr: (b, i, 0))],
        out_specs=pl.BlockSpec((None, tr, cols), lambda b, i, cr: (b, i, 0)))
    return pl.pallas_call(body, name=name, grid_spec=gs, out_shape=S(r.shape, BF16),
                          compiler_params=_cp(("parallel", "parallel")))(c_arr, g, r)


def sum_chips(name, r, c_arr):
    _, hr, cols = r.shape
    tr = _rows_tile(hr, cols, mult=16)
    nb = hr // tr

    def body(c_ref, r_ref, o_ref):
        acc = r_ref[0].astype(F32)
        for s in range(1, N_BLK):
            acc = acc + r_ref[s].astype(F32)
        o_ref[...] = acc

    gs = pltpu.PrefetchScalarGridSpec(
        num_scalar_prefetch=1, grid=(nb,),
        in_specs=[pl.BlockSpec((N_BLK, tr, cols), lambda i, cr: (0, i, 0))],
        out_specs=pl.BlockSpec((tr, cols), lambda i, cr: (cr[0] * nb + i, 0)))
    return pl.pallas_call(body, name=name, grid_spec=gs, out_shape=S((2 * hr, cols), F32),
                          compiler_params=_cp(("parallel",)))(c_arr, r)


def adamw(name, w, m, v, gs):
    rows, cols = w.shape
    two = len(gs) == 2
    span = rows // 2 if two else rows
    tr = _rows_tile(span, cols, 1 << 19)
    nb = span // tr

    def body(w_ref, m_ref, v_ref, *rest):
        g_refs, (go_ref, d_ref, mo_ref, vo_ref) = rest[:len(gs)], rest[len(gs):]
        if two:
            g = jnp.where(pl.program_id(0) < nb, g_refs[0][...], g_refs[1][...])
        else:
            g = g_refs[0][...]
        mn = ADAM_B1 * m_ref[...] + (1.0 - ADAM_B1) * g
        vn = ADAM_B2 * v_ref[...] + (1.0 - ADAM_B2) * jnp.square(g)
        m_hat = mn / (1.0 - ADAM_B1 ** ADAM_STEP)
        v_hat = vn / (1.0 - ADAM_B2 ** ADAM_STEP)
        go_ref[...] = g
        d_ref[...] = -ADAM_LR * (m_hat / (jnp.sqrt(v_hat) + ADAM_EPS) + ADAM_WD * w_ref[...])
        mo_ref[...] = mn
        vo_ref[...] = vn

    full = pl.BlockSpec((tr, cols), lambda i: (i, 0))
    if two:
        g_specs = [pl.BlockSpec((tr, cols), lambda i: (jnp.minimum(i, nb - 1), 0)),
                   pl.BlockSpec((tr, cols), lambda i: (jnp.maximum(i - nb, 0), 0))]
    else:
        g_specs = [full]
    return pl.pallas_call(
        body, name=name, grid=(2 * nb if two else nb,), in_specs=[full, full, full] + g_specs, out_specs=[full] * 4,
        out_shape=[S((rows, cols), F32)] * 4, compiler_params=_cp(("parallel",)))(w, m, v, *gs)


WEIGHTS = ["ffn1_norm_pre", "ffn1_norm_post", "ffn1_w_in", "ffn1_w_out", "mix_norm_pre", "mix_norm_post", "mix_w_in",
           "gdn_conv_w", "gdn_a_log", "gdn_dt_bias", "gdn_norm_w", "gdn_w_o", "cnv_pw1_b", "cnv_dw_w", "cnv_dw_b",
           "cnv_ln_g", "cnv_ln_b", "cnv_w_o", "cnv_b_o", "mix_w_out", "ffn2_norm_pre", "ffn2_norm_post", "ffn2_w_in",
           "ffn2_w_out"]
BIG = {"ffn1_w_in": True, "ffn1_w_out": False, "mix_w_in": True, "gdn_conv_w": True, "gdn_w_o": False,
       "cnv_dw_w": True, "cnv_w_o": False, "mix_w_out": False, "ffn2_w_in": True, "ffn2_w_out": False}
TINY = {"gdn_conv_w": (32, LANES), "cnv_dw_w": (64, LANES)}
SMALL = [n for n in WEIGHTS if n not in BIG]


def _whole(name, blocks):
    if BIG[name]:
        return jnp.transpose(blocks, (1, 0, 2)).reshape(blocks.shape[1], N_BLK * blocks.shape[2])
    return blocks.reshape(N_BLK * blocks.shape[1], blocks.shape[2])


def _blocks(name, whole):
    r, cfull = whole.shape
    if BIG[name]:
        blk = jnp.transpose(whole.reshape(r, N_BLK, cfull // N_BLK), (1, 0, 2))
    else:
        blk = whole.reshape(N_BLK, r // N_BLK, cfull)
    if name in TINY:
        tr, tc = TINY[name]
        flat = blk.reshape(N_BLK, -1)
        blk = jnp.pad(flat, ((0, 0), (0, tr * tc - flat.shape[1]))).reshape(N_BLK, tr, tc)
    return blk


def _pack(parts):
    rows = []
    for p in parts:
        flat = p.reshape(-1).astype(F32)
        rows.append(jnp.pad(flat, (0, (-flat.shape[0]) % LANES)).reshape(-1, LANES))
    out = jnp.concatenate(rows, axis=0)
    return jnp.pad(out, ((0, (-out.shape[0]) % 8), (0, 0)))


def _unpack(packed, shapes):
    out, r = [], 0
    for shp in shapes:
        size = math.prod(shp)
        nr = -(-size // LANES)
        out.append(packed[r:r + nr].reshape(-1)[:size].reshape(shp))
        r += nr
    return out


def kernel(x, ffn1_norm_pre, ffn1_norm_post, ffn1_w_in, ffn1_w_out, mix_norm_pre, mix_norm_post, mix_w_in, gdn_conv_w, gdn_a_log, gdn_dt_bias, gdn_norm_w, gdn_w_o, cnv_pw1_b, cnv_dw_w, cnv_dw_b, cnv_ln_g, cnv_ln_b, cnv_w_o, cnv_b_o, mix_w_out, ffn2_norm_pre, ffn2_norm_post, ffn2_w_in, ffn2_w_out, loss_target, m_ffn1_norm_pre, m_ffn1_norm_post, m_ffn1_w_in, m_ffn1_w_out, m_mix_norm_pre, m_mix_norm_post, m_mix_w_in, m_gdn_conv_w, m_gdn_a_log, m_gdn_dt_bias, m_gdn_norm_w, m_gdn_w_o, m_cnv_pw1_b, m_cnv_dw_w, m_cnv_dw_b, m_cnv_ln_g, m_cnv_ln_b, m_cnv_w_o, m_cnv_b_o, m_mix_w_out, m_ffn2_norm_pre, m_ffn2_norm_post, m_ffn2_w_in, m_ffn2_w_out, v_ffn1_norm_pre, v_ffn1_norm_post, v_ffn1_w_in, v_ffn1_w_out, v_mix_norm_pre, v_mix_norm_post, v_mix_w_in, v_gdn_conv_w, v_gdn_a_log, v_gdn_dt_bias, v_gdn_norm_w, v_gdn_w_o, v_cnv_pw1_b, v_cnv_dw_w, v_cnv_dw_b, v_cnv_ln_g, v_cnv_ln_b, v_cnv_w_o, v_cnv_b_o, v_mix_w_out, v_ffn2_norm_pre, v_ffn2_norm_post, v_ffn2_w_in, v_ffn2_w_out):
    args = locals()
    wts = {n: args[n] for n in WEIGHTS}
    mom = {n: args["m_" + n] for n in WEIGHTS}
    var = {n: args["v_" + n] for n in WEIGHTS}
    big = list(BIG)

    gathered = dict(zip(big, gather_weights([wts[n].astype(BF16) for n in big])))
    layers = []
    for l in range(DEPTH):
        wl = {n: _whole(n, gathered[n][:, l]) for n in big}
        wl.update({n: wts[n][l] for n in SMALL})
        layers.append(prep_layer(wl))

    loss, dx, grads = local_step(x[0], loss_target[0], layers)
    gw = [unprep_grads(g) for g in grads]

    small_shapes = [wts[n].shape for n in SMALL]
    packed = _pack([jnp.stack([gw[l][n] for l in range(DEPTH)]) for n in SMALL] + [loss])
    total = allreduce_small(packed)
    small_g = dict(zip(SMALL, _unpack(total, small_shapes)))
    loss_sum = total[sum(-(-math.prod(s) // LANES) for s in small_shapes), 0]

    c_arr = lax.axis_index("c").astype(jnp.int32).reshape(1)
    keys = [(n, l) for n in big for l in range(DEPTH)]
    blocks = [_blocks(n, gw[l][n]) for n, l in keys]
    from_sib = rs_sibling(blocks)
    partial = [add_half("add_half", g, r, c_arr) for g, r in zip(blocks, from_sib)]
    by_chip = rs_chips(partial)
    halves = [sum_chips("sum_chips", r, c_arr) for r in by_chip]
    summed = dict(zip(keys, ag_sibling(halves)))

    out_g, out_d, out_m, out_v = {}, {}, {}, {}
    for n in big:
        shp = wts[n].shape
        gs = [summed[(n, l)] for l in range(DEPTH)]
        if n in TINY:
            gs = [jnp.concatenate([g.reshape(-1)[:shp[1] * shp[2]].reshape(shp[1], shp[2]) for g in gs], axis=0)]
        two_d = lambda a: a.reshape(DEPTH * shp[1], shp[2])
        res = adamw("adamw", two_d(wts[n]), two_d(mom[n]), two_d(var[n]), gs)
        out_g[n], out_d[n], out_m[n], out_v[n] = [r.reshape(shp) for r in res]

    pk = lambda d: _pack([d[n] for n in SMALL])
    res = adamw("adamw_small", pk(wts), pk(mom), pk(var), [pk(small_g)])
    for d, r in zip((out_g, out_d, out_m, out_v), res):
        d.update(dict(zip(SMALL, _unpack(r, small_shapes))))

    return (loss_sum, dx[None], *[out_g[n] for n in WEIGHTS], *[out_d[n] for n in WEIGHTS],
            *[out_m[n] for n in WEIGHTS], *[out_v[n] for n in WEIGHTS])
```

```python
import functools
import math

import jax
import jax.numpy as jnp
from jax import lax
from jax.experimental import pallas as pl
from jax.experimental.pallas import tpu as pltpu

F32, BF16 = jnp.float32, jnp.bfloat16
S = jax.ShapeDtypeStruct

D_MODEL = 1024
D_FF = 2816
HEADS = 8
DK = 128
CHUNK = 64
GDN_CONV = 4
CNV_K = 31
W_QKV = 3 * HEADS * DK
W_Z = HEADS * DK
W_GLU = 2 * D_MODEL
W_GATE = 2 * D_MODEL
P_IN = W_QKV + W_Z + 2 * HEADS + W_GLU + W_GATE
LANES = 128
P_ALL = W_QKV + W_Z + W_GLU + W_GATE + LANES
COL_Z = W_QKV // LANES
COL_GLU = (W_QKV + W_Z) // LANES
COL_GATE = (W_QKV + W_Z + W_GLU) // LANES
COL_BA = (W_QKV + W_Z + W_GLU + W_GATE) // LANES
RMS_EPS = 1e-6
LN_EPS = 1e-5
DEPTH = 2
N_BLK = 4
VMEM_LIMIT = 56 * 1024 * 1024

ADAM_LR, ADAM_B1, ADAM_B2, ADAM_EPS, ADAM_WD, ADAM_STEP = 0.001, 0.9, 0.999, 1e-08, 0.01, 10


def _cp(sem):
    return pltpu.CompilerParams(dimension_semantics=sem, vmem_limit_bytes=VMEM_LIMIT)


MM_VMEM_BUDGET = 36 * 1024 * 1024


def _mm_tiles(m, n, k_bytes_a, k_bytes_b, out_bytes):
    best = None
    for tm in (1024, 512, 256, 128):
        if m % tm:
            continue
        for tn in (1024, 512, 640, 256, 384, 128):
            if n % tn:
                continue
            need = 2 * (tm * k_bytes_a + tn * k_bytes_b + tm * tn * out_bytes)
            if need <= MM_VMEM_BUDGET and (best is None or tm * tn > best[0] * best[1]):
                best = (tm, tn)
    if best is None:
        raise ValueError((m, n, k_bytes_a, k_bytes_b))
    return best


def mm_nt_sum(name, parts, b):
    m, n = parts[0][0].shape[0], b.shape[0]
    k_total = sum(a.shape[1] for a, _ in parts)
    tm, tn = _mm_tiles(m, n, k_total * 2, k_total * 2, 4)
    n_p = len(parts)

    def body(*refs):
        o_ref = refs[2 * n_p]
        acc = None
        for a_ref, b_ref in zip(refs[:n_p], refs[n_p:2 * n_p]):
            t = lax.dot_general(a_ref[...], b_ref[...], (((1,), (1,)), ((), ())), preferred_element_type=F32)
            acc = t if acc is None else acc + t
        o_ref[...] = acc

    a_specs = [pl.BlockSpec((tm, a.shape[1]), lambda i, j: (i, 0)) for a, _ in parts]
    b_specs = [pl.BlockSpec((tn, a.shape[1]), functools.partial(lambda i, j, c: (j, c), c=col)) for a, col in parts]
    return pl.pallas_call(
        body, name=name, grid=(m // tm, n // tn), in_specs=a_specs + b_specs,
        out_specs=pl.BlockSpec((tm, tn), lambda i, j: (i, j)), out_shape=S((m, n), F32),
        compiler_params=_cp(("parallel", "parallel")))(*[a for a, _ in parts], *([b] * n_p))


def mm(name, a, b, ta=False, tb=False, out_dtype=F32):
    k = a.shape[0] if ta else a.shape[1]
    m = a.shape[1] if ta else a.shape[0]
    n = b.shape[0] if tb else b.shape[1]
    assert k == (b.shape[1] if tb else b.shape[0]), (name, a.shape, b.shape)
    tm, tn = _mm_tiles(m, n, k * a.dtype.itemsize, k * b.dtype.itemsize, jnp.dtype(out_dtype).itemsize)
    a_spec = pl.BlockSpec((k, tm), lambda i, j: (0, i)) if ta else pl.BlockSpec((tm, k), lambda i, j: (i, 0))
    b_spec = pl.BlockSpec((tn, k), lambda i, j: (j, 0)) if tb else pl.BlockSpec((k, tn), lambda i, j: (0, j))
    dims = (((0 if ta else 1,), (1 if tb else 0,)), ((), ()))

    def body(a_ref, b_ref, o_ref):
        o_ref[...] = lax.dot_general(a_ref[...], b_ref[...], dims, preferred_element_type=F32).astype(o_ref.dtype)

    return pl.pallas_call(
        body, name=name, grid=(m // tm, n // tn), in_specs=[a_spec, b_spec],
        out_specs=pl.BlockSpec((tm, tn), lambda i, j: (i, j)), out_shape=S((m, n), out_dtype),
        compiler_params=_cp(("parallel", "parallel")))(a, b)


def ew_fwd(name, fn, grid, ins, outs):
    n_in = len(ins)

    def body(*refs):
        vals = [r[...].astype(F32) for r in refs[:n_in]]
        res = fn(pl.program_id(0), *vals)
        for r, v in zip(refs[n_in:], res):
            r[...] = v.astype(r.dtype)

    out = pl.pallas_call(
        body, name=name, grid=grid, in_specs=[s for _, s in ins], out_specs=[s for _, s in outs],
        out_shape=[sd for sd, _ in outs], compiler_params=_cp(("parallel", "parallel")))(*[a for a, _ in ins])
    return out


def ew_bwd(name, fn, grid, ins, cts, wrt, acc, add=None):
    n_in, n_ct, n_wrt, n_acc = len(ins), len(cts), len(wrt), len(acc)
    has_add = add is not None

    def body(*refs):
        in_refs = refs[:n_in]
        ct_refs = refs[n_in:n_in + n_ct]
        pos = n_in + n_ct
        add_ref = refs[pos] if has_add else None
        pos += 1 if has_add else 0
        wrt_refs = refs[pos:pos + n_wrt]
        acc_refs = refs[pos + n_wrt:pos + n_wrt + n_acc]
        col, tok = pl.program_id(0), pl.program_id(1)
        vals = [r[...].astype(F32) for r in in_refs]
        _, vjp = jax.vjp(lambda *a: fn(col, *a), *vals)
        grads = vjp(tuple(c[...].astype(F32) for c in ct_refs))
        for pos_w, ((idx, _, _), r) in enumerate(zip(wrt, wrt_refs)):
            g = grads[idx]
            if has_add and pos_w == 0:
                g = g + add_ref[...]
            r[...] = g.astype(r.dtype)
        for (idx, _, _, over_cols), r in zip(acc, acc_refs):
            first = (tok == 0) & (col == 0) if over_cols else tok == 0

            @pl.when(first)
            def _():
                r[...] = jnp.zeros_like(r)

            r[...] += grads[idx]

    arrays = [a for a, _ in ins] + [a for a, _ in cts] + ([add[0]] if has_add else [])
    in_specs = [s for _, s in ins] + [s for _, s in cts] + ([add[1]] if has_add else [])
    over_any = any(o for *_, o in acc)
    out = pl.pallas_call(
        body, name=name, grid=grid, in_specs=in_specs,
        out_specs=[s for _, _, s in wrt] + [s for _, _, s, _ in acc],
        out_shape=[sd for _, sd, _ in wrt] + [sd for _, sd, _, _ in acc],
        compiler_params=_cp(("arbitrary" if over_any else "parallel", "arbitrary")))(*arrays)
    return out


def _tok(width, col=0):
    return lambda tm: pl.BlockSpec((tm, width), lambda j, i: (i, col))


def _tokcol(off=0):
    return lambda tm: pl.BlockSpec((tm, LANES), lambda j, i: (i, off + j))


def _par(width, col=0):
    return pl.BlockSpec((1, width), lambda j, i: (0, col))


def _parcol(off=0):
    return pl.BlockSpec((1, LANES), lambda j, i: (0, off + j))


def _rms(x, w, eps=RMS_EPS):
    return x * lax.rsqrt(jnp.mean(x * x, axis=-1, keepdims=True) + eps) * w


def _silu(x):
    return x * jax.nn.sigmoid(x)


def fn_rms(col, x, w):
    return (_rms(x, w),)


def fn_swiglu(col, gate, up):
    return (_silu(gate) * up,)


def swiglu_bwd(name, u, da, tm):
    n_tok, f2 = u.shape
    f = f2 // 2

    def body(g_ref, up_ref, da_ref, o_ref):
        g, d = g_ref[...].astype(F32), da_ref[...].astype(F32)
        s = jax.nn.sigmoid(g)
        o_ref[:, :f] = (d * up_ref[...].astype(F32) * (s * (1.0 + g * (1.0 - s)))).astype(o_ref.dtype)
        o_ref[:, f:] = (d * (g * s)).astype(o_ref.dtype)

    half = lambda c: pl.BlockSpec((tm, f), lambda i: (i, c))
    return pl.pallas_call(
        body, name=name, grid=(n_tok // tm,), in_specs=[half(0), half(1), half(0)],
        out_specs=pl.BlockSpec((tm, f2), lambda i: (i, 0)), out_shape=S((n_tok, f2), BF16),
        compiler_params=_cp(("parallel",)))(u, u, da)


def fn_gdnpost(col, c):
    typ = col // HEADS
    y = _silu(c)
    n = y * lax.rsqrt(jnp.sum(y * y, axis=-1, keepdims=True) + 1e-6)
    n = n * jnp.where(typ == 0, DK ** -0.5, 1.0)
    return (jnp.where(typ < 2, n, y),)


def fn_gdnout(col, o, z, nw):
    return (_rms(o, nw) * _silu(z),)


def fn_glu(col, a, g, ba, bg):
    return ((a + ba) * jax.nn.sigmoid(g + bg),)


def fn_lnsilu(col, h, g, b):
    mu = jnp.mean(h, axis=-1, keepdims=True)
    var = jnp.mean(jnp.square(h - mu), axis=-1, keepdims=True)
    return (_silu((h - mu) * lax.rsqrt(var + LN_EPS) * g + b),)


def fn_merge(col, ya, yb, ga, gb, bo):
    return (jax.nn.sigmoid(ga) * ya + jax.nn.sigmoid(gb) * (yb + bo),)


HALO = 32


def conv_fwd(name, x, col_off, n_ch, w, bias, tb):
    n_tok = x.shape[0]
    k = w.shape[0]
    nt = n_tok // tb

    def body(xp_ref, xc_ref, w_ref, *rest):
        if bias is not None:
            b_ref, o_ref, xs = rest
        else:
            o_ref, xs = rest
        i = pl.program_id(1)
        xs[0:HALO, :] = jnp.where(i == 0, 0.0, xp_ref[tb - HALO:tb, :].astype(F32))
        xs[HALO:HALO + tb, :] = xc_ref[...].astype(F32)
        acc = jnp.zeros((tb, LANES), F32)
        for j in range(k):
            s = k - 1 - j
            acc = acc + w_ref[j:j + 1, :] * xs[HALO - s:HALO - s + tb, :]
        if bias is not None:
            acc = acc + b_ref[...]
        o_ref[...] = acc

    in_specs = [pl.BlockSpec((tb, LANES), lambda j, i: (jnp.maximum(i - 1, 0), col_off + j)),
                pl.BlockSpec((tb, LANES), lambda j, i: (i, col_off + j)),
                pl.BlockSpec((k, LANES), lambda j, i: (0, j))]
    args = [x, x, w]
    if bias is not None:
        in_specs.append(pl.BlockSpec((1, LANES), lambda j, i: (0, j)))
        args.append(bias)
    return pl.pallas_call(
        body, name=name, grid=(n_ch // LANES, nt), in_specs=in_specs,
        out_specs=pl.BlockSpec((tb, LANES), lambda j, i: (i, j)), out_shape=S((n_tok, n_ch), F32),
        scratch_shapes=[pltpu.VMEM((HALO + tb, LANES), F32)],
        compiler_params=_cp(("parallel", "parallel")))(*args)


def conv_bwd(name, x, col_off, n_ch, w, dy, dx_dtype, tb):
    n_tok = x.shape[0]
    k = w.shape[0]
    nt = n_tok // tb

    def body(xp_ref, xc_ref, w_ref, dyc_ref, dyn_ref, dx_ref, dw_ref, db_ref, xs, dys):
        i = pl.program_id(1)
        xs[0:HALO, :] = jnp.where(i == 0, 0.0, xp_ref[tb - HALO:tb, :].astype(F32))
        xs[HALO:HALO + tb, :] = xc_ref[...].astype(F32)
        dyc = dyc_ref[...]
        dys[0:tb, :] = dyc
        dys[tb:tb + HALO, :] = jnp.where(i == nt - 1, 0.0, dyn_ref[0:HALO, :])

        @pl.when(i == 0)
        def _():
            dw_ref[...] = jnp.zeros_like(dw_ref)
            db_ref[...] = jnp.zeros_like(db_ref)

        acc = jnp.zeros((tb, LANES), F32)
        for j in range(k):
            s = k - 1 - j
            acc = acc + w_ref[j:j + 1, :] * dys[s:s + tb, :]
            dw_ref[j:j + 1, :] += jnp.sum(dyc * xs[HALO - s:HALO - s + tb, :], axis=0, keepdims=True)
        dx_ref[...] = acc.astype(dx_ref.dtype)
        db_ref[...] += jnp.sum(dyc, axis=0, keepdims=True)

    in_specs = [pl.BlockSpec((tb, LANES), lambda j, i: (jnp.maximum(i - 1, 0), col_off + j)),
                pl.BlockSpec((tb, LANES), lambda j, i: (i, col_off + j)),
                pl.BlockSpec((k, LANES), lambda j, i: (0, j)),
                pl.BlockSpec((tb, LANES), lambda j, i: (i, j)),
                pl.BlockSpec((tb, LANES), lambda j, i: (jnp.minimum(i + 1, nt - 1), j))]
    return pl.pallas_call(
        body, name=name, grid=(n_ch // LANES, nt), in_specs=in_specs,
        out_specs=[pl.BlockSpec((tb, LANES), lambda j, i: (i, j)),
                   pl.BlockSpec((k, LANES), lambda j, i: (0, j)),
                   pl.BlockSpec((1, LANES), lambda j, i: (0, j))],
        out_shape=[S((n_tok, n_ch), dx_dtype), S((k, n_ch), F32), S((1, n_ch), F32)],
        scratch_shapes=[pltpu.VMEM((HALO + tb, LANES), F32), pltpu.VMEM((tb + HALO, LANES), F32)],
        compiler_params=_cp(("parallel", "arbitrary")))(x, x, w, dy, dy)


def gdnconv_fwd(name, p, w, tb):
    n_tok = p.shape[0]
    k = w.shape[0]
    nt = n_tok // tb

    def body(xp_ref, xc_ref, w_ref, o_ref, xs):
        j, i = pl.program_id(0), pl.program_id(1)
        xs[0:HALO, :] = jnp.where(i == 0, 0.0, xp_ref[tb - HALO:tb, :].astype(F32))
        xs[HALO:HALO + tb, :] = xc_ref[...].astype(F32)
        c = jnp.zeros((tb, LANES), F32)
        for t in range(k):
            s = k - 1 - t
            c = c + w_ref[t:t + 1, :] * xs[HALO - s:HALO - s + tb, :]
        y = c * jax.nn.sigmoid(c)
        r = lax.rsqrt(jnp.sum(y * y, axis=-1, keepdims=True) + 1e-6) * jnp.where(j < HEADS, DK ** -0.5, 1.0)
        o_ref[...] = jnp.where(j < 2 * HEADS, y * r, y)

    return pl.pallas_call(
        body, name=name, grid=(W_QKV // LANES, nt),
        in_specs=[pl.BlockSpec((tb, LANES), lambda j, i: (jnp.maximum(i - 1, 0), j)),
                  pl.BlockSpec((tb, LANES), lambda j, i: (i, j)),
                  pl.BlockSpec((k, LANES), lambda j, i: (0, j))],
        out_specs=pl.BlockSpec((tb, LANES), lambda j, i: (i, j)), out_shape=S((n_tok, W_QKV), F32),
        scratch_shapes=[pltpu.VMEM((HALO + tb, LANES), F32)],
        compiler_params=_cp(("parallel", "parallel")))(p, p, w)


def gdnconv_bwd(name, p, w, dn, tb):
    n_tok = p.shape[0]
    k = w.shape[0]
    nt = n_tok // tb
    ext = tb + HALO

    def body(xp_ref, xc_ref, xn_ref, w_ref, dnc_ref, dnn_ref, dx_ref, dw_ref, xs, dns, dcs):
        j, i = pl.program_id(0), pl.program_id(1)
        xs[0:HALO, :] = jnp.where(i == 0, 0.0, xp_ref[tb - HALO:tb, :].astype(F32))
        xs[HALO:HALO + tb, :] = xc_ref[...].astype(F32)
        xs[HALO + tb:HALO + ext, :] = jnp.where(i == nt - 1, 0.0, xn_ref[0:HALO, :].astype(F32))
        dns[0:tb, :] = dnc_ref[...]
        dns[tb:ext, :] = jnp.where(i == nt - 1, 0.0, dnn_ref[0:HALO, :])

        @pl.when(i == 0)
        def _():
            dw_ref[...] = jnp.zeros_like(dw_ref)

        c = jnp.zeros((ext, LANES), F32)
        for t in range(k):
            s = k - 1 - t
            c = c + w_ref[t:t + 1, :] * xs[HALO - s:HALO - s + ext, :]
        d = dns[...]
        sg = jax.nn.sigmoid(c)
        y = c * sg
        r = lax.rsqrt(jnp.sum(y * y, axis=-1, keepdims=True) + 1e-6)
        scale = jnp.where(j < HEADS, DK ** -0.5, 1.0)
        dy_norm = scale * (d * r - y * (r * r * r) * jnp.sum(d * y, axis=-1, keepdims=True))
        dy = jnp.where(j < 2 * HEADS, dy_norm, d)
        dc = dy * (sg * (1.0 + c * (1.0 - sg)))
        dcs[...] = dc
        acc = jnp.zeros((tb, LANES), F32)
        for t in range(k):
            s = k - 1 - t
            acc = acc + w_ref[t:t + 1, :] * dcs[s:s + tb, :]
            dw_ref[t:t + 1, :] += jnp.sum(dcs[0:tb, :] * xs[HALO - s:HALO - s + tb, :], axis=0, keepdims=True)
        dx_ref[...] = acc.astype(dx_ref.dtype)

    cur = lambda j, i: (i, j)
    nxt = lambda j, i: (jnp.minimum(i + 1, nt - 1), j)
    return pl.pallas_call(
        body, name=name, grid=(W_QKV // LANES, nt),
        in_specs=[pl.BlockSpec((tb, LANES), lambda j, i: (jnp.maximum(i - 1, 0), j)),
                  pl.BlockSpec((tb, LANES), cur), pl.BlockSpec((tb, LANES), nxt),
                  pl.BlockSpec((k, LANES), lambda j, i: (0, j)),
                  pl.BlockSpec((tb, LANES), cur), pl.BlockSpec((tb, LANES), nxt)],
        out_specs=[pl.BlockSpec((tb, LANES), cur), pl.BlockSpec((k, LANES), lambda j, i: (0, j))],
        out_shape=[S((n_tok, W_QKV), BF16), S((k, W_QKV), F32)],
        scratch_shapes=[pltpu.VMEM((HALO + ext, LANES), F32), pltpu.VMEM((ext, LANES), F32),
                        pltpu.VMEM((ext, LANES), F32)],
        compiler_params=_cp(("parallel", "arbitrary")))(p, p, p, w, dn, dn)


GDN_GROUP = 2


def _dotb(a, b, ca, cb):
    return lax.dot_general(a.astype(BF16), b.astype(BF16), (((ca,), (cb,)), ((), ())), preferred_element_type=F32)


def _dot32(a, b, ca, cb):
    return lax.dot_general(a, b, (((ca,), (cb,)), ((), ())), preferred_element_type=F32,
                           precision=lax.Precision.HIGHEST)


def _dot3_many(xs, ys, ca, cb):
    xh = [x.astype(BF16) for x in xs]
    xl = [(x - h.astype(F32)).astype(BF16) for x, h in zip(xs, xh)]
    yh = [y.astype(BF16) for y in ys]
    yl = [(y - h.astype(F32)).astype(BF16) for y, h in zip(ys, yh)]
    dg = lambda p, q: lax.dot_general(p, q, (((ca,), (cb,)), ((), ())), preferred_element_type=F32)
    hh = [dg(p, q) for p, q in zip(xh, yh)]
    hl = [dg(p, q) for p, q in zip(xh, yl)]
    lh = [dg(p, q) for p, q in zip(xl, yh)]
    return [a + (b + c) for a, b, c in zip(hh, hl, lh)]


@jax.custom_vjp
def _mm3_many(xs, ys):
    return _dot3_many(xs, ys, 1, 0)


def _mm3_fwd(xs, ys):
    return _dot3_many(xs, ys, 1, 0), (xs, ys)


def _mm3_bwd(res, cts):
    xs, ys = res
    return _dot3_many(cts, ys, 1, 1), _dot3_many(xs, cts, 0, 0)


_mm3_many.defvjp(_mm3_fwd, _mm3_bwd)


@jax.custom_vjp
def _inv_unit_lower_many(mats):
    n = mats[0].shape[0]
    eye = (lax.broadcasted_iota(jnp.int32, (n, n), 0) == lax.broadcasted_iota(jnp.int32, (n, n), 1)).astype(F32)
    inv = [eye - a for a in mats]
    p = list(mats)
    for _ in range(int(math.log2(n)) - 1):
        p = _dot3_many(p, p, 1, 0)
        upd = _dot3_many(inv, p, 1, 0)
        inv = [i + u for i, u in zip(inv, upd)]
    return inv


def _inv_fwd(mats):
    t = _inv_unit_lower_many(mats)
    return t, t


def _inv_bwd(t, dt):
    x = _dot3_many(t, dt, 0, 0)
    return ([-y for y in _dot3_many(x, t, 1, 1)],)


_inv_unit_lower_many.defvjp(_inv_fwd, _inv_bwd)


def _softplus(x):
    return jnp.maximum(x, 0.0) + jnp.log(1.0 + jnp.exp(-jnp.abs(x)))


def _gdn_intra(qs, ks, vs, pbas, alog, dtb):
    c = pbas[0].shape[0]
    row = lax.broadcasted_iota(jnp.int32, (c, c), 0)
    colm = lax.broadcasted_iota(jnp.int32, (c, c), 1)
    causal, strict = row >= colm, row > colm
    tril = causal.astype(F32)
    lane = lax.broadcasted_iota(jnp.int32, (1, LANES), 1)
    sub = lax.broadcasted_iota(jnp.int32, (LANES, 1), 0)
    last = (lax.broadcasted_iota(jnp.int32, (c, 1), 0) == c - 1).astype(F32)
    beta_all = [jax.nn.sigmoid(pb) for pb in pbas]
    g_all = [-jnp.exp(alog) * _softplus(pb + dtb) for pb in pbas]
    gc_all = [_dot32(tril, ga, 1, 0) for ga in g_all]
    gr_all = [_dot32(ga, tril, 0, 1) for ga in g_all]
    idx = [(g, h) for g in range(len(pbas)) for h in range(HEADS)]
    beta = [jnp.sum(beta_all[g] * (lane == h).astype(F32), axis=1, keepdims=True) for g, h in idx]
    gc = [jnp.sum(gc_all[g] * (lane == HEADS + h).astype(F32), axis=1, keepdims=True) for g, h in idx]
    gr = [jnp.sum(gr_all[g] * (sub == HEADS + h).astype(F32), axis=0, keepdims=True) for g, h in idx]
    decay = [jnp.where(causal, jnp.exp(jnp.where(causal, a - b, 0.0)), 0.0) for a, b in zip(gc, gr)]
    kk = [_dotb(k, k, 1, 1) for k in ks]
    tinv = _inv_unit_lower_many([jnp.where(strict, x * d * b, 0.0) for x, d, b in zip(kk, decay, beta)])
    eg = [jnp.exp(a) for a in gc]
    g_last = [jnp.sum(a * last, axis=0, keepdims=True) for a in gc]
    us = _mm3_many(tinv, [v * b for v, b in zip(vs, beta)])
    ws = _mm3_many(tinv, [k * (b * e) for k, b, e in zip(ks, beta, eg)])
    qds = [q * e for q, e in zip(qs, eg)]
    kds = [k * jnp.exp(gl - a) for k, gl, a in zip(ks, g_last, gc)]
    qks = [_dotb(q, k, 1, 1) * d for q, k, d in zip(qs, ks, decay)]
    decs = [jnp.exp(gl) for gl in g_last]
    return us, ws, qds, kds, qks, decs


def _gdn_seq(us, ws, qds, kds, qks, decs, states):
    corr = [_dotb(w, st, 1, 0) for w, st in zip(ws, states)]
    from_state = [_dotb(qd, st, 1, 0) for qd, st in zip(qds, states)]
    v_new = [u - x for u, x in zip(us, corr)]
    intra = [_dotb(qk, vn, 1, 0) for qk, vn in zip(qks, v_new)]
    upd = [_dotb(kd, vn, 0, 0) for kd, vn in zip(kds, v_new)]
    outs = [a + b for a, b in zip(from_state, intra)]
    news = [st * d + x for st, d, x in zip(states, decs, upd)]
    return outs, news


def _heads(ref, rows=slice(None), base=0):
    return [ref[rows, (base + h) * DK:(base + h + 1) * DK].astype(F32) for h in range(HEADS)]


def _qk_heads(ref, rows=slice(None)):
    return [ref[rows, h * DK:h * DK + CHUNK].astype(F32) for h in range(HEADS)]


def _put_heads(ref, vals, rows=slice(None), base=0):
    for h in range(HEADS):
        ref[rows, (base + h) * DK:(base + h + 1) * DK] = vals[h].astype(ref.dtype)


def _put_qk(ref, vals, rows=slice(None)):
    for h in range(HEADS):
        ref[rows, h * DK:h * DK + CHUNK] = vals[h].astype(ref.dtype)
        ref[rows, h * DK + CHUNK:(h + 1) * DK] = jnp.zeros(vals[h].shape, ref.dtype)


def _group(n_chunks):
    return GDN_GROUP if n_chunks % GDN_GROUP == 0 else 1


def _decs(ref):
    return [ref[h:h + 1, 0:1] for h in range(HEADS)]


def gdn_intra_fwd(name, qkvn, p, alog, dtb):
    n_tok = qkvn.shape[0]
    n = n_tok // CHUNK
    grp = _group(n)
    hd = HEADS * DK
    rb = grp * CHUNK

    def body(q_ref, k_ref, v_ref, pba_ref, al_ref, dt_ref, u_ref, w_ref, qd_ref, kd_ref, qk_ref, dec_ref):
        rows = [slice(g * CHUNK, (g + 1) * CHUNK) for g in range(grp)]
        cat = lambda ref: [t for r in rows for t in _heads(ref, r)]
        us, ws, qds, kds, qks, decs = _gdn_intra(cat(q_ref), cat(k_ref), cat(v_ref), [pba_ref[r, :].astype(F32) for r in rows],
                                                 al_ref[...], dt_ref[...])
        for g, r in enumerate(rows):
            part = slice(g * HEADS, (g + 1) * HEADS)
            _put_heads(u_ref, us[part], r)
            _put_heads(w_ref, ws[part], r)
            _put_heads(qd_ref, qds[part], r)
            _put_heads(kd_ref, kds[part], r)
            _put_qk(qk_ref, qks[part], r)
            for h in range(HEADS):
                dec_ref[g * HEADS + h:g * HEADS + h + 1, :] = jnp.broadcast_to(decs[g * HEADS + h], (1, LANES))

    blk = lambda c: pl.BlockSpec((rb, hd), lambda i: (i, c))
    par = pl.BlockSpec((1, LANES), lambda i: (0, 0))
    return pl.pallas_call(
        body, name=name, grid=(n // grp,),
        in_specs=[blk(0), blk(1), blk(2), pl.BlockSpec((rb, LANES), lambda i: (i, COL_BA)), par, par],
        out_specs=[blk(0)] * 5 + [pl.BlockSpec((grp * HEADS, LANES), lambda i: (i, 0))],
        out_shape=[S((n_tok, hd), F32)] + [S((n_tok, hd), BF16)] * 4 + [S((n * HEADS, LANES), F32)],
        compiler_params=_cp(("parallel",)))(qkvn, qkvn, qkvn, p, alog, dtb)


def gdn_seq_fwd(name, u, w, qd, kd, qk, dec):
    n_tok = u.shape[0]
    n = n_tok // CHUNK
    hd = HEADS * DK

    def body(u_ref, w_ref, qd_ref, kd_ref, qk_ref, dec_ref, o_ref, s_ref, st):
        @pl.when(pl.program_id(0) == 0)
        def _():
            st[...] = jnp.zeros_like(st)

        s_ref[...] = st[...].astype(s_ref.dtype)
        states = [st[h * DK:(h + 1) * DK, :] for h in range(HEADS)]
        outs, news = _gdn_seq(_heads(u_ref), _heads(w_ref), _heads(qd_ref), _heads(kd_ref), _qk_heads(qk_ref),
                              _decs(dec_ref), states)
        _put_heads(o_ref, outs)
        for h in range(HEADS):
            st[h * DK:(h + 1) * DK, :] = news[h]

    blk = pl.BlockSpec((CHUNK, hd), lambda i: (i, 0))
    return pl.pallas_call(
        body, name=name, grid=(n,),
        in_specs=[blk] * 5 + [pl.BlockSpec((HEADS, LANES), lambda i: (i, 0))],
        out_specs=[blk, pl.BlockSpec((None, hd, DK), lambda i: (i, 0, 0))],
        out_shape=[S((n_tok, hd), F32), S((n, hd, DK), BF16)],
        scratch_shapes=[pltpu.VMEM((hd, DK), F32)],
        compiler_params=_cp(("arbitrary",)))(u, w, qd, kd, qk, dec)


def gdn_seq_bwd(name, u, w, qd, kd, qk, dec, states, do):
    n_tok = u.shape[0]
    n = n_tok // CHUNK
    hd = HEADS * DK

    def body(u_ref, w_ref, qd_ref, kd_ref, qk_ref, dec_ref, s_ref, do_ref,
             du_ref, dw_ref, dqd_ref, dkd_ref, dqk_ref, ddec_ref, dst):
        @pl.when(pl.program_id(0) == 0)
        def _():
            dst[...] = jnp.zeros_like(dst)

        states = [s_ref[h * DK:(h + 1) * DK, :].astype(F32) for h in range(HEADS)]
        _, vjp = jax.vjp(_gdn_seq, _heads(u_ref), _heads(w_ref), _heads(qd_ref), _heads(kd_ref), _qk_heads(qk_ref),
                         _decs(dec_ref), states)
        d_news = [dst[h * DK:(h + 1) * DK, :] for h in range(HEADS)]
        du, dw, dqd, dkd, dqk, ddec, dstates = vjp((_heads(do_ref), d_news))
        _put_heads(du_ref, du)
        _put_heads(dw_ref, dw)
        _put_heads(dqd_ref, dqd)
        _put_heads(dkd_ref, dkd)
        _put_qk(dqk_ref, dqk)
        for h in range(HEADS):
            ddec_ref[h:h + 1, :] = jnp.broadcast_to(ddec[h], (1, LANES))
            dst[h * DK:(h + 1) * DK, :] = dstates[h]

    blk = pl.BlockSpec((CHUNK, hd), lambda i: (n - 1 - i, 0))
    dspec = pl.BlockSpec((HEADS, LANES), lambda i: (n - 1 - i, 0))
    return pl.pallas_call(
        body, name=name, grid=(n,),
        in_specs=[blk] * 5 + [dspec, pl.BlockSpec((None, hd, DK), lambda i: (n - 1 - i, 0, 0)), blk],
        out_specs=[blk] * 5 + [dspec],
        out_shape=[S((n_tok, hd), F32)] * 5 + [S((n * HEADS, LANES), F32)],
        scratch_shapes=[pltpu.VMEM((hd, DK), F32)],
        compiler_params=_cp(("arbitrary",)))(u, w, qd, kd, qk, dec, states, do)


def gdn_intra_bwd(name, qkvn, p, alog, dtb, du, dw, dqd, dkd, dqk, ddec):
    n_tok = qkvn.shape[0]
    n = n_tok // CHUNK
    grp = _group(n)
    hd = HEADS * DK
    rb = grp * CHUNK

    def body(q_ref, k_ref, v_ref, pba_ref, al_ref, dt_ref, du_ref, dw_ref, dqd_ref, dkd_ref, dqk_ref, ddec_ref,
             dqkv_ref, dpba_ref, dal_ref, ddt_ref):
        @pl.when(pl.program_id(0) == 0)
        def _():
            dal_ref[...] = jnp.zeros_like(dal_ref)
            ddt_ref[...] = jnp.zeros_like(ddt_ref)

        rows = [slice(g * CHUNK, (g + 1) * CHUNK) for g in range(grp)]
        cat = lambda ref: [t for r in rows for t in _heads(ref, r)]
        _, vjp = jax.vjp(_gdn_intra, cat(q_ref), cat(k_ref), cat(v_ref), [pba_ref[r, :].astype(F32) for r in rows],
                         al_ref[...], dt_ref[...])
        cts = (cat(du_ref), cat(dw_ref), cat(dqd_ref), cat(dkd_ref), [t for r in rows for t in _qk_heads(dqk_ref, r)],
               [ddec_ref[i:i + 1, 0:1] for i in range(grp * HEADS)])
        dq, dk, dv, dpba, dal, ddt = vjp(cts)
        for g, r in enumerate(rows):
            part = slice(g * HEADS, (g + 1) * HEADS)
            _put_heads(dqkv_ref, dq[part], r, 0)
            _put_heads(dqkv_ref, dk[part], r, HEADS)
            _put_heads(dqkv_ref, dv[part], r, 2 * HEADS)
            dpba_ref[r, :] = dpba[g].astype(dpba_ref.dtype)
        dal_ref[...] += dal
        ddt_ref[...] += ddt

    blk = lambda c: pl.BlockSpec((rb, hd), lambda i: (i, c))
    par = pl.BlockSpec((1, LANES), lambda i: (0, 0))
    return pl.pallas_call(
        body, name=name, grid=(n // grp,),
        in_specs=[blk(0), blk(1), blk(2), pl.BlockSpec((rb, LANES), lambda i: (i, COL_BA)), par, par]
        + [blk(0)] * 5 + [pl.BlockSpec((grp * HEADS, LANES), lambda i: (i, 0))],
        out_specs=[pl.BlockSpec((rb, 3 * hd), lambda i: (i, 0)), pl.BlockSpec((rb, LANES), lambda i: (i, 0)), par, par],
        out_shape=[S((n_tok, 3 * hd), F32), S((n_tok, LANES), BF16), S((1, LANES), F32), S((1, LANES), F32)],
        compiler_params=_cp(("arbitrary",)))(qkvn, qkvn, qkvn, p, alog, dtb, du, dw, dqd, dkd, dqk, ddec)


def loss_head(name, y, target, tm):
    n_tok, d = y.shape

    def body(y_ref, t_ref, dy_ref, l_ref):
        @pl.when(pl.program_id(0) == 0)
        def _():
            l_ref[...] = jnp.zeros_like(l_ref)

        e = y_ref[...] - t_ref[...]
        dy_ref[...] = e * (1.0 / d)
        l_ref[...] += jnp.sum(e * e, keepdims=True) * (0.5 / d)

    spec = pl.BlockSpec((tm, d), lambda i: (i, 0))
    return pl.pallas_call(
        body, name=name, grid=(n_tok // tm,), in_specs=[spec, spec],
        out_specs=[spec, pl.BlockSpec((1, 1), lambda i: (0, 0))], out_shape=[S((n_tok, d), F32), S((1, 1), F32)],
        compiler_params=_cp(("arbitrary",)))(y, target)


def _tm(n_tok):
    return min(512, n_tok)


def ffn_fwd(tag, x, w):
    n_tok = x.shape[0]
    tm = _tm(n_tok)
    g1 = (1, n_tok // tm)
    tD = _tok(D_MODEL)(tm)
    (h,) = ew_fwd(tag + "_rms", fn_rms, g1, [(x, tD), (w["norm_pre"], _par(D_MODEL))], [(S((n_tok, D_MODEL), BF16), tD)])
    u = mm(tag + "_in", h, w["w_in"], out_dtype=BF16)
    tF = lambda c: _tok(D_FF, c)(tm)
    (a,) = ew_fwd(tag + "_swiglu", fn_swiglu, g1, [(u, tF(0)), (u, tF(1))], [(S((n_tok, D_FF), BF16), tF(0))])
    f = mm(tag + "_out", a, w["w_out"])
    fn_res = lambda col, x_, f_, w_: (x_ + 0.5 * _rms(f_, w_),)
    (xo,) = ew_fwd(tag + "_res", fn_res, g1, [(x, tD), (f, tD), (w["norm_post"], _par(D_MODEL))],
                   [(S((n_tok, D_MODEL), F32), tD)])
    return xo, dict(x=x, h=h, u=u, a=a, f=f)


def ffn_bwd(tag, dxo, sv, w):
    n_tok = dxo.shape[0]
    tm = _tm(n_tok)
    g1 = (1, n_tok // tm)
    tD = _tok(D_MODEL)(tm)
    pD = _par(D_MODEL)
    fn_post = lambda col, f_, w_: (0.5 * _rms(f_, w_),)
    df, d_post = ew_bwd(tag + "_res_b", fn_post, g1, [(sv["f"], tD), (w["norm_post"], pD)], [(dxo, tD)],
                        [(0, S((n_tok, D_MODEL), BF16), tD)], [(1, S((1, D_MODEL), F32), pD, False)])
    da = mm(tag + "_out_bx", df, w["w_out"], tb=True, out_dtype=BF16)
    d_wout = mm(tag + "_out_bw", sv["a"], df, ta=True)
    tF = lambda c: _tok(D_FF, c)(tm)
    du = swiglu_bwd(tag + "_swiglu_b", sv["u"], da, tm)
    dh = mm(tag + "_in_bx", du, w["w_in"], tb=True)
    d_win = mm(tag + "_in_bw", sv["h"], du, ta=True)
    dx, d_pre = ew_bwd(tag + "_rms_b", fn_rms, g1, [(sv["x"], tD), (w["norm_pre"], pD)], [(dh, tD)],
                       [(0, S((n_tok, D_MODEL), F32), tD)], [(1, S((1, D_MODEL), F32), pD, False)], add=(dxo, tD))
    return dx, dict(norm_pre=d_pre, norm_post=d_post, w_in=d_win, w_out=d_wout)


def mix_fwd(tag, x, w):
    n_tok = x.shape[0]
    tm = _tm(n_tok)
    nt = n_tok // tm
    g1 = (1, nt)
    tD = _tok(D_MODEL)(tm)
    pD = _par(D_MODEL)
    (h,) = ew_fwd(tag + "_rms", fn_rms, g1, [(x, tD), (w["norm_pre"], pD)], [(S((n_tok, D_MODEL), BF16), tD)])
    p = mm(tag + "_in", h, w["w_all"], out_dtype=BF16)
    qkvn = gdnconv_fwd(tag + "_gconv", p, w["conv_w"], tm)
    tC = _tokcol()(tm)
    intra = gdn_intra_fwd(tag + "_gintra", qkvn, p, w["alog"], w["dtb"])
    o, states = gdn_seq_fwd(tag + "_gseq", *intra)
    (on,) = ew_fwd(tag + "_gout", fn_gdnout, (HEADS, nt),
                   [(o, tC), (p, _tokcol(COL_Z)(tm)), (w["gdn_norm_w"], _par(LANES))],
                   [(S((n_tok, D_MODEL), BF16), tC)])
    ya = mm(tag + "_go", on, w["gdn_w_o"])
    (hglu,) = ew_fwd(tag + "_glu", fn_glu, (D_MODEL // LANES, nt),
                     [(p, _tokcol(COL_GLU)(tm)), (p, _tokcol(COL_GLU + D_MODEL // LANES)(tm)),
                      (w["pw1_b"], _parcol(0)), (w["pw1_b"], _parcol(D_MODEL // LANES))],
                     [(S((n_tok, D_MODEL), F32), tC)])
    hc = conv_fwd(tag + "_cconv", hglu, 0, D_MODEL, w["dw_w"], w["dw_b"], tm)
    (hs,) = ew_fwd(tag + "_ln", fn_lnsilu, g1, [(hc, tD), (w["ln_g"], pD), (w["ln_b"], pD)],
                   [(S((n_tok, D_MODEL), BF16), tD)])
    yb = mm(tag + "_co", hs, w["cnv_w_o"])
    tG = lambda cb: pl.BlockSpec((tm, D_MODEL), lambda j, i: (i, cb))
    gcol = (W_QKV + W_Z + W_GLU) // D_MODEL
    (ym,) = ew_fwd(tag + "_merge", fn_merge, g1, [(ya, tD), (yb, tD), (p, tG(gcol)), (p, tG(gcol + 1)), (w["b_o"], pD)],
                   [(S((n_tok, D_MODEL), BF16), tD)])
    y = mm(tag + "_wo", ym, w["w_out"])
    fn_res = lambda col, x_, f_, w_: (x_ + _rms(f_, w_),)
    (xo,) = ew_fwd(tag + "_res", fn_res, g1, [(x, tD), (y, tD), (w["norm_post"], pD)], [(S((n_tok, D_MODEL), F32), tD)])
    sv = dict(x=x, h=h, p=p, qkvn=qkvn, intra=intra, states=states, o=o, on=on, ya=ya, hglu=hglu, hc=hc, hs=hs, yb=yb, ym=ym, y=y)
    return xo, sv


def mix_bwd(tag, dxo, sv, w):
    n_tok = dxo.shape[0]
    tm = _tm(n_tok)
    nt = n_tok // tm
    g1 = (1, nt)
    tD = _tok(D_MODEL)(tm)
    pD = _par(D_MODEL)
    tC = _tokcol()(tm)
    p = sv["p"]
    sD = lambda dt: S((n_tok, D_MODEL), dt)
    fn_post = lambda col, f_, w_: (_rms(f_, w_),)
    dy, d_post = ew_bwd(tag + "_res_b", fn_post, g1, [(sv["y"], tD), (w["norm_post"], pD)], [(dxo, tD)],
                        [(0, sD(BF16), tD)], [(1, S((1, D_MODEL), F32), pD, False)])
    dym = mm(tag + "_wo_bx", dy, w["w_out"], tb=True)
    d_wout = mm(tag + "_wo_bw", sv["ym"], dy, ta=True)
    tG = lambda cb: pl.BlockSpec((tm, D_MODEL), lambda j, i: (i, cb))
    gcol = (W_QKV + W_Z + W_GLU) // D_MODEL
    dya, dyb, dga, dgb, d_bo = ew_bwd(
        tag + "_merge_b", fn_merge, g1, [(sv["ya"], tD), (sv["yb"], tD), (p, tG(gcol)), (p, tG(gcol + 1)), (w["b_o"], pD)],
        [(dym, tD)], [(0, sD(BF16), tD), (1, sD(BF16), tD), (2, sD(BF16), tD), (3, sD(BF16), tD)],
        [(4, S((1, D_MODEL), F32), pD, False)])
    dhs = mm(tag + "_co_bx", dyb, w["cnv_w_o"], tb=True)
    d_cwo = mm(tag + "_co_bw", sv["hs"], dyb, ta=True)
    dhc, d_lng, d_lnb = ew_bwd(tag + "_ln_b", fn_lnsilu, g1, [(sv["hc"], tD), (w["ln_g"], pD), (w["ln_b"], pD)], [(dhs, tD)],
                               [(0, sD(F32), tD)], [(1, S((1, D_MODEL), F32), pD, False), (2, S((1, D_MODEL), F32), pD, False)])
    dhglu, d_dww, d_dwb = conv_bwd(tag + "_cconv_b", sv["hglu"], 0, D_MODEL, w["dw_w"], dhc, F32, tm)
    nc = D_MODEL // LANES
    dpa, dpg, d_ba, d_bg = ew_bwd(
        tag + "_glu_b", fn_glu, (nc, nt),
        [(p, _tokcol(COL_GLU)(tm)), (p, _tokcol(COL_GLU + nc)(tm)), (w["pw1_b"], _parcol(0)), (w["pw1_b"], _parcol(nc))],
        [(dhglu, tC)], [(0, sD(BF16), tC), (1, sD(BF16), tC)],
        [(2, S((1, D_MODEL), F32), _parcol(0), False), (3, S((1, D_MODEL), F32), _parcol(0), False)])
    don = mm(tag + "_go_bx", dya, w["gdn_w_o"], tb=True)
    d_gwo = mm(tag + "_go_bw", sv["on"], dya, ta=True)
    do, dz, d_gnw = ew_bwd(tag + "_gout_b", fn_gdnout, (HEADS, nt),
                           [(sv["o"], tC), (p, _tokcol(COL_Z)(tm)), (w["gdn_norm_w"], _par(LANES))], [(don, tC)],
                           [(0, sD(F32), tC), (1, sD(BF16), tC)], [(2, S((1, LANES), F32), _par(LANES), True)])
    d_intra = gdn_seq_bwd(tag + "_gseq_b", *sv["intra"], sv["states"], do)
    dqkvn, dpba, d_alog, d_dtb = gdn_intra_bwd(tag + "_gintra_b", sv["qkvn"], p, w["alog"], w["dtb"], *d_intra)
    dqkv, d_convw = gdnconv_bwd(tag + "_gconv_b", p, w["conv_w"], dqkvn, tm)
    nd = D_MODEL // LANES
    pieces = [(dqkv, 0, 0), (dz, W_QKV // D_MODEL, COL_Z), (dpa, COL_GLU // nd, COL_GLU), (dpg, COL_GLU // nd + 1, COL_GLU + nd),
              (dga, COL_GATE // nd, COL_GATE), (dgb, COL_GATE // nd + 1, COL_GATE + nd), (dpba, COL_BA, COL_BA)]
    dh = mm_nt_sum(tag + "_in_bx", [(a, blk) for a, blk, _ in pieces], w["w_all"])
    d_wall = [mm(tag + "_in_bw", sv["h"], a, ta=True) for a, _, _ in pieces]
    dx, d_pre = ew_bwd(tag + "_rms_b", fn_rms, g1, [(sv["x"], tD), (w["norm_pre"], pD)], [(dh, tD)],
                       [(0, sD(F32), tD)], [(1, S((1, D_MODEL), F32), pD, False)], add=(dxo, tD))
    grads = dict(norm_pre=d_pre, norm_post=d_post, w_all=d_wall, conv_w=d_convw, alog=d_alog, dtb=d_dtb,
                 gdn_norm_w=d_gnw, gdn_w_o=d_gwo, pw1_b=jnp.concatenate([d_ba, d_bg], axis=1), dw_w=d_dww,
                 dw_b=d_dwb, ln_g=d_lng, ln_b=d_lnb, cnv_w_o=d_cwo, b_o=d_bo, w_out=d_wout)
    return dx, grads


def local_step(x, target, layers):
    saved = []
    for i, lw in enumerate(layers):
        x, s1 = ffn_fwd("ffn", x, lw["ffn1"])
        x, s2 = mix_fwd("mix", x, lw["mix"])
        x, s3 = ffn_fwd("ffn", x, lw["ffn2"])
        saved.append((s1, s2, s3))
    dx, loss = loss_head("loss", x, target, _tm(x.shape[0]))
    grads = [None] * len(layers)
    for i in reversed(range(len(layers))):
        lw = layers[i]
        s1, s2, s3 = saved[i]
        dx, g3 = ffn_bwd("ffn", dx, s3, lw["ffn2"])
        dx, g2 = mix_bwd("mix", dx, s2, lw["mix"])
        dx, g1 = ffn_bwd("ffn", dx, s1, lw["ffn1"])
        grads[i] = dict(ffn1=g1, mix=g2, ffn2=g3)
    return loss, dx, grads


_O_BA = W_QKV + W_Z
_O_GLU = _O_BA + 2 * HEADS


def prep_layer(wl):
    row = lambda v: v.reshape(1, -1).astype(F32)
    bf = lambda v: v.astype(BF16)
    lanes8 = lambda v: jnp.zeros((1, LANES), F32).at[0, HEADS:2 * HEADS].set(v.astype(F32))
    mw = bf(wl["mix_w_in"])
    w_all = jnp.concatenate([mw[:, :_O_BA], mw[:, _O_GLU:], mw[:, _O_BA:_O_GLU],
                             jnp.zeros((D_MODEL, LANES - 2 * HEADS), BF16)], axis=1)
    ffn = lambda k: dict(norm_pre=row(wl[k + "_norm_pre"]), norm_post=row(wl[k + "_norm_post"]),
                         w_in=bf(wl[k + "_w_in"]), w_out=bf(wl[k + "_w_out"]))
    mix = dict(norm_pre=row(wl["mix_norm_pre"]), norm_post=row(wl["mix_norm_post"]), w_all=w_all,
               conv_w=wl["gdn_conv_w"].astype(F32), alog=lanes8(wl["gdn_a_log"]), dtb=lanes8(wl["gdn_dt_bias"]),
               gdn_norm_w=row(wl["gdn_norm_w"]), gdn_w_o=bf(wl["gdn_w_o"]), pw1_b=row(wl["cnv_pw1_b"]),
               dw_w=wl["cnv_dw_w"].astype(F32), dw_b=row(wl["cnv_dw_b"]), ln_g=row(wl["cnv_ln_g"]),
               ln_b=row(wl["cnv_ln_b"]), cnv_w_o=bf(wl["cnv_w_o"]), b_o=row(wl["cnv_b_o"]), w_out=bf(wl["mix_w_out"]))
    return dict(ffn1=ffn("ffn1"), mix=mix, ffn2=ffn("ffn2"))


def unprep_grads(g):
    m = g["mix"]
    dqkv, dz, dpa, dpg, dga, dgb, dba = m["w_all"]
    out = {}
    for k in ("ffn1", "ffn2"):
        out[k + "_norm_pre"] = g[k]["norm_pre"][0]
        out[k + "_norm_post"] = g[k]["norm_post"][0]
        out[k + "_w_in"] = g[k]["w_in"]
        out[k + "_w_out"] = g[k]["w_out"]
    out.update(
        mix_norm_pre=m["norm_pre"][0], mix_norm_post=m["norm_post"][0],
        mix_w_in=jnp.concatenate([dqkv, dz, dba[:, :2 * HEADS], dpa, dpg, dga, dgb], axis=1),
        gdn_conv_w=m["conv_w"], gdn_a_log=m["alog"][0, HEADS:2 * HEADS], gdn_dt_bias=m["dtb"][0, HEADS:2 * HEADS],
        gdn_norm_w=m["gdn_norm_w"][0], gdn_w_o=m["gdn_w_o"], cnv_pw1_b=m["pw1_b"][0], cnv_dw_w=m["dw_w"],
        cnv_dw_b=m["dw_b"][0], cnv_ln_g=m["ln_g"][0], cnv_ln_b=m["ln_b"][0], cnv_w_o=m["cnv_w_o"], cnv_b_o=m["b_o"][0],
        mix_w_out=m["w_out"])
    return out


MESH = pl.DeviceIdType.MESH
ANY = pl.BlockSpec(memory_space=pl.ANY)
N_DEV = 8


def _pos():
    return lax.axis_index("x"), lax.axis_index("y"), lax.axis_index("c")


def _other_chips(x, y):
    return [(1 - x, y), (x, 1 - y), (1 - x, 1 - y)]


def gather_weights(ws):
    n = len(ws)

    def body(*refs):
        w_refs, o_refs = refs[:n], refs[n:2 * n]
        send_sems, recv_sems, loc_sems = refs[2 * n:]
        x, y, c = _pos()
        b = 2 * x + y
        chips = _other_chips(x, y)

        def ici(a, j, blk):
            return pltpu.make_async_remote_copy(
                src_ref=w_refs[a].at[c], dst_ref=o_refs[a].at[blk, c], send_sem=send_sems.at[4 * a + j],
                recv_sem=recv_sems.at[4 * a + j], device_id=(chips[j][0], chips[j][1], c), device_id_type=MESH)

        def d2d(a, layer):
            part = o_refs[a].at[:, pl.ds(layer, 1)]
            return pltpu.make_async_remote_copy(
                src_ref=part, dst_ref=part, send_sem=send_sems.at[4 * a + 3], recv_sem=recv_sems.at[4 * a + 3],
                device_id=(x, y, 1 - c), device_id_type=MESH)

        locs = [pltpu.make_async_copy(w_refs[a].at[c], o_refs[a].at[b, c], loc_sems.at[a]) for a in range(n)]
        for cp in locs:
            cp.start()
        sends = [ici(a, j, b) for a in range(n) for j in range(3)]
        for cp in sends:
            cp.start()
        for a in range(n):
            for j in range(3):
                ici(a, j, 2 * chips[j][0] + chips[j][1]).wait_recv()
        for cp in locs:
            cp.wait()
        passed = [d2d(a, c) for a in range(n)]
        for cp in passed:
            cp.start()
        for a in range(n):
            d2d(a, 1 - c).wait_recv()
        for cp in sends + passed:
            cp.wait_send()

    return pl.pallas_call(
        body, name="gather_weights", in_specs=[ANY] * n, out_specs=[ANY] * n,
        out_shape=[S((N_BLK,) + w.shape, w.dtype) for w in ws],
        scratch_shapes=[pltpu.SemaphoreType.DMA((4 * n,)), pltpu.SemaphoreType.DMA((4 * n,)),
                        pltpu.SemaphoreType.DMA((n,))])(*ws)


def rs_sibling(gs):
    n = len(gs)

    def body(*refs):
        g_refs, r_refs = refs[:n], refs[n:2 * n]
        send_sems, recv_sems = refs[2 * n:]
        x, y, c = _pos()

        def cp(k):
            hr = gs[k].shape[1] // 2
            return pltpu.make_async_remote_copy(
                src_ref=g_refs[k].at[:, pl.ds((1 - c) * hr, hr)], dst_ref=r_refs[k], send_sem=send_sems.at[k],
                recv_sem=recv_sems.at[k], device_id=(x, y, 1 - c), device_id_type=MESH)

        cps = [cp(k) for k in range(n)]
        for d in cps:
            d.start()
        for d in cps:
            d.wait_recv()
        for d in cps:
            d.wait_send()

    return pl.pallas_call(
        body, name="rs_sibling", in_specs=[ANY] * n, out_specs=[ANY] * n,
        out_shape=[S((N_BLK, g.shape[1] // 2, g.shape[2]), g.dtype) for g in gs],
        scratch_shapes=[pltpu.SemaphoreType.DMA((n,)), pltpu.SemaphoreType.DMA((n,))])(*gs)


def rs_chips(ps):
    n = len(ps)

    def body(*refs):
        p_refs, r_refs = refs[:n], refs[n:2 * n]
        send_sems, recv_sems, loc_sems = refs[2 * n:]
        x, y, c = _pos()
        b = 2 * x + y
        chips = _other_chips(x, y)

        def ici(k, j, src_blk, dst_slot):
            return pltpu.make_async_remote_copy(
                src_ref=p_refs[k].at[src_blk], dst_ref=r_refs[k].at[dst_slot], send_sem=send_sems.at[3 * k + j],
                recv_sem=recv_sems.at[3 * k + j], device_id=(chips[j][0], chips[j][1], c), device_id_type=MESH)

        locs = [pltpu.make_async_copy(p_refs[k].at[b], r_refs[k].at[b], loc_sems.at[k]) for k in range(n)]
        for d in locs:
            d.start()
        sends = [ici(k, j, 2 * chips[j][0] + chips[j][1], b) for k in range(n) for j in range(3)]
        for d in sends:
            d.start()
        for k in range(n):
            for j in range(3):
                ici(k, j, b, 2 * chips[j][0] + chips[j][1]).wait_recv()
        for d in locs:
            d.wait()
        for d in sends:
            d.wait_send()

    return pl.pallas_call(
        body, name="rs_chips", in_specs=[ANY] * n, out_specs=[ANY] * n,
        out_shape=[S(p.shape, p.dtype) for p in ps],
        scratch_shapes=[pltpu.SemaphoreType.DMA((3 * n,)), pltpu.SemaphoreType.DMA((3 * n,)),
                        pltpu.SemaphoreType.DMA((n,))])(*ps)


def ag_sibling(fs):
    n = len(fs)

    def body(*refs):
        o_refs = refs[n:2 * n]
        send_sems, recv_sems = refs[2 * n:]
        x, y, c = _pos()

        def cp(k, half):
            hr = fs[k].shape[0] // 2
            rows = o_refs[k].at[pl.ds(half * hr, hr)]
            return pltpu.make_async_remote_copy(
                src_ref=rows, dst_ref=rows, send_sem=send_sems.at[k], recv_sem=recv_sems.at[k],
                device_id=(x, y, 1 - c), device_id_type=MESH)

        cps = [cp(k, c) for k in range(n)]
        for d in cps:
            d.start()
        for k in range(n):
            cp(k, 1 - c).wait_recv()
        for d in cps:
            d.wait_send()

    return pl.pallas_call(
        body, name="ag_sibling", in_specs=[ANY] * n, out_specs=[ANY] * n,
        out_shape=[S(f.shape, f.dtype) for f in fs], input_output_aliases={k: k for k in range(n)},
        scratch_shapes=[pltpu.SemaphoreType.DMA((n,)), pltpu.SemaphoreType.DMA((n,))])(*fs)


def allreduce_small(v):
    rows = v.shape[0]

    def body(v_ref, o_ref, buf, send_sems, recv_sems):
        x, y, c = _pos()
        me = 4 * x + 2 * y + c
        buf[me] = v_ref[...]

        def cp(d, slot):
            dx, dy, dc = (d >> 2) & 1, (d >> 1) & 1, d & 1
            peer = (1 - x if dx else x, 1 - y if dy else y, 1 - c if dc else c)
            return pltpu.make_async_remote_copy(
                src_ref=v_ref, dst_ref=buf.at[slot], send_sem=send_sems.at[d - 1], recv_sem=recv_sems.at[d - 1],
                device_id=peer, device_id_type=MESH)

        cps = [cp(d, me) for d in range(1, N_DEV)]
        for d in cps:
            d.start()
        for d in range(1, N_DEV):
            dx, dy, dc = (d >> 2) & 1, (d >> 1) & 1, d & 1
            src = 4 * (1 - x if dx else x) + 2 * (1 - y if dy else y) + (1 - c if dc else c)
            cp(d, src).wait_recv()
        for d in cps:
            d.wait_send()
        acc = buf[0]
        for s in range(1, N_DEV):
            acc = acc + buf[s]
        o_ref[...] = acc

    vm = pl.BlockSpec(memory_space=pltpu.VMEM)
    return pl.pallas_call(
        body, name="allreduce_small", in_specs=[vm], out_specs=vm, out_shape=S(v.shape, v.dtype),
        scratch_shapes=[pltpu.VMEM((N_DEV, rows, LANES), F32), pltpu.SemaphoreType.DMA((N_DEV - 1,)),
                        pltpu.SemaphoreType.DMA((N_DEV - 1,))])(v)


def _rows_tile(rows, cols, cap_bytes=1 << 20, mult=8):
    best = None
    for t in range(mult, rows + 1, mult):
        if rows % t == 0 and t * cols * 4 <= cap_bytes:
            best = t
    return best if best is not None else rows


def add_half(name, g, r, c_arr):
    _, hr, cols = r.shape
    tr = _rows_tile(hr, cols, mult=16)
    nb = hr // tr

    def body(c_ref, g_ref, r_ref, o_ref):
        o_ref[...] = (g_ref[...] + r_ref[...]).astype(o_ref.dtype)

    gs = pltpu.PrefetchScalarGridSpec(
        num_scalar_prefetch=1, grid=(N_BLK, nb),
        in_specs=[pl.BlockSpec((None, tr, cols), lambda b, i, cr: (b, cr[0] * nb + i, 0)),
                  pl.BlockSpec((None, tr, cols), lambda b, i, cr: (b, i, 0))],
        out_specs=pl.BlockSpec((None, tr, cols), lambda b, i, cr: (b, i, 0)))
    return pl.pallas_call(body, name=name, grid_spec=gs, out_shape=S(r.shape, BF16),
                          compiler_params=_cp(("parallel", "parallel")))(c_arr, g, r)


def sum_chips(name, r, c_arr):
    _, hr, cols = r.shape
    tr = _rows_tile(hr, cols, mult=16)
    nb = hr // tr

    def body(c_ref, r_ref, o_ref):
        acc = r_ref[0].astype(F32)
        for s in range(1, N_BLK):
            acc = acc + r_ref[s].astype(F32)
        o_ref[...] = acc

    gs = pltpu.PrefetchScalarGridSpec(
        num_scalar_prefetch=1, grid=(nb,),
        in_specs=[pl.BlockSpec((N_BLK, tr, cols), lambda i, cr: (0, i, 0))],
        out_specs=pl.BlockSpec((tr, cols), lambda i, cr: (cr[0] * nb + i, 0)))
    return pl.pallas_call(body, name=name, grid_spec=gs, out_shape=S((2 * hr, cols), F32),
                          compiler_params=_cp(("parallel",)))(c_arr, r)


def adamw(name, w, m, v, gs):
    rows, cols = w.shape
    two = len(gs) == 2
    span = rows // 2 if two else rows
    tr = _rows_tile(span, cols, 1 << 19)
    nb = span // tr

    def body(w_ref, m_ref, v_ref, *rest):
        g_refs, (go_ref, d_ref, mo_ref, vo_ref) = rest[:len(gs)], rest[len(gs):]
        if two:
            g = jnp.where(pl.program_id(0) < nb, g_refs[0][...], g_refs[1][...])
        else:
            g = g_refs[0][...]
        mn = ADAM_B1 * m_ref[...] + (1.0 - ADAM_B1) * g
        vn = ADAM_B2 * v_ref[...] + (1.0 - ADAM_B2) * jnp.square(g)
        m_hat = mn / (1.0 - ADAM_B1 ** ADAM_STEP)
        v_hat = vn / (1.0 - ADAM_B2 ** ADAM_STEP)
        go_ref[...] = g
        d_ref[...] = -ADAM_LR * (m_hat / (jnp.sqrt(v_hat) + ADAM_EPS) + ADAM_WD * w_ref[...])
        mo_ref[...] = mn
        vo_ref[...] = vn

    full = pl.BlockSpec((tr, cols), lambda i: (i, 0))
    if two:
        g_specs = [pl.BlockSpec((tr, cols), lambda i: (jnp.minimum(i, nb - 1), 0)),
                   pl.BlockSpec((tr, cols), lambda i: (jnp.maximum(i - nb, 0), 0))]
    else:
        g_specs = [full]
    return pl.pallas_call(
        body, name=name, grid=(2 * nb if two else nb,), in_specs=[full, full, full] + g_specs, out_specs=[full] * 4,
        out_shape=[S((rows, cols), F32)] * 4, compiler_params=_cp(("parallel",)))(w, m, v, *gs)


WEIGHTS = ["ffn1_norm_pre", "ffn1_norm_post", "ffn1_w_in", "ffn1_w_out", "mix_norm_pre", "mix_norm_post", "mix_w_in",
           "gdn_conv_w", "gdn_a_log", "gdn_dt_bias", "gdn_norm_w", "gdn_w_o", "cnv_pw1_b", "cnv_dw_w", "cnv_dw_b",
           "cnv_ln_g", "cnv_ln_b", "cnv_w_o", "cnv_b_o", "mix_w_out", "ffn2_norm_pre", "ffn2_norm_post", "ffn2_w_in",
           "ffn2_w_out"]
BIG = {"ffn1_w_in": True, "ffn1_w_out": False, "mix_w_in": True, "gdn_conv_w": True, "gdn_w_o": False,
       "cnv_dw_w": True, "cnv_w_o": False, "mix_w_out": False, "ffn2_w_in": True, "ffn2_w_out": False}
TINY = {"gdn_conv_w": (32, LANES), "cnv_dw_w": (64, LANES)}
SMALL = [n for n in WEIGHTS if n not in BIG]


def _whole(name, blocks):
    if BIG[name]:
        return jnp.transpose(blocks, (1, 0, 2)).reshape(blocks.shape[1], N_BLK * blocks.shape[2])
    return blocks.reshape(N_BLK * blocks.shape[1], blocks.shape[2])


def _blocks(name, whole):
    r, cfull = whole.shape
    if BIG[name]:
        blk = jnp.transpose(whole.reshape(r, N_BLK, cfull // N_BLK), (1, 0, 2))
    else:
        blk = whole.reshape(N_BLK, r // N_BLK, cfull)
    if name in TINY:
        tr, tc = TINY[name]
        flat = blk.reshape(N_BLK, -1)
        blk = jnp.pad(flat, ((0, 0), (0, tr * tc - flat.shape[1]))).reshape(N_BLK, tr, tc)
    return blk


def _pack(parts):
    rows = []
    for p in parts:
        flat = p.reshape(-1).astype(F32)
        rows.append(jnp.pad(flat, (0, (-flat.shape[0]) % LANES)).reshape(-1, LANES))
    out = jnp.concatenate(rows, axis=0)
    return jnp.pad(out, ((0, (-out.shape[0]) % 8), (0, 0)))


def _unpack(packed, shapes):
    out, r = [], 0
    for shp in shapes:
        size = math.prod(shp)
        nr = -(-size // LANES)
        out.append(packed[r:r + nr].reshape(-1)[:size].reshape(shp))
        r += nr
    return out


def kernel(x, ffn1_norm_pre, ffn1_norm_post, ffn1_w_in, ffn1_w_out, mix_norm_pre, mix_norm_post, mix_w_in, gdn_conv_w, gdn_a_log, gdn_dt_bias, gdn_norm_w, gdn_w_o, cnv_pw1_b, cnv_dw_w, cnv_dw_b, cnv_ln_g, cnv_ln_b, cnv_w_o, cnv_b_o, mix_w_out, ffn2_norm_pre, ffn2_norm_post, ffn2_w_in, ffn2_w_out, loss_target, m_ffn1_norm_pre, m_ffn1_norm_post, m_ffn1_w_in, m_ffn1_w_out, m_mix_norm_pre, m_mix_norm_post, m_mix_w_in, m_gdn_conv_w, m_gdn_a_log, m_gdn_dt_bias, m_gdn_norm_w, m_gdn_w_o, m_cnv_pw1_b, m_cnv_dw_w, m_cnv_dw_b, m_cnv_ln_g, m_cnv_ln_b, m_cnv_w_o, m_cnv_b_o, m_mix_w_out, m_ffn2_norm_pre, m_ffn2_norm_post, m_ffn2_w_in, m_ffn2_w_out, v_ffn1_norm_pre, v_ffn1_norm_post, v_ffn1_w_in, v_ffn1_w_out, v_mix_norm_pre, v_mix_norm_post, v_mix_w_in, v_gdn_conv_w, v_gdn_a_log, v_gdn_dt_bias, v_gdn_norm_w, v_gdn_w_o, v_cnv_pw1_b, v_cnv_dw_w, v_cnv_dw_b, v_cnv_ln_g, v_cnv_ln_b, v_cnv_w_o, v_cnv_b_o, v_mix_w_out, v_ffn2_norm_pre, v_ffn2_norm_post, v_ffn2_w_in, v_ffn2_w_out):
    args = locals()
    wts = {n: args[n] for n in WEIGHTS}
    mom = {n: args["m_" + n] for n in WEIGHTS}
    var = {n: args["v_" + n] for n in WEIGHTS}
    big = list(BIG)

    gathered = dict(zip(big, gather_weights([wts[n].astype(BF16) for n in big])))
    layers = []
    for l in range(DEPTH):
        wl = {n: _whole(n, gathered[n][:, l]) for n in big}
        wl.update({n: wts[n][l] for n in SMALL})
        layers.append(prep_layer(wl))

    loss, dx, grads = local_step(x[0], loss_target[0], layers)
    gw = [unprep_grads(g) for g in grads]

    small_shapes = [wts[n].shape for n in SMALL]
    packed = _pack([jnp.stack([gw[l][n] for l in range(DEPTH)]) for n in SMALL] + [loss])
    total = allreduce_small(packed)
    small_g = dict(zip(SMALL, _unpack(total, small_shapes)))
    loss_sum = total[sum(-(-math.prod(s) // LANES) for s in small_shapes), 0]

    c_arr = lax.axis_index("c").astype(jnp.int32).reshape(1)
    keys = [(n, l) for n in big for l in range(DEPTH)]
    blocks = [_blocks(n, gw[l][n]) for n, l in keys]
    from_sib = rs_sibling(blocks)
    partial = [add_half("add_half", g, r, c_arr) for g, r in zip(blocks, from_sib)]
    by_chip = rs_chips(partial)
    halves = [sum_chips("sum_chips", r, c_arr) for r in by_chip]
    summed = dict(zip(keys, ag_sibling(halves)))

    out_g, out_d, out_m, out_v = {}, {}, {}, {}
    for n in big:
        shp = wts[n].shape
        gs = [summed[(n, l)] for l in range(DEPTH)]
        if n in TINY:
            gs = [jnp.concatenate([g.reshape(-1)[:shp[1] * shp[2]].reshape(shp[1], shp[2]) for g in gs], axis=0)]
        two_d = lambda a: a.reshape(DEPTH * shp[1], shp[2])
        res = adamw("adamw", two_d(wts[n]), two_d(mom[n]), two_d(var[n]), gs)
        out_g[n], out_d[n], out_m[n], out_v[n] = [r.reshape(shp) for r in res]

    pk = lambda d: _pack([d[n] for n in SMALL])
    res = adamw("adamw_small", pk(wts), pk(mom), pk(var), [pk(small_g)])
    for d, r in zip((out_g, out_d, out_m, out_v), res):
        d.update(dict(zip(SMALL, _unpack(r, small_shapes))))

    return (loss_sum, dx[None], *[out_g[n] for n in WEIGHTS], *[out_d[n] for n in WEIGHTS],
            *[out_m[n] for n in WEIGHTS], *[out_v[n] for n in WEIGHTS])
```

```python
import functools
import math

import jax
import jax.numpy as jnp
from jax import lax
from jax.experimental import pallas as pl
from jax.experimental.pallas import tpu as pltpu

F32, BF16 = jnp.float32, jnp.bfloat16
S = jax.ShapeDtypeStruct

D_MODEL = 1024
D_FF = 2816
HEADS = 8
DK = 128
CHUNK = 64
GDN_CONV = 4
CNV_K = 31
W_QKV = 3 * HEADS * DK
W_Z = HEADS * DK
W_GLU = 2 * D_MODEL
W_GATE = 2 * D_MODEL
P_IN = W_QKV + W_Z + 2 * HEADS + W_GLU + W_GATE
LANES = 128
P_ALL = W_QKV + W_Z + W_GLU + W_GATE + LANES
COL_Z = W_QKV // LANES
COL_GLU = (W_QKV + W_Z) // LANES
COL_GATE = (W_QKV + W_Z + W_GLU) // LANES
COL_BA = (W_QKV + W_Z + W_GLU + W_GATE) // LANES
RMS_EPS = 1e-6
LN_EPS = 1e-5
DEPTH = 2
N_BLK = 4
VMEM_LIMIT = 56 * 1024 * 1024

ADAM_LR, ADAM_B1, ADAM_B2, ADAM_EPS, ADAM_WD, ADAM_STEP = 0.001, 0.9, 0.999, 1e-08, 0.01, 10


def _cp(sem):
    return pltpu.CompilerParams(dimension_semantics=sem, vmem_limit_bytes=VMEM_LIMIT)


MM_VMEM_BUDGET = 36 * 1024 * 1024


def _mm_tiles(m, n, k_bytes_a, k_bytes_b, out_bytes):
    best = None
    for tm in (1024, 512, 256, 128):
        if m % tm:
            continue
        for tn in (1024, 512, 640, 256, 384, 128):
            if n % tn:
                continue
            need = 2 * (tm * k_bytes_a + tn * k_bytes_b + tm * tn * out_bytes)
            if need <= MM_VMEM_BUDGET and (best is None or tm * tn > best[0] * best[1]):
                best = (tm, tn)
    if best is None:
        raise ValueError((m, n, k_bytes_a, k_bytes_b))
    return best


def mm_nt_sum(name, parts, b):
    m, n = parts[0][0].shape[0], b.shape[0]
    k_total = sum(a.shape[1] for a, _ in parts)
    tm, tn = _mm_tiles(m, n, k_total * 2, k_total * 2, 4)
    n_p = len(parts)

    def body(*refs):
        o_ref = refs[2 * n_p]
        acc = None
        for a_ref, b_ref in zip(refs[:n_p], refs[n_p:2 * n_p]):
            t = lax.dot_general(a_ref[...], b_ref[...], (((1,), (1,)), ((), ())), preferred_element_type=F32)
            acc = t if acc is None else acc + t
        o_ref[...] = acc

    a_specs = [pl.BlockSpec((tm, a.shape[1]), lambda i, j: (i, 0)) for a, _ in parts]
    b_specs = [pl.BlockSpec((tn, a.shape[1]), functools.partial(lambda i, j, c: (j, c), c=col)) for a, col in parts]
    return pl.pallas_call(
        body, name=name, grid=(m // tm, n // tn), in_specs=a_specs + b_specs,
        out_specs=pl.BlockSpec((tm, tn), lambda i, j: (i, j)), out_shape=S((m, n), F32),
        compiler_params=_cp(("parallel", "parallel")))(*[a for a, _ in parts], *([b] * n_p))


def _take(jobs):
    return jobs.pop(0) if jobs else None


def mm(name, a, b, ta=False, tb=False, out_dtype=F32, job=None):
    if job is not None:
        return _mm_with_job(name, a, b, ta, tb, out_dtype, job)
    k = a.shape[0] if ta else a.shape[1]
    m = a.shape[1] if ta else a.shape[0]
    n = b.shape[0] if tb else b.shape[1]
    assert k == (b.shape[1] if tb else b.shape[0]), (name, a.shape, b.shape)
    tm, tn = _mm_tiles(m, n, k * a.dtype.itemsize, k * b.dtype.itemsize, jnp.dtype(out_dtype).itemsize)
    a_spec = pl.BlockSpec((k, tm), lambda i, j: (0, i)) if ta else pl.BlockSpec((tm, k), lambda i, j: (i, 0))
    b_spec = pl.BlockSpec((tn, k), lambda i, j: (j, 0)) if tb else pl.BlockSpec((k, tn), lambda i, j: (0, j))
    dims = (((0 if ta else 1,), (1 if tb else 0,)), ((), ()))

    def body(a_ref, b_ref, o_ref):
        o_ref[...] = lax.dot_general(a_ref[...], b_ref[...], dims, preferred_element_type=F32).astype(o_ref.dtype)

    return pl.pallas_call(
        body, name=name, grid=(m // tm, n // tn), in_specs=[a_spec, b_spec],
        out_specs=pl.BlockSpec((tm, tn), lambda i, j: (i, j)), out_shape=S((m, n), out_dtype),
        compiler_params=_cp(("parallel", "parallel")))(a, b)


def ew_fwd(name, fn, grid, ins, outs):
    n_in = len(ins)

    def body(*refs):
        vals = [r[...].astype(F32) for r in refs[:n_in]]
        res = fn(pl.program_id(0), *vals)
        for r, v in zip(refs[n_in:], res):
            r[...] = v.astype(r.dtype)

    out = pl.pallas_call(
        body, name=name, grid=grid, in_specs=[s for _, s in ins], out_specs=[s for _, s in outs],
        out_shape=[sd for sd, _ in outs], compiler_params=_cp(("parallel", "parallel")))(*[a for a, _ in ins])
    return out


def ew_bwd(name, fn, grid, ins, cts, wrt, acc, add=None):
    n_in, n_ct, n_wrt, n_acc = len(ins), len(cts), len(wrt), len(acc)
    has_add = add is not None

    def body(*refs):
        in_refs = refs[:n_in]
        ct_refs = refs[n_in:n_in + n_ct]
        pos = n_in + n_ct
        add_ref = refs[pos] if has_add else None
        pos += 1 if has_add else 0
        wrt_refs = refs[pos:pos + n_wrt]
        acc_refs = refs[pos + n_wrt:pos + n_wrt + n_acc]
        col, tok = pl.program_id(0), pl.program_id(1)
        vals = [r[...].astype(F32) for r in in_refs]
        _, vjp = jax.vjp(lambda *a: fn(col, *a), *vals)
        grads = vjp(tuple(c[...].astype(F32) for c in ct_refs))
        for pos_w, ((idx, _, _), r) in enumerate(zip(wrt, wrt_refs)):
            g = grads[idx]
            if has_add and pos_w == 0:
                g = g + add_ref[...]
            r[...] = g.astype(r.dtype)
        for (idx, _, _, over_cols), r in zip(acc, acc_refs):
            first = (tok == 0) & (col == 0) if over_cols else tok == 0

            @pl.when(first)
            def _():
                r[...] = jnp.zeros_like(r)

            r[...] += grads[idx]

    arrays = [a for a, _ in ins] + [a for a, _ in cts] + ([add[0]] if has_add else [])
    in_specs = [s for _, s in ins] + [s for _, s in cts] + ([add[1]] if has_add else [])
    over_any = any(o for *_, o in acc)
    out = pl.pallas_call(
        body, name=name, grid=grid, in_specs=in_specs,
        out_specs=[s for _, _, s in wrt] + [s for _, _, s, _ in acc],
        out_shape=[sd for _, sd, _ in wrt] + [sd for _, sd, _, _ in acc],
        compiler_params=_cp(("arbitrary" if over_any else "parallel", "arbitrary")))(*arrays)
    return out


def _tok(width, col=0):
    return lambda tm: pl.BlockSpec((tm, width), lambda j, i: (i, col))


def _tokcol(off=0):
    return lambda tm: pl.BlockSpec((tm, LANES), lambda j, i: (i, off + j))


def _par(width, col=0):
    return pl.BlockSpec((1, width), lambda j, i: (0, col))


def _parcol(off=0):
    return pl.BlockSpec((1, LANES), lambda j, i: (0, off + j))


def _rms(x, w, eps=RMS_EPS):
    return x * lax.rsqrt(jnp.mean(x * x, axis=-1, keepdims=True) + eps) * w


def _silu(x):
    return x * jax.nn.sigmoid(x)


def fn_rms(col, x, w):
    return (_rms(x, w),)


def fn_swiglu(col, gate, up):
    return (_silu(gate) * up,)


def swiglu_bwd(name, u, da, tm):
    n_tok, f2 = u.shape
    f = f2 // 2

    def body(g_ref, up_ref, da_ref, o_ref):
        g, d = g_ref[...].astype(F32), da_ref[...].astype(F32)
        s = jax.nn.sigmoid(g)
        o_ref[:, :f] = (d * up_ref[...].astype(F32) * (s * (1.0 + g * (1.0 - s)))).astype(o_ref.dtype)
        o_ref[:, f:] = (d * (g * s)).astype(o_ref.dtype)

    half = lambda c: pl.BlockSpec((tm, f), lambda i: (i, c))
    return pl.pallas_call(
        body, name=name, grid=(n_tok // tm,), in_specs=[half(0), half(1), half(0)],
        out_specs=pl.BlockSpec((tm, f2), lambda i: (i, 0)), out_shape=S((n_tok, f2), BF16),
        compiler_params=_cp(("parallel",)))(u, u, da)


def fn_gdnpost(col, c):
    typ = col // HEADS
    y = _silu(c)
    n = y * lax.rsqrt(jnp.sum(y * y, axis=-1, keepdims=True) + 1e-6)
    n = n * jnp.where(typ == 0, DK ** -0.5, 1.0)
    return (jnp.where(typ < 2, n, y),)


def fn_gdnout(col, o, z, nw):
    return (_rms(o, nw) * _silu(z),)


def fn_glu(col, a, g, ba, bg):
    return ((a + ba) * jax.nn.sigmoid(g + bg),)


def fn_lnsilu(col, h, g, b):
    mu = jnp.mean(h, axis=-1, keepdims=True)
    var = jnp.mean(jnp.square(h - mu), axis=-1, keepdims=True)
    return (_silu((h - mu) * lax.rsqrt(var + LN_EPS) * g + b),)


def fn_merge(col, ya, yb, ga, gb, bo):
    return (jax.nn.sigmoid(ga) * ya + jax.nn.sigmoid(gb) * (yb + bo),)


HALO = 32


def conv_fwd(name, x, col_off, n_ch, w, bias, tb):
    n_tok = x.shape[0]
    k = w.shape[0]
    nt = n_tok // tb

    def body(xp_ref, xc_ref, w_ref, *rest):
        if bias is not None:
            b_ref, o_ref, xs = rest
        else:
            o_ref, xs = rest
        i = pl.program_id(1)
        xs[0:HALO, :] = jnp.where(i == 0, 0.0, xp_ref[tb - HALO:tb, :].astype(F32))
        xs[HALO:HALO + tb, :] = xc_ref[...].astype(F32)
        acc = jnp.zeros((tb, LANES), F32)
        for j in range(k):
            s = k - 1 - j
            acc = acc + w_ref[j:j + 1, :] * xs[HALO - s:HALO - s + tb, :]
        if bias is not None:
            acc = acc + b_ref[...]
        o_ref[...] = acc

    in_specs = [pl.BlockSpec((tb, LANES), lambda j, i: (jnp.maximum(i - 1, 0), col_off + j)),
                pl.BlockSpec((tb, LANES), lambda j, i: (i, col_off + j)),
                pl.BlockSpec((k, LANES), lambda j, i: (0, j))]
    args = [x, x, w]
    if bias is not None:
        in_specs.append(pl.BlockSpec((1, LANES), lambda j, i: (0, j)))
        args.append(bias)
    return pl.pallas_call(
        body, name=name, grid=(n_ch // LANES, nt), in_specs=in_specs,
        out_specs=pl.BlockSpec((tb, LANES), lambda j, i: (i, j)), out_shape=S((n_tok, n_ch), F32),
        scratch_shapes=[pltpu.VMEM((HALO + tb, LANES), F32)],
        compiler_params=_cp(("parallel", "parallel")))(*args)


def conv_bwd(name, x, col_off, n_ch, w, dy, dx_dtype, tb):
    n_tok = x.shape[0]
    k = w.shape[0]
    nt = n_tok // tb

    def body(xp_ref, xc_ref, w_ref, dyc_ref, dyn_ref, dx_ref, dw_ref, db_ref, xs, dys):
        i = pl.program_id(1)
        xs[0:HALO, :] = jnp.where(i == 0, 0.0, xp_ref[tb - HALO:tb, :].astype(F32))
        xs[HALO:HALO + tb, :] = xc_ref[...].astype(F32)
        dyc = dyc_ref[...]
        dys[0:tb, :] = dyc
        dys[tb:tb + HALO, :] = jnp.where(i == nt - 1, 0.0, dyn_ref[0:HALO, :])

        @pl.when(i == 0)
        def _():
            dw_ref[...] = jnp.zeros_like(dw_ref)
            db_ref[...] = jnp.zeros_like(db_ref)

        acc = jnp.zeros((tb, LANES), F32)
        for j in range(k):
            s = k - 1 - j
            acc = acc + w_ref[j:j + 1, :] * dys[s:s + tb, :]
            dw_ref[j:j + 1, :] += jnp.sum(dyc * xs[HALO - s:HALO - s + tb, :], axis=0, keepdims=True)
        dx_ref[...] = acc.astype(dx_ref.dtype)
        db_ref[...] += jnp.sum(dyc, axis=0, keepdims=True)

    in_specs = [pl.BlockSpec((tb, LANES), lambda j, i: (jnp.maximum(i - 1, 0), col_off + j)),
                pl.BlockSpec((tb, LANES), lambda j, i: (i, col_off + j)),
                pl.BlockSpec((k, LANES), lambda j, i: (0, j)),
                pl.BlockSpec((tb, LANES), lambda j, i: (i, j)),
                pl.BlockSpec((tb, LANES), lambda j, i: (jnp.minimum(i + 1, nt - 1), j))]
    return pl.pallas_call(
        body, name=name, grid=(n_ch // LANES, nt), in_specs=in_specs,
        out_specs=[pl.BlockSpec((tb, LANES), lambda j, i: (i, j)),
                   pl.BlockSpec((k, LANES), lambda j, i: (0, j)),
                   pl.BlockSpec((1, LANES), lambda j, i: (0, j))],
        out_shape=[S((n_tok, n_ch), dx_dtype), S((k, n_ch), F32), S((1, n_ch), F32)],
        scratch_shapes=[pltpu.VMEM((HALO + tb, LANES), F32), pltpu.VMEM((tb + HALO, LANES), F32)],
        compiler_params=_cp(("parallel", "arbitrary")))(x, x, w, dy, dy)


def gdnconv_fwd(name, p, w, tb):
    n_tok = p.shape[0]
    k = w.shape[0]
    nt = n_tok // tb

    def body(xp_ref, xc_ref, w_ref, o_ref, xs):
        j, i = pl.program_id(0), pl.program_id(1)
        xs[0:HALO, :] = jnp.where(i == 0, 0.0, xp_ref[tb - HALO:tb, :].astype(F32))
        xs[HALO:HALO + tb, :] = xc_ref[...].astype(F32)
        c = jnp.zeros((tb, LANES), F32)
        for t in range(k):
            s = k - 1 - t
            c = c + w_ref[t:t + 1, :] * xs[HALO - s:HALO - s + tb, :]
        y = c * jax.nn.sigmoid(c)
        r = lax.rsqrt(jnp.sum(y * y, axis=-1, keepdims=True) + 1e-6) * jnp.where(j < HEADS, DK ** -0.5, 1.0)
        o_ref[...] = jnp.where(j < 2 * HEADS, y * r, y)

    return pl.pallas_call(
        body, name=name, grid=(W_QKV // LANES, nt),
        in_specs=[pl.BlockSpec((tb, LANES), lambda j, i: (jnp.maximum(i - 1, 0), j)),
                  pl.BlockSpec((tb, LANES), lambda j, i: (i, j)),
                  pl.BlockSpec((k, LANES), lambda j, i: (0, j))],
        out_specs=pl.BlockSpec((tb, LANES), lambda j, i: (i, j)), out_shape=S((n_tok, W_QKV), F32),
        scratch_shapes=[pltpu.VMEM((HALO + tb, LANES), F32)],
        compiler_params=_cp(("parallel", "parallel")))(p, p, w)


def gdnconv_bwd(name, p, w, dn, tb):
    n_tok = p.shape[0]
    k = w.shape[0]
    nt = n_tok // tb
    ext = tb + HALO

    def body(xp_ref, xc_ref, xn_ref, w_ref, dnc_ref, dnn_ref, dx_ref, dw_ref, xs, dns, dcs):
        j, i = pl.program_id(0), pl.program_id(1)
        xs[0:HALO, :] = jnp.where(i == 0, 0.0, xp_ref[tb - HALO:tb, :].astype(F32))
        xs[HALO:HALO + tb, :] = xc_ref[...].astype(F32)
        xs[HALO + tb:HALO + ext, :] = jnp.where(i == nt - 1, 0.0, xn_ref[0:HALO, :].astype(F32))
        dns[0:tb, :] = dnc_ref[...]
        dns[tb:ext, :] = jnp.where(i == nt - 1, 0.0, dnn_ref[0:HALO, :])

        @pl.when(i == 0)
        def _():
            dw_ref[...] = jnp.zeros_like(dw_ref)

        c = jnp.zeros((ext, LANES), F32)
        for t in range(k):
            s = k - 1 - t
            c = c + w_ref[t:t + 1, :] * xs[HALO - s:HALO - s + ext, :]
        d = dns[...]
        sg = jax.nn.sigmoid(c)
        y = c * sg
        r = lax.rsqrt(jnp.sum(y * y, axis=-1, keepdims=True) + 1e-6)
        scale = jnp.where(j < HEADS, DK ** -0.5, 1.0)
        dy_norm = scale * (d * r - y * (r * r * r) * jnp.sum(d * y, axis=-1, keepdims=True))
        dy = jnp.where(j < 2 * HEADS, dy_norm, d)
        dc = dy * (sg * (1.0 + c * (1.0 - sg)))
        dcs[...] = dc
        acc = jnp.zeros((tb, LANES), F32)
        for t in range(k):
            s = k - 1 - t
            acc = acc + w_ref[t:t + 1, :] * dcs[s:s + tb, :]
            dw_ref[t:t + 1, :] += jnp.sum(dcs[0:tb, :] * xs[HALO - s:HALO - s + tb, :], axis=0, keepdims=True)
        dx_ref[...] = acc.astype(dx_ref.dtype)

    cur = lambda j, i: (i, j)
    nxt = lambda j, i: (jnp.minimum(i + 1, nt - 1), j)
    return pl.pallas_call(
        body, name=name, grid=(W_QKV // LANES, nt),
        in_specs=[pl.BlockSpec((tb, LANES), lambda j, i: (jnp.maximum(i - 1, 0), j)),
                  pl.BlockSpec((tb, LANES), cur), pl.BlockSpec((tb, LANES), nxt),
                  pl.BlockSpec((k, LANES), lambda j, i: (0, j)),
                  pl.BlockSpec((tb, LANES), cur), pl.BlockSpec((tb, LANES), nxt)],
        out_specs=[pl.BlockSpec((tb, LANES), cur), pl.BlockSpec((k, LANES), lambda j, i: (0, j))],
        out_shape=[S((n_tok, W_QKV), BF16), S((k, W_QKV), F32)],
        scratch_shapes=[pltpu.VMEM((HALO + ext, LANES), F32), pltpu.VMEM((ext, LANES), F32),
                        pltpu.VMEM((ext, LANES), F32)],
        compiler_params=_cp(("parallel", "arbitrary")))(p, p, p, w, dn, dn)


GDN_GROUP = 2


def _dotb(a, b, ca, cb):
    return lax.dot_general(a.astype(BF16), b.astype(BF16), (((ca,), (cb,)), ((), ())), preferred_element_type=F32)


def _dot32(a, b, ca, cb):
    return lax.dot_general(a, b, (((ca,), (cb,)), ((), ())), preferred_element_type=F32,
                           precision=lax.Precision.HIGHEST)


def _dot3_many(xs, ys, ca, cb):
    xh = [x.astype(BF16) for x in xs]
    xl = [(x - h.astype(F32)).astype(BF16) for x, h in zip(xs, xh)]
    yh = [y.astype(BF16) for y in ys]
    yl = [(y - h.astype(F32)).astype(BF16) for y, h in zip(ys, yh)]
    dg = lambda p, q: lax.dot_general(p, q, (((ca,), (cb,)), ((), ())), preferred_element_type=F32)
    hh = [dg(p, q) for p, q in zip(xh, yh)]
    hl = [dg(p, q) for p, q in zip(xh, yl)]
    lh = [dg(p, q) for p, q in zip(xl, yh)]
    return [a + (b + c) for a, b, c in zip(hh, hl, lh)]


@jax.custom_vjp
def _mm3_many(xs, ys):
    return _dot3_many(xs, ys, 1, 0)


def _mm3_fwd(xs, ys):
    return _dot3_many(xs, ys, 1, 0), (xs, ys)


def _mm3_bwd(res, cts):
    xs, ys = res
    return _dot3_many(cts, ys, 1, 1), _dot3_many(xs, cts, 0, 0)


_mm3_many.defvjp(_mm3_fwd, _mm3_bwd)


@jax.custom_vjp
def _inv_unit_lower_many(mats):
    n = mats[0].shape[0]
    eye = (lax.broadcasted_iota(jnp.int32, (n, n), 0) == lax.broadcasted_iota(jnp.int32, (n, n), 1)).astype(F32)
    inv = [eye - a for a in mats]
    p = list(mats)
    for _ in range(int(math.log2(n)) - 1):
        p = _dot3_many(p, p, 1, 0)
        upd = _dot3_many(inv, p, 1, 0)
        inv = [i + u for i, u in zip(inv, upd)]
    return inv


def _inv_fwd(mats):
    t = _inv_unit_lower_many(mats)
    return t, t


def _inv_bwd(t, dt):
    x = _dot3_many(t, dt, 0, 0)
    return ([-y for y in _dot3_many(x, t, 1, 1)],)


_inv_unit_lower_many.defvjp(_inv_fwd, _inv_bwd)


def _softplus(x):
    return jnp.maximum(x, 0.0) + jnp.log(1.0 + jnp.exp(-jnp.abs(x)))


def _gdn_intra(qs, ks, vs, pbas, alog, dtb):
    c = pbas[0].shape[0]
    row = lax.broadcasted_iota(jnp.int32, (c, c), 0)
    colm = lax.broadcasted_iota(jnp.int32, (c, c), 1)
    causal, strict = row >= colm, row > colm
    tril = causal.astype(F32)
    lane = lax.broadcasted_iota(jnp.int32, (1, LANES), 1)
    sub = lax.broadcasted_iota(jnp.int32, (LANES, 1), 0)
    last = (lax.broadcasted_iota(jnp.int32, (c, 1), 0) == c - 1).astype(F32)
    beta_all = [jax.nn.sigmoid(pb) for pb in pbas]
    g_all = [-jnp.exp(alog) * _softplus(pb + dtb) for pb in pbas]
    gc_all = [_dot32(tril, ga, 1, 0) for ga in g_all]
    gr_all = [_dot32(ga, tril, 0, 1) for ga in g_all]
    idx = [(g, h) for g in range(len(pbas)) for h in range(HEADS)]
    beta = [jnp.sum(beta_all[g] * (lane == h).astype(F32), axis=1, keepdims=True) for g, h in idx]
    gc = [jnp.sum(gc_all[g] * (lane == HEADS + h).astype(F32), axis=1, keepdims=True) for g, h in idx]
    gr = [jnp.sum(gr_all[g] * (sub == HEADS + h).astype(F32), axis=0, keepdims=True) for g, h in idx]
    decay = [jnp.where(causal, jnp.exp(jnp.where(causal, a - b, 0.0)), 0.0) for a, b in zip(gc, gr)]
    kk = [_dotb(k, k, 1, 1) for k in ks]
    tinv = _inv_unit_lower_many([jnp.where(strict, x * d * b, 0.0) for x, d, b in zip(kk, decay, beta)])
    eg = [jnp.exp(a) for a in gc]
    g_last = [jnp.sum(a * last, axis=0, keepdims=True) for a in gc]
    us = _mm3_many(tinv, [v * b for v, b in zip(vs, beta)])
    ws = _mm3_many(tinv, [k * (b * e) for k, b, e in zip(ks, beta, eg)])
    qds = [q * e for q, e in zip(qs, eg)]
    kds = [k * jnp.exp(gl - a) for k, gl, a in zip(ks, g_last, gc)]
    qks = [_dotb(q, k, 1, 1) * d for q, k, d in zip(qs, ks, decay)]
    decs = [jnp.exp(gl) for gl in g_last]
    return us, ws, qds, kds, qks, decs


def _gdn_seq(us, ws, qds, kds, qks, decs, states):
    corr = [_dotb(w, st, 1, 0) for w, st in zip(ws, states)]
    from_state = [_dotb(qd, st, 1, 0) for qd, st in zip(qds, states)]
    v_new = [u - x for u, x in zip(us, corr)]
    intra = [_dotb(qk, vn, 1, 0) for qk, vn in zip(qks, v_new)]
    upd = [_dotb(kd, vn, 0, 0) for kd, vn in zip(kds, v_new)]
    outs = [a + b for a, b in zip(from_state, intra)]
    news = [st * d + x for st, d, x in zip(states, decs, upd)]
    return outs, news


def _heads(ref, rows=slice(None), base=0):
    return [ref[rows, (base + h) * DK:(base + h + 1) * DK].astype(F32) for h in range(HEADS)]


def _qk_heads(ref, rows=slice(None)):
    return [ref[rows, h * DK:h * DK + CHUNK].astype(F32) for h in range(HEADS)]


def _put_heads(ref, vals, rows=slice(None), base=0):
    for h in range(HEADS):
        ref[rows, (base + h) * DK:(base + h + 1) * DK] = vals[h].astype(ref.dtype)


def _put_qk(ref, vals, rows=slice(None)):
    for h in range(HEADS):
        ref[rows, h * DK:h * DK + CHUNK] = vals[h].astype(ref.dtype)
        ref[rows, h * DK + CHUNK:(h + 1) * DK] = jnp.zeros(vals[h].shape, ref.dtype)


def _group(n_chunks):
    return GDN_GROUP if n_chunks % GDN_GROUP == 0 else 1


def _decs(ref):
    return [ref[h:h + 1, 0:1] for h in range(HEADS)]


def gdn_intra_fwd(name, qkvn, p, alog, dtb):
    n_tok = qkvn.shape[0]
    n = n_tok // CHUNK
    grp = _group(n)
    hd = HEADS * DK
    rb = grp * CHUNK

    def body(q_ref, k_ref, v_ref, pba_ref, al_ref, dt_ref, u_ref, w_ref, qd_ref, kd_ref, qk_ref, dec_ref):
        rows = [slice(g * CHUNK, (g + 1) * CHUNK) for g in range(grp)]
        cat = lambda ref: [t for r in rows for t in _heads(ref, r)]
        us, ws, qds, kds, qks, decs = _gdn_intra(cat(q_ref), cat(k_ref), cat(v_ref), [pba_ref[r, :].astype(F32) for r in rows],
                                                 al_ref[...], dt_ref[...])
        for g, r in enumerate(rows):
            part = slice(g * HEADS, (g + 1) * HEADS)
            _put_heads(u_ref, us[part], r)
            _put_heads(w_ref, ws[part], r)
            _put_heads(qd_ref, qds[part], r)
            _put_heads(kd_ref, kds[part], r)
            _put_qk(qk_ref, qks[part], r)
            for h in range(HEADS):
                dec_ref[g * HEADS + h:g * HEADS + h + 1, :] = jnp.broadcast_to(decs[g * HEADS + h], (1, LANES))

    blk = lambda c: pl.BlockSpec((rb, hd), lambda i: (i, c))
    par = pl.BlockSpec((1, LANES), lambda i: (0, 0))
    return pl.pallas_call(
        body, name=name, grid=(n // grp,),
        in_specs=[blk(0), blk(1), blk(2), pl.BlockSpec((rb, LANES), lambda i: (i, COL_BA)), par, par],
        out_specs=[blk(0)] * 5 + [pl.BlockSpec((grp * HEADS, LANES), lambda i: (i, 0))],
        out_shape=[S((n_tok, hd), F32)] + [S((n_tok, hd), BF16)] * 4 + [S((n * HEADS, LANES), F32)],
        compiler_params=_cp(("parallel",)))(qkvn, qkvn, qkvn, p, alog, dtb)


def gdn_seq_fwd(name, u, w, qd, kd, qk, dec):
    n_tok = u.shape[0]
    n = n_tok // CHUNK
    hd = HEADS * DK

    def body(u_ref, w_ref, qd_ref, kd_ref, qk_ref, dec_ref, o_ref, s_ref, st):
        @pl.when(pl.program_id(0) == 0)
        def _():
            st[...] = jnp.zeros_like(st)

        s_ref[...] = st[...].astype(s_ref.dtype)
        states = [st[h * DK:(h + 1) * DK, :] for h in range(HEADS)]
        outs, news = _gdn_seq(_heads(u_ref), _heads(w_ref), _heads(qd_ref), _heads(kd_ref), _qk_heads(qk_ref),
                              _decs(dec_ref), states)
        _put_heads(o_ref, outs)
        for h in range(HEADS):
            st[h * DK:(h + 1) * DK, :] = news[h]

    blk = pl.BlockSpec((CHUNK, hd), lambda i: (i, 0))
    return pl.pallas_call(
        body, name=name, grid=(n,),
        in_specs=[blk] * 5 + [pl.BlockSpec((HEADS, LANES), lambda i: (i, 0))],
        out_specs=[blk, pl.BlockSpec((None, hd, DK), lambda i: (i, 0, 0))],
        out_shape=[S((n_tok, hd), F32), S((n, hd, DK), BF16)],
        scratch_shapes=[pltpu.VMEM((hd, DK), F32)],
        compiler_params=_cp(("arbitrary",)))(u, w, qd, kd, qk, dec)


def gdn_seq_bwd(name, u, w, qd, kd, qk, dec, states, do):
    n_tok = u.shape[0]
    n = n_tok // CHUNK
    hd = HEADS * DK

    def body(u_ref, w_ref, qd_ref, kd_ref, qk_ref, dec_ref, s_ref, do_ref,
             du_ref, dw_ref, dqd_ref, dkd_ref, dqk_ref, ddec_ref, dst):
        @pl.when(pl.program_id(0) == 0)
        def _():
            dst[...] = jnp.zeros_like(dst)

        states = [s_ref[h * DK:(h + 1) * DK, :].astype(F32) for h in range(HEADS)]
        _, vjp = jax.vjp(_gdn_seq, _heads(u_ref), _heads(w_ref), _heads(qd_ref), _heads(kd_ref), _qk_heads(qk_ref),
                         _decs(dec_ref), states)
        d_news = [dst[h * DK:(h + 1) * DK, :] for h in range(HEADS)]
        du, dw, dqd, dkd, dqk, ddec, dstates = vjp((_heads(do_ref), d_news))
        _put_heads(du_ref, du)
        _put_heads(dw_ref, dw)
        _put_heads(dqd_ref, dqd)
        _put_heads(dkd_ref, dkd)
        _put_qk(dqk_ref, dqk)
        for h in range(HEADS):
            ddec_ref[h:h + 1, :] = jnp.broadcast_to(ddec[h], (1, LANES))
            dst[h * DK:(h + 1) * DK, :] = dstates[h]

    blk = pl.BlockSpec((CHUNK, hd), lambda i: (n - 1 - i, 0))
    dspec = pl.BlockSpec((HEADS, LANES), lambda i: (n - 1 - i, 0))
    return pl.pallas_call(
        body, name=name, grid=(n,),
        in_specs=[blk] * 5 + [dspec, pl.BlockSpec((None, hd, DK), lambda i: (n - 1 - i, 0, 0)), blk],
        out_specs=[blk] * 5 + [dspec],
        out_shape=[S((n_tok, hd), F32)] * 5 + [S((n * HEADS, LANES), F32)],
        scratch_shapes=[pltpu.VMEM((hd, DK), F32)],
        compiler_params=_cp(("arbitrary",)))(u, w, qd, kd, qk, dec, states, do)


def gdn_intra_bwd(name, qkvn, p, alog, dtb, du, dw, dqd, dkd, dqk, ddec):
    n_tok = qkvn.shape[0]
    n = n_tok // CHUNK
    grp = _group(n)
    hd = HEADS * DK
    rb = grp * CHUNK

    def body(q_ref, k_ref, v_ref, pba_ref, al_ref, dt_ref, du_ref, dw_ref, dqd_ref, dkd_ref, dqk_ref, ddec_ref,
             dqkv_ref, dpba_ref, dal_ref, ddt_ref):
        @pl.when(pl.program_id(0) == 0)
        def _():
            dal_ref[...] = jnp.zeros_like(dal_ref)
            ddt_ref[...] = jnp.zeros_like(ddt_ref)

        rows = [slice(g * CHUNK, (g + 1) * CHUNK) for g in range(grp)]
        cat = lambda ref: [t for r in rows for t in _heads(ref, r)]
        _, vjp = jax.vjp(_gdn_intra, cat(q_ref), cat(k_ref), cat(v_ref), [pba_ref[r, :].astype(F32) for r in rows],
                         al_ref[...], dt_ref[...])
        cts = (cat(du_ref), cat(dw_ref), cat(dqd_ref), cat(dkd_ref), [t for r in rows for t in _qk_heads(dqk_ref, r)],
               [ddec_ref[i:i + 1, 0:1] for i in range(grp * HEADS)])
        dq, dk, dv, dpba, dal, ddt = vjp(cts)
        for g, r in enumerate(rows):
            part = slice(g * HEADS, (g + 1) * HEADS)
            _put_heads(dqkv_ref, dq[part], r, 0)
            _put_heads(dqkv_ref, dk[part], r, HEADS)
            _put_heads(dqkv_ref, dv[part], r, 2 * HEADS)
            dpba_ref[r, :] = dpba[g].astype(dpba_ref.dtype)
        dal_ref[...] += dal
        ddt_ref[...] += ddt

    blk = lambda c: pl.BlockSpec((rb, hd), lambda i: (i, c))
    par = pl.BlockSpec((1, LANES), lambda i: (0, 0))
    return pl.pallas_call(
        body, name=name, grid=(n // grp,),
        in_specs=[blk(0), blk(1), blk(2), pl.BlockSpec((rb, LANES), lambda i: (i, COL_BA)), par, par]
        + [blk(0)] * 5 + [pl.BlockSpec((grp * HEADS, LANES), lambda i: (i, 0))],
        out_specs=[pl.BlockSpec((rb, 3 * hd), lambda i: (i, 0)), pl.BlockSpec((rb, LANES), lambda i: (i, 0)), par, par],
        out_shape=[S((n_tok, 3 * hd), F32), S((n_tok, LANES), BF16), S((1, LANES), F32), S((1, LANES), F32)],
        compiler_params=_cp(("arbitrary",)))(qkvn, qkvn, qkvn, p, alog, dtb, du, dw, dqd, dkd, dqk, ddec)


def loss_head(name, y, target, tm):
    n_tok, d = y.shape

    def body(y_ref, t_ref, dy_ref, l_ref):
        @pl.when(pl.program_id(0) == 0)
        def _():
            l_ref[...] = jnp.zeros_like(l_ref)

        e = y_ref[...] - t_ref[...]
        dy_ref[...] = e * (1.0 / d)
        l_ref[...] += jnp.sum(e * e, keepdims=True) * (0.5 / d)

    spec = pl.BlockSpec((tm, d), lambda i: (i, 0))
    return pl.pallas_call(
        body, name=name, grid=(n_tok // tm,), in_specs=[spec, spec],
        out_specs=[spec, pl.BlockSpec((1, 1), lambda i: (0, 0))], out_shape=[S((n_tok, d), F32), S((1, 1), F32)],
        compiler_params=_cp(("arbitrary",)))(y, target)


def _tm(n_tok):
    return min(512, n_tok)


def ffn_fwd(tag, x, w, jobs=None):
    n_tok = x.shape[0]
    tm = _tm(n_tok)
    g1 = (1, n_tok // tm)
    tD = _tok(D_MODEL)(tm)
    (h,) = ew_fwd(tag + "_rms", fn_rms, g1, [(x, tD), (w["norm_pre"], _par(D_MODEL))], [(S((n_tok, D_MODEL), BF16), tD)])
    u = mm(tag + "_in", h, w["w_in"], out_dtype=BF16, job=_take(jobs))
    tF = lambda c: _tok(D_FF, c)(tm)
    (a,) = ew_fwd(tag + "_swiglu", fn_swiglu, g1, [(u, tF(0)), (u, tF(1))], [(S((n_tok, D_FF), BF16), tF(0))])
    f = mm(tag + "_out", a, w["w_out"], job=_take(jobs))
    fn_res = lambda col, x_, f_, w_: (x_ + 0.5 * _rms(f_, w_),)
    (xo,) = ew_fwd(tag + "_res", fn_res, g1, [(x, tD), (f, tD), (w["norm_post"], _par(D_MODEL))],
                   [(S((n_tok, D_MODEL), F32), tD)])
    return xo, dict(x=x, h=h, u=u, a=a, f=f)


def ffn_bwd(tag, dxo, sv, w, jobs=None):
    n_tok = dxo.shape[0]
    tm = _tm(n_tok)
    g1 = (1, n_tok // tm)
    tD = _tok(D_MODEL)(tm)
    pD = _par(D_MODEL)
    fn_post = lambda col, f_, w_: (0.5 * _rms(f_, w_),)
    df, d_post = ew_bwd(tag + "_res_b", fn_post, g1, [(sv["f"], tD), (w["norm_post"], pD)], [(dxo, tD)],
                        [(0, S((n_tok, D_MODEL), BF16), tD)], [(1, S((1, D_MODEL), F32), pD, False)])
    da = mm(tag + "_out_bx", df, w["w_out"], tb=True, out_dtype=BF16, job=_take(jobs))
    d_wout = mm(tag + "_out_bw", sv["a"], df, ta=True, job=_take(jobs))
    tF = lambda c: _tok(D_FF, c)(tm)
    du = swiglu_bwd(tag + "_swiglu_b", sv["u"], da, tm)
    dh = mm(tag + "_in_bx", du, w["w_in"], tb=True, job=_take(jobs))
    d_win = mm(tag + "_in_bw", sv["h"], du, ta=True, job=_take(jobs))
    dx, d_pre = ew_bwd(tag + "_rms_b", fn_rms, g1, [(sv["x"], tD), (w["norm_pre"], pD)], [(dh, tD)],
                       [(0, S((n_tok, D_MODEL), F32), tD)], [(1, S((1, D_MODEL), F32), pD, False)], add=(dxo, tD))
    return dx, dict(norm_pre=d_pre, norm_post=d_post, w_in=d_win, w_out=d_wout)


def mix_fwd(tag, x, w, jobs=None):
    n_tok = x.shape[0]
    tm = _tm(n_tok)
    nt = n_tok // tm
    g1 = (1, nt)
    tD = _tok(D_MODEL)(tm)
    pD = _par(D_MODEL)
    (h,) = ew_fwd(tag + "_rms", fn_rms, g1, [(x, tD), (w["norm_pre"], pD)], [(S((n_tok, D_MODEL), BF16), tD)])
    p = mm(tag + "_in", h, w["w_all"], out_dtype=BF16, job=_take(jobs))
    qkvn = gdnconv_fwd(tag + "_gconv", p, w["conv_w"], tm)
    tC = _tokcol()(tm)
    intra = gdn_intra_fwd(tag + "_gintra", qkvn, p, w["alog"], w["dtb"])
    o, states = gdn_seq_fwd(tag + "_gseq", *intra)
    (on,) = ew_fwd(tag + "_gout", fn_gdnout, (HEADS, nt),
                   [(o, tC), (p, _tokcol(COL_Z)(tm)), (w["gdn_norm_w"], _par(LANES))],
                   [(S((n_tok, D_MODEL), BF16), tC)])
    ya = mm(tag + "_go", on, w["gdn_w_o"], job=_take(jobs))
    (hglu,) = ew_fwd(tag + "_glu", fn_glu, (D_MODEL // LANES, nt),
                     [(p, _tokcol(COL_GLU)(tm)), (p, _tokcol(COL_GLU + D_MODEL // LANES)(tm)),
                      (w["pw1_b"], _parcol(0)), (w["pw1_b"], _parcol(D_MODEL // LANES))],
                     [(S((n_tok, D_MODEL), F32), tC)])
    hc = conv_fwd(tag + "_cconv", hglu, 0, D_MODEL, w["dw_w"], w["dw_b"], tm)
    (hs,) = ew_fwd(tag + "_ln", fn_lnsilu, g1, [(hc, tD), (w["ln_g"], pD), (w["ln_b"], pD)],
                   [(S((n_tok, D_MODEL), BF16), tD)])
    yb = mm(tag + "_co", hs, w["cnv_w_o"], job=_take(jobs))
    tG = lambda cb: pl.BlockSpec((tm, D_MODEL), lambda j, i: (i, cb))
    gcol = (W_QKV + W_Z + W_GLU) // D_MODEL
    (ym,) = ew_fwd(tag + "_merge", fn_merge, g1, [(ya, tD), (yb, tD), (p, tG(gcol)), (p, tG(gcol + 1)), (w["b_o"], pD)],
                   [(S((n_tok, D_MODEL), BF16), tD)])
    y = mm(tag + "_wo", ym, w["w_out"], job=_take(jobs))
    fn_res = lambda col, x_, f_, w_: (x_ + _rms(f_, w_),)
    (xo,) = ew_fwd(tag + "_res", fn_res, g1, [(x, tD), (y, tD), (w["norm_post"], pD)], [(S((n_tok, D_MODEL), F32), tD)])
    sv = dict(x=x, h=h, p=p, qkvn=qkvn, intra=intra, states=states, o=o, on=on, ya=ya, hglu=hglu, hc=hc, hs=hs, yb=yb, ym=ym, y=y)
    return xo, sv


def mix_bwd(tag, dxo, sv, w, jobs=None):
    n_tok = dxo.shape[0]
    tm = _tm(n_tok)
    nt = n_tok // tm
    g1 = (1, nt)
    tD = _tok(D_MODEL)(tm)
    pD = _par(D_MODEL)
    tC = _tokcol()(tm)
    p = sv["p"]
    sD = lambda dt: S((n_tok, D_MODEL), dt)
    fn_post = lambda col, f_, w_: (_rms(f_, w_),)
    dy, d_post = ew_bwd(tag + "_res_b", fn_post, g1, [(sv["y"], tD), (w["norm_post"], pD)], [(dxo, tD)],
                        [(0, sD(BF16), tD)], [(1, S((1, D_MODEL), F32), pD, False)])
    dym = mm(tag + "_wo_bx", dy, w["w_out"], tb=True, job=_take(jobs))
    d_wout = mm(tag + "_wo_bw", sv["ym"], dy, ta=True, job=_take(jobs))
    tG = lambda cb: pl.BlockSpec((tm, D_MODEL), lambda j, i: (i, cb))
    gcol = (W_QKV + W_Z + W_GLU) // D_MODEL
    dya, dyb, dga, dgb, d_bo = ew_bwd(
        tag + "_merge_b", fn_merge, g1, [(sv["ya"], tD), (sv["yb"], tD), (p, tG(gcol)), (p, tG(gcol + 1)), (w["b_o"], pD)],
        [(dym, tD)], [(0, sD(BF16), tD), (1, sD(BF16), tD), (2, sD(BF16), tD), (3, sD(BF16), tD)],
        [(4, S((1, D_MODEL), F32), pD, False)])
    dhs = mm(tag + "_co_bx", dyb, w["cnv_w_o"], tb=True, job=_take(jobs))
    d_cwo = mm(tag + "_co_bw", sv["hs"], dyb, ta=True, job=_take(jobs))
    dhc, d_lng, d_lnb = ew_bwd(tag + "_ln_b", fn_lnsilu, g1, [(sv["hc"], tD), (w["ln_g"], pD), (w["ln_b"], pD)], [(dhs, tD)],
                               [(0, sD(F32), tD)], [(1, S((1, D_MODEL), F32), pD, False), (2, S((1, D_MODEL), F32), pD, False)])
    dhglu, d_dww, d_dwb = conv_bwd(tag + "_cconv_b", sv["hglu"], 0, D_MODEL, w["dw_w"], dhc, F32, tm)
    nc = D_MODEL // LANES
    dpa, dpg, d_ba, d_bg = ew_bwd(
        tag + "_glu_b", fn_glu, (nc, nt),
        [(p, _tokcol(COL_GLU)(tm)), (p, _tokcol(COL_GLU + nc)(tm)), (w["pw1_b"], _parcol(0)), (w["pw1_b"], _parcol(nc))],
        [(dhglu, tC)], [(0, sD(BF16), tC), (1, sD(BF16), tC)],
        [(2, S((1, D_MODEL), F32), _parcol(0), False), (3, S((1, D_MODEL), F32), _parcol(0), False)])
    don = mm(tag + "_go_bx", dya, w["gdn_w_o"], tb=True, job=_take(jobs))
    d_gwo = mm(tag + "_go_bw", sv["on"], dya, ta=True, job=_take(jobs))
    do, dz, d_gnw = ew_bwd(tag + "_gout_b", fn_gdnout, (HEADS, nt),
                           [(sv["o"], tC), (p, _tokcol(COL_Z)(tm)), (w["gdn_norm_w"], _par(LANES))], [(don, tC)],
                           [(0, sD(F32), tC), (1, sD(BF16), tC)], [(2, S((1, LANES), F32), _par(LANES), True)])
    d_intra = gdn_seq_bwd(tag + "_gseq_b", *sv["intra"], sv["states"], do)
    dqkvn, dpba, d_alog, d_dtb = gdn_intra_bwd(tag + "_gintra_b", sv["qkvn"], p, w["alog"], w["dtb"], *d_intra)
    dqkv, d_convw = gdnconv_bwd(tag + "_gconv_b", p, w["conv_w"], dqkvn, tm)
    nd = D_MODEL // LANES
    pieces = [(dqkv, 0, 0), (dz, W_QKV // D_MODEL, COL_Z), (dpa, COL_GLU // nd, COL_GLU), (dpg, COL_GLU // nd + 1, COL_GLU + nd),
              (dga, COL_GATE // nd, COL_GATE), (dgb, COL_GATE // nd + 1, COL_GATE + nd), (dpba, COL_BA, COL_BA)]
    dh = mm_nt_sum(tag + "_in_bx", [(a, blk) for a, blk, _ in pieces], w["w_all"])
    d_wall = [mm(tag + "_in_bw", sv["h"], a, ta=True, job=_take(jobs)) for a, _, _ in pieces]
    dx, d_pre = ew_bwd(tag + "_rms_b", fn_rms, g1, [(sv["x"], tD), (w["norm_pre"], pD)], [(dh, tD)],
                       [(0, sD(F32), tD)], [(1, S((1, D_MODEL), F32), pD, False)], add=(dxo, tD))
    grads = dict(norm_pre=d_pre, norm_post=d_post, w_all=d_wall, conv_w=d_convw, alog=d_alog, dtb=d_dtb,
                 gdn_norm_w=d_gnw, gdn_w_o=d_gwo, pw1_b=jnp.concatenate([d_ba, d_bg], axis=1), dw_w=d_dww,
                 dw_b=d_dwb, ln_g=d_lng, ln_b=d_lnb, cnv_w_o=d_cwo, b_o=d_bo, w_out=d_wout)
    return dx, grads


def local_step(x, target, layers):
    saved = []
    for lw in layers:
        x, sv = layer_fwd(x, lw)
        saved.append(sv)
    dx, loss = loss_head("loss", x, target, _tm(x.shape[0]))
    grads = [None] * len(layers)
    for i in reversed(range(len(layers))):
        dx, grads[i] = layer_bwd(dx, saved[i], layers[i])
    return loss, dx, grads


def layer_fwd(x, lw, jobs=None):
    x, s1 = ffn_fwd("ffn", x, lw["ffn1"], jobs)
    x, s2 = mix_fwd("mix", x, lw["mix"], jobs)
    x, s3 = ffn_fwd("ffn", x, lw["ffn2"], jobs)
    return x, (s1, s2, s3)


def layer_bwd(dx, saved, lw, jobs=None):
    s1, s2, s3 = saved
    dx, g3 = ffn_bwd("ffn", dx, s3, lw["ffn2"], jobs)
    dx, g2 = mix_bwd("mix", dx, s2, lw["mix"], jobs)
    dx, g1 = ffn_bwd("ffn", dx, s1, lw["ffn1"], jobs)
    return dx, dict(ffn1=g1, mix=g2, ffn2=g3)


_O_BA = W_QKV + W_Z
_O_GLU = _O_BA + 2 * HEADS


def prep_layer(wl):
    row = lambda v: v.reshape(1, -1).astype(F32)
    bf = lambda v: v.astype(BF16)
    lanes8 = lambda v: jnp.zeros((1, LANES), F32).at[0, HEADS:2 * HEADS].set(v.astype(F32))
    mw = bf(wl["mix_w_in"])
    w_all = jnp.concatenate([mw[:, :_O_BA], mw[:, _O_GLU:], mw[:, _O_BA:_O_GLU],
                             jnp.zeros((D_MODEL, LANES - 2 * HEADS), BF16)], axis=1)
    ffn = lambda k: dict(norm_pre=row(wl[k + "_norm_pre"]), norm_post=row(wl[k + "_norm_post"]),
                         w_in=bf(wl[k + "_w_in"]), w_out=bf(wl[k + "_w_out"]))
    mix = dict(norm_pre=row(wl["mix_norm_pre"]), norm_post=row(wl["mix_norm_post"]), w_all=w_all,
               conv_w=wl["gdn_conv_w"].astype(F32), alog=lanes8(wl["gdn_a_log"]), dtb=lanes8(wl["gdn_dt_bias"]),
               gdn_norm_w=row(wl["gdn_norm_w"]), gdn_w_o=bf(wl["gdn_w_o"]), pw1_b=row(wl["cnv_pw1_b"]),
               dw_w=wl["cnv_dw_w"].astype(F32), dw_b=row(wl["cnv_dw_b"]), ln_g=row(wl["cnv_ln_g"]),
               ln_b=row(wl["cnv_ln_b"]), cnv_w_o=bf(wl["cnv_w_o"]), b_o=row(wl["cnv_b_o"]), w_out=bf(wl["mix_w_out"]))
    return dict(ffn1=ffn("ffn1"), mix=mix, ffn2=ffn("ffn2"))


def unprep_grads(g):
    m = g["mix"]
    dqkv, dz, dpa, dpg, dga, dgb, dba = m["w_all"]
    out = {}
    for k in ("ffn1", "ffn2"):
        out[k + "_norm_pre"] = g[k]["norm_pre"][0]
        out[k + "_norm_post"] = g[k]["norm_post"][0]
        out[k + "_w_in"] = g[k]["w_in"]
        out[k + "_w_out"] = g[k]["w_out"]
    out.update(
        mix_norm_pre=m["norm_pre"][0], mix_norm_post=m["norm_post"][0],
        mix_w_in=jnp.concatenate([dqkv, dz, dba[:, :2 * HEADS], dpa, dpg, dga, dgb], axis=1),
        gdn_conv_w=m["conv_w"], gdn_a_log=m["alog"][0, HEADS:2 * HEADS], gdn_dt_bias=m["dtb"][0, HEADS:2 * HEADS],
        gdn_norm_w=m["gdn_norm_w"][0], gdn_w_o=m["gdn_w_o"], cnv_pw1_b=m["pw1_b"][0], cnv_dw_w=m["dw_w"],
        cnv_dw_b=m["dw_b"][0], cnv_ln_g=m["ln_g"][0], cnv_ln_b=m["ln_b"][0], cnv_w_o=m["cnv_w_o"], cnv_b_o=m["b_o"][0],
        mix_w_out=m["w_out"])
    return out


MESH = pl.DeviceIdType.MESH
ANY = pl.BlockSpec(memory_space=pl.ANY)
N_DEV = 8


def _pos():
    return lax.axis_index("x"), lax.axis_index("y"), lax.axis_index("c")


def _other_chips(x, y):
    return [(1 - x, y), (x, 1 - y), (1 - x, 1 - y)]


class Job:
    def __init__(self, ins, outs, n_sems, copies, aliases=None):
        self.ins, self.outs, self.n_sems, self.copies = ins, outs, n_sems, copies
        self.aliases = aliases or {}
        self.results = None

    def scratch(self):
        return [pltpu.SemaphoreType.DMA((self.n_sems,)), pltpu.SemaphoreType.DMA((self.n_sems,))]

    def start(self, in_refs, out_refs, sems):
        for cp in self.copies(in_refs, out_refs, sems, False):
            cp.start()

    def finish(self, in_refs, out_refs, sems):
        for cp in self.copies(in_refs, out_refs, sems, True):
            cp.wait_recv()
        for cp in self.copies(in_refs, out_refs, sems, False):
            cp.wait_send()


def run_job(name, job):
    n_i, n_o = len(job.ins), len(job.outs)

    def body(*refs):
        in_refs, out_refs, sems = refs[:n_i], refs[n_i:n_i + n_o], refs[n_i + n_o:]
        job.start(in_refs, out_refs, sems)
        job.finish(in_refs, out_refs, sems)

    job.results = pl.pallas_call(
        body, name=name, in_specs=[ANY] * n_i, out_specs=[ANY] * n_o, out_shape=job.outs,
        input_output_aliases=job.aliases, scratch_shapes=job.scratch())(*job.ins)
    return job.results


def _mm_with_job(name, a, b, ta, tb, out_dtype, job):
    k = a.shape[0] if ta else a.shape[1]
    m = a.shape[1] if ta else a.shape[0]
    n = b.shape[0] if tb else b.shape[1]
    tm, tn = _mm_tiles(m, n, k * a.dtype.itemsize, k * b.dtype.itemsize, jnp.dtype(out_dtype).itemsize)
    a_spec = pl.BlockSpec((k, tm), lambda i, j: (0, i)) if ta else pl.BlockSpec((tm, k), lambda i, j: (i, 0))
    b_spec = pl.BlockSpec((tn, k), lambda i, j: (j, 0)) if tb else pl.BlockSpec((k, tn), lambda i, j: (0, j))
    dims = (((0 if ta else 1,), (1 if tb else 0,)), ((), ()))
    n_i, n_o = len(job.ins), len(job.outs)
    gm, gn = m // tm, n // tn

    def body(a_ref, b_ref, *rest):
        in_refs, o_ref, out_refs, sems = rest[:n_i], rest[n_i], rest[n_i + 1:n_i + 1 + n_o], rest[n_i + 1 + n_o:]
        i, j = pl.program_id(0), pl.program_id(1)

        @pl.when((i == 0) & (j == 0))
        def _():
            job.start(in_refs, out_refs, sems)

        o_ref[...] = lax.dot_general(a_ref[...], b_ref[...], dims, preferred_element_type=F32).astype(o_ref.dtype)

        @pl.when((i == gm - 1) & (j == gn - 1))
        def _():
            job.finish(in_refs, out_refs, sems)

    res = pl.pallas_call(
        body, name=name + "_c", grid=(gm, gn), in_specs=[a_spec, b_spec] + [ANY] * n_i,
        out_specs=[pl.BlockSpec((tm, tn), lambda i, j: (i, j))] + [ANY] * n_o,
        out_shape=[S((m, n), out_dtype)] + list(job.outs),
        input_output_aliases={2 + ki: 1 + ko for ki, ko in job.aliases.items()},
        scratch_shapes=job.scratch(),
        compiler_params=_cp(("arbitrary", "arbitrary")))(a, b, *job.ins)
    job.results = res[1:]
    return res[0]


def _halved(rows):
    return rows % 32 == 0


def job_gather_ici(shards):
    n = len(shards)

    def copies(in_refs, out_refs, sems, recv):
        x, y, c = _pos()
        b = 2 * x + y
        chips = _other_chips(x, y)
        cps = []
        for a in range(n):
            hr = shards[a].shape[0] // 2
            for j in range(3):
                blk = 2 * chips[j][0] + chips[j][1] if recv else b
                if _halved(shards[a].shape[0]):
                    src, dst = in_refs[a].at[pl.ds(c * hr, hr)], out_refs[a].at[blk, pl.ds(c * hr, hr)]
                else:
                    src, dst = in_refs[a], out_refs[a].at[blk]
                cps.append(pltpu.make_async_remote_copy(
                    src_ref=src, dst_ref=dst, send_sem=sems[0].at[3 * a + j], recv_sem=sems[1].at[3 * a + j],
                    device_id=(chips[j][0], chips[j][1], c), device_id_type=MESH))
        return cps

    return Job(list(shards), [S((N_BLK,) + w.shape, w.dtype) for w in shards], 3 * n, copies)


def job_gather_sibling(lands):
    idx = [a for a, w in enumerate(lands) if _halved(w.shape[1])]

    def copies(in_refs, out_refs, sems, recv):
        x, y, c = _pos()
        chips = _other_chips(x, y)
        half = 1 - c if recv else c
        cps = []
        for pos, a in enumerate(idx):
            hr = lands[a].shape[1] // 2
            for j in range(3):
                rows = out_refs[a].at[2 * chips[j][0] + chips[j][1], pl.ds(half * hr, hr)]
                cps.append(pltpu.make_async_remote_copy(
                    src_ref=rows, dst_ref=rows, send_sem=sems[0].at[3 * pos + j], recv_sem=sems[1].at[3 * pos + j],
                    device_id=(x, y, 1 - c), device_id_type=MESH))
        return cps

    return Job(list(lands), [S(w.shape, w.dtype) for w in lands], 3 * len(idx), copies,
               aliases={a: a for a in range(len(lands))})


def job_rs_chips(ps):
    n = len(ps)

    def copies(in_refs, out_refs, sems, recv):
        x, y, c = _pos()
        b = 2 * x + y
        chips = _other_chips(x, y)
        cps = []
        for k in range(n):
            for j in range(3):
                other = 2 * chips[j][0] + chips[j][1]
                src_blk, dst_slot = (b, other) if recv else (other, b)
                cps.append(pltpu.make_async_remote_copy(
                    src_ref=in_refs[k].at[src_blk], dst_ref=out_refs[k].at[dst_slot], send_sem=sems[0].at[3 * k + j],
                    recv_sem=sems[1].at[3 * k + j], device_id=(chips[j][0], chips[j][1], c), device_id_type=MESH))
        return cps

    return Job(list(ps), [S(p.shape, p.dtype) for p in ps], 3 * n, copies)


def rs_sibling(gs):
    n = len(gs)

    def body(*refs):
        g_refs, r_refs = refs[:n], refs[n:2 * n]
        send_sems, recv_sems = refs[2 * n:]
        x, y, c = _pos()

        def cp(k):
            hr = gs[k].shape[1] // 2
            return pltpu.make_async_remote_copy(
                src_ref=g_refs[k].at[:, pl.ds((1 - c) * hr, hr)], dst_ref=r_refs[k], send_sem=send_sems.at[k],
                recv_sem=recv_sems.at[k], device_id=(x, y, 1 - c), device_id_type=MESH)

        cps = [cp(k) for k in range(n)]
        for d in cps:
            d.start()
        for d in cps:
            d.wait_recv()
        for d in cps:
            d.wait_send()

    return pl.pallas_call(
        body, name="rs_sibling", in_specs=[ANY] * n, out_specs=[ANY] * n,
        out_shape=[S((N_BLK, g.shape[1] // 2, g.shape[2]), g.dtype) for g in gs],
        scratch_shapes=[pltpu.SemaphoreType.DMA((n,)), pltpu.SemaphoreType.DMA((n,))])(*gs)


def ag_sibling(fs):
    n = len(fs)

    def body(*refs):
        o_refs = refs[n:2 * n]
        send_sems, recv_sems = refs[2 * n:]
        x, y, c = _pos()

        def cp(k, half):
            hr = fs[k].shape[0] // 2
            rows = o_refs[k].at[pl.ds(half * hr, hr)]
            return pltpu.make_async_remote_copy(
                src_ref=rows, dst_ref=rows, send_sem=send_sems.at[k], recv_sem=recv_sems.at[k],
                device_id=(x, y, 1 - c), device_id_type=MESH)

        cps = [cp(k, c) for k in range(n)]
        for d in cps:
            d.start()
        for k in range(n):
            cp(k, 1 - c).wait_recv()
        for d in cps:
            d.wait_send()

    return pl.pallas_call(
        body, name="ag_sibling", in_specs=[ANY] * n, out_specs=[ANY] * n,
        out_shape=[S(f.shape, f.dtype) for f in fs], input_output_aliases={k: k for k in range(n)},
        scratch_shapes=[pltpu.SemaphoreType.DMA((n,)), pltpu.SemaphoreType.DMA((n,))])(*fs)


def allreduce_small(v):
    rows = v.shape[0]

    def body(v_ref, o_ref, buf, send_sems, recv_sems):
        x, y, c = _pos()
        me = 4 * x + 2 * y + c
        buf[me] = v_ref[...]

        def cp(d, slot):
            dx, dy, dc = (d >> 2) & 1, (d >> 1) & 1, d & 1
            peer = (1 - x if dx else x, 1 - y if dy else y, 1 - c if dc else c)
            return pltpu.make_async_remote_copy(
                src_ref=v_ref, dst_ref=buf.at[slot], send_sem=send_sems.at[d - 1], recv_sem=recv_sems.at[d - 1],
                device_id=peer, device_id_type=MESH)

        cps = [cp(d, me) for d in range(1, N_DEV)]
        for d in cps:
            d.start()
        for d in range(1, N_DEV):
            dx, dy, dc = (d >> 2) & 1, (d >> 1) & 1, d & 1
            src = 4 * (1 - x if dx else x) + 2 * (1 - y if dy else y) + (1 - c if dc else c)
            cp(d, src).wait_recv()
        for d in cps:
            d.wait_send()
        acc = buf[0]
        for s in range(1, N_DEV):
            acc = acc + buf[s]
        o_ref[...] = acc

    vm = pl.BlockSpec(memory_space=pltpu.VMEM)
    return pl.pallas_call(
        body, name="allreduce_small", in_specs=[vm], out_specs=vm, out_shape=S(v.shape, v.dtype),
        scratch_shapes=[pltpu.VMEM((N_DEV, rows, LANES), F32), pltpu.SemaphoreType.DMA((N_DEV - 1,)),
                        pltpu.SemaphoreType.DMA((N_DEV - 1,))])(v)


def _rows_tile(rows, cols, cap_bytes=1 << 20, mult=8):
    best = None
    for t in range(mult, rows + 1, mult):
        if rows % t == 0 and t * cols * 4 <= cap_bytes:
            best = t
    return best if best is not None else rows


def add_half(name, g, r, c_arr):
    _, hr, cols = r.shape
    tr = _rows_tile(hr, cols, mult=16)
    nb = hr // tr

    def body(c_ref, g_ref, r_ref, o_ref):
        o_ref[...] = (g_ref[...] + r_ref[...]).astype(o_ref.dtype)

    gs = pltpu.PrefetchScalarGridSpec(
        num_scalar_prefetch=1, grid=(N_BLK, nb),
        in_specs=[pl.BlockSpec((None, tr, cols), lambda b, i, cr: (b, cr[0] * nb + i, 0)),
                  pl.BlockSpec((None, tr, cols), lambda b, i, cr: (b, i, 0))],
        out_specs=pl.BlockSpec((None, tr, cols), lambda b, i, cr: (b, i, 0)))
    return pl.pallas_call(body, name=name, grid_spec=gs, out_shape=S(r.shape, BF16),
                          compiler_params=_cp(("parallel", "parallel")))(c_arr, g, r)


def sum_chips(name, r, own, cb_arr):
    _, hr, cols = r.shape
    tr = _rows_tile(hr, cols, mult=16)
    nb = hr // tr

    def body(cb_ref, *refs):
        o_ref = refs[N_BLK + 1]
        b = cb_ref[1]
        acc = None
        for s in range(N_BLK):
            term = jnp.where(b == s, refs[N_BLK][...], refs[s][...]).astype(F32)
            acc = term if acc is None else acc + term
        o_ref[...] = acc

    slot = lambda s: pl.BlockSpec((None, tr, cols), lambda i, cb: (jnp.where(cb[1] == s, (s + 1) % N_BLK, s), i, 0))
    gs = pltpu.PrefetchScalarGridSpec(
        num_scalar_prefetch=1, grid=(nb,),
        in_specs=[slot(s) for s in range(N_BLK)] + [pl.BlockSpec((None, tr, cols), lambda i, cb: (cb[1], i, 0))],
        out_specs=pl.BlockSpec((tr, cols), lambda i, cb: (cb[0] * nb + i, 0)))
    return pl.pallas_call(body, name=name, grid_spec=gs, out_shape=S((2 * hr, cols), F32),
                          compiler_params=_cp(("parallel",)))(cb_arr, *([r] * N_BLK), own)


def adamw(name, w, m, v, gs):
    rows, cols = w.shape
    two = len(gs) == 2
    span = rows // 2 if two else rows
    tr = _rows_tile(span, cols, 1 << 19)
    nb = span // tr

    def body(w_ref, m_ref, v_ref, *rest):
        g_refs, (go_ref, d_ref, mo_ref, vo_ref) = rest[:len(gs)], rest[len(gs):]
        if two:
            g = jnp.where(pl.program_id(0) < nb, g_refs[0][...], g_refs[1][...])
        else:
            g = g_refs[0][...]
        mn = ADAM_B1 * m_ref[...] + (1.0 - ADAM_B1) * g
        vn = ADAM_B2 * v_ref[...] + (1.0 - ADAM_B2) * jnp.square(g)
        m_hat = mn / (1.0 - ADAM_B1 ** ADAM_STEP)
        v_hat = vn / (1.0 - ADAM_B2 ** ADAM_STEP)
        go_ref[...] = g
        d_ref[...] = -ADAM_LR * (m_hat / (jnp.sqrt(v_hat) + ADAM_EPS) + ADAM_WD * w_ref[...])
        mo_ref[...] = mn
        vo_ref[...] = vn

    full = pl.BlockSpec((tr, cols), lambda i: (i, 0))
    if two:
        g_specs = [pl.BlockSpec((tr, cols), lambda i: (jnp.minimum(i, nb - 1), 0)),
                   pl.BlockSpec((tr, cols), lambda i: (jnp.maximum(i - nb, 0), 0))]
    else:
        g_specs = [full]
    return pl.pallas_call(
        body, name=name, grid=(2 * nb if two else nb,), in_specs=[full, full, full] + g_specs, out_specs=[full] * 4,
        out_shape=[S((rows, cols), F32)] * 4, compiler_params=_cp(("parallel",)))(w, m, v, *gs)


WEIGHTS = ["ffn1_norm_pre", "ffn1_norm_post", "ffn1_w_in", "ffn1_w_out", "mix_norm_pre", "mix_norm_post", "mix_w_in",
           "gdn_conv_w", "gdn_a_log", "gdn_dt_bias", "gdn_norm_w", "gdn_w_o", "cnv_pw1_b", "cnv_dw_w", "cnv_dw_b",
           "cnv_ln_g", "cnv_ln_b", "cnv_w_o", "cnv_b_o", "mix_w_out", "ffn2_norm_pre", "ffn2_norm_post", "ffn2_w_in",
           "ffn2_w_out"]
BIG = {"ffn1_w_in": True, "ffn1_w_out": False, "mix_w_in": True, "gdn_conv_w": True, "gdn_w_o": False,
       "cnv_dw_w": True, "cnv_w_o": False, "mix_w_out": False, "ffn2_w_in": True, "ffn2_w_out": False}
TINY = {"gdn_conv_w": (32, LANES), "cnv_dw_w": (64, LANES)}
SMALL = [n for n in WEIGHTS if n not in BIG]
GATHER_GROUPS = [["ffn1_w_in"], ["ffn1_w_out"], ["mix_w_in"], ["gdn_conv_w", "gdn_w_o"], ["cnv_dw_w", "cnv_w_o"],
                 ["mix_w_out"], ["ffn2_w_in"], ["ffn2_w_out"]]
RS_GROUPS = [["ffn1_w_in"], ["ffn1_w_out"], ["mix_w_in"], ["ffn2_w_in"], ["gdn_conv_w", "gdn_w_o"],
             ["cnv_dw_w", "cnv_w_o"], ["mix_w_out"], ["ffn2_w_out"]]


def _whole(name, blocks):
    if BIG[name]:
        return jnp.transpose(blocks, (1, 0, 2)).reshape(blocks.shape[1], N_BLK * blocks.shape[2])
    return blocks.reshape(N_BLK * blocks.shape[1], blocks.shape[2])


def _blocks(name, whole):
    r, cfull = whole.shape
    if BIG[name]:
        blk = jnp.transpose(whole.reshape(r, N_BLK, cfull // N_BLK), (1, 0, 2))
    else:
        blk = whole.reshape(N_BLK, r // N_BLK, cfull)
    if name in TINY:
        tr, tc = TINY[name]
        flat = blk.reshape(N_BLK, -1)
        blk = jnp.pad(flat, ((0, 0), (0, tr * tc - flat.shape[1]))).reshape(N_BLK, tr, tc)
    return blk


def _pack(parts):
    rows = []
    for p in parts:
        flat = p.reshape(-1).astype(F32)
        rows.append(jnp.pad(flat, (0, (-flat.shape[0]) % LANES)).reshape(-1, LANES))
    out = jnp.concatenate(rows, axis=0)
    return jnp.pad(out, ((0, (-out.shape[0]) % 8), (0, 0)))


def _unpack(packed, shapes):
    out, r = [], 0
    for shp in shapes:
        size = math.prod(shp)
        nr = -(-size // LANES)
        out.append(packed[r:r + nr].reshape(-1)[:size].reshape(shp))
        r += nr
    return out


def kernel(x, ffn1_norm_pre, ffn1_norm_post, ffn1_w_in, ffn1_w_out, mix_norm_pre, mix_norm_post, mix_w_in, gdn_conv_w, gdn_a_log, gdn_dt_bias, gdn_norm_w, gdn_w_o, cnv_pw1_b, cnv_dw_w, cnv_dw_b, cnv_ln_g, cnv_ln_b, cnv_w_o, cnv_b_o, mix_w_out, ffn2_norm_pre, ffn2_norm_post, ffn2_w_in, ffn2_w_out, loss_target, m_ffn1_norm_pre, m_ffn1_norm_post, m_ffn1_w_in, m_ffn1_w_out, m_mix_norm_pre, m_mix_norm_post, m_mix_w_in, m_gdn_conv_w, m_gdn_a_log, m_gdn_dt_bias, m_gdn_norm_w, m_gdn_w_o, m_cnv_pw1_b, m_cnv_dw_w, m_cnv_dw_b, m_cnv_ln_g, m_cnv_ln_b, m_cnv_w_o, m_cnv_b_o, m_mix_w_out, m_ffn2_norm_pre, m_ffn2_norm_post, m_ffn2_w_in, m_ffn2_w_out, v_ffn1_norm_pre, v_ffn1_norm_post, v_ffn1_w_in, v_ffn1_w_out, v_mix_norm_pre, v_mix_norm_post, v_mix_w_in, v_gdn_conv_w, v_gdn_a_log, v_gdn_dt_bias, v_gdn_norm_w, v_gdn_w_o, v_cnv_pw1_b, v_cnv_dw_w, v_cnv_dw_b, v_cnv_ln_g, v_cnv_ln_b, v_cnv_w_o, v_cnv_b_o, v_mix_w_out, v_ffn2_norm_pre, v_ffn2_norm_post, v_ffn2_w_in, v_ffn2_w_out):
    args = locals()
    wts = {n: args[n] for n in WEIGHTS}
    mom = {n: args["m_" + n] for n in WEIGHTS}
    var = {n: args["v_" + n] for n in WEIGHTS}
    big = list(BIG)

    mx, my, mc = _pos()
    mb = 2 * mx + my
    cb_arr = jnp.stack([mc, mb]).astype(jnp.int32)

    def my_shards(l):
        return [wts[n][l].astype(BF16) for n in big]

    def whole_layer(l, own, lands):
        wl = {}
        for n, o, land in zip(big, own, lands):
            wl[n] = _whole(n, lax.dynamic_update_index_in_dim(land, o, mb, 0))
        wl.update({n: wts[n][l] for n in SMALL})
        return prep_layer(wl)

    def grouped(make_job, arrays, groups):
        return [make_job([arrays[big.index(n)] for n in grp]) for grp in groups]

    def by_name(jobs, groups):
        res = dict(zip([n for grp in groups for n in grp], [r for jb in jobs for r in jb.results]))
        return [res[n] for n in big]

    own0, own1 = my_shards(0), my_shards(1)
    lands0 = run_job("gather0_sib", job_gather_sibling(run_job("gather0", job_gather_ici(own0))))
    layer0 = whole_layer(0, own0, lands0)
    jobs = grouped(job_gather_ici, own1, GATHER_GROUPS)
    pending = list(jobs)
    act, saved0 = layer_fwd(x[0], layer0, pending)
    for jb in pending:
        run_job("gather1_rest", jb)
    layer1 = whole_layer(1, own1, run_job("gather1_sib", job_gather_sibling(by_name(jobs, GATHER_GROUPS))))
    act, saved1 = layer_fwd(act, layer1)
    dx, loss = loss_head("loss", act, loss_target[0], _tm(act.shape[0]))

    def partials(gw_l):
        blocks = [_blocks(n, gw_l[n]) for n in big]
        return [add_half("add_half", g, r, cb_arr) for g, r in zip(blocks, rs_sibling(blocks))]

    dx, g1 = layer_bwd(dx, saved1, layer1)
    gw1 = unprep_grads(g1)
    part1 = partials(gw1)
    jobs = grouped(job_rs_chips, part1, RS_GROUPS)
    pending = list(jobs)
    dx, g0 = layer_bwd(dx, saved0, layer0, pending)
    for jb in pending:
        run_job("rs1_rest", jb)
    chip1 = by_name(jobs, RS_GROUPS)
    gw0 = unprep_grads(g0)
    part0 = partials(gw0)
    chip0 = run_job("rs0_chips", job_rs_chips(part0))
    gw = [gw0, gw1]

    small_shapes = [wts[n].shape for n in SMALL]
    packed = _pack([jnp.stack([gw[l][n] for l in range(DEPTH)]) for n in SMALL] + [loss])
    total = allreduce_small(packed)
    small_g = dict(zip(SMALL, _unpack(total, small_shapes)))
    loss_sum = total[sum(-(-math.prod(s) // LANES) for s in small_shapes), 0]

    keys = [(n, l) for l in range(DEPTH) for n in big]
    halves = [sum_chips("sum_chips", r, p, cb_arr) for r, p in zip(list(chip0) + list(chip1), part0 + part1)]
    summed = dict(zip(keys, ag_sibling(halves)))

    out_g, out_d, out_m, out_v = {}, {}, {}, {}
    for n in big:
        shp = wts[n].shape
        gs = [summed[(n, l)] for l in range(DEPTH)]
        if n in TINY:
            gs = [jnp.concatenate([g.reshape(-1)[:shp[1] * shp[2]].reshape(shp[1], shp[2]) for g in gs], axis=0)]
        two_d = lambda a: a.reshape(DEPTH * shp[1], shp[2])
        res = adamw("adamw", two_d(wts[n]), two_d(mom[n]), two_d(var[n]), gs)
        out_g[n], out_d[n], out_m[n], out_v[n] = [r.reshape(shp) for r in res]

    pk = lambda d: _pack([d[n] for n in SMALL])
    res = adamw("adamw_small", pk(wts), pk(mom), pk(var), [pk(small_g)])
    for d, r in zip((out_g, out_d, out_m, out_v), res):
        d.update(dict(zip(SMALL, _unpack(r, small_shapes))))

    return (loss_sum, dx[None], *[out_g[n] for n in WEIGHTS], *[out_d[n] for n in WEIGHTS],
            *[out_m[n] for n in WEIGHTS], *[out_v[n] for n in WEIGHTS])
```

```python
import functools
import math

import jax
import jax.numpy as jnp
from jax import lax
from jax.experimental import pallas as pl
from jax.experimental.pallas import tpu as pltpu

F32, BF16 = jnp.float32, jnp.bfloat16
S = jax.ShapeDtypeStruct

D_MODEL = 1024
D_FF = 2816
HEADS = 8
DK = 128
CHUNK = 64
GDN_CONV = 4
CNV_K = 31
W_QKV = 3 * HEADS * DK
W_Z = HEADS * DK
W_GLU = 2 * D_MODEL
W_GATE = 2 * D_MODEL
P_IN = W_QKV + W_Z + 2 * HEADS + W_GLU + W_GATE
LANES = 128
P_ALL = W_QKV + W_Z + W_GLU + W_GATE + LANES
COL_Z = W_QKV // LANES
COL_GLU = (W_QKV + W_Z) // LANES
COL_GATE = (W_QKV + W_Z + W_GLU) // LANES
COL_BA = (W_QKV + W_Z + W_GLU + W_GATE) // LANES
RMS_EPS = 1e-6
LN_EPS = 1e-5
DEPTH = 2
N_BLK = 4
VMEM_LIMIT = 56 * 1024 * 1024

ADAM_LR, ADAM_B1, ADAM_B2, ADAM_EPS, ADAM_WD, ADAM_STEP = 0.001, 0.9, 0.999, 1e-08, 0.01, 10


def _cp(sem):
    return pltpu.CompilerParams(dimension_semantics=sem, vmem_limit_bytes=VMEM_LIMIT)


MM_VMEM_BUDGET = 36 * 1024 * 1024


def _mm_tiles(m, n, k_bytes_a, k_bytes_b, out_bytes, tn_fixed=None):
    best = None
    for tm in (1024, 512, 256, 128):
        if m % tm:
            continue
        for tn in ((tn_fixed,) if tn_fixed else (1024, 512, 640, 256, 384, 128)):
            if n % tn:
                continue
            need = 2 * (tm * k_bytes_a + tn * k_bytes_b + tm * tn * out_bytes)
            if need <= MM_VMEM_BUDGET and (best is None or tm * tn > best[0] * best[1]):
                best = (tm, tn)
    if best is None:
        raise ValueError((m, n, k_bytes_a, k_bytes_b))
    return best


def mm_nt_sum(name, parts, b):
    m, n = parts[0][0].shape[0], b.shape[0]
    k_total = sum(a.shape[1] for a, _ in parts)
    tm, tn = _mm_tiles(m, n, k_total * 2, k_total * 2, 4)
    n_p = len(parts)

    def body(*refs):
        o_ref = refs[2 * n_p]
        acc = None
        for a_ref, b_ref in zip(refs[:n_p], refs[n_p:2 * n_p]):
            t = lax.dot_general(a_ref[...], b_ref[...], (((1,), (1,)), ((), ())), preferred_element_type=F32)
            acc = t if acc is None else acc + t
        o_ref[...] = acc

    a_specs = [pl.BlockSpec((tm, a.shape[1]), lambda i, j: (i, 0)) for a, _ in parts]
    b_specs = [pl.BlockSpec((tn, a.shape[1]), functools.partial(lambda i, j, c: (j, c), c=col)) for a, col in parts]
    return pl.pallas_call(
        body, name=name, grid=(m // tm, n // tn), in_specs=a_specs + b_specs,
        out_specs=pl.BlockSpec((tm, tn), lambda i, j: (i, j)), out_shape=S((m, n), F32),
        compiler_params=_cp(("parallel", "parallel")))(*[a for a, _ in parts], *([b] * n_p))


def _take(jobs):
    return jobs.pop(0) if jobs else None


def mm(name, a, b, ta=False, tb=False, out_dtype=F32, job=None, out_blocked=False):
    k = a.shape[0] if ta else a.shape[1]
    m = a.shape[1] if ta else a.shape[0]
    blocked = b.ndim == 3
    cb = b.shape[2] if blocked else None
    osz = jnp.dtype(out_dtype).itemsize
    if blocked and not tb:
        n = N_BLK * cb
        tm, tn = _mm_tiles(m, n, k * 2, k * 2, osz, tn_fixed=cb)
        b_spec = pl.BlockSpec((None, k, cb), lambda i, j: (j, 0, 0))
    elif blocked:
        n = b.shape[1]
        assert k == N_BLK * cb, (name, a.shape, b.shape)
        tm, tn = _mm_tiles(m, n, k * 2, k * 2, osz)
        b_spec = pl.BlockSpec((N_BLK, tn, cb), lambda i, j: (0, j, 0))
    else:
        n = b.shape[0] if tb else b.shape[1]
        assert k == (b.shape[1] if tb else b.shape[0]), (name, a.shape, b.shape)
        tm, tn = _mm_tiles(m, n, k * 2, k * 2, osz, tn_fixed=n // N_BLK if out_blocked else None)
        b_spec = pl.BlockSpec((tn, k), lambda i, j: (j, 0)) if tb else pl.BlockSpec((k, tn), lambda i, j: (0, j))
    a_spec = pl.BlockSpec((k, tm), lambda i, j: (0, i)) if ta else pl.BlockSpec((tm, k), lambda i, j: (i, 0))
    if out_blocked:
        o_spec, o_shape = pl.BlockSpec((None, tm, tn), lambda i, j: (j, i, 0)), S((N_BLK, m, tn), out_dtype)
    else:
        o_spec, o_shape = pl.BlockSpec((tm, tn), lambda i, j: (i, j)), S((m, n), out_dtype)
    dims = (((0 if ta else 1,), (1 if tb else 0,)), ((), ()))
    gm, gn = m // tm, n // tn

    def product(a_ref, b_ref):
        if blocked and tb:
            acc = None
            for q in range(N_BLK):
                t = lax.dot_general(a_ref[:, q * cb:(q + 1) * cb], b_ref[q], (((1,), (1,)), ((), ())),
                                    preferred_element_type=F32)
                acc = t if acc is None else acc + t
            return acc
        return lax.dot_general(a_ref[...], b_ref[...], dims, preferred_element_type=F32)

    if job is None:
        def body(a_ref, b_ref, o_ref):
            o_ref[...] = product(a_ref, b_ref).astype(o_ref.dtype)

        return pl.pallas_call(body, name=name, grid=(gm, gn), in_specs=[a_spec, b_spec], out_specs=o_spec,
                              out_shape=o_shape, compiler_params=_cp(("parallel", "parallel")))(a, b)

    n_i, n_o = len(job.ins), len(job.outs)

    def body(a_ref, b_ref, *rest):
        in_refs, o_ref, out_refs, sems = rest[:n_i], rest[n_i], rest[n_i + 1:n_i + 1 + n_o], rest[n_i + 1 + n_o:]
        i, j = pl.program_id(0), pl.program_id(1)

        @pl.when((i == 0) & (j == 0))
        def _():
            job.start(in_refs, out_refs, sems)

        o_ref[...] = product(a_ref, b_ref).astype(o_ref.dtype)

        @pl.when((i == gm - 1) & (j == gn - 1))
        def _():
            job.finish(in_refs, out_refs, sems)

    res = pl.pallas_call(
        body, name=name + "_c", grid=(gm, gn), in_specs=[a_spec, b_spec] + [ANY] * n_i,
        out_specs=[o_spec] + [ANY] * n_o, out_shape=[o_shape] + list(job.outs),
        input_output_aliases={2 + ki: 1 + ko for ki, ko in job.aliases.items()}, scratch_shapes=job.scratch(),
        compiler_params=_cp(("arbitrary", "arbitrary")))(a, b, *job.ins)
    job.results = res[1:]
    return res[0]


def ew_fwd(name, fn, grid, ins, outs):
    n_in = len(ins)

    def body(*refs):
        vals = [r[...].astype(F32) for r in refs[:n_in]]
        res = fn(pl.program_id(0), *vals)
        for r, v in zip(refs[n_in:], res):
            r[...] = v.astype(r.dtype)

    out = pl.pallas_call(
        body, name=name, grid=grid, in_specs=[s for _, s in ins], out_specs=[s for _, s in outs],
        out_shape=[sd for sd, _ in outs], compiler_params=_cp(("parallel", "parallel")))(*[a for a, _ in ins])
    return out


def ew_bwd(name, fn, grid, ins, cts, wrt, acc, add=None):
    n_in, n_ct, n_wrt, n_acc = len(ins), len(cts), len(wrt), len(acc)
    has_add = add is not None

    def body(*refs):
        in_refs = refs[:n_in]
        ct_refs = refs[n_in:n_in + n_ct]
        pos = n_in + n_ct
        add_ref = refs[pos] if has_add else None
        pos += 1 if has_add else 0
        wrt_refs = refs[pos:pos + n_wrt]
        acc_refs = refs[pos + n_wrt:pos + n_wrt + n_acc]
        col, tok = pl.program_id(0), pl.program_id(1)
        vals = [r[...].astype(F32) for r in in_refs]
        _, vjp = jax.vjp(lambda *a: fn(col, *a), *vals)
        grads = vjp(tuple(c[...].astype(F32) for c in ct_refs))
        for pos_w, ((idx, _, _), r) in enumerate(zip(wrt, wrt_refs)):
            g = grads[idx]
            if has_add and pos_w == 0:
                g = g + add_ref[...]
            r[...] = g.astype(r.dtype)
        for (idx, _, _, over_cols), r in zip(acc, acc_refs):
            first = (tok == 0) & (col == 0) if over_cols else tok == 0

            @pl.when(first)
            def _():
                r[...] = jnp.zeros_like(r)

            r[...] += grads[idx]

    arrays = [a for a, _ in ins] + [a for a, _ in cts] + ([add[0]] if has_add else [])
    in_specs = [s for _, s in ins] + [s for _, s in cts] + ([add[1]] if has_add else [])
    over_any = any(o for *_, o in acc)
    out = pl.pallas_call(
        body, name=name, grid=grid, in_specs=in_specs,
        out_specs=[s for _, _, s in wrt] + [s for _, _, s, _ in acc],
        out_shape=[sd for _, sd, _ in wrt] + [sd for _, sd, _, _ in acc],
        compiler_params=_cp(("arbitrary" if over_any else "parallel", "arbitrary")))(*arrays)
    return out


def _tok(width, col=0):
    return lambda tm: pl.BlockSpec((tm, width), lambda j, i: (i, col))


def _tokcol(off=0):
    return lambda tm: pl.BlockSpec((tm, LANES), lambda j, i: (i, off + j))


def _par(width, col=0):
    return pl.BlockSpec((1, width), lambda j, i: (0, col))


def _parcol(off=0):
    return pl.BlockSpec((1, LANES), lambda j, i: (0, off + j))


def _rms(x, w, eps=RMS_EPS):
    return x * lax.rsqrt(jnp.mean(x * x, axis=-1, keepdims=True) + eps) * w


def _silu(x):
    return x * jax.nn.sigmoid(x)


def fn_rms(col, x, w):
    return (_rms(x, w),)


def fn_swiglu(col, gate, up):
    return (_silu(gate) * up,)


def swiglu_bwd(name, u, da, tm):
    n_tok, f2 = u.shape
    f = f2 // 2

    def body(g_ref, up_ref, da_ref, o_ref):
        g, d = g_ref[...].astype(F32), da_ref[...].astype(F32)
        s = jax.nn.sigmoid(g)
        o_ref[:, :f] = (d * up_ref[...].astype(F32) * (s * (1.0 + g * (1.0 - s)))).astype(o_ref.dtype)
        o_ref[:, f:] = (d * (g * s)).astype(o_ref.dtype)

    half = lambda c: pl.BlockSpec((tm, f), lambda i: (i, c))
    return pl.pallas_call(
        body, name=name, grid=(n_tok // tm,), in_specs=[half(0), half(1), half(0)],
        out_specs=pl.BlockSpec((tm, f2), lambda i: (i, 0)), out_shape=S((n_tok, f2), BF16),
        compiler_params=_cp(("parallel",)))(u, u, da)


def fn_gdnpost(col, c):
    typ = col // HEADS
    y = _silu(c)
    n = y * lax.rsqrt(jnp.sum(y * y, axis=-1, keepdims=True) + 1e-6)
    n = n * jnp.where(typ == 0, DK ** -0.5, 1.0)
    return (jnp.where(typ < 2, n, y),)


def fn_gdnout(col, o, z, nw):
    return (_rms(o, nw) * _silu(z),)


def fn_glu(col, a, g, ba, bg):
    return ((a + ba) * jax.nn.sigmoid(g + bg),)


def fn_lnsilu(col, h, g, b):
    mu = jnp.mean(h, axis=-1, keepdims=True)
    var = jnp.mean(jnp.square(h - mu), axis=-1, keepdims=True)
    return (_silu((h - mu) * lax.rsqrt(var + LN_EPS) * g + b),)


def fn_merge(col, ya, yb, ga, gb, bo):
    return (jax.nn.sigmoid(ga) * ya + jax.nn.sigmoid(gb) * (yb + bo),)


HALO = 32


def conv_fwd(name, x, col_off, n_ch, w, bias, tb):
    n_tok = x.shape[0]
    k = w.shape[0]
    nt = n_tok // tb

    def body(xp_ref, xc_ref, w_ref, *rest):
        if bias is not None:
            b_ref, o_ref, xs = rest
        else:
            o_ref, xs = rest
        i = pl.program_id(1)
        xs[0:HALO, :] = jnp.where(i == 0, 0.0, xp_ref[tb - HALO:tb, :].astype(F32))
        xs[HALO:HALO + tb, :] = xc_ref[...].astype(F32)
        acc = jnp.zeros((tb, LANES), F32)
        for j in range(k):
            s = k - 1 - j
            acc = acc + w_ref[j:j + 1, :] * xs[HALO - s:HALO - s + tb, :]
        if bias is not None:
            acc = acc + b_ref[...]
        o_ref[...] = acc

    in_specs = [pl.BlockSpec((tb, LANES), lambda j, i: (jnp.maximum(i - 1, 0), col_off + j)),
                pl.BlockSpec((tb, LANES), lambda j, i: (i, col_off + j)),
                pl.BlockSpec((k, LANES), lambda j, i: (0, j))]
    args = [x, x, w]
    if bias is not None:
        in_specs.append(pl.BlockSpec((1, LANES), lambda j, i: (0, j)))
        args.append(bias)
    return pl.pallas_call(
        body, name=name, grid=(n_ch // LANES, nt), in_specs=in_specs,
        out_specs=pl.BlockSpec((tb, LANES), lambda j, i: (i, j)), out_shape=S((n_tok, n_ch), F32),
        scratch_shapes=[pltpu.VMEM((HALO + tb, LANES), F32)],
        compiler_params=_cp(("parallel", "parallel")))(*args)


def conv_bwd(name, x, col_off, n_ch, w, dy, dx_dtype, tb):
    n_tok = x.shape[0]
    k = w.shape[0]
    nt = n_tok // tb

    def body(xp_ref, xc_ref, w_ref, dyc_ref, dyn_ref, dx_ref, dw_ref, db_ref, xs, dys):
        i = pl.program_id(1)
        xs[0:HALO, :] = jnp.where(i == 0, 0.0, xp_ref[tb - HALO:tb, :].astype(F32))
        xs[HALO:HALO + tb, :] = xc_ref[...].astype(F32)
        dyc = dyc_ref[...]
        dys[0:tb, :] = dyc
        dys[tb:tb + HALO, :] = jnp.where(i == nt - 1, 0.0, dyn_ref[0:HALO, :])

        @pl.when(i == 0)
        def _():
            dw_ref[...] = jnp.zeros_like(dw_ref)
            db_ref[...] = jnp.zeros_like(db_ref)

        acc = jnp.zeros((tb, LANES), F32)
        for j in range(k):
            s = k - 1 - j
            acc = acc + w_ref[j:j + 1, :] * dys[s:s + tb, :]
            dw_ref[j:j + 1, :] += jnp.sum(dyc * xs[HALO - s:HALO - s + tb, :], axis=0, keepdims=True)
        dx_ref[...] = acc.astype(dx_ref.dtype)
        db_ref[...] += jnp.sum(dyc, axis=0, keepdims=True)

    in_specs = [pl.BlockSpec((tb, LANES), lambda j, i: (jnp.maximum(i - 1, 0), col_off + j)),
                pl.BlockSpec((tb, LANES), lambda j, i: (i, col_off + j)),
                pl.BlockSpec((k, LANES), lambda j, i: (0, j)),
                pl.BlockSpec((tb, LANES), lambda j, i: (i, j)),
                pl.BlockSpec((tb, LANES), lambda j, i: (jnp.minimum(i + 1, nt - 1), j))]
    return pl.pallas_call(
        body, name=name, grid=(n_ch // LANES, nt), in_specs=in_specs,
        out_specs=[pl.BlockSpec((tb, LANES), lambda j, i: (i, j)),
                   pl.BlockSpec((k, LANES), lambda j, i: (0, j)),
                   pl.BlockSpec((1, LANES), lambda j, i: (0, j))],
        out_shape=[S((n_tok, n_ch), dx_dtype), S((k, n_ch), F32), S((1, n_ch), F32)],
        scratch_shapes=[pltpu.VMEM((HALO + tb, LANES), F32), pltpu.VMEM((tb + HALO, LANES), F32)],
        compiler_params=_cp(("parallel", "arbitrary")))(x, x, w, dy, dy)


def gdnconv_fwd(name, p, w, tb):
    n_tok = p.shape[0]
    k = w.shape[0]
    nt = n_tok // tb

    def body(xp_ref, xc_ref, w_ref, o_ref, xs):
        j, i = pl.program_id(0), pl.program_id(1)
        xs[0:HALO, :] = jnp.where(i == 0, 0.0, xp_ref[tb - HALO:tb, :].astype(F32))
        xs[HALO:HALO + tb, :] = xc_ref[...].astype(F32)
        c = jnp.zeros((tb, LANES), F32)
        for t in range(k):
            s = k - 1 - t
            c = c + w_ref[t:t + 1, :] * xs[HALO - s:HALO - s + tb, :]
        y = c * jax.nn.sigmoid(c)
        r = lax.rsqrt(jnp.sum(y * y, axis=-1, keepdims=True) + 1e-6) * jnp.where(j < HEADS, DK ** -0.5, 1.0)
        o_ref[...] = jnp.where(j < 2 * HEADS, y * r, y)

    return pl.pallas_call(
        body, name=name, grid=(W_QKV // LANES, nt),
        in_specs=[pl.BlockSpec((tb, LANES), lambda j, i: (jnp.maximum(i - 1, 0), j)),
                  pl.BlockSpec((tb, LANES), lambda j, i: (i, j)),
                  pl.BlockSpec((k, LANES), lambda j, i: (0, j))],
        out_specs=pl.BlockSpec((tb, LANES), lambda j, i: (i, j)), out_shape=S((n_tok, W_QKV), F32),
        scratch_shapes=[pltpu.VMEM((HALO + tb, LANES), F32)],
        compiler_params=_cp(("parallel", "parallel")))(p, p, w)


def gdnconv_bwd(name, p, w, dn, tb):
    n_tok = p.shape[0]
    k = w.shape[0]
    nt = n_tok // tb
    ext = tb + HALO

    def body(xp_ref, xc_ref, xn_ref, w_ref, dnc_ref, dnn_ref, dx_ref, dw_ref, xs, dns, dcs):
        j, i = pl.program_id(0), pl.program_id(1)
        xs[0:HALO, :] = jnp.where(i == 0, 0.0, xp_ref[tb - HALO:tb, :].astype(F32))
        xs[HALO:HALO + tb, :] = xc_ref[...].astype(F32)
        xs[HALO + tb:HALO + ext, :] = jnp.where(i == nt - 1, 0.0, xn_ref[0:HALO, :].astype(F32))
        dns[0:tb, :] = dnc_ref[...]
        dns[tb:ext, :] = jnp.where(i == nt - 1, 0.0, dnn_ref[0:HALO, :])

        @pl.when(i == 0)
        def _():
            dw_ref[...] = jnp.zeros_like(dw_ref)

        c = jnp.zeros((ext, LANES), F32)
        for t in range(k):
            s = k - 1 - t
            c = c + w_ref[t:t + 1, :] * xs[HALO - s:HALO - s + ext, :]
        d = dns[...]
        sg = jax.nn.sigmoid(c)
        y = c * sg
        r = lax.rsqrt(jnp.sum(y * y, axis=-1, keepdims=True) + 1e-6)
        scale = jnp.where(j < HEADS, DK ** -0.5, 1.0)
        dy_norm = scale * (d * r - y * (r * r * r) * jnp.sum(d * y, axis=-1, keepdims=True))
        dy = jnp.where(j < 2 * HEADS, dy_norm, d)
        dc = dy * (sg * (1.0 + c * (1.0 - sg)))
        dcs[...] = dc
        acc = jnp.zeros((tb, LANES), F32)
        for t in range(k):
            s = k - 1 - t
            acc = acc + w_ref[t:t + 1, :] * dcs[s:s + tb, :]
            dw_ref[t:t + 1, :] += jnp.sum(dcs[0:tb, :] * xs[HALO - s:HALO - s + tb, :], axis=0, keepdims=True)
        dx_ref[...] = acc.astype(dx_ref.dtype)

    cur = lambda j, i: (i, j)
    nxt = lambda j, i: (jnp.minimum(i + 1, nt - 1), j)
    return pl.pallas_call(
        body, name=name, grid=(W_QKV // LANES, nt),
        in_specs=[pl.BlockSpec((tb, LANES), lambda j, i: (jnp.maximum(i - 1, 0), j)),
                  pl.BlockSpec((tb, LANES), cur), pl.BlockSpec((tb, LANES), nxt),
                  pl.BlockSpec((k, LANES), lambda j, i: (0, j)),
                  pl.BlockSpec((tb, LANES), cur), pl.BlockSpec((tb, LANES), nxt)],
        out_specs=[pl.BlockSpec((tb, LANES), cur), pl.BlockSpec((k, LANES), lambda j, i: (0, j))],
        out_shape=[S((n_tok, W_QKV), BF16), S((k, W_QKV), F32)],
        scratch_shapes=[pltpu.VMEM((HALO + ext, LANES), F32), pltpu.VMEM((ext, LANES), F32),
                        pltpu.VMEM((ext, LANES), F32)],
        compiler_params=_cp(("parallel", "arbitrary")))(p, p, p, w, dn, dn)


GDN_GROUP = 2


def _dotb(a, b, ca, cb):
    return lax.dot_general(a.astype(BF16), b.astype(BF16), (((ca,), (cb,)), ((), ())), preferred_element_type=F32)


def _dot32(a, b, ca, cb):
    return lax.dot_general(a, b, (((ca,), (cb,)), ((), ())), preferred_element_type=F32,
                           precision=lax.Precision.HIGHEST)


def _dot3_many(xs, ys, ca, cb):
    xh = [x.astype(BF16) for x in xs]
    xl = [(x - h.astype(F32)).astype(BF16) for x, h in zip(xs, xh)]
    yh = [y.astype(BF16) for y in ys]
    yl = [(y - h.astype(F32)).astype(BF16) for y, h in zip(ys, yh)]
    dg = lambda p, q: lax.dot_general(p, q, (((ca,), (cb,)), ((), ())), preferred_element_type=F32)
    hh = [dg(p, q) for p, q in zip(xh, yh)]
    hl = [dg(p, q) for p, q in zip(xh, yl)]
    lh = [dg(p, q) for p, q in zip(xl, yh)]
    return [a + (b + c) for a, b, c in zip(hh, hl, lh)]


@jax.custom_vjp
def _mm3_many(xs, ys):
    return _dot3_many(xs, ys, 1, 0)


def _mm3_fwd(xs, ys):
    return _dot3_many(xs, ys, 1, 0), (xs, ys)


def _mm3_bwd(res, cts):
    xs, ys = res
    return _dot3_many(cts, ys, 1, 1), _dot3_many(xs, cts, 0, 0)


_mm3_many.defvjp(_mm3_fwd, _mm3_bwd)


@jax.custom_vjp
def _inv_unit_lower_many(mats):
    n = mats[0].shape[0]
    eye = (lax.broadcasted_iota(jnp.int32, (n, n), 0) == lax.broadcasted_iota(jnp.int32, (n, n), 1)).astype(F32)
    inv = [eye - a for a in mats]
    p = list(mats)
    for _ in range(int(math.log2(n)) - 1):
        p = _dot3_many(p, p, 1, 0)
        upd = _dot3_many(inv, p, 1, 0)
        inv = [i + u for i, u in zip(inv, upd)]
    return inv


def _inv_fwd(mats):
    t = _inv_unit_lower_many(mats)
    return t, t


def _inv_bwd(t, dt):
    x = _dot3_many(t, dt, 0, 0)
    return ([-y for y in _dot3_many(x, t, 1, 1)],)


_inv_unit_lower_many.defvjp(_inv_fwd, _inv_bwd)


def _softplus(x):
    return jnp.maximum(x, 0.0) + jnp.log(1.0 + jnp.exp(-jnp.abs(x)))


def _gdn_intra(qs, ks, vs, pbas, alog, dtb):
    c = pbas[0].shape[0]
    row = lax.broadcasted_iota(jnp.int32, (c, c), 0)
    colm = lax.broadcasted_iota(jnp.int32, (c, c), 1)
    causal, strict = row >= colm, row > colm
    tril = causal.astype(F32)
    lane = lax.broadcasted_iota(jnp.int32, (1, LANES), 1)
    sub = lax.broadcasted_iota(jnp.int32, (LANES, 1), 0)
    last = (lax.broadcasted_iota(jnp.int32, (c, 1), 0) == c - 1).astype(F32)
    beta_all = [jax.nn.sigmoid(pb) for pb in pbas]
    g_all = [-jnp.exp(alog) * _softplus(pb + dtb) for pb in pbas]
    gc_all = [_dot32(tril, ga, 1, 0) for ga in g_all]
    gr_all = [_dot32(ga, tril, 0, 1) for ga in g_all]
    idx = [(g, h) for g in range(len(pbas)) for h in range(HEADS)]
    beta = [jnp.sum(beta_all[g] * (lane == h).astype(F32), axis=1, keepdims=True) for g, h in idx]
    gc = [jnp.sum(gc_all[g] * (lane == HEADS + h).astype(F32), axis=1, keepdims=True) for g, h in idx]
    gr = [jnp.sum(gr_all[g] * (sub == HEADS + h).astype(F32), axis=0, keepdims=True) for g, h in idx]
    decay = [jnp.where(causal, jnp.exp(jnp.where(causal, a - b, 0.0)), 0.0) for a, b in zip(gc, gr)]
    kk = [_dotb(k, k, 1, 1) for k in ks]
    tinv = _inv_unit_lower_many([jnp.where(strict, x * d * b, 0.0) for x, d, b in zip(kk, decay, beta)])
    eg = [jnp.exp(a) for a in gc]
    g_last = [jnp.sum(a * last, axis=0, keepdims=True) for a in gc]
    us = _mm3_many(tinv, [v * b for v, b in zip(vs, beta)])
    ws = _mm3_many(tinv, [k * (b * e) for k, b, e in zip(ks, beta, eg)])
    qds = [q * e for q, e in zip(qs, eg)]
    kds = [k * jnp.exp(gl - a) for k, gl, a in zip(ks, g_last, gc)]
    qks = [_dotb(q, k, 1, 1) * d for q, k, d in zip(qs, ks, decay)]
    decs = [jnp.exp(gl) for gl in g_last]
    return us, ws, qds, kds, qks, decs


def _gdn_seq(us, ws, qds, kds, qks, decs, states):
    corr = [_dotb(w, st, 1, 0) for w, st in zip(ws, states)]
    from_state = [_dotb(qd, st, 1, 0) for qd, st in zip(qds, states)]
    v_new = [u - x for u, x in zip(us, corr)]
    intra = [_dotb(qk, vn, 1, 0) for qk, vn in zip(qks, v_new)]
    upd = [_dotb(kd, vn, 0, 0) for kd, vn in zip(kds, v_new)]
    outs = [a + b for a, b in zip(from_state, intra)]
    news = [st * d + x for st, d, x in zip(states, decs, upd)]
    return outs, news


def _heads(ref, rows=slice(None), base=0):
    return [ref[rows, (base + h) * DK:(base + h + 1) * DK].astype(F32) for h in range(HEADS)]


def _qk_heads(ref, rows=slice(None)):
    return [ref[rows, h * DK:h * DK + CHUNK].astype(F32) for h in range(HEADS)]


def _put_heads(ref, vals, rows=slice(None), base=0):
    for h in range(HEADS):
        ref[rows, (base + h) * DK:(base + h + 1) * DK] = vals[h].astype(ref.dtype)


def _put_qk(ref, vals, rows=slice(None)):
    for h in range(HEADS):
        ref[rows, h * DK:h * DK + CHUNK] = vals[h].astype(ref.dtype)
        ref[rows, h * DK + CHUNK:(h + 1) * DK] = jnp.zeros(vals[h].shape, ref.dtype)


def _group(n_chunks):
    return GDN_GROUP if n_chunks % GDN_GROUP == 0 else 1


def _decs(ref):
    return [ref[h:h + 1, 0:1] for h in range(HEADS)]


def gdn_intra_fwd(name, qkvn, p, alog, dtb):
    n_tok = qkvn.shape[0]
    n = n_tok // CHUNK
    grp = _group(n)
    hd = HEADS * DK
    rb = grp * CHUNK

    def body(q_ref, k_ref, v_ref, pba_ref, al_ref, dt_ref, u_ref, w_ref, qd_ref, kd_ref, qk_ref, dec_ref):
        rows = [slice(g * CHUNK, (g + 1) * CHUNK) for g in range(grp)]
        cat = lambda ref: [t for r in rows for t in _heads(ref, r)]
        us, ws, qds, kds, qks, decs = _gdn_intra(cat(q_ref), cat(k_ref), cat(v_ref), [pba_ref[r, :].astype(F32) for r in rows],
                                                 al_ref[...], dt_ref[...])
        for g, r in enumerate(rows):
            part = slice(g * HEADS, (g + 1) * HEADS)
            _put_heads(u_ref, us[part], r)
            _put_heads(w_ref, ws[part], r)
            _put_heads(qd_ref, qds[part], r)
            _put_heads(kd_ref, kds[part], r)
            _put_qk(qk_ref, qks[part], r)
            for h in range(HEADS):
                dec_ref[g * HEADS + h:g * HEADS + h + 1, :] = jnp.broadcast_to(decs[g * HEADS + h], (1, LANES))

    blk = lambda c: pl.BlockSpec((rb, hd), lambda i: (i, c))
    par = pl.BlockSpec((1, LANES), lambda i: (0, 0))
    return pl.pallas_call(
        body, name=name, grid=(n // grp,),
        in_specs=[blk(0), blk(1), blk(2), pl.BlockSpec((rb, LANES), lambda i: (i, COL_BA)), par, par],
        out_specs=[blk(0)] * 5 + [pl.BlockSpec((grp * HEADS, LANES), lambda i: (i, 0))],
        out_shape=[S((n_tok, hd), F32)] + [S((n_tok, hd), BF16)] * 4 + [S((n * HEADS, LANES), F32)],
        compiler_params=_cp(("parallel",)))(qkvn, qkvn, qkvn, p, alog, dtb)


def gdn_seq_fwd(name, u, w, qd, kd, qk, dec):
    n_tok = u.shape[0]
    n = n_tok // CHUNK
    hd = HEADS * DK

    def body(u_ref, w_ref, qd_ref, kd_ref, qk_ref, dec_ref, o_ref, s_ref, st):
        @pl.when(pl.program_id(0) == 0)
        def _():
            st[...] = jnp.zeros_like(st)

        s_ref[...] = st[...].astype(s_ref.dtype)
        states = [st[h * DK:(h + 1) * DK, :] for h in range(HEADS)]
        outs, news = _gdn_seq(_heads(u_ref), _heads(w_ref), _heads(qd_ref), _heads(kd_ref), _qk_heads(qk_ref),
                              _decs(dec_ref), states)
        _put_heads(o_ref, outs)
        for h in range(HEADS):
            st[h * DK:(h + 1) * DK, :] = news[h]

    blk = pl.BlockSpec((CHUNK, hd), lambda i: (i, 0))
    return pl.pallas_call(
        body, name=name, grid=(n,),
        in_specs=[blk] * 5 + [pl.BlockSpec((HEADS, LANES), lambda i: (i, 0))],
        out_specs=[blk, pl.BlockSpec((None, hd, DK), lambda i: (i, 0, 0))],
        out_shape=[S((n_tok, hd), F32), S((n, hd, DK), BF16)],
        scratch_shapes=[pltpu.VMEM((hd, DK), F32)],
        compiler_params=_cp(("arbitrary",)))(u, w, qd, kd, qk, dec)


def gdn_seq_bwd(name, u, w, qd, kd, qk, dec, states, do):
    n_tok = u.shape[0]
    n = n_tok // CHUNK
    hd = HEADS * DK

    def body(u_ref, w_ref, qd_ref, kd_ref, qk_ref, dec_ref, s_ref, do_ref,
             du_ref, dw_ref, dqd_ref, dkd_ref, dqk_ref, ddec_ref, dst):
        @pl.when(pl.program_id(0) == 0)
        def _():
            dst[...] = jnp.zeros_like(dst)

        states = [s_ref[h * DK:(h + 1) * DK, :].astype(F32) for h in range(HEADS)]
        _, vjp = jax.vjp(_gdn_seq, _heads(u_ref), _heads(w_ref), _heads(qd_ref), _heads(kd_ref), _qk_heads(qk_ref),
                         _decs(dec_ref), states)
        d_news = [dst[h * DK:(h + 1) * DK, :] for h in range(HEADS)]
        du, dw, dqd, dkd, dqk, ddec, dstates = vjp((_heads(do_ref), d_news))
        _put_heads(du_ref, du)
        _put_heads(dw_ref, dw)
        _put_heads(dqd_ref, dqd)
        _put_heads(dkd_ref, dkd)
        _put_qk(dqk_ref, dqk)
        for h in range(HEADS):
            ddec_ref[h:h + 1, :] = jnp.broadcast_to(ddec[h], (1, LANES))
            dst[h * DK:(h + 1) * DK, :] = dstates[h]

    blk = pl.BlockSpec((CHUNK, hd), lambda i: (n - 1 - i, 0))
    dspec = pl.BlockSpec((HEADS, LANES), lambda i: (n - 1 - i, 0))
    return pl.pallas_call(
        body, name=name, grid=(n,),
        in_specs=[blk] * 5 + [dspec, pl.BlockSpec((None, hd, DK), lambda i: (n - 1 - i, 0, 0)), blk],
        out_specs=[blk] * 5 + [dspec],
        out_shape=[S((n_tok, hd), F32)] * 5 + [S((n * HEADS, LANES), F32)],
        scratch_shapes=[pltpu.VMEM((hd, DK), F32)],
        compiler_params=_cp(("arbitrary",)))(u, w, qd, kd, qk, dec, states, do)


def gdn_intra_bwd(name, qkvn, p, alog, dtb, du, dw, dqd, dkd, dqk, ddec):
    n_tok = qkvn.shape[0]
    n = n_tok // CHUNK
    grp = _group(n)
    hd = HEADS * DK
    rb = grp * CHUNK

    def body(q_ref, k_ref, v_ref, pba_ref, al_ref, dt_ref, du_ref, dw_ref, dqd_ref, dkd_ref, dqk_ref, ddec_ref,
             dqkv_ref, dpba_ref, dal_ref, ddt_ref):
        @pl.when(pl.program_id(0) == 0)
        def _():
            dal_ref[...] = jnp.zeros_like(dal_ref)
            ddt_ref[...] = jnp.zeros_like(ddt_ref)

        rows = [slice(g * CHUNK, (g + 1) * CHUNK) for g in range(grp)]
        cat = lambda ref: [t for r in rows for t in _heads(ref, r)]
        _, vjp = jax.vjp(_gdn_intra, cat(q_ref), cat(k_ref), cat(v_ref), [pba_ref[r, :].astype(F32) for r in rows],
                         al_ref[...], dt_ref[...])
        cts = (cat(du_ref), cat(dw_ref), cat(dqd_ref), cat(dkd_ref), [t for r in rows for t in _qk_heads(dqk_ref, r)],
               [ddec_ref[i:i + 1, 0:1] for i in range(grp * HEADS)])
        dq, dk, dv, dpba, dal, ddt = vjp(cts)
        for g, r in enumerate(rows):
            part = slice(g * HEADS, (g + 1) * HEADS)
            _put_heads(dqkv_ref, dq[part], r, 0)
            _put_heads(dqkv_ref, dk[part], r, HEADS)
            _put_heads(dqkv_ref, dv[part], r, 2 * HEADS)
            dpba_ref[r, :] = dpba[g].astype(dpba_ref.dtype)
        dal_ref[...] += dal
        ddt_ref[...] += ddt

    blk = lambda c: pl.BlockSpec((rb, hd), lambda i: (i, c))
    par = pl.BlockSpec((1, LANES), lambda i: (0, 0))
    return pl.pallas_call(
        body, name=name, grid=(n // grp,),
        in_specs=[blk(0), blk(1), blk(2), pl.BlockSpec((rb, LANES), lambda i: (i, COL_BA)), par, par]
        + [blk(0)] * 5 + [pl.BlockSpec((grp * HEADS, LANES), lambda i: (i, 0))],
        out_specs=[pl.BlockSpec((rb, 3 * hd), lambda i: (i, 0)), pl.BlockSpec((rb, LANES), lambda i: (i, 0)), par, par],
        out_shape=[S((n_tok, 3 * hd), F32), S((n_tok, LANES), BF16), S((1, LANES), F32), S((1, LANES), F32)],
        compiler_params=_cp(("arbitrary",)))(qkvn, qkvn, qkvn, p, alog, dtb, du, dw, dqd, dkd, dqk, ddec)


def loss_head(name, y, target, tm):
    n_tok, d = y.shape

    def body(y_ref, t_ref, dy_ref, l_ref):
        @pl.when(pl.program_id(0) == 0)
        def _():
            l_ref[...] = jnp.zeros_like(l_ref)

        e = y_ref[...] - t_ref[...]
        dy_ref[...] = e * (1.0 / d)
        l_ref[...] += jnp.sum(e * e, keepdims=True) * (0.5 / d)

    spec = pl.BlockSpec((tm, d), lambda i: (i, 0))
    return pl.pallas_call(
        body, name=name, grid=(n_tok // tm,), in_specs=[spec, spec],
        out_specs=[spec, pl.BlockSpec((1, 1), lambda i: (0, 0))], out_shape=[S((n_tok, d), F32), S((1, 1), F32)],
        compiler_params=_cp(("arbitrary",)))(y, target)


def _tm(n_tok):
    return min(512, n_tok)


def ffn_fwd(tag, x, w, jobs=None):
    n_tok = x.shape[0]
    tm = _tm(n_tok)
    g1 = (1, n_tok // tm)
    tD = _tok(D_MODEL)(tm)
    (h,) = ew_fwd(tag + "_rms", fn_rms, g1, [(x, tD), (w["norm_pre"], _par(D_MODEL))], [(S((n_tok, D_MODEL), BF16), tD)])
    u = mm(tag + "_in", h, w["w_in"], out_dtype=BF16, job=_take(jobs))
    tF = lambda c: _tok(D_FF, c)(tm)
    (a,) = ew_fwd(tag + "_swiglu", fn_swiglu, g1, [(u, tF(0)), (u, tF(1))], [(S((n_tok, D_FF), BF16), tF(0))])
    f = mm(tag + "_out", a, w["w_out"], job=_take(jobs))
    fn_res = lambda col, x_, f_, w_: (x_ + 0.5 * _rms(f_, w_),)
    (xo,) = ew_fwd(tag + "_res", fn_res, g1, [(x, tD), (f, tD), (w["norm_post"], _par(D_MODEL))],
                   [(S((n_tok, D_MODEL), F32), tD)])
    return xo, dict(x=x, h=h, u=u, a=a, f=f)


def ffn_bwd(tag, dxo, sv, w, jobs=None):
    n_tok = dxo.shape[0]
    tm = _tm(n_tok)
    g1 = (1, n_tok // tm)
    tD = _tok(D_MODEL)(tm)
    pD = _par(D_MODEL)
    fn_post = lambda col, f_, w_: (0.5 * _rms(f_, w_),)
    df, d_post = ew_bwd(tag + "_res_b", fn_post, g1, [(sv["f"], tD), (w["norm_post"], pD)], [(dxo, tD)],
                        [(0, S((n_tok, D_MODEL), BF16), tD)], [(1, S((1, D_MODEL), F32), pD, False)])
    da = mm(tag + "_out_bx", df, w["w_out"], tb=True, out_dtype=BF16, job=_take(jobs))
    d_wout = mm(tag + "_out_bw", sv["a"], df, ta=True, job=_take(jobs))
    tF = lambda c: _tok(D_FF, c)(tm)
    du = swiglu_bwd(tag + "_swiglu_b", sv["u"], da, tm)
    dh = mm(tag + "_in_bx", du, w["w_in"], tb=True, job=_take(jobs))
    d_win = mm(tag + "_in_bw", sv["h"], du, ta=True, job=_take(jobs), out_blocked=True)
    dx, d_pre = ew_bwd(tag + "_rms_b", fn_rms, g1, [(sv["x"], tD), (w["norm_pre"], pD)], [(dh, tD)],
                       [(0, S((n_tok, D_MODEL), F32), tD)], [(1, S((1, D_MODEL), F32), pD, False)], add=(dxo, tD))
    return dx, dict(norm_pre=d_pre, norm_post=d_post, w_in=d_win, w_out=d_wout)


def mix_fwd(tag, x, w, jobs=None):
    n_tok = x.shape[0]
    tm = _tm(n_tok)
    nt = n_tok // tm
    g1 = (1, nt)
    tD = _tok(D_MODEL)(tm)
    pD = _par(D_MODEL)
    (h,) = ew_fwd(tag + "_rms", fn_rms, g1, [(x, tD), (w["norm_pre"], pD)], [(S((n_tok, D_MODEL), BF16), tD)])
    p = mm(tag + "_in", h, w["w_all"], out_dtype=BF16, job=_take(jobs))
    qkvn = gdnconv_fwd(tag + "_gconv", p, w["conv_w"], tm)
    tC = _tokcol()(tm)
    intra = gdn_intra_fwd(tag + "_gintra", qkvn, p, w["alog"], w["dtb"])
    o, states = gdn_seq_fwd(tag + "_gseq", *intra)
    (on,) = ew_fwd(tag + "_gout", fn_gdnout, (HEADS, nt),
                   [(o, tC), (p, _tokcol(COL_Z)(tm)), (w["gdn_norm_w"], _par(LANES))],
                   [(S((n_tok, D_MODEL), BF16), tC)])
    ya = mm(tag + "_go", on, w["gdn_w_o"], job=_take(jobs))
    (hglu,) = ew_fwd(tag + "_glu", fn_glu, (D_MODEL // LANES, nt),
                     [(p, _tokcol(COL_GLU)(tm)), (p, _tokcol(COL_GLU + D_MODEL // LANES)(tm)),
                      (w["pw1_b"], _parcol(0)), (w["pw1_b"], _parcol(D_MODEL // LANES))],
                     [(S((n_tok, D_MODEL), F32), tC)])
    hc = conv_fwd(tag + "_cconv", hglu, 0, D_MODEL, w["dw_w"], w["dw_b"], tm)
    (hs,) = ew_fwd(tag + "_ln", fn_lnsilu, g1, [(hc, tD), (w["ln_g"], pD), (w["ln_b"], pD)],
                   [(S((n_tok, D_MODEL), BF16), tD)])
    yb = mm(tag + "_co", hs, w["cnv_w_o"], job=_take(jobs))
    tG = lambda cb: pl.BlockSpec((tm, D_MODEL), lambda j, i: (i, cb))
    gcol = (W_QKV + W_Z + W_GLU) // D_MODEL
    (ym,) = ew_fwd(tag + "_merge", fn_merge, g1, [(ya, tD), (yb, tD), (p, tG(gcol)), (p, tG(gcol + 1)), (w["b_o"], pD)],
                   [(S((n_tok, D_MODEL), BF16), tD)])
    y = mm(tag + "_wo", ym, w["w_out"], job=_take(jobs))
    fn_res = lambda col, x_, f_, w_: (x_ + _rms(f_, w_),)
    (xo,) = ew_fwd(tag + "_res", fn_res, g1, [(x, tD), (y, tD), (w["norm_post"], pD)], [(S((n_tok, D_MODEL), F32), tD)])
    sv = dict(x=x, h=h, p=p, qkvn=qkvn, intra=intra, states=states, o=o, on=on, ya=ya, hglu=hglu, hc=hc, hs=hs, yb=yb, ym=ym, y=y)
    return xo, sv


def mix_bwd(tag, dxo, sv, w, jobs=None):
    n_tok = dxo.shape[0]
    tm = _tm(n_tok)
    nt = n_tok // tm
    g1 = (1, nt)
    tD = _tok(D_MODEL)(tm)
    pD = _par(D_MODEL)
    tC = _tokcol()(tm)
    p = sv["p"]
    sD = lambda dt: S((n_tok, D_MODEL), dt)
    fn_post = lambda col, f_, w_: (_rms(f_, w_),)
    dy, d_post = ew_bwd(tag + "_res_b", fn_post, g1, [(sv["y"], tD), (w["norm_post"], pD)], [(dxo, tD)],
                        [(0, sD(BF16), tD)], [(1, S((1, D_MODEL), F32), pD, False)])
    dym = mm(tag + "_wo_bx", dy, w["w_out"], tb=True, job=_take(jobs))
    d_wout = mm(tag + "_wo_bw", sv["ym"], dy, ta=True, job=_take(jobs))
    tG = lambda cb: pl.BlockSpec((tm, D_MODEL), lambda j, i: (i, cb))
    gcol = (W_QKV + W_Z + W_GLU) // D_MODEL
    dya, dyb, dga, dgb, d_bo = ew_bwd(
        tag + "_merge_b", fn_merge, g1, [(sv["ya"], tD), (sv["yb"], tD), (p, tG(gcol)), (p, tG(gcol + 1)), (w["b_o"], pD)],
        [(dym, tD)], [(0, sD(BF16), tD), (1, sD(BF16), tD), (2, sD(BF16), tD), (3, sD(BF16), tD)],
        [(4, S((1, D_MODEL), F32), pD, False)])
    dhs = mm(tag + "_co_bx", dyb, w["cnv_w_o"], tb=True, job=_take(jobs))
    d_cwo = mm(tag + "_co_bw", sv["hs"], dyb, ta=True, job=_take(jobs))
    dhc, d_lng, d_lnb = ew_bwd(tag + "_ln_b", fn_lnsilu, g1, [(sv["hc"], tD), (w["ln_g"], pD), (w["ln_b"], pD)], [(dhs, tD)],
                               [(0, sD(F32), tD)], [(1, S((1, D_MODEL), F32), pD, False), (2, S((1, D_MODEL), F32), pD, False)])
    dhglu, d_dww, d_dwb = conv_bwd(tag + "_cconv_b", sv["hglu"], 0, D_MODEL, w["dw_w"], dhc, F32, tm)
    nc = D_MODEL // LANES
    dpa, dpg, d_ba, d_bg = ew_bwd(
        tag + "_glu_b", fn_glu, (nc, nt),
        [(p, _tokcol(COL_GLU)(tm)), (p, _tokcol(COL_GLU + nc)(tm)), (w["pw1_b"], _parcol(0)), (w["pw1_b"], _parcol(nc))],
        [(dhglu, tC)], [(0, sD(BF16), tC), (1, sD(BF16), tC)],
        [(2, S((1, D_MODEL), F32), _parcol(0), False), (3, S((1, D_MODEL), F32), _parcol(0), False)])
    don = mm(tag + "_go_bx", dya, w["gdn_w_o"], tb=True, job=_take(jobs))
    d_gwo = mm(tag + "_go_bw", sv["on"], dya, ta=True, job=_take(jobs))
    do, dz, d_gnw = ew_bwd(tag + "_gout_b", fn_gdnout, (HEADS, nt),
                           [(sv["o"], tC), (p, _tokcol(COL_Z)(tm)), (w["gdn_norm_w"], _par(LANES))], [(don, tC)],
                           [(0, sD(F32), tC), (1, sD(BF16), tC)], [(2, S((1, LANES), F32), _par(LANES), True)])
    d_intra = gdn_seq_bwd(tag + "_gseq_b", *sv["intra"], sv["states"], do)
    dqkvn, dpba, d_alog, d_dtb = gdn_intra_bwd(tag + "_gintra_b", sv["qkvn"], p, w["alog"], w["dtb"], *d_intra)
    dqkv, d_convw = gdnconv_bwd(tag + "_gconv_b", p, w["conv_w"], dqkvn, tm)
    nd = D_MODEL // LANES
    pieces = [(dqkv, 0, 0), (dz, W_QKV // D_MODEL, COL_Z), (dpa, COL_GLU // nd, COL_GLU), (dpg, COL_GLU // nd + 1, COL_GLU + nd),
              (dga, COL_GATE // nd, COL_GATE), (dgb, COL_GATE // nd + 1, COL_GATE + nd), (dpba, COL_BA, COL_BA)]
    dh = mm_nt_sum(tag + "_in_bx", [(a, blk) for a, blk, _ in pieces], w["w_all"])
    d_wall = [mm(tag + "_in_bw", sv["h"], a, ta=True, job=_take(jobs)) for a, _, _ in pieces]
    dx, d_pre = ew_bwd(tag + "_rms_b", fn_rms, g1, [(sv["x"], tD), (w["norm_pre"], pD)], [(dh, tD)],
                       [(0, sD(F32), tD)], [(1, S((1, D_MODEL), F32), pD, False)], add=(dxo, tD))
    grads = dict(norm_pre=d_pre, norm_post=d_post, w_all=d_wall, conv_w=d_convw, alog=d_alog, dtb=d_dtb,
                 gdn_norm_w=d_gnw, gdn_w_o=d_gwo, pw1_b=jnp.concatenate([d_ba, d_bg], axis=1), dw_w=d_dww,
                 dw_b=d_dwb, ln_g=d_lng, ln_b=d_lnb, cnv_w_o=d_cwo, b_o=d_bo, w_out=d_wout)
    return dx, grads


def local_step(x, target, layers):
    saved = []
    for lw in layers:
        x, sv = layer_fwd(x, lw)
        saved.append(sv)
    dx, loss = loss_head("loss", x, target, _tm(x.shape[0]))
    grads = [None] * len(layers)
    for i in reversed(range(len(layers))):
        dx, grads[i] = layer_bwd(dx, saved[i], layers[i])
    return loss, dx, grads


def layer_fwd(x, lw, jobs=None):
    x, s1 = ffn_fwd("ffn", x, lw["ffn1"], jobs)
    x, s2 = mix_fwd("mix", x, lw["mix"], jobs)
    x, s3 = ffn_fwd("ffn", x, lw["ffn2"], jobs)
    return x, (s1, s2, s3)


def layer_bwd(dx, saved, lw, jobs=None):
    s1, s2, s3 = saved
    dx, g3 = ffn_bwd("ffn", dx, s3, lw["ffn2"], jobs)
    dx, g2 = mix_bwd("mix", dx, s2, lw["mix"], jobs)
    dx, g1 = ffn_bwd("ffn", dx, s1, lw["ffn1"], jobs)
    return dx, dict(ffn1=g1, mix=g2, ffn2=g3)


_O_BA = W_QKV + W_Z
_O_GLU = _O_BA + 2 * HEADS


_MIX_BLK = P_IN // N_BLK
_MIX_B1 = _O_BA - _MIX_BLK
assert _O_BA + HEADS == 2 * _MIX_BLK
BLOCKED = ("ffn1_w_in", "ffn2_w_in", "mix_w_in")


def prep_layer(wl):
    row = lambda v: v.reshape(1, -1).astype(F32)
    bf = lambda v: v.astype(BF16)
    lanes8 = lambda v: jnp.zeros((1, LANES), F32).at[0, HEADS:2 * HEADS].set(v.astype(F32))
    mw = bf(wl["mix_w_in"])
    pad = jnp.zeros((D_MODEL, LANES - 2 * HEADS), BF16)
    if mw.ndim == 3:
        w_all = jnp.concatenate([mw[0], mw[1][:, :_MIX_B1], mw[2][:, HEADS:], mw[3], mw[1][:, _MIX_B1:],
                                 mw[2][:, :HEADS], pad], axis=1)
    else:
        w_all = jnp.concatenate([mw[:, :_O_BA], mw[:, _O_GLU:], mw[:, _O_BA:_O_GLU], pad], axis=1)
    blocked = lambda v: v if v.ndim == 3 else jnp.transpose(v.reshape(v.shape[0], N_BLK, -1), (1, 0, 2))
    ffn = lambda k: dict(norm_pre=row(wl[k + "_norm_pre"]), norm_post=row(wl[k + "_norm_post"]),
                         w_in=blocked(bf(wl[k + "_w_in"])), w_out=bf(wl[k + "_w_out"]))
    mix = dict(norm_pre=row(wl["mix_norm_pre"]), norm_post=row(wl["mix_norm_post"]), w_all=w_all,
               conv_w=wl["gdn_conv_w"].astype(F32), alog=lanes8(wl["gdn_a_log"]), dtb=lanes8(wl["gdn_dt_bias"]),
               gdn_norm_w=row(wl["gdn_norm_w"]), gdn_w_o=bf(wl["gdn_w_o"]), pw1_b=row(wl["cnv_pw1_b"]),
               dw_w=wl["cnv_dw_w"].astype(F32), dw_b=row(wl["cnv_dw_b"]), ln_g=row(wl["cnv_ln_g"]),
               ln_b=row(wl["cnv_ln_b"]), cnv_w_o=bf(wl["cnv_w_o"]), b_o=row(wl["cnv_b_o"]), w_out=bf(wl["mix_w_out"]))
    return dict(ffn1=ffn("ffn1"), mix=mix, ffn2=ffn("ffn2"))


def unprep_grads(g):
    m = g["mix"]
    dqkv, dz, dpa, dpg, dga, dgb, dba = m["w_all"]
    out = {}
    for k in ("ffn1", "ffn2"):
        out[k + "_norm_pre"] = g[k]["norm_pre"][0]
        out[k + "_norm_post"] = g[k]["norm_post"][0]
        blk = g[k]["w_in"]
        out[k + "_w_in#blocks"] = blk
        out[k + "_w_in"] = jnp.transpose(blk, (1, 0, 2)).reshape(blk.shape[1], N_BLK * blk.shape[2])
        out[k + "_w_out"] = g[k]["w_out"]
    out["mix_w_in#blocks"] = jnp.stack([
        dqkv[:, :_MIX_BLK], jnp.concatenate([dqkv[:, _MIX_BLK:], dz, dba[:, :HEADS]], axis=1),
        jnp.concatenate([dba[:, HEADS:2 * HEADS], dpa, dpg[:, :_MIX_B1 - D_MODEL]], axis=1),
        jnp.concatenate([dpg[:, _MIX_B1 - D_MODEL:], dga, dgb], axis=1)])
    out.update(
        mix_norm_pre=m["norm_pre"][0], mix_norm_post=m["norm_post"][0],
        mix_w_in=jnp.concatenate([dqkv, dz, dba[:, :2 * HEADS], dpa, dpg, dga, dgb], axis=1),
        gdn_conv_w=m["conv_w"], gdn_a_log=m["alog"][0, HEADS:2 * HEADS], gdn_dt_bias=m["dtb"][0, HEADS:2 * HEADS],
        gdn_norm_w=m["gdn_norm_w"][0], gdn_w_o=m["gdn_w_o"], cnv_pw1_b=m["pw1_b"][0], cnv_dw_w=m["dw_w"],
        cnv_dw_b=m["dw_b"][0], cnv_ln_g=m["ln_g"][0], cnv_ln_b=m["ln_b"][0], cnv_w_o=m["cnv_w_o"], cnv_b_o=m["b_o"][0],
        mix_w_out=m["w_out"])
    return out


MESH = pl.DeviceIdType.MESH
ANY = pl.BlockSpec(memory_space=pl.ANY)
N_DEV = 8


def _pos():
    return lax.axis_index("x"), lax.axis_index("y"), lax.axis_index("c")


def _other_chips(x, y):
    return [(1 - x, y), (x, 1 - y), (1 - x, 1 - y)]


class Job:
    def __init__(self, ins, outs, n_sems, copies, aliases=None):
        self.ins, self.outs, self.n_sems, self.copies = ins, outs, n_sems, copies
        self.aliases = aliases or {}
        self.results = None

    def scratch(self):
        return [pltpu.SemaphoreType.DMA((self.n_sems,)), pltpu.SemaphoreType.DMA((self.n_sems,))]

    def start(self, in_refs, out_refs, sems):
        for cp in self.copies(in_refs, out_refs, sems, False):
            cp.start()

    def finish(self, in_refs, out_refs, sems):
        for cp in self.copies(in_refs, out_refs, sems, True):
            cp.wait_recv()
        for cp in self.copies(in_refs, out_refs, sems, False):
            cp.wait_send()


def run_job(name, job):
    n_i, n_o = len(job.ins), len(job.outs)

    def body(*refs):
        in_refs, out_refs, sems = refs[:n_i], refs[n_i:n_i + n_o], refs[n_i + n_o:]
        job.start(in_refs, out_refs, sems)
        job.finish(in_refs, out_refs, sems)

    job.results = pl.pallas_call(
        body, name=name, in_specs=[ANY] * n_i, out_specs=[ANY] * n_o, out_shape=job.outs,
        input_output_aliases=job.aliases, scratch_shapes=job.scratch())(*job.ins)
    return job.results


def _halved(rows):
    return rows % 32 == 0


def job_gather_ici(shards):
    n = len(shards)

    def copies(in_refs, out_refs, sems, recv):
        x, y, c = _pos()
        b = 2 * x + y
        chips = _other_chips(x, y)
        cps = []
        for a in range(n):
            hr = shards[a].shape[0] // 2
            for j in range(3):
                blk = 2 * chips[j][0] + chips[j][1] if recv else b
                if _halved(shards[a].shape[0]):
                    src, dst = in_refs[a].at[pl.ds(c * hr, hr)], out_refs[a].at[blk, pl.ds(c * hr, hr)]
                else:
                    src, dst = in_refs[a], out_refs[a].at[blk]
                cps.append(pltpu.make_async_remote_copy(
                    src_ref=src, dst_ref=dst, send_sem=sems[0].at[3 * a + j], recv_sem=sems[1].at[3 * a + j],
                    device_id=(chips[j][0], chips[j][1], c), device_id_type=MESH))
        return cps

    return Job(list(shards), [S((N_BLK,) + w.shape, w.dtype) for w in shards], 3 * n, copies)


def job_gather_sibling(lands):
    idx = [a for a, w in enumerate(lands) if _halved(w.shape[1])]

    def copies(in_refs, out_refs, sems, recv):
        x, y, c = _pos()
        chips = _other_chips(x, y)
        half = 1 - c if recv else c
        cps = []
        for pos, a in enumerate(idx):
            hr = lands[a].shape[1] // 2
            for j in range(3):
                rows = out_refs[a].at[2 * chips[j][0] + chips[j][1], pl.ds(half * hr, hr)]
                cps.append(pltpu.make_async_remote_copy(
                    src_ref=rows, dst_ref=rows, send_sem=sems[0].at[3 * pos + j], recv_sem=sems[1].at[3 * pos + j],
                    device_id=(x, y, 1 - c), device_id_type=MESH))
        return cps

    return Job(list(lands), [S(w.shape, w.dtype) for w in lands], 3 * len(idx), copies,
               aliases={a: a for a in range(len(lands))})


def job_rs_chips(ps):
    n = len(ps)

    def copies(in_refs, out_refs, sems, recv):
        x, y, c = _pos()
        b = 2 * x + y
        chips = _other_chips(x, y)
        cps = []
        for k in range(n):
            for j in range(3):
                other = 2 * chips[j][0] + chips[j][1]
                src_blk, dst_slot = (b, other) if recv else (other, b)
                cps.append(pltpu.make_async_remote_copy(
                    src_ref=in_refs[k].at[src_blk], dst_ref=out_refs[k].at[dst_slot], send_sem=sems[0].at[3 * k + j],
                    recv_sem=sems[1].at[3 * k + j], device_id=(chips[j][0], chips[j][1], c), device_id_type=MESH))
        return cps

    return Job(list(ps), [S(p.shape, p.dtype) for p in ps], 3 * n, copies)


def rs_sibling(gs):
    n = len(gs)

    def body(*refs):
        g_refs, r_refs = refs[:n], refs[n:2 * n]
        send_sems, recv_sems = refs[2 * n:]
        x, y, c = _pos()

        def cp(k):
            hr = gs[k].shape[1] // 2
            return pltpu.make_async_remote_copy(
                src_ref=g_refs[k].at[:, pl.ds((1 - c) * hr, hr)], dst_ref=r_refs[k], send_sem=send_sems.at[k],
                recv_sem=recv_sems.at[k], device_id=(x, y, 1 - c), device_id_type=MESH)

        cps = [cp(k) for k in range(n)]
        for d in cps:
            d.start()
        for d in cps:
            d.wait_recv()
        for d in cps:
            d.wait_send()

    return pl.pallas_call(
        body, name="rs_sibling", in_specs=[ANY] * n, out_specs=[ANY] * n,
        out_shape=[S((N_BLK, g.shape[1] // 2, g.shape[2]), g.dtype) for g in gs],
        scratch_shapes=[pltpu.SemaphoreType.DMA((n,)), pltpu.SemaphoreType.DMA((n,))])(*gs)


def ag_sibling(fs):
    n = len(fs)

    def body(*refs):
        o_refs = refs[n:2 * n]
        send_sems, recv_sems = refs[2 * n:]
        x, y, c = _pos()

        def cp(k, half):
            hr = fs[k].shape[0] // 2
            rows = o_refs[k].at[pl.ds(half * hr, hr)]
            return pltpu.make_async_remote_copy(
                src_ref=rows, dst_ref=rows, send_sem=send_sems.at[k], recv_sem=recv_sems.at[k],
                device_id=(x, y, 1 - c), device_id_type=MESH)

        cps = [cp(k, c) for k in range(n)]
        for d in cps:
            d.start()
        for k in range(n):
            cp(k, 1 - c).wait_recv()
        for d in cps:
            d.wait_send()

    return pl.pallas_call(
        body, name="ag_sibling", in_specs=[ANY] * n, out_specs=[ANY] * n,
        out_shape=[S(f.shape, f.dtype) for f in fs], input_output_aliases={k: k for k in range(n)},
        scratch_shapes=[pltpu.SemaphoreType.DMA((n,)), pltpu.SemaphoreType.DMA((n,))])(*fs)


def allreduce_small(v):
    rows = v.shape[0]

    def body(v_ref, o_ref, buf, send_sems, recv_sems):
        x, y, c = _pos()
        me = 4 * x + 2 * y + c
        buf[me] = v_ref[...]

        def cp(d, slot):
            dx, dy, dc = (d >> 2) & 1, (d >> 1) & 1, d & 1
            peer = (1 - x if dx else x, 1 - y if dy else y, 1 - c if dc else c)
            return pltpu.make_async_remote_copy(
                src_ref=v_ref, dst_ref=buf.at[slot], send_sem=send_sems.at[d - 1], recv_sem=recv_sems.at[d - 1],
                device_id=peer, device_id_type=MESH)

        cps = [cp(d, me) for d in range(1, N_DEV)]
        for d in cps:
            d.start()
        for d in range(1, N_DEV):
            dx, dy, dc = (d >> 2) & 1, (d >> 1) & 1, d & 1
            src = 4 * (1 - x if dx else x) + 2 * (1 - y if dy else y) + (1 - c if dc else c)
            cp(d, src).wait_recv()
        for d in cps:
            d.wait_send()
        acc = buf[0]
        for s in range(1, N_DEV):
            acc = acc + buf[s]
        o_ref[...] = acc

    vm = pl.BlockSpec(memory_space=pltpu.VMEM)
    return pl.pallas_call(
        body, name="allreduce_small", in_specs=[vm], out_specs=vm, out_shape=S(v.shape, v.dtype),
        scratch_shapes=[pltpu.VMEM((N_DEV, rows, LANES), F32), pltpu.SemaphoreType.DMA((N_DEV - 1,)),
                        pltpu.SemaphoreType.DMA((N_DEV - 1,))])(v)


def _rows_tile(rows, cols, cap_bytes=1 << 20, mult=8):
    best = None
    for t in range(mult, rows + 1, mult):
        if rows % t == 0 and t * cols * 4 <= cap_bytes:
            best = t
    return best if best is not None else rows


def add_half(name, g, r, c_arr):
    _, hr, cols = r.shape
    tr = _rows_tile(hr, cols, mult=16)
    nb = hr // tr

    def body(c_ref, g_ref, r_ref, o_ref):
        o_ref[...] = (g_ref[...] + r_ref[...]).astype(o_ref.dtype)

    gs = pltpu.PrefetchScalarGridSpec(
        num_scalar_prefetch=1, grid=(N_BLK, nb),
        in_specs=[pl.BlockSpec((None, tr, cols), lambda b, i, cr: (b, cr[0] * nb + i, 0)),
                  pl.BlockSpec((None, tr, cols), lambda b, i, cr: (b, i, 0))],
        out_specs=pl.BlockSpec((None, tr, cols), lambda b, i, cr: (b, i, 0)))
    return pl.pallas_call(body, name=name, grid_spec=gs, out_shape=S(r.shape, BF16),
                          compiler_params=_cp(("parallel", "parallel")))(c_arr, g, r)


def sum_chips(name, r, own, cb_arr):
    _, hr, cols = r.shape
    tr = _rows_tile(hr, cols, mult=16)
    nb = hr // tr

    def body(cb_ref, *refs):
        o_ref = refs[N_BLK + 1]
        b = cb_ref[1]
        acc = None
        for s in range(N_BLK):
            term = jnp.where(b == s, refs[N_BLK][...], refs[s][...]).astype(F32)
            acc = term if acc is None else acc + term
        o_ref[...] = acc

    slot = lambda s: pl.BlockSpec((None, tr, cols), lambda i, cb: (jnp.where(cb[1] == s, (s + 1) % N_BLK, s), i, 0))
    gs = pltpu.PrefetchScalarGridSpec(
        num_scalar_prefetch=1, grid=(nb,),
        in_specs=[slot(s) for s in range(N_BLK)] + [pl.BlockSpec((None, tr, cols), lambda i, cb: (cb[1], i, 0))],
        out_specs=pl.BlockSpec((tr, cols), lambda i, cb: (cb[0] * nb + i, 0)))
    return pl.pallas_call(body, name=name, grid_spec=gs, out_shape=S((2 * hr, cols), F32),
                          compiler_params=_cp(("parallel",)))(cb_arr, *([r] * N_BLK), own)


def adamw(name, w, m, v, gs):
    rows, cols = w.shape
    two = len(gs) == 2
    span = rows // 2 if two else rows
    tr = _rows_tile(span, cols, 1 << 19)
    nb = span // tr

    def body(w_ref, m_ref, v_ref, *rest):
        g_refs, (go_ref, d_ref, mo_ref, vo_ref) = rest[:len(gs)], rest[len(gs):]
        if two:
            g = jnp.where(pl.program_id(0) < nb, g_refs[0][...], g_refs[1][...])
        else:
            g = g_refs[0][...]
        mn = ADAM_B1 * m_ref[...] + (1.0 - ADAM_B1) * g
        vn = ADAM_B2 * v_ref[...] + (1.0 - ADAM_B2) * jnp.square(g)
        m_hat = mn / (1.0 - ADAM_B1 ** ADAM_STEP)
        v_hat = vn / (1.0 - ADAM_B2 ** ADAM_STEP)
        go_ref[...] = g
        d_ref[...] = -ADAM_LR * (m_hat / (jnp.sqrt(v_hat) + ADAM_EPS) + ADAM_WD * w_ref[...])
        mo_ref[...] = mn
        vo_ref[...] = vn

    full = pl.BlockSpec((tr, cols), lambda i: (i, 0))
    if two:
        g_specs = [pl.BlockSpec((tr, cols), lambda i: (jnp.minimum(i, nb - 1), 0)),
                   pl.BlockSpec((tr, cols), lambda i: (jnp.maximum(i - nb, 0), 0))]
    else:
        g_specs = [full]
    return pl.pallas_call(
        body, name=name, grid=(2 * nb if two else nb,), in_specs=[full, full, full] + g_specs, out_specs=[full] * 4,
        out_shape=[S((rows, cols), F32)] * 4, compiler_params=_cp(("parallel",)))(w, m, v, *gs)


WEIGHTS = ["ffn1_norm_pre", "ffn1_norm_post", "ffn1_w_in", "ffn1_w_out", "mix_norm_pre", "mix_norm_post", "mix_w_in",
           "gdn_conv_w", "gdn_a_log", "gdn_dt_bias", "gdn_norm_w", "gdn_w_o", "cnv_pw1_b", "cnv_dw_w", "cnv_dw_b",
           "cnv_ln_g", "cnv_ln_b", "cnv_w_o", "cnv_b_o", "mix_w_out", "ffn2_norm_pre", "ffn2_norm_post", "ffn2_w_in",
           "ffn2_w_out"]
BIG = {"ffn1_w_in": True, "ffn1_w_out": False, "mix_w_in": True, "gdn_conv_w": True, "gdn_w_o": False,
       "cnv_dw_w": True, "cnv_w_o": False, "mix_w_out": False, "ffn2_w_in": True, "ffn2_w_out": False}
TINY = {"gdn_conv_w": (32, LANES), "cnv_dw_w": (64, LANES)}
SMALL = [n for n in WEIGHTS if n not in BIG]
GATHER_GROUPS = [["ffn1_w_in"], ["ffn1_w_out"], ["mix_w_in"], ["gdn_conv_w", "gdn_w_o"], ["cnv_dw_w", "cnv_w_o"],
                 ["mix_w_out"], ["ffn2_w_in"], ["ffn2_w_out"]]
RS_GROUPS = [["ffn1_w_in"], ["ffn1_w_out"], ["mix_w_in"], ["ffn2_w_in"], ["gdn_conv_w", "gdn_w_o"],
             ["cnv_dw_w", "cnv_w_o"], ["mix_w_out"], ["ffn2_w_out"]]


def _whole(name, blocks):
    if BIG[name]:
        return jnp.transpose(blocks, (1, 0, 2)).reshape(blocks.shape[1], N_BLK * blocks.shape[2])
    return blocks.reshape(N_BLK * blocks.shape[1], blocks.shape[2])


def _blocks(name, whole):
    r, cfull = whole.shape
    if BIG[name]:
        blk = jnp.transpose(whole.reshape(r, N_BLK, cfull // N_BLK), (1, 0, 2))
    else:
        blk = whole.reshape(N_BLK, r // N_BLK, cfull)
    if name in TINY:
        tr, tc = TINY[name]
        flat = blk.reshape(N_BLK, -1)
        blk = jnp.pad(flat, ((0, 0), (0, tr * tc - flat.shape[1]))).reshape(N_BLK, tr, tc)
    return blk


def _pack(parts):
    rows = []
    for p in parts:
        flat = p.reshape(-1).astype(F32)
        rows.append(jnp.pad(flat, (0, (-flat.shape[0]) % LANES)).reshape(-1, LANES))
    out = jnp.concatenate(rows, axis=0)
    return jnp.pad(out, ((0, (-out.shape[0]) % 8), (0, 0)))


def _unpack(packed, shapes):
    out, r = [], 0
    for shp in shapes:
        size = math.prod(shp)
        nr = -(-size // LANES)
        out.append(packed[r:r + nr].reshape(-1)[:size].reshape(shp))
        r += nr
    return out


def kernel(x, ffn1_norm_pre, ffn1_norm_post, ffn1_w_in, ffn1_w_out, mix_norm_pre, mix_norm_post, mix_w_in, gdn_conv_w, gdn_a_log, gdn_dt_bias, gdn_norm_w, gdn_w_o, cnv_pw1_b, cnv_dw_w, cnv_dw_b, cnv_ln_g, cnv_ln_b, cnv_w_o, cnv_b_o, mix_w_out, ffn2_norm_pre, ffn2_norm_post, ffn2_w_in, ffn2_w_out, loss_target, m_ffn1_norm_pre, m_ffn1_norm_post, m_ffn1_w_in, m_ffn1_w_out, m_mix_norm_pre, m_mix_norm_post, m_mix_w_in, m_gdn_conv_w, m_gdn_a_log, m_gdn_dt_bias, m_gdn_norm_w, m_gdn_w_o, m_cnv_pw1_b, m_cnv_dw_w, m_cnv_dw_b, m_cnv_ln_g, m_cnv_ln_b, m_cnv_w_o, m_cnv_b_o, m_mix_w_out, m_ffn2_norm_pre, m_ffn2_norm_post, m_ffn2_w_in, m_ffn2_w_out, v_ffn1_norm_pre, v_ffn1_norm_post, v_ffn1_w_in, v_ffn1_w_out, v_mix_norm_pre, v_mix_norm_post, v_mix_w_in, v_gdn_conv_w, v_gdn_a_log, v_gdn_dt_bias, v_gdn_norm_w, v_gdn_w_o, v_cnv_pw1_b, v_cnv_dw_w, v_cnv_dw_b, v_cnv_ln_g, v_cnv_ln_b, v_cnv_w_o, v_cnv_b_o, v_mix_w_out, v_ffn2_norm_pre, v_ffn2_norm_post, v_ffn2_w_in, v_ffn2_w_out):
    args = locals()
    wts = {n: args[n] for n in WEIGHTS}
    mom = {n: args["m_" + n] for n in WEIGHTS}
    var = {n: args["v_" + n] for n in WEIGHTS}
    big = list(BIG)

    mx, my, mc = _pos()
    mb = 2 * mx + my
    cb_arr = jnp.stack([mc, mb]).astype(jnp.int32)

    def my_shards(l):
        return [wts[n][l].astype(BF16) for n in big]

    def whole_layer(l, own, lands):
        wl = {}
        for n, o, land in zip(big, own, lands):
            blocks = lax.dynamic_update_index_in_dim(land, o, mb, 0)
            wl[n] = blocks if n in BLOCKED else _whole(n, blocks)
        wl.update({n: wts[n][l] for n in SMALL})
        return prep_layer(wl)

    def grouped(make_job, arrays, groups):
        return [make_job([arrays[big.index(n)] for n in grp]) for grp in groups]

    def by_name(jobs, groups):
        res = dict(zip([n for grp in groups for n in grp], [r for jb in jobs for r in jb.results]))
        return [res[n] for n in big]

    own0, own1 = my_shards(0), my_shards(1)
    lands0 = run_job("gather0_sib", job_gather_sibling(run_job("gather0", job_gather_ici(own0))))
    layer0 = whole_layer(0, own0, lands0)
    jobs = grouped(job_gather_ici, own1, GATHER_GROUPS)
    pending = list(jobs)
    act, saved0 = layer_fwd(x[0], layer0, pending)
    for jb in pending:
        run_job("gather1_rest", jb)
    layer1 = whole_layer(1, own1, run_job("gather1_sib", job_gather_sibling(by_name(jobs, GATHER_GROUPS))))
    act, saved1 = layer_fwd(act, layer1)
    dx, loss = loss_head("loss", act, loss_target[0], _tm(act.shape[0]))

    def partials(gw_l):
        blocks = [gw_l[n + "#blocks"] if n in BLOCKED else _blocks(n, gw_l[n]) for n in big]
        return [add_half("add_half", g, r, cb_arr) for g, r in zip(blocks, rs_sibling(blocks))]

    dx, g1 = layer_bwd(dx, saved1, layer1)
    gw1 = unprep_grads(g1)
    part1 = partials(gw1)
    jobs = grouped(job_rs_chips, part1, RS_GROUPS)
    pending = list(jobs)
    dx, g0 = layer_bwd(dx, saved0, layer0, pending)
    for jb in pending:
        run_job("rs1_rest", jb)
    chip1 = by_name(jobs, RS_GROUPS)
    gw0 = unprep_grads(g0)
    part0 = partials(gw0)
    chip0 = run_job("rs0_chips", job_rs_chips(part0))
    gw = [gw0, gw1]

    small_shapes = [wts[n].shape for n in SMALL]
    packed = _pack([jnp.stack([gw[l][n] for l in range(DEPTH)]) for n in SMALL] + [loss])
    total = allreduce_small(packed)
    small_g = dict(zip(SMALL, _unpack(total, small_shapes)))
    loss_sum = total[sum(-(-math.prod(s) // LANES) for s in small_shapes), 0]

    keys = [(n, l) for l in range(DEPTH) for n in big]
    halves = [sum_chips("sum_chips", r, p, cb_arr) for r, p in zip(list(chip0) + list(chip1), part0 + part1)]
    summed = dict(zip(keys, ag_sibling(halves)))

    out_g, out_d, out_m, out_v = {}, {}, {}, {}
    for n in big:
        shp = wts[n].shape
        gs = [summed[(n, l)] for l in range(DEPTH)]
        if n in TINY:
            gs = [jnp.concatenate([g.reshape(-1)[:shp[1] * shp[2]].reshape(shp[1], shp[2]) for g in gs], axis=0)]
        two_d = lambda a: a.reshape(DEPTH * shp[1], shp[2])
        res = adamw("adamw", two_d(wts[n]), two_d(mom[n]), two_d(var[n]), gs)
        out_g[n], out_d[n], out_m[n], out_v[n] = [r.reshape(shp) for r in res]

    pk = lambda d: _pack([d[n] for n in SMALL])
    res = adamw("adamw_small", pk(wts), pk(mom), pk(var), [pk(small_g)])
    for d, r in zip((out_g, out_d, out_m, out_v), res):
        d.update(dict(zip(SMALL, _unpack(r, small_shapes))))

    return (loss_sum, dx[None], *[out_g[n] for n in WEIGHTS], *[out_d[n] for n in WEIGHTS],
            *[out_m[n] for n in WEIGHTS], *[out_v[n] for n in WEIGHTS])
```

```python
import functools
import math

import jax
import jax.numpy as jnp
from jax import lax
from jax.experimental import pallas as pl
from jax.experimental.pallas import tpu as pltpu

F32, BF16 = jnp.float32, jnp.bfloat16
S = jax.ShapeDtypeStruct

D_MODEL = 1024
D_FF = 2816
HEADS = 8
DK = 128
CHUNK = 64
GDN_CONV = 4
CNV_K = 31
W_QKV = 3 * HEADS * DK
W_Z = HEADS * DK
W_GLU = 2 * D_MODEL
W_GATE = 2 * D_MODEL
P_IN = W_QKV + W_Z + 2 * HEADS + W_GLU + W_GATE
LANES = 128
P_ALL = W_QKV + W_Z + W_GLU + W_GATE + LANES
COL_Z = W_QKV // LANES
COL_GLU = (W_QKV + W_Z) // LANES
COL_GATE = (W_QKV + W_Z + W_GLU) // LANES
COL_BA = (W_QKV + W_Z + W_GLU + W_GATE) // LANES
RMS_EPS = 1e-6
LN_EPS = 1e-5
DEPTH = 2
N_BLK = 4
VMEM_LIMIT = 56 * 1024 * 1024

ADAM_LR, ADAM_B1, ADAM_B2, ADAM_EPS, ADAM_WD, ADAM_STEP = 0.001, 0.9, 0.999, 1e-08, 0.01, 10


def _cp(sem):
    return pltpu.CompilerParams(dimension_semantics=sem, vmem_limit_bytes=VMEM_LIMIT)


MM_VMEM_BUDGET = 36 * 1024 * 1024


def _mm_tiles(m, n, k_bytes_a, k_bytes_b, out_bytes, tn_fixed=None):
    best = None
    for tm in (1024, 512, 256, 128):
        if m % tm:
            continue
        for tn in ((tn_fixed,) if tn_fixed else (1024, 512, 640, 256, 384, 128)):
            if n % tn:
                continue
            need = 2 * (tm * k_bytes_a + tn * k_bytes_b + tm * tn * out_bytes)
            if need <= MM_VMEM_BUDGET and (best is None or tm * tn > best[0] * best[1]):
                best = (tm, tn)
    if best is None:
        raise ValueError((m, n, k_bytes_a, k_bytes_b))
    return best


def mm_nt_sum(name, parts, b):
    m, n = parts[0][0].shape[0], b.shape[0]
    k_total = sum(a.shape[1] for a, _ in parts)
    tm, tn = _mm_tiles(m, n, k_total * 2, k_total * 2, 4)
    n_p = len(parts)

    def body(*refs):
        o_ref = refs[2 * n_p]
        acc = None
        for a_ref, b_ref in zip(refs[:n_p], refs[n_p:2 * n_p]):
            t = lax.dot_general(a_ref[...], b_ref[...], (((1,), (1,)), ((), ())), preferred_element_type=F32)
            acc = t if acc is None else acc + t
        o_ref[...] = acc

    a_specs = [pl.BlockSpec((tm, a.shape[1]), lambda i, j: (i, 0)) for a, _ in parts]
    b_specs = [pl.BlockSpec((tn, a.shape[1]), functools.partial(lambda i, j, c: (j, c), c=col)) for a, col in parts]
    return pl.pallas_call(
        body, name=name, grid=(m // tm, n // tn), in_specs=a_specs + b_specs,
        out_specs=pl.BlockSpec((tm, tn), lambda i, j: (i, j)), out_shape=S((m, n), F32),
        compiler_params=_cp(("parallel", "parallel")))(*[a for a, _ in parts], *([b] * n_p))


def _take(jobs, host):
    for jb in jobs or []:
        if jb.host == host:
            jobs.remove(jb)
            return jb
    return None


def _pcall(body, job, *, name, grid, in_specs, out_specs, out_shape, scratch_shapes=(), semantics):
    in_specs, out_specs, out_shape, scratch_shapes = list(in_specs), list(out_specs), list(out_shape), list(scratch_shapes)
    if job is None:
        return lambda *args: pl.pallas_call(
            body, name=name, grid=grid, in_specs=in_specs, out_specs=out_specs, out_shape=out_shape,
            scratch_shapes=scratch_shapes, compiler_params=_cp(semantics))(*args)
    n_in, n_out, n_sc = len(in_specs), len(out_specs), len(scratch_shapes)
    n_i, n_o = len(job.ins), len(job.outs)

    def wrapped(*refs):
        cut = [n_in, n_i, n_out, n_o, n_sc]
        parts, pos = [], 0
        for c in cut:
            parts.append(refs[pos:pos + c])
            pos += c
        ins, j_in, outs, j_out, own = parts
        sems = refs[pos:]
        ids = [pl.program_id(d) for d in range(len(grid))]
        first = functools.reduce(lambda p, q: p & q, [i == 0 for i in ids])
        last = functools.reduce(lambda p, q: p & q, [i == g - 1 for i, g in zip(ids, grid)])

        @pl.when(first)
        def _():
            job.start(j_in, j_out, sems)

        body(*ins, *outs, *own)

        @pl.when(last)
        def _():
            job.finish(j_in, j_out, sems)

    def call(*args):
        res = pl.pallas_call(
            wrapped, name=name + "_c", grid=grid, in_specs=in_specs + [ANY] * n_i, out_specs=out_specs + [ANY] * n_o,
            out_shape=out_shape + list(job.outs),
            input_output_aliases={n_in + ki: n_out + ko for ki, ko in job.aliases.items()},
            scratch_shapes=scratch_shapes + job.scratch(),
            compiler_params=_cp(("arbitrary",) * len(grid)))(*args, *job.ins)
        job.results = res[n_out:]
        return res[:n_out]

    return call


def mm(name, a, b, ta=False, tb=False, out_dtype=F32, job=None, out_blocked=False):
    k = a.shape[0] if ta else a.shape[1]
    m = a.shape[1] if ta else a.shape[0]
    blocked = b.ndim == 3
    cb = b.shape[2] if blocked else None
    osz = jnp.dtype(out_dtype).itemsize
    if blocked and not tb:
        n = N_BLK * cb
        tm, tn = _mm_tiles(m, n, k * 2, k * 2, osz, tn_fixed=cb)
        b_spec = pl.BlockSpec((None, k, cb), lambda i, j: (j, 0, 0))
    elif blocked:
        n = b.shape[1]
        assert k == N_BLK * cb, (name, a.shape, b.shape)
        tm, tn = _mm_tiles(m, n, k * 2, k * 2, osz)
        b_spec = pl.BlockSpec((N_BLK, tn, cb), lambda i, j: (0, j, 0))
    else:
        n = b.shape[0] if tb else b.shape[1]
        assert k == (b.shape[1] if tb else b.shape[0]), (name, a.shape, b.shape)
        tm, tn = _mm_tiles(m, n, k * 2, k * 2, osz, tn_fixed=n // N_BLK if out_blocked else None)
        b_spec = pl.BlockSpec((tn, k), lambda i, j: (j, 0)) if tb else pl.BlockSpec((k, tn), lambda i, j: (0, j))
    a_spec = pl.BlockSpec((k, tm), lambda i, j: (0, i)) if ta else pl.BlockSpec((tm, k), lambda i, j: (i, 0))
    if out_blocked:
        o_spec, o_shape = pl.BlockSpec((None, tm, tn), lambda i, j: (j, i, 0)), S((N_BLK, m, tn), out_dtype)
    else:
        o_spec, o_shape = pl.BlockSpec((tm, tn), lambda i, j: (i, j)), S((m, n), out_dtype)
    dims = (((0 if ta else 1,), (1 if tb else 0,)), ((), ()))
    gm, gn = m // tm, n // tn

    def product(a_ref, b_ref):
        if blocked and tb:
            acc = None
            for q in range(N_BLK):
                t = lax.dot_general(a_ref[:, q * cb:(q + 1) * cb], b_ref[q], (((1,), (1,)), ((), ())),
                                    preferred_element_type=F32)
                acc = t if acc is None else acc + t
            return acc
        return lax.dot_general(a_ref[...], b_ref[...], dims, preferred_element_type=F32)

    def body(a_ref, b_ref, o_ref):
        o_ref[...] = product(a_ref, b_ref).astype(o_ref.dtype)

    return _pcall(body, job, name=name, grid=(gm, gn), in_specs=[a_spec, b_spec], out_specs=[o_spec],
                  out_shape=[o_shape], semantics=("parallel", "parallel"))(a, b)[0]


def ew_fwd(name, fn, grid, ins, outs):
    n_in = len(ins)

    def body(*refs):
        vals = [r[...].astype(F32) for r in refs[:n_in]]
        res = fn(pl.program_id(0), *vals)
        for r, v in zip(refs[n_in:], res):
            r[...] = v.astype(r.dtype)

    out = pl.pallas_call(
        body, name=name, grid=grid, in_specs=[s for _, s in ins], out_specs=[s for _, s in outs],
        out_shape=[sd for sd, _ in outs], compiler_params=_cp(("parallel", "parallel")))(*[a for a, _ in ins])
    return out


def ew_bwd(name, fn, grid, ins, cts, wrt, acc, add=None):
    n_in, n_ct, n_wrt, n_acc = len(ins), len(cts), len(wrt), len(acc)
    has_add = add is not None

    def body(*refs):
        in_refs = refs[:n_in]
        ct_refs = refs[n_in:n_in + n_ct]
        pos = n_in + n_ct
        add_ref = refs[pos] if has_add else None
        pos += 1 if has_add else 0
        wrt_refs = refs[pos:pos + n_wrt]
        acc_refs = refs[pos + n_wrt:pos + n_wrt + n_acc]
        col, tok = pl.program_id(0), pl.program_id(1)
        vals = [r[...].astype(F32) for r in in_refs]
        _, vjp = jax.vjp(lambda *a: fn(col, *a), *vals)
        grads = vjp(tuple(c[...].astype(F32) for c in ct_refs))
        for pos_w, ((idx, _, _), r) in enumerate(zip(wrt, wrt_refs)):
            g = grads[idx]
            if has_add and pos_w == 0:
                g = g + add_ref[...]
            r[...] = g.astype(r.dtype)
        for (idx, _, _, over_cols), r in zip(acc, acc_refs):
            first = (tok == 0) & (col == 0) if over_cols else tok == 0

            @pl.when(first)
            def _():
                r[...] = jnp.zeros_like(r)

            r[...] += grads[idx]

    arrays = [a for a, _ in ins] + [a for a, _ in cts] + ([add[0]] if has_add else [])
    in_specs = [s for _, s in ins] + [s for _, s in cts] + ([add[1]] if has_add else [])
    over_any = any(o for *_, o in acc)
    out = pl.pallas_call(
        body, name=name, grid=grid, in_specs=in_specs,
        out_specs=[s for _, _, s in wrt] + [s for _, _, s, _ in acc],
        out_shape=[sd for _, sd, _ in wrt] + [sd for _, sd, _, _ in acc],
        compiler_params=_cp(("arbitrary" if over_any else "parallel", "arbitrary")))(*arrays)
    return out


def _tok(width, col=0):
    return lambda tm: pl.BlockSpec((tm, width), lambda j, i: (i, col))


def _tokcol(off=0):
    return lambda tm: pl.BlockSpec((tm, LANES), lambda j, i: (i, off + j))


def _par(width, col=0):
    return pl.BlockSpec((1, width), lambda j, i: (0, col))


def _parcol(off=0):
    return pl.BlockSpec((1, LANES), lambda j, i: (0, off + j))


def _rms(x, w, eps=RMS_EPS):
    return x * lax.rsqrt(jnp.mean(x * x, axis=-1, keepdims=True) + eps) * w


def _silu(x):
    return x * jax.nn.sigmoid(x)


def fn_rms(col, x, w):
    return (_rms(x, w),)


def fn_swiglu(col, gate, up):
    return (_silu(gate) * up,)


def swiglu_bwd(name, u, da, tm):
    n_tok, f2 = u.shape
    f = f2 // 2

    def body(g_ref, up_ref, da_ref, o_ref):
        g, d = g_ref[...].astype(F32), da_ref[...].astype(F32)
        s = jax.nn.sigmoid(g)
        o_ref[:, :f] = (d * up_ref[...].astype(F32) * (s * (1.0 + g * (1.0 - s)))).astype(o_ref.dtype)
        o_ref[:, f:] = (d * (g * s)).astype(o_ref.dtype)

    half = lambda c: pl.BlockSpec((tm, f), lambda i: (i, c))
    return pl.pallas_call(
        body, name=name, grid=(n_tok // tm,), in_specs=[half(0), half(1), half(0)],
        out_specs=pl.BlockSpec((tm, f2), lambda i: (i, 0)), out_shape=S((n_tok, f2), BF16),
        compiler_params=_cp(("parallel",)))(u, u, da)


def fn_gdnpost(col, c):
    typ = col // HEADS
    y = _silu(c)
    n = y * lax.rsqrt(jnp.sum(y * y, axis=-1, keepdims=True) + 1e-6)
    n = n * jnp.where(typ == 0, DK ** -0.5, 1.0)
    return (jnp.where(typ < 2, n, y),)


def fn_gdnout(col, o, z, nw):
    return (_rms(o, nw) * _silu(z),)


def fn_glu(col, a, g, ba, bg):
    return ((a + ba) * jax.nn.sigmoid(g + bg),)


def fn_lnsilu(col, h, g, b):
    mu = jnp.mean(h, axis=-1, keepdims=True)
    var = jnp.mean(jnp.square(h - mu), axis=-1, keepdims=True)
    return (_silu((h - mu) * lax.rsqrt(var + LN_EPS) * g + b),)


def fn_merge(col, ya, yb, ga, gb, bo):
    return (jax.nn.sigmoid(ga) * ya + jax.nn.sigmoid(gb) * (yb + bo),)


HALO = 32


def conv_fwd(name, x, col_off, n_ch, w, bias, tb):
    n_tok = x.shape[0]
    k = w.shape[0]
    nt = n_tok // tb

    def body(xp_ref, xc_ref, w_ref, *rest):
        if bias is not None:
            b_ref, o_ref, xs = rest
        else:
            o_ref, xs = rest
        i = pl.program_id(1)
        xs[0:HALO, :] = jnp.where(i == 0, 0.0, xp_ref[tb - HALO:tb, :].astype(F32))
        xs[HALO:HALO + tb, :] = xc_ref[...].astype(F32)
        acc = jnp.zeros((tb, LANES), F32)
        for j in range(k):
            s = k - 1 - j
            acc = acc + w_ref[j:j + 1, :] * xs[HALO - s:HALO - s + tb, :]
        if bias is not None:
            acc = acc + b_ref[...]
        o_ref[...] = acc

    in_specs = [pl.BlockSpec((tb, LANES), lambda j, i: (jnp.maximum(i - 1, 0), col_off + j)),
                pl.BlockSpec((tb, LANES), lambda j, i: (i, col_off + j)),
                pl.BlockSpec((k, LANES), lambda j, i: (0, j))]
    args = [x, x, w]
    if bias is not None:
        in_specs.append(pl.BlockSpec((1, LANES), lambda j, i: (0, j)))
        args.append(bias)
    return pl.pallas_call(
        body, name=name, grid=(n_ch // LANES, nt), in_specs=in_specs,
        out_specs=pl.BlockSpec((tb, LANES), lambda j, i: (i, j)), out_shape=S((n_tok, n_ch), F32),
        scratch_shapes=[pltpu.VMEM((HALO + tb, LANES), F32)],
        compiler_params=_cp(("parallel", "parallel")))(*args)


def conv_bwd(name, x, col_off, n_ch, w, dy, dx_dtype, tb):
    n_tok = x.shape[0]
    k = w.shape[0]
    nt = n_tok // tb

    def body(xp_ref, xc_ref, w_ref, dyc_ref, dyn_ref, dx_ref, dw_ref, db_ref, xs, dys):
        i = pl.program_id(1)
        xs[0:HALO, :] = jnp.where(i == 0, 0.0, xp_ref[tb - HALO:tb, :].astype(F32))
        xs[HALO:HALO + tb, :] = xc_ref[...].astype(F32)
        dyc = dyc_ref[...]
        dys[0:tb, :] = dyc
        dys[tb:tb + HALO, :] = jnp.where(i == nt - 1, 0.0, dyn_ref[0:HALO, :])

        @pl.when(i == 0)
        def _():
            dw_ref[...] = jnp.zeros_like(dw_ref)
            db_ref[...] = jnp.zeros_like(db_ref)

        acc = jnp.zeros((tb, LANES), F32)
        for j in range(k):
            s = k - 1 - j
            acc = acc + w_ref[j:j + 1, :] * dys[s:s + tb, :]
            dw_ref[j:j + 1, :] += jnp.sum(dyc * xs[HALO - s:HALO - s + tb, :], axis=0, keepdims=True)
        dx_ref[...] = acc.astype(dx_ref.dtype)
        db_ref[...] += jnp.sum(dyc, axis=0, keepdims=True)

    in_specs = [pl.BlockSpec((tb, LANES), lambda j, i: (jnp.maximum(i - 1, 0), col_off + j)),
                pl.BlockSpec((tb, LANES), lambda j, i: (i, col_off + j)),
                pl.BlockSpec((k, LANES), lambda j, i: (0, j)),
                pl.BlockSpec((tb, LANES), lambda j, i: (i, j)),
                pl.BlockSpec((tb, LANES), lambda j, i: (jnp.minimum(i + 1, nt - 1), j))]
    return pl.pallas_call(
        body, name=name, grid=(n_ch // LANES, nt), in_specs=in_specs,
        out_specs=[pl.BlockSpec((tb, LANES), lambda j, i: (i, j)),
                   pl.BlockSpec((k, LANES), lambda j, i: (0, j)),
                   pl.BlockSpec((1, LANES), lambda j, i: (0, j))],
        out_shape=[S((n_tok, n_ch), dx_dtype), S((k, n_ch), F32), S((1, n_ch), F32)],
        scratch_shapes=[pltpu.VMEM((HALO + tb, LANES), F32), pltpu.VMEM((tb + HALO, LANES), F32)],
        compiler_params=_cp(("parallel", "arbitrary")))(x, x, w, dy, dy)


def gdnconv_fwd(name, p, w, tb, job=None):
    n_tok = p.shape[0]
    k = w.shape[0]
    nt = n_tok // tb

    def body(xp_ref, xc_ref, w_ref, o_ref, xs):
        j, i = pl.program_id(0), pl.program_id(1)
        xs[0:HALO, :] = jnp.where(i == 0, 0.0, xp_ref[tb - HALO:tb, :].astype(F32))
        xs[HALO:HALO + tb, :] = xc_ref[...].astype(F32)
        c = jnp.zeros((tb, LANES), F32)
        for t in range(k):
            s = k - 1 - t
            c = c + w_ref[t:t + 1, :] * xs[HALO - s:HALO - s + tb, :]
        y = c * jax.nn.sigmoid(c)
        r = lax.rsqrt(jnp.sum(y * y, axis=-1, keepdims=True) + 1e-6) * jnp.where(j < HEADS, DK ** -0.5, 1.0)
        o_ref[...] = jnp.where(j < 2 * HEADS, y * r, y)

    return _pcall(
        body, job, name=name, grid=(W_QKV // LANES, nt),
        in_specs=[pl.BlockSpec((tb, LANES), lambda j, i: (jnp.maximum(i - 1, 0), j)),
                  pl.BlockSpec((tb, LANES), lambda j, i: (i, j)),
                  pl.BlockSpec((k, LANES), lambda j, i: (0, j))],
        out_specs=[pl.BlockSpec((tb, LANES), lambda j, i: (i, j))], out_shape=[S((n_tok, W_QKV), F32)],
        scratch_shapes=[pltpu.VMEM((HALO + tb, LANES), F32)], semantics=("parallel", "parallel"))(p, p, w)[0]


def gdnconv_bwd(name, p, w, dn, tb, job=None):
    n_tok = p.shape[0]
    k = w.shape[0]
    nt = n_tok // tb
    ext = tb + HALO

    def body(xp_ref, xc_ref, xn_ref, w_ref, dnc_ref, dnn_ref, dx_ref, dw_ref, xs, dns, dcs):
        j, i = pl.program_id(0), pl.program_id(1)
        xs[0:HALO, :] = jnp.where(i == 0, 0.0, xp_ref[tb - HALO:tb, :].astype(F32))
        xs[HALO:HALO + tb, :] = xc_ref[...].astype(F32)
        xs[HALO + tb:HALO + ext, :] = jnp.where(i == nt - 1, 0.0, xn_ref[0:HALO, :].astype(F32))
        dns[0:tb, :] = dnc_ref[...]
        dns[tb:ext, :] = jnp.where(i == nt - 1, 0.0, dnn_ref[0:HALO, :])

        @pl.when(i == 0)
        def _():
            dw_ref[...] = jnp.zeros_like(dw_ref)

        c = jnp.zeros((ext, LANES), F32)
        for t in range(k):
            s = k - 1 - t
            c = c + w_ref[t:t + 1, :] * xs[HALO - s:HALO - s + ext, :]
        d = dns[...]
        sg = jax.nn.sigmoid(c)
        y = c * sg
        r = lax.rsqrt(jnp.sum(y * y, axis=-1, keepdims=True) + 1e-6)
        scale = jnp.where(j < HEADS, DK ** -0.5, 1.0)
        dy_norm = scale * (d * r - y * (r * r * r) * jnp.sum(d * y, axis=-1, keepdims=True))
        dy = jnp.where(j < 2 * HEADS, dy_norm, d)
        dc = dy * (sg * (1.0 + c * (1.0 - sg)))
        dcs[...] = dc
        acc = jnp.zeros((tb, LANES), F32)
        for t in range(k):
            s = k - 1 - t
            acc = acc + w_ref[t:t + 1, :] * dcs[s:s + tb, :]
            dw_ref[t:t + 1, :] += jnp.sum(dcs[0:tb, :] * xs[HALO - s:HALO - s + tb, :], axis=0, keepdims=True)
        dx_ref[...] = acc.astype(dx_ref.dtype)

    cur = lambda j, i: (i, j)
    nxt = lambda j, i: (jnp.minimum(i + 1, nt - 1), j)
    return _pcall(
        body, job, name=name, grid=(W_QKV // LANES, nt),
        in_specs=[pl.BlockSpec((tb, LANES), lambda j, i: (jnp.maximum(i - 1, 0), j)),
                  pl.BlockSpec((tb, LANES), cur), pl.BlockSpec((tb, LANES), nxt),
                  pl.BlockSpec((k, LANES), lambda j, i: (0, j)),
                  pl.BlockSpec((tb, LANES), cur), pl.BlockSpec((tb, LANES), nxt)],
        out_specs=[pl.BlockSpec((tb, LANES), cur), pl.BlockSpec((k, LANES), lambda j, i: (0, j))],
        out_shape=[S((n_tok, W_QKV), BF16), S((k, W_QKV), F32)],
        scratch_shapes=[pltpu.VMEM((HALO + ext, LANES), F32), pltpu.VMEM((ext, LANES), F32),
                        pltpu.VMEM((ext, LANES), F32)],
        semantics=("parallel", "arbitrary"))(p, p, p, w, dn, dn)


GDN_GROUP = 2


def _dotb(a, b, ca, cb):
    return lax.dot_general(a.astype(BF16), b.astype(BF16), (((ca,), (cb,)), ((), ())), preferred_element_type=F32)


def _dot32(a, b, ca, cb):
    return lax.dot_general(a, b, (((ca,), (cb,)), ((), ())), preferred_element_type=F32,
                           precision=lax.Precision.HIGHEST)


def _dot3_many(xs, ys, ca, cb):
    xh = [x.astype(BF16) for x in xs]
    xl = [(x - h.astype(F32)).astype(BF16) for x, h in zip(xs, xh)]
    yh = [y.astype(BF16) for y in ys]
    yl = [(y - h.astype(F32)).astype(BF16) for y, h in zip(ys, yh)]
    dg = lambda p, q: lax.dot_general(p, q, (((ca,), (cb,)), ((), ())), preferred_element_type=F32)
    hh = [dg(p, q) for p, q in zip(xh, yh)]
    hl = [dg(p, q) for p, q in zip(xh, yl)]
    lh = [dg(p, q) for p, q in zip(xl, yh)]
    return [a + (b + c) for a, b, c in zip(hh, hl, lh)]


@jax.custom_vjp
def _mm3_many(xs, ys):
    return _dot3_many(xs, ys, 1, 0)


def _mm3_fwd(xs, ys):
    return _dot3_many(xs, ys, 1, 0), (xs, ys)


def _mm3_bwd(res, cts):
    xs, ys = res
    return _dot3_many(cts, ys, 1, 1), _dot3_many(xs, cts, 0, 0)


_mm3_many.defvjp(_mm3_fwd, _mm3_bwd)


@jax.custom_vjp
def _inv_unit_lower_many(mats):
    n = mats[0].shape[0]
    eye = (lax.broadcasted_iota(jnp.int32, (n, n), 0) == lax.broadcasted_iota(jnp.int32, (n, n), 1)).astype(F32)
    inv = [eye - a for a in mats]
    p = list(mats)
    for _ in range(int(math.log2(n)) - 1):
        p = _dot3_many(p, p, 1, 0)
        upd = _dot3_many(inv, p, 1, 0)
        inv = [i + u for i, u in zip(inv, upd)]
    return inv


def _inv_fwd(mats):
    t = _inv_unit_lower_many(mats)
    return t, t


def _inv_bwd(t, dt):
    x = _dot3_many(t, dt, 0, 0)
    return ([-y for y in _dot3_many(x, t, 1, 1)],)


_inv_unit_lower_many.defvjp(_inv_fwd, _inv_bwd)


def _softplus(x):
    return jnp.maximum(x, 0.0) + jnp.log(1.0 + jnp.exp(-jnp.abs(x)))


def _gdn_intra(qs, ks, vs, pbas, alog, dtb):
    c = pbas[0].shape[0]
    row = lax.broadcasted_iota(jnp.int32, (c, c), 0)
    colm = lax.broadcasted_iota(jnp.int32, (c, c), 1)
    causal, strict = row >= colm, row > colm
    tril = causal.astype(F32)
    lane = lax.broadcasted_iota(jnp.int32, (1, LANES), 1)
    sub = lax.broadcasted_iota(jnp.int32, (LANES, 1), 0)
    last = (lax.broadcasted_iota(jnp.int32, (c, 1), 0) == c - 1).astype(F32)
    beta_all = [jax.nn.sigmoid(pb) for pb in pbas]
    g_all = [-jnp.exp(alog) * _softplus(pb + dtb) for pb in pbas]
    gc_all = [_dot32(tril, ga, 1, 0) for ga in g_all]
    gr_all = [_dot32(ga, tril, 0, 1) for ga in g_all]
    idx = [(g, h) for g in range(len(pbas)) for h in range(HEADS)]
    beta = [jnp.sum(beta_all[g] * (lane == h).astype(F32), axis=1, keepdims=True) for g, h in idx]
    gc = [jnp.sum(gc_all[g] * (lane == HEADS + h).astype(F32), axis=1, keepdims=True) for g, h in idx]
    gr = [jnp.sum(gr_all[g] * (sub == HEADS + h).astype(F32), axis=0, keepdims=True) for g, h in idx]
    decay = [jnp.where(causal, jnp.exp(jnp.where(causal, a - b, 0.0)), 0.0) for a, b in zip(gc, gr)]
    kk = [_dotb(k, k, 1, 1) for k in ks]
    tinv = _inv_unit_lower_many([jnp.where(strict, x * d * b, 0.0) for x, d, b in zip(kk, decay, beta)])
    eg = [jnp.exp(a) for a in gc]
    g_last = [jnp.sum(a * last, axis=0, keepdims=True) for a in gc]
    us = _mm3_many(tinv, [v * b for v, b in zip(vs, beta)])
    ws = _mm3_many(tinv, [k * (b * e) for k, b, e in zip(ks, beta, eg)])
    qds = [q * e for q, e in zip(qs, eg)]
    kds = [k * jnp.exp(gl - a) for k, gl, a in zip(ks, g_last, gc)]
    qks = [_dotb(q, k, 1, 1) * d for q, k, d in zip(qs, ks, decay)]
    decs = [jnp.exp(gl) for gl in g_last]
    return us, ws, qds, kds, qks, decs


def _gdn_seq(us, ws, qds, kds, qks, decs, states):
    corr = [_dotb(w, st, 1, 0) for w, st in zip(ws, states)]
    from_state = [_dotb(qd, st, 1, 0) for qd, st in zip(qds, states)]
    v_new = [u - x for u, x in zip(us, corr)]
    intra = [_dotb(qk, vn, 1, 0) for qk, vn in zip(qks, v_new)]
    upd = [_dotb(kd, vn, 0, 0) for kd, vn in zip(kds, v_new)]
    outs = [a + b for a, b in zip(from_state, intra)]
    news = [st * d + x for st, d, x in zip(states, decs, upd)]
    return outs, news


def _heads(ref, rows=slice(None), base=0):
    return [ref[rows, (base + h) * DK:(base + h + 1) * DK].astype(F32) for h in range(HEADS)]


def _qk_heads(ref, rows=slice(None)):
    return [ref[rows, h * DK:h * DK + CHUNK].astype(F32) for h in range(HEADS)]


def _put_heads(ref, vals, rows=slice(None), base=0):
    for h in range(HEADS):
        ref[rows, (base + h) * DK:(base + h + 1) * DK] = vals[h].astype(ref.dtype)


def _put_qk(ref, vals, rows=slice(None)):
    for h in range(HEADS):
        ref[rows, h * DK:h * DK + CHUNK] = vals[h].astype(ref.dtype)
        ref[rows, h * DK + CHUNK:(h + 1) * DK] = jnp.zeros(vals[h].shape, ref.dtype)


def _group(n_chunks):
    return GDN_GROUP if n_chunks % GDN_GROUP == 0 else 1


def _decs(ref):
    return [ref[h:h + 1, 0:1] for h in range(HEADS)]


def gdn_intra_fwd(name, qkvn, p, alog, dtb, job=None):
    n_tok = qkvn.shape[0]
    n = n_tok // CHUNK
    grp = _group(n)
    hd = HEADS * DK
    rb = grp * CHUNK

    def body(q_ref, k_ref, v_ref, pba_ref, al_ref, dt_ref, u_ref, w_ref, qd_ref, kd_ref, qk_ref, dec_ref):
        rows = [slice(g * CHUNK, (g + 1) * CHUNK) for g in range(grp)]
        cat = lambda ref: [t for r in rows for t in _heads(ref, r)]
        us, ws, qds, kds, qks, decs = _gdn_intra(cat(q_ref), cat(k_ref), cat(v_ref), [pba_ref[r, :].astype(F32) for r in rows],
                                                 al_ref[...], dt_ref[...])
        for g, r in enumerate(rows):
            part = slice(g * HEADS, (g + 1) * HEADS)
            _put_heads(u_ref, us[part], r)
            _put_heads(w_ref, ws[part], r)
            _put_heads(qd_ref, qds[part], r)
            _put_heads(kd_ref, kds[part], r)
            _put_qk(qk_ref, qks[part], r)
            for h in range(HEADS):
                dec_ref[g * HEADS + h:g * HEADS + h + 1, :] = jnp.broadcast_to(decs[g * HEADS + h], (1, LANES))

    blk = lambda c: pl.BlockSpec((rb, hd), lambda i: (i, c))
    par = pl.BlockSpec((1, LANES), lambda i: (0, 0))
    return _pcall(
        body, job, name=name, grid=(n // grp,),
        in_specs=[blk(0), blk(1), blk(2), pl.BlockSpec((rb, LANES), lambda i: (i, COL_BA)), par, par],
        out_specs=[blk(0)] * 5 + [pl.BlockSpec((grp * HEADS, LANES), lambda i: (i, 0))],
        out_shape=[S((n_tok, hd), F32)] + [S((n_tok, hd), BF16)] * 4 + [S((n * HEADS, LANES), F32)],
        semantics=("parallel",))(qkvn, qkvn, qkvn, p, alog, dtb)


def gdn_seq_fwd(name, u, w, qd, kd, qk, dec):
    n_tok = u.shape[0]
    n = n_tok // CHUNK
    hd = HEADS * DK

    def body(u_ref, w_ref, qd_ref, kd_ref, qk_ref, dec_ref, o_ref, s_ref, st):
        @pl.when(pl.program_id(0) == 0)
        def _():
            st[...] = jnp.zeros_like(st)

        s_ref[...] = st[...].astype(s_ref.dtype)
        states = [st[h * DK:(h + 1) * DK, :] for h in range(HEADS)]
        outs, news = _gdn_seq(_heads(u_ref), _heads(w_ref), _heads(qd_ref), _heads(kd_ref), _qk_heads(qk_ref),
                              _decs(dec_ref), states)
        _put_heads(o_ref, outs)
        for h in range(HEADS):
            st[h * DK:(h + 1) * DK, :] = news[h]

    blk = pl.BlockSpec((CHUNK, hd), lambda i: (i, 0))
    return pl.pallas_call(
        body, name=name, grid=(n,),
        in_specs=[blk] * 5 + [pl.BlockSpec((HEADS, LANES), lambda i: (i, 0))],
        out_specs=[blk, pl.BlockSpec((None, hd, DK), lambda i: (i, 0, 0))],
        out_shape=[S((n_tok, hd), F32), S((n, hd, DK), BF16)],
        scratch_shapes=[pltpu.VMEM((hd, DK), F32)],
        compiler_params=_cp(("arbitrary",)))(u, w, qd, kd, qk, dec)


def gdn_seq_bwd(name, u, w, qd, kd, qk, dec, states, do):
    n_tok = u.shape[0]
    n = n_tok // CHUNK
    hd = HEADS * DK

    def body(u_ref, w_ref, qd_ref, kd_ref, qk_ref, dec_ref, s_ref, do_ref,
             du_ref, dw_ref, dqd_ref, dkd_ref, dqk_ref, ddec_ref, dst):
        @pl.when(pl.program_id(0) == 0)
        def _():
            dst[...] = jnp.zeros_like(dst)

        states = [s_ref[h * DK:(h + 1) * DK, :].astype(F32) for h in range(HEADS)]
        _, vjp = jax.vjp(_gdn_seq, _heads(u_ref), _heads(w_ref), _heads(qd_ref), _heads(kd_ref), _qk_heads(qk_ref),
                         _decs(dec_ref), states)
        d_news = [dst[h * DK:(h + 1) * DK, :] for h in range(HEADS)]
        du, dw, dqd, dkd, dqk, ddec, dstates = vjp((_heads(do_ref), d_news))
        _put_heads(du_ref, du)
        _put_heads(dw_ref, dw)
        _put_heads(dqd_ref, dqd)
        _put_heads(dkd_ref, dkd)
        _put_qk(dqk_ref, dqk)
        for h in range(HEADS):
            ddec_ref[h:h + 1, :] = jnp.broadcast_to(ddec[h], (1, LANES))
            dst[h * DK:(h + 1) * DK, :] = dstates[h]

    blk = pl.BlockSpec((CHUNK, hd), lambda i: (n - 1 - i, 0))
    dspec = pl.BlockSpec((HEADS, LANES), lambda i: (n - 1 - i, 0))
    return pl.pallas_call(
        body, name=name, grid=(n,),
        in_specs=[blk] * 5 + [dspec, pl.BlockSpec((None, hd, DK), lambda i: (n - 1 - i, 0, 0)), blk],
        out_specs=[blk] * 5 + [dspec],
        out_shape=[S((n_tok, hd), F32)] * 5 + [S((n * HEADS, LANES), F32)],
        scratch_shapes=[pltpu.VMEM((hd, DK), F32)],
        compiler_params=_cp(("arbitrary",)))(u, w, qd, kd, qk, dec, states, do)


def gdn_intra_bwd(name, qkvn, p, alog, dtb, du, dw, dqd, dkd, dqk, ddec, job=None):
    n_tok = qkvn.shape[0]
    n = n_tok // CHUNK
    grp = _group(n)
    hd = HEADS * DK
    rb = grp * CHUNK

    def body(q_ref, k_ref, v_ref, pba_ref, al_ref, dt_ref, du_ref, dw_ref, dqd_ref, dkd_ref, dqk_ref, ddec_ref,
             dqkv_ref, dpba_ref, dal_ref, ddt_ref):
        @pl.when(pl.program_id(0) == 0)
        def _():
            dal_ref[...] = jnp.zeros_like(dal_ref)
            ddt_ref[...] = jnp.zeros_like(ddt_ref)

        rows = [slice(g * CHUNK, (g + 1) * CHUNK) for g in range(grp)]
        cat = lambda ref: [t for r in rows for t in _heads(ref, r)]
        _, vjp = jax.vjp(_gdn_intra, cat(q_ref), cat(k_ref), cat(v_ref), [pba_ref[r, :].astype(F32) for r in rows],
                         al_ref[...], dt_ref[...])
        cts = (cat(du_ref), cat(dw_ref), cat(dqd_ref), cat(dkd_ref), [t for r in rows for t in _qk_heads(dqk_ref, r)],
               [ddec_ref[i:i + 1, 0:1] for i in range(grp * HEADS)])
        dq, dk, dv, dpba, dal, ddt = vjp(cts)
        for g, r in enumerate(rows):
            part = slice(g * HEADS, (g + 1) * HEADS)
            _put_heads(dqkv_ref, dq[part], r, 0)
            _put_heads(dqkv_ref, dk[part], r, HEADS)
            _put_heads(dqkv_ref, dv[part], r, 2 * HEADS)
            dpba_ref[r, :] = dpba[g].astype(dpba_ref.dtype)
        dal_ref[...] += dal
        ddt_ref[...] += ddt

    blk = lambda c: pl.BlockSpec((rb, hd), lambda i: (i, c))
    par = pl.BlockSpec((1, LANES), lambda i: (0, 0))
    return _pcall(
        body, job, name=name, grid=(n // grp,),
        in_specs=[blk(0), blk(1), blk(2), pl.BlockSpec((rb, LANES), lambda i: (i, COL_BA)), par, par]
        + [blk(0)] * 5 + [pl.BlockSpec((grp * HEADS, LANES), lambda i: (i, 0))],
        out_specs=[pl.BlockSpec((rb, 3 * hd), lambda i: (i, 0)), pl.BlockSpec((rb, LANES), lambda i: (i, 0)), par, par],
        out_shape=[S((n_tok, 3 * hd), F32), S((n_tok, LANES), BF16), S((1, LANES), F32), S((1, LANES), F32)],
        semantics=("arbitrary",))(qkvn, qkvn, qkvn, p, alog, dtb, du, dw, dqd, dkd, dqk, ddec)


def loss_head(name, y, target, tm):
    n_tok, d = y.shape

    def body(y_ref, t_ref, dy_ref, l_ref):
        @pl.when(pl.program_id(0) == 0)
        def _():
            l_ref[...] = jnp.zeros_like(l_ref)

        e = y_ref[...] - t_ref[...]
        dy_ref[...] = e * (1.0 / d)
        l_ref[...] += jnp.sum(e * e, keepdims=True) * (0.5 / d)

    spec = pl.BlockSpec((tm, d), lambda i: (i, 0))
    return pl.pallas_call(
        body, name=name, grid=(n_tok // tm,), in_specs=[spec, spec],
        out_specs=[spec, pl.BlockSpec((1, 1), lambda i: (0, 0))], out_shape=[S((n_tok, d), F32), S((1, 1), F32)],
        compiler_params=_cp(("arbitrary",)))(y, target)


def _tm(n_tok):
    return min(512, n_tok)


def _gconv_tb(n_tok):
    return 1024 if n_tok % 1024 == 0 else _tm(n_tok)


def ffn_fwd(tag, x, w, jobs=None):
    n_tok = x.shape[0]
    tm = _tm(n_tok)
    g1 = (1, n_tok // tm)
    tD = _tok(D_MODEL)(tm)
    (h,) = ew_fwd(tag + "_rms", fn_rms, g1, [(x, tD), (w["norm_pre"], _par(D_MODEL))], [(S((n_tok, D_MODEL), BF16), tD)])
    u = mm(tag + "_in", h, w["w_in"], out_dtype=BF16, job=_take(jobs, "ffn_in"))
    tF = lambda c: _tok(D_FF, c)(tm)
    (a,) = ew_fwd(tag + "_swiglu", fn_swiglu, g1, [(u, tF(0)), (u, tF(1))], [(S((n_tok, D_FF), BF16), tF(0))])
    f = mm(tag + "_out", a, w["w_out"], job=_take(jobs, "ffn_out"))
    fn_res = lambda col, x_, f_, w_: (x_ + 0.5 * _rms(f_, w_),)
    (xo,) = ew_fwd(tag + "_res", fn_res, g1, [(x, tD), (f, tD), (w["norm_post"], _par(D_MODEL))],
                   [(S((n_tok, D_MODEL), F32), tD)])
    return xo, dict(x=x, h=h, u=u, a=a, f=f)


def ffn_bwd(tag, dxo, sv, w, jobs=None):
    n_tok = dxo.shape[0]
    tm = _tm(n_tok)
    g1 = (1, n_tok // tm)
    tD = _tok(D_MODEL)(tm)
    pD = _par(D_MODEL)
    fn_post = lambda col, f_, w_: (0.5 * _rms(f_, w_),)
    df, d_post = ew_bwd(tag + "_res_b", fn_post, g1, [(sv["f"], tD), (w["norm_post"], pD)], [(dxo, tD)],
                        [(0, S((n_tok, D_MODEL), BF16), tD)], [(1, S((1, D_MODEL), F32), pD, False)])
    da = mm(tag + "_out_bx", df, w["w_out"], tb=True, out_dtype=BF16, job=_take(jobs, "ffn_out_bx"))
    d_wout = mm(tag + "_out_bw", sv["a"], df, ta=True, job=_take(jobs, "ffn_out_bw"))
    tF = lambda c: _tok(D_FF, c)(tm)
    du = swiglu_bwd(tag + "_swiglu_b", sv["u"], da, tm)
    dh = mm(tag + "_in_bx", du, w["w_in"], tb=True, job=_take(jobs, "ffn_in_bx"))
    d_win = mm(tag + "_in_bw", sv["h"], du, ta=True, job=_take(jobs, "ffn_in_bw"), out_blocked=True)
    dx, d_pre = ew_bwd(tag + "_rms_b", fn_rms, g1, [(sv["x"], tD), (w["norm_pre"], pD)], [(dh, tD)],
                       [(0, S((n_tok, D_MODEL), F32), tD)], [(1, S((1, D_MODEL), F32), pD, False)], add=(dxo, tD))
    return dx, dict(norm_pre=d_pre, norm_post=d_post, w_in=d_win, w_out=d_wout)


def mix_fwd(tag, x, w, jobs=None):
    n_tok = x.shape[0]
    tm = _tm(n_tok)
    nt = n_tok // tm
    g1 = (1, nt)
    tD = _tok(D_MODEL)(tm)
    pD = _par(D_MODEL)
    (h,) = ew_fwd(tag + "_rms", fn_rms, g1, [(x, tD), (w["norm_pre"], pD)], [(S((n_tok, D_MODEL), BF16), tD)])
    p = mm(tag + "_in", h, w["w_all"], out_dtype=BF16, job=_take(jobs, "mix_in"))
    qkvn = gdnconv_fwd(tag + "_gconv", p, w["conv_w"], _gconv_tb(n_tok), job=_take(jobs, "gconv"))
    tC = _tokcol()(tm)
    intra = gdn_intra_fwd(tag + "_gintra", qkvn, p, w["alog"], w["dtb"], job=_take(jobs, "gintra"))
    o, states = gdn_seq_fwd(tag + "_gseq", *intra)
    (on,) = ew_fwd(tag + "_gout", fn_gdnout, (HEADS, nt),
                   [(o, tC), (p, _tokcol(COL_Z)(tm)), (w["gdn_norm_w"], _par(LANES))],
                   [(S((n_tok, D_MODEL), BF16), tC)])
    ya = mm(tag + "_go", on, w["gdn_w_o"], job=_take(jobs, "mix_small"))
    (hglu,) = ew_fwd(tag + "_glu", fn_glu, (D_MODEL // LANES, nt),
                     [(p, _tokcol(COL_GLU)(tm)), (p, _tokcol(COL_GLU + D_MODEL // LANES)(tm)),
                      (w["pw1_b"], _parcol(0)), (w["pw1_b"], _parcol(D_MODEL // LANES))],
                     [(S((n_tok, D_MODEL), F32), tC)])
    hc = conv_fwd(tag + "_cconv", hglu, 0, D_MODEL, w["dw_w"], w["dw_b"], tm)
    (hs,) = ew_fwd(tag + "_ln", fn_lnsilu, g1, [(hc, tD), (w["ln_g"], pD), (w["ln_b"], pD)],
                   [(S((n_tok, D_MODEL), BF16), tD)])
    yb = mm(tag + "_co", hs, w["cnv_w_o"], job=_take(jobs, "mix_small"))
    tG = lambda cb: pl.BlockSpec((tm, D_MODEL), lambda j, i: (i, cb))
    gcol = (W_QKV + W_Z + W_GLU) // D_MODEL
    (ym,) = ew_fwd(tag + "_merge", fn_merge, g1, [(ya, tD), (yb, tD), (p, tG(gcol)), (p, tG(gcol + 1)), (w["b_o"], pD)],
                   [(S((n_tok, D_MODEL), BF16), tD)])
    y = mm(tag + "_wo", ym, w["w_out"], job=_take(jobs, "mix_small"))
    fn_res = lambda col, x_, f_, w_: (x_ + _rms(f_, w_),)
    (xo,) = ew_fwd(tag + "_res", fn_res, g1, [(x, tD), (y, tD), (w["norm_post"], pD)], [(S((n_tok, D_MODEL), F32), tD)])
    sv = dict(x=x, h=h, p=p, qkvn=qkvn, intra=intra, states=states, o=o, on=on, ya=ya, hglu=hglu, hc=hc, hs=hs, yb=yb, ym=ym, y=y)
    return xo, sv


def mix_bwd(tag, dxo, sv, w, jobs=None):
    n_tok = dxo.shape[0]
    tm = _tm(n_tok)
    nt = n_tok // tm
    g1 = (1, nt)
    tD = _tok(D_MODEL)(tm)
    pD = _par(D_MODEL)
    tC = _tokcol()(tm)
    p = sv["p"]
    sD = lambda dt: S((n_tok, D_MODEL), dt)
    fn_post = lambda col, f_, w_: (_rms(f_, w_),)
    dy, d_post = ew_bwd(tag + "_res_b", fn_post, g1, [(sv["y"], tD), (w["norm_post"], pD)], [(dxo, tD)],
                        [(0, sD(BF16), tD)], [(1, S((1, D_MODEL), F32), pD, False)])
    dym = mm(tag + "_wo_bx", dy, w["w_out"], tb=True, job=_take(jobs, "mix_small"))
    d_wout = mm(tag + "_wo_bw", sv["ym"], dy, ta=True, job=_take(jobs, "mix_small"))
    tG = lambda cb: pl.BlockSpec((tm, D_MODEL), lambda j, i: (i, cb))
    gcol = (W_QKV + W_Z + W_GLU) // D_MODEL
    dya, dyb, dga, dgb, d_bo = ew_bwd(
        tag + "_merge_b", fn_merge, g1, [(sv["ya"], tD), (sv["yb"], tD), (p, tG(gcol)), (p, tG(gcol + 1)), (w["b_o"], pD)],
        [(dym, tD)], [(0, sD(BF16), tD), (1, sD(BF16), tD), (2, sD(BF16), tD), (3, sD(BF16), tD)],
        [(4, S((1, D_MODEL), F32), pD, False)])
    dhs = mm(tag + "_co_bx", dyb, w["cnv_w_o"], tb=True, job=_take(jobs, "mix_small"))
    d_cwo = mm(tag + "_co_bw", sv["hs"], dyb, ta=True, job=_take(jobs, "mix_small"))
    dhc, d_lng, d_lnb = ew_bwd(tag + "_ln_b", fn_lnsilu, g1, [(sv["hc"], tD), (w["ln_g"], pD), (w["ln_b"], pD)], [(dhs, tD)],
                               [(0, sD(F32), tD)], [(1, S((1, D_MODEL), F32), pD, False), (2, S((1, D_MODEL), F32), pD, False)])
    dhglu, d_dww, d_dwb = conv_bwd(tag + "_cconv_b", sv["hglu"], 0, D_MODEL, w["dw_w"], dhc, F32, tm)
    nc = D_MODEL // LANES
    dpa, dpg, d_ba, d_bg = ew_bwd(
        tag + "_glu_b", fn_glu, (nc, nt),
        [(p, _tokcol(COL_GLU)(tm)), (p, _tokcol(COL_GLU + nc)(tm)), (w["pw1_b"], _parcol(0)), (w["pw1_b"], _parcol(nc))],
        [(dhglu, tC)], [(0, sD(BF16), tC), (1, sD(BF16), tC)],
        [(2, S((1, D_MODEL), F32), _parcol(0), False), (3, S((1, D_MODEL), F32), _parcol(0), False)])
    don = mm(tag + "_go_bx", dya, w["gdn_w_o"], tb=True, job=_take(jobs, "mix_small"))
    d_gwo = mm(tag + "_go_bw", sv["on"], dya, ta=True, job=_take(jobs, "mix_small"))
    do, dz, d_gnw = ew_bwd(tag + "_gout_b", fn_gdnout, (HEADS, nt),
                           [(sv["o"], tC), (p, _tokcol(COL_Z)(tm)), (w["gdn_norm_w"], _par(LANES))], [(don, tC)],
                           [(0, sD(F32), tC), (1, sD(BF16), tC)], [(2, S((1, LANES), F32), _par(LANES), True)])
    d_intra = gdn_seq_bwd(tag + "_gseq_b", *sv["intra"], sv["states"], do)
    dqkvn, dpba, d_alog, d_dtb = gdn_intra_bwd(tag + "_gintra_b", sv["qkvn"], p, w["alog"], w["dtb"], *d_intra,
                                               job=_take(jobs, "gintra_b"))
    dqkv, d_convw = gdnconv_bwd(tag + "_gconv_b", p, w["conv_w"], dqkvn, _gconv_tb(n_tok), job=_take(jobs, "gconv_b"))
    nd = D_MODEL // LANES
    pieces = [(dqkv, 0, 0), (dz, W_QKV // D_MODEL, COL_Z), (dpa, COL_GLU // nd, COL_GLU), (dpg, COL_GLU // nd + 1, COL_GLU + nd),
              (dga, COL_GATE // nd, COL_GATE), (dgb, COL_GATE // nd + 1, COL_GATE + nd), (dpba, COL_BA, COL_BA)]
    dh = mm_nt_sum(tag + "_in_bx", [(a, blk) for a, blk, _ in pieces], w["w_all"])
    d_wall = [mm(tag + "_in_bw", sv["h"], a, ta=True, job=_take(jobs, "mix_small")) for a, _, _ in pieces]
    dx, d_pre = ew_bwd(tag + "_rms_b", fn_rms, g1, [(sv["x"], tD), (w["norm_pre"], pD)], [(dh, tD)],
                       [(0, sD(F32), tD)], [(1, S((1, D_MODEL), F32), pD, False)], add=(dxo, tD))
    grads = dict(norm_pre=d_pre, norm_post=d_post, w_all=d_wall, conv_w=d_convw, alog=d_alog, dtb=d_dtb,
                 gdn_norm_w=d_gnw, gdn_w_o=d_gwo, pw1_b=jnp.concatenate([d_ba, d_bg], axis=1), dw_w=d_dww,
                 dw_b=d_dwb, ln_g=d_lng, ln_b=d_lnb, cnv_w_o=d_cwo, b_o=d_bo, w_out=d_wout)
    return dx, grads


def local_step(x, target, layers):
    saved = []
    for lw in layers:
        x, sv = layer_fwd(x, lw)
        saved.append(sv)
    dx, loss = loss_head("loss", x, target, _tm(x.shape[0]))
    grads = [None] * len(layers)
    for i in reversed(range(len(layers))):
        dx, grads[i] = layer_bwd(dx, saved[i], layers[i])
    return loss, dx, grads


def layer_fwd(x, lw, jobs=None):
    x, s1 = ffn_fwd("ffn", x, lw["ffn1"], jobs)
    x, s2 = mix_fwd("mix", x, lw["mix"], jobs)
    x, s3 = ffn_fwd("ffn", x, lw["ffn2"], jobs)
    return x, (s1, s2, s3)


def layer_bwd(dx, saved, lw, jobs=None):
    s1, s2, s3 = saved
    dx, g3 = ffn_bwd("ffn", dx, s3, lw["ffn2"], jobs)
    dx, g2 = mix_bwd("mix", dx, s2, lw["mix"], jobs)
    dx, g1 = ffn_bwd("ffn", dx, s1, lw["ffn1"], jobs)
    return dx, dict(ffn1=g1, mix=g2, ffn2=g3)


_O_BA = W_QKV + W_Z
_O_GLU = _O_BA + 2 * HEADS


_MIX_BLK = P_IN // N_BLK
_MIX_B1 = _O_BA - _MIX_BLK
assert _O_BA + HEADS == 2 * _MIX_BLK
BLOCKED = ("ffn1_w_in", "ffn2_w_in", "mix_w_in")


def _row(v):
    return v.reshape(1, -1).astype(F32)


def prep_ffn(wl, k):
    w_in = wl[k + "_w_in"].astype(BF16)
    if w_in.ndim == 2:
        w_in = jnp.transpose(w_in.reshape(w_in.shape[0], N_BLK, -1), (1, 0, 2))
    return dict(norm_pre=_row(wl[k + "_norm_pre"]), norm_post=_row(wl[k + "_norm_post"]), w_in=w_in,
                w_out=wl[k + "_w_out"].astype(BF16))


def prep_layer(wl):
    return dict(ffn1=prep_ffn(wl, "ffn1"), mix=prep_mix(wl), ffn2=prep_ffn(wl, "ffn2"))


def prep_mix(wl):
    row = _row
    bf = lambda v: v.astype(BF16)
    lanes8 = lambda v: jnp.zeros((1, LANES), F32).at[0, HEADS:2 * HEADS].set(v.astype(F32))
    mw = bf(wl["mix_w_in"])
    pad = jnp.zeros((D_MODEL, LANES - 2 * HEADS), BF16)
    if mw.ndim == 3:
        w_all = jnp.concatenate([mw[0], mw[1][:, :_MIX_B1], mw[2][:, HEADS:], mw[3], mw[1][:, _MIX_B1:],
                                 mw[2][:, :HEADS], pad], axis=1)
    else:
        w_all = jnp.concatenate([mw[:, :_O_BA], mw[:, _O_GLU:], mw[:, _O_BA:_O_GLU], pad], axis=1)
    return dict(norm_pre=row(wl["mix_norm_pre"]), norm_post=row(wl["mix_norm_post"]), w_all=w_all,
               conv_w=wl["gdn_conv_w"].astype(F32), alog=lanes8(wl["gdn_a_log"]), dtb=lanes8(wl["gdn_dt_bias"]),
               gdn_norm_w=row(wl["gdn_norm_w"]), gdn_w_o=bf(wl["gdn_w_o"]), pw1_b=row(wl["cnv_pw1_b"]),
               dw_w=wl["cnv_dw_w"].astype(F32), dw_b=row(wl["cnv_dw_b"]), ln_g=row(wl["cnv_ln_g"]),
               ln_b=row(wl["cnv_ln_b"]), cnv_w_o=bf(wl["cnv_w_o"]), b_o=row(wl["cnv_b_o"]), w_out=bf(wl["mix_w_out"]))


def unprep_grads(g):
    return {**unprep_ffn(g["ffn1"], "ffn1"), **unprep_mix(g["mix"]), **unprep_ffn(g["ffn2"], "ffn2")}


def unprep_ffn(g, k):
    blk = g["w_in"]
    return {k + "_norm_pre": g["norm_pre"][0], k + "_norm_post": g["norm_post"][0], k + "_w_in#blocks": blk,
            k + "_w_in": jnp.transpose(blk, (1, 0, 2)).reshape(blk.shape[1], N_BLK * blk.shape[2]),
            k + "_w_out": g["w_out"]}


def unprep_mix(m):
    dqkv, dz, dpa, dpg, dga, dgb, dba = m["w_all"]
    out = {}
    out["mix_w_in#blocks"] = jnp.stack([
        dqkv[:, :_MIX_BLK], jnp.concatenate([dqkv[:, _MIX_BLK:], dz, dba[:, :HEADS]], axis=1),
        jnp.concatenate([dba[:, HEADS:2 * HEADS], dpa, dpg[:, :_MIX_B1 - D_MODEL]], axis=1),
        jnp.concatenate([dpg[:, _MIX_B1 - D_MODEL:], dga, dgb], axis=1)])
    out.update(
        mix_norm_pre=m["norm_pre"][0], mix_norm_post=m["norm_post"][0],
        mix_w_in=jnp.concatenate([dqkv, dz, dba[:, :2 * HEADS], dpa, dpg, dga, dgb], axis=1),
        gdn_conv_w=m["conv_w"], gdn_a_log=m["alog"][0, HEADS:2 * HEADS], gdn_dt_bias=m["dtb"][0, HEADS:2 * HEADS],
        gdn_norm_w=m["gdn_norm_w"][0], gdn_w_o=m["gdn_w_o"], cnv_pw1_b=m["pw1_b"][0], cnv_dw_w=m["dw_w"],
        cnv_dw_b=m["dw_b"][0], cnv_ln_g=m["ln_g"][0], cnv_ln_b=m["ln_b"][0], cnv_w_o=m["cnv_w_o"], cnv_b_o=m["b_o"][0],
        mix_w_out=m["w_out"])
    return out


MESH = pl.DeviceIdType.MESH
ANY = pl.BlockSpec(memory_space=pl.ANY)
N_DEV = 8


def _pos():
    return lax.axis_index("x"), lax.axis_index("y"), lax.axis_index("c")


def _other_chips(x, y):
    return [(1 - x, y), (x, 1 - y), (1 - x, 1 - y)]


class Job:
    def __init__(self, ins, outs, n_sems, copies, aliases=None):
        self.ins, self.outs, self.n_sems, self.copies = ins, outs, n_sems, copies
        self.aliases = aliases or {}
        self.results = None
        self.host = None

    def scratch(self):
        return [pltpu.SemaphoreType.DMA((self.n_sems,)), pltpu.SemaphoreType.DMA((self.n_sems,))]

    def start(self, in_refs, out_refs, sems):
        for cp in self.copies(in_refs, out_refs, sems, False):
            cp.start()

    def finish(self, in_refs, out_refs, sems):
        for cp in self.copies(in_refs, out_refs, sems, True):
            cp.wait_recv()
        for cp in self.copies(in_refs, out_refs, sems, False):
            cp.wait_send()


def run_job(name, job):
    n_i, n_o = len(job.ins), len(job.outs)

    def body(*refs):
        in_refs, out_refs, sems = refs[:n_i], refs[n_i:n_i + n_o], refs[n_i + n_o:]
        job.start(in_refs, out_refs, sems)
        job.finish(in_refs, out_refs, sems)

    job.results = pl.pallas_call(
        body, name=name, in_specs=[ANY] * n_i, out_specs=[ANY] * n_o, out_shape=job.outs,
        input_output_aliases=job.aliases, scratch_shapes=job.scratch())(*job.ins)
    return job.results


def _halved(rows):
    return rows % 32 == 0


def job_gather_ici(shards):
    n = len(shards)

    def copies(in_refs, out_refs, sems, recv):
        x, y, c = _pos()
        b = 2 * x + y
        chips = _other_chips(x, y)
        cps = []
        for a in range(n):
            hr = shards[a].shape[0] // 2
            for j in range(3):
                blk = 2 * chips[j][0] + chips[j][1] if recv else b
                if _halved(shards[a].shape[0]):
                    src, dst = in_refs[a].at[pl.ds(c * hr, hr)], out_refs[a].at[blk, pl.ds(c * hr, hr)]
                else:
                    src, dst = in_refs[a], out_refs[a].at[blk]
                cps.append(pltpu.make_async_remote_copy(
                    src_ref=src, dst_ref=dst, send_sem=sems[0].at[3 * a + j], recv_sem=sems[1].at[3 * a + j],
                    device_id=(chips[j][0], chips[j][1], c), device_id_type=MESH))
        return cps

    return Job(list(shards), [S((N_BLK,) + w.shape, w.dtype) for w in shards], 3 * n, copies)


def job_gather_sibling(lands):
    idx = [a for a, w in enumerate(lands) if _halved(w.shape[1])]

    def copies(in_refs, out_refs, sems, recv):
        x, y, c = _pos()
        chips = _other_chips(x, y)
        half = 1 - c if recv else c
        cps = []
        for pos, a in enumerate(idx):
            hr = lands[a].shape[1] // 2
            for j in range(3):
                rows = out_refs[a].at[2 * chips[j][0] + chips[j][1], pl.ds(half * hr, hr)]
                cps.append(pltpu.make_async_remote_copy(
                    src_ref=rows, dst_ref=rows, send_sem=sems[0].at[3 * pos + j], recv_sem=sems[1].at[3 * pos + j],
                    device_id=(x, y, 1 - c), device_id_type=MESH))
        return cps

    return Job(list(lands), [S(w.shape, w.dtype) for w in lands], 3 * len(idx), copies,
               aliases={a: a for a in range(len(lands))})


def job_rs_chips(ps):
    n = len(ps)

    def copies(in_refs, out_refs, sems, recv):
        x, y, c = _pos()
        b = 2 * x + y
        chips = _other_chips(x, y)
        cps = []
        for k in range(n):
            for j in range(3):
                other = 2 * chips[j][0] + chips[j][1]
                src_blk, dst_slot = (b, other) if recv else (other, b)
                cps.append(pltpu.make_async_remote_copy(
                    src_ref=in_refs[k].at[src_blk], dst_ref=out_refs[k].at[dst_slot], send_sem=sems[0].at[3 * k + j],
                    recv_sem=sems[1].at[3 * k + j], device_id=(chips[j][0], chips[j][1], c), device_id_type=MESH))
        return cps

    return Job(list(ps), [S(p.shape, p.dtype) for p in ps], 3 * n, copies)


def rs_sibling(gs):
    n = len(gs)

    def body(*refs):
        g_refs, r_refs = refs[:n], refs[n:2 * n]
        send_sems, recv_sems = refs[2 * n:]
        x, y, c = _pos()

        def cp(k):
            hr = gs[k].shape[1] // 2
            return pltpu.make_async_remote_copy(
                src_ref=g_refs[k].at[:, pl.ds((1 - c) * hr, hr)], dst_ref=r_refs[k], send_sem=send_sems.at[k],
                recv_sem=recv_sems.at[k], device_id=(x, y, 1 - c), device_id_type=MESH)

        cps = [cp(k) for k in range(n)]
        for d in cps:
            d.start()
        for d in cps:
            d.wait_recv()
        for d in cps:
            d.wait_send()

    return pl.pallas_call(
        body, name="rs_sibling", in_specs=[ANY] * n, out_specs=[ANY] * n,
        out_shape=[S((N_BLK, g.shape[1] // 2, g.shape[2]), g.dtype) for g in gs],
        scratch_shapes=[pltpu.SemaphoreType.DMA((n,)), pltpu.SemaphoreType.DMA((n,))])(*gs)


def ag_sibling(fs):
    n = len(fs)

    def body(*refs):
        o_refs = refs[n:2 * n]
        send_sems, recv_sems = refs[2 * n:]
        x, y, c = _pos()

        def cp(k, half):
            hr = fs[k].shape[0] // 2
            rows = o_refs[k].at[pl.ds(half * hr, hr)]
            return pltpu.make_async_remote_copy(
                src_ref=rows, dst_ref=rows, send_sem=send_sems.at[k], recv_sem=recv_sems.at[k],
                device_id=(x, y, 1 - c), device_id_type=MESH)

        cps = [cp(k, c) for k in range(n)]
        for d in cps:
            d.start()
        for k in range(n):
            cp(k, 1 - c).wait_recv()
        for d in cps:
            d.wait_send()

    return pl.pallas_call(
        body, name="ag_sibling", in_specs=[ANY] * n, out_specs=[ANY] * n,
        out_shape=[S(f.shape, f.dtype) for f in fs], input_output_aliases={k: k for k in range(n)},
        scratch_shapes=[pltpu.SemaphoreType.DMA((n,)), pltpu.SemaphoreType.DMA((n,))])(*fs)


def allreduce_small(v):
    rows = v.shape[0]

    def body(v_ref, o_ref, buf, send_sems, recv_sems):
        x, y, c = _pos()
        me = 4 * x + 2 * y + c
        buf[me] = v_ref[...]

        def cp(d, slot):
            dx, dy, dc = (d >> 2) & 1, (d >> 1) & 1, d & 1
            peer = (1 - x if dx else x, 1 - y if dy else y, 1 - c if dc else c)
            return pltpu.make_async_remote_copy(
                src_ref=v_ref, dst_ref=buf.at[slot], send_sem=send_sems.at[d - 1], recv_sem=recv_sems.at[d - 1],
                device_id=peer, device_id_type=MESH)

        cps = [cp(d, me) for d in range(1, N_DEV)]
        for d in cps:
            d.start()
        for d in range(1, N_DEV):
            dx, dy, dc = (d >> 2) & 1, (d >> 1) & 1, d & 1
            src = 4 * (1 - x if dx else x) + 2 * (1 - y if dy else y) + (1 - c if dc else c)
            cp(d, src).wait_recv()
        for d in cps:
            d.wait_send()
        acc = buf[0]
        for s in range(1, N_DEV):
            acc = acc + buf[s]
        o_ref[...] = acc

    vm = pl.BlockSpec(memory_space=pltpu.VMEM)
    return pl.pallas_call(
        body, name="allreduce_small", in_specs=[vm], out_specs=vm, out_shape=S(v.shape, v.dtype),
        scratch_shapes=[pltpu.VMEM((N_DEV, rows, LANES), F32), pltpu.SemaphoreType.DMA((N_DEV - 1,)),
                        pltpu.SemaphoreType.DMA((N_DEV - 1,))])(v)


def _rows_tile(rows, cols, cap_bytes=1 << 20, mult=8):
    best = None
    for t in range(mult, rows + 1, mult):
        if rows % t == 0 and t * cols * 4 <= cap_bytes:
            best = t
    return best if best is not None else rows


def add_half(name, g, r, c_arr):
    _, hr, cols = r.shape
    tr = _rows_tile(hr, cols, mult=16)
    nb = hr // tr

    def body(c_ref, g_ref, r_ref, o_ref):
        o_ref[...] = (g_ref[...] + r_ref[...]).astype(o_ref.dtype)

    gs = pltpu.PrefetchScalarGridSpec(
        num_scalar_prefetch=1, grid=(N_BLK, nb),
        in_specs=[pl.BlockSpec((None, tr, cols), lambda b, i, cr: (b, cr[0] * nb + i, 0)),
                  pl.BlockSpec((None, tr, cols), lambda b, i, cr: (b, i, 0))],
        out_specs=pl.BlockSpec((None, tr, cols), lambda b, i, cr: (b, i, 0)))
    return pl.pallas_call(body, name=name, grid_spec=gs, out_shape=S(r.shape, BF16),
                          compiler_params=_cp(("parallel", "parallel")))(c_arr, g, r)


def sum_chips(name, r, own, cb_arr):
    _, hr, cols = r.shape
    tr = _rows_tile(hr, cols, mult=16)
    nb = hr // tr

    def body(cb_ref, *refs):
        o_ref = refs[N_BLK + 1]
        b = cb_ref[1]
        acc = None
        for s in range(N_BLK):
            term = jnp.where(b == s, refs[N_BLK][...], refs[s][...]).astype(F32)
            acc = term if acc is None else acc + term
        o_ref[...] = acc

    slot = lambda s: pl.BlockSpec((None, tr, cols), lambda i, cb: (jnp.where(cb[1] == s, (s + 1) % N_BLK, s), i, 0))
    gs = pltpu.PrefetchScalarGridSpec(
        num_scalar_prefetch=1, grid=(nb,),
        in_specs=[slot(s) for s in range(N_BLK)] + [pl.BlockSpec((None, tr, cols), lambda i, cb: (cb[1], i, 0))],
        out_specs=pl.BlockSpec((tr, cols), lambda i, cb: (cb[0] * nb + i, 0)))
    return pl.pallas_call(body, name=name, grid_spec=gs, out_shape=S((2 * hr, cols), F32),
                          compiler_params=_cp(("parallel",)))(cb_arr, *([r] * N_BLK), own)


def adamw(name, w, m, v, gs):
    rows, cols = w.shape
    two = len(gs) == 2
    span = rows // 2 if two else rows
    tr = _rows_tile(span, cols, 1 << 19)
    nb = span // tr

    def body(w_ref, m_ref, v_ref, *rest):
        g_refs, (go_ref, d_ref, mo_ref, vo_ref) = rest[:len(gs)], rest[len(gs):]
        if two:
            g = jnp.where(pl.program_id(0) < nb, g_refs[0][...], g_refs[1][...])
        else:
            g = g_refs[0][...]
        mn = ADAM_B1 * m_ref[...] + (1.0 - ADAM_B1) * g
        vn = ADAM_B2 * v_ref[...] + (1.0 - ADAM_B2) * jnp.square(g)
        m_hat = mn / (1.0 - ADAM_B1 ** ADAM_STEP)
        v_hat = vn / (1.0 - ADAM_B2 ** ADAM_STEP)
        go_ref[...] = g
        d_ref[...] = -ADAM_LR * (m_hat / (jnp.sqrt(v_hat) + ADAM_EPS) + ADAM_WD * w_ref[...])
        mo_ref[...] = mn
        vo_ref[...] = vn

    full = pl.BlockSpec((tr, cols), lambda i: (i, 0))
    if two:
        g_specs = [pl.BlockSpec((tr, cols), lambda i: (jnp.minimum(i, nb - 1), 0)),
                   pl.BlockSpec((tr, cols), lambda i: (jnp.maximum(i - nb, 0), 0))]
    else:
        g_specs = [full]
    return pl.pallas_call(
        body, name=name, grid=(2 * nb if two else nb,), in_specs=[full, full, full] + g_specs, out_specs=[full] * 4,
        out_shape=[S((rows, cols), F32)] * 4, compiler_params=_cp(("parallel",)))(w, m, v, *gs)


WEIGHTS = ["ffn1_norm_pre", "ffn1_norm_post", "ffn1_w_in", "ffn1_w_out", "mix_norm_pre", "mix_norm_post", "mix_w_in",
           "gdn_conv_w", "gdn_a_log", "gdn_dt_bias", "gdn_norm_w", "gdn_w_o", "cnv_pw1_b", "cnv_dw_w", "cnv_dw_b",
           "cnv_ln_g", "cnv_ln_b", "cnv_w_o", "cnv_b_o", "mix_w_out", "ffn2_norm_pre", "ffn2_norm_post", "ffn2_w_in",
           "ffn2_w_out"]
BIG = {"ffn1_w_in": True, "ffn1_w_out": False, "mix_w_in": True, "gdn_conv_w": True, "gdn_w_o": False,
       "cnv_dw_w": True, "cnv_w_o": False, "mix_w_out": False, "ffn2_w_in": True, "ffn2_w_out": False}
TINY = {"gdn_conv_w": (32, LANES), "cnv_dw_w": (64, LANES)}
SMALL = [n for n in WEIGHTS if n not in BIG]
SUB = {"ffn1": ["ffn1_w_in", "ffn1_w_out"], "ffn2": ["ffn2_w_in", "ffn2_w_out"],
       "mix": ["mix_w_in", "gdn_conv_w", "gdn_w_o", "cnv_dw_w", "cnv_w_o", "mix_w_out"]}
GATHER_ON_FFN_MIX = [("ffn_in", ["mix_w_in"]), ("ffn_out", SUB["mix"][1:])]
RS_ON_FFN_MIX = [("ffn_in_bx", ["mix_w_in"]), ("ffn_out_bx", SUB["mix"][1:])]


def _whole(name, blocks):
    if BIG[name]:
        return jnp.transpose(blocks, (1, 0, 2)).reshape(blocks.shape[1], N_BLK * blocks.shape[2])
    return blocks.reshape(N_BLK * blocks.shape[1], blocks.shape[2])


def _blocks(name, whole):
    r, cfull = whole.shape
    if BIG[name]:
        blk = jnp.transpose(whole.reshape(r, N_BLK, cfull // N_BLK), (1, 0, 2))
    else:
        blk = whole.reshape(N_BLK, r // N_BLK, cfull)
    if name in TINY:
        tr, tc = TINY[name]
        flat = blk.reshape(N_BLK, -1)
        blk = jnp.pad(flat, ((0, 0), (0, tr * tc - flat.shape[1]))).reshape(N_BLK, tr, tc)
    return blk


def _pack(parts):
    rows = []
    for p in parts:
        flat = p.reshape(-1).astype(F32)
        rows.append(jnp.pad(flat, (0, (-flat.shape[0]) % LANES)).reshape(-1, LANES))
    out = jnp.concatenate(rows, axis=0)
    return jnp.pad(out, ((0, (-out.shape[0]) % 8), (0, 0)))


def _unpack(packed, shapes):
    out, r = [], 0
    for shp in shapes:
        size = math.prod(shp)
        nr = -(-size // LANES)
        out.append(packed[r:r + nr].reshape(-1)[:size].reshape(shp))
        r += nr
    return out


def kernel(x, ffn1_norm_pre, ffn1_norm_post, ffn1_w_in, ffn1_w_out, mix_norm_pre, mix_norm_post, mix_w_in, gdn_conv_w, gdn_a_log, gdn_dt_bias, gdn_norm_w, gdn_w_o, cnv_pw1_b, cnv_dw_w, cnv_dw_b, cnv_ln_g, cnv_ln_b, cnv_w_o, cnv_b_o, mix_w_out, ffn2_norm_pre, ffn2_norm_post, ffn2_w_in, ffn2_w_out, loss_target, m_ffn1_norm_pre, m_ffn1_norm_post, m_ffn1_w_in, m_ffn1_w_out, m_mix_norm_pre, m_mix_norm_post, m_mix_w_in, m_gdn_conv_w, m_gdn_a_log, m_gdn_dt_bias, m_gdn_norm_w, m_gdn_w_o, m_cnv_pw1_b, m_cnv_dw_w, m_cnv_dw_b, m_cnv_ln_g, m_cnv_ln_b, m_cnv_w_o, m_cnv_b_o, m_mix_w_out, m_ffn2_norm_pre, m_ffn2_norm_post, m_ffn2_w_in, m_ffn2_w_out, v_ffn1_norm_pre, v_ffn1_norm_post, v_ffn1_w_in, v_ffn1_w_out, v_mix_norm_pre, v_mix_norm_post, v_mix_w_in, v_gdn_conv_w, v_gdn_a_log, v_gdn_dt_bias, v_gdn_norm_w, v_gdn_w_o, v_cnv_pw1_b, v_cnv_dw_w, v_cnv_dw_b, v_cnv_ln_g, v_cnv_ln_b, v_cnv_w_o, v_cnv_b_o, v_mix_w_out, v_ffn2_norm_pre, v_ffn2_norm_post, v_ffn2_w_in, v_ffn2_w_out):
    args = locals()
    wts = {n: args[n] for n in WEIGHTS}
    mom = {n: args["m_" + n] for n in WEIGHTS}
    var = {n: args["v_" + n] for n in WEIGHTS}
    big = list(BIG)

    mx, my, mc = _pos()
    mb = 2 * mx + my
    cb_arr = jnp.stack([mc, mb]).astype(jnp.int32)

    own = {(n, l): wts[n][l].astype(BF16) for n in big for l in range(DEPTH)}

    def planned(make_job, arrays, plan):
        jobs = []
        for host, names in plan:
            jb = make_job([arrays[n] for n in names])
            jb.host, jb.names = host, names
            jobs.append(jb)
        return jobs

    def landed(jobs):
        res = {}
        for jb in jobs:
            if jb.results is None:
                run_job("comm_alone", jb)
            res.update(zip(jb.names, jb.results))
        return res

    stages = [(l, s) for l in range(DEPTH) for s in ("ffn1", "mix", "ffn2")]
    carried_by_ffn = lambda s: GATHER_ON_FFN_MIX if s == "mix" else [("ffn_in", [s + "_w_in"]), ("ffn_out", [s + "_w_out"])]
    carried_by_mix = lambda s: [("mix_in", [s + "_w_in", s + "_w_out"])]

    def gather_jobs(l, s, carrier):
        plan = carried_by_mix(s) if carrier == "mix" else carried_by_ffn(s)
        return planned(job_gather_ici, {n: own[(n, l)] for n in SUB[s]}, plan)

    def sub_weights(l, s, jobs):
        lands = landed(jobs)
        names = SUB[s]
        lands = dict(zip(names, run_job("gather_sib", job_gather_sibling([lands[n] for n in names]))))
        wl = {}
        for n in names:
            blocks = lax.dynamic_update_index_in_dim(lands[n], own[(n, l)], mb, 0)
            wl[n] = blocks if n in BLOCKED else _whole(n, blocks)
        wl.update({n: wts[n][l] for n in SMALL})
        return prep_mix(wl) if s == "mix" else prep_ffn(wl, s)

    act = x[0]
    saved, weights = {}, {}
    jobs = gather_jobs(0, "ffn1", "ffn")
    for i, (l, s) in enumerate(stages):
        weights[(l, s)] = sub_weights(l, s, jobs)
        jobs = gather_jobs(*stages[i + 1], "mix" if s == "mix" else "ffn") if i + 1 < len(stages) else []
        pending = list(jobs)
        fwd = mix_fwd if s == "mix" else ffn_fwd
        act, saved[(l, s)] = fwd("mix" if s == "mix" else "ffn", act, weights[(l, s)], pending)
    dx, loss = loss_head("loss", act, loss_target[0], _tm(act.shape[0]))

    def rs_jobs(l, s, g, carrier):
        if s == "mix":
            gw_s = unprep_mix(g)
        else:
            gw_s = unprep_ffn(g, s)
        small_grads[l].update({n: gw_s[n] for n in gw_s if n in SMALL})
        names = SUB[s]
        blocks = [gw_s[n + "#blocks"] if n in BLOCKED else _blocks(n, gw_s[n]) for n in names]
        parts = [add_half("add_half", b_, r, cb_arr) for b_, r in zip(blocks, rs_sibling(blocks))]
        partial_of.update({(n, l): p for n, p in zip(names, parts)})
        if carrier is None:
            plan = [(None, names)]
        elif carrier == "mix":
            plan = [("gintra_b", names)]
        elif s == "mix":
            plan = RS_ON_FFN_MIX
        else:
            plan = [("ffn_in_bx", [s + "_w_in"]), ("ffn_out_bx", [s + "_w_out"])]
        return planned(job_rs_chips, dict(zip(names, parts)), plan)

    small_grads = [{} for _ in range(DEPTH)]
    partial_of, chip_of = {}, {}
    jobs, jobs_key = [], None
    for i, (l, s) in enumerate(reversed(stages)):
        pending = list(jobs)
        bwd = mix_bwd if s == "mix" else ffn_bwd
        dx, g = bwd("mix" if s == "mix" else "ffn", dx, saved[(l, s)], weights[(l, s)], pending)
        if jobs:
            chip_of.update({(n, jobs_key): r for n, r in landed(jobs).items()})
        nxt = list(reversed(stages))[i + 1][1] if i + 1 < len(stages) else None
        jobs, jobs_key = rs_jobs(l, s, g, None if nxt is None else ("mix" if nxt == "mix" else "ffn")), l
    chip_of.update({(n, jobs_key): r for n, r in landed(jobs).items()})
    gw = small_grads

    small_shapes = [wts[n].shape for n in SMALL]
    packed = _pack([jnp.stack([gw[l][n] for l in range(DEPTH)]) for n in SMALL] + [loss])
    total = allreduce_small(packed)
    small_g = dict(zip(SMALL, _unpack(total, small_shapes)))
    loss_sum = total[sum(-(-math.prod(s) // LANES) for s in small_shapes), 0]

    keys = [(n, l) for l in range(DEPTH) for n in big]
    halves = [sum_chips("sum_chips", chip_of[k], partial_of[k], cb_arr) for k in keys]
    summed = dict(zip(keys, ag_sibling(halves)))

    out_g, out_d, out_m, out_v = {}, {}, {}, {}
    for n in big:
        shp = wts[n].shape
        gs = [summed[(n, l)] for l in range(DEPTH)]
        if n in TINY:
            gs = [jnp.concatenate([g.reshape(-1)[:shp[1] * shp[2]].reshape(shp[1], shp[2]) for g in gs], axis=0)]
        two_d = lambda a: a.reshape(DEPTH * shp[1], shp[2])
        res = adamw("adamw", two_d(wts[n]), two_d(mom[n]), two_d(var[n]), gs)
        out_g[n], out_d[n], out_m[n], out_v[n] = [r.reshape(shp) for r in res]

    pk = lambda d: _pack([d[n] for n in SMALL])
    res = adamw("adamw_small", pk(wts), pk(mom), pk(var), [pk(small_g)])
    for d, r in zip((out_g, out_d, out_m, out_v), res):
        d.update(dict(zip(SMALL, _unpack(r, small_shapes))))

    return (loss_sum, dx[None], *[out_g[n] for n in WEIGHTS], *[out_d[n] for n in WEIGHTS],
            *[out_m[n] for n in WEIGHTS], *[out_v[n] for n in WEIGHTS])
```

```python
import functools
import math

import jax
import jax.numpy as jnp
from jax import lax
from jax.experimental import pallas as pl
from jax.experimental.pallas import tpu as pltpu

F32, BF16 = jnp.float32, jnp.bfloat16
S = jax.ShapeDtypeStruct

D_MODEL = 1024
D_FF = 2816
HEADS = 8
DK = 128
CHUNK = 64
GDN_CONV = 4
CNV_K = 31
W_QKV = 3 * HEADS * DK
W_Z = HEADS * DK
W_GLU = 2 * D_MODEL
W_GATE = 2 * D_MODEL
P_IN = W_QKV + W_Z + 2 * HEADS + W_GLU + W_GATE
LANES = 128
P_ALL = W_QKV + W_Z + W_GLU + W_GATE + LANES
COL_Z = W_QKV // LANES
COL_GLU = (W_QKV + W_Z) // LANES
COL_GATE = (W_QKV + W_Z + W_GLU) // LANES
COL_BA = (W_QKV + W_Z + W_GLU + W_GATE) // LANES
RMS_EPS = 1e-6
LN_EPS = 1e-5
DEPTH = 2
N_BLK = 4
VMEM_LIMIT = 56 * 1024 * 1024
GRAD_DT = BF16

ADAM_LR, ADAM_B1, ADAM_B2, ADAM_EPS, ADAM_WD, ADAM_STEP = 0.001, 0.9, 0.999, 1e-08, 0.01, 10


def _cp(sem):
    return pltpu.CompilerParams(dimension_semantics=sem, vmem_limit_bytes=VMEM_LIMIT)


MM_VMEM_BUDGET = 36 * 1024 * 1024


def _mm_tiles(m, n, k_bytes_a, k_bytes_b, out_bytes, tn_fixed=None):
    best = None
    for tm in (1024, 512, 256, 128):
        if m % tm:
            continue
        for tn in ((tn_fixed,) if tn_fixed else (1024, 512, 640, 256, 384, 128)):
            if n % tn:
                continue
            need = 2 * (tm * k_bytes_a + tn * k_bytes_b + tm * tn * out_bytes)
            if need <= MM_VMEM_BUDGET and (best is None or tm * tn > best[0] * best[1]):
                best = (tm, tn)
    if best is None:
        raise ValueError((m, n, k_bytes_a, k_bytes_b))
    return best


def mm_nt_sum(name, parts, b):
    m, n = parts[0][0].shape[0], b.shape[0]
    k_total = sum(a.shape[1] for a, _ in parts)
    tm, tn = _mm_tiles(m, n, k_total * 2, k_total * 2, 4)
    n_p = len(parts)

    def body(*refs):
        o_ref = refs[2 * n_p]
        acc = None
        for a_ref, b_ref in zip(refs[:n_p], refs[n_p:2 * n_p]):
            t = lax.dot_general(a_ref[...], b_ref[...], (((1,), (1,)), ((), ())), preferred_element_type=F32)
            acc = t if acc is None else acc + t
        o_ref[...] = acc

    a_specs = [pl.BlockSpec((tm, a.shape[1]), lambda i, j: (i, 0)) for a, _ in parts]
    b_specs = [pl.BlockSpec((tn, a.shape[1]), functools.partial(lambda i, j, c: (j, c), c=col)) for a, col in parts]
    return pl.pallas_call(
        body, name=name, grid=(m // tm, n // tn), in_specs=a_specs + b_specs,
        out_specs=pl.BlockSpec((tm, tn), lambda i, j: (i, j)), out_shape=S((m, n), F32),
        compiler_params=_cp(("parallel", "parallel")))(*[a for a, _ in parts], *([b] * n_p))


def _take(jobs, host):
    for jb in jobs or []:
        if jb.host == host:
            jobs.remove(jb)
            return jb
    return None


def _pcall(body, job, *, name, grid, in_specs, out_specs, out_shape, scratch_shapes=(), semantics):
    in_specs, out_specs, out_shape, scratch_shapes = list(in_specs), list(out_specs), list(out_shape), list(scratch_shapes)
    if job is None:
        return lambda *args: pl.pallas_call(
            body, name=name, grid=grid, in_specs=in_specs, out_specs=out_specs, out_shape=out_shape,
            scratch_shapes=scratch_shapes, compiler_params=_cp(semantics))(*args)
    n_in, n_out, n_sc = len(in_specs), len(out_specs), len(scratch_shapes)
    n_i, n_o = len(job.ins), len(job.outs)

    def wrapped(*refs):
        cut = [n_in, n_i, n_out, n_o, n_sc]
        parts, pos = [], 0
        for c in cut:
            parts.append(refs[pos:pos + c])
            pos += c
        ins, j_in, outs, j_out, own = parts
        sems = refs[pos:]
        ids = [pl.program_id(d) for d in range(len(grid))]
        first = functools.reduce(lambda p, q: p & q, [i == 0 for i in ids])
        last = functools.reduce(lambda p, q: p & q, [i == g - 1 for i, g in zip(ids, grid)])

        @pl.when(first)
        def _():
            job.start(j_in, j_out, sems)

        body(*ins, *outs, *own)

        @pl.when(last)
        def _():
            job.finish(j_in, j_out, sems)

    def call(*args):
        res = pl.pallas_call(
            wrapped, name=name + "_c", grid=grid, in_specs=in_specs + [ANY] * n_i, out_specs=out_specs + [ANY] * n_o,
            out_shape=out_shape + list(job.outs),
            input_output_aliases={n_in + ki: n_out + ko for ki, ko in job.aliases.items()},
            scratch_shapes=scratch_shapes + job.scratch(),
            compiler_params=_cp(("arbitrary",) * len(grid)))(*args, *job.ins)
        job.results = res[n_out:]
        return res[:n_out]

    return call


def mm(name, a, b, ta=False, tb=False, out_dtype=F32, job=None, out_blocked=False):
    k = a.shape[0] if ta else a.shape[1]
    m = a.shape[1] if ta else a.shape[0]
    blocked = b.ndim == 3
    cb = b.shape[2] if blocked else None
    osz = jnp.dtype(out_dtype).itemsize
    if blocked and not tb:
        n = N_BLK * cb
        tm, tn = _mm_tiles(m, n, k * 2, k * 2, osz, tn_fixed=cb)
        b_spec = pl.BlockSpec((None, k, cb), lambda i, j: (j, 0, 0))
    elif blocked:
        n = b.shape[1]
        assert k == N_BLK * cb, (name, a.shape, b.shape)
        tm, tn = _mm_tiles(m, n, k * 2, k * 2, osz)
        b_spec = pl.BlockSpec((N_BLK, tn, cb), lambda i, j: (0, j, 0))
    else:
        n = b.shape[0] if tb else b.shape[1]
        assert k == (b.shape[1] if tb else b.shape[0]), (name, a.shape, b.shape)
        tm, tn = _mm_tiles(m, n, k * 2, k * 2, osz, tn_fixed=n // N_BLK if out_blocked else None)
        b_spec = pl.BlockSpec((tn, k), lambda i, j: (j, 0)) if tb else pl.BlockSpec((k, tn), lambda i, j: (0, j))
    a_spec = pl.BlockSpec((k, tm), lambda i, j: (0, i)) if ta else pl.BlockSpec((tm, k), lambda i, j: (i, 0))
    if out_blocked:
        o_spec, o_shape = pl.BlockSpec((None, tm, tn), lambda i, j: (j, i, 0)), S((N_BLK, m, tn), out_dtype)
    else:
        o_spec, o_shape = pl.BlockSpec((tm, tn), lambda i, j: (i, j)), S((m, n), out_dtype)
    dims = (((0 if ta else 1,), (1 if tb else 0,)), ((), ()))
    gm, gn = m // tm, n // tn

    def product(a_ref, b_ref):
        if blocked and tb:
            acc = None
            for q in range(N_BLK):
                t = lax.dot_general(a_ref[:, q * cb:(q + 1) * cb], b_ref[q], (((1,), (1,)), ((), ())),
                                    preferred_element_type=F32)
                acc = t if acc is None else acc + t
            return acc
        return lax.dot_general(a_ref[...], b_ref[...], dims, preferred_element_type=F32)

    def body(a_ref, b_ref, o_ref):
        o_ref[...] = product(a_ref, b_ref).astype(o_ref.dtype)

    return _pcall(body, job, name=name, grid=(gm, gn), in_specs=[a_spec, b_spec], out_specs=[o_spec],
                  out_shape=[o_shape], semantics=("parallel", "parallel"))(a, b)[0]


def ew_fwd(name, fn, grid, ins, outs):
    n_in = len(ins)

    def body(*refs):
        vals = [r[...].astype(F32) for r in refs[:n_in]]
        res = fn(pl.program_id(0), *vals)
        for r, v in zip(refs[n_in:], res):
            r[...] = v.astype(r.dtype)

    out = pl.pallas_call(
        body, name=name, grid=grid, in_specs=[s for _, s in ins], out_specs=[s for _, s in outs],
        out_shape=[sd for sd, _ in outs], compiler_params=_cp(("parallel", "parallel")))(*[a for a, _ in ins])
    return out


def ew_bwd(name, fn, grid, ins, cts, wrt, acc, add=None):
    n_in, n_ct, n_wrt, n_acc = len(ins), len(cts), len(wrt), len(acc)
    has_add = add is not None

    def body(*refs):
        in_refs = refs[:n_in]
        ct_refs = refs[n_in:n_in + n_ct]
        pos = n_in + n_ct
        add_ref = refs[pos] if has_add else None
        pos += 1 if has_add else 0
        wrt_refs = refs[pos:pos + n_wrt]
        acc_refs = refs[pos + n_wrt:pos + n_wrt + n_acc]
        col, tok = pl.program_id(0), pl.program_id(1)
        vals = [r[...].astype(F32) for r in in_refs]
        _, vjp = jax.vjp(lambda *a: fn(col, *a), *vals)
        grads = vjp(tuple(c[...].astype(F32) for c in ct_refs))
        for pos_w, ((idx, _, _), r) in enumerate(zip(wrt, wrt_refs)):
            g = grads[idx]
            if has_add and pos_w == 0:
                g = g + add_ref[...]
            r[...] = g.astype(r.dtype)
        for (idx, _, _, over_cols), r in zip(acc, acc_refs):
            first = (tok == 0) & (col == 0) if over_cols else tok == 0

            @pl.when(first)
            def _():
                r[...] = jnp.zeros_like(r)

            r[...] += grads[idx]

    arrays = [a for a, _ in ins] + [a for a, _ in cts] + ([add[0]] if has_add else [])
    in_specs = [s for _, s in ins] + [s for _, s in cts] + ([add[1]] if has_add else [])
    over_any = any(o for *_, o in acc)
    out = pl.pallas_call(
        body, name=name, grid=grid, in_specs=in_specs,
        out_specs=[s for _, _, s in wrt] + [s for _, _, s, _ in acc],
        out_shape=[sd for _, sd, _ in wrt] + [sd for _, sd, _, _ in acc],
        compiler_params=_cp(("arbitrary" if over_any else "parallel", "arbitrary")))(*arrays)
    return out


def _tok(width, col=0):
    return lambda tm: pl.BlockSpec((tm, width), lambda j, i: (i, col))


def _tokcol(off=0):
    return lambda tm: pl.BlockSpec((tm, LANES), lambda j, i: (i, off + j))


def _par(width, col=0):
    return pl.BlockSpec((1, width), lambda j, i: (0, col))


def _parcol(off=0):
    return pl.BlockSpec((1, LANES), lambda j, i: (0, off + j))


def _rms(x, w, eps=RMS_EPS):
    return x * lax.rsqrt(jnp.mean(x * x, axis=-1, keepdims=True) + eps) * w


def _silu(x):
    return x * jax.nn.sigmoid(x)


def fn_rms(col, x, w):
    return (_rms(x, w),)


def fn_swiglu(col, gate, up):
    return (_silu(gate) * up,)


def swiglu_bwd(name, u, da, tm):
    n_tok, f2 = u.shape
    f = f2 // 2

    def body(g_ref, up_ref, da_ref, o_ref):
        g, d = g_ref[...].astype(F32), da_ref[...].astype(F32)
        s = jax.nn.sigmoid(g)
        o_ref[:, :f] = (d * up_ref[...].astype(F32) * (s * (1.0 + g * (1.0 - s)))).astype(o_ref.dtype)
        o_ref[:, f:] = (d * (g * s)).astype(o_ref.dtype)

    half = lambda c: pl.BlockSpec((tm, f), lambda i: (i, c))
    return pl.pallas_call(
        body, name=name, grid=(n_tok // tm,), in_specs=[half(0), half(1), half(0)],
        out_specs=pl.BlockSpec((tm, f2), lambda i: (i, 0)), out_shape=S((n_tok, f2), BF16),
        compiler_params=_cp(("parallel",)))(u, u, da)


def fn_gdnpost(col, c):
    typ = col // HEADS
    y = _silu(c)
    n = y * lax.rsqrt(jnp.sum(y * y, axis=-1, keepdims=True) + 1e-6)
    n = n * jnp.where(typ == 0, DK ** -0.5, 1.0)
    return (jnp.where(typ < 2, n, y),)


def fn_gdnout(col, o, z, nw):
    return (_rms(o, nw) * _silu(z),)


def fn_glu(col, a, g, ba, bg):
    return ((a + ba) * jax.nn.sigmoid(g + bg),)


def fn_lnsilu(col, h, g, b):
    mu = jnp.mean(h, axis=-1, keepdims=True)
    var = jnp.mean(jnp.square(h - mu), axis=-1, keepdims=True)
    return (_silu((h - mu) * lax.rsqrt(var + LN_EPS) * g + b),)


def fn_merge(col, ya, yb, ga, gb, bo):
    return (jax.nn.sigmoid(ga) * ya + jax.nn.sigmoid(gb) * (yb + bo),)


HALO = 32


def conv_fwd(name, x, col_off, n_ch, w, bias, tb):
    n_tok = x.shape[0]
    k = w.shape[0]
    nt = n_tok // tb

    def body(xp_ref, xc_ref, w_ref, *rest):
        if bias is not None:
            b_ref, o_ref, xs = rest
        else:
            o_ref, xs = rest
        i = pl.program_id(1)
        xs[0:HALO, :] = jnp.where(i == 0, 0.0, xp_ref[tb - HALO:tb, :].astype(F32))
        xs[HALO:HALO + tb, :] = xc_ref[...].astype(F32)
        acc = jnp.zeros((tb, LANES), F32)
        for j in range(k):
            s = k - 1 - j
            acc = acc + w_ref[j:j + 1, :] * xs[HALO - s:HALO - s + tb, :]
        if bias is not None:
            acc = acc + b_ref[...]
        o_ref[...] = acc

    in_specs = [pl.BlockSpec((tb, LANES), lambda j, i: (jnp.maximum(i - 1, 0), col_off + j)),
                pl.BlockSpec((tb, LANES), lambda j, i: (i, col_off + j)),
                pl.BlockSpec((k, LANES), lambda j, i: (0, j))]
    args = [x, x, w]
    if bias is not None:
        in_specs.append(pl.BlockSpec((1, LANES), lambda j, i: (0, j)))
        args.append(bias)
    return pl.pallas_call(
        body, name=name, grid=(n_ch // LANES, nt), in_specs=in_specs,
        out_specs=pl.BlockSpec((tb, LANES), lambda j, i: (i, j)), out_shape=S((n_tok, n_ch), F32),
        scratch_shapes=[pltpu.VMEM((HALO + tb, LANES), F32)],
        compiler_params=_cp(("parallel", "parallel")))(*args)


def conv_bwd(name, x, col_off, n_ch, w, dy, dx_dtype, tb):
    n_tok = x.shape[0]
    k = w.shape[0]
    nt = n_tok // tb

    def body(xp_ref, xc_ref, w_ref, dyc_ref, dyn_ref, dx_ref, dw_ref, db_ref, xs, dys):
        i = pl.program_id(1)
        xs[0:HALO, :] = jnp.where(i == 0, 0.0, xp_ref[tb - HALO:tb, :].astype(F32))
        xs[HALO:HALO + tb, :] = xc_ref[...].astype(F32)
        dyc = dyc_ref[...]
        dys[0:tb, :] = dyc
        dys[tb:tb + HALO, :] = jnp.where(i == nt - 1, 0.0, dyn_ref[0:HALO, :])

        @pl.when(i == 0)
        def _():
            dw_ref[...] = jnp.zeros_like(dw_ref)
            db_ref[...] = jnp.zeros_like(db_ref)

        acc = jnp.zeros((tb, LANES), F32)
        for j in range(k):
            s = k - 1 - j
            acc = acc + w_ref[j:j + 1, :] * dys[s:s + tb, :]
            dw_ref[j:j + 1, :] += jnp.sum(dyc * xs[HALO - s:HALO - s + tb, :], axis=0, keepdims=True)
        dx_ref[...] = acc.astype(dx_ref.dtype)
        db_ref[...] += jnp.sum(dyc, axis=0, keepdims=True)

    in_specs = [pl.BlockSpec((tb, LANES), lambda j, i: (jnp.maximum(i - 1, 0), col_off + j)),
                pl.BlockSpec((tb, LANES), lambda j, i: (i, col_off + j)),
                pl.BlockSpec((k, LANES), lambda j, i: (0, j)),
                pl.BlockSpec((tb, LANES), lambda j, i: (i, j)),
                pl.BlockSpec((tb, LANES), lambda j, i: (jnp.minimum(i + 1, nt - 1), j))]
    return pl.pallas_call(
        body, name=name, grid=(n_ch // LANES, nt), in_specs=in_specs,
        out_specs=[pl.BlockSpec((tb, LANES), lambda j, i: (i, j)),
                   pl.BlockSpec((k, LANES), lambda j, i: (0, j)),
                   pl.BlockSpec((1, LANES), lambda j, i: (0, j))],
        out_shape=[S((n_tok, n_ch), dx_dtype), S((k, n_ch), F32), S((1, n_ch), F32)],
        scratch_shapes=[pltpu.VMEM((HALO + tb, LANES), F32), pltpu.VMEM((tb + HALO, LANES), F32)],
        compiler_params=_cp(("parallel", "arbitrary")))(x, x, w, dy, dy)


def gdnconv_fwd(name, p, w, tb, job=None):
    n_tok = p.shape[0]
    k = w.shape[0]
    nt = n_tok // tb

    def body(xp_ref, xc_ref, w_ref, o_ref, xs):
        j, i = pl.program_id(0), pl.program_id(1)
        xs[0:HALO, :] = jnp.where(i == 0, 0.0, xp_ref[tb - HALO:tb, :].astype(F32))
        xs[HALO:HALO + tb, :] = xc_ref[...].astype(F32)
        c = jnp.zeros((tb, LANES), F32)
        for t in range(k):
            s = k - 1 - t
            c = c + w_ref[t:t + 1, :] * xs[HALO - s:HALO - s + tb, :]
        y = c * jax.nn.sigmoid(c)
        r = lax.rsqrt(jnp.sum(y * y, axis=-1, keepdims=True) + 1e-6) * jnp.where(j < HEADS, DK ** -0.5, 1.0)
        o_ref[...] = jnp.where(j < 2 * HEADS, y * r, y)

    return _pcall(
        body, job, name=name, grid=(W_QKV // LANES, nt),
        in_specs=[pl.BlockSpec((tb, LANES), lambda j, i: (jnp.maximum(i - 1, 0), j)),
                  pl.BlockSpec((tb, LANES), lambda j, i: (i, j)),
                  pl.BlockSpec((k, LANES), lambda j, i: (0, j))],
        out_specs=[pl.BlockSpec((tb, LANES), lambda j, i: (i, j))], out_shape=[S((n_tok, W_QKV), F32)],
        scratch_shapes=[pltpu.VMEM((HALO + tb, LANES), F32)], semantics=("parallel", "parallel"))(p, p, w)[0]


def gdnconv_bwd(name, p, w, dn, tb, job=None):
    n_tok = p.shape[0]
    k = w.shape[0]
    nt = n_tok // tb
    ext = tb + HALO

    def body(xp_ref, xc_ref, xn_ref, w_ref, dnc_ref, dnn_ref, dx_ref, dw_ref, xs, dns, dcs):
        j, i = pl.program_id(0), pl.program_id(1)
        xs[0:HALO, :] = jnp.where(i == 0, 0.0, xp_ref[tb - HALO:tb, :].astype(F32))
        xs[HALO:HALO + tb, :] = xc_ref[...].astype(F32)
        xs[HALO + tb:HALO + ext, :] = jnp.where(i == nt - 1, 0.0, xn_ref[0:HALO, :].astype(F32))
        dns[0:tb, :] = dnc_ref[...]
        dns[tb:ext, :] = jnp.where(i == nt - 1, 0.0, dnn_ref[0:HALO, :])

        @pl.when(i == 0)
        def _():
            dw_ref[...] = jnp.zeros_like(dw_ref)

        c = jnp.zeros((ext, LANES), F32)
        for t in range(k):
            s = k - 1 - t
            c = c + w_ref[t:t + 1, :] * xs[HALO - s:HALO - s + ext, :]
        d = dns[...]
        sg = jax.nn.sigmoid(c)
        y = c * sg
        r = lax.rsqrt(jnp.sum(y * y, axis=-1, keepdims=True) + 1e-6)
        scale = jnp.where(j < HEADS, DK ** -0.5, 1.0)
        dy_norm = scale * (d * r - y * (r * r * r) * jnp.sum(d * y, axis=-1, keepdims=True))
        dy = jnp.where(j < 2 * HEADS, dy_norm, d)
        dc = dy * (sg * (1.0 + c * (1.0 - sg)))
        dcs[...] = dc
        acc = jnp.zeros((tb, LANES), F32)
        for t in range(k):
            s = k - 1 - t
            acc = acc + w_ref[t:t + 1, :] * dcs[s:s + tb, :]
            dw_ref[t:t + 1, :] += jnp.sum(dcs[0:tb, :] * xs[HALO - s:HALO - s + tb, :], axis=0, keepdims=True)
        dx_ref[...] = acc.astype(dx_ref.dtype)

    cur = lambda j, i: (i, j)
    nxt = lambda j, i: (jnp.minimum(i + 1, nt - 1), j)
    return _pcall(
        body, job, name=name, grid=(W_QKV // LANES, nt),
        in_specs=[pl.BlockSpec((tb, LANES), lambda j, i: (jnp.maximum(i - 1, 0), j)),
                  pl.BlockSpec((tb, LANES), cur), pl.BlockSpec((tb, LANES), nxt),
                  pl.BlockSpec((k, LANES), lambda j, i: (0, j)),
                  pl.BlockSpec((tb, LANES), cur), pl.BlockSpec((tb, LANES), nxt)],
        out_specs=[pl.BlockSpec((tb, LANES), cur), pl.BlockSpec((k, LANES), lambda j, i: (0, j))],
        out_shape=[S((n_tok, W_QKV), BF16), S((k, W_QKV), F32)],
        scratch_shapes=[pltpu.VMEM((HALO + ext, LANES), F32), pltpu.VMEM((ext, LANES), F32),
                        pltpu.VMEM((ext, LANES), F32)],
        semantics=("parallel", "arbitrary"))(p, p, p, w, dn, dn)


GDN_GROUP = 2


def _dotb(a, b, ca, cb):
    return lax.dot_general(a.astype(BF16), b.astype(BF16), (((ca,), (cb,)), ((), ())), preferred_element_type=F32)


def _dot32(a, b, ca, cb):
    return lax.dot_general(a, b, (((ca,), (cb,)), ((), ())), preferred_element_type=F32,
                           precision=lax.Precision.HIGHEST)


def _dot3_many(xs, ys, ca, cb):
    xh = [x.astype(BF16) for x in xs]
    xl = [(x - h.astype(F32)).astype(BF16) for x, h in zip(xs, xh)]
    yh = [y.astype(BF16) for y in ys]
    yl = [(y - h.astype(F32)).astype(BF16) for y, h in zip(ys, yh)]
    dg = lambda p, q: lax.dot_general(p, q, (((ca,), (cb,)), ((), ())), preferred_element_type=F32)
    hh = [dg(p, q) for p, q in zip(xh, yh)]
    hl = [dg(p, q) for p, q in zip(xh, yl)]
    lh = [dg(p, q) for p, q in zip(xl, yh)]
    return [a + (b + c) for a, b, c in zip(hh, hl, lh)]


@jax.custom_vjp
def _mm3_many(xs, ys):
    return _dot3_many(xs, ys, 1, 0)


def _mm3_fwd(xs, ys):
    return _dot3_many(xs, ys, 1, 0), (xs, ys)


def _mm3_bwd(res, cts):
    xs, ys = res
    return _dot3_many(cts, ys, 1, 1), _dot3_many(xs, cts, 0, 0)


_mm3_many.defvjp(_mm3_fwd, _mm3_bwd)


@jax.custom_vjp
def _inv_unit_lower_many(mats):
    n = mats[0].shape[0]
    eye = (lax.broadcasted_iota(jnp.int32, (n, n), 0) == lax.broadcasted_iota(jnp.int32, (n, n), 1)).astype(F32)
    inv = [eye - a for a in mats]
    p = list(mats)
    for _ in range(int(math.log2(n)) - 1):
        p = _dot3_many(p, p, 1, 0)
        upd = _dot3_many(inv, p, 1, 0)
        inv = [i + u for i, u in zip(inv, upd)]
    return inv


def _inv_fwd(mats):
    t = _inv_unit_lower_many(mats)
    return t, t


def _inv_bwd(t, dt):
    x = _dot3_many(t, dt, 0, 0)
    return ([-y for y in _dot3_many(x, t, 1, 1)],)


_inv_unit_lower_many.defvjp(_inv_fwd, _inv_bwd)


def _softplus(x):
    return jnp.maximum(x, 0.0) + jnp.log(1.0 + jnp.exp(-jnp.abs(x)))


def _gdn_intra(qs, ks, vs, pbas, alog, dtb):
    c = pbas[0].shape[0]
    row = lax.broadcasted_iota(jnp.int32, (c, c), 0)
    colm = lax.broadcasted_iota(jnp.int32, (c, c), 1)
    causal, strict = row >= colm, row > colm
    tril = causal.astype(F32)
    lane = lax.broadcasted_iota(jnp.int32, (1, LANES), 1)
    sub = lax.broadcasted_iota(jnp.int32, (LANES, 1), 0)
    last = (lax.broadcasted_iota(jnp.int32, (c, 1), 0) == c - 1).astype(F32)
    beta_all = [jax.nn.sigmoid(pb) for pb in pbas]
    g_all = [-jnp.exp(alog) * _softplus(pb + dtb) for pb in pbas]
    gc_all = [_dot32(tril, ga, 1, 0) for ga in g_all]
    gr_all = [_dot32(ga, tril, 0, 1) for ga in g_all]
    idx = [(g, h) for g in range(len(pbas)) for h in range(HEADS)]
    beta = [jnp.sum(beta_all[g] * (lane == h).astype(F32), axis=1, keepdims=True) for g, h in idx]
    gc = [jnp.sum(gc_all[g] * (lane == HEADS + h).astype(F32), axis=1, keepdims=True) for g, h in idx]
    gr = [jnp.sum(gr_all[g] * (sub == HEADS + h).astype(F32), axis=0, keepdims=True) for g, h in idx]
    decay = [jnp.where(causal, jnp.exp(jnp.where(causal, a - b, 0.0)), 0.0) for a, b in zip(gc, gr)]
    kk = [_dotb(k, k, 1, 1) for k in ks]
    tinv = _inv_unit_lower_many([jnp.where(strict, x * d * b, 0.0) for x, d, b in zip(kk, decay, beta)])
    eg = [jnp.exp(a) for a in gc]
    g_last = [jnp.sum(a * last, axis=0, keepdims=True) for a in gc]
    us = _mm3_many(tinv, [v * b for v, b in zip(vs, beta)])
    ws = _mm3_many(tinv, [k * (b * e) for k, b, e in zip(ks, beta, eg)])
    qds = [q * e for q, e in zip(qs, eg)]
    kds = [k * jnp.exp(gl - a) for k, gl, a in zip(ks, g_last, gc)]
    qks = [_dotb(q, k, 1, 1) * d for q, k, d in zip(qs, ks, decay)]
    decs = [jnp.exp(gl) for gl in g_last]
    return us, ws, qds, kds, qks, decs


def _gdn_seq(us, ws, qds, kds, qks, decs, states):
    corr = [_dotb(w, st, 1, 0) for w, st in zip(ws, states)]
    from_state = [_dotb(qd, st, 1, 0) for qd, st in zip(qds, states)]
    v_new = [u - x for u, x in zip(us, corr)]
    intra = [_dotb(qk, vn, 1, 0) for qk, vn in zip(qks, v_new)]
    upd = [_dotb(kd, vn, 0, 0) for kd, vn in zip(kds, v_new)]
    outs = [a + b for a, b in zip(from_state, intra)]
    news = [st * d + x for st, d, x in zip(states, decs, upd)]
    return outs, news


def _heads(ref, rows=slice(None), base=0):
    return [ref[rows, (base + h) * DK:(base + h + 1) * DK].astype(F32) for h in range(HEADS)]


def _qk_heads(ref, rows=slice(None)):
    return [ref[rows, h * DK:h * DK + CHUNK].astype(F32) for h in range(HEADS)]


def _put_heads(ref, vals, rows=slice(None), base=0):
    for h in range(HEADS):
        ref[rows, (base + h) * DK:(base + h + 1) * DK] = vals[h].astype(ref.dtype)


def _put_qk(ref, vals, rows=slice(None)):
    for h in range(HEADS):
        ref[rows, h * DK:h * DK + CHUNK] = vals[h].astype(ref.dtype)
        ref[rows, h * DK + CHUNK:(h + 1) * DK] = jnp.zeros(vals[h].shape, ref.dtype)


def _group(n_chunks):
    return GDN_GROUP if n_chunks % GDN_GROUP == 0 else 1


def _decs(ref):
    return [ref[h:h + 1, 0:1] for h in range(HEADS)]


def gdn_intra_fwd(name, qkvn, p, alog, dtb, job=None):
    n_tok = qkvn.shape[0]
    n = n_tok // CHUNK
    grp = _group(n)
    hd = HEADS * DK
    rb = grp * CHUNK

    def body(q_ref, k_ref, v_ref, pba_ref, al_ref, dt_ref, u_ref, w_ref, qd_ref, kd_ref, qk_ref, dec_ref):
        rows = [slice(g * CHUNK, (g + 1) * CHUNK) for g in range(grp)]
        cat = lambda ref: [t for r in rows for t in _heads(ref, r)]
        us, ws, qds, kds, qks, decs = _gdn_intra(cat(q_ref), cat(k_ref), cat(v_ref), [pba_ref[r, :].astype(F32) for r in rows],
                                                 al_ref[...], dt_ref[...])
        for g, r in enumerate(rows):
            part = slice(g * HEADS, (g + 1) * HEADS)
            _put_heads(u_ref, us[part], r)
            _put_heads(w_ref, ws[part], r)
            _put_heads(qd_ref, qds[part], r)
            _put_heads(kd_ref, kds[part], r)
            _put_qk(qk_ref, qks[part], r)
            for h in range(HEADS):
                dec_ref[g * HEADS + h:g * HEADS + h + 1, :] = jnp.broadcast_to(decs[g * HEADS + h], (1, LANES))

    blk = lambda c: pl.BlockSpec((rb, hd), lambda i: (i, c))
    par = pl.BlockSpec((1, LANES), lambda i: (0, 0))
    return _pcall(
        body, job, name=name, grid=(n // grp,),
        in_specs=[blk(0), blk(1), blk(2), pl.BlockSpec((rb, LANES), lambda i: (i, COL_BA)), par, par],
        out_specs=[blk(0)] * 5 + [pl.BlockSpec((grp * HEADS, LANES), lambda i: (i, 0))],
        out_shape=[S((n_tok, hd), F32)] + [S((n_tok, hd), BF16)] * 4 + [S((n * HEADS, LANES), F32)],
        semantics=("parallel",))(qkvn, qkvn, qkvn, p, alog, dtb)


def gdn_seq_fwd(name, u, w, qd, kd, qk, dec):
    n_tok = u.shape[0]
    n = n_tok // CHUNK
    hd = HEADS * DK

    def body(u_ref, w_ref, qd_ref, kd_ref, qk_ref, dec_ref, o_ref, s_ref, st):
        @pl.when(pl.program_id(0) == 0)
        def _():
            st[...] = jnp.zeros_like(st)

        s_ref[...] = st[...].astype(s_ref.dtype)
        states = [st[h * DK:(h + 1) * DK, :] for h in range(HEADS)]
        outs, news = _gdn_seq(_heads(u_ref), _heads(w_ref), _heads(qd_ref), _heads(kd_ref), _qk_heads(qk_ref),
                              _decs(dec_ref), states)
        _put_heads(o_ref, outs)
        for h in range(HEADS):
            st[h * DK:(h + 1) * DK, :] = news[h]

    blk = pl.BlockSpec((CHUNK, hd), lambda i: (i, 0))
    return pl.pallas_call(
        body, name=name, grid=(n,),
        in_specs=[blk] * 5 + [pl.BlockSpec((HEADS, LANES), lambda i: (i, 0))],
        out_specs=[blk, pl.BlockSpec((None, hd, DK), lambda i: (i, 0, 0))],
        out_shape=[S((n_tok, hd), F32), S((n, hd, DK), BF16)],
        scratch_shapes=[pltpu.VMEM((hd, DK), F32)],
        compiler_params=_cp(("arbitrary",)))(u, w, qd, kd, qk, dec)


def gdn_seq_bwd(name, u, w, qd, kd, qk, dec, states, do):
    n_tok = u.shape[0]
    n = n_tok // CHUNK
    hd = HEADS * DK

    def body(u_ref, w_ref, qd_ref, kd_ref, qk_ref, dec_ref, s_ref, do_ref,
             du_ref, dw_ref, dqd_ref, dkd_ref, dqk_ref, ddec_ref, dst):
        @pl.when(pl.program_id(0) == 0)
        def _():
            dst[...] = jnp.zeros_like(dst)

        states = [s_ref[h * DK:(h + 1) * DK, :].astype(F32) for h in range(HEADS)]
        _, vjp = jax.vjp(_gdn_seq, _heads(u_ref), _heads(w_ref), _heads(qd_ref), _heads(kd_ref), _qk_heads(qk_ref),
                         _decs(dec_ref), states)
        d_news = [dst[h * DK:(h + 1) * DK, :] for h in range(HEADS)]
        du, dw, dqd, dkd, dqk, ddec, dstates = vjp((_heads(do_ref), d_news))
        _put_heads(du_ref, du)
        _put_heads(dw_ref, dw)
        _put_heads(dqd_ref, dqd)
        _put_heads(dkd_ref, dkd)
        _put_qk(dqk_ref, dqk)
        for h in range(HEADS):
            ddec_ref[h:h + 1, :] = jnp.broadcast_to(ddec[h], (1, LANES))
            dst[h * DK:(h + 1) * DK, :] = dstates[h]

    blk = pl.BlockSpec((CHUNK, hd), lambda i: (n - 1 - i, 0))
    dspec = pl.BlockSpec((HEADS, LANES), lambda i: (n - 1 - i, 0))
    return pl.pallas_call(
        body, name=name, grid=(n,),
        in_specs=[blk] * 5 + [dspec, pl.BlockSpec((None, hd, DK), lambda i: (n - 1 - i, 0, 0)), blk],
        out_specs=[blk] * 5 + [dspec],
        out_shape=[S((n_tok, hd), F32)] * 5 + [S((n * HEADS, LANES), F32)],
        scratch_shapes=[pltpu.VMEM((hd, DK), F32)],
        compiler_params=_cp(("arbitrary",)))(u, w, qd, kd, qk, dec, states, do)


def gdn_intra_bwd(name, qkvn, p, alog, dtb, du, dw, dqd, dkd, dqk, ddec, job=None):
    n_tok = qkvn.shape[0]
    n = n_tok // CHUNK
    grp = _group(n)
    hd = HEADS * DK
    rb = grp * CHUNK

    def body(q_ref, k_ref, v_ref, pba_ref, al_ref, dt_ref, du_ref, dw_ref, dqd_ref, dkd_ref, dqk_ref, ddec_ref,
             dqkv_ref, dpba_ref, dal_ref, ddt_ref):
        @pl.when(pl.program_id(0) == 0)
        def _():
            dal_ref[...] = jnp.zeros_like(dal_ref)
            ddt_ref[...] = jnp.zeros_like(ddt_ref)

        rows = [slice(g * CHUNK, (g + 1) * CHUNK) for g in range(grp)]
        cat = lambda ref: [t for r in rows for t in _heads(ref, r)]
        _, vjp = jax.vjp(_gdn_intra, cat(q_ref), cat(k_ref), cat(v_ref), [pba_ref[r, :].astype(F32) for r in rows],
                         al_ref[...], dt_ref[...])
        cts = (cat(du_ref), cat(dw_ref), cat(dqd_ref), cat(dkd_ref), [t for r in rows for t in _qk_heads(dqk_ref, r)],
               [ddec_ref[i:i + 1, 0:1] for i in range(grp * HEADS)])
        dq, dk, dv, dpba, dal, ddt = vjp(cts)
        for g, r in enumerate(rows):
            part = slice(g * HEADS, (g + 1) * HEADS)
            _put_heads(dqkv_ref, dq[part], r, 0)
            _put_heads(dqkv_ref, dk[part], r, HEADS)
            _put_heads(dqkv_ref, dv[part], r, 2 * HEADS)
            dpba_ref[r, :] = dpba[g].astype(dpba_ref.dtype)
        dal_ref[...] += dal
        ddt_ref[...] += ddt

    blk = lambda c: pl.BlockSpec((rb, hd), lambda i: (i, c))
    par = pl.BlockSpec((1, LANES), lambda i: (0, 0))
    return _pcall(
        body, job, name=name, grid=(n // grp,),
        in_specs=[blk(0), blk(1), blk(2), pl.BlockSpec((rb, LANES), lambda i: (i, COL_BA)), par, par]
        + [blk(0)] * 5 + [pl.BlockSpec((grp * HEADS, LANES), lambda i: (i, 0))],
        out_specs=[pl.BlockSpec((rb, 3 * hd), lambda i: (i, 0)), pl.BlockSpec((rb, LANES), lambda i: (i, 0)), par, par],
        out_shape=[S((n_tok, 3 * hd), F32), S((n_tok, LANES), BF16), S((1, LANES), F32), S((1, LANES), F32)],
        semantics=("arbitrary",))(qkvn, qkvn, qkvn, p, alog, dtb, du, dw, dqd, dkd, dqk, ddec)


def loss_head(name, y, target, tm):
    n_tok, d = y.shape

    def body(y_ref, t_ref, dy_ref, l_ref):
        @pl.when(pl.program_id(0) == 0)
        def _():
            l_ref[...] = jnp.zeros_like(l_ref)

        e = y_ref[...] - t_ref[...]
        dy_ref[...] = e * (1.0 / d)
        l_ref[...] += jnp.sum(e * e, keepdims=True) * (0.5 / d)

    spec = pl.BlockSpec((tm, d), lambda i: (i, 0))
    return pl.pallas_call(
        body, name=name, grid=(n_tok // tm,), in_specs=[spec, spec],
        out_specs=[spec, pl.BlockSpec((1, 1), lambda i: (0, 0))], out_shape=[S((n_tok, d), F32), S((1, 1), F32)],
        compiler_params=_cp(("arbitrary",)))(y, target)


def _tm(n_tok):
    return min(512, n_tok)


def _gconv_tb(n_tok):
    return 1024 if n_tok % 1024 == 0 else _tm(n_tok)


def ffn_fwd(tag, x, w, jobs=None):
    n_tok = x.shape[0]
    tm = _tm(n_tok)
    g1 = (1, n_tok // tm)
    tD = _tok(D_MODEL)(tm)
    (h,) = ew_fwd(tag + "_rms", fn_rms, g1, [(x, tD), (w["norm_pre"], _par(D_MODEL))], [(S((n_tok, D_MODEL), BF16), tD)])
    u = mm(tag + "_in", h, w["w_in"], out_dtype=BF16, job=_take(jobs, "ffn_in"))
    tF = lambda c: _tok(D_FF, c)(tm)
    (a,) = ew_fwd(tag + "_swiglu", fn_swiglu, g1, [(u, tF(0)), (u, tF(1))], [(S((n_tok, D_FF), BF16), tF(0))])
    f = mm(tag + "_out", a, w["w_out"], job=_take(jobs, "ffn_out"))
    fn_res = lambda col, x_, f_, w_: (x_ + 0.5 * _rms(f_, w_),)
    (xo,) = ew_fwd(tag + "_res", fn_res, g1, [(x, tD), (f, tD), (w["norm_post"], _par(D_MODEL))],
                   [(S((n_tok, D_MODEL), F32), tD)])
    return xo, dict(x=x, h=h, u=u, a=a, f=f)


def ffn_bwd(tag, dxo, sv, w, jobs=None):
    n_tok = dxo.shape[0]
    tm = _tm(n_tok)
    g1 = (1, n_tok // tm)
    tD = _tok(D_MODEL)(tm)
    pD = _par(D_MODEL)
    fn_post = lambda col, f_, w_: (0.5 * _rms(f_, w_),)
    df, d_post = ew_bwd(tag + "_res_b", fn_post, g1, [(sv["f"], tD), (w["norm_post"], pD)], [(dxo, tD)],
                        [(0, S((n_tok, D_MODEL), BF16), tD)], [(1, S((1, D_MODEL), F32), pD, False)])
    da = mm(tag + "_out_bx", df, w["w_out"], tb=True, out_dtype=BF16, job=_take(jobs, "ffn_out_bx"))
    d_wout = mm(tag + "_out_bw", sv["a"], df, ta=True, out_dtype=GRAD_DT, job=_take(jobs, "ffn_out_bw"))
    tF = lambda c: _tok(D_FF, c)(tm)
    du = swiglu_bwd(tag + "_swiglu_b", sv["u"], da, tm)
    dh = mm(tag + "_in_bx", du, w["w_in"], tb=True, job=_take(jobs, "ffn_in_bx"))
    d_win = mm(tag + "_in_bw", sv["h"], du, ta=True, out_dtype=GRAD_DT, job=_take(jobs, "ffn_in_bw"), out_blocked=True)
    dx, d_pre = ew_bwd(tag + "_rms_b", fn_rms, g1, [(sv["x"], tD), (w["norm_pre"], pD)], [(dh, tD)],
                       [(0, S((n_tok, D_MODEL), F32), tD)], [(1, S((1, D_MODEL), F32), pD, False)], add=(dxo, tD))
    return dx, dict(norm_pre=d_pre, norm_post=d_post, w_in=d_win, w_out=d_wout)


def mix_fwd(tag, x, w, jobs=None):
    n_tok = x.shape[0]
    tm = _tm(n_tok)
    nt = n_tok // tm
    g1 = (1, nt)
    tD = _tok(D_MODEL)(tm)
    pD = _par(D_MODEL)
    (h,) = ew_fwd(tag + "_rms", fn_rms, g1, [(x, tD), (w["norm_pre"], pD)], [(S((n_tok, D_MODEL), BF16), tD)])
    p = mm(tag + "_in", h, w["w_all"], out_dtype=BF16, job=_take(jobs, "mix_in"))
    qkvn = gdnconv_fwd(tag + "_gconv", p, w["conv_w"], _gconv_tb(n_tok), job=_take(jobs, "gconv"))
    tC = _tokcol()(tm)
    intra = gdn_intra_fwd(tag + "_gintra", qkvn, p, w["alog"], w["dtb"], job=_take(jobs, "gintra"))
    o, states = gdn_seq_fwd(tag + "_gseq", *intra)
    (on,) = ew_fwd(tag + "_gout", fn_gdnout, (HEADS, nt),
                   [(o, tC), (p, _tokcol(COL_Z)(tm)), (w["gdn_norm_w"], _par(LANES))],
                   [(S((n_tok, D_MODEL), BF16), tC)])
    ya = mm(tag + "_go", on, w["gdn_w_o"], job=_take(jobs, "mix_small"))
    (hglu,) = ew_fwd(tag + "_glu", fn_glu, (D_MODEL // LANES, nt),
                     [(p, _tokcol(COL_GLU)(tm)), (p, _tokcol(COL_GLU + D_MODEL // LANES)(tm)),
                      (w["pw1_b"], _parcol(0)), (w["pw1_b"], _parcol(D_MODEL // LANES))],
                     [(S((n_tok, D_MODEL), F32), tC)])
    hc = conv_fwd(tag + "_cconv", hglu, 0, D_MODEL, w["dw_w"], w["dw_b"], tm)
    (hs,) = ew_fwd(tag + "_ln", fn_lnsilu, g1, [(hc, tD), (w["ln_g"], pD), (w["ln_b"], pD)],
                   [(S((n_tok, D_MODEL), BF16), tD)])
    yb = mm(tag + "_co", hs, w["cnv_w_o"], job=_take(jobs, "mix_small"))
    tG = lambda cb: pl.BlockSpec((tm, D_MODEL), lambda j, i: (i, cb))
    gcol = (W_QKV + W_Z + W_GLU) // D_MODEL
    (ym,) = ew_fwd(tag + "_merge", fn_merge, g1, [(ya, tD), (yb, tD), (p, tG(gcol)), (p, tG(gcol + 1)), (w["b_o"], pD)],
                   [(S((n_tok, D_MODEL), BF16), tD)])
    y = mm(tag + "_wo", ym, w["w_out"], job=_take(jobs, "mix_small"))
    fn_res = lambda col, x_, f_, w_: (x_ + _rms(f_, w_),)
    (xo,) = ew_fwd(tag + "_res", fn_res, g1, [(x, tD), (y, tD), (w["norm_post"], pD)], [(S((n_tok, D_MODEL), F32), tD)])
    sv = dict(x=x, h=h, p=p, qkvn=qkvn, intra=intra, states=states, o=o, on=on, ya=ya, hglu=hglu, hc=hc, hs=hs, yb=yb, ym=ym, y=y)
    return xo, sv


def mix_bwd(tag, dxo, sv, w, jobs=None):
    n_tok = dxo.shape[0]
    tm = _tm(n_tok)
    nt = n_tok // tm
    g1 = (1, nt)
    tD = _tok(D_MODEL)(tm)
    pD = _par(D_MODEL)
    tC = _tokcol()(tm)
    p = sv["p"]
    sD = lambda dt: S((n_tok, D_MODEL), dt)
    fn_post = lambda col, f_, w_: (_rms(f_, w_),)
    dy, d_post = ew_bwd(tag + "_res_b", fn_post, g1, [(sv["y"], tD), (w["norm_post"], pD)], [(dxo, tD)],
                        [(0, sD(BF16), tD)], [(1, S((1, D_MODEL), F32), pD, False)])
    dym = mm(tag + "_wo_bx", dy, w["w_out"], tb=True, job=_take(jobs, "mix_small"))
    d_wout = mm(tag + "_wo_bw", sv["ym"], dy, ta=True, out_dtype=GRAD_DT, job=_take(jobs, "mix_small"))
    tG = lambda cb: pl.BlockSpec((tm, D_MODEL), lambda j, i: (i, cb))
    gcol = (W_QKV + W_Z + W_GLU) // D_MODEL
    dya, dyb, dga, dgb, d_bo = ew_bwd(
        tag + "_merge_b", fn_merge, g1, [(sv["ya"], tD), (sv["yb"], tD), (p, tG(gcol)), (p, tG(gcol + 1)), (w["b_o"], pD)],
        [(dym, tD)], [(0, sD(BF16), tD), (1, sD(BF16), tD), (2, sD(BF16), tD), (3, sD(BF16), tD)],
        [(4, S((1, D_MODEL), F32), pD, False)])
    dhs = mm(tag + "_co_bx", dyb, w["cnv_w_o"], tb=True, job=_take(jobs, "mix_small"))
    d_cwo = mm(tag + "_co_bw", sv["hs"], dyb, ta=True, out_dtype=GRAD_DT, job=_take(jobs, "mix_small"))
    dhc, d_lng, d_lnb = ew_bwd(tag + "_ln_b", fn_lnsilu, g1, [(sv["hc"], tD), (w["ln_g"], pD), (w["ln_b"], pD)], [(dhs, tD)],
                               [(0, sD(F32), tD)], [(1, S((1, D_MODEL), F32), pD, False), (2, S((1, D_MODEL), F32), pD, False)])
    dhglu, d_dww, d_dwb = conv_bwd(tag + "_cconv_b", sv["hglu"], 0, D_MODEL, w["dw_w"], dhc, F32, tm)
    nc = D_MODEL // LANES
    dpa, dpg, d_ba, d_bg = ew_bwd(
        tag + "_glu_b", fn_glu, (nc, nt),
        [(p, _tokcol(COL_GLU)(tm)), (p, _tokcol(COL_GLU + nc)(tm)), (w["pw1_b"], _parcol(0)), (w["pw1_b"], _parcol(nc))],
        [(dhglu, tC)], [(0, sD(BF16), tC), (1, sD(BF16), tC)],
        [(2, S((1, D_MODEL), F32), _parcol(0), False), (3, S((1, D_MODEL), F32), _parcol(0), False)])
    don = mm(tag + "_go_bx", dya, w["gdn_w_o"], tb=True, job=_take(jobs, "mix_small"))
    d_gwo = mm(tag + "_go_bw", sv["on"], dya, ta=True, out_dtype=GRAD_DT, job=_take(jobs, "mix_small"))
    do, dz, d_gnw = ew_bwd(tag + "_gout_b", fn_gdnout, (HEADS, nt),
                           [(sv["o"], tC), (p, _tokcol(COL_Z)(tm)), (w["gdn_norm_w"], _par(LANES))], [(don, tC)],
                           [(0, sD(F32), tC), (1, sD(BF16), tC)], [(2, S((1, LANES), F32), _par(LANES), True)])
    d_intra = gdn_seq_bwd(tag + "_gseq_b", *sv["intra"], sv["states"], do)
    dqkvn, dpba, d_alog, d_dtb = gdn_intra_bwd(tag + "_gintra_b", sv["qkvn"], p, w["alog"], w["dtb"], *d_intra,
                                               job=_take(jobs, "gintra_b"))
    dqkv, d_convw = gdnconv_bwd(tag + "_gconv_b", p, w["conv_w"], dqkvn, _gconv_tb(n_tok), job=_take(jobs, "gconv_b"))
    nd = D_MODEL // LANES
    pieces = [(dqkv, 0, 0), (dz, W_QKV // D_MODEL, COL_Z), (dpa, COL_GLU // nd, COL_GLU), (dpg, COL_GLU // nd + 1, COL_GLU + nd),
              (dga, COL_GATE // nd, COL_GATE), (dgb, COL_GATE // nd + 1, COL_GATE + nd), (dpba, COL_BA, COL_BA)]
    dh = mm_nt_sum(tag + "_in_bx", [(a, blk) for a, blk, _ in pieces], w["w_all"])
    d_wall = [mm(tag + "_in_bw", sv["h"], a, ta=True, out_dtype=GRAD_DT, job=_take(jobs, "mix_small"))
              for a, _, _ in pieces]
    dx, d_pre = ew_bwd(tag + "_rms_b", fn_rms, g1, [(sv["x"], tD), (w["norm_pre"], pD)], [(dh, tD)],
                       [(0, sD(F32), tD)], [(1, S((1, D_MODEL), F32), pD, False)], add=(dxo, tD))
    grads = dict(norm_pre=d_pre, norm_post=d_post, w_all=d_wall, conv_w=d_convw, alog=d_alog, dtb=d_dtb,
                 gdn_norm_w=d_gnw, gdn_w_o=d_gwo, pw1_b=jnp.concatenate([d_ba, d_bg], axis=1), dw_w=d_dww,
                 dw_b=d_dwb, ln_g=d_lng, ln_b=d_lnb, cnv_w_o=d_cwo, b_o=d_bo, w_out=d_wout)
    return dx, grads


def local_step(x, target, layers):
    saved = []
    for lw in layers:
        x, sv = layer_fwd(x, lw)
        saved.append(sv)
    dx, loss = loss_head("loss", x, target, _tm(x.shape[0]))
    grads = [None] * len(layers)
    for i in reversed(range(len(layers))):
        dx, grads[i] = layer_bwd(dx, saved[i], layers[i])
    return loss, dx, grads


def layer_fwd(x, lw, jobs=None):
    x, s1 = ffn_fwd("ffn", x, lw["ffn1"], jobs)
    x, s2 = mix_fwd("mix", x, lw["mix"], jobs)
    x, s3 = ffn_fwd("ffn", x, lw["ffn2"], jobs)
    return x, (s1, s2, s3)


def layer_bwd(dx, saved, lw, jobs=None):
    s1, s2, s3 = saved
    dx, g3 = ffn_bwd("ffn", dx, s3, lw["ffn2"], jobs)
    dx, g2 = mix_bwd("mix", dx, s2, lw["mix"], jobs)
    dx, g1 = ffn_bwd("ffn", dx, s1, lw["ffn1"], jobs)
    return dx, dict(ffn1=g1, mix=g2, ffn2=g3)


_O_BA = W_QKV + W_Z
_O_GLU = _O_BA + 2 * HEADS


_MIX_BLK = P_IN // N_BLK
_MIX_B1 = _O_BA - _MIX_BLK
assert _O_BA + HEADS == 2 * _MIX_BLK
BLOCKED = ("ffn1_w_in", "ffn2_w_in", "mix_w_in")


def _row(v):
    return v.reshape(1, -1).astype(F32)


def prep_ffn(wl, k):
    w_in = wl[k + "_w_in"].astype(BF16)
    if w_in.ndim == 2:
        w_in = jnp.transpose(w_in.reshape(w_in.shape[0], N_BLK, -1), (1, 0, 2))
    return dict(norm_pre=_row(wl[k + "_norm_pre"]), norm_post=_row(wl[k + "_norm_post"]), w_in=w_in,
                w_out=wl[k + "_w_out"].astype(BF16))


def prep_layer(wl):
    return dict(ffn1=prep_ffn(wl, "ffn1"), mix=prep_mix(wl), ffn2=prep_ffn(wl, "ffn2"))


def prep_mix(wl):
    row = _row
    bf = lambda v: v.astype(BF16)
    lanes8 = lambda v: jnp.zeros((1, LANES), F32).at[0, HEADS:2 * HEADS].set(v.astype(F32))
    mw = bf(wl["mix_w_in"])
    pad = jnp.zeros((D_MODEL, LANES - 2 * HEADS), BF16)
    if mw.ndim == 3:
        w_all = jnp.concatenate([mw[0], mw[1][:, :_MIX_B1], mw[2][:, HEADS:], mw[3], mw[1][:, _MIX_B1:],
                                 mw[2][:, :HEADS], pad], axis=1)
    else:
        w_all = jnp.concatenate([mw[:, :_O_BA], mw[:, _O_GLU:], mw[:, _O_BA:_O_GLU], pad], axis=1)
    return dict(norm_pre=row(wl["mix_norm_pre"]), norm_post=row(wl["mix_norm_post"]), w_all=w_all,
               conv_w=wl["gdn_conv_w"].astype(F32), alog=lanes8(wl["gdn_a_log"]), dtb=lanes8(wl["gdn_dt_bias"]),
               gdn_norm_w=row(wl["gdn_norm_w"]), gdn_w_o=bf(wl["gdn_w_o"]), pw1_b=row(wl["cnv_pw1_b"]),
               dw_w=wl["cnv_dw_w"].astype(F32), dw_b=row(wl["cnv_dw_b"]), ln_g=row(wl["cnv_ln_g"]),
               ln_b=row(wl["cnv_ln_b"]), cnv_w_o=bf(wl["cnv_w_o"]), b_o=row(wl["cnv_b_o"]), w_out=bf(wl["mix_w_out"]))


def unprep_grads(g):
    return {**unprep_ffn(g["ffn1"], "ffn1"), **unprep_mix(g["mix"]), **unprep_ffn(g["ffn2"], "ffn2")}


def unprep_ffn(g, k):
    blk = g["w_in"]
    return {k + "_norm_pre": g["norm_pre"][0], k + "_norm_post": g["norm_post"][0], k + "_w_in#blocks": blk,
            k + "_w_in": jnp.transpose(blk, (1, 0, 2)).reshape(blk.shape[1], N_BLK * blk.shape[2]),
            k + "_w_out": g["w_out"]}


def unprep_mix(m):
    dqkv, dz, dpa, dpg, dga, dgb, dba = m["w_all"]
    out = {}
    out["mix_w_in#blocks"] = jnp.stack([
        dqkv[:, :_MIX_BLK], jnp.concatenate([dqkv[:, _MIX_BLK:], dz, dba[:, :HEADS]], axis=1),
        jnp.concatenate([dba[:, HEADS:2 * HEADS], dpa, dpg[:, :_MIX_B1 - D_MODEL]], axis=1),
        jnp.concatenate([dpg[:, _MIX_B1 - D_MODEL:], dga, dgb], axis=1)])
    out.update(
        mix_norm_pre=m["norm_pre"][0], mix_norm_post=m["norm_post"][0],
        mix_w_in=jnp.concatenate([dqkv, dz, dba[:, :2 * HEADS], dpa, dpg, dga, dgb], axis=1),
        gdn_conv_w=m["conv_w"], gdn_a_log=m["alog"][0, HEADS:2 * HEADS], gdn_dt_bias=m["dtb"][0, HEADS:2 * HEADS],
        gdn_norm_w=m["gdn_norm_w"][0], gdn_w_o=m["gdn_w_o"], cnv_pw1_b=m["pw1_b"][0], cnv_dw_w=m["dw_w"],
        cnv_dw_b=m["dw_b"][0], cnv_ln_g=m["ln_g"][0], cnv_ln_b=m["ln_b"][0], cnv_w_o=m["cnv_w_o"], cnv_b_o=m["b_o"][0],
        mix_w_out=m["w_out"])
    return out


MESH = pl.DeviceIdType.MESH
ANY = pl.BlockSpec(memory_space=pl.ANY)
N_DEV = 8


def _pos():
    return lax.axis_index("x"), lax.axis_index("y"), lax.axis_index("c")


def _other_chips(x, y):
    return [(1 - x, y), (x, 1 - y), (1 - x, 1 - y)]


class Job:
    def __init__(self, ins, outs, n_sems, copies, aliases=None):
        self.ins, self.outs, self.n_sems, self.copies = ins, outs, n_sems, copies
        self.aliases = aliases or {}
        self.results = None
        self.host = None

    def scratch(self):
        return [pltpu.SemaphoreType.DMA((self.n_sems,)), pltpu.SemaphoreType.DMA((self.n_sems,))]

    def start(self, in_refs, out_refs, sems):
        for cp in self.copies(in_refs, out_refs, sems, False):
            cp.start()

    def finish(self, in_refs, out_refs, sems):
        for cp in self.copies(in_refs, out_refs, sems, True):
            cp.wait_recv()
        for cp in self.copies(in_refs, out_refs, sems, False):
            cp.wait_send()


def run_job(name, job):
    n_i, n_o = len(job.ins), len(job.outs)

    def body(*refs):
        in_refs, out_refs, sems = refs[:n_i], refs[n_i:n_i + n_o], refs[n_i + n_o:]
        job.start(in_refs, out_refs, sems)
        job.finish(in_refs, out_refs, sems)

    job.results = pl.pallas_call(
        body, name=name, in_specs=[ANY] * n_i, out_specs=[ANY] * n_o, out_shape=job.outs,
        input_output_aliases=job.aliases, scratch_shapes=job.scratch())(*job.ins)
    return job.results


def _halved(rows):
    return rows % 32 == 0


def job_gather_ici(shards):
    n = len(shards)

    def copies(in_refs, out_refs, sems, recv):
        x, y, c = _pos()
        b = 2 * x + y
        chips = _other_chips(x, y)
        cps = []
        for a in range(n):
            hr = shards[a].shape[0] // 2
            for j in range(3):
                blk = 2 * chips[j][0] + chips[j][1] if recv else b
                if _halved(shards[a].shape[0]):
                    src, dst = in_refs[a].at[pl.ds(c * hr, hr)], out_refs[a].at[blk, pl.ds(c * hr, hr)]
                else:
                    src, dst = in_refs[a], out_refs[a].at[blk]
                cps.append(pltpu.make_async_remote_copy(
                    src_ref=src, dst_ref=dst, send_sem=sems[0].at[3 * a + j], recv_sem=sems[1].at[3 * a + j],
                    device_id=(chips[j][0], chips[j][1], c), device_id_type=MESH))
        return cps

    return Job(list(shards), [S((N_BLK,) + w.shape, w.dtype) for w in shards], 3 * n, copies)


def job_gather_sibling(lands):
    idx = [a for a, w in enumerate(lands) if _halved(w.shape[1])]

    def copies(in_refs, out_refs, sems, recv):
        x, y, c = _pos()
        chips = _other_chips(x, y)
        half = 1 - c if recv else c
        cps = []
        for pos, a in enumerate(idx):
            hr = lands[a].shape[1] // 2
            for j in range(3):
                rows = out_refs[a].at[2 * chips[j][0] + chips[j][1], pl.ds(half * hr, hr)]
                cps.append(pltpu.make_async_remote_copy(
                    src_ref=rows, dst_ref=rows, send_sem=sems[0].at[3 * pos + j], recv_sem=sems[1].at[3 * pos + j],
                    device_id=(x, y, 1 - c), device_id_type=MESH))
        return cps

    return Job(list(lands), [S(w.shape, w.dtype) for w in lands], 3 * len(idx), copies,
               aliases={a: a for a in range(len(lands))})


def job_rs_chips(ps):
    n = len(ps)

    def copies(in_refs, out_refs, sems, recv):
        x, y, c = _pos()
        b = 2 * x + y
        chips = _other_chips(x, y)
        cps = []
        for k in range(n):
            for j in range(3):
                other = 2 * chips[j][0] + chips[j][1]
                src_blk, dst_slot = (b, other) if recv else (other, b)
                cps.append(pltpu.make_async_remote_copy(
                    src_ref=in_refs[k].at[src_blk], dst_ref=out_refs[k].at[dst_slot], send_sem=sems[0].at[3 * k + j],
                    recv_sem=sems[1].at[3 * k + j], device_id=(chips[j][0], chips[j][1], c), device_id_type=MESH))
        return cps

    return Job(list(ps), [S(p.shape, p.dtype) for p in ps], 3 * n, copies)


def rs_sibling(gs):
    n = len(gs)

    def body(*refs):
        g_refs, r_refs = refs[:n], refs[n:2 * n]
        send_sems, recv_sems = refs[2 * n:]
        x, y, c = _pos()

        def cp(k):
            hr = gs[k].shape[1] // 2
            return pltpu.make_async_remote_copy(
                src_ref=g_refs[k].at[:, pl.ds((1 - c) * hr, hr)], dst_ref=r_refs[k], send_sem=send_sems.at[k],
                recv_sem=recv_sems.at[k], device_id=(x, y, 1 - c), device_id_type=MESH)

        cps = [cp(k) for k in range(n)]
        for d in cps:
            d.start()
        for d in cps:
            d.wait_recv()
        for d in cps:
            d.wait_send()

    return pl.pallas_call(
        body, name="rs_sibling", in_specs=[ANY] * n, out_specs=[ANY] * n,
        out_shape=[S((N_BLK, g.shape[1] // 2, g.shape[2]), g.dtype) for g in gs],
        scratch_shapes=[pltpu.SemaphoreType.DMA((n,)), pltpu.SemaphoreType.DMA((n,))])(*gs)


def ag_sibling(fs):
    n = len(fs)

    def body(*refs):
        o_refs = refs[n:2 * n]
        send_sems, recv_sems = refs[2 * n:]
        x, y, c = _pos()

        def cp(k, half):
            hr = fs[k].shape[0] // 2
            rows = o_refs[k].at[pl.ds(half * hr, hr)]
            return pltpu.make_async_remote_copy(
                src_ref=rows, dst_ref=rows, send_sem=send_sems.at[k], recv_sem=recv_sems.at[k],
                device_id=(x, y, 1 - c), device_id_type=MESH)

        cps = [cp(k, c) for k in range(n)]
        for d in cps:
            d.start()
        for k in range(n):
            cp(k, 1 - c).wait_recv()
        for d in cps:
            d.wait_send()

    return pl.pallas_call(
        body, name="ag_sibling", in_specs=[ANY] * n, out_specs=[ANY] * n,
        out_shape=[S(f.shape, f.dtype) for f in fs], input_output_aliases={k: k for k in range(n)},
        scratch_shapes=[pltpu.SemaphoreType.DMA((n,)), pltpu.SemaphoreType.DMA((n,))])(*fs)


def allreduce_small(v):
    rows = v.shape[0]

    def body(v_ref, o_ref, buf, send_sems, recv_sems):
        x, y, c = _pos()
        me = 4 * x + 2 * y + c
        buf[me] = v_ref[...]

        def cp(d, slot):
            dx, dy, dc = (d >> 2) & 1, (d >> 1) & 1, d & 1
            peer = (1 - x if dx else x, 1 - y if dy else y, 1 - c if dc else c)
            return pltpu.make_async_remote_copy(
                src_ref=v_ref, dst_ref=buf.at[slot], send_sem=send_sems.at[d - 1], recv_sem=recv_sems.at[d - 1],
                device_id=peer, device_id_type=MESH)

        cps = [cp(d, me) for d in range(1, N_DEV)]
        for d in cps:
            d.start()
        for d in range(1, N_DEV):
            dx, dy, dc = (d >> 2) & 1, (d >> 1) & 1, d & 1
            src = 4 * (1 - x if dx else x) + 2 * (1 - y if dy else y) + (1 - c if dc else c)
            cp(d, src).wait_recv()
        for d in cps:
            d.wait_send()
        acc = buf[0]
        for s in range(1, N_DEV):
            acc = acc + buf[s]
        o_ref[...] = acc

    vm = pl.BlockSpec(memory_space=pltpu.VMEM)
    return pl.pallas_call(
        body, name="allreduce_small", in_specs=[vm], out_specs=vm, out_shape=S(v.shape, v.dtype),
        scratch_shapes=[pltpu.VMEM((N_DEV, rows, LANES), F32), pltpu.SemaphoreType.DMA((N_DEV - 1,)),
                        pltpu.SemaphoreType.DMA((N_DEV - 1,))])(v)


def _rows_tile(rows, cols, cap_bytes=1 << 20, mult=8):
    best = None
    for t in range(mult, rows + 1, mult):
        if rows % t == 0 and t * cols * 4 <= cap_bytes:
            best = t
    return best if best is not None else rows


def add_half(name, g, r, c_arr):
    _, hr, cols = r.shape
    tr = _rows_tile(hr, cols, mult=16)
    nb = hr // tr

    def body(c_ref, g_ref, r_ref, o_ref):
        o_ref[...] = (g_ref[...].astype(F32) + r_ref[...].astype(F32)).astype(o_ref.dtype)

    gs = pltpu.PrefetchScalarGridSpec(
        num_scalar_prefetch=1, grid=(N_BLK, nb),
        in_specs=[pl.BlockSpec((None, tr, cols), lambda b, i, cr: (b, cr[0] * nb + i, 0)),
                  pl.BlockSpec((None, tr, cols), lambda b, i, cr: (b, i, 0))],
        out_specs=pl.BlockSpec((None, tr, cols), lambda b, i, cr: (b, i, 0)))
    return pl.pallas_call(body, name=name, grid_spec=gs, out_shape=S(r.shape, BF16),
                          compiler_params=_cp(("parallel", "parallel")))(c_arr, g, r)


def sum_chips(name, r, own, cb_arr):
    _, hr, cols = r.shape
    tr = _rows_tile(hr, cols, mult=16)
    nb = hr // tr

    def body(cb_ref, *refs):
        o_ref = refs[N_BLK + 1]
        b = cb_ref[1]
        acc = None
        for s in range(N_BLK):
            term = jnp.where(b == s, refs[N_BLK][...], refs[s][...]).astype(F32)
            acc = term if acc is None else acc + term
        o_ref[...] = acc

    slot = lambda s: pl.BlockSpec((None, tr, cols), lambda i, cb: (jnp.where(cb[1] == s, (s + 1) % N_BLK, s), i, 0))
    gs = pltpu.PrefetchScalarGridSpec(
        num_scalar_prefetch=1, grid=(nb,),
        in_specs=[slot(s) for s in range(N_BLK)] + [pl.BlockSpec((None, tr, cols), lambda i, cb: (cb[1], i, 0))],
        out_specs=pl.BlockSpec((tr, cols), lambda i, cb: (cb[0] * nb + i, 0)))
    return pl.pallas_call(body, name=name, grid_spec=gs, out_shape=S((2 * hr, cols), F32),
                          compiler_params=_cp(("parallel",)))(cb_arr, *([r] * N_BLK), own)


def adamw(name, w, m, v, gs):
    rows, cols = w.shape
    two = len(gs) == 2
    span = rows // 2 if two else rows
    tr = _rows_tile(span, cols, 1 << 19)
    nb = span // tr

    def body(w_ref, m_ref, v_ref, *rest):
        g_refs, (go_ref, d_ref, mo_ref, vo_ref) = rest[:len(gs)], rest[len(gs):]
        if two:
            g = jnp.where(pl.program_id(0) < nb, g_refs[0][...], g_refs[1][...])
        else:
            g = g_refs[0][...]
        mn = ADAM_B1 * m_ref[...] + (1.0 - ADAM_B1) * g
        vn = ADAM_B2 * v_ref[...] + (1.0 - ADAM_B2) * jnp.square(g)
        m_hat = mn / (1.0 - ADAM_B1 ** ADAM_STEP)
        v_hat = vn / (1.0 - ADAM_B2 ** ADAM_STEP)
        go_ref[...] = g
        d_ref[...] = -ADAM_LR * (m_hat / (jnp.sqrt(v_hat) + ADAM_EPS) + ADAM_WD * w_ref[...])
        mo_ref[...] = mn
        vo_ref[...] = vn

    full = pl.BlockSpec((tr, cols), lambda i: (i, 0))
    if two:
        g_specs = [pl.BlockSpec((tr, cols), lambda i: (jnp.minimum(i, nb - 1), 0)),
                   pl.BlockSpec((tr, cols), lambda i: (jnp.maximum(i - nb, 0), 0))]
    else:
        g_specs = [full]
    return pl.pallas_call(
        body, name=name, grid=(2 * nb if two else nb,), in_specs=[full, full, full] + g_specs, out_specs=[full] * 4,
        out_shape=[S((rows, cols), F32)] * 4, compiler_params=_cp(("parallel",)))(w, m, v, *gs)


WEIGHTS = ["ffn1_norm_pre", "ffn1_norm_post", "ffn1_w_in", "ffn1_w_out", "mix_norm_pre", "mix_norm_post", "mix_w_in",
           "gdn_conv_w", "gdn_a_log", "gdn_dt_bias", "gdn_norm_w", "gdn_w_o", "cnv_pw1_b", "cnv_dw_w", "cnv_dw_b",
           "cnv_ln_g", "cnv_ln_b", "cnv_w_o", "cnv_b_o", "mix_w_out", "ffn2_norm_pre", "ffn2_norm_post", "ffn2_w_in",
           "ffn2_w_out"]
BIG = {"ffn1_w_in": True, "ffn1_w_out": False, "mix_w_in": True, "gdn_conv_w": True, "gdn_w_o": False,
       "cnv_dw_w": True, "cnv_w_o": False, "mix_w_out": False, "ffn2_w_in": True, "ffn2_w_out": False}
TINY = {"gdn_conv_w": (32, LANES), "cnv_dw_w": (64, LANES)}
SMALL = [n for n in WEIGHTS if n not in BIG]
SUB = {"ffn1": ["ffn1_w_in", "ffn1_w_out"], "ffn2": ["ffn2_w_in", "ffn2_w_out"],
       "mix": ["mix_w_in", "gdn_conv_w", "gdn_w_o", "cnv_dw_w", "cnv_w_o", "mix_w_out"]}
GATHER_ON_FFN_MIX = [("ffn_in", ["mix_w_in"]), ("ffn_out", SUB["mix"][1:])]
RS_ON_FFN_MIX = [("ffn_in_bx", ["mix_w_in"]), ("ffn_out_bx", SUB["mix"][1:])]


def _whole(name, blocks):
    if BIG[name]:
        return jnp.transpose(blocks, (1, 0, 2)).reshape(blocks.shape[1], N_BLK * blocks.shape[2])
    return blocks.reshape(N_BLK * blocks.shape[1], blocks.shape[2])


def _blocks(name, whole):
    r, cfull = whole.shape
    if BIG[name]:
        blk = jnp.transpose(whole.reshape(r, N_BLK, cfull // N_BLK), (1, 0, 2))
    else:
        blk = whole.reshape(N_BLK, r // N_BLK, cfull)
    if name in TINY:
        tr, tc = TINY[name]
        flat = blk.reshape(N_BLK, -1)
        blk = jnp.pad(flat, ((0, 0), (0, tr * tc - flat.shape[1]))).reshape(N_BLK, tr, tc)
    return blk.astype(GRAD_DT)


def _pack(parts):
    rows = []
    for p in parts:
        flat = p.reshape(-1).astype(F32)
        rows.append(jnp.pad(flat, (0, (-flat.shape[0]) % LANES)).reshape(-1, LANES))
    out = jnp.concatenate(rows, axis=0)
    return jnp.pad(out, ((0, (-out.shape[0]) % 8), (0, 0)))


def _unpack(packed, shapes):
    out, r = [], 0
    for shp in shapes:
        size = math.prod(shp)
        nr = -(-size // LANES)
        out.append(packed[r:r + nr].reshape(-1)[:size].reshape(shp))
        r += nr
    return out


def kernel(x, ffn1_norm_pre, ffn1_norm_post, ffn1_w_in, ffn1_w_out, mix_norm_pre, mix_norm_post, mix_w_in, gdn_conv_w, gdn_a_log, gdn_dt_bias, gdn_norm_w, gdn_w_o, cnv_pw1_b, cnv_dw_w, cnv_dw_b, cnv_ln_g, cnv_ln_b, cnv_w_o, cnv_b_o, mix_w_out, ffn2_norm_pre, ffn2_norm_post, ffn2_w_in, ffn2_w_out, loss_target, m_ffn1_norm_pre, m_ffn1_norm_post, m_ffn1_w_in, m_ffn1_w_out, m_mix_norm_pre, m_mix_norm_post, m_mix_w_in, m_gdn_conv_w, m_gdn_a_log, m_gdn_dt_bias, m_gdn_norm_w, m_gdn_w_o, m_cnv_pw1_b, m_cnv_dw_w, m_cnv_dw_b, m_cnv_ln_g, m_cnv_ln_b, m_cnv_w_o, m_cnv_b_o, m_mix_w_out, m_ffn2_norm_pre, m_ffn2_norm_post, m_ffn2_w_in, m_ffn2_w_out, v_ffn1_norm_pre, v_ffn1_norm_post, v_ffn1_w_in, v_ffn1_w_out, v_mix_norm_pre, v_mix_norm_post, v_mix_w_in, v_gdn_conv_w, v_gdn_a_log, v_gdn_dt_bias, v_gdn_norm_w, v_gdn_w_o, v_cnv_pw1_b, v_cnv_dw_w, v_cnv_dw_b, v_cnv_ln_g, v_cnv_ln_b, v_cnv_w_o, v_cnv_b_o, v_mix_w_out, v_ffn2_norm_pre, v_ffn2_norm_post, v_ffn2_w_in, v_ffn2_w_out):
    args = locals()
    wts = {n: args[n] for n in WEIGHTS}
    mom = {n: args["m_" + n] for n in WEIGHTS}
    var = {n: args["v_" + n] for n in WEIGHTS}
    big = list(BIG)

    mx, my, mc = _pos()
    mb = 2 * mx + my
    cb_arr = jnp.stack([mc, mb]).astype(jnp.int32)

    own = {(n, l): wts[n][l].astype(BF16) for n in big for l in range(DEPTH)}

    def planned(make_job, arrays, plan):
        jobs = []
        for host, names in plan:
            jb = make_job([arrays[n] for n in names])
            jb.host, jb.names = host, names
            jobs.append(jb)
        return jobs

    def landed(jobs):
        res = {}
        for jb in jobs:
            if jb.results is None:
                run_job("comm_alone", jb)
            res.update(zip(jb.names, jb.results))
        return res

    stages = [(l, s) for l in range(DEPTH) for s in ("ffn1", "mix", "ffn2")]
    carried_by_ffn = lambda s: GATHER_ON_FFN_MIX if s == "mix" else [("ffn_in", [s + "_w_in"]), ("ffn_out", [s + "_w_out"])]
    carried_by_mix = lambda s: [("mix_in", [s + "_w_in", s + "_w_out"])]

    def gather_jobs(l, s, carrier):
        plan = carried_by_mix(s) if carrier == "mix" else carried_by_ffn(s)
        return planned(job_gather_ici, {n: own[(n, l)] for n in SUB[s]}, plan)

    def sub_weights(l, s, jobs):
        lands = landed(jobs)
        names = SUB[s]
        lands = dict(zip(names, run_job("gather_sib", job_gather_sibling([lands[n] for n in names]))))
        wl = {}
        for n in names:
            blocks = lax.dynamic_update_index_in_dim(lands[n], own[(n, l)], mb, 0)
            wl[n] = blocks if n in BLOCKED else _whole(n, blocks)
        wl.update({n: wts[n][l] for n in SMALL})
        return prep_mix(wl) if s == "mix" else prep_ffn(wl, s)

    act = x[0]
    saved, weights = {}, {}
    jobs = gather_jobs(0, "ffn1", "ffn")
    for i, (l, s) in enumerate(stages):
        weights[(l, s)] = sub_weights(l, s, jobs)
        jobs = gather_jobs(*stages[i + 1], "mix" if s == "mix" else "ffn") if i + 1 < len(stages) else []
        pending = list(jobs)
        fwd = mix_fwd if s == "mix" else ffn_fwd
        act, saved[(l, s)] = fwd("mix" if s == "mix" else "ffn", act, weights[(l, s)], pending)
    dx, loss = loss_head("loss", act, loss_target[0], _tm(act.shape[0]))

    def rs_jobs(l, s, g, carrier):
        if s == "mix":
            gw_s = unprep_mix(g)
        else:
            gw_s = unprep_ffn(g, s)
        small_grads[l].update({n: gw_s[n] for n in gw_s if n in SMALL})
        names = SUB[s]
        blocks = [gw_s[n + "#blocks"] if n in BLOCKED else _blocks(n, gw_s[n]) for n in names]
        parts = [add_half("add_half", b_, r, cb_arr) for b_, r in zip(blocks, rs_sibling(blocks))]
        partial_of.update({(n, l): p for n, p in zip(names, parts)})
        if carrier is None:
            plan = [(None, names)]
        elif carrier == "mix":
            plan = [("gintra_b", names)]
        elif s == "mix":
            plan = RS_ON_FFN_MIX
        else:
            plan = [("ffn_in_bx", [s + "_w_in"]), ("ffn_out_bx", [s + "_w_out"])]
        return planned(job_rs_chips, dict(zip(names, parts)), plan)

    small_grads = [{} for _ in range(DEPTH)]
    partial_of, chip_of = {}, {}
    jobs, jobs_key = [], None
    for i, (l, s) in enumerate(reversed(stages)):
        pending = list(jobs)
        bwd = mix_bwd if s == "mix" else ffn_bwd
        dx, g = bwd("mix" if s == "mix" else "ffn", dx, saved[(l, s)], weights[(l, s)], pending)
        if jobs:
            chip_of.update({(n, jobs_key): r for n, r in landed(jobs).items()})
        nxt = list(reversed(stages))[i + 1][1] if i + 1 < len(stages) else None
        jobs, jobs_key = rs_jobs(l, s, g, None if nxt is None else ("mix" if nxt == "mix" else "ffn")), l
    chip_of.update({(n, jobs_key): r for n, r in landed(jobs).items()})
    gw = small_grads

    small_shapes = [wts[n].shape for n in SMALL]
    packed = _pack([jnp.stack([gw[l][n] for l in range(DEPTH)]) for n in SMALL] + [loss])
    total = allreduce_small(packed)
    small_g = dict(zip(SMALL, _unpack(total, small_shapes)))
    loss_sum = total[sum(-(-math.prod(s) // LANES) for s in small_shapes), 0]

    keys = [(n, l) for l in range(DEPTH) for n in big]
    halves = [sum_chips("sum_chips", chip_of[k], partial_of[k], cb_arr) for k in keys]
    summed = dict(zip(keys, ag_sibling(halves)))

    out_g, out_d, out_m, out_v = {}, {}, {}, {}
    for n in big:
        shp = wts[n].shape
        gs = [summed[(n, l)] for l in range(DEPTH)]
        if n in TINY:
            gs = [jnp.concatenate([g.reshape(-1)[:shp[1] * shp[2]].reshape(shp[1], shp[2]) for g in gs], axis=0)]
        two_d = lambda a: a.reshape(DEPTH * shp[1], shp[2])
        res = adamw("adamw", two_d(wts[n]), two_d(mom[n]), two_d(var[n]), gs)
        out_g[n], out_d[n], out_m[n], out_v[n] = [r.reshape(shp) for r in res]

    pk = lambda d: _pack([d[n] for n in SMALL])
    res = adamw("adamw_small", pk(wts), pk(mom), pk(var), [pk(small_g)])
    for d, r in zip((out_g, out_d, out_m, out_v), res):
        d.update(dict(zip(SMALL, _unpack(r, small_shapes))))

    return (loss_sum, dx[None], *[out_g[n] for n in WEIGHTS], *[out_d[n] for n in WEIGHTS],
            *[out_m[n] for n in WEIGHTS], *[out_v[n] for n in WEIGHTS])
```

```python
import functools
import math

import jax
import jax.numpy as jnp
from jax import lax
from jax.experimental import pallas as pl
from jax.experimental.pallas import tpu as pltpu

F32, BF16 = jnp.float32, jnp.bfloat16
S = jax.ShapeDtypeStruct

D_MODEL = 1024
D_FF = 2816
HEADS = 8
DK = 128
CHUNK = 64
GDN_CONV = 4
CNV_K = 31
W_QKV = 3 * HEADS * DK
W_Z = HEADS * DK
W_GLU = 2 * D_MODEL
W_GATE = 2 * D_MODEL
P_IN = W_QKV + W_Z + 2 * HEADS + W_GLU + W_GATE
LANES = 128
P_ALL = W_QKV + W_Z + W_GLU + W_GATE + LANES
COL_Z = W_QKV // LANES
COL_GLU = (W_QKV + W_Z) // LANES
COL_GATE = (W_QKV + W_Z + W_GLU) // LANES
COL_BA = (W_QKV + W_Z + W_GLU + W_GATE) // LANES
RMS_EPS = 1e-6
LN_EPS = 1e-5
DEPTH = 2
N_BLK = 4
VMEM_LIMIT = 56 * 1024 * 1024
GRAD_DT = BF16

ADAM_LR, ADAM_B1, ADAM_B2, ADAM_EPS, ADAM_WD, ADAM_STEP = 0.001, 0.9, 0.999, 1e-08, 0.01, 10


def _cp(sem):
    return pltpu.CompilerParams(dimension_semantics=sem, vmem_limit_bytes=VMEM_LIMIT)


MM_VMEM_BUDGET = 36 * 1024 * 1024


def _mm_tiles(m, n, k_bytes_a, k_bytes_b, out_bytes, tn_fixed=None):
    best = None
    for tm in (1024, 512, 256, 128):
        if m % tm:
            continue
        for tn in ((tn_fixed,) if tn_fixed else (1024, 512, 640, 256, 384, 128)):
            if n % tn:
                continue
            need = 2 * (tm * k_bytes_a + tn * k_bytes_b + tm * tn * out_bytes)
            if need <= MM_VMEM_BUDGET and (best is None or tm * tn > best[0] * best[1]):
                best = (tm, tn)
    if best is None:
        raise ValueError((m, n, k_bytes_a, k_bytes_b))
    return best


def mm_nt_sum(name, parts, b):
    m, n = parts[0][0].shape[0], b.shape[0]
    k_total = sum(a.shape[1] for a, _ in parts)
    tm, tn = _mm_tiles(m, n, k_total * 2, k_total * 2, 4)
    n_p = len(parts)

    def body(*refs):
        o_ref = refs[2 * n_p]
        acc = None
        for a_ref, b_ref in zip(refs[:n_p], refs[n_p:2 * n_p]):
            t = lax.dot_general(a_ref[...], b_ref[...], (((1,), (1,)), ((), ())), preferred_element_type=F32)
            acc = t if acc is None else acc + t
        o_ref[...] = acc

    a_specs = [pl.BlockSpec((tm, a.shape[1]), lambda i, j: (i, 0)) for a, _ in parts]
    b_specs = [pl.BlockSpec((tn, a.shape[1]), functools.partial(lambda i, j, c: (j, c), c=col)) for a, col in parts]
    return pl.pallas_call(
        body, name=name, grid=(m // tm, n // tn), in_specs=a_specs + b_specs,
        out_specs=pl.BlockSpec((tm, tn), lambda i, j: (i, j)), out_shape=S((m, n), F32),
        compiler_params=_cp(("parallel", "parallel")))(*[a for a, _ in parts], *([b] * n_p))


def _take(jobs, host):
    for jb in jobs or []:
        if jb.host == host:
            jobs.remove(jb)
            return jb
    return None


def _pcall(body, job, *, name, grid, in_specs, out_specs, out_shape, scratch_shapes=(), semantics):
    in_specs, out_specs, out_shape, scratch_shapes = list(in_specs), list(out_specs), list(out_shape), list(scratch_shapes)
    if job is None:
        return lambda *args: pl.pallas_call(
            body, name=name, grid=grid, in_specs=in_specs, out_specs=out_specs, out_shape=out_shape,
            scratch_shapes=scratch_shapes, compiler_params=_cp(semantics))(*args)
    n_in, n_out, n_sc = len(in_specs), len(out_specs), len(scratch_shapes)
    n_i, n_o = len(job.ins), len(job.outs)

    def wrapped(*refs):
        cut = [n_in, n_i, n_out, n_o, n_sc]
        parts, pos = [], 0
        for c in cut:
            parts.append(refs[pos:pos + c])
            pos += c
        ins, j_in, outs, j_out, own = parts
        sems = refs[pos:]
        ids = [pl.program_id(d) for d in range(len(grid))]
        first = functools.reduce(lambda p, q: p & q, [i == 0 for i in ids])
        last = functools.reduce(lambda p, q: p & q, [i == g - 1 for i, g in zip(ids, grid)])

        @pl.when(first)
        def _():
            job.start(j_in, j_out, sems)

        body(*ins, *outs, *own)

        @pl.when(last)
        def _():
            job.finish(j_in, j_out, sems)

    def call(*args):
        res = pl.pallas_call(
            wrapped, name=name + "_c", grid=grid, in_specs=in_specs + [ANY] * n_i, out_specs=out_specs + [ANY] * n_o,
            out_shape=out_shape + list(job.outs),
            input_output_aliases={n_in + ki: n_out + ko for ki, ko in job.aliases.items()},
            scratch_shapes=scratch_shapes + job.scratch(),
            compiler_params=_cp(("arbitrary",) * len(grid)))(*args, *job.ins)
        job.results = res[n_out:]
        return res[:n_out]

    return call


def mm(name, a, b, ta=False, tb=False, out_dtype=F32, job=None, out_blocked=False):
    k = a.shape[0] if ta else a.shape[1]
    m = a.shape[1] if ta else a.shape[0]
    blocked = b.ndim == 3
    cb = b.shape[2] if blocked else None
    osz = jnp.dtype(out_dtype).itemsize
    if blocked and not tb:
        n = N_BLK * cb
        tm, tn = _mm_tiles(m, n, k * 2, k * 2, osz, tn_fixed=cb)
        b_spec = pl.BlockSpec((None, k, cb), lambda i, j: (j, 0, 0))
    elif blocked:
        n = b.shape[1]
        assert k == N_BLK * cb, (name, a.shape, b.shape)
        tm, tn = _mm_tiles(m, n, k * 2, k * 2, osz)
        b_spec = pl.BlockSpec((N_BLK, tn, cb), lambda i, j: (0, j, 0))
    else:
        n = b.shape[0] if tb else b.shape[1]
        assert k == (b.shape[1] if tb else b.shape[0]), (name, a.shape, b.shape)
        tm, tn = _mm_tiles(m, n, k * 2, k * 2, osz, tn_fixed=n // N_BLK if out_blocked else None)
        b_spec = pl.BlockSpec((tn, k), lambda i, j: (j, 0)) if tb else pl.BlockSpec((k, tn), lambda i, j: (0, j))
    a_spec = pl.BlockSpec((k, tm), lambda i, j: (0, i)) if ta else pl.BlockSpec((tm, k), lambda i, j: (i, 0))
    if out_blocked:
        o_spec, o_shape = pl.BlockSpec((None, tm, tn), lambda i, j: (j, i, 0)), S((N_BLK, m, tn), out_dtype)
    else:
        o_spec, o_shape = pl.BlockSpec((tm, tn), lambda i, j: (i, j)), S((m, n), out_dtype)
    dims = (((0 if ta else 1,), (1 if tb else 0,)), ((), ()))
    gm, gn = m // tm, n // tn

    def product(a_ref, b_ref):
        if blocked and tb:
            acc = None
            for q in range(N_BLK):
                t = lax.dot_general(a_ref[:, q * cb:(q + 1) * cb], b_ref[q], (((1,), (1,)), ((), ())),
                                    preferred_element_type=F32)
                acc = t if acc is None else acc + t
            return acc
        return lax.dot_general(a_ref[...], b_ref[...], dims, preferred_element_type=F32)

    def body(a_ref, b_ref, o_ref):
        o_ref[...] = product(a_ref, b_ref).astype(o_ref.dtype)

    return _pcall(body, job, name=name, grid=(gm, gn), in_specs=[a_spec, b_spec], out_specs=[o_spec],
                  out_shape=[o_shape], semantics=("parallel", "parallel"))(a, b)[0]


def ew_fwd(name, fn, grid, ins, outs):
    n_in = len(ins)

    def body(*refs):
        vals = [r[...].astype(F32) for r in refs[:n_in]]
        res = fn(pl.program_id(0), *vals)
        for r, v in zip(refs[n_in:], res):
            r[...] = v.astype(r.dtype)

    out = pl.pallas_call(
        body, name=name, grid=grid, in_specs=[s for _, s in ins], out_specs=[s for _, s in outs],
        out_shape=[sd for sd, _ in outs], compiler_params=_cp(("parallel", "parallel")))(*[a for a, _ in ins])
    return out


def ew_bwd(name, fn, grid, ins, cts, wrt, acc, add=None):
    n_in, n_ct, n_wrt, n_acc = len(ins), len(cts), len(wrt), len(acc)
    has_add = add is not None

    def body(*refs):
        in_refs = refs[:n_in]
        ct_refs = refs[n_in:n_in + n_ct]
        pos = n_in + n_ct
        add_ref = refs[pos] if has_add else None
        pos += 1 if has_add else 0
        wrt_refs = refs[pos:pos + n_wrt]
        acc_refs = refs[pos + n_wrt:pos + n_wrt + n_acc]
        col, tok = pl.program_id(0), pl.program_id(1)
        vals = [r[...].astype(F32) for r in in_refs]
        _, vjp = jax.vjp(lambda *a: fn(col, *a), *vals)
        grads = vjp(tuple(c[...].astype(F32) for c in ct_refs))
        for pos_w, ((idx, _, _), r) in enumerate(zip(wrt, wrt_refs)):
            g = grads[idx]
            if has_add and pos_w == 0:
                g = g + add_ref[...]
            r[...] = g.astype(r.dtype)
        for (idx, _, _, over_cols), r in zip(acc, acc_refs):
            first = (tok == 0) & (col == 0) if over_cols else tok == 0

            @pl.when(first)
            def _():
                r[...] = jnp.zeros_like(r)

            r[...] += grads[idx]

    arrays = [a for a, _ in ins] + [a for a, _ in cts] + ([add[0]] if has_add else [])
    in_specs = [s for _, s in ins] + [s for _, s in cts] + ([add[1]] if has_add else [])
    over_any = any(o for *_, o in acc)
    out = pl.pallas_call(
        body, name=name, grid=grid, in_specs=in_specs,
        out_specs=[s for _, _, s in wrt] + [s for _, _, s, _ in acc],
        out_shape=[sd for _, sd, _ in wrt] + [sd for _, sd, _, _ in acc],
        compiler_params=_cp(("arbitrary" if over_any else "parallel", "arbitrary")))(*arrays)
    return out


def _tok(width, col=0):
    return lambda tm: pl.BlockSpec((tm, width), lambda j, i: (i, col))


def _tokcol(off=0):
    return lambda tm: pl.BlockSpec((tm, LANES), lambda j, i: (i, off + j))


def _par(width, col=0):
    return pl.BlockSpec((1, width), lambda j, i: (0, col))


def _parcol(off=0):
    return pl.BlockSpec((1, LANES), lambda j, i: (0, off + j))


def _rms(x, w, eps=RMS_EPS):
    return x * lax.rsqrt(jnp.mean(x * x, axis=-1, keepdims=True) + eps) * w


def _silu(x):
    return x * jax.nn.sigmoid(x)


def fn_rms(col, x, w):
    return (_rms(x, w),)


def fn_swiglu(col, gate, up):
    return (_silu(gate) * up,)


def swiglu_bwd(name, u, da, tm):
    n_tok, f2 = u.shape
    f = f2 // 2

    def body(g_ref, up_ref, da_ref, o_ref):
        g, d = g_ref[...].astype(F32), da_ref[...].astype(F32)
        s = jax.nn.sigmoid(g)
        o_ref[:, :f] = (d * up_ref[...].astype(F32) * (s * (1.0 + g * (1.0 - s)))).astype(o_ref.dtype)
        o_ref[:, f:] = (d * (g * s)).astype(o_ref.dtype)

    half = lambda c: pl.BlockSpec((tm, f), lambda i: (i, c))
    return pl.pallas_call(
        body, name=name, grid=(n_tok // tm,), in_specs=[half(0), half(1), half(0)],
        out_specs=pl.BlockSpec((tm, f2), lambda i: (i, 0)), out_shape=S((n_tok, f2), BF16),
        compiler_params=_cp(("parallel",)))(u, u, da)


def fn_gdnout(col, o, z, nw):
    return (_rms(o, nw) * _silu(z),)


def fn_glu(col, a, g, ba, bg):
    return ((a + ba) * jax.nn.sigmoid(g + bg),)


def fn_lnsilu(col, h, g, b):
    mu = jnp.mean(h, axis=-1, keepdims=True)
    var = jnp.mean(jnp.square(h - mu), axis=-1, keepdims=True)
    return (_silu((h - mu) * lax.rsqrt(var + LN_EPS) * g + b),)


def fn_merge(col, ya, yb, ga, gb, bo):
    return (jax.nn.sigmoid(ga) * ya + jax.nn.sigmoid(gb) * (yb + bo),)


HALO = 32


def conv_fwd(name, x, col_off, n_ch, w, bias, tb):
    n_tok = x.shape[0]
    k = w.shape[0]
    nt = n_tok // tb

    def body(xp_ref, xc_ref, w_ref, *rest):
        if bias is not None:
            b_ref, o_ref, xs = rest
        else:
            o_ref, xs = rest
        i = pl.program_id(1)
        xs[0:HALO, :] = jnp.where(i == 0, 0.0, xp_ref[tb - HALO:tb, :].astype(F32))
        xs[HALO:HALO + tb, :] = xc_ref[...].astype(F32)
        acc = jnp.zeros((tb, LANES), F32)
        for j in range(k):
            s = k - 1 - j
            acc = acc + w_ref[j:j + 1, :] * xs[HALO - s:HALO - s + tb, :]
        if bias is not None:
            acc = acc + b_ref[...]
        o_ref[...] = acc

    in_specs = [pl.BlockSpec((tb, LANES), lambda j, i: (jnp.maximum(i - 1, 0), col_off + j)),
                pl.BlockSpec((tb, LANES), lambda j, i: (i, col_off + j)),
                pl.BlockSpec((k, LANES), lambda j, i: (0, j))]
    args = [x, x, w]
    if bias is not None:
        in_specs.append(pl.BlockSpec((1, LANES), lambda j, i: (0, j)))
        args.append(bias)
    return pl.pallas_call(
        body, name=name, grid=(n_ch // LANES, nt), in_specs=in_specs,
        out_specs=pl.BlockSpec((tb, LANES), lambda j, i: (i, j)), out_shape=S((n_tok, n_ch), F32),
        scratch_shapes=[pltpu.VMEM((HALO + tb, LANES), F32)],
        compiler_params=_cp(("parallel", "parallel")))(*args)


def conv_bwd(name, x, col_off, n_ch, w, dy, dx_dtype, tb):
    n_tok = x.shape[0]
    k = w.shape[0]
    nt = n_tok // tb

    def body(xp_ref, xc_ref, w_ref, dyc_ref, dyn_ref, dx_ref, dw_ref, db_ref, xs, dys):
        i = pl.program_id(1)
        xs[0:HALO, :] = jnp.where(i == 0, 0.0, xp_ref[tb - HALO:tb, :].astype(F32))
        xs[HALO:HALO + tb, :] = xc_ref[...].astype(F32)
        dyc = dyc_ref[...]
        dys[0:tb, :] = dyc
        dys[tb:tb + HALO, :] = jnp.where(i == nt - 1, 0.0, dyn_ref[0:HALO, :])

        @pl.when(i == 0)
        def _():
            dw_ref[...] = jnp.zeros_like(dw_ref)
            db_ref[...] = jnp.zeros_like(db_ref)

        acc = jnp.zeros((tb, LANES), F32)
        for j in range(k):
            s = k - 1 - j
            acc = acc + w_ref[j:j + 1, :] * dys[s:s + tb, :]
            dw_ref[j:j + 1, :] += jnp.sum(dyc * xs[HALO - s:HALO - s + tb, :], axis=0, keepdims=True)
        dx_ref[...] = acc.astype(dx_ref.dtype)
        db_ref[...] += jnp.sum(dyc, axis=0, keepdims=True)

    in_specs = [pl.BlockSpec((tb, LANES), lambda j, i: (jnp.maximum(i - 1, 0), col_off + j)),
                pl.BlockSpec((tb, LANES), lambda j, i: (i, col_off + j)),
                pl.BlockSpec((k, LANES), lambda j, i: (0, j)),
                pl.BlockSpec((tb, LANES), lambda j, i: (i, j)),
                pl.BlockSpec((tb, LANES), lambda j, i: (jnp.minimum(i + 1, nt - 1), j))]
    return pl.pallas_call(
        body, name=name, grid=(n_ch // LANES, nt), in_specs=in_specs,
        out_specs=[pl.BlockSpec((tb, LANES), lambda j, i: (i, j)),
                   pl.BlockSpec((k, LANES), lambda j, i: (0, j)),
                   pl.BlockSpec((1, LANES), lambda j, i: (0, j))],
        out_shape=[S((n_tok, n_ch), dx_dtype), S((k, n_ch), F32), S((1, n_ch), F32)],
        scratch_shapes=[pltpu.VMEM((HALO + tb, LANES), F32), pltpu.VMEM((tb + HALO, LANES), F32)],
        compiler_params=_cp(("parallel", "arbitrary")))(x, x, w, dy, dy)


def gdnconv_fwd(name, p, w, tb, job=None):
    n_tok = p.shape[0]
    k = w.shape[0]
    nt = n_tok // tb

    def body(xp_ref, xc_ref, w_ref, o_ref, xs):
        j, i = pl.program_id(0), pl.program_id(1)
        xs[0:HALO, :] = jnp.where(i == 0, 0.0, xp_ref[tb - HALO:tb, :].astype(F32))
        xs[HALO:HALO + tb, :] = xc_ref[...].astype(F32)
        c = jnp.zeros((tb, LANES), F32)
        for t in range(k):
            s = k - 1 - t
            c = c + w_ref[t:t + 1, :] * xs[HALO - s:HALO - s + tb, :]
        y = c * jax.nn.sigmoid(c)
        r = lax.rsqrt(jnp.sum(y * y, axis=-1, keepdims=True) + 1e-6) * jnp.where(j < HEADS, DK ** -0.5, 1.0)
        o_ref[...] = jnp.where(j < 2 * HEADS, y * r, y)

    return _pcall(
        body, job, name=name, grid=(W_QKV // LANES, nt),
        in_specs=[pl.BlockSpec((tb, LANES), lambda j, i: (jnp.maximum(i - 1, 0), j)),
                  pl.BlockSpec((tb, LANES), lambda j, i: (i, j)),
                  pl.BlockSpec((k, LANES), lambda j, i: (0, j))],
        out_specs=[pl.BlockSpec((tb, LANES), lambda j, i: (i, j))], out_shape=[S((n_tok, W_QKV), F32)],
        scratch_shapes=[pltpu.VMEM((HALO + tb, LANES), F32)], semantics=("parallel", "parallel"))(p, p, w)[0]


def gdnconv_bwd(name, p, w, dn, tb, job=None):
    n_tok = p.shape[0]
    k = w.shape[0]
    nt = n_tok // tb
    ext = tb + HALO

    def body(xp_ref, xc_ref, xn_ref, w_ref, dnc_ref, dnn_ref, dx_ref, dw_ref, xs, dns, dcs):
        j, i = pl.program_id(0), pl.program_id(1)
        xs[0:HALO, :] = jnp.where(i == 0, 0.0, xp_ref[tb - HALO:tb, :].astype(F32))
        xs[HALO:HALO + tb, :] = xc_ref[...].astype(F32)
        xs[HALO + tb:HALO + ext, :] = jnp.where(i == nt - 1, 0.0, xn_ref[0:HALO, :].astype(F32))
        dns[0:tb, :] = dnc_ref[...]
        dns[tb:ext, :] = jnp.where(i == nt - 1, 0.0, dnn_ref[0:HALO, :])

        @pl.when(i == 0)
        def _():
            dw_ref[...] = jnp.zeros_like(dw_ref)

        c = jnp.zeros((ext, LANES), F32)
        for t in range(k):
            s = k - 1 - t
            c = c + w_ref[t:t + 1, :] * xs[HALO - s:HALO - s + ext, :]
        d = dns[...]
        sg = jax.nn.sigmoid(c)
        y = c * sg
        r = lax.rsqrt(jnp.sum(y * y, axis=-1, keepdims=True) + 1e-6)
        scale = jnp.where(j < HEADS, DK ** -0.5, 1.0)
        dy_norm = scale * (d * r - y * (r * r * r) * jnp.sum(d * y, axis=-1, keepdims=True))
        dy = jnp.where(j < 2 * HEADS, dy_norm, d)
        dc = dy * (sg * (1.0 + c * (1.0 - sg)))
        dcs[...] = dc
        acc = jnp.zeros((tb, LANES), F32)
        for t in range(k):
            s = k - 1 - t
            acc = acc + w_ref[t:t + 1, :] * dcs[s:s + tb, :]
            dw_ref[t:t + 1, :] += jnp.sum(dcs[0:tb, :] * xs[HALO - s:HALO - s + tb, :], axis=0, keepdims=True)
        dx_ref[...] = acc.astype(dx_ref.dtype)

    cur = lambda j, i: (i, j)
    nxt = lambda j, i: (jnp.minimum(i + 1, nt - 1), j)
    return _pcall(
        body, job, name=name, grid=(W_QKV // LANES, nt),
        in_specs=[pl.BlockSpec((tb, LANES), lambda j, i: (jnp.maximum(i - 1, 0), j)),
                  pl.BlockSpec((tb, LANES), cur), pl.BlockSpec((tb, LANES), nxt),
                  pl.BlockSpec((k, LANES), lambda j, i: (0, j)),
                  pl.BlockSpec((tb, LANES), cur), pl.BlockSpec((tb, LANES), nxt)],
        out_specs=[pl.BlockSpec((tb, LANES), cur), pl.BlockSpec((k, LANES), lambda j, i: (0, j))],
        out_shape=[S((n_tok, W_QKV), BF16), S((k, W_QKV), F32)],
        scratch_shapes=[pltpu.VMEM((HALO + ext, LANES), F32), pltpu.VMEM((ext, LANES), F32),
                        pltpu.VMEM((ext, LANES), F32)],
        semantics=("parallel", "arbitrary"))(p, p, p, w, dn, dn)


GDN_GROUP = 2


def _dotb(a, b, ca, cb):
    return lax.dot_general(a.astype(BF16), b.astype(BF16), (((ca,), (cb,)), ((), ())), preferred_element_type=F32)


def _dot32(a, b, ca, cb):
    return lax.dot_general(a, b, (((ca,), (cb,)), ((), ())), preferred_element_type=F32,
                           precision=lax.Precision.HIGHEST)


def _dot3_many(xs, ys, ca, cb):
    xh = [x.astype(BF16) for x in xs]
    xl = [(x - h.astype(F32)).astype(BF16) for x, h in zip(xs, xh)]
    yh = [y.astype(BF16) for y in ys]
    yl = [(y - h.astype(F32)).astype(BF16) for y, h in zip(ys, yh)]
    dg = lambda p, q: lax.dot_general(p, q, (((ca,), (cb,)), ((), ())), preferred_element_type=F32)
    hh = [dg(p, q) for p, q in zip(xh, yh)]
    hl = [dg(p, q) for p, q in zip(xh, yl)]
    lh = [dg(p, q) for p, q in zip(xl, yh)]
    return [a + (b + c) for a, b, c in zip(hh, hl, lh)]


@jax.custom_vjp
def _mm3_many(xs, ys):
    return _dot3_many(xs, ys, 1, 0)


def _mm3_fwd(xs, ys):
    return _dot3_many(xs, ys, 1, 0), (xs, ys)


def _mm3_bwd(res, cts):
    xs, ys = res
    return _dot3_many(cts, ys, 1, 1), _dot3_many(xs, cts, 0, 0)


_mm3_many.defvjp(_mm3_fwd, _mm3_bwd)


@jax.custom_vjp
def _inv_unit_lower_many(mats):
    n = mats[0].shape[0]
    eye = (lax.broadcasted_iota(jnp.int32, (n, n), 0) == lax.broadcasted_iota(jnp.int32, (n, n), 1)).astype(F32)
    inv = [eye - a for a in mats]
    p = list(mats)
    for _ in range(int(math.log2(n)) - 1):
        p = _dot3_many(p, p, 1, 0)
        upd = _dot3_many(inv, p, 1, 0)
        inv = [i + u for i, u in zip(inv, upd)]
    return inv


def _inv_fwd(mats):
    t = _inv_unit_lower_many(mats)
    return t, t


def _inv_bwd(t, dt):
    x = _dot3_many(t, dt, 0, 0)
    return ([-y for y in _dot3_many(x, t, 1, 1)],)


_inv_unit_lower_many.defvjp(_inv_fwd, _inv_bwd)


@jax.custom_vjp
def _inv_given(mats, saved):
    return list(saved)


def _inv_given_fwd(mats, saved):
    return list(saved), list(saved)


def _inv_given_bwd(t, dt):
    return _inv_bwd(t, dt)[0], [jnp.zeros_like(s) for s in t]


_inv_given.defvjp(_inv_given_fwd, _inv_given_bwd)


def _softplus(x):
    return jnp.maximum(x, 0.0) + jnp.log(1.0 + jnp.exp(-jnp.abs(x)))


def _gdn_intra(qs, ks, vs, pbas, alog, dtb, tinv_saved=None):
    c = pbas[0].shape[0]
    row = lax.broadcasted_iota(jnp.int32, (c, c), 0)
    colm = lax.broadcasted_iota(jnp.int32, (c, c), 1)
    causal, strict = row >= colm, row > colm
    tril = causal.astype(F32)
    lane = lax.broadcasted_iota(jnp.int32, (1, LANES), 1)
    sub = lax.broadcasted_iota(jnp.int32, (LANES, 1), 0)
    last = (lax.broadcasted_iota(jnp.int32, (c, 1), 0) == c - 1).astype(F32)
    beta_all = [jax.nn.sigmoid(pb) for pb in pbas]
    g_all = [-jnp.exp(alog) * _softplus(pb + dtb) for pb in pbas]
    gc_all = [_dot32(tril, ga, 1, 0) for ga in g_all]
    gr_all = [_dot32(ga, tril, 0, 1) for ga in g_all]
    idx = [(g, h) for g in range(len(pbas)) for h in range(HEADS)]
    beta = [jnp.sum(beta_all[g] * (lane == h).astype(F32), axis=1, keepdims=True) for g, h in idx]
    gc = [jnp.sum(gc_all[g] * (lane == HEADS + h).astype(F32), axis=1, keepdims=True) for g, h in idx]
    gr = [jnp.sum(gr_all[g] * (sub == HEADS + h).astype(F32), axis=0, keepdims=True) for g, h in idx]
    decay = [jnp.where(causal, jnp.exp(jnp.where(causal, a - b, 0.0)), 0.0) for a, b in zip(gc, gr)]
    kk = [_dotb(k, k, 1, 1) for k in ks]
    a_mats = [jnp.where(strict, x * d * b, 0.0) for x, d, b in zip(kk, decay, beta)]
    tinv = _inv_unit_lower_many(a_mats) if tinv_saved is None else _inv_given(a_mats, tinv_saved)
    eg = [jnp.exp(a) for a in gc]
    g_last = [jnp.sum(a * last, axis=0, keepdims=True) for a in gc]
    us = _mm3_many(tinv, [v * b for v, b in zip(vs, beta)])
    ws = _mm3_many(tinv, [k * (b * e) for k, b, e in zip(ks, beta, eg)])
    qds = [q * e for q, e in zip(qs, eg)]
    kds = [k * jnp.exp(gl - a) for k, gl, a in zip(ks, g_last, gc)]
    qks = [_dotb(q, k, 1, 1) * d for q, k, d in zip(qs, ks, decay)]
    decs = [jnp.exp(gl) for gl in g_last]
    return us, ws, qds, kds, qks, decs, tinv


def _gdn_seq(us, ws, qds, kds, qks, decs, states):
    corr = [_dotb(w, st, 1, 0) for w, st in zip(ws, states)]
    from_state = [_dotb(qd, st, 1, 0) for qd, st in zip(qds, states)]
    v_new = [u - x for u, x in zip(us, corr)]
    intra = [_dotb(qk, vn, 1, 0) for qk, vn in zip(qks, v_new)]
    upd = [_dotb(kd, vn, 0, 0) for kd, vn in zip(kds, v_new)]
    outs = [a + b for a, b in zip(from_state, intra)]
    news = [st * d + x for st, d, x in zip(states, decs, upd)]
    return outs, news


def _heads(ref, rows=slice(None), base=0):
    return [ref[rows, (base + h) * DK:(base + h + 1) * DK].astype(F32) for h in range(HEADS)]


def _qk_heads(ref, rows=slice(None)):
    return [ref[rows, h * DK:h * DK + CHUNK].astype(F32) for h in range(HEADS)]


def _put_heads(ref, vals, rows=slice(None), base=0):
    for h in range(HEADS):
        ref[rows, (base + h) * DK:(base + h + 1) * DK] = vals[h].astype(ref.dtype)


def _put_qk(ref, vals, rows=slice(None)):
    for h in range(HEADS):
        ref[rows, h * DK:h * DK + CHUNK] = vals[h].astype(ref.dtype)
        ref[rows, h * DK + CHUNK:(h + 1) * DK] = jnp.zeros(vals[h].shape, ref.dtype)


def _group(n_chunks):
    return GDN_GROUP if n_chunks % GDN_GROUP == 0 else 1


def gdn_intra_fwd(name, qkvn, p, alog, dtb, job=None):
    n_tok = qkvn.shape[0]
    n = n_tok // CHUNK
    grp = _group(n)
    hd = HEADS * DK
    rb = grp * CHUNK

    def body(q_ref, k_ref, v_ref, pba_ref, al_ref, dt_ref, u_ref, w_ref, qd_ref, kd_ref, qk_ref, ti_ref, dec_ref):
        rows = [slice(g * CHUNK, (g + 1) * CHUNK) for g in range(grp)]
        cat = lambda ref: [t for r in rows for t in _heads(ref, r)]
        us, ws, qds, kds, qks, decs, tinv = _gdn_intra(cat(q_ref), cat(k_ref), cat(v_ref),
                                                       [pba_ref[r, :].astype(F32) for r in rows], al_ref[...], dt_ref[...])
        for g, r in enumerate(rows):
            part = slice(g * HEADS, (g + 1) * HEADS)
            _put_heads(u_ref, us[part], r)
            _put_heads(w_ref, ws[part], r)
            _put_heads(qd_ref, qds[part], r)
            _put_heads(kd_ref, kds[part], r)
            _put_qk(qk_ref, qks[part], r)
            _put_qk(ti_ref, tinv[part], r)
            for h in range(HEADS):
                dec_ref[g * HEADS + h:g * HEADS + h + 1, :] = jnp.broadcast_to(decs[g * HEADS + h], (1, LANES))

    blk = lambda c: pl.BlockSpec((rb, hd), lambda i: (i, c))
    par = pl.BlockSpec((1, LANES), lambda i: (0, 0))
    return _pcall(
        body, job, name=name, grid=(n // grp,),
        in_specs=[blk(0), blk(1), blk(2), pl.BlockSpec((rb, LANES), lambda i: (i, COL_BA)), par, par],
        out_specs=[blk(0)] * 6 + [pl.BlockSpec((grp * HEADS, LANES), lambda i: (i, 0))],
        out_shape=[S((n_tok, hd), F32)] + [S((n_tok, hd), BF16)] * 4 + [S((n_tok, hd), F32), S((n * HEADS, LANES), F32)],
        semantics=("parallel",))(qkvn, qkvn, qkvn, p, alog, dtb)


SEQ_GROUP = 2


def _decs_at(ref, g):
    return [ref[g * HEADS + h:g * HEADS + h + 1, 0:1] for h in range(HEADS)]


def gdn_seq_fwd(name, u, w, qd, kd, qk, dec):
    n_tok = u.shape[0]
    n = n_tok // CHUNK
    grp = SEQ_GROUP if n % SEQ_GROUP == 0 else 1
    hd = HEADS * DK

    def body(u_ref, w_ref, qd_ref, kd_ref, qk_ref, dec_ref, o_ref, s_ref, st):
        @pl.when(pl.program_id(0) == 0)
        def _():
            st[...] = jnp.zeros_like(st)

        states = [st[h * DK:(h + 1) * DK, :] for h in range(HEADS)]
        for g in range(grp):
            r = slice(g * CHUNK, (g + 1) * CHUNK)
            for h in range(HEADS):
                s_ref[g, h * DK:(h + 1) * DK, :] = states[h].astype(s_ref.dtype)
            outs, states = _gdn_seq(_heads(u_ref, r), _heads(w_ref, r), _heads(qd_ref, r), _heads(kd_ref, r),
                                    _qk_heads(qk_ref, r), _decs_at(dec_ref, g), states)
            _put_heads(o_ref, outs, r)
        for h in range(HEADS):
            st[h * DK:(h + 1) * DK, :] = states[h]

    blk = pl.BlockSpec((grp * CHUNK, hd), lambda i: (i, 0))
    return pl.pallas_call(
        body, name=name, grid=(n // grp,),
        in_specs=[blk] * 5 + [pl.BlockSpec((grp * HEADS, LANES), lambda i: (i, 0))],
        out_specs=[blk, pl.BlockSpec((grp, hd, DK), lambda i: (i, 0, 0))],
        out_shape=[S((n_tok, hd), F32), S((n, hd, DK), BF16)],
        scratch_shapes=[pltpu.VMEM((hd, DK), F32)],
        compiler_params=_cp(("arbitrary",)))(u, w, qd, kd, qk, dec)


def gdn_seq_bwd(name, u, w, qd, kd, qk, dec, states, do):
    n_tok = u.shape[0]
    n = n_tok // CHUNK
    grp = SEQ_GROUP if n % SEQ_GROUP == 0 else 1
    ns = n // grp
    hd = HEADS * DK

    def body(u_ref, w_ref, qd_ref, kd_ref, qk_ref, dec_ref, s_ref, do_ref,
             du_ref, dw_ref, dqd_ref, dkd_ref, dqk_ref, ddec_ref, dst):
        @pl.when(pl.program_id(0) == 0)
        def _():
            dst[...] = jnp.zeros_like(dst)

        d_news = [dst[h * DK:(h + 1) * DK, :] for h in range(HEADS)]
        for g in reversed(range(grp)):
            r = slice(g * CHUNK, (g + 1) * CHUNK)
            states = [s_ref[g, h * DK:(h + 1) * DK, :].astype(F32) for h in range(HEADS)]
            _, vjp = jax.vjp(_gdn_seq, _heads(u_ref, r), _heads(w_ref, r), _heads(qd_ref, r), _heads(kd_ref, r),
                             _qk_heads(qk_ref, r), _decs_at(dec_ref, g), states)
            du, dw, dqd, dkd, dqk, ddec, d_news = vjp((_heads(do_ref, r), d_news))
            _put_heads(du_ref, du, r)
            _put_heads(dw_ref, dw, r)
            _put_heads(dqd_ref, dqd, r)
            _put_heads(dkd_ref, dkd, r)
            _put_qk(dqk_ref, dqk, r)
            for h in range(HEADS):
                ddec_ref[g * HEADS + h:g * HEADS + h + 1, :] = jnp.broadcast_to(ddec[h], (1, LANES))
        for h in range(HEADS):
            dst[h * DK:(h + 1) * DK, :] = d_news[h]

    blk = pl.BlockSpec((grp * CHUNK, hd), lambda i: (ns - 1 - i, 0))
    dspec = pl.BlockSpec((grp * HEADS, LANES), lambda i: (ns - 1 - i, 0))
    return pl.pallas_call(
        body, name=name, grid=(ns,),
        in_specs=[blk] * 5 + [dspec, pl.BlockSpec((grp, hd, DK), lambda i: (ns - 1 - i, 0, 0)), blk],
        out_specs=[blk] * 5 + [dspec],
        out_shape=[S((n_tok, hd), F32)] * 5 + [S((n * HEADS, LANES), F32)],
        scratch_shapes=[pltpu.VMEM((hd, DK), F32)],
        compiler_params=_cp(("arbitrary",)))(u, w, qd, kd, qk, dec, states, do)


def gdn_intra_bwd(name, qkvn, p, alog, dtb, du, dw, dqd, dkd, dqk, ddec, tinv, job=None):
    n_tok = qkvn.shape[0]
    n = n_tok // CHUNK
    grp = _group(n)
    hd = HEADS * DK
    rb = grp * CHUNK

    def body(q_ref, k_ref, v_ref, pba_ref, al_ref, dt_ref, du_ref, dw_ref, dqd_ref, dkd_ref, dqk_ref, ddec_ref, ti_ref,
             dqkv_ref, dpba_ref, dal_ref, ddt_ref):
        @pl.when(pl.program_id(0) == 0)
        def _():
            dal_ref[...] = jnp.zeros_like(dal_ref)
            ddt_ref[...] = jnp.zeros_like(ddt_ref)

        rows = [slice(g * CHUNK, (g + 1) * CHUNK) for g in range(grp)]
        cat = lambda ref: [t for r in rows for t in _heads(ref, r)]
        kept = [t for r in rows for t in _qk_heads(ti_ref, r)]
        _, vjp = jax.vjp(lambda *a: _gdn_intra(*a, tinv_saved=kept)[:6], cat(q_ref), cat(k_ref), cat(v_ref),
                         [pba_ref[r, :].astype(F32) for r in rows], al_ref[...], dt_ref[...])
        cts = (cat(du_ref), cat(dw_ref), cat(dqd_ref), cat(dkd_ref), [t for r in rows for t in _qk_heads(dqk_ref, r)],
               [ddec_ref[i:i + 1, 0:1] for i in range(grp * HEADS)])
        dq, dk, dv, dpba, dal, ddt = vjp(cts)
        for g, r in enumerate(rows):
            part = slice(g * HEADS, (g + 1) * HEADS)
            _put_heads(dqkv_ref, dq[part], r, 0)
            _put_heads(dqkv_ref, dk[part], r, HEADS)
            _put_heads(dqkv_ref, dv[part], r, 2 * HEADS)
            dpba_ref[r, :] = dpba[g].astype(dpba_ref.dtype)
        dal_ref[...] += dal
        ddt_ref[...] += ddt

    blk = lambda c: pl.BlockSpec((rb, hd), lambda i: (i, c))
    par = pl.BlockSpec((1, LANES), lambda i: (0, 0))
    return _pcall(
        body, job, name=name, grid=(n // grp,),
        in_specs=[blk(0), blk(1), blk(2), pl.BlockSpec((rb, LANES), lambda i: (i, COL_BA)), par, par]
        + [blk(0)] * 5 + [pl.BlockSpec((grp * HEADS, LANES), lambda i: (i, 0)), blk(0)],
        out_specs=[pl.BlockSpec((rb, 3 * hd), lambda i: (i, 0)), pl.BlockSpec((rb, LANES), lambda i: (i, 0)), par, par],
        out_shape=[S((n_tok, 3 * hd), F32), S((n_tok, LANES), BF16), S((1, LANES), F32), S((1, LANES), F32)],
        semantics=("arbitrary",))(qkvn, qkvn, qkvn, p, alog, dtb, du, dw, dqd, dkd, dqk, ddec, tinv)


def loss_head(name, y, target, tm):
    n_tok, d = y.shape

    def body(y_ref, t_ref, dy_ref, l_ref):
        @pl.when(pl.program_id(0) == 0)
        def _():
            l_ref[...] = jnp.zeros_like(l_ref)

        e = y_ref[...] - t_ref[...]
        dy_ref[...] = e * (1.0 / d)
        l_ref[...] += jnp.sum(e * e, keepdims=True) * (0.5 / d)

    spec = pl.BlockSpec((tm, d), lambda i: (i, 0))
    return pl.pallas_call(
        body, name=name, grid=(n_tok // tm,), in_specs=[spec, spec],
        out_specs=[spec, pl.BlockSpec((1, 1), lambda i: (0, 0))], out_shape=[S((n_tok, d), F32), S((1, 1), F32)],
        compiler_params=_cp(("arbitrary",)))(y, target)


def _tm(n_tok):
    return min(512, n_tok)


def _gconv_tb(n_tok):
    return 1024 if n_tok % 1024 == 0 else _tm(n_tok)


def ffn_fwd(tag, x, w, jobs=None):
    n_tok = x.shape[0]
    tm = _tm(n_tok)
    g1 = (1, n_tok // tm)
    tD = _tok(D_MODEL)(tm)
    (h,) = ew_fwd(tag + "_rms", fn_rms, g1, [(x, tD), (w["norm_pre"], _par(D_MODEL))], [(S((n_tok, D_MODEL), BF16), tD)])
    u = mm(tag + "_in", h, w["w_in"], out_dtype=BF16, job=_take(jobs, "ffn_in"))
    tF = lambda c: _tok(D_FF, c)(tm)
    (a,) = ew_fwd(tag + "_swiglu", fn_swiglu, g1, [(u, tF(0)), (u, tF(1))], [(S((n_tok, D_FF), BF16), tF(0))])
    f = mm(tag + "_out", a, w["w_out"], job=_take(jobs, "ffn_out"))
    fn_res = lambda col, x_, f_, w_: (x_ + 0.5 * _rms(f_, w_),)
    (xo,) = ew_fwd(tag + "_res", fn_res, g1, [(x, tD), (f, tD), (w["norm_post"], _par(D_MODEL))],
                   [(S((n_tok, D_MODEL), F32), tD)])
    return xo, dict(x=x, h=h, u=u, a=a, f=f)


def ffn_bwd(tag, dxo, sv, w, jobs=None):
    n_tok = dxo.shape[0]
    tm = _tm(n_tok)
    g1 = (1, n_tok // tm)
    tD = _tok(D_MODEL)(tm)
    pD = _par(D_MODEL)
    fn_post = lambda col, f_, w_: (0.5 * _rms(f_, w_),)
    df, d_post = ew_bwd(tag + "_res_b", fn_post, g1, [(sv["f"], tD), (w["norm_post"], pD)], [(dxo, tD)],
                        [(0, S((n_tok, D_MODEL), BF16), tD)], [(1, S((1, D_MODEL), F32), pD, False)])
    da = mm(tag + "_out_bx", df, w["w_out"], tb=True, out_dtype=BF16, job=_take(jobs, "ffn_out_bx"))
    d_wout = mm(tag + "_out_bw", sv["a"], df, ta=True, out_dtype=GRAD_DT, job=_take(jobs, "ffn_out_bw"))
    tF = lambda c: _tok(D_FF, c)(tm)
    du = swiglu_bwd(tag + "_swiglu_b", sv["u"], da, tm)
    dh = mm(tag + "_in_bx", du, w["w_in"], tb=True, job=_take(jobs, "ffn_in_bx"))
    d_win = mm(tag + "_in_bw", sv["h"], du, ta=True, out_dtype=GRAD_DT, job=_take(jobs, "ffn_in_bw"), out_blocked=True)
    dx, d_pre = ew_bwd(tag + "_rms_b", fn_rms, g1, [(sv["x"], tD), (w["norm_pre"], pD)], [(dh, tD)],
                       [(0, S((n_tok, D_MODEL), F32), tD)], [(1, S((1, D_MODEL), F32), pD, False)], add=(dxo, tD))
    return dx, dict(norm_pre=d_pre, norm_post=d_post, w_in=d_win, w_out=d_wout)


def mix_fwd(tag, x, w, jobs=None):
    n_tok = x.shape[0]
    tm = _tm(n_tok)
    nt = n_tok // tm
    g1 = (1, nt)
    tD = _tok(D_MODEL)(tm)
    pD = _par(D_MODEL)
    (h,) = ew_fwd(tag + "_rms", fn_rms, g1, [(x, tD), (w["norm_pre"], pD)], [(S((n_tok, D_MODEL), BF16), tD)])
    p = mm(tag + "_in", h, w["w_all"], out_dtype=BF16, job=_take(jobs, "mix_in"))
    qkvn = gdnconv_fwd(tag + "_gconv", p, w["conv_w"], _gconv_tb(n_tok), job=_take(jobs, "gconv"))
    tC = _tokcol()(tm)
    *intra, tinv, dec = gdn_intra_fwd(tag + "_gintra", qkvn, p, w["alog"], w["dtb"], job=_take(jobs, "gintra"))
    intra.append(dec)
    o, states = gdn_seq_fwd(tag + "_gseq", *intra)
    (on,) = ew_fwd(tag + "_gout", fn_gdnout, (HEADS, nt),
                   [(o, tC), (p, _tokcol(COL_Z)(tm)), (w["gdn_norm_w"], _par(LANES))],
                   [(S((n_tok, D_MODEL), BF16), tC)])
    ya = mm(tag + "_go", on, w["gdn_w_o"], job=_take(jobs, "mix_small"))
    (hglu,) = ew_fwd(tag + "_glu", fn_glu, (D_MODEL // LANES, nt),
                     [(p, _tokcol(COL_GLU)(tm)), (p, _tokcol(COL_GLU + D_MODEL // LANES)(tm)),
                      (w["pw1_b"], _parcol(0)), (w["pw1_b"], _parcol(D_MODEL // LANES))],
                     [(S((n_tok, D_MODEL), F32), tC)])
    hc = conv_fwd(tag + "_cconv", hglu, 0, D_MODEL, w["dw_w"], w["dw_b"], tm)
    (hs,) = ew_fwd(tag + "_ln", fn_lnsilu, g1, [(hc, tD), (w["ln_g"], pD), (w["ln_b"], pD)],
                   [(S((n_tok, D_MODEL), BF16), tD)])
    yb = mm(tag + "_co", hs, w["cnv_w_o"], job=_take(jobs, "mix_small"))
    tG = lambda cb: pl.BlockSpec((tm, D_MODEL), lambda j, i: (i, cb))
    gcol = (W_QKV + W_Z + W_GLU) // D_MODEL
    (ym,) = ew_fwd(tag + "_merge", fn_merge, g1, [(ya, tD), (yb, tD), (p, tG(gcol)), (p, tG(gcol + 1)), (w["b_o"], pD)],
                   [(S((n_tok, D_MODEL), BF16), tD)])
    y = mm(tag + "_wo", ym, w["w_out"], job=_take(jobs, "mix_small"))
    fn_res = lambda col, x_, f_, w_: (x_ + _rms(f_, w_),)
    (xo,) = ew_fwd(tag + "_res", fn_res, g1, [(x, tD), (y, tD), (w["norm_post"], pD)], [(S((n_tok, D_MODEL), F32), tD)])
    sv = dict(x=x, h=h, p=p, qkvn=qkvn, intra=intra, tinv=tinv, states=states, o=o, on=on, ya=ya, hglu=hglu, hc=hc, hs=hs, yb=yb, ym=ym, y=y)
    return xo, sv


def mix_bwd(tag, dxo, sv, w, jobs=None):
    n_tok = dxo.shape[0]
    tm = _tm(n_tok)
    nt = n_tok // tm
    g1 = (1, nt)
    tD = _tok(D_MODEL)(tm)
    pD = _par(D_MODEL)
    tC = _tokcol()(tm)
    p = sv["p"]
    sD = lambda dt: S((n_tok, D_MODEL), dt)
    fn_post = lambda col, f_, w_: (_rms(f_, w_),)
    dy, d_post = ew_bwd(tag + "_res_b", fn_post, g1, [(sv["y"], tD), (w["norm_post"], pD)], [(dxo, tD)],
                        [(0, sD(BF16), tD)], [(1, S((1, D_MODEL), F32), pD, False)])
    dym = mm(tag + "_wo_bx", dy, w["w_out"], tb=True, job=_take(jobs, "mix_small"))
    d_wout = mm(tag + "_wo_bw", sv["ym"], dy, ta=True, out_dtype=GRAD_DT, job=_take(jobs, "mix_small"))
    tG = lambda cb: pl.BlockSpec((tm, D_MODEL), lambda j, i: (i, cb))
    gcol = (W_QKV + W_Z + W_GLU) // D_MODEL
    dya, dyb, dga, dgb, d_bo = ew_bwd(
        tag + "_merge_b", fn_merge, g1, [(sv["ya"], tD), (sv["yb"], tD), (p, tG(gcol)), (p, tG(gcol + 1)), (w["b_o"], pD)],
        [(dym, tD)], [(0, sD(BF16), tD), (1, sD(BF16), tD), (2, sD(BF16), tD), (3, sD(BF16), tD)],
        [(4, S((1, D_MODEL), F32), pD, False)])
    dhs = mm(tag + "_co_bx", dyb, w["cnv_w_o"], tb=True, job=_take(jobs, "mix_small"))
    d_cwo = mm(tag + "_co_bw", sv["hs"], dyb, ta=True, out_dtype=GRAD_DT, job=_take(jobs, "mix_small"))
    dhc, d_lng, d_lnb = ew_bwd(tag + "_ln_b", fn_lnsilu, g1, [(sv["hc"], tD), (w["ln_g"], pD), (w["ln_b"], pD)], [(dhs, tD)],
                               [(0, sD(F32), tD)], [(1, S((1, D_MODEL), F32), pD, False), (2, S((1, D_MODEL), F32), pD, False)])
    dhglu, d_dww, d_dwb = conv_bwd(tag + "_cconv_b", sv["hglu"], 0, D_MODEL, w["dw_w"], dhc, F32, tm)
    nc = D_MODEL // LANES
    dpa, dpg, d_ba, d_bg = ew_bwd(
        tag + "_glu_b", fn_glu, (nc, nt),
        [(p, _tokcol(COL_GLU)(tm)), (p, _tokcol(COL_GLU + nc)(tm)), (w["pw1_b"], _parcol(0)), (w["pw1_b"], _parcol(nc))],
        [(dhglu, tC)], [(0, sD(BF16), tC), (1, sD(BF16), tC)],
        [(2, S((1, D_MODEL), F32), _parcol(0), False), (3, S((1, D_MODEL), F32), _parcol(0), False)])
    don = mm(tag + "_go_bx", dya, w["gdn_w_o"], tb=True, job=_take(jobs, "mix_small"))
    d_gwo = mm(tag + "_go_bw", sv["on"], dya, ta=True, out_dtype=GRAD_DT, job=_take(jobs, "mix_small"))
    do, dz, d_gnw = ew_bwd(tag + "_gout_b", fn_gdnout, (HEADS, nt),
                           [(sv["o"], tC), (p, _tokcol(COL_Z)(tm)), (w["gdn_norm_w"], _par(LANES))], [(don, tC)],
                           [(0, sD(F32), tC), (1, sD(BF16), tC)], [(2, S((1, LANES), F32), _par(LANES), True)])
    d_intra = gdn_seq_bwd(tag + "_gseq_b", *sv["intra"], sv["states"], do)
    dqkvn, dpba, d_alog, d_dtb = gdn_intra_bwd(tag + "_gintra_b", sv["qkvn"], p, w["alog"], w["dtb"], *d_intra,
                                               sv["tinv"], job=_take(jobs, "gintra_b"))
    dqkv, d_convw = gdnconv_bwd(tag + "_gconv_b", p, w["conv_w"], dqkvn, _gconv_tb(n_tok), job=_take(jobs, "gconv_b"))
    nd = D_MODEL // LANES
    pieces = [(dqkv, 0, 0), (dz, W_QKV // D_MODEL, COL_Z), (dpa, COL_GLU // nd, COL_GLU), (dpg, COL_GLU // nd + 1, COL_GLU + nd),
              (dga, COL_GATE // nd, COL_GATE), (dgb, COL_GATE // nd + 1, COL_GATE + nd), (dpba, COL_BA, COL_BA)]
    dh = mm_nt_sum(tag + "_in_bx", [(a, blk) for a, blk, _ in pieces], w["w_all"])
    d_wall = [mm(tag + "_in_bw", sv["h"], a, ta=True, out_dtype=GRAD_DT, job=_take(jobs, "mix_small"))
              for a, _, _ in pieces]
    dx, d_pre = ew_bwd(tag + "_rms_b", fn_rms, g1, [(sv["x"], tD), (w["norm_pre"], pD)], [(dh, tD)],
                       [(0, sD(F32), tD)], [(1, S((1, D_MODEL), F32), pD, False)], add=(dxo, tD))
    grads = dict(norm_pre=d_pre, norm_post=d_post, w_all=d_wall, conv_w=d_convw, alog=d_alog, dtb=d_dtb,
                 gdn_norm_w=d_gnw, gdn_w_o=d_gwo, pw1_b=jnp.concatenate([d_ba, d_bg], axis=1), dw_w=d_dww,
                 dw_b=d_dwb, ln_g=d_lng, ln_b=d_lnb, cnv_w_o=d_cwo, b_o=d_bo, w_out=d_wout)
    return dx, grads


def local_step(x, target, layers):
    saved = []
    for lw in layers:
        x, sv = layer_fwd(x, lw)
        saved.append(sv)
    dx, loss = loss_head("loss", x, target, _tm(x.shape[0]))
    grads = [None] * len(layers)
    for i in reversed(range(len(layers))):
        dx, grads[i] = layer_bwd(dx, saved[i], layers[i])
    return loss, dx, grads


def layer_fwd(x, lw, jobs=None):
    x, s1 = ffn_fwd("ffn", x, lw["ffn1"], jobs)
    x, s2 = mix_fwd("mix", x, lw["mix"], jobs)
    x, s3 = ffn_fwd("ffn", x, lw["ffn2"], jobs)
    return x, (s1, s2, s3)


def layer_bwd(dx, saved, lw, jobs=None):
    s1, s2, s3 = saved
    dx, g3 = ffn_bwd("ffn", dx, s3, lw["ffn2"], jobs)
    dx, g2 = mix_bwd("mix", dx, s2, lw["mix"], jobs)
    dx, g1 = ffn_bwd("ffn", dx, s1, lw["ffn1"], jobs)
    return dx, dict(ffn1=g1, mix=g2, ffn2=g3)


_O_BA = W_QKV + W_Z
_O_GLU = _O_BA + 2 * HEADS


_MIX_BLK = P_IN // N_BLK
_MIX_B1 = _O_BA - _MIX_BLK
assert _O_BA + HEADS == 2 * _MIX_BLK
BLOCKED = ("ffn1_w_in", "ffn2_w_in", "mix_w_in")


def _row(v):
    return v.reshape(1, -1).astype(F32)


def prep_ffn(wl, k):
    w_in = wl[k + "_w_in"].astype(BF16)
    if w_in.ndim == 2:
        w_in = jnp.transpose(w_in.reshape(w_in.shape[0], N_BLK, -1), (1, 0, 2))
    return dict(norm_pre=_row(wl[k + "_norm_pre"]), norm_post=_row(wl[k + "_norm_post"]), w_in=w_in,
                w_out=wl[k + "_w_out"].astype(BF16))


def prep_layer(wl):
    return dict(ffn1=prep_ffn(wl, "ffn1"), mix=prep_mix(wl), ffn2=prep_ffn(wl, "ffn2"))


def prep_mix(wl):
    row = _row
    bf = lambda v: v.astype(BF16)
    lanes8 = lambda v: jnp.zeros((1, LANES), F32).at[0, HEADS:2 * HEADS].set(v.astype(F32))
    mw = bf(wl["mix_w_in"])
    pad = jnp.zeros((D_MODEL, LANES - 2 * HEADS), BF16)
    if mw.ndim == 3:
        w_all = jnp.concatenate([mw[0], mw[1][:, :_MIX_B1], mw[2][:, HEADS:], mw[3], mw[1][:, _MIX_B1:],
                                 mw[2][:, :HEADS], pad], axis=1)
    else:
        w_all = jnp.concatenate([mw[:, :_O_BA], mw[:, _O_GLU:], mw[:, _O_BA:_O_GLU], pad], axis=1)
    return dict(norm_pre=row(wl["mix_norm_pre"]), norm_post=row(wl["mix_norm_post"]), w_all=w_all,
               conv_w=wl["gdn_conv_w"].astype(F32), alog=lanes8(wl["gdn_a_log"]), dtb=lanes8(wl["gdn_dt_bias"]),
               gdn_norm_w=row(wl["gdn_norm_w"]), gdn_w_o=bf(wl["gdn_w_o"]), pw1_b=row(wl["cnv_pw1_b"]),
               dw_w=wl["cnv_dw_w"].astype(F32), dw_b=row(wl["cnv_dw_b"]), ln_g=row(wl["cnv_ln_g"]),
               ln_b=row(wl["cnv_ln_b"]), cnv_w_o=bf(wl["cnv_w_o"]), b_o=row(wl["cnv_b_o"]), w_out=bf(wl["mix_w_out"]))


def unprep_grads(g):
    return {**unprep_ffn(g["ffn1"], "ffn1"), **unprep_mix(g["mix"]), **unprep_ffn(g["ffn2"], "ffn2")}


def unprep_ffn(g, k):
    blk = g["w_in"]
    return {k + "_norm_pre": g["norm_pre"][0], k + "_norm_post": g["norm_post"][0], k + "_w_in#blocks": blk,
            k + "_w_in": jnp.transpose(blk, (1, 0, 2)).reshape(blk.shape[1], N_BLK * blk.shape[2]),
            k + "_w_out": g["w_out"]}


def unprep_mix(m):
    dqkv, dz, dpa, dpg, dga, dgb, dba = m["w_all"]
    out = {}
    out["mix_w_in#blocks"] = jnp.stack([
        dqkv[:, :_MIX_BLK], jnp.concatenate([dqkv[:, _MIX_BLK:], dz, dba[:, :HEADS]], axis=1),
        jnp.concatenate([dba[:, HEADS:2 * HEADS], dpa, dpg[:, :_MIX_B1 - D_MODEL]], axis=1),
        jnp.concatenate([dpg[:, _MIX_B1 - D_MODEL:], dga, dgb], axis=1)])
    out.update(
        mix_norm_pre=m["norm_pre"][0], mix_norm_post=m["norm_post"][0],
        mix_w_in=jnp.concatenate([dqkv, dz, dba[:, :2 * HEADS], dpa, dpg, dga, dgb], axis=1),
        gdn_conv_w=m["conv_w"], gdn_a_log=m["alog"][0, HEADS:2 * HEADS], gdn_dt_bias=m["dtb"][0, HEADS:2 * HEADS],
        gdn_norm_w=m["gdn_norm_w"][0], gdn_w_o=m["gdn_w_o"], cnv_pw1_b=m["pw1_b"][0], cnv_dw_w=m["dw_w"],
        cnv_dw_b=m["dw_b"][0], cnv_ln_g=m["ln_g"][0], cnv_ln_b=m["ln_b"][0], cnv_w_o=m["cnv_w_o"], cnv_b_o=m["b_o"][0],
        mix_w_out=m["w_out"])
    return out


MESH = pl.DeviceIdType.MESH
ANY = pl.BlockSpec(memory_space=pl.ANY)
N_DEV = 8


def _pos():
    return lax.axis_index("x"), lax.axis_index("y"), lax.axis_index("c")


def _other_chips(x, y):
    return [(1 - x, y), (x, 1 - y), (1 - x, 1 - y)]


class Job:
    def __init__(self, ins, outs, n_sems, copies, aliases=None):
        self.ins, self.outs, self.n_sems, self.copies = ins, outs, n_sems, copies
        self.aliases = aliases or {}
        self.results = None
        self.host = None

    def scratch(self):
        return [pltpu.SemaphoreType.DMA((self.n_sems,)), pltpu.SemaphoreType.DMA((self.n_sems,))]

    def start(self, in_refs, out_refs, sems):
        for cp in self.copies(in_refs, out_refs, sems, False):
            cp.start()

    def finish(self, in_refs, out_refs, sems):
        for cp in self.copies(in_refs, out_refs, sems, True):
            cp.wait_recv()
        for cp in self.copies(in_refs, out_refs, sems, False):
            cp.wait_send()


def run_job(name, job):
    n_i, n_o = len(job.ins), len(job.outs)

    def body(*refs):
        in_refs, out_refs, sems = refs[:n_i], refs[n_i:n_i + n_o], refs[n_i + n_o:]
        job.start(in_refs, out_refs, sems)
        job.finish(in_refs, out_refs, sems)

    job.results = pl.pallas_call(
        body, name=name, in_specs=[ANY] * n_i, out_specs=[ANY] * n_o, out_shape=job.outs,
        input_output_aliases=job.aliases, scratch_shapes=job.scratch())(*job.ins)
    return job.results


def _halved(rows):
    return rows % 32 == 0


def job_gather_ici(shards):
    n = len(shards)

    def copies(in_refs, out_refs, sems, recv):
        x, y, c = _pos()
        b = 2 * x + y
        chips = _other_chips(x, y)
        cps = []
        for a in range(n):
            hr = shards[a].shape[0] // 2
            for j in range(3):
                blk = 2 * chips[j][0] + chips[j][1] if recv else b
                if _halved(shards[a].shape[0]):
                    src, dst = in_refs[a].at[pl.ds(c * hr, hr)], out_refs[a].at[blk, pl.ds(c * hr, hr)]
                else:
                    src, dst = in_refs[a], out_refs[a].at[blk]
                cps.append(pltpu.make_async_remote_copy(
                    src_ref=src, dst_ref=dst, send_sem=sems[0].at[3 * a + j], recv_sem=sems[1].at[3 * a + j],
                    device_id=(chips[j][0], chips[j][1], c), device_id_type=MESH))
        return cps

    return Job(list(shards), [S((N_BLK,) + w.shape, w.dtype) for w in shards], 3 * n, copies)


def job_gather_sibling(lands):
    idx = [a for a, w in enumerate(lands) if _halved(w.shape[1])]

    def copies(in_refs, out_refs, sems, recv):
        x, y, c = _pos()
        chips = _other_chips(x, y)
        half = 1 - c if recv else c
        cps = []
        for pos, a in enumerate(idx):
            hr = lands[a].shape[1] // 2
            for j in range(3):
                rows = out_refs[a].at[2 * chips[j][0] + chips[j][1], pl.ds(half * hr, hr)]
                cps.append(pltpu.make_async_remote_copy(
                    src_ref=rows, dst_ref=rows, send_sem=sems[0].at[3 * pos + j], recv_sem=sems[1].at[3 * pos + j],
                    device_id=(x, y, 1 - c), device_id_type=MESH))
        return cps

    return Job(list(lands), [S(w.shape, w.dtype) for w in lands], 3 * len(idx), copies,
               aliases={a: a for a in range(len(lands))})


def job_rs_chips(ps):
    n = len(ps)

    def copies(in_refs, out_refs, sems, recv):
        x, y, c = _pos()
        b = 2 * x + y
        chips = _other_chips(x, y)
        cps = []
        for k in range(n):
            for j in range(3):
                other = 2 * chips[j][0] + chips[j][1]
                src_blk, dst_slot = (b, other) if recv else (other, b)
                cps.append(pltpu.make_async_remote_copy(
                    src_ref=in_refs[k].at[src_blk], dst_ref=out_refs[k].at[dst_slot], send_sem=sems[0].at[3 * k + j],
                    recv_sem=sems[1].at[3 * k + j], device_id=(chips[j][0], chips[j][1], c), device_id_type=MESH))
        return cps

    return Job(list(ps), [S(p.shape, p.dtype) for p in ps], 3 * n, copies)


def rs_sibling(gs):
    n = len(gs)

    def body(*refs):
        g_refs, r_refs = refs[:n], refs[n:2 * n]
        send_sems, recv_sems = refs[2 * n:]
        x, y, c = _pos()

        def cp(k):
            hr = gs[k].shape[1] // 2
            return pltpu.make_async_remote_copy(
                src_ref=g_refs[k].at[:, pl.ds((1 - c) * hr, hr)], dst_ref=r_refs[k], send_sem=send_sems.at[k],
                recv_sem=recv_sems.at[k], device_id=(x, y, 1 - c), device_id_type=MESH)

        cps = [cp(k) for k in range(n)]
        for d in cps:
            d.start()
        for d in cps:
            d.wait_recv()
        for d in cps:
            d.wait_send()

    return pl.pallas_call(
        body, name="rs_sibling", in_specs=[ANY] * n, out_specs=[ANY] * n,
        out_shape=[S((N_BLK, g.shape[1] // 2, g.shape[2]), g.dtype) for g in gs],
        scratch_shapes=[pltpu.SemaphoreType.DMA((n,)), pltpu.SemaphoreType.DMA((n,))])(*gs)


def ag_sibling(fs):
    n = len(fs)

    def body(*refs):
        o_refs = refs[n:2 * n]
        send_sems, recv_sems = refs[2 * n:]
        x, y, c = _pos()

        def cp(k, half):
            hr = fs[k].shape[0] // 2
            rows = o_refs[k].at[pl.ds(half * hr, hr)]
            return pltpu.make_async_remote_copy(
                src_ref=rows, dst_ref=rows, send_sem=send_sems.at[k], recv_sem=recv_sems.at[k],
                device_id=(x, y, 1 - c), device_id_type=MESH)

        cps = [cp(k, c) for k in range(n)]
        for d in cps:
            d.start()
        for k in range(n):
            cp(k, 1 - c).wait_recv()
        for d in cps:
            d.wait_send()

    return pl.pallas_call(
        body, name="ag_sibling", in_specs=[ANY] * n, out_specs=[ANY] * n,
        out_shape=[S(f.shape, f.dtype) for f in fs], input_output_aliases={k: k for k in range(n)},
        scratch_shapes=[pltpu.SemaphoreType.DMA((n,)), pltpu.SemaphoreType.DMA((n,))])(*fs)


def allreduce_small(v):
    rows = v.shape[0]

    def body(v_ref, o_ref, buf, send_sems, recv_sems):
        x, y, c = _pos()
        me = 4 * x + 2 * y + c
        buf[me] = v_ref[...]

        def cp(d, slot):
            dx, dy, dc = (d >> 2) & 1, (d >> 1) & 1, d & 1
            peer = (1 - x if dx else x, 1 - y if dy else y, 1 - c if dc else c)
            return pltpu.make_async_remote_copy(
                src_ref=v_ref, dst_ref=buf.at[slot], send_sem=send_sems.at[d - 1], recv_sem=recv_sems.at[d - 1],
                device_id=peer, device_id_type=MESH)

        cps = [cp(d, me) for d in range(1, N_DEV)]
        for d in cps:
            d.start()
        for d in range(1, N_DEV):
            dx, dy, dc = (d >> 2) & 1, (d >> 1) & 1, d & 1
            src = 4 * (1 - x if dx else x) + 2 * (1 - y if dy else y) + (1 - c if dc else c)
            cp(d, src).wait_recv()
        for d in cps:
            d.wait_send()
        acc = buf[0]
        for s in range(1, N_DEV):
            acc = acc + buf[s]
        o_ref[...] = acc

    vm = pl.BlockSpec(memory_space=pltpu.VMEM)
    return pl.pallas_call(
        body, name="allreduce_small", in_specs=[vm], out_specs=vm, out_shape=S(v.shape, v.dtype),
        scratch_shapes=[pltpu.VMEM((N_DEV, rows, LANES), F32), pltpu.SemaphoreType.DMA((N_DEV - 1,)),
                        pltpu.SemaphoreType.DMA((N_DEV - 1,))])(v)


def _rows_tile(rows, cols, cap_bytes=1 << 20, mult=8):
    best = None
    for t in range(mult, rows + 1, mult):
        if rows % t == 0 and t * cols * 4 <= cap_bytes:
            best = t
    return best if best is not None else rows


def add_half(name, g, r, c_arr):
    _, hr, cols = r.shape
    tr = _rows_tile(hr, cols, mult=16)
    nb = hr // tr

    def body(c_ref, g_ref, r_ref, o_ref):
        o_ref[...] = (g_ref[...].astype(F32) + r_ref[...].astype(F32)).astype(o_ref.dtype)

    gs = pltpu.PrefetchScalarGridSpec(
        num_scalar_prefetch=1, grid=(N_BLK, nb),
        in_specs=[pl.BlockSpec((None, tr, cols), lambda b, i, cr: (b, cr[0] * nb + i, 0)),
                  pl.BlockSpec((None, tr, cols), lambda b, i, cr: (b, i, 0))],
        out_specs=pl.BlockSpec((None, tr, cols), lambda b, i, cr: (b, i, 0)))
    return pl.pallas_call(body, name=name, grid_spec=gs, out_shape=S(r.shape, BF16),
                          compiler_params=_cp(("parallel", "parallel")))(c_arr, g, r)


def sum_chips(name, r, own, cb_arr):
    _, hr, cols = r.shape
    tr = _rows_tile(hr, cols, mult=16)
    nb = hr // tr

    def body(cb_ref, *refs):
        o_ref = refs[N_BLK + 1]
        b = cb_ref[1]
        acc = None
        for s in range(N_BLK):
            term = jnp.where(b == s, refs[N_BLK][...], refs[s][...]).astype(F32)
            acc = term if acc is None else acc + term
        o_ref[...] = acc

    slot = lambda s: pl.BlockSpec((None, tr, cols), lambda i, cb: (jnp.where(cb[1] == s, (s + 1) % N_BLK, s), i, 0))
    gs = pltpu.PrefetchScalarGridSpec(
        num_scalar_prefetch=1, grid=(nb,),
        in_specs=[slot(s) for s in range(N_BLK)] + [pl.BlockSpec((None, tr, cols), lambda i, cb: (cb[1], i, 0))],
        out_specs=pl.BlockSpec((tr, cols), lambda i, cb: (cb[0] * nb + i, 0)))
    return pl.pallas_call(body, name=name, grid_spec=gs, out_shape=S((2 * hr, cols), F32),
                          compiler_params=_cp(("parallel",)))(cb_arr, *([r] * N_BLK), own)


def adamw(name, w, m, v, gs):
    rows, cols = w.shape
    two = len(gs) == 2
    span = rows // 2 if two else rows
    tr = _rows_tile(span, cols, 1 << 19)
    nb = span // tr

    def body(w_ref, m_ref, v_ref, *rest):
        g_refs, (go_ref, d_ref, mo_ref, vo_ref) = rest[:len(gs)], rest[len(gs):]
        if two:
            g = jnp.where(pl.program_id(0) < nb, g_refs[0][...], g_refs[1][...])
        else:
            g = g_refs[0][...]
        mn = ADAM_B1 * m_ref[...] + (1.0 - ADAM_B1) * g
        vn = ADAM_B2 * v_ref[...] + (1.0 - ADAM_B2) * jnp.square(g)
        m_hat = mn / (1.0 - ADAM_B1 ** ADAM_STEP)
        v_hat = vn / (1.0 - ADAM_B2 ** ADAM_STEP)
        go_ref[...] = g
        d_ref[...] = -ADAM_LR * (m_hat / (jnp.sqrt(v_hat) + ADAM_EPS) + ADAM_WD * w_ref[...])
        mo_ref[...] = mn
        vo_ref[...] = vn

    full = pl.BlockSpec((tr, cols), lambda i: (i, 0))
    if two:
        g_specs = [pl.BlockSpec((tr, cols), lambda i: (jnp.minimum(i, nb - 1), 0)),
                   pl.BlockSpec((tr, cols), lambda i: (jnp.maximum(i - nb, 0), 0))]
    else:
        g_specs = [full]
    return pl.pallas_call(
        body, name=name, grid=(2 * nb if two else nb,), in_specs=[full, full, full] + g_specs, out_specs=[full] * 4,
        out_shape=[S((rows, cols), F32)] * 4, compiler_params=_cp(("parallel",)))(w, m, v, *gs)


WEIGHTS = ["ffn1_norm_pre", "ffn1_norm_post", "ffn1_w_in", "ffn1_w_out", "mix_norm_pre", "mix_norm_post", "mix_w_in",
           "gdn_conv_w", "gdn_a_log", "gdn_dt_bias", "gdn_norm_w", "gdn_w_o", "cnv_pw1_b", "cnv_dw_w", "cnv_dw_b",
           "cnv_ln_g", "cnv_ln_b", "cnv_w_o", "cnv_b_o", "mix_w_out", "ffn2_norm_pre", "ffn2_norm_post", "ffn2_w_in",
           "ffn2_w_out"]
BIG = {"ffn1_w_in": True, "ffn1_w_out": False, "mix_w_in": True, "gdn_conv_w": True, "gdn_w_o": False,
       "cnv_dw_w": True, "cnv_w_o": False, "mix_w_out": False, "ffn2_w_in": True, "ffn2_w_out": False}
TINY = {"gdn_conv_w": (32, LANES), "cnv_dw_w": (64, LANES)}
SMALL = [n for n in WEIGHTS if n not in BIG]
SUB = {"ffn1": ["ffn1_w_in", "ffn1_w_out"], "ffn2": ["ffn2_w_in", "ffn2_w_out"],
       "mix": ["mix_w_in", "gdn_conv_w", "gdn_w_o", "cnv_dw_w", "cnv_w_o", "mix_w_out"]}
GATHER_ON_FFN_MIX = [("ffn_in", ["mix_w_in"]), ("ffn_out", SUB["mix"][1:])]
RS_ON_FFN_MIX = [("ffn_in_bx", ["mix_w_in"]), ("ffn_out_bx", SUB["mix"][1:])]


def _whole(name, blocks):
    if BIG[name]:
        return jnp.transpose(blocks, (1, 0, 2)).reshape(blocks.shape[1], N_BLK * blocks.shape[2])
    return blocks.reshape(N_BLK * blocks.shape[1], blocks.shape[2])


def _blocks(name, whole):
    r, cfull = whole.shape
    if BIG[name]:
        blk = jnp.transpose(whole.reshape(r, N_BLK, cfull // N_BLK), (1, 0, 2))
    else:
        blk = whole.reshape(N_BLK, r // N_BLK, cfull)
    if name in TINY:
        tr, tc = TINY[name]
        flat = blk.reshape(N_BLK, -1)
        blk = jnp.pad(flat, ((0, 0), (0, tr * tc - flat.shape[1]))).reshape(N_BLK, tr, tc)
    return blk.astype(GRAD_DT)


def _pack(parts):
    rows = []
    for p in parts:
        flat = p.reshape(-1).astype(F32)
        rows.append(jnp.pad(flat, (0, (-flat.shape[0]) % LANES)).reshape(-1, LANES))
    out = jnp.concatenate(rows, axis=0)
    return jnp.pad(out, ((0, (-out.shape[0]) % 8), (0, 0)))


def _unpack(packed, shapes):
    out, r = [], 0
    for shp in shapes:
        size = math.prod(shp)
        nr = -(-size // LANES)
        out.append(packed[r:r + nr].reshape(-1)[:size].reshape(shp))
        r += nr
    return out


def kernel(x, ffn1_norm_pre, ffn1_norm_post, ffn1_w_in, ffn1_w_out, mix_norm_pre, mix_norm_post, mix_w_in, gdn_conv_w, gdn_a_log, gdn_dt_bias, gdn_norm_w, gdn_w_o, cnv_pw1_b, cnv_dw_w, cnv_dw_b, cnv_ln_g, cnv_ln_b, cnv_w_o, cnv_b_o, mix_w_out, ffn2_norm_pre, ffn2_norm_post, ffn2_w_in, ffn2_w_out, loss_target, m_ffn1_norm_pre, m_ffn1_norm_post, m_ffn1_w_in, m_ffn1_w_out, m_mix_norm_pre, m_mix_norm_post, m_mix_w_in, m_gdn_conv_w, m_gdn_a_log, m_gdn_dt_bias, m_gdn_norm_w, m_gdn_w_o, m_cnv_pw1_b, m_cnv_dw_w, m_cnv_dw_b, m_cnv_ln_g, m_cnv_ln_b, m_cnv_w_o, m_cnv_b_o, m_mix_w_out, m_ffn2_norm_pre, m_ffn2_norm_post, m_ffn2_w_in, m_ffn2_w_out, v_ffn1_norm_pre, v_ffn1_norm_post, v_ffn1_w_in, v_ffn1_w_out, v_mix_norm_pre, v_mix_norm_post, v_mix_w_in, v_gdn_conv_w, v_gdn_a_log, v_gdn_dt_bias, v_gdn_norm_w, v_gdn_w_o, v_cnv_pw1_b, v_cnv_dw_w, v_cnv_dw_b, v_cnv_ln_g, v_cnv_ln_b, v_cnv_w_o, v_cnv_b_o, v_mix_w_out, v_ffn2_norm_pre, v_ffn2_norm_post, v_ffn2_w_in, v_ffn2_w_out):
    args = locals()
    wts = {n: args[n] for n in WEIGHTS}
    mom = {n: args["m_" + n] for n in WEIGHTS}
    var = {n: args["v_" + n] for n in WEIGHTS}
    big = list(BIG)

    mx, my, mc = _pos()
    mb = 2 * mx + my
    cb_arr = jnp.stack([mc, mb]).astype(jnp.int32)

    own = {(n, l): wts[n][l].astype(BF16) for n in big for l in range(DEPTH)}

    def planned(make_job, arrays, plan):
        jobs = []
        for host, names in plan:
            jb = make_job([arrays[n] for n in names])
            jb.host, jb.names = host, names
            jobs.append(jb)
        return jobs

    def landed(jobs):
        res = {}
        for jb in jobs:
            if jb.results is None:
                run_job("comm_alone", jb)
            res.update(zip(jb.names, jb.results))
        return res

    stages = [(l, s) for l in range(DEPTH) for s in ("ffn1", "mix", "ffn2")]
    carried_by_ffn = lambda s: GATHER_ON_FFN_MIX if s == "mix" else [("ffn_in", [s + "_w_in"]), ("ffn_out", [s + "_w_out"])]
    carried_by_mix = lambda s: [("mix_in", [s + "_w_in", s + "_w_out"])]

    def gather_jobs(l, s, carrier):
        plan = carried_by_mix(s) if carrier == "mix" else carried_by_ffn(s)
        return planned(job_gather_ici, {n: own[(n, l)] for n in SUB[s]}, plan)

    def sub_weights(l, s, jobs):
        lands = landed(jobs)
        names = SUB[s]
        lands = dict(zip(names, run_job("gather_sib", job_gather_sibling([lands[n] for n in names]))))
        wl = {}
        for n in names:
            blocks = lax.dynamic_update_index_in_dim(lands[n], own[(n, l)], mb, 0)
            wl[n] = blocks if n in BLOCKED else _whole(n, blocks)
        wl.update({n: wts[n][l] for n in SMALL})
        return prep_mix(wl) if s == "mix" else prep_ffn(wl, s)

    act = x[0]
    saved, weights = {}, {}
    jobs = gather_jobs(0, "ffn1", "ffn")
    for i, (l, s) in enumerate(stages):
        weights[(l, s)] = sub_weights(l, s, jobs)
        jobs = gather_jobs(*stages[i + 1], "mix" if s == "mix" else "ffn") if i + 1 < len(stages) else []
        pending = list(jobs)
        fwd = mix_fwd if s == "mix" else ffn_fwd
        act, saved[(l, s)] = fwd("mix" if s == "mix" else "ffn", act, weights[(l, s)], pending)
    dx, loss = loss_head("loss", act, loss_target[0], _tm(act.shape[0]))

    def rs_jobs(l, s, g, carrier):
        if s == "mix":
            gw_s = unprep_mix(g)
        else:
            gw_s = unprep_ffn(g, s)
        small_grads[l].update({n: gw_s[n] for n in gw_s if n in SMALL})
        names = SUB[s]
        blocks = [gw_s[n + "#blocks"] if n in BLOCKED else _blocks(n, gw_s[n]) for n in names]
        parts = [add_half("add_half", b_, r, cb_arr) for b_, r in zip(blocks, rs_sibling(blocks))]
        partial_of.update({(n, l): p for n, p in zip(names, parts)})
        if carrier is None:
            plan = [(None, names)]
        elif carrier == "mix":
            plan = [("gintra_b", names)]
        elif s == "mix":
            plan = RS_ON_FFN_MIX
        else:
            plan = [("ffn_in_bx", [s + "_w_in"]), ("ffn_out_bx", [s + "_w_out"])]
        return planned(job_rs_chips, dict(zip(names, parts)), plan)

    small_grads = [{} for _ in range(DEPTH)]
    partial_of, chip_of = {}, {}
    jobs, jobs_key = [], None
    for i, (l, s) in enumerate(reversed(stages)):
        pending = list(jobs)
        bwd = mix_bwd if s == "mix" else ffn_bwd
        dx, g = bwd("mix" if s == "mix" else "ffn", dx, saved[(l, s)], weights[(l, s)], pending)
        if jobs:
            chip_of.update({(n, jobs_key): r for n, r in landed(jobs).items()})
        nxt = list(reversed(stages))[i + 1][1] if i + 1 < len(stages) else None
        jobs, jobs_key = rs_jobs(l, s, g, None if nxt is None else ("mix" if nxt == "mix" else "ffn")), l
    chip_of.update({(n, jobs_key): r for n, r in landed(jobs).items()})
    gw = small_grads

    small_shapes = [wts[n].shape for n in SMALL]
    packed = _pack([jnp.stack([gw[l][n] for l in range(DEPTH)]) for n in SMALL] + [loss])
    total = allreduce_small(packed)
    small_g = dict(zip(SMALL, _unpack(total, small_shapes)))
    loss_sum = total[sum(-(-math.prod(s) // LANES) for s in small_shapes), 0]

    keys = [(n, l) for l in range(DEPTH) for n in big]
    halves = [sum_chips("sum_chips", chip_of[k], partial_of[k], cb_arr) for k in keys]
    summed = dict(zip(keys, ag_sibling(halves)))

    out_g, out_d, out_m, out_v = {}, {}, {}, {}
    for n in big:
        shp = wts[n].shape
        gs = [summed[(n, l)] for l in range(DEPTH)]
        if n in TINY:
            gs = [jnp.concatenate([g.reshape(-1)[:shp[1] * shp[2]].reshape(shp[1], shp[2]) for g in gs], axis=0)]
        two_d = lambda a: a.reshape(DEPTH * shp[1], shp[2])
        res = adamw("adamw", two_d(wts[n]), two_d(mom[n]), two_d(var[n]), gs)
        out_g[n], out_d[n], out_m[n], out_v[n] = [r.reshape(shp) for r in res]

    pk = lambda d: _pack([d[n] for n in SMALL])
    res = adamw("adamw_small", pk(wts), pk(mom), pk(var), [pk(small_g)])
    for d, r in zip((out_g, out_d, out_m, out_v), res):
        d.update(dict(zip(SMALL, _unpack(r, small_shapes))))

    return (loss_sum, dx[None], *[out_g[n] for n in WEIGHTS], *[out_d[n] for n in WEIGHTS],
            *[out_m[n] for n in WEIGHTS], *[out_v[n] for n in WEIGHTS])
```

```python
import functools
import math

import jax
import jax.numpy as jnp
from jax import lax
from jax.experimental import pallas as pl
from jax.experimental.pallas import tpu as pltpu

F32, BF16 = jnp.float32, jnp.bfloat16
S = jax.ShapeDtypeStruct

D_MODEL = 1024
D_FF = 2816
HEADS = 8
DK = 128
CHUNK = 64
GDN_CONV = 4
CNV_K = 31
W_QKV = 3 * HEADS * DK
W_Z = HEADS * DK
W_GLU = 2 * D_MODEL
W_GATE = 2 * D_MODEL
P_IN = W_QKV + W_Z + 2 * HEADS + W_GLU + W_GATE
LANES = 128
P_ALL = W_QKV + W_Z + W_GLU + W_GATE + LANES
COL_Z = W_QKV // LANES
COL_GLU = (W_QKV + W_Z) // LANES
COL_GATE = (W_QKV + W_Z + W_GLU) // LANES
COL_BA = (W_QKV + W_Z + W_GLU + W_GATE) // LANES
RMS_EPS = 1e-6
LN_EPS = 1e-5
DEPTH = 2
N_BLK = 4
VMEM_LIMIT = 56 * 1024 * 1024
GRAD_DT = BF16

ADAM_LR, ADAM_B1, ADAM_B2, ADAM_EPS, ADAM_WD, ADAM_STEP = 0.001, 0.9, 0.999, 1e-08, 0.01, 10


def _cp(sem):
    return pltpu.CompilerParams(dimension_semantics=sem, vmem_limit_bytes=VMEM_LIMIT)


MM_VMEM_BUDGET = 36 * 1024 * 1024


def _mm_tiles(m, n, k_bytes_a, k_bytes_b, out_bytes, tn_fixed=None):
    best = None
    for tm in (1024, 512, 256, 128):
        if m % tm:
            continue
        for tn in ((tn_fixed,) if tn_fixed else (1024, 512, 640, 256, 384, 128)):
            if n % tn:
                continue
            need = 2 * (tm * k_bytes_a + tn * k_bytes_b + tm * tn * out_bytes)
            if need <= MM_VMEM_BUDGET and (best is None or tm * tn > best[0] * best[1]):
                best = (tm, tn)
    if best is None:
        raise ValueError((m, n, k_bytes_a, k_bytes_b))
    return best


def mm_nt_sum(name, parts, b):
    m, n = parts[0][0].shape[0], b.shape[0]
    k_total = sum(a.shape[1] for a, _ in parts)
    tm, tn = _mm_tiles(m, n, k_total * 2, k_total * 2, 4)
    n_p = len(parts)

    def body(*refs):
        o_ref = refs[2 * n_p]
        acc = None
        for a_ref, b_ref in zip(refs[:n_p], refs[n_p:2 * n_p]):
            t = lax.dot_general(a_ref[...], b_ref[...], (((1,), (1,)), ((), ())), preferred_element_type=F32)
            acc = t if acc is None else acc + t
        o_ref[...] = acc

    a_specs = [pl.BlockSpec((tm, a.shape[1]), lambda i, j: (i, 0)) for a, _ in parts]
    b_specs = [pl.BlockSpec((tn, a.shape[1]), functools.partial(lambda i, j, c: (j, c), c=col)) for a, col in parts]
    return pl.pallas_call(
        body, name=name, grid=(m // tm, n // tn), in_specs=a_specs + b_specs,
        out_specs=pl.BlockSpec((tm, tn), lambda i, j: (i, j)), out_shape=S((m, n), F32),
        compiler_params=_cp(("parallel", "parallel")))(*[a for a, _ in parts], *([b] * n_p))


def _take(jobs, host):
    for jb in jobs or []:
        if jb.host == host:
            jobs.remove(jb)
            return jb
    return None


def _pcall(body, job, *, name, grid, in_specs, out_specs, out_shape, scratch_shapes=(), semantics):
    in_specs, out_specs, out_shape, scratch_shapes = list(in_specs), list(out_specs), list(out_shape), list(scratch_shapes)
    if job is None:
        return lambda *args: pl.pallas_call(
            body, name=name, grid=grid, in_specs=in_specs, out_specs=out_specs, out_shape=out_shape,
            scratch_shapes=scratch_shapes, compiler_params=_cp(semantics))(*args)
    n_in, n_out, n_sc = len(in_specs), len(out_specs), len(scratch_shapes)
    n_i, n_o = len(job.ins), len(job.outs)

    def wrapped(*refs):
        cut = [n_in, n_i, n_out, n_o, n_sc]
        parts, pos = [], 0
        for c in cut:
            parts.append(refs[pos:pos + c])
            pos += c
        ins, j_in, outs, j_out, own = parts
        sems = refs[pos:]
        ids = [pl.program_id(d) for d in range(len(grid))]
        first = functools.reduce(lambda p, q: p & q, [i == 0 for i in ids])
        last = functools.reduce(lambda p, q: p & q, [i == g - 1 for i, g in zip(ids, grid)])

        @pl.when(first)
        def _():
            job.start(j_in, j_out, sems)

        body(*ins, *outs, *own)

        @pl.when(last)
        def _():
            job.finish(j_in, j_out, sems)

    def call(*args):
        res = pl.pallas_call(
            wrapped, name=name + "_c", grid=grid, in_specs=in_specs + [ANY] * n_i, out_specs=out_specs + [ANY] * n_o,
            out_shape=out_shape + list(job.outs),
            input_output_aliases={n_in + ki: n_out + ko for ki, ko in job.aliases.items()},
            scratch_shapes=scratch_shapes + job.scratch(),
            compiler_params=_cp(("arbitrary",) * len(grid)))(*args, *job.ins)
        job.results = res[n_out:]
        return res[:n_out]

    return call


def mm(name, a, b, ta=False, tb=False, out_dtype=F32, job=None, out_blocked=False):
    k = a.shape[0] if ta else a.shape[1]
    m = a.shape[1] if ta else a.shape[0]
    blocked = b.ndim == 3
    cb = b.shape[2] if blocked else None
    osz = jnp.dtype(out_dtype).itemsize
    if blocked and not tb:
        n = N_BLK * cb
        tm, tn = _mm_tiles(m, n, k * 2, k * 2, osz, tn_fixed=cb)
        b_spec = pl.BlockSpec((None, k, cb), lambda i, j: (j, 0, 0))
    elif blocked:
        n = b.shape[1]
        assert k == N_BLK * cb, (name, a.shape, b.shape)
        tm, tn = _mm_tiles(m, n, k * 2, k * 2, osz)
        b_spec = pl.BlockSpec((N_BLK, tn, cb), lambda i, j: (0, j, 0))
    else:
        n = b.shape[0] if tb else b.shape[1]
        assert k == (b.shape[1] if tb else b.shape[0]), (name, a.shape, b.shape)
        tm, tn = _mm_tiles(m, n, k * 2, k * 2, osz, tn_fixed=n // N_BLK if out_blocked else None)
        b_spec = pl.BlockSpec((tn, k), lambda i, j: (j, 0)) if tb else pl.BlockSpec((k, tn), lambda i, j: (0, j))
    a_spec = pl.BlockSpec((k, tm), lambda i, j: (0, i)) if ta else pl.BlockSpec((tm, k), lambda i, j: (i, 0))
    if out_blocked:
        o_spec, o_shape = pl.BlockSpec((None, tm, tn), lambda i, j: (j, i, 0)), S((N_BLK, m, tn), out_dtype)
    else:
        o_spec, o_shape = pl.BlockSpec((tm, tn), lambda i, j: (i, j)), S((m, n), out_dtype)
    dims = (((0 if ta else 1,), (1 if tb else 0,)), ((), ()))
    gm, gn = m // tm, n // tn

    def product(a_ref, b_ref):
        if blocked and tb:
            acc = None
            for q in range(N_BLK):
                t = lax.dot_general(a_ref[:, q * cb:(q + 1) * cb], b_ref[q], (((1,), (1,)), ((), ())),
                                    preferred_element_type=F32)
                acc = t if acc is None else acc + t
            return acc
        return lax.dot_general(a_ref[...], b_ref[...], dims, preferred_element_type=F32)

    def body(a_ref, b_ref, o_ref):
        o_ref[...] = product(a_ref, b_ref).astype(o_ref.dtype)

    return _pcall(body, job, name=name, grid=(gm, gn), in_specs=[a_spec, b_spec], out_specs=[o_spec],
                  out_shape=[o_shape], semantics=("parallel", "parallel"))(a, b)[0]


def ew_fwd(name, fn, grid, ins, outs):
    n_in = len(ins)

    def body(*refs):
        vals = [r[...].astype(F32) for r in refs[:n_in]]
        res = fn(pl.program_id(0), *vals)
        for r, v in zip(refs[n_in:], res):
            r[...] = v.astype(r.dtype)

    out = pl.pallas_call(
        body, name=name, grid=grid, in_specs=[s for _, s in ins], out_specs=[s for _, s in outs],
        out_shape=[sd for sd, _ in outs], compiler_params=_cp(("parallel", "parallel")))(*[a for a, _ in ins])
    return out


def ew_bwd(name, fn, grid, ins, cts, wrt, acc, add=None):
    n_in, n_ct, n_wrt, n_acc = len(ins), len(cts), len(wrt), len(acc)
    has_add = add is not None

    def body(*refs):
        in_refs = refs[:n_in]
        ct_refs = refs[n_in:n_in + n_ct]
        pos = n_in + n_ct
        add_ref = refs[pos] if has_add else None
        pos += 1 if has_add else 0
        wrt_refs = refs[pos:pos + n_wrt]
        acc_refs = refs[pos + n_wrt:pos + n_wrt + n_acc]
        col, tok = pl.program_id(0), pl.program_id(1)
        vals = [r[...].astype(F32) for r in in_refs]
        _, vjp = jax.vjp(lambda *a: fn(col, *a), *vals)
        grads = vjp(tuple(c[...].astype(F32) for c in ct_refs))
        for pos_w, ((idx, _, _), r) in enumerate(zip(wrt, wrt_refs)):
            g = grads[idx]
            if has_add and pos_w == 0:
                g = g + add_ref[...]
            r[...] = g.astype(r.dtype)
        for (idx, _, _, over_cols), r in zip(acc, acc_refs):
            first = (tok == 0) & (col == 0) if over_cols else tok == 0

            @pl.when(first)
            def _():
                r[...] = jnp.zeros_like(r)

            r[...] += grads[idx]

    arrays = [a for a, _ in ins] + [a for a, _ in cts] + ([add[0]] if has_add else [])
    in_specs = [s for _, s in ins] + [s for _, s in cts] + ([add[1]] if has_add else [])
    over_any = any(o for *_, o in acc)
    out = pl.pallas_call(
        body, name=name, grid=grid, in_specs=in_specs,
        out_specs=[s for _, _, s in wrt] + [s for _, _, s, _ in acc],
        out_shape=[sd for _, sd, _ in wrt] + [sd for _, sd, _, _ in acc],
        compiler_params=_cp(("arbitrary" if over_any else "parallel", "arbitrary")))(*arrays)
    return out


def _tok(width, col=0):
    return lambda tm: pl.BlockSpec((tm, width), lambda j, i: (i, col))


def _tokcol(off=0):
    return lambda tm: pl.BlockSpec((tm, LANES), lambda j, i: (i, off + j))


def _par(width, col=0):
    return pl.BlockSpec((1, width), lambda j, i: (0, col))


def _parcol(off=0):
    return pl.BlockSpec((1, LANES), lambda j, i: (0, off + j))


def _rms(x, w, eps=RMS_EPS):
    return x * lax.rsqrt(jnp.mean(x * x, axis=-1, keepdims=True) + eps) * w


def _silu(x):
    return x * jax.nn.sigmoid(x)


def fn_rms(col, x, w):
    return (_rms(x, w),)


def fn_swiglu(col, gate, up):
    return (_silu(gate) * up,)


def swiglu_bwd(name, u, da, tm):
    n_tok, f2 = u.shape
    f = f2 // 2

    def body(g_ref, up_ref, da_ref, o_ref):
        g, d = g_ref[...].astype(F32), da_ref[...].astype(F32)
        s = jax.nn.sigmoid(g)
        o_ref[:, :f] = (d * up_ref[...].astype(F32) * (s * (1.0 + g * (1.0 - s)))).astype(o_ref.dtype)
        o_ref[:, f:] = (d * (g * s)).astype(o_ref.dtype)

    half = lambda c: pl.BlockSpec((tm, f), lambda i: (i, c))
    return pl.pallas_call(
        body, name=name, grid=(n_tok // tm,), in_specs=[half(0), half(1), half(0)],
        out_specs=pl.BlockSpec((tm, f2), lambda i: (i, 0)), out_shape=S((n_tok, f2), BF16),
        compiler_params=_cp(("parallel",)))(u, u, da)


def fn_gdnout(col, o, z, nw):
    return (_rms(o, nw) * _silu(z),)


def fn_glu(col, a, g, ba, bg):
    return ((a + ba) * jax.nn.sigmoid(g + bg),)


def fn_lnsilu(col, h, g, b):
    mu = jnp.mean(h, axis=-1, keepdims=True)
    var = jnp.mean(jnp.square(h - mu), axis=-1, keepdims=True)
    return (_silu((h - mu) * lax.rsqrt(var + LN_EPS) * g + b),)


def fn_merge(col, ya, yb, ga, gb, bo):
    return (jax.nn.sigmoid(ga) * ya + jax.nn.sigmoid(gb) * (yb + bo),)


HALO = 32


def conv_fwd(name, x, col_off, n_ch, w, bias, tb):
    n_tok = x.shape[0]
    k = w.shape[0]
    nt = n_tok // tb

    def body(xp_ref, xc_ref, w_ref, *rest):
        if bias is not None:
            b_ref, o_ref, xs = rest
        else:
            o_ref, xs = rest
        i = pl.program_id(1)
        xs[0:HALO, :] = jnp.where(i == 0, 0.0, xp_ref[tb - HALO:tb, :].astype(F32))
        xs[HALO:HALO + tb, :] = xc_ref[...].astype(F32)
        acc = jnp.zeros((tb, LANES), F32)
        for j in range(k):
            s = k - 1 - j
            acc = acc + w_ref[j:j + 1, :] * xs[HALO - s:HALO - s + tb, :]
        if bias is not None:
            acc = acc + b_ref[...]
        o_ref[...] = acc

    in_specs = [pl.BlockSpec((tb, LANES), lambda j, i: (jnp.maximum(i - 1, 0), col_off + j)),
                pl.BlockSpec((tb, LANES), lambda j, i: (i, col_off + j)),
                pl.BlockSpec((k, LANES), lambda j, i: (0, j))]
    args = [x, x, w]
    if bias is not None:
        in_specs.append(pl.BlockSpec((1, LANES), lambda j, i: (0, j)))
        args.append(bias)
    return pl.pallas_call(
        body, name=name, grid=(n_ch // LANES, nt), in_specs=in_specs,
        out_specs=pl.BlockSpec((tb, LANES), lambda j, i: (i, j)), out_shape=S((n_tok, n_ch), F32),
        scratch_shapes=[pltpu.VMEM((HALO + tb, LANES), F32)],
        compiler_params=_cp(("parallel", "parallel")))(*args)


def conv_bwd(name, x, col_off, n_ch, w, dy, dx_dtype, tb):
    n_tok = x.shape[0]
    k = w.shape[0]
    nt = n_tok // tb

    def body(xp_ref, xc_ref, w_ref, dyc_ref, dyn_ref, dx_ref, dw_ref, db_ref, xs, dys):
        i = pl.program_id(1)
        xs[0:HALO, :] = jnp.where(i == 0, 0.0, xp_ref[tb - HALO:tb, :].astype(F32))
        xs[HALO:HALO + tb, :] = xc_ref[...].astype(F32)
        dyc = dyc_ref[...]
        dys[0:tb, :] = dyc
        dys[tb:tb + HALO, :] = jnp.where(i == nt - 1, 0.0, dyn_ref[0:HALO, :])

        @pl.when(i == 0)
        def _():
            dw_ref[...] = jnp.zeros_like(dw_ref)
            db_ref[...] = jnp.zeros_like(db_ref)

        acc = jnp.zeros((tb, LANES), F32)
        for j in range(k):
            s = k - 1 - j
            acc = acc + w_ref[j:j + 1, :] * dys[s:s + tb, :]
            dw_ref[j:j + 1, :] += jnp.sum(dyc * xs[HALO - s:HALO - s + tb, :], axis=0, keepdims=True)
        dx_ref[...] = acc.astype(dx_ref.dtype)
        db_ref[...] += jnp.sum(dyc, axis=0, keepdims=True)

    in_specs = [pl.BlockSpec((tb, LANES), lambda j, i: (jnp.maximum(i - 1, 0), col_off + j)),
                pl.BlockSpec((tb, LANES), lambda j, i: (i, col_off + j)),
                pl.BlockSpec((k, LANES), lambda j, i: (0, j)),
                pl.BlockSpec((tb, LANES), lambda j, i: (i, j)),
                pl.BlockSpec((tb, LANES), lambda j, i: (jnp.minimum(i + 1, nt - 1), j))]
    return pl.pallas_call(
        body, name=name, grid=(n_ch // LANES, nt), in_specs=in_specs,
        out_specs=[pl.BlockSpec((tb, LANES), lambda j, i: (i, j)),
                   pl.BlockSpec((k, LANES), lambda j, i: (0, j)),
                   pl.BlockSpec((1, LANES), lambda j, i: (0, j))],
        out_shape=[S((n_tok, n_ch), dx_dtype), S((k, n_ch), F32), S((1, n_ch), F32)],
        scratch_shapes=[pltpu.VMEM((HALO + tb, LANES), F32), pltpu.VMEM((tb + HALO, LANES), F32)],
        compiler_params=_cp(("parallel", "arbitrary")))(x, x, w, dy, dy)


def gdnconv_fwd(name, p, w, tb, job=None):
    n_tok = p.shape[0]
    k = w.shape[0]
    nt = n_tok // tb

    def body(xp_ref, xc_ref, w_ref, o_ref, xs):
        j, i = pl.program_id(0), pl.program_id(1)
        xs[0:HALO, :] = jnp.where(i == 0, 0.0, xp_ref[tb - HALO:tb, :].astype(F32))
        xs[HALO:HALO + tb, :] = xc_ref[...].astype(F32)
        c = jnp.zeros((tb, LANES), F32)
        for t in range(k):
            s = k - 1 - t
            c = c + w_ref[t:t + 1, :] * xs[HALO - s:HALO - s + tb, :]
        y = c * jax.nn.sigmoid(c)
        r = lax.rsqrt(jnp.sum(y * y, axis=-1, keepdims=True) + 1e-6) * jnp.where(j < HEADS, DK ** -0.5, 1.0)
        o_ref[...] = jnp.where(j < 2 * HEADS, y * r, y)

    return _pcall(
        body, job, name=name, grid=(W_QKV // LANES, nt),
        in_specs=[pl.BlockSpec((tb, LANES), lambda j, i: (jnp.maximum(i - 1, 0), j)),
                  pl.BlockSpec((tb, LANES), lambda j, i: (i, j)),
                  pl.BlockSpec((k, LANES), lambda j, i: (0, j))],
        out_specs=[pl.BlockSpec((tb, LANES), lambda j, i: (i, j))], out_shape=[S((n_tok, W_QKV), F32)],
        scratch_shapes=[pltpu.VMEM((HALO + tb, LANES), F32)], semantics=("parallel", "parallel"))(p, p, w)[0]


def gdnconv_bwd(name, p, w, dn, tb, job=None):
    n_tok = p.shape[0]
    k = w.shape[0]
    nt = n_tok // tb
    ext = tb + HALO

    def body(xp_ref, xc_ref, xn_ref, w_ref, dnc_ref, dnn_ref, dx_ref, dw_ref, xs, dns, dcs):
        j, i = pl.program_id(0), pl.program_id(1)
        xs[0:HALO, :] = jnp.where(i == 0, 0.0, xp_ref[tb - HALO:tb, :].astype(F32))
        xs[HALO:HALO + tb, :] = xc_ref[...].astype(F32)
        xs[HALO + tb:HALO + ext, :] = jnp.where(i == nt - 1, 0.0, xn_ref[0:HALO, :].astype(F32))
        dns[0:tb, :] = dnc_ref[...]
        dns[tb:ext, :] = jnp.where(i == nt - 1, 0.0, dnn_ref[0:HALO, :])

        @pl.when(i == 0)
        def _():
            dw_ref[...] = jnp.zeros_like(dw_ref)

        c = jnp.zeros((ext, LANES), F32)
        for t in range(k):
            s = k - 1 - t
            c = c + w_ref[t:t + 1, :] * xs[HALO - s:HALO - s + ext, :]
        d = dns[...]
        sg = jax.nn.sigmoid(c)
        y = c * sg
        r = lax.rsqrt(jnp.sum(y * y, axis=-1, keepdims=True) + 1e-6)
        scale = jnp.where(j < HEADS, DK ** -0.5, 1.0)
        dy_norm = scale * (d * r - y * (r * r * r) * jnp.sum(d * y, axis=-1, keepdims=True))
        dy = jnp.where(j < 2 * HEADS, dy_norm, d)
        dc = dy * (sg * (1.0 + c * (1.0 - sg)))
        dcs[...] = dc
        acc = jnp.zeros((tb, LANES), F32)
        for t in range(k):
            s = k - 1 - t
            acc = acc + w_ref[t:t + 1, :] * dcs[s:s + tb, :]
            dw_ref[t:t + 1, :] += jnp.sum(dcs[0:tb, :] * xs[HALO - s:HALO - s + tb, :], axis=0, keepdims=True)
        dx_ref[...] = acc.astype(dx_ref.dtype)

    cur = lambda j, i: (i, j)
    nxt = lambda j, i: (jnp.minimum(i + 1, nt - 1), j)
    return _pcall(
        body, job, name=name, grid=(W_QKV // LANES, nt),
        in_specs=[pl.BlockSpec((tb, LANES), lambda j, i: (jnp.maximum(i - 1, 0), j)),
                  pl.BlockSpec((tb, LANES), cur), pl.BlockSpec((tb, LANES), nxt),
                  pl.BlockSpec((k, LANES), lambda j, i: (0, j)),
                  pl.BlockSpec((tb, LANES), cur), pl.BlockSpec((tb, LANES), nxt)],
        out_specs=[pl.BlockSpec((tb, LANES), cur), pl.BlockSpec((k, LANES), lambda j, i: (0, j))],
        out_shape=[S((n_tok, W_QKV), BF16), S((k, W_QKV), F32)],
        scratch_shapes=[pltpu.VMEM((HALO + ext, LANES), F32), pltpu.VMEM((ext, LANES), F32),
                        pltpu.VMEM((ext, LANES), F32)],
        semantics=("parallel", "arbitrary"))(p, p, p, w, dn, dn)


GDN_GROUP = 2


def _dotb(a, b, ca, cb):
    return lax.dot_general(a.astype(BF16), b.astype(BF16), (((ca,), (cb,)), ((), ())), preferred_element_type=F32)


def _dot32(a, b, ca, cb):
    return lax.dot_general(a, b, (((ca,), (cb,)), ((), ())), preferred_element_type=F32,
                           precision=lax.Precision.HIGHEST)


def _dot3_many(xs, ys, ca, cb):
    xh = [x.astype(BF16) for x in xs]
    xl = [(x - h.astype(F32)).astype(BF16) for x, h in zip(xs, xh)]
    yh = [y.astype(BF16) for y in ys]
    yl = [(y - h.astype(F32)).astype(BF16) for y, h in zip(ys, yh)]
    dg = lambda p, q: lax.dot_general(p, q, (((ca,), (cb,)), ((), ())), preferred_element_type=F32)
    hh = [dg(p, q) for p, q in zip(xh, yh)]
    hl = [dg(p, q) for p, q in zip(xh, yl)]
    lh = [dg(p, q) for p, q in zip(xl, yh)]
    return [a + (b + c) for a, b, c in zip(hh, hl, lh)]


@jax.custom_vjp
def _mm3_many(xs, ys):
    return _dot3_many(xs, ys, 1, 0)


def _mm3_fwd(xs, ys):
    return _dot3_many(xs, ys, 1, 0), (xs, ys)


def _mm3_bwd(res, cts):
    xs, ys = res
    return _dot3_many(cts, ys, 1, 1), _dot3_many(xs, cts, 0, 0)


_mm3_many.defvjp(_mm3_fwd, _mm3_bwd)


@jax.custom_vjp
def _inv_unit_lower_many(mats):
    n = mats[0].shape[0]
    eye = (lax.broadcasted_iota(jnp.int32, (n, n), 0) == lax.broadcasted_iota(jnp.int32, (n, n), 1)).astype(F32)
    inv = [eye - a for a in mats]
    p = list(mats)
    for _ in range(int(math.log2(n)) - 1):
        p = _dot3_many(p, p, 1, 0)
        upd = _dot3_many(inv, p, 1, 0)
        inv = [i + u for i, u in zip(inv, upd)]
    return inv


def _inv_fwd(mats):
    t = _inv_unit_lower_many(mats)
    return t, t


def _inv_bwd(t, dt):
    x = _dot3_many(t, dt, 0, 0)
    return ([-y for y in _dot3_many(x, t, 1, 1)],)


_inv_unit_lower_many.defvjp(_inv_fwd, _inv_bwd)


@jax.custom_vjp
def _inv_given(mats, saved):
    return list(saved)


def _inv_given_fwd(mats, saved):
    return list(saved), list(saved)


def _inv_given_bwd(t, dt):
    return _inv_bwd(t, dt)[0], [jnp.zeros_like(s) for s in t]


_inv_given.defvjp(_inv_given_fwd, _inv_given_bwd)


def _softplus(x):
    return jnp.maximum(x, 0.0) + jnp.log(1.0 + jnp.exp(-jnp.abs(x)))


def _gdn_intra(qs, ks, vs, pbas, alog, dtb, tinv_saved=None):
    c = pbas[0].shape[0]
    row = lax.broadcasted_iota(jnp.int32, (c, c), 0)
    colm = lax.broadcasted_iota(jnp.int32, (c, c), 1)
    causal, strict = row >= colm, row > colm
    tril = causal.astype(F32)
    lane = lax.broadcasted_iota(jnp.int32, (1, LANES), 1)
    sub = lax.broadcasted_iota(jnp.int32, (LANES, 1), 0)
    last = (lax.broadcasted_iota(jnp.int32, (c, 1), 0) == c - 1).astype(F32)
    beta_all = [jax.nn.sigmoid(pb) for pb in pbas]
    g_all = [-jnp.exp(alog) * _softplus(pb + dtb) for pb in pbas]
    gc_all = [_dot32(tril, ga, 1, 0) for ga in g_all]
    gr_all = [_dot32(ga, tril, 0, 1) for ga in g_all]
    idx = [(g, h) for g in range(len(pbas)) for h in range(HEADS)]
    beta = [jnp.sum(beta_all[g] * (lane == h).astype(F32), axis=1, keepdims=True) for g, h in idx]
    gc = [jnp.sum(gc_all[g] * (lane == HEADS + h).astype(F32), axis=1, keepdims=True) for g, h in idx]
    gr = [jnp.sum(gr_all[g] * (sub == HEADS + h).astype(F32), axis=0, keepdims=True) for g, h in idx]
    decay = [jnp.where(causal, jnp.exp(jnp.where(causal, a - b, 0.0)), 0.0) for a, b in zip(gc, gr)]
    kk = [_dotb(k, k, 1, 1) for k in ks]
    a_mats = [jnp.where(strict, x * d * b, 0.0) for x, d, b in zip(kk, decay, beta)]
    tinv = _inv_unit_lower_many(a_mats) if tinv_saved is None else _inv_given(a_mats, tinv_saved)
    eg = [jnp.exp(a) for a in gc]
    g_last = [jnp.sum(a * last, axis=0, keepdims=True) for a in gc]
    us = _mm3_many(tinv, [v * b for v, b in zip(vs, beta)])
    ws = _mm3_many(tinv, [k * (b * e) for k, b, e in zip(ks, beta, eg)])
    qds = [q * e for q, e in zip(qs, eg)]
    kds = [k * jnp.exp(gl - a) for k, gl, a in zip(ks, g_last, gc)]
    qks = [_dotb(q, k, 1, 1) * d for q, k, d in zip(qs, ks, decay)]
    decs = [jnp.exp(gl) for gl in g_last]
    return us, ws, qds, kds, qks, decs, tinv


def _gdn_seq(us, ws, qds, kds, qks, decs, states):
    corr = [_dotb(w, st, 1, 0) for w, st in zip(ws, states)]
    from_state = [_dotb(qd, st, 1, 0) for qd, st in zip(qds, states)]
    v_new = [u - x for u, x in zip(us, corr)]
    intra = [_dotb(qk, vn, 1, 0) for qk, vn in zip(qks, v_new)]
    upd = [_dotb(kd, vn, 0, 0) for kd, vn in zip(kds, v_new)]
    outs = [a + b for a, b in zip(from_state, intra)]
    news = [st * d + x for st, d, x in zip(states, decs, upd)]
    return outs, news


def _heads(ref, rows=slice(None), base=0):
    return [ref[rows, (base + h) * DK:(base + h + 1) * DK].astype(F32) for h in range(HEADS)]


def _qk_heads(ref, rows=slice(None)):
    return [ref[rows, h * DK:h * DK + CHUNK].astype(F32) for h in range(HEADS)]


def _put_heads(ref, vals, rows=slice(None), base=0):
    for h in range(HEADS):
        ref[rows, (base + h) * DK:(base + h + 1) * DK] = vals[h].astype(ref.dtype)


def _put_qk(ref, vals, rows=slice(None)):
    for h in range(HEADS):
        ref[rows, h * DK:h * DK + CHUNK] = vals[h].astype(ref.dtype)
        ref[rows, h * DK + CHUNK:(h + 1) * DK] = jnp.zeros(vals[h].shape, ref.dtype)


def _group(n_chunks):
    return GDN_GROUP if n_chunks % GDN_GROUP == 0 else 1


def gdn_intra_fwd(name, qkvn, p, alog, dtb, job=None):
    n_tok = qkvn.shape[0]
    n = n_tok // CHUNK
    grp = _group(n)
    hd = HEADS * DK
    rb = grp * CHUNK

    def body(q_ref, k_ref, v_ref, pba_ref, al_ref, dt_ref, u_ref, w_ref, qd_ref, kd_ref, qk_ref, ti_ref, dec_ref):
        rows = [slice(g * CHUNK, (g + 1) * CHUNK) for g in range(grp)]
        cat = lambda ref: [t for r in rows for t in _heads(ref, r)]
        us, ws, qds, kds, qks, decs, tinv = _gdn_intra(cat(q_ref), cat(k_ref), cat(v_ref),
                                                       [pba_ref[r, :].astype(F32) for r in rows], al_ref[...], dt_ref[...])
        for g, r in enumerate(rows):
            part = slice(g * HEADS, (g + 1) * HEADS)
            _put_heads(u_ref, us[part], r)
            _put_heads(w_ref, ws[part], r)
            _put_heads(qd_ref, qds[part], r)
            _put_heads(kd_ref, kds[part], r)
            _put_qk(qk_ref, qks[part], r)
            _put_qk(ti_ref, tinv[part], r)
            for h in range(HEADS):
                dec_ref[g * HEADS + h:g * HEADS + h + 1, :] = jnp.broadcast_to(decs[g * HEADS + h], (1, LANES))

    blk = lambda c: pl.BlockSpec((rb, hd), lambda i: (i, c))
    par = pl.BlockSpec((1, LANES), lambda i: (0, 0))
    return _pcall(
        body, job, name=name, grid=(n // grp,),
        in_specs=[blk(0), blk(1), blk(2), pl.BlockSpec((rb, LANES), lambda i: (i, COL_BA)), par, par],
        out_specs=[blk(0)] * 6 + [pl.BlockSpec((grp * HEADS, LANES), lambda i: (i, 0))],
        out_shape=[S((n_tok, hd), F32)] + [S((n_tok, hd), BF16)] * 4 + [S((n_tok, hd), F32), S((n * HEADS, LANES), F32)],
        semantics=("parallel",))(qkvn, qkvn, qkvn, p, alog, dtb)


SEQ_GROUP = 2


def _decs_at(ref, g):
    return [ref[g * HEADS + h:g * HEADS + h + 1, 0:1] for h in range(HEADS)]


def gdn_seq_fwd(name, u, w, qd, kd, qk, dec):
    n_tok = u.shape[0]
    n = n_tok // CHUNK
    grp = SEQ_GROUP if n % SEQ_GROUP == 0 else 1
    hd = HEADS * DK

    def body(u_ref, w_ref, qd_ref, kd_ref, qk_ref, dec_ref, o_ref, s_ref, st):
        @pl.when(pl.program_id(0) == 0)
        def _():
            st[...] = jnp.zeros_like(st)

        states = [st[h * DK:(h + 1) * DK, :] for h in range(HEADS)]
        for g in range(grp):
            r = slice(g * CHUNK, (g + 1) * CHUNK)
            for h in range(HEADS):
                s_ref[g, h * DK:(h + 1) * DK, :] = states[h].astype(s_ref.dtype)
            outs, states = _gdn_seq(_heads(u_ref, r), _heads(w_ref, r), _heads(qd_ref, r), _heads(kd_ref, r),
                                    _qk_heads(qk_ref, r), _decs_at(dec_ref, g), states)
            _put_heads(o_ref, outs, r)
        for h in range(HEADS):
            st[h * DK:(h + 1) * DK, :] = states[h]

    blk = pl.BlockSpec((grp * CHUNK, hd), lambda i: (i, 0))
    return pl.pallas_call(
        body, name=name, grid=(n // grp,),
        in_specs=[blk] * 5 + [pl.BlockSpec((grp * HEADS, LANES), lambda i: (i, 0))],
        out_specs=[blk, pl.BlockSpec((grp, hd, DK), lambda i: (i, 0, 0))],
        out_shape=[S((n_tok, hd), F32), S((n, hd, DK), BF16)],
        scratch_shapes=[pltpu.VMEM((hd, DK), F32)],
        compiler_params=_cp(("arbitrary",)))(u, w, qd, kd, qk, dec)


def gdn_seq_bwd(name, u, w, qd, kd, qk, dec, states, do):
    n_tok = u.shape[0]
    n = n_tok // CHUNK
    grp = SEQ_GROUP if n % SEQ_GROUP == 0 else 1
    ns = n // grp
    hd = HEADS * DK

    def body(u_ref, w_ref, qd_ref, kd_ref, qk_ref, dec_ref, s_ref, do_ref,
             du_ref, dw_ref, dqd_ref, dkd_ref, dqk_ref, ddec_ref, dst):
        @pl.when(pl.program_id(0) == 0)
        def _():
            dst[...] = jnp.zeros_like(dst)

        d_news = [dst[h * DK:(h + 1) * DK, :] for h in range(HEADS)]
        for g in reversed(range(grp)):
            r = slice(g * CHUNK, (g + 1) * CHUNK)
            states = [s_ref[g, h * DK:(h + 1) * DK, :].astype(F32) for h in range(HEADS)]
            _, vjp = jax.vjp(_gdn_seq, _heads(u_ref, r), _heads(w_ref, r), _heads(qd_ref, r), _heads(kd_ref, r),
                             _qk_heads(qk_ref, r), _decs_at(dec_ref, g), states)
            du, dw, dqd, dkd, dqk, ddec, d_news = vjp((_heads(do_ref, r), d_news))
            _put_heads(du_ref, du, r)
            _put_heads(dw_ref, dw, r)
            _put_heads(dqd_ref, dqd, r)
            _put_heads(dkd_ref, dkd, r)
            _put_qk(dqk_ref, dqk, r)
            for h in range(HEADS):
                ddec_ref[g * HEADS + h:g * HEADS + h + 1, :] = jnp.broadcast_to(ddec[h], (1, LANES))
        for h in range(HEADS):
            dst[h * DK:(h + 1) * DK, :] = d_news[h]

    blk = pl.BlockSpec((grp * CHUNK, hd), lambda i: (ns - 1 - i, 0))
    dspec = pl.BlockSpec((grp * HEADS, LANES), lambda i: (ns - 1 - i, 0))
    return pl.pallas_call(
        body, name=name, grid=(ns,),
        in_specs=[blk] * 5 + [dspec, pl.BlockSpec((grp, hd, DK), lambda i: (ns - 1 - i, 0, 0)), blk],
        out_specs=[blk] * 5 + [dspec],
        out_shape=[S((n_tok, hd), F32)] * 5 + [S((n * HEADS, LANES), F32)],
        scratch_shapes=[pltpu.VMEM((hd, DK), F32)],
        compiler_params=_cp(("arbitrary",)))(u, w, qd, kd, qk, dec, states, do)


def gdn_intra_bwd(name, qkvn, p, alog, dtb, du, dw, dqd, dkd, dqk, ddec, tinv, job=None):
    n_tok = qkvn.shape[0]
    n = n_tok // CHUNK
    grp = _group(n)
    hd = HEADS * DK
    rb = grp * CHUNK

    def body(q_ref, k_ref, v_ref, pba_ref, al_ref, dt_ref, du_ref, dw_ref, dqd_ref, dkd_ref, dqk_ref, ddec_ref, ti_ref,
             dqkv_ref, dpba_ref, dal_ref, ddt_ref):
        @pl.when(pl.program_id(0) == 0)
        def _():
            dal_ref[...] = jnp.zeros_like(dal_ref)
            ddt_ref[...] = jnp.zeros_like(ddt_ref)

        rows = [slice(g * CHUNK, (g + 1) * CHUNK) for g in range(grp)]
        cat = lambda ref: [t for r in rows for t in _heads(ref, r)]
        kept = [t for r in rows for t in _qk_heads(ti_ref, r)]
        _, vjp = jax.vjp(lambda *a: _gdn_intra(*a, tinv_saved=kept)[:6], cat(q_ref), cat(k_ref), cat(v_ref),
                         [pba_ref[r, :].astype(F32) for r in rows], al_ref[...], dt_ref[...])
        cts = (cat(du_ref), cat(dw_ref), cat(dqd_ref), cat(dkd_ref), [t for r in rows for t in _qk_heads(dqk_ref, r)],
               [ddec_ref[i:i + 1, 0:1] for i in range(grp * HEADS)])
        dq, dk, dv, dpba, dal, ddt = vjp(cts)
        for g, r in enumerate(rows):
            part = slice(g * HEADS, (g + 1) * HEADS)
            _put_heads(dqkv_ref, dq[part], r, 0)
            _put_heads(dqkv_ref, dk[part], r, HEADS)
            _put_heads(dqkv_ref, dv[part], r, 2 * HEADS)
            dpba_ref[r, :] = dpba[g].astype(dpba_ref.dtype)
        dal_ref[...] += dal
        ddt_ref[...] += ddt

    blk = lambda c: pl.BlockSpec((rb, hd), lambda i: (i, c))
    par = pl.BlockSpec((1, LANES), lambda i: (0, 0))
    return _pcall(
        body, job, name=name, grid=(n // grp,),
        in_specs=[blk(0), blk(1), blk(2), pl.BlockSpec((rb, LANES), lambda i: (i, COL_BA)), par, par]
        + [blk(0)] * 5 + [pl.BlockSpec((grp * HEADS, LANES), lambda i: (i, 0)), blk(0)],
        out_specs=[pl.BlockSpec((rb, 3 * hd), lambda i: (i, 0)), pl.BlockSpec((rb, LANES), lambda i: (i, 0)), par, par],
        out_shape=[S((n_tok, 3 * hd), F32), S((n_tok, LANES), BF16), S((1, LANES), F32), S((1, LANES), F32)],
        semantics=("arbitrary",))(qkvn, qkvn, qkvn, p, alog, dtb, du, dw, dqd, dkd, dqk, ddec, tinv)


def loss_head(name, y, target, tm):
    n_tok, d = y.shape

    def body(y_ref, t_ref, dy_ref, l_ref):
        @pl.when(pl.program_id(0) == 0)
        def _():
            l_ref[...] = jnp.zeros_like(l_ref)

        e = y_ref[...] - t_ref[...]
        dy_ref[...] = e * (1.0 / d)
        l_ref[...] += jnp.sum(e * e, keepdims=True) * (0.5 / d)

    spec = pl.BlockSpec((tm, d), lambda i: (i, 0))
    return pl.pallas_call(
        body, name=name, grid=(n_tok // tm,), in_specs=[spec, spec],
        out_specs=[spec, pl.BlockSpec((1, 1), lambda i: (0, 0))], out_shape=[S((n_tok, d), F32), S((1, 1), F32)],
        compiler_params=_cp(("arbitrary",)))(y, target)


def _tm(n_tok):
    return min(512, n_tok)


def _gconv_tb(n_tok):
    return 1024 if n_tok % 1024 == 0 else _tm(n_tok)


def ffn_fwd(tag, x, w, jobs=None):
    n_tok = x.shape[0]
    tm = _tm(n_tok)
    g1 = (1, n_tok // tm)
    tD = _tok(D_MODEL)(tm)
    (h,) = ew_fwd(tag + "_rms", fn_rms, g1, [(x, tD), (w["norm_pre"], _par(D_MODEL))], [(S((n_tok, D_MODEL), BF16), tD)])
    u = mm(tag + "_in", h, w["w_in"], out_dtype=BF16, job=_take(jobs, "ffn_in"))
    tF = lambda c: _tok(D_FF, c)(tm)
    (a,) = ew_fwd(tag + "_swiglu", fn_swiglu, g1, [(u, tF(0)), (u, tF(1))], [(S((n_tok, D_FF), BF16), tF(0))])
    f = mm(tag + "_out", a, w["w_out"], job=_take(jobs, "ffn_out"))
    fn_res = lambda col, x_, f_, w_: (x_ + 0.5 * _rms(f_, w_),)
    (xo,) = ew_fwd(tag + "_res", fn_res, g1, [(x, tD), (f, tD), (w["norm_post"], _par(D_MODEL))],
                   [(S((n_tok, D_MODEL), F32), tD)])
    return xo, dict(x=x, h=h, u=u, a=a, f=f)


def ffn_bwd(tag, dxo, sv, w, jobs=None):
    n_tok = dxo.shape[0]
    tm = _tm(n_tok)
    g1 = (1, n_tok // tm)
    tD = _tok(D_MODEL)(tm)
    pD = _par(D_MODEL)
    fn_post = lambda col, f_, w_: (0.5 * _rms(f_, w_),)
    df, d_post = ew_bwd(tag + "_res_b", fn_post, g1, [(sv["f"], tD), (w["norm_post"], pD)], [(dxo, tD)],
                        [(0, S((n_tok, D_MODEL), BF16), tD)], [(1, S((1, D_MODEL), F32), pD, False)])
    da = mm(tag + "_out_bx", df, w["w_out"], tb=True, out_dtype=BF16, job=_take(jobs, "ffn_out_bx"))
    d_wout = mm(tag + "_out_bw", sv["a"], df, ta=True, out_dtype=GRAD_DT, job=_take(jobs, "ffn_out_bw"))
    tF = lambda c: _tok(D_FF, c)(tm)
    du = swiglu_bwd(tag + "_swiglu_b", sv["u"], da, tm)
    dh = mm(tag + "_in_bx", du, w["w_in"], tb=True, job=_take(jobs, "ffn_in_bx"))
    d_win = mm(tag + "_in_bw", sv["h"], du, ta=True, out_dtype=GRAD_DT, job=_take(jobs, "ffn_in_bw"), out_blocked=True)
    dx, d_pre = ew_bwd(tag + "_rms_b", fn_rms, g1, [(sv["x"], tD), (w["norm_pre"], pD)], [(dh, tD)],
                       [(0, S((n_tok, D_MODEL), F32), tD)], [(1, S((1, D_MODEL), F32), pD, False)], add=(dxo, tD))
    return dx, dict(norm_pre=d_pre, norm_post=d_post, w_in=d_win, w_out=d_wout)


def mix_fwd(tag, x, w, jobs=None):
    n_tok = x.shape[0]
    tm = _tm(n_tok)
    nt = n_tok // tm
    g1 = (1, nt)
    tD = _tok(D_MODEL)(tm)
    pD = _par(D_MODEL)
    (h,) = ew_fwd(tag + "_rms", fn_rms, g1, [(x, tD), (w["norm_pre"], pD)], [(S((n_tok, D_MODEL), BF16), tD)])
    p = mm(tag + "_in", h, w["w_all"], out_dtype=BF16, job=_take(jobs, "mix_in"))
    qkvn = gdnconv_fwd(tag + "_gconv", p, w["conv_w"], _gconv_tb(n_tok), job=_take(jobs, "gconv"))
    tC = _tokcol()(tm)
    *intra, tinv, dec = gdn_intra_fwd(tag + "_gintra", qkvn, p, w["alog"], w["dtb"], job=_take(jobs, "gintra"))
    intra.append(dec)
    o, states = gdn_seq_fwd(tag + "_gseq", *intra)
    (on,) = ew_fwd(tag + "_gout", fn_gdnout, (HEADS, nt),
                   [(o, tC), (p, _tokcol(COL_Z)(tm)), (w["gdn_norm_w"], _par(LANES))],
                   [(S((n_tok, D_MODEL), BF16), tC)])
    ya = mm(tag + "_go", on, w["gdn_w_o"], job=_take(jobs, "mix_small"))
    (hglu,) = ew_fwd(tag + "_glu", fn_glu, (D_MODEL // LANES, nt),
                     [(p, _tokcol(COL_GLU)(tm)), (p, _tokcol(COL_GLU + D_MODEL // LANES)(tm)),
                      (w["pw1_b"], _parcol(0)), (w["pw1_b"], _parcol(D_MODEL // LANES))],
                     [(S((n_tok, D_MODEL), F32), tC)])
    hc = conv_fwd(tag + "_cconv", hglu, 0, D_MODEL, w["dw_w"], w["dw_b"], tm)
    (hs,) = ew_fwd(tag + "_ln", fn_lnsilu, g1, [(hc, tD), (w["ln_g"], pD), (w["ln_b"], pD)],
                   [(S((n_tok, D_MODEL), BF16), tD)])
    yb = mm(tag + "_co", hs, w["cnv_w_o"], job=_take(jobs, "mix_small"))
    tG = lambda cb: pl.BlockSpec((tm, D_MODEL), lambda j, i: (i, cb))
    gcol = (W_QKV + W_Z + W_GLU) // D_MODEL
    (ym,) = ew_fwd(tag + "_merge", fn_merge, g1, [(ya, tD), (yb, tD), (p, tG(gcol)), (p, tG(gcol + 1)), (w["b_o"], pD)],
                   [(S((n_tok, D_MODEL), BF16), tD)])
    y = mm(tag + "_wo", ym, w["w_out"], job=_take(jobs, "mix_small"))
    fn_res = lambda col, x_, f_, w_: (x_ + _rms(f_, w_),)
    (xo,) = ew_fwd(tag + "_res", fn_res, g1, [(x, tD), (y, tD), (w["norm_post"], pD)], [(S((n_tok, D_MODEL), F32), tD)])
    sv = dict(x=x, h=h, p=p, qkvn=qkvn, intra=intra, tinv=tinv, states=states, o=o, on=on, ya=ya, hglu=hglu, hc=hc, hs=hs, yb=yb, ym=ym, y=y)
    return xo, sv


def mix_bwd(tag, dxo, sv, w, jobs=None):
    n_tok = dxo.shape[0]
    tm = _tm(n_tok)
    nt = n_tok // tm
    g1 = (1, nt)
    tD = _tok(D_MODEL)(tm)
    pD = _par(D_MODEL)
    tC = _tokcol()(tm)
    p = sv["p"]
    sD = lambda dt: S((n_tok, D_MODEL), dt)
    fn_post = lambda col, f_, w_: (_rms(f_, w_),)
    dy, d_post = ew_bwd(tag + "_res_b", fn_post, g1, [(sv["y"], tD), (w["norm_post"], pD)], [(dxo, tD)],
                        [(0, sD(BF16), tD)], [(1, S((1, D_MODEL), F32), pD, False)])
    dym = mm(tag + "_wo_bx", dy, w["w_out"], tb=True, job=_take(jobs, "mix_small"))
    d_wout = mm(tag + "_wo_bw", sv["ym"], dy, ta=True, out_dtype=GRAD_DT, job=_take(jobs, "mix_small"))
    tG = lambda cb: pl.BlockSpec((tm, D_MODEL), lambda j, i: (i, cb))
    gcol = (W_QKV + W_Z + W_GLU) // D_MODEL
    dya, dyb, dga, dgb, d_bo = ew_bwd(
        tag + "_merge_b", fn_merge, g1, [(sv["ya"], tD), (sv["yb"], tD), (p, tG(gcol)), (p, tG(gcol + 1)), (w["b_o"], pD)],
        [(dym, tD)], [(0, sD(BF16), tD), (1, sD(BF16), tD), (2, sD(BF16), tD), (3, sD(BF16), tD)],
        [(4, S((1, D_MODEL), F32), pD, False)])
    dhs = mm(tag + "_co_bx", dyb, w["cnv_w_o"], tb=True, job=_take(jobs, "mix_small"))
    d_cwo = mm(tag + "_co_bw", sv["hs"], dyb, ta=True, out_dtype=GRAD_DT, job=_take(jobs, "mix_small"))
    dhc, d_lng, d_lnb = ew_bwd(tag + "_ln_b", fn_lnsilu, g1, [(sv["hc"], tD), (w["ln_g"], pD), (w["ln_b"], pD)], [(dhs, tD)],
                               [(0, sD(F32), tD)], [(1, S((1, D_MODEL), F32), pD, False), (2, S((1, D_MODEL), F32), pD, False)])
    dhglu, d_dww, d_dwb = conv_bwd(tag + "_cconv_b", sv["hglu"], 0, D_MODEL, w["dw_w"], dhc, F32, tm)
    nc = D_MODEL // LANES
    dpa, dpg, d_ba, d_bg = ew_bwd(
        tag + "_glu_b", fn_glu, (nc, nt),
        [(p, _tokcol(COL_GLU)(tm)), (p, _tokcol(COL_GLU + nc)(tm)), (w["pw1_b"], _parcol(0)), (w["pw1_b"], _parcol(nc))],
        [(dhglu, tC)], [(0, sD(BF16), tC), (1, sD(BF16), tC)],
        [(2, S((1, D_MODEL), F32), _parcol(0), False), (3, S((1, D_MODEL), F32), _parcol(0), False)])
    don = mm(tag + "_go_bx", dya, w["gdn_w_o"], tb=True, job=_take(jobs, "mix_small"))
    d_gwo = mm(tag + "_go_bw", sv["on"], dya, ta=True, out_dtype=GRAD_DT, job=_take(jobs, "mix_small"))
    do, dz, d_gnw = ew_bwd(tag + "_gout_b", fn_gdnout, (HEADS, nt),
                           [(sv["o"], tC), (p, _tokcol(COL_Z)(tm)), (w["gdn_norm_w"], _par(LANES))], [(don, tC)],
                           [(0, sD(F32), tC), (1, sD(BF16), tC)], [(2, S((1, LANES), F32), _par(LANES), True)])
    d_intra = gdn_seq_bwd(tag + "_gseq_b", *sv["intra"], sv["states"], do)
    dqkvn, dpba, d_alog, d_dtb = gdn_intra_bwd(tag + "_gintra_b", sv["qkvn"], p, w["alog"], w["dtb"], *d_intra,
                                               sv["tinv"], job=_take(jobs, "gintra_b"))
    dqkv, d_convw = gdnconv_bwd(tag + "_gconv_b", p, w["conv_w"], dqkvn, _gconv_tb(n_tok), job=_take(jobs, "gconv_b"))
    nd = D_MODEL // LANES
    pieces = [(dqkv, 0, 0), (dz, W_QKV // D_MODEL, COL_Z), (dpa, COL_GLU // nd, COL_GLU), (dpg, COL_GLU // nd + 1, COL_GLU + nd),
              (dga, COL_GATE // nd, COL_GATE), (dgb, COL_GATE // nd + 1, COL_GATE + nd), (dpba, COL_BA, COL_BA)]
    dh = mm_nt_sum(tag + "_in_bx", [(a, blk) for a, blk, _ in pieces], w["w_all"])
    d_wall = [mm(tag + "_in_bw", sv["h"], a, ta=True, out_dtype=GRAD_DT, job=_take(jobs, "mix_small"))
              for a, _, _ in pieces]
    dx, d_pre = ew_bwd(tag + "_rms_b", fn_rms, g1, [(sv["x"], tD), (w["norm_pre"], pD)], [(dh, tD)],
                       [(0, sD(F32), tD)], [(1, S((1, D_MODEL), F32), pD, False)], add=(dxo, tD))
    grads = dict(norm_pre=d_pre, norm_post=d_post, w_all=d_wall, conv_w=d_convw, alog=d_alog, dtb=d_dtb,
                 gdn_norm_w=d_gnw, gdn_w_o=d_gwo, pw1_b=jnp.concatenate([d_ba, d_bg], axis=1), dw_w=d_dww,
                 dw_b=d_dwb, ln_g=d_lng, ln_b=d_lnb, cnv_w_o=d_cwo, b_o=d_bo, w_out=d_wout)
    return dx, grads


def local_step(x, target, layers):
    saved = []
    for lw in layers:
        x, sv = layer_fwd(x, lw)
        saved.append(sv)
    dx, loss = loss_head("loss", x, target, _tm(x.shape[0]))
    grads = [None] * len(layers)
    for i in reversed(range(len(layers))):
        dx, grads[i] = layer_bwd(dx, saved[i], layers[i])
    return loss, dx, grads


def layer_fwd(x, lw, jobs=None):
    x, s1 = ffn_fwd("ffn", x, lw["ffn1"], jobs)
    x, s2 = mix_fwd("mix", x, lw["mix"], jobs)
    x, s3 = ffn_fwd("ffn", x, lw["ffn2"], jobs)
    return x, (s1, s2, s3)


def layer_bwd(dx, saved, lw, jobs=None):
    s1, s2, s3 = saved
    dx, g3 = ffn_bwd("ffn", dx, s3, lw["ffn2"], jobs)
    dx, g2 = mix_bwd("mix", dx, s2, lw["mix"], jobs)
    dx, g1 = ffn_bwd("ffn", dx, s1, lw["ffn1"], jobs)
    return dx, dict(ffn1=g1, mix=g2, ffn2=g3)


_O_BA = W_QKV + W_Z
_O_GLU = _O_BA + 2 * HEADS


_MIX_BLK = P_IN // N_BLK
_MIX_B1 = _O_BA - _MIX_BLK
assert _O_BA + HEADS == 2 * _MIX_BLK
BLOCKED = ("ffn1_w_in", "ffn2_w_in", "mix_w_in")


def _row(v):
    return v.reshape(1, -1).astype(F32)


def prep_ffn(wl, k):
    w_in = wl[k + "_w_in"].astype(BF16)
    if w_in.ndim == 2:
        w_in = jnp.transpose(w_in.reshape(w_in.shape[0], N_BLK, -1), (1, 0, 2))
    return dict(norm_pre=_row(wl[k + "_norm_pre"]), norm_post=_row(wl[k + "_norm_post"]), w_in=w_in,
                w_out=wl[k + "_w_out"].astype(BF16))


def prep_layer(wl):
    return dict(ffn1=prep_ffn(wl, "ffn1"), mix=prep_mix(wl), ffn2=prep_ffn(wl, "ffn2"))


def prep_mix(wl):
    row = _row
    bf = lambda v: v.astype(BF16)
    lanes8 = lambda v: jnp.zeros((1, LANES), F32).at[0, HEADS:2 * HEADS].set(v.astype(F32))
    mw = bf(wl["mix_w_in"])
    pad = jnp.zeros((D_MODEL, LANES - 2 * HEADS), BF16)
    if mw.ndim == 3:
        w_all = jnp.concatenate([mw[0], mw[1][:, :_MIX_B1], mw[2][:, HEADS:], mw[3], mw[1][:, _MIX_B1:],
                                 mw[2][:, :HEADS], pad], axis=1)
    else:
        w_all = jnp.concatenate([mw[:, :_O_BA], mw[:, _O_GLU:], mw[:, _O_BA:_O_GLU], pad], axis=1)
    return dict(norm_pre=row(wl["mix_norm_pre"]), norm_post=row(wl["mix_norm_post"]), w_all=w_all,
               conv_w=wl["gdn_conv_w"].astype(F32), alog=lanes8(wl["gdn_a_log"]), dtb=lanes8(wl["gdn_dt_bias"]),
               gdn_norm_w=row(wl["gdn_norm_w"]), gdn_w_o=bf(wl["gdn_w_o"]), pw1_b=row(wl["cnv_pw1_b"]),
               dw_w=wl["cnv_dw_w"].astype(F32), dw_b=row(wl["cnv_dw_b"]), ln_g=row(wl["cnv_ln_g"]),
               ln_b=row(wl["cnv_ln_b"]), cnv_w_o=bf(wl["cnv_w_o"]), b_o=row(wl["cnv_b_o"]), w_out=bf(wl["mix_w_out"]))


def unprep_grads(g):
    return {**unprep_ffn(g["ffn1"], "ffn1"), **unprep_mix(g["mix"]), **unprep_ffn(g["ffn2"], "ffn2")}


def unprep_ffn(g, k):
    blk = g["w_in"]
    return {k + "_norm_pre": g["norm_pre"][0], k + "_norm_post": g["norm_post"][0], k + "_w_in#blocks": blk,
            k + "_w_in": jnp.transpose(blk, (1, 0, 2)).reshape(blk.shape[1], N_BLK * blk.shape[2]),
            k + "_w_out": g["w_out"]}


def unprep_mix(m):
    dqkv, dz, dpa, dpg, dga, dgb, dba = m["w_all"]
    out = {}
    out["mix_w_in#blocks"] = jnp.stack([
        dqkv[:, :_MIX_BLK], jnp.concatenate([dqkv[:, _MIX_BLK:], dz, dba[:, :HEADS]], axis=1),
        jnp.concatenate([dba[:, HEADS:2 * HEADS], dpa, dpg[:, :_MIX_B1 - D_MODEL]], axis=1),
        jnp.concatenate([dpg[:, _MIX_B1 - D_MODEL:], dga, dgb], axis=1)])
    out.update(
        mix_norm_pre=m["norm_pre"][0], mix_norm_post=m["norm_post"][0],
        mix_w_in=jnp.concatenate([dqkv, dz, dba[:, :2 * HEADS], dpa, dpg, dga, dgb], axis=1),
        gdn_conv_w=m["conv_w"], gdn_a_log=m["alog"][0, HEADS:2 * HEADS], gdn_dt_bias=m["dtb"][0, HEADS:2 * HEADS],
        gdn_norm_w=m["gdn_norm_w"][0], gdn_w_o=m["gdn_w_o"], cnv_pw1_b=m["pw1_b"][0], cnv_dw_w=m["dw_w"],
        cnv_dw_b=m["dw_b"][0], cnv_ln_g=m["ln_g"][0], cnv_ln_b=m["ln_b"][0], cnv_w_o=m["cnv_w_o"], cnv_b_o=m["b_o"][0],
        mix_w_out=m["w_out"])
    return out


MESH = pl.DeviceIdType.MESH
ANY = pl.BlockSpec(memory_space=pl.ANY)
N_DEV = 8


def _pos():
    return lax.axis_index("x"), lax.axis_index("y"), lax.axis_index("c")


def _other_chips(x, y):
    return [(1 - x, y), (x, 1 - y), (1 - x, 1 - y)]


class Job:
    def __init__(self, ins, outs, n_sems, copies, aliases=None):
        self.ins, self.outs, self.n_sems, self.copies = ins, outs, n_sems, copies
        self.aliases = aliases or {}
        self.results = None
        self.host = None

    def scratch(self):
        return [pltpu.SemaphoreType.DMA((self.n_sems,)), pltpu.SemaphoreType.DMA((self.n_sems,))]

    def start(self, in_refs, out_refs, sems):
        for cp in self.copies(in_refs, out_refs, sems, False):
            cp.start()

    def finish(self, in_refs, out_refs, sems):
        for cp in self.copies(in_refs, out_refs, sems, True):
            cp.wait_recv()
        for cp in self.copies(in_refs, out_refs, sems, False):
            cp.wait_send()


def run_job(name, job):
    n_i, n_o = len(job.ins), len(job.outs)

    def body(*refs):
        in_refs, out_refs, sems = refs[:n_i], refs[n_i:n_i + n_o], refs[n_i + n_o:]
        job.start(in_refs, out_refs, sems)
        job.finish(in_refs, out_refs, sems)

    job.results = pl.pallas_call(
        body, name=name, in_specs=[ANY] * n_i, out_specs=[ANY] * n_o, out_shape=job.outs,
        input_output_aliases=job.aliases, scratch_shapes=job.scratch())(*job.ins)
    return job.results


def _halved(rows):
    return rows % 32 == 0


def job_gather_ici(shards):
    n = len(shards)

    def copies(in_refs, out_refs, sems, recv):
        x, y, c = _pos()
        b = 2 * x + y
        chips = _other_chips(x, y)
        cps = []
        for a in range(n):
            hr = shards[a].shape[0] // 2
            for j in range(3):
                blk = 2 * chips[j][0] + chips[j][1] if recv else b
                if _halved(shards[a].shape[0]):
                    src, dst = in_refs[a].at[pl.ds(c * hr, hr)], out_refs[a].at[blk, pl.ds(c * hr, hr)]
                else:
                    src, dst = in_refs[a], out_refs[a].at[blk]
                cps.append(pltpu.make_async_remote_copy(
                    src_ref=src, dst_ref=dst, send_sem=sems[0].at[3 * a + j], recv_sem=sems[1].at[3 * a + j],
                    device_id=(chips[j][0], chips[j][1], c), device_id_type=MESH))
        return cps

    return Job(list(shards), [S((N_BLK,) + w.shape, w.dtype) for w in shards], 3 * n, copies)


def job_gather_sibling(lands):
    idx = [a for a, w in enumerate(lands) if _halved(w.shape[1])]

    def copies(in_refs, out_refs, sems, recv):
        x, y, c = _pos()
        chips = _other_chips(x, y)
        half = 1 - c if recv else c
        cps = []
        for pos, a in enumerate(idx):
            hr = lands[a].shape[1] // 2
            for j in range(3):
                rows = out_refs[a].at[2 * chips[j][0] + chips[j][1], pl.ds(half * hr, hr)]
                cps.append(pltpu.make_async_remote_copy(
                    src_ref=rows, dst_ref=rows, send_sem=sems[0].at[3 * pos + j], recv_sem=sems[1].at[3 * pos + j],
                    device_id=(x, y, 1 - c), device_id_type=MESH))
        return cps

    return Job(list(lands), [S(w.shape, w.dtype) for w in lands], 3 * len(idx), copies,
               aliases={a: a for a in range(len(lands))})


def job_rs_chips(ps):
    n = len(ps)

    def copies(in_refs, out_refs, sems, recv):
        x, y, c = _pos()
        b = 2 * x + y
        chips = _other_chips(x, y)
        cps = []
        for k in range(n):
            for j in range(3):
                other = 2 * chips[j][0] + chips[j][1]
                src_blk, dst_slot = (b, other) if recv else (other, b)
                cps.append(pltpu.make_async_remote_copy(
                    src_ref=in_refs[k].at[src_blk], dst_ref=out_refs[k].at[dst_slot], send_sem=sems[0].at[3 * k + j],
                    recv_sem=sems[1].at[3 * k + j], device_id=(chips[j][0], chips[j][1], c), device_id_type=MESH))
        return cps

    return Job(list(ps), [S(p.shape, p.dtype) for p in ps], 3 * n, copies)


def rs_sibling(gs):
    n = len(gs)

    def body(*refs):
        g_refs, r_refs = refs[:n], refs[n:2 * n]
        send_sems, recv_sems = refs[2 * n:]
        x, y, c = _pos()

        def cp(k):
            hr = gs[k].shape[1] // 2
            return pltpu.make_async_remote_copy(
                src_ref=g_refs[k].at[:, pl.ds((1 - c) * hr, hr)], dst_ref=r_refs[k], send_sem=send_sems.at[k],
                recv_sem=recv_sems.at[k], device_id=(x, y, 1 - c), device_id_type=MESH)

        cps = [cp(k) for k in range(n)]
        for d in cps:
            d.start()
        for d in cps:
            d.wait_recv()
        for d in cps:
            d.wait_send()

    return pl.pallas_call(
        body, name="rs_sibling", in_specs=[ANY] * n, out_specs=[ANY] * n,
        out_shape=[S((N_BLK, g.shape[1] // 2, g.shape[2]), g.dtype) for g in gs],
        scratch_shapes=[pltpu.SemaphoreType.DMA((n,)), pltpu.SemaphoreType.DMA((n,))])(*gs)


def ag_sibling(fs):
    n = len(fs)

    def body(*refs):
        o_refs = refs[n:2 * n]
        send_sems, recv_sems = refs[2 * n:]
        x, y, c = _pos()

        def cp(k, half):
            hr = fs[k].shape[0] // 2
            rows = o_refs[k].at[pl.ds(half * hr, hr)]
            return pltpu.make_async_remote_copy(
                src_ref=rows, dst_ref=rows, send_sem=send_sems.at[k], recv_sem=recv_sems.at[k],
                device_id=(x, y, 1 - c), device_id_type=MESH)

        cps = [cp(k, c) for k in range(n)]
        for d in cps:
            d.start()
        for k in range(n):
            cp(k, 1 - c).wait_recv()
        for d in cps:
            d.wait_send()

    return pl.pallas_call(
        body, name="ag_sibling", in_specs=[ANY] * n, out_specs=[ANY] * n,
        out_shape=[S(f.shape, f.dtype) for f in fs], input_output_aliases={k: k for k in range(n)},
        scratch_shapes=[pltpu.SemaphoreType.DMA((n,)), pltpu.SemaphoreType.DMA((n,))])(*fs)


def allreduce_small(v):
    rows = v.shape[0]

    def body(v_ref, o_ref, buf, send_sems, recv_sems):
        x, y, c = _pos()
        me = 4 * x + 2 * y + c
        buf[me] = v_ref[...]

        def cp(d, slot):
            dx, dy, dc = (d >> 2) & 1, (d >> 1) & 1, d & 1
            peer = (1 - x if dx else x, 1 - y if dy else y, 1 - c if dc else c)
            return pltpu.make_async_remote_copy(
                src_ref=v_ref, dst_ref=buf.at[slot], send_sem=send_sems.at[d - 1], recv_sem=recv_sems.at[d - 1],
                device_id=peer, device_id_type=MESH)

        cps = [cp(d, me) for d in range(1, N_DEV)]
        for d in cps:
            d.start()
        for d in range(1, N_DEV):
            dx, dy, dc = (d >> 2) & 1, (d >> 1) & 1, d & 1
            src = 4 * (1 - x if dx else x) + 2 * (1 - y if dy else y) + (1 - c if dc else c)
            cp(d, src).wait_recv()
        for d in cps:
            d.wait_send()
        acc = buf[0]
        for s in range(1, N_DEV):
            acc = acc + buf[s]
        o_ref[...] = acc

    vm = pl.BlockSpec(memory_space=pltpu.VMEM)
    return pl.pallas_call(
        body, name="allreduce_small", in_specs=[vm], out_specs=vm, out_shape=S(v.shape, v.dtype),
        scratch_shapes=[pltpu.VMEM((N_DEV, rows, LANES), F32), pltpu.SemaphoreType.DMA((N_DEV - 1,)),
                        pltpu.SemaphoreType.DMA((N_DEV - 1,))])(v)


def _rows_tile(rows, cols, cap_bytes=1 << 20, mult=8):
    best = None
    for t in range(mult, rows + 1, mult):
        if rows % t == 0 and t * cols * 4 <= cap_bytes:
            best = t
    return best if best is not None else rows


def add_half(name, g, r, c_arr):
    _, hr, cols = r.shape
    tr = _rows_tile(hr, cols, mult=16)
    nb = hr // tr

    def body(c_ref, g_ref, r_ref, o_ref):
        o_ref[...] = (g_ref[...].astype(F32) + r_ref[...].astype(F32)).astype(o_ref.dtype)

    gs = pltpu.PrefetchScalarGridSpec(
        num_scalar_prefetch=1, grid=(N_BLK, nb),
        in_specs=[pl.BlockSpec((None, tr, cols), lambda b, i, cr: (b, cr[0] * nb + i, 0)),
                  pl.BlockSpec((None, tr, cols), lambda b, i, cr: (b, i, 0))],
        out_specs=pl.BlockSpec((None, tr, cols), lambda b, i, cr: (b, i, 0)))
    return pl.pallas_call(body, name=name, grid_spec=gs, out_shape=S(r.shape, BF16),
                          compiler_params=_cp(("parallel", "parallel")))(c_arr, g, r)


def sum_chips(name, r, own, cb_arr):
    _, hr, cols = r.shape
    tr = _rows_tile(hr, cols, mult=16)
    nb = hr // tr

    def body(cb_ref, *refs):
        o_ref = refs[N_BLK + 1]
        b = cb_ref[1]
        acc = None
        for s in range(N_BLK):
            term = jnp.where(b == s, refs[N_BLK][...], refs[s][...]).astype(F32)
            acc = term if acc is None else acc + term
        o_ref[...] = acc

    slot = lambda s: pl.BlockSpec((None, tr, cols), lambda i, cb: (jnp.where(cb[1] == s, (s + 1) % N_BLK, s), i, 0))
    gs = pltpu.PrefetchScalarGridSpec(
        num_scalar_prefetch=1, grid=(nb,),
        in_specs=[slot(s) for s in range(N_BLK)] + [pl.BlockSpec((None, tr, cols), lambda i, cb: (cb[1], i, 0))],
        out_specs=pl.BlockSpec((tr, cols), lambda i, cb: (cb[0] * nb + i, 0)))
    return pl.pallas_call(body, name=name, grid_spec=gs, out_shape=S((2 * hr, cols), F32),
                          compiler_params=_cp(("parallel",)))(cb_arr, *([r] * N_BLK), own)


def adamw(name, w, m, v, gs):
    rows, cols = w.shape
    two = len(gs) == 2
    span = rows // 2 if two else rows
    tr = _rows_tile(span, cols, 1 << 20)
    nb = span // tr

    def body(w_ref, m_ref, v_ref, *rest):
        g_refs, (go_ref, d_ref, mo_ref, vo_ref) = rest[:len(gs)], rest[len(gs):]
        if two:
            g = jnp.where(pl.program_id(0) < nb, g_refs[0][...], g_refs[1][...])
        else:
            g = g_refs[0][...]
        mn = ADAM_B1 * m_ref[...] + (1.0 - ADAM_B1) * g
        vn = ADAM_B2 * v_ref[...] + (1.0 - ADAM_B2) * jnp.square(g)
        m_hat = mn / (1.0 - ADAM_B1 ** ADAM_STEP)
        v_hat = vn / (1.0 - ADAM_B2 ** ADAM_STEP)
        go_ref[...] = g
        d_ref[...] = -ADAM_LR * (m_hat / (jnp.sqrt(v_hat) + ADAM_EPS) + ADAM_WD * w_ref[...])
        mo_ref[...] = mn
        vo_ref[...] = vn

    full = pl.BlockSpec((tr, cols), lambda i: (i, 0))
    if two:
        g_specs = [pl.BlockSpec((tr, cols), lambda i: (jnp.minimum(i, nb - 1), 0)),
                   pl.BlockSpec((tr, cols), lambda i: (jnp.maximum(i - nb, 0), 0))]
    else:
        g_specs = [full]
    return pl.pallas_call(
        body, name=name, grid=(2 * nb if two else nb,), in_specs=[full, full, full] + g_specs, out_specs=[full] * 4,
        out_shape=[S((rows, cols), F32)] * 4, compiler_params=_cp(("parallel",)))(w, m, v, *gs)


WEIGHTS = ["ffn1_norm_pre", "ffn1_norm_post", "ffn1_w_in", "ffn1_w_out", "mix_norm_pre", "mix_norm_post", "mix_w_in",
           "gdn_conv_w", "gdn_a_log", "gdn_dt_bias", "gdn_norm_w", "gdn_w_o", "cnv_pw1_b", "cnv_dw_w", "cnv_dw_b",
           "cnv_ln_g", "cnv_ln_b", "cnv_w_o", "cnv_b_o", "mix_w_out", "ffn2_norm_pre", "ffn2_norm_post", "ffn2_w_in",
           "ffn2_w_out"]
BIG = {"ffn1_w_in": True, "ffn1_w_out": False, "mix_w_in": True, "gdn_conv_w": True, "gdn_w_o": False,
       "cnv_dw_w": True, "cnv_w_o": False, "mix_w_out": False, "ffn2_w_in": True, "ffn2_w_out": False}
TINY = {"gdn_conv_w": (32, LANES), "cnv_dw_w": (64, LANES)}
SMALL = [n for n in WEIGHTS if n not in BIG]
SUB = {"ffn1": ["ffn1_w_in", "ffn1_w_out"], "ffn2": ["ffn2_w_in", "ffn2_w_out"],
       "mix": ["mix_w_in", "gdn_conv_w", "gdn_w_o", "cnv_dw_w", "cnv_w_o", "mix_w_out"]}
GATHER_ON_FFN_MIX = [("ffn_in", ["mix_w_in"]), ("ffn_out", SUB["mix"][1:])]
RS_ON_FFN_MIX = [("ffn_in_bx", ["mix_w_in"]), ("ffn_out_bx", SUB["mix"][1:])]


def _whole(name, blocks):
    if BIG[name]:
        return jnp.transpose(blocks, (1, 0, 2)).reshape(blocks.shape[1], N_BLK * blocks.shape[2])
    return blocks.reshape(N_BLK * blocks.shape[1], blocks.shape[2])


def _blocks(name, whole):
    r, cfull = whole.shape
    if BIG[name]:
        blk = jnp.transpose(whole.reshape(r, N_BLK, cfull // N_BLK), (1, 0, 2))
    else:
        blk = whole.reshape(N_BLK, r // N_BLK, cfull)
    if name in TINY:
        tr, tc = TINY[name]
        flat = blk.reshape(N_BLK, -1)
        blk = jnp.pad(flat, ((0, 0), (0, tr * tc - flat.shape[1]))).reshape(N_BLK, tr, tc)
    return blk.astype(GRAD_DT)


def _pack(parts):
    rows = []
    for p in parts:
        flat = p.reshape(-1).astype(F32)
        rows.append(jnp.pad(flat, (0, (-flat.shape[0]) % LANES)).reshape(-1, LANES))
    out = jnp.concatenate(rows, axis=0)
    return jnp.pad(out, ((0, (-out.shape[0]) % 8), (0, 0)))


def _unpack(packed, shapes):
    out, r = [], 0
    for shp in shapes:
        size = math.prod(shp)
        nr = -(-size // LANES)
        out.append(packed[r:r + nr].reshape(-1)[:size].reshape(shp))
        r += nr
    return out


def kernel(x, ffn1_norm_pre, ffn1_norm_post, ffn1_w_in, ffn1_w_out, mix_norm_pre, mix_norm_post, mix_w_in, gdn_conv_w, gdn_a_log, gdn_dt_bias, gdn_norm_w, gdn_w_o, cnv_pw1_b, cnv_dw_w, cnv_dw_b, cnv_ln_g, cnv_ln_b, cnv_w_o, cnv_b_o, mix_w_out, ffn2_norm_pre, ffn2_norm_post, ffn2_w_in, ffn2_w_out, loss_target, m_ffn1_norm_pre, m_ffn1_norm_post, m_ffn1_w_in, m_ffn1_w_out, m_mix_norm_pre, m_mix_norm_post, m_mix_w_in, m_gdn_conv_w, m_gdn_a_log, m_gdn_dt_bias, m_gdn_norm_w, m_gdn_w_o, m_cnv_pw1_b, m_cnv_dw_w, m_cnv_dw_b, m_cnv_ln_g, m_cnv_ln_b, m_cnv_w_o, m_cnv_b_o, m_mix_w_out, m_ffn2_norm_pre, m_ffn2_norm_post, m_ffn2_w_in, m_ffn2_w_out, v_ffn1_norm_pre, v_ffn1_norm_post, v_ffn1_w_in, v_ffn1_w_out, v_mix_norm_pre, v_mix_norm_post, v_mix_w_in, v_gdn_conv_w, v_gdn_a_log, v_gdn_dt_bias, v_gdn_norm_w, v_gdn_w_o, v_cnv_pw1_b, v_cnv_dw_w, v_cnv_dw_b, v_cnv_ln_g, v_cnv_ln_b, v_cnv_w_o, v_cnv_b_o, v_mix_w_out, v_ffn2_norm_pre, v_ffn2_norm_post, v_ffn2_w_in, v_ffn2_w_out):
    args = locals()
    wts = {n: args[n] for n in WEIGHTS}
    mom = {n: args["m_" + n] for n in WEIGHTS}
    var = {n: args["v_" + n] for n in WEIGHTS}
    big = list(BIG)

    mx, my, mc = _pos()
    mb = 2 * mx + my
    cb_arr = jnp.stack([mc, mb]).astype(jnp.int32)

    own = {(n, l): wts[n][l].astype(BF16) for n in big for l in range(DEPTH)}

    def planned(make_job, arrays, plan):
        jobs = []
        for host, names in plan:
            jb = make_job([arrays[n] for n in names])
            jb.host, jb.names = host, names
            jobs.append(jb)
        return jobs

    def landed(jobs):
        res = {}
        for jb in jobs:
            if jb.results is None:
                run_job("comm_alone", jb)
            res.update(zip(jb.names, jb.results))
        return res

    stages = [(l, s) for l in range(DEPTH) for s in ("ffn1", "mix", "ffn2")]
    carried_by_ffn = lambda s: GATHER_ON_FFN_MIX if s == "mix" else [("ffn_in", [s + "_w_in"]), ("ffn_out", [s + "_w_out"])]
    carried_by_mix = lambda s: [("mix_in", [s + "_w_in", s + "_w_out"])]

    def gather_jobs(l, s, carrier):
        plan = carried_by_mix(s) if carrier == "mix" else carried_by_ffn(s)
        return planned(job_gather_ici, {n: own[(n, l)] for n in SUB[s]}, plan)

    def sub_weights(l, s, jobs):
        lands = landed(jobs)
        names = SUB[s]
        lands = dict(zip(names, run_job("gather_sib", job_gather_sibling([lands[n] for n in names]))))
        wl = {}
        for n in names:
            blocks = lax.dynamic_update_index_in_dim(lands[n], own[(n, l)], mb, 0)
            wl[n] = blocks if n in BLOCKED else _whole(n, blocks)
        wl.update({n: wts[n][l] for n in SMALL})
        return prep_mix(wl) if s == "mix" else prep_ffn(wl, s)

    act = x[0]
    saved, weights = {}, {}
    jobs = gather_jobs(0, "ffn1", "ffn")
    for i, (l, s) in enumerate(stages):
        weights[(l, s)] = sub_weights(l, s, jobs)
        jobs = gather_jobs(*stages[i + 1], "mix" if s == "mix" else "ffn") if i + 1 < len(stages) else []
        pending = list(jobs)
        fwd = mix_fwd if s == "mix" else ffn_fwd
        act, saved[(l, s)] = fwd("mix" if s == "mix" else "ffn", act, weights[(l, s)], pending)
    dx, loss = loss_head("loss", act, loss_target[0], _tm(act.shape[0]))

    def rs_jobs(l, s, g, carrier):
        if s == "mix":
            gw_s = unprep_mix(g)
        else:
            gw_s = unprep_ffn(g, s)
        small_grads[l].update({n: gw_s[n] for n in gw_s if n in SMALL})
        names = SUB[s]
        blocks = [gw_s[n + "#blocks"] if n in BLOCKED else _blocks(n, gw_s[n]) for n in names]
        parts = [add_half("add_half", b_, r, cb_arr) for b_, r in zip(blocks, rs_sibling(blocks))]
        partial_of.update({(n, l): p for n, p in zip(names, parts)})
        if carrier is None:
            plan = [(None, names)]
        elif carrier == "mix":
            plan = [("gintra_b", names)]
        elif s == "mix":
            plan = RS_ON_FFN_MIX
        else:
            plan = [("ffn_in_bx", [s + "_w_in"]), ("ffn_out_bx", [s + "_w_out"])]
        return planned(job_rs_chips, dict(zip(names, parts)), plan)

    small_grads = [{} for _ in range(DEPTH)]
    partial_of, chip_of = {}, {}
    jobs, jobs_key = [], None
    for i, (l, s) in enumerate(reversed(stages)):
        pending = list(jobs)
        bwd = mix_bwd if s == "mix" else ffn_bwd
        dx, g = bwd("mix" if s == "mix" else "ffn", dx, saved[(l, s)], weights[(l, s)], pending)
        if jobs:
            chip_of.update({(n, jobs_key): r for n, r in landed(jobs).items()})
        nxt = list(reversed(stages))[i + 1][1] if i + 1 < len(stages) else None
        jobs, jobs_key = rs_jobs(l, s, g, None if nxt is None else ("mix" if nxt == "mix" else "ffn")), l
    chip_of.update({(n, jobs_key): r for n, r in landed(jobs).items()})
    gw = small_grads

    small_shapes = [wts[n].shape for n in SMALL]
    packed = _pack([jnp.stack([gw[l][n] for l in range(DEPTH)]) for n in SMALL] + [loss])
    total = allreduce_small(packed)
    small_g = dict(zip(SMALL, _unpack(total, small_shapes)))
    loss_sum = total[sum(-(-math.prod(s) // LANES) for s in small_shapes), 0]

    keys = [(n, l) for l in range(DEPTH) for n in big]
    halves = [sum_chips("sum_chips", chip_of[k], partial_of[k], cb_arr) for k in keys]
    summed = dict(zip(keys, ag_sibling(halves)))

    out_g, out_d, out_m, out_v = {}, {}, {}, {}
    for n in big:
        shp = wts[n].shape
        gs = [summed[(n, l)] for l in range(DEPTH)]
        if n in TINY:
            gs = [jnp.concatenate([g.reshape(-1)[:shp[1] * shp[2]].reshape(shp[1], shp[2]) for g in gs], axis=0)]
        two_d = lambda a: a.reshape(DEPTH * shp[1], shp[2])
        res = adamw("adamw", two_d(wts[n]), two_d(mom[n]), two_d(var[n]), gs)
        out_g[n], out_d[n], out_m[n], out_v[n] = [r.reshape(shp) for r in res]

    pk = lambda d: _pack([d[n] for n in SMALL])
    res = adamw("adamw_small", pk(wts), pk(mom), pk(var), [pk(small_g)])
    for d, r in zip((out_g, out_d, out_m, out_v), res):
        d.update(dict(zip(SMALL, _unpack(r, small_shapes))))

    return (loss_sum, dx[None], *[out_g[n] for n in WEIGHTS], *[out_d[n] for n in WEIGHTS],
            *[out_m[n] for n in WEIGHTS], *[out_v[n] for n in WEIGHTS])
```

```python
import functools
import math

import jax
import jax.numpy as jnp
from jax import lax
from jax.experimental import pallas as pl
from jax.experimental.pallas import tpu as pltpu

F32, BF16 = jnp.float32, jnp.bfloat16
S = jax.ShapeDtypeStruct

D_MODEL = 1024
D_FF = 2816
HEADS = 8
DK = 128
CHUNK = 64
GDN_CONV = 4
CNV_K = 31
W_QKV = 3 * HEADS * DK
W_Z = HEADS * DK
W_GLU = 2 * D_MODEL
W_GATE = 2 * D_MODEL
P_IN = W_QKV + W_Z + 2 * HEADS + W_GLU + W_GATE
LANES = 128
P_ALL = W_QKV + W_Z + W_GLU + W_GATE + LANES
COL_Z = W_QKV // LANES
COL_GLU = (W_QKV + W_Z) // LANES
COL_GATE = (W_QKV + W_Z + W_GLU) // LANES
COL_BA = (W_QKV + W_Z + W_GLU + W_GATE) // LANES
RMS_EPS = 1e-6
LN_EPS = 1e-5
DEPTH = 2
N_BLK = 4
VMEM_LIMIT = 56 * 1024 * 1024
GRAD_DT = BF16

ADAM_LR, ADAM_B1, ADAM_B2, ADAM_EPS, ADAM_WD, ADAM_STEP = 0.001, 0.9, 0.999, 1e-08, 0.01, 10


def _cp(sem):
    return pltpu.CompilerParams(dimension_semantics=sem, vmem_limit_bytes=VMEM_LIMIT)


MM_VMEM_BUDGET = 36 * 1024 * 1024


def _mm_tiles(m, n, k_bytes_a, k_bytes_b, out_bytes, tn_fixed=None):
    best = None
    for tm in (1024, 512, 256, 128):
        if m % tm:
            continue
        for tn in ((tn_fixed,) if tn_fixed else (1024, 512, 640, 256, 384, 128)):
            if n % tn:
                continue
            need = 2 * (tm * k_bytes_a + tn * k_bytes_b + tm * tn * out_bytes)
            if need <= MM_VMEM_BUDGET and (best is None or tm * tn > best[0] * best[1]):
                best = (tm, tn)
    if best is None:
        raise ValueError((m, n, k_bytes_a, k_bytes_b))
    return best


def mm_nt_sum(name, parts, b):
    m, n = parts[0][0].shape[0], b.shape[0]
    k_total = sum(a.shape[1] for a, _ in parts)
    tm, tn = _mm_tiles(m, n, k_total * 2, k_total * 2, 4)
    n_p = len(parts)

    def body(*refs):
        o_ref = refs[2 * n_p]
        acc = None
        for a_ref, b_ref in zip(refs[:n_p], refs[n_p:2 * n_p]):
            t = lax.dot_general(a_ref[...], b_ref[...], (((1,), (1,)), ((), ())), preferred_element_type=F32)
            acc = t if acc is None else acc + t
        o_ref[...] = acc

    a_specs = [pl.BlockSpec((tm, a.shape[1]), lambda i, j: (i, 0)) for a, _ in parts]
    b_specs = [pl.BlockSpec((tn, a.shape[1]), functools.partial(lambda i, j, c: (j, c), c=col)) for a, col in parts]
    return pl.pallas_call(
        body, name=name, grid=(m // tm, n // tn), in_specs=a_specs + b_specs,
        out_specs=pl.BlockSpec((tm, tn), lambda i, j: (i, j)), out_shape=S((m, n), F32),
        compiler_params=_cp(("parallel", "parallel")))(*[a for a, _ in parts], *([b] * n_p))


def _take(jobs, host):
    for jb in jobs or []:
        if jb.host == host:
            jobs.remove(jb)
            return jb
    return None


def _pcall(body, job, *, name, grid, in_specs, out_specs, out_shape, scratch_shapes=(), semantics):
    in_specs, out_specs, out_shape, scratch_shapes = list(in_specs), list(out_specs), list(out_shape), list(scratch_shapes)
    if job is None:
        return lambda *args: pl.pallas_call(
            body, name=name, grid=grid, in_specs=in_specs, out_specs=out_specs, out_shape=out_shape,
            scratch_shapes=scratch_shapes, compiler_params=_cp(semantics))(*args)
    n_in, n_out, n_sc = len(in_specs), len(out_specs), len(scratch_shapes)
    n_i, n_o = len(job.ins), len(job.outs)

    def wrapped(*refs):
        cut = [n_in, n_i, n_out, n_o, n_sc]
        parts, pos = [], 0
        for c in cut:
            parts.append(refs[pos:pos + c])
            pos += c
        ins, j_in, outs, j_out, own = parts
        sems = refs[pos:]
        ids = [pl.program_id(d) for d in range(len(grid))]
        first = functools.reduce(lambda p, q: p & q, [i == 0 for i in ids])
        last = functools.reduce(lambda p, q: p & q, [i == g - 1 for i, g in zip(ids, grid)])

        @pl.when(first)
        def _():
            job.start(j_in, j_out, sems)

        body(*ins, *outs, *own)

        @pl.when(last)
        def _():
            job.finish(j_in, j_out, sems)

    def call(*args):
        res = pl.pallas_call(
            wrapped, name=name + "_c", grid=grid, in_specs=in_specs + [ANY] * n_i, out_specs=out_specs + [ANY] * n_o,
            out_shape=out_shape + list(job.outs),
            input_output_aliases={n_in + ki: n_out + ko for ki, ko in job.aliases.items()},
            scratch_shapes=scratch_shapes + job.scratch(),
            compiler_params=_cp(("arbitrary",) * len(grid)))(*args, *job.ins)
        job.results = res[n_out:]
        return res[:n_out]

    return call


def mm(name, a, b, ta=False, tb=False, out_dtype=F32, job=None, out_blocked=False):
    k = a.shape[0] if ta else a.shape[1]
    m = a.shape[1] if ta else a.shape[0]
    blocked = b.ndim == 3
    cb = b.shape[2] if blocked else None
    osz = jnp.dtype(out_dtype).itemsize
    if blocked and not tb:
        n = N_BLK * cb
        tm, tn = _mm_tiles(m, n, k * 2, k * 2, osz, tn_fixed=cb)
        b_spec = pl.BlockSpec((None, k, cb), lambda i, j: (j, 0, 0))
    elif blocked:
        n = b.shape[1]
        assert k == N_BLK * cb, (name, a.shape, b.shape)
        tm, tn = _mm_tiles(m, n, k * 2, k * 2, osz)
        b_spec = pl.BlockSpec((N_BLK, tn, cb), lambda i, j: (0, j, 0))
    else:
        n = b.shape[0] if tb else b.shape[1]
        assert k == (b.shape[1] if tb else b.shape[0]), (name, a.shape, b.shape)
        tm, tn = _mm_tiles(m, n, k * 2, k * 2, osz, tn_fixed=n // N_BLK if out_blocked else None)
        b_spec = pl.BlockSpec((tn, k), lambda i, j: (j, 0)) if tb else pl.BlockSpec((k, tn), lambda i, j: (0, j))
    a_spec = pl.BlockSpec((k, tm), lambda i, j: (0, i)) if ta else pl.BlockSpec((tm, k), lambda i, j: (i, 0))
    if out_blocked:
        o_spec, o_shape = pl.BlockSpec((None, tm, tn), lambda i, j: (j, i, 0)), S((N_BLK, m, tn), out_dtype)
    else:
        o_spec, o_shape = pl.BlockSpec((tm, tn), lambda i, j: (i, j)), S((m, n), out_dtype)
    dims = (((0 if ta else 1,), (1 if tb else 0,)), ((), ()))
    gm, gn = m // tm, n // tn

    def product(a_ref, b_ref):
        if blocked and tb:
            acc = None
            for q in range(N_BLK):
                t = lax.dot_general(a_ref[:, q * cb:(q + 1) * cb], b_ref[q], (((1,), (1,)), ((), ())),
                                    preferred_element_type=F32)
                acc = t if acc is None else acc + t
            return acc
        return lax.dot_general(a_ref[...], b_ref[...], dims, preferred_element_type=F32)

    def body(a_ref, b_ref, o_ref):
        o_ref[...] = product(a_ref, b_ref).astype(o_ref.dtype)

    return _pcall(body, job, name=name, grid=(gm, gn), in_specs=[a_spec, b_spec], out_specs=[o_spec],
                  out_shape=[o_shape], semantics=("parallel", "parallel"))(a, b)[0]


def ew_fwd(name, fn, grid, ins, outs):
    n_in = len(ins)

    def body(*refs):
        vals = [r[...].astype(F32) for r in refs[:n_in]]
        res = fn(pl.program_id(0), *vals)
        for r, v in zip(refs[n_in:], res):
            r[...] = v.astype(r.dtype)

    out = pl.pallas_call(
        body, name=name, grid=grid, in_specs=[s for _, s in ins], out_specs=[s for _, s in outs],
        out_shape=[sd for sd, _ in outs], compiler_params=_cp(("parallel", "parallel")))(*[a for a, _ in ins])
    return out


def ew_bwd(name, fn, grid, ins, cts, wrt, acc, add=None):
    n_in, n_ct, n_wrt, n_acc = len(ins), len(cts), len(wrt), len(acc)
    has_add = add is not None

    def body(*refs):
        in_refs = refs[:n_in]
        ct_refs = refs[n_in:n_in + n_ct]
        pos = n_in + n_ct
        add_ref = refs[pos] if has_add else None
        pos += 1 if has_add else 0
        wrt_refs = refs[pos:pos + n_wrt]
        acc_refs = refs[pos + n_wrt:pos + n_wrt + n_acc]
        col, tok = pl.program_id(0), pl.program_id(1)
        vals = [r[...].astype(F32) for r in in_refs]
        _, vjp = jax.vjp(lambda *a: fn(col, *a), *vals)
        grads = vjp(tuple(c[...].astype(F32) for c in ct_refs))
        for pos_w, ((idx, _, _), r) in enumerate(zip(wrt, wrt_refs)):
            g = grads[idx]
            if has_add and pos_w == 0:
                g = g + add_ref[...]
            r[...] = g.astype(r.dtype)
        for (idx, _, _, over_cols), r in zip(acc, acc_refs):
            first = (tok == 0) & (col == 0) if over_cols else tok == 0

            @pl.when(first)
            def _():
                r[...] = jnp.zeros_like(r)

            r[...] += grads[idx]

    arrays = [a for a, _ in ins] + [a for a, _ in cts] + ([add[0]] if has_add else [])
    in_specs = [s for _, s in ins] + [s for _, s in cts] + ([add[1]] if has_add else [])
    over_any = any(o for *_, o in acc)
    out = pl.pallas_call(
        body, name=name, grid=grid, in_specs=in_specs,
        out_specs=[s for _, _, s in wrt] + [s for _, _, s, _ in acc],
        out_shape=[sd for _, sd, _ in wrt] + [sd for _, sd, _, _ in acc],
        compiler_params=_cp(("arbitrary" if over_any else "parallel", "arbitrary")))(*arrays)
    return out


def _tok(width, col=0):
    return lambda tm: pl.BlockSpec((tm, width), lambda j, i: (i, col))


def _tokcol(off=0):
    return lambda tm: pl.BlockSpec((tm, LANES), lambda j, i: (i, off + j))


def _par(width, col=0):
    return pl.BlockSpec((1, width), lambda j, i: (0, col))


def _parcol(off=0):
    return pl.BlockSpec((1, LANES), lambda j, i: (0, off + j))


def _rms(x, w, eps=RMS_EPS):
    return x * lax.rsqrt(jnp.mean(x * x, axis=-1, keepdims=True) + eps) * w


def _silu(x):
    return x * jax.nn.sigmoid(x)


def fn_rms(col, x, w):
    return (_rms(x, w),)


def fn_swiglu(col, gate, up):
    return (_silu(gate) * up,)


def swiglu_bwd(name, u, da, tm):
    n_tok, f2 = u.shape
    f = f2 // 2

    def body(g_ref, up_ref, da_ref, o_ref):
        g, d = g_ref[...].astype(F32), da_ref[...].astype(F32)
        s = jax.nn.sigmoid(g)
        o_ref[:, :f] = (d * up_ref[...].astype(F32) * (s * (1.0 + g * (1.0 - s)))).astype(o_ref.dtype)
        o_ref[:, f:] = (d * (g * s)).astype(o_ref.dtype)

    half = lambda c: pl.BlockSpec((tm, f), lambda i: (i, c))
    return pl.pallas_call(
        body, name=name, grid=(n_tok // tm,), in_specs=[half(0), half(1), half(0)],
        out_specs=pl.BlockSpec((tm, f2), lambda i: (i, 0)), out_shape=S((n_tok, f2), BF16),
        compiler_params=_cp(("parallel",)))(u, u, da)


def fn_gdnout(col, o, z, nw):
    return (_rms(o, nw) * _silu(z),)


def fn_glu(col, a, g, ba, bg):
    return ((a + ba) * jax.nn.sigmoid(g + bg),)


def fn_lnsilu(col, h, g, b):
    mu = jnp.mean(h, axis=-1, keepdims=True)
    var = jnp.mean(jnp.square(h - mu), axis=-1, keepdims=True)
    return (_silu((h - mu) * lax.rsqrt(var + LN_EPS) * g + b),)


def fn_merge(col, ya, yb, ga, gb, bo):
    return (jax.nn.sigmoid(ga) * ya + jax.nn.sigmoid(gb) * (yb + bo),)


HALO = 32


def conv_fwd(name, x, col_off, n_ch, w, bias, tb):
    n_tok = x.shape[0]
    k = w.shape[0]
    nt = n_tok // tb

    def body(xp_ref, xc_ref, w_ref, *rest):
        if bias is not None:
            b_ref, o_ref, xs = rest
        else:
            o_ref, xs = rest
        i = pl.program_id(1)
        xs[0:HALO, :] = jnp.where(i == 0, 0.0, xp_ref[tb - HALO:tb, :].astype(F32))
        xs[HALO:HALO + tb, :] = xc_ref[...].astype(F32)
        acc = jnp.zeros((tb, LANES), F32)
        for j in range(k):
            s = k - 1 - j
            acc = acc + w_ref[j:j + 1, :] * xs[HALO - s:HALO - s + tb, :]
        if bias is not None:
            acc = acc + b_ref[...]
        o_ref[...] = acc

    in_specs = [pl.BlockSpec((tb, LANES), lambda j, i: (jnp.maximum(i - 1, 0), col_off + j)),
                pl.BlockSpec((tb, LANES), lambda j, i: (i, col_off + j)),
                pl.BlockSpec((k, LANES), lambda j, i: (0, j))]
    args = [x, x, w]
    if bias is not None:
        in_specs.append(pl.BlockSpec((1, LANES), lambda j, i: (0, j)))
        args.append(bias)
    return pl.pallas_call(
        body, name=name, grid=(n_ch // LANES, nt), in_specs=in_specs,
        out_specs=pl.BlockSpec((tb, LANES), lambda j, i: (i, j)), out_shape=S((n_tok, n_ch), F32),
        scratch_shapes=[pltpu.VMEM((HALO + tb, LANES), F32)],
        compiler_params=_cp(("parallel", "parallel")))(*args)


def conv_bwd(name, x, col_off, n_ch, w, dy, dx_dtype, tb):
    n_tok = x.shape[0]
    k = w.shape[0]
    nt = n_tok // tb

    def body(xp_ref, xc_ref, w_ref, dyc_ref, dyn_ref, dx_ref, dw_ref, db_ref, xs, dys):
        i = pl.program_id(1)
        xs[0:HALO, :] = jnp.where(i == 0, 0.0, xp_ref[tb - HALO:tb, :].astype(F32))
        xs[HALO:HALO + tb, :] = xc_ref[...].astype(F32)
        dyc = dyc_ref[...]
        dys[0:tb, :] = dyc
        dys[tb:tb + HALO, :] = jnp.where(i == nt - 1, 0.0, dyn_ref[0:HALO, :])

        @pl.when(i == 0)
        def _():
            dw_ref[...] = jnp.zeros_like(dw_ref)
            db_ref[...] = jnp.zeros_like(db_ref)

        acc = jnp.zeros((tb, LANES), F32)
        for j in range(k):
            s = k - 1 - j
            acc = acc + w_ref[j:j + 1, :] * dys[s:s + tb, :]
            dw_ref[j:j + 1, :] += jnp.sum(dyc * xs[HALO - s:HALO - s + tb, :], axis=0, keepdims=True)
        dx_ref[...] = acc.astype(dx_ref.dtype)
        db_ref[...] += jnp.sum(dyc, axis=0, keepdims=True)

    in_specs = [pl.BlockSpec((tb, LANES), lambda j, i: (jnp.maximum(i - 1, 0), col_off + j)),
                pl.BlockSpec((tb, LANES), lambda j, i: (i, col_off + j)),
                pl.BlockSpec((k, LANES), lambda j, i: (0, j)),
                pl.BlockSpec((tb, LANES), lambda j, i: (i, j)),
                pl.BlockSpec((tb, LANES), lambda j, i: (jnp.minimum(i + 1, nt - 1), j))]
    return pl.pallas_call(
        body, name=name, grid=(n_ch // LANES, nt), in_specs=in_specs,
        out_specs=[pl.BlockSpec((tb, LANES), lambda j, i: (i, j)),
                   pl.BlockSpec((k, LANES), lambda j, i: (0, j)),
                   pl.BlockSpec((1, LANES), lambda j, i: (0, j))],
        out_shape=[S((n_tok, n_ch), dx_dtype), S((k, n_ch), F32), S((1, n_ch), F32)],
        scratch_shapes=[pltpu.VMEM((HALO + tb, LANES), F32), pltpu.VMEM((tb + HALO, LANES), F32)],
        compiler_params=_cp(("parallel", "arbitrary")))(x, x, w, dy, dy)


def gdnconv_fwd(name, p, w, tb, job=None):
    n_tok = p.shape[0]
    k = w.shape[0]
    nt = n_tok // tb

    def body(xp_ref, xc_ref, w_ref, o_ref, xs):
        j, i = pl.program_id(0), pl.program_id(1)
        xs[0:HALO, :] = jnp.where(i == 0, 0.0, xp_ref[tb - HALO:tb, :].astype(F32))
        xs[HALO:HALO + tb, :] = xc_ref[...].astype(F32)
        c = jnp.zeros((tb, LANES), F32)
        for t in range(k):
            s = k - 1 - t
            c = c + w_ref[t:t + 1, :] * xs[HALO - s:HALO - s + tb, :]
        y = c * jax.nn.sigmoid(c)
        r = lax.rsqrt(jnp.sum(y * y, axis=-1, keepdims=True) + 1e-6) * jnp.where(j < HEADS, DK ** -0.5, 1.0)
        o_ref[...] = jnp.where(j < 2 * HEADS, y * r, y)

    return _pcall(
        body, job, name=name, grid=(W_QKV // LANES, nt),
        in_specs=[pl.BlockSpec((tb, LANES), lambda j, i: (jnp.maximum(i - 1, 0), j)),
                  pl.BlockSpec((tb, LANES), lambda j, i: (i, j)),
                  pl.BlockSpec((k, LANES), lambda j, i: (0, j))],
        out_specs=[pl.BlockSpec((tb, LANES), lambda j, i: (i, j))], out_shape=[S((n_tok, W_QKV), F32)],
        scratch_shapes=[pltpu.VMEM((HALO + tb, LANES), F32)], semantics=("parallel", "parallel"))(p, p, w)[0]


def gdnconv_bwd(name, p, w, dn, tb, job=None):
    n_tok = p.shape[0]
    k = w.shape[0]
    nt = n_tok // tb
    ext = tb + HALO

    def body(xp_ref, xc_ref, xn_ref, w_ref, dnc_ref, dnn_ref, dx_ref, dw_ref, xs, dns, dcs):
        j, i = pl.program_id(0), pl.program_id(1)
        xs[0:HALO, :] = jnp.where(i == 0, 0.0, xp_ref[tb - HALO:tb, :].astype(F32))
        xs[HALO:HALO + tb, :] = xc_ref[...].astype(F32)
        xs[HALO + tb:HALO + ext, :] = jnp.where(i == nt - 1, 0.0, xn_ref[0:HALO, :].astype(F32))
        dns[0:tb, :] = dnc_ref[...]
        dns[tb:ext, :] = jnp.where(i == nt - 1, 0.0, dnn_ref[0:HALO, :])

        @pl.when(i == 0)
        def _():
            dw_ref[...] = jnp.zeros_like(dw_ref)

        c = jnp.zeros((ext, LANES), F32)
        for t in range(k):
            s = k - 1 - t
            c = c + w_ref[t:t + 1, :] * xs[HALO - s:HALO - s + ext, :]
        d = dns[...]
        sg = jax.nn.sigmoid(c)
        y = c * sg
        r = lax.rsqrt(jnp.sum(y * y, axis=-1, keepdims=True) + 1e-6)
        scale = jnp.where(j < HEADS, DK ** -0.5, 1.0)
        dy_norm = scale * (d * r - y * (r * r * r) * jnp.sum(d * y, axis=-1, keepdims=True))
        dy = jnp.where(j < 2 * HEADS, dy_norm, d)
        dc = dy * (sg * (1.0 + c * (1.0 - sg)))
        dcs[...] = dc
        acc = jnp.zeros((tb, LANES), F32)
        for t in range(k):
            s = k - 1 - t
            acc = acc + w_ref[t:t + 1, :] * dcs[s:s + tb, :]
            dw_ref[t:t + 1, :] += jnp.sum(dcs[0:tb, :] * xs[HALO - s:HALO - s + tb, :], axis=0, keepdims=True)
        dx_ref[...] = acc.astype(dx_ref.dtype)

    cur = lambda j, i: (i, j)
    nxt = lambda j, i: (jnp.minimum(i + 1, nt - 1), j)
    return _pcall(
        body, job, name=name, grid=(W_QKV // LANES, nt),
        in_specs=[pl.BlockSpec((tb, LANES), lambda j, i: (jnp.maximum(i - 1, 0), j)),
                  pl.BlockSpec((tb, LANES), cur), pl.BlockSpec((tb, LANES), nxt),
                  pl.BlockSpec((k, LANES), lambda j, i: (0, j)),
                  pl.BlockSpec((tb, LANES), cur), pl.BlockSpec((tb, LANES), nxt)],
        out_specs=[pl.BlockSpec((tb, LANES), cur), pl.BlockSpec((k, LANES), lambda j, i: (0, j))],
        out_shape=[S((n_tok, W_QKV), BF16), S((k, W_QKV), F32)],
        scratch_shapes=[pltpu.VMEM((HALO + ext, LANES), F32), pltpu.VMEM((ext, LANES), F32),
                        pltpu.VMEM((ext, LANES), F32)],
        semantics=("parallel", "arbitrary"))(p, p, p, w, dn, dn)


GDN_GROUP = 4


def _dotb(a, b, ca, cb):
    return lax.dot_general(a.astype(BF16), b.astype(BF16), (((ca,), (cb,)), ((), ())), preferred_element_type=F32)


def _dot32(a, b, ca, cb):
    return lax.dot_general(a, b, (((ca,), (cb,)), ((), ())), preferred_element_type=F32,
                           precision=lax.Precision.HIGHEST)


def _dot3_many(xs, ys, ca, cb):
    xh = [x.astype(BF16) for x in xs]
    xl = [(x - h.astype(F32)).astype(BF16) for x, h in zip(xs, xh)]
    yh = [y.astype(BF16) for y in ys]
    yl = [(y - h.astype(F32)).astype(BF16) for y, h in zip(ys, yh)]
    dg = lambda p, q: lax.dot_general(p, q, (((ca,), (cb,)), ((), ())), preferred_element_type=F32)
    hh = [dg(p, q) for p, q in zip(xh, yh)]
    hl = [dg(p, q) for p, q in zip(xh, yl)]
    lh = [dg(p, q) for p, q in zip(xl, yh)]
    return [a + (b + c) for a, b, c in zip(hh, hl, lh)]


@jax.custom_vjp
def _mm3_many(xs, ys):
    return _dot3_many(xs, ys, 1, 0)


def _mm3_fwd(xs, ys):
    return _dot3_many(xs, ys, 1, 0), (xs, ys)


def _mm3_bwd(res, cts):
    xs, ys = res
    return _dot3_many(cts, ys, 1, 1), _dot3_many(xs, cts, 0, 0)


_mm3_many.defvjp(_mm3_fwd, _mm3_bwd)


@jax.custom_vjp
def _inv_unit_lower_many(mats):
    n = mats[0].shape[0]
    eye = (lax.broadcasted_iota(jnp.int32, (n, n), 0) == lax.broadcasted_iota(jnp.int32, (n, n), 1)).astype(F32)
    inv = [eye - a for a in mats]
    p = list(mats)
    for _ in range(int(math.log2(n)) - 1):
        p = _dot3_many(p, p, 1, 0)
        upd = _dot3_many(inv, p, 1, 0)
        inv = [i + u for i, u in zip(inv, upd)]
    return inv


def _inv_fwd(mats):
    t = _inv_unit_lower_many(mats)
    return t, t


def _inv_bwd(t, dt):
    x = _dot3_many(t, dt, 0, 0)
    return ([-y for y in _dot3_many(x, t, 1, 1)],)


_inv_unit_lower_many.defvjp(_inv_fwd, _inv_bwd)


@jax.custom_vjp
def _inv_given(mats, saved):
    return list(saved)


def _inv_given_fwd(mats, saved):
    return list(saved), list(saved)


def _inv_given_bwd(t, dt):
    return _inv_bwd(t, dt)[0], [jnp.zeros_like(s) for s in t]


_inv_given.defvjp(_inv_given_fwd, _inv_given_bwd)


def _softplus(x):
    return jnp.maximum(x, 0.0) + jnp.log(1.0 + jnp.exp(-jnp.abs(x)))


def _gdn_intra(qs, ks, vs, pbas, alog, dtb, tinv_saved=None):
    c = pbas[0].shape[0]
    row = lax.broadcasted_iota(jnp.int32, (c, c), 0)
    colm = lax.broadcasted_iota(jnp.int32, (c, c), 1)
    causal, strict = row >= colm, row > colm
    tril = causal.astype(F32)
    lane = lax.broadcasted_iota(jnp.int32, (1, LANES), 1)
    sub = lax.broadcasted_iota(jnp.int32, (LANES, 1), 0)
    last = (lax.broadcasted_iota(jnp.int32, (c, 1), 0) == c - 1).astype(F32)
    beta_all = [jax.nn.sigmoid(pb) for pb in pbas]
    g_all = [-jnp.exp(alog) * _softplus(pb + dtb) for pb in pbas]
    gc_all = [_dot32(tril, ga, 1, 0) for ga in g_all]
    gr_all = [_dot32(ga, tril, 0, 1) for ga in g_all]
    idx = [(g, h) for g in range(len(pbas)) for h in range(HEADS)]
    beta = [jnp.sum(beta_all[g] * (lane == h).astype(F32), axis=1, keepdims=True) for g, h in idx]
    gc = [jnp.sum(gc_all[g] * (lane == HEADS + h).astype(F32), axis=1, keepdims=True) for g, h in idx]
    gr = [jnp.sum(gr_all[g] * (sub == HEADS + h).astype(F32), axis=0, keepdims=True) for g, h in idx]
    decay = [jnp.where(causal, jnp.exp(jnp.where(causal, a - b, 0.0)), 0.0) for a, b in zip(gc, gr)]
    kk = [_dotb(k, k, 1, 1) for k in ks]
    a_mats = [jnp.where(strict, x * d * b, 0.0) for x, d, b in zip(kk, decay, beta)]
    tinv = _inv_unit_lower_many(a_mats) if tinv_saved is None else _inv_given(a_mats, tinv_saved)
    eg = [jnp.exp(a) for a in gc]
    g_last = [jnp.sum(a * last, axis=0, keepdims=True) for a in gc]
    us = _mm3_many(tinv, [v * b for v, b in zip(vs, beta)])
    ws = _mm3_many(tinv, [k * (b * e) for k, b, e in zip(ks, beta, eg)])
    qds = [q * e for q, e in zip(qs, eg)]
    kds = [k * jnp.exp(gl - a) for k, gl, a in zip(ks, g_last, gc)]
    qks = [_dotb(q, k, 1, 1) * d for q, k, d in zip(qs, ks, decay)]
    decs = [jnp.exp(gl) for gl in g_last]
    return us, ws, qds, kds, qks, decs, tinv


def _gdn_seq(us, ws, qds, kds, qks, decs, states):
    corr = [_dotb(w, st, 1, 0) for w, st in zip(ws, states)]
    from_state = [_dotb(qd, st, 1, 0) for qd, st in zip(qds, states)]
    v_new = [u - x for u, x in zip(us, corr)]
    intra = [_dotb(qk, vn, 1, 0) for qk, vn in zip(qks, v_new)]
    upd = [_dotb(kd, vn, 0, 0) for kd, vn in zip(kds, v_new)]
    outs = [a + b for a, b in zip(from_state, intra)]
    news = [st * d + x for st, d, x in zip(states, decs, upd)]
    return outs, news


def _heads(ref, rows=slice(None), base=0):
    return [ref[rows, (base + h) * DK:(base + h + 1) * DK].astype(F32) for h in range(HEADS)]


def _qk_heads(ref, rows=slice(None)):
    return [ref[rows, h * DK:h * DK + CHUNK].astype(F32) for h in range(HEADS)]


def _put_heads(ref, vals, rows=slice(None), base=0):
    for h in range(HEADS):
        ref[rows, (base + h) * DK:(base + h + 1) * DK] = vals[h].astype(ref.dtype)


def _put_qk(ref, vals, rows=slice(None)):
    for h in range(HEADS):
        ref[rows, h * DK:h * DK + CHUNK] = vals[h].astype(ref.dtype)
        ref[rows, h * DK + CHUNK:(h + 1) * DK] = jnp.zeros(vals[h].shape, ref.dtype)


def _group(n_chunks):
    return GDN_GROUP if n_chunks % GDN_GROUP == 0 else 1


def gdn_intra_fwd(name, qkvn, p, alog, dtb, job=None):
    n_tok = qkvn.shape[0]
    n = n_tok // CHUNK
    grp = _group(n)
    hd = HEADS * DK
    rb = grp * CHUNK

    def body(q_ref, k_ref, v_ref, pba_ref, al_ref, dt_ref, u_ref, w_ref, qd_ref, kd_ref, qk_ref, ti_ref, dec_ref):
        rows = [slice(g * CHUNK, (g + 1) * CHUNK) for g in range(grp)]
        cat = lambda ref: [t for r in rows for t in _heads(ref, r)]
        us, ws, qds, kds, qks, decs, tinv = _gdn_intra(cat(q_ref), cat(k_ref), cat(v_ref),
                                                       [pba_ref[r, :].astype(F32) for r in rows], al_ref[...], dt_ref[...])
        for g, r in enumerate(rows):
            part = slice(g * HEADS, (g + 1) * HEADS)
            _put_heads(u_ref, us[part], r)
            _put_heads(w_ref, ws[part], r)
            _put_heads(qd_ref, qds[part], r)
            _put_heads(kd_ref, kds[part], r)
            _put_qk(qk_ref, qks[part], r)
            _put_qk(ti_ref, tinv[part], r)
            for h in range(HEADS):
                dec_ref[g * HEADS + h:g * HEADS + h + 1, :] = jnp.broadcast_to(decs[g * HEADS + h], (1, LANES))

    blk = lambda c: pl.BlockSpec((rb, hd), lambda i: (i, c))
    par = pl.BlockSpec((1, LANES), lambda i: (0, 0))
    return _pcall(
        body, job, name=name, grid=(n // grp,),
        in_specs=[blk(0), blk(1), blk(2), pl.BlockSpec((rb, LANES), lambda i: (i, COL_BA)), par, par],
        out_specs=[blk(0)] * 6 + [pl.BlockSpec((grp * HEADS, LANES), lambda i: (i, 0))],
        out_shape=[S((n_tok, hd), F32)] + [S((n_tok, hd), BF16)] * 4 + [S((n_tok, hd), F32), S((n * HEADS, LANES), F32)],
        semantics=("parallel",))(qkvn, qkvn, qkvn, p, alog, dtb)


SEQ_GROUP = 2


def _decs_at(ref, g):
    return [ref[g * HEADS + h:g * HEADS + h + 1, 0:1] for h in range(HEADS)]


def gdn_seq_fwd(name, u, w, qd, kd, qk, dec):
    n_tok = u.shape[0]
    n = n_tok // CHUNK
    grp = SEQ_GROUP if n % SEQ_GROUP == 0 else 1
    hd = HEADS * DK

    def body(u_ref, w_ref, qd_ref, kd_ref, qk_ref, dec_ref, o_ref, s_ref, st):
        @pl.when(pl.program_id(0) == 0)
        def _():
            st[...] = jnp.zeros_like(st)

        states = [st[h * DK:(h + 1) * DK, :] for h in range(HEADS)]
        for g in range(grp):
            r = slice(g * CHUNK, (g + 1) * CHUNK)
            for h in range(HEADS):
                s_ref[g, h * DK:(h + 1) * DK, :] = states[h].astype(s_ref.dtype)
            outs, states = _gdn_seq(_heads(u_ref, r), _heads(w_ref, r), _heads(qd_ref, r), _heads(kd_ref, r),
                                    _qk_heads(qk_ref, r), _decs_at(dec_ref, g), states)
            _put_heads(o_ref, outs, r)
        for h in range(HEADS):
            st[h * DK:(h + 1) * DK, :] = states[h]

    blk = pl.BlockSpec((grp * CHUNK, hd), lambda i: (i, 0))
    return pl.pallas_call(
        body, name=name, grid=(n // grp,),
        in_specs=[blk] * 5 + [pl.BlockSpec((grp * HEADS, LANES), lambda i: (i, 0))],
        out_specs=[blk, pl.BlockSpec((grp, hd, DK), lambda i: (i, 0, 0))],
        out_shape=[S((n_tok, hd), F32), S((n, hd, DK), BF16)],
        scratch_shapes=[pltpu.VMEM((hd, DK), F32)],
        compiler_params=_cp(("arbitrary",)))(u, w, qd, kd, qk, dec)


def gdn_seq_bwd(name, u, w, qd, kd, qk, dec, states, do):
    n_tok = u.shape[0]
    n = n_tok // CHUNK
    grp = SEQ_GROUP if n % SEQ_GROUP == 0 else 1
    ns = n // grp
    hd = HEADS * DK

    def body(u_ref, w_ref, qd_ref, kd_ref, qk_ref, dec_ref, s_ref, do_ref,
             du_ref, dw_ref, dqd_ref, dkd_ref, dqk_ref, ddec_ref, dst):
        @pl.when(pl.program_id(0) == 0)
        def _():
            dst[...] = jnp.zeros_like(dst)

        d_news = [dst[h * DK:(h + 1) * DK, :] for h in range(HEADS)]
        for g in reversed(range(grp)):
            r = slice(g * CHUNK, (g + 1) * CHUNK)
            states = [s_ref[g, h * DK:(h + 1) * DK, :].astype(F32) for h in range(HEADS)]
            _, vjp = jax.vjp(_gdn_seq, _heads(u_ref, r), _heads(w_ref, r), _heads(qd_ref, r), _heads(kd_ref, r),
                             _qk_heads(qk_ref, r), _decs_at(dec_ref, g), states)
            du, dw, dqd, dkd, dqk, ddec, d_news = vjp((_heads(do_ref, r), d_news))
            _put_heads(du_ref, du, r)
            _put_heads(dw_ref, dw, r)
            _put_heads(dqd_ref, dqd, r)
            _put_heads(dkd_ref, dkd, r)
            _put_qk(dqk_ref, dqk, r)
            for h in range(HEADS):
                ddec_ref[g * HEADS + h:g * HEADS + h + 1, :] = jnp.broadcast_to(ddec[h], (1, LANES))
        for h in range(HEADS):
            dst[h * DK:(h + 1) * DK, :] = d_news[h]

    blk = pl.BlockSpec((grp * CHUNK, hd), lambda i: (ns - 1 - i, 0))
    dspec = pl.BlockSpec((grp * HEADS, LANES), lambda i: (ns - 1 - i, 0))
    return pl.pallas_call(
        body, name=name, grid=(ns,),
        in_specs=[blk] * 5 + [dspec, pl.BlockSpec((grp, hd, DK), lambda i: (ns - 1 - i, 0, 0)), blk],
        out_specs=[blk] * 5 + [dspec],
        out_shape=[S((n_tok, hd), F32)] * 5 + [S((n * HEADS, LANES), F32)],
        scratch_shapes=[pltpu.VMEM((hd, DK), F32)],
        compiler_params=_cp(("arbitrary",)))(u, w, qd, kd, qk, dec, states, do)


def gdn_intra_bwd(name, qkvn, p, alog, dtb, du, dw, dqd, dkd, dqk, ddec, tinv, job=None):
    n_tok = qkvn.shape[0]
    n = n_tok // CHUNK
    grp = _group(n)
    hd = HEADS * DK
    rb = grp * CHUNK

    def body(q_ref, k_ref, v_ref, pba_ref, al_ref, dt_ref, du_ref, dw_ref, dqd_ref, dkd_ref, dqk_ref, ddec_ref, ti_ref,
             dqkv_ref, dpba_ref, dal_ref, ddt_ref):
        @pl.when(pl.program_id(0) == 0)
        def _():
            dal_ref[...] = jnp.zeros_like(dal_ref)
            ddt_ref[...] = jnp.zeros_like(ddt_ref)

        rows = [slice(g * CHUNK, (g + 1) * CHUNK) for g in range(grp)]
        cat = lambda ref: [t for r in rows for t in _heads(ref, r)]
        kept = [t for r in rows for t in _qk_heads(ti_ref, r)]
        _, vjp = jax.vjp(lambda *a: _gdn_intra(*a, tinv_saved=kept)[:6], cat(q_ref), cat(k_ref), cat(v_ref),
                         [pba_ref[r, :].astype(F32) for r in rows], al_ref[...], dt_ref[...])
        cts = (cat(du_ref), cat(dw_ref), cat(dqd_ref), cat(dkd_ref), [t for r in rows for t in _qk_heads(dqk_ref, r)],
               [ddec_ref[i:i + 1, 0:1] for i in range(grp * HEADS)])
        dq, dk, dv, dpba, dal, ddt = vjp(cts)
        for g, r in enumerate(rows):
            part = slice(g * HEADS, (g + 1) * HEADS)
            _put_heads(dqkv_ref, dq[part], r, 0)
            _put_heads(dqkv_ref, dk[part], r, HEADS)
            _put_heads(dqkv_ref, dv[part], r, 2 * HEADS)
            dpba_ref[r, :] = dpba[g].astype(dpba_ref.dtype)
        dal_ref[...] += dal
        ddt_ref[...] += ddt

    blk = lambda c: pl.BlockSpec((rb, hd), lambda i: (i, c))
    par = pl.BlockSpec((1, LANES), lambda i: (0, 0))
    return _pcall(
        body, job, name=name, grid=(n // grp,),
        in_specs=[blk(0), blk(1), blk(2), pl.BlockSpec((rb, LANES), lambda i: (i, COL_BA)), par, par]
        + [blk(0)] * 5 + [pl.BlockSpec((grp * HEADS, LANES), lambda i: (i, 0)), blk(0)],
        out_specs=[pl.BlockSpec((rb, 3 * hd), lambda i: (i, 0)), pl.BlockSpec((rb, LANES), lambda i: (i, 0)), par, par],
        out_shape=[S((n_tok, 3 * hd), F32), S((n_tok, LANES), BF16), S((1, LANES), F32), S((1, LANES), F32)],
        semantics=("arbitrary",))(qkvn, qkvn, qkvn, p, alog, dtb, du, dw, dqd, dkd, dqk, ddec, tinv)


def loss_head(name, y, target, tm):
    n_tok, d = y.shape

    def body(y_ref, t_ref, dy_ref, l_ref):
        @pl.when(pl.program_id(0) == 0)
        def _():
            l_ref[...] = jnp.zeros_like(l_ref)

        e = y_ref[...] - t_ref[...]
        dy_ref[...] = e * (1.0 / d)
        l_ref[...] += jnp.sum(e * e, keepdims=True) * (0.5 / d)

    spec = pl.BlockSpec((tm, d), lambda i: (i, 0))
    return pl.pallas_call(
        body, name=name, grid=(n_tok // tm,), in_specs=[spec, spec],
        out_specs=[spec, pl.BlockSpec((1, 1), lambda i: (0, 0))], out_shape=[S((n_tok, d), F32), S((1, 1), F32)],
        compiler_params=_cp(("arbitrary",)))(y, target)


def _tm(n_tok):
    return min(512, n_tok)


def _gconv_tb(n_tok):
    return 1024 if n_tok % 1024 == 0 else _tm(n_tok)


def ffn_fwd(tag, x, w, jobs=None):
    n_tok = x.shape[0]
    tm = _tm(n_tok)
    g1 = (1, n_tok // tm)
    tD = _tok(D_MODEL)(tm)
    (h,) = ew_fwd(tag + "_rms", fn_rms, g1, [(x, tD), (w["norm_pre"], _par(D_MODEL))], [(S((n_tok, D_MODEL), BF16), tD)])
    u = mm(tag + "_in", h, w["w_in"], out_dtype=BF16, job=_take(jobs, "ffn_in"))
    tF = lambda c: _tok(D_FF, c)(tm)
    (a,) = ew_fwd(tag + "_swiglu", fn_swiglu, g1, [(u, tF(0)), (u, tF(1))], [(S((n_tok, D_FF), BF16), tF(0))])
    f = mm(tag + "_out", a, w["w_out"], job=_take(jobs, "ffn_out"))
    fn_res = lambda col, x_, f_, w_: (x_ + 0.5 * _rms(f_, w_),)
    (xo,) = ew_fwd(tag + "_res", fn_res, g1, [(x, tD), (f, tD), (w["norm_post"], _par(D_MODEL))],
                   [(S((n_tok, D_MODEL), F32), tD)])
    return xo, dict(x=x, h=h, u=u, a=a, f=f)


def ffn_bwd(tag, dxo, sv, w, jobs=None):
    n_tok = dxo.shape[0]
    tm = _tm(n_tok)
    g1 = (1, n_tok // tm)
    tD = _tok(D_MODEL)(tm)
    pD = _par(D_MODEL)
    fn_post = lambda col, f_, w_: (0.5 * _rms(f_, w_),)
    df, d_post = ew_bwd(tag + "_res_b", fn_post, g1, [(sv["f"], tD), (w["norm_post"], pD)], [(dxo, tD)],
                        [(0, S((n_tok, D_MODEL), BF16), tD)], [(1, S((1, D_MODEL), F32), pD, False)])
    da = mm(tag + "_out_bx", df, w["w_out"], tb=True, out_dtype=BF16, job=_take(jobs, "ffn_out_bx"))
    d_wout = mm(tag + "_out_bw", sv["a"], df, ta=True, out_dtype=GRAD_DT, job=_take(jobs, "ffn_out_bw"))
    tF = lambda c: _tok(D_FF, c)(tm)
    du = swiglu_bwd(tag + "_swiglu_b", sv["u"], da, tm)
    dh = mm(tag + "_in_bx", du, w["w_in"], tb=True, job=_take(jobs, "ffn_in_bx"))
    d_win = mm(tag + "_in_bw", sv["h"], du, ta=True, out_dtype=GRAD_DT, job=_take(jobs, "ffn_in_bw"), out_blocked=True)
    dx, d_pre = ew_bwd(tag + "_rms_b", fn_rms, g1, [(sv["x"], tD), (w["norm_pre"], pD)], [(dh, tD)],
                       [(0, S((n_tok, D_MODEL), F32), tD)], [(1, S((1, D_MODEL), F32), pD, False)], add=(dxo, tD))
    return dx, dict(norm_pre=d_pre, norm_post=d_post, w_in=d_win, w_out=d_wout)


def mix_fwd(tag, x, w, jobs=None):
    n_tok = x.shape[0]
    tm = _tm(n_tok)
    nt = n_tok // tm
    g1 = (1, nt)
    tD = _tok(D_MODEL)(tm)
    pD = _par(D_MODEL)
    (h,) = ew_fwd(tag + "_rms", fn_rms, g1, [(x, tD), (w["norm_pre"], pD)], [(S((n_tok, D_MODEL), BF16), tD)])
    p = mm(tag + "_in", h, w["w_all"], out_dtype=BF16, job=_take(jobs, "mix_in"))
    qkvn = gdnconv_fwd(tag + "_gconv", p, w["conv_w"], _gconv_tb(n_tok), job=_take(jobs, "gconv"))
    tC = _tokcol()(tm)
    *intra, tinv, dec = gdn_intra_fwd(tag + "_gintra", qkvn, p, w["alog"], w["dtb"], job=_take(jobs, "gintra"))
    intra.append(dec)
    o, states = gdn_seq_fwd(tag + "_gseq", *intra)
    (on,) = ew_fwd(tag + "_gout", fn_gdnout, (HEADS, nt),
                   [(o, tC), (p, _tokcol(COL_Z)(tm)), (w["gdn_norm_w"], _par(LANES))],
                   [(S((n_tok, D_MODEL), BF16), tC)])
    ya = mm(tag + "_go", on, w["gdn_w_o"], job=_take(jobs, "mix_small"))
    (hglu,) = ew_fwd(tag + "_glu", fn_glu, (D_MODEL // LANES, nt),
                     [(p, _tokcol(COL_GLU)(tm)), (p, _tokcol(COL_GLU + D_MODEL // LANES)(tm)),
                      (w["pw1_b"], _parcol(0)), (w["pw1_b"], _parcol(D_MODEL // LANES))],
                     [(S((n_tok, D_MODEL), F32), tC)])
    hc = conv_fwd(tag + "_cconv", hglu, 0, D_MODEL, w["dw_w"], w["dw_b"], tm)
    (hs,) = ew_fwd(tag + "_ln", fn_lnsilu, g1, [(hc, tD), (w["ln_g"], pD), (w["ln_b"], pD)],
                   [(S((n_tok, D_MODEL), BF16), tD)])
    yb = mm(tag + "_co", hs, w["cnv_w_o"], job=_take(jobs, "mix_small"))
    tG = lambda cb: pl.BlockSpec((tm, D_MODEL), lambda j, i: (i, cb))
    gcol = (W_QKV + W_Z + W_GLU) // D_MODEL
    (ym,) = ew_fwd(tag + "_merge", fn_merge, g1, [(ya, tD), (yb, tD), (p, tG(gcol)), (p, tG(gcol + 1)), (w["b_o"], pD)],
                   [(S((n_tok, D_MODEL), BF16), tD)])
    y = mm(tag + "_wo", ym, w["w_out"], job=_take(jobs, "mix_small"))
    fn_res = lambda col, x_, f_, w_: (x_ + _rms(f_, w_),)
    (xo,) = ew_fwd(tag + "_res", fn_res, g1, [(x, tD), (y, tD), (w["norm_post"], pD)], [(S((n_tok, D_MODEL), F32), tD)])
    sv = dict(x=x, h=h, p=p, qkvn=qkvn, intra=intra, tinv=tinv, states=states, o=o, on=on, ya=ya, hglu=hglu, hc=hc, hs=hs, yb=yb, ym=ym, y=y)
    return xo, sv


def mix_bwd(tag, dxo, sv, w, jobs=None):
    n_tok = dxo.shape[0]
    tm = _tm(n_tok)
    nt = n_tok // tm
    g1 = (1, nt)
    tD = _tok(D_MODEL)(tm)
    pD = _par(D_MODEL)
    tC = _tokcol()(tm)
    p = sv["p"]
    sD = lambda dt: S((n_tok, D_MODEL), dt)
    fn_post = lambda col, f_, w_: (_rms(f_, w_),)
    dy, d_post = ew_bwd(tag + "_res_b", fn_post, g1, [(sv["y"], tD), (w["norm_post"], pD)], [(dxo, tD)],
                        [(0, sD(BF16), tD)], [(1, S((1, D_MODEL), F32), pD, False)])
    dym = mm(tag + "_wo_bx", dy, w["w_out"], tb=True, job=_take(jobs, "mix_small"))
    d_wout = mm(tag + "_wo_bw", sv["ym"], dy, ta=True, out_dtype=GRAD_DT, job=_take(jobs, "mix_small"))
    tG = lambda cb: pl.BlockSpec((tm, D_MODEL), lambda j, i: (i, cb))
    gcol = (W_QKV + W_Z + W_GLU) // D_MODEL
    dya, dyb, dga, dgb, d_bo = ew_bwd(
        tag + "_merge_b", fn_merge, g1, [(sv["ya"], tD), (sv["yb"], tD), (p, tG(gcol)), (p, tG(gcol + 1)), (w["b_o"], pD)],
        [(dym, tD)], [(0, sD(BF16), tD), (1, sD(BF16), tD), (2, sD(BF16), tD), (3, sD(BF16), tD)],
        [(4, S((1, D_MODEL), F32), pD, False)])
    dhs = mm(tag + "_co_bx", dyb, w["cnv_w_o"], tb=True, job=_take(jobs, "mix_small"))
    d_cwo = mm(tag + "_co_bw", sv["hs"], dyb, ta=True, out_dtype=GRAD_DT, job=_take(jobs, "mix_small"))
    dhc, d_lng, d_lnb = ew_bwd(tag + "_ln_b", fn_lnsilu, g1, [(sv["hc"], tD), (w["ln_g"], pD), (w["ln_b"], pD)], [(dhs, tD)],
                               [(0, sD(F32), tD)], [(1, S((1, D_MODEL), F32), pD, False), (2, S((1, D_MODEL), F32), pD, False)])
    dhglu, d_dww, d_dwb = conv_bwd(tag + "_cconv_b", sv["hglu"], 0, D_MODEL, w["dw_w"], dhc, F32, tm)
    nc = D_MODEL // LANES
    dpa, dpg, d_ba, d_bg = ew_bwd(
        tag + "_glu_b", fn_glu, (nc, nt),
        [(p, _tokcol(COL_GLU)(tm)), (p, _tokcol(COL_GLU + nc)(tm)), (w["pw1_b"], _parcol(0)), (w["pw1_b"], _parcol(nc))],
        [(dhglu, tC)], [(0, sD(BF16), tC), (1, sD(BF16), tC)],
        [(2, S((1, D_MODEL), F32), _parcol(0), False), (3, S((1, D_MODEL), F32), _parcol(0), False)])
    don = mm(tag + "_go_bx", dya, w["gdn_w_o"], tb=True, job=_take(jobs, "mix_small"))
    d_gwo = mm(tag + "_go_bw", sv["on"], dya, ta=True, out_dtype=GRAD_DT, job=_take(jobs, "mix_small"))
    do, dz, d_gnw = ew_bwd(tag + "_gout_b", fn_gdnout, (HEADS, nt),
                           [(sv["o"], tC), (p, _tokcol(COL_Z)(tm)), (w["gdn_norm_w"], _par(LANES))], [(don, tC)],
                           [(0, sD(F32), tC), (1, sD(BF16), tC)], [(2, S((1, LANES), F32), _par(LANES), True)])
    d_intra = gdn_seq_bwd(tag + "_gseq_b", *sv["intra"], sv["states"], do)
    dqkvn, dpba, d_alog, d_dtb = gdn_intra_bwd(tag + "_gintra_b", sv["qkvn"], p, w["alog"], w["dtb"], *d_intra,
                                               sv["tinv"], job=_take(jobs, "gintra_b"))
    dqkv, d_convw = gdnconv_bwd(tag + "_gconv_b", p, w["conv_w"], dqkvn, _gconv_tb(n_tok), job=_take(jobs, "gconv_b"))
    nd = D_MODEL // LANES
    pieces = [(dqkv, 0, 0), (dz, W_QKV // D_MODEL, COL_Z), (dpa, COL_GLU // nd, COL_GLU), (dpg, COL_GLU // nd + 1, COL_GLU + nd),
              (dga, COL_GATE // nd, COL_GATE), (dgb, COL_GATE // nd + 1, COL_GATE + nd), (dpba, COL_BA, COL_BA)]
    dh = mm_nt_sum(tag + "_in_bx", [(a, blk) for a, blk, _ in pieces], w["w_all"])
    d_wall = [mm(tag + "_in_bw", sv["h"], a, ta=True, out_dtype=GRAD_DT, job=_take(jobs, "mix_small"))
              for a, _, _ in pieces]
    dx, d_pre = ew_bwd(tag + "_rms_b", fn_rms, g1, [(sv["x"], tD), (w["norm_pre"], pD)], [(dh, tD)],
                       [(0, sD(F32), tD)], [(1, S((1, D_MODEL), F32), pD, False)], add=(dxo, tD))
    grads = dict(norm_pre=d_pre, norm_post=d_post, w_all=d_wall, conv_w=d_convw, alog=d_alog, dtb=d_dtb,
                 gdn_norm_w=d_gnw, gdn_w_o=d_gwo, pw1_b=jnp.concatenate([d_ba, d_bg], axis=1), dw_w=d_dww,
                 dw_b=d_dwb, ln_g=d_lng, ln_b=d_lnb, cnv_w_o=d_cwo, b_o=d_bo, w_out=d_wout)
    return dx, grads


def local_step(x, target, layers):
    saved = []
    for lw in layers:
        x, sv = layer_fwd(x, lw)
        saved.append(sv)
    dx, loss = loss_head("loss", x, target, _tm(x.shape[0]))
    grads = [None] * len(layers)
    for i in reversed(range(len(layers))):
        dx, grads[i] = layer_bwd(dx, saved[i], layers[i])
    return loss, dx, grads


def layer_fwd(x, lw, jobs=None):
    x, s1 = ffn_fwd("ffn", x, lw["ffn1"], jobs)
    x, s2 = mix_fwd("mix", x, lw["mix"], jobs)
    x, s3 = ffn_fwd("ffn", x, lw["ffn2"], jobs)
    return x, (s1, s2, s3)


def layer_bwd(dx, saved, lw, jobs=None):
    s1, s2, s3 = saved
    dx, g3 = ffn_bwd("ffn", dx, s3, lw["ffn2"], jobs)
    dx, g2 = mix_bwd("mix", dx, s2, lw["mix"], jobs)
    dx, g1 = ffn_bwd("ffn", dx, s1, lw["ffn1"], jobs)
    return dx, dict(ffn1=g1, mix=g2, ffn2=g3)


_O_BA = W_QKV + W_Z
_O_GLU = _O_BA + 2 * HEADS


_MIX_BLK = P_IN // N_BLK
_MIX_B1 = _O_BA - _MIX_BLK
assert _O_BA + HEADS == 2 * _MIX_BLK
BLOCKED = ("ffn1_w_in", "ffn2_w_in", "mix_w_in")


def _row(v):
    return v.reshape(1, -1).astype(F32)


def prep_ffn(wl, k):
    w_in = wl[k + "_w_in"].astype(BF16)
    if w_in.ndim == 2:
        w_in = jnp.transpose(w_in.reshape(w_in.shape[0], N_BLK, -1), (1, 0, 2))
    return dict(norm_pre=_row(wl[k + "_norm_pre"]), norm_post=_row(wl[k + "_norm_post"]), w_in=w_in,
                w_out=wl[k + "_w_out"].astype(BF16))


def prep_layer(wl):
    return dict(ffn1=prep_ffn(wl, "ffn1"), mix=prep_mix(wl), ffn2=prep_ffn(wl, "ffn2"))


def prep_mix(wl):
    row = _row
    bf = lambda v: v.astype(BF16)
    lanes8 = lambda v: jnp.zeros((1, LANES), F32).at[0, HEADS:2 * HEADS].set(v.astype(F32))
    mw = bf(wl["mix_w_in"])
    pad = jnp.zeros((D_MODEL, LANES - 2 * HEADS), BF16)
    if mw.ndim == 3:
        w_all = jnp.concatenate([mw[0], mw[1][:, :_MIX_B1], mw[2][:, HEADS:], mw[3], mw[1][:, _MIX_B1:],
                                 mw[2][:, :HEADS], pad], axis=1)
    else:
        w_all = jnp.concatenate([mw[:, :_O_BA], mw[:, _O_GLU:], mw[:, _O_BA:_O_GLU], pad], axis=1)
    return dict(norm_pre=row(wl["mix_norm_pre"]), norm_post=row(wl["mix_norm_post"]), w_all=w_all,
               conv_w=wl["gdn_conv_w"].astype(F32), alog=lanes8(wl["gdn_a_log"]), dtb=lanes8(wl["gdn_dt_bias"]),
               gdn_norm_w=row(wl["gdn_norm_w"]), gdn_w_o=bf(wl["gdn_w_o"]), pw1_b=row(wl["cnv_pw1_b"]),
               dw_w=wl["cnv_dw_w"].astype(F32), dw_b=row(wl["cnv_dw_b"]), ln_g=row(wl["cnv_ln_g"]),
               ln_b=row(wl["cnv_ln_b"]), cnv_w_o=bf(wl["cnv_w_o"]), b_o=row(wl["cnv_b_o"]), w_out=bf(wl["mix_w_out"]))


def unprep_grads(g):
    return {**unprep_ffn(g["ffn1"], "ffn1"), **unprep_mix(g["mix"]), **unprep_ffn(g["ffn2"], "ffn2")}


def unprep_ffn(g, k):
    blk = g["w_in"]
    return {k + "_norm_pre": g["norm_pre"][0], k + "_norm_post": g["norm_post"][0], k + "_w_in#blocks": blk,
            k + "_w_in": jnp.transpose(blk, (1, 0, 2)).reshape(blk.shape[1], N_BLK * blk.shape[2]),
            k + "_w_out": g["w_out"]}


def unprep_mix(m):
    dqkv, dz, dpa, dpg, dga, dgb, dba = m["w_all"]
    out = {}
    out["mix_w_in#blocks"] = jnp.stack([
        dqkv[:, :_MIX_BLK], jnp.concatenate([dqkv[:, _MIX_BLK:], dz, dba[:, :HEADS]], axis=1),
        jnp.concatenate([dba[:, HEADS:2 * HEADS], dpa, dpg[:, :_MIX_B1 - D_MODEL]], axis=1),
        jnp.concatenate([dpg[:, _MIX_B1 - D_MODEL:], dga, dgb], axis=1)])
    out.update(
        mix_norm_pre=m["norm_pre"][0], mix_norm_post=m["norm_post"][0],
        mix_w_in=jnp.concatenate([dqkv, dz, dba[:, :2 * HEADS], dpa, dpg, dga, dgb], axis=1),
        gdn_conv_w=m["conv_w"], gdn_a_log=m["alog"][0, HEADS:2 * HEADS], gdn_dt_bias=m["dtb"][0, HEADS:2 * HEADS],
        gdn_norm_w=m["gdn_norm_w"][0], gdn_w_o=m["gdn_w_o"], cnv_pw1_b=m["pw1_b"][0], cnv_dw_w=m["dw_w"],
        cnv_dw_b=m["dw_b"][0], cnv_ln_g=m["ln_g"][0], cnv_ln_b=m["ln_b"][0], cnv_w_o=m["cnv_w_o"], cnv_b_o=m["b_o"][0],
        mix_w_out=m["w_out"])
    return out


MESH = pl.DeviceIdType.MESH
ANY = pl.BlockSpec(memory_space=pl.ANY)
N_DEV = 8


def _pos():
    return lax.axis_index("x"), lax.axis_index("y"), lax.axis_index("c")


def _other_chips(x, y):
    return [(1 - x, y), (x, 1 - y), (1 - x, 1 - y)]


class Job:
    def __init__(self, ins, outs, n_sems, copies, aliases=None):
        self.ins, self.outs, self.n_sems, self.copies = ins, outs, n_sems, copies
        self.aliases = aliases or {}
        self.results = None
        self.host = None

    def scratch(self):
        return [pltpu.SemaphoreType.DMA((self.n_sems,)), pltpu.SemaphoreType.DMA((self.n_sems,))]

    def start(self, in_refs, out_refs, sems):
        for cp in self.copies(in_refs, out_refs, sems, False):
            cp.start()

    def finish(self, in_refs, out_refs, sems):
        for cp in self.copies(in_refs, out_refs, sems, True):
            cp.wait_recv()
        for cp in self.copies(in_refs, out_refs, sems, False):
            cp.wait_send()


def run_job(name, job):
    n_i, n_o = len(job.ins), len(job.outs)

    def body(*refs):
        in_refs, out_refs, sems = refs[:n_i], refs[n_i:n_i + n_o], refs[n_i + n_o:]
        job.start(in_refs, out_refs, sems)
        job.finish(in_refs, out_refs, sems)

    job.results = pl.pallas_call(
        body, name=name, in_specs=[ANY] * n_i, out_specs=[ANY] * n_o, out_shape=job.outs,
        input_output_aliases=job.aliases, scratch_shapes=job.scratch())(*job.ins)
    return job.results


def _halved(rows):
    return rows % 32 == 0


def job_gather_ici(shards):
    n = len(shards)

    def copies(in_refs, out_refs, sems, recv):
        x, y, c = _pos()
        b = 2 * x + y
        chips = _other_chips(x, y)
        cps = []
        for a in range(n):
            hr = shards[a].shape[0] // 2
            for j in range(3):
                blk = 2 * chips[j][0] + chips[j][1] if recv else b
                if _halved(shards[a].shape[0]):
                    src, dst = in_refs[a].at[pl.ds(c * hr, hr)], out_refs[a].at[blk, pl.ds(c * hr, hr)]
                else:
                    src, dst = in_refs[a], out_refs[a].at[blk]
                cps.append(pltpu.make_async_remote_copy(
                    src_ref=src, dst_ref=dst, send_sem=sems[0].at[3 * a + j], recv_sem=sems[1].at[3 * a + j],
                    device_id=(chips[j][0], chips[j][1], c), device_id_type=MESH))
        return cps

    return Job(list(shards), [S((N_BLK,) + w.shape, w.dtype) for w in shards], 3 * n, copies)


def job_gather_sibling(lands):
    idx = [a for a, w in enumerate(lands) if _halved(w.shape[1])]

    def copies(in_refs, out_refs, sems, recv):
        x, y, c = _pos()
        chips = _other_chips(x, y)
        half = 1 - c if recv else c
        cps = []
        for pos, a in enumerate(idx):
            hr = lands[a].shape[1] // 2
            for j in range(3):
                rows = out_refs[a].at[2 * chips[j][0] + chips[j][1], pl.ds(half * hr, hr)]
                cps.append(pltpu.make_async_remote_copy(
                    src_ref=rows, dst_ref=rows, send_sem=sems[0].at[3 * pos + j], recv_sem=sems[1].at[3 * pos + j],
                    device_id=(x, y, 1 - c), device_id_type=MESH))
        return cps

    return Job(list(lands), [S(w.shape, w.dtype) for w in lands], 3 * len(idx), copies,
               aliases={a: a for a in range(len(lands))})


def job_rs_chips(ps):
    n = len(ps)

    def copies(in_refs, out_refs, sems, recv):
        x, y, c = _pos()
        b = 2 * x + y
        chips = _other_chips(x, y)
        cps = []
        for k in range(n):
            for j in range(3):
                other = 2 * chips[j][0] + chips[j][1]
                src_blk, dst_slot = (b, other) if recv else (other, b)
                cps.append(pltpu.make_async_remote_copy(
                    src_ref=in_refs[k].at[src_blk], dst_ref=out_refs[k].at[dst_slot], send_sem=sems[0].at[3 * k + j],
                    recv_sem=sems[1].at[3 * k + j], device_id=(chips[j][0], chips[j][1], c), device_id_type=MESH))
        return cps

    return Job(list(ps), [S(p.shape, p.dtype) for p in ps], 3 * n, copies)


def rs_sibling(gs):
    n = len(gs)

    def body(*refs):
        g_refs, r_refs = refs[:n], refs[n:2 * n]
        send_sems, recv_sems = refs[2 * n:]
        x, y, c = _pos()

        def cp(k):
            hr = gs[k].shape[1] // 2
            return pltpu.make_async_remote_copy(
                src_ref=g_refs[k].at[:, pl.ds((1 - c) * hr, hr)], dst_ref=r_refs[k], send_sem=send_sems.at[k],
                recv_sem=recv_sems.at[k], device_id=(x, y, 1 - c), device_id_type=MESH)

        cps = [cp(k) for k in range(n)]
        for d in cps:
            d.start()
        for d in cps:
            d.wait_recv()
        for d in cps:
            d.wait_send()

    return pl.pallas_call(
        body, name="rs_sibling", in_specs=[ANY] * n, out_specs=[ANY] * n,
        out_shape=[S((N_BLK, g.shape[1] // 2, g.shape[2]), g.dtype) for g in gs],
        scratch_shapes=[pltpu.SemaphoreType.DMA((n,)), pltpu.SemaphoreType.DMA((n,))])(*gs)


def ag_sibling(fs):
    n = len(fs)

    def body(*refs):
        o_refs = refs[n:2 * n]
        send_sems, recv_sems = refs[2 * n:]
        x, y, c = _pos()

        def cp(k, half):
            hr = fs[k].shape[0] // 2
            rows = o_refs[k].at[pl.ds(half * hr, hr)]
            return pltpu.make_async_remote_copy(
                src_ref=rows, dst_ref=rows, send_sem=send_sems.at[k], recv_sem=recv_sems.at[k],
                device_id=(x, y, 1 - c), device_id_type=MESH)

        cps = [cp(k, c) for k in range(n)]
        for d in cps:
            d.start()
        for k in range(n):
            cp(k, 1 - c).wait_recv()
        for d in cps:
            d.wait_send()

    return pl.pallas_call(
        body, name="ag_sibling", in_specs=[ANY] * n, out_specs=[ANY] * n,
        out_shape=[S(f.shape, f.dtype) for f in fs], input_output_aliases={k: k for k in range(n)},
        scratch_shapes=[pltpu.SemaphoreType.DMA((n,)), pltpu.SemaphoreType.DMA((n,))])(*fs)


def allreduce_small(v):
    rows = v.shape[0]

    def body(v_ref, o_ref, buf, send_sems, recv_sems):
        x, y, c = _pos()
        me = 4 * x + 2 * y + c
        buf[me] = v_ref[...]

        def cp(d, slot):
            dx, dy, dc = (d >> 2) & 1, (d >> 1) & 1, d & 1
            peer = (1 - x if dx else x, 1 - y if dy else y, 1 - c if dc else c)
            return pltpu.make_async_remote_copy(
                src_ref=v_ref, dst_ref=buf.at[slot], send_sem=send_sems.at[d - 1], recv_sem=recv_sems.at[d - 1],
                device_id=peer, device_id_type=MESH)

        cps = [cp(d, me) for d in range(1, N_DEV)]
        for d in cps:
            d.start()
        for d in range(1, N_DEV):
            dx, dy, dc = (d >> 2) & 1, (d >> 1) & 1, d & 1
            src = 4 * (1 - x if dx else x) + 2 * (1 - y if dy else y) + (1 - c if dc else c)
            cp(d, src).wait_recv()
        for d in cps:
            d.wait_send()
        acc = buf[0]
        for s in range(1, N_DEV):
            acc = acc + buf[s]
        o_ref[...] = acc

    vm = pl.BlockSpec(memory_space=pltpu.VMEM)
    return pl.pallas_call(
        body, name="allreduce_small", in_specs=[vm], out_specs=vm, out_shape=S(v.shape, v.dtype),
        scratch_shapes=[pltpu.VMEM((N_DEV, rows, LANES), F32), pltpu.SemaphoreType.DMA((N_DEV - 1,)),
                        pltpu.SemaphoreType.DMA((N_DEV - 1,))])(v)


def _rows_tile(rows, cols, cap_bytes=1 << 20, mult=8):
    best = None
    for t in range(mult, rows + 1, mult):
        if rows % t == 0 and t * cols * 4 <= cap_bytes:
            best = t
    return best if best is not None else rows


def add_half(name, g, r, c_arr):
    _, hr, cols = r.shape
    tr = _rows_tile(hr, cols, mult=16)
    nb = hr // tr

    def body(c_ref, g_ref, r_ref, o_ref):
        o_ref[...] = (g_ref[...].astype(F32) + r_ref[...].astype(F32)).astype(o_ref.dtype)

    gs = pltpu.PrefetchScalarGridSpec(
        num_scalar_prefetch=1, grid=(N_BLK, nb),
        in_specs=[pl.BlockSpec((None, tr, cols), lambda b, i, cr: (b, cr[0] * nb + i, 0)),
                  pl.BlockSpec((None, tr, cols), lambda b, i, cr: (b, i, 0))],
        out_specs=pl.BlockSpec((None, tr, cols), lambda b, i, cr: (b, i, 0)))
    return pl.pallas_call(body, name=name, grid_spec=gs, out_shape=S(r.shape, BF16),
                          compiler_params=_cp(("parallel", "parallel")))(c_arr, g, r)


def sum_chips(name, r, own, cb_arr):
    _, hr, cols = r.shape
    tr = _rows_tile(hr, cols, mult=16)
    nb = hr // tr

    def body(cb_ref, *refs):
        o_ref = refs[N_BLK + 1]
        b = cb_ref[1]
        acc = None
        for s in range(N_BLK):
            term = jnp.where(b == s, refs[N_BLK][...], refs[s][...]).astype(F32)
            acc = term if acc is None else acc + term
        o_ref[...] = acc

    slot = lambda s: pl.BlockSpec((None, tr, cols), lambda i, cb: (jnp.where(cb[1] == s, (s + 1) % N_BLK, s), i, 0))
    gs = pltpu.PrefetchScalarGridSpec(
        num_scalar_prefetch=1, grid=(nb,),
        in_specs=[slot(s) for s in range(N_BLK)] + [pl.BlockSpec((None, tr, cols), lambda i, cb: (cb[1], i, 0))],
        out_specs=pl.BlockSpec((tr, cols), lambda i, cb: (cb[0] * nb + i, 0)))
    return pl.pallas_call(body, name=name, grid_spec=gs, out_shape=S((2 * hr, cols), F32),
                          compiler_params=_cp(("parallel",)))(cb_arr, *([r] * N_BLK), own)


def adamw(name, w, m, v, gs):
    rows, cols = w.shape
    two = len(gs) == 2
    span = rows // 2 if two else rows
    tr = _rows_tile(span, cols, 1 << 20)
    nb = span // tr

    def body(w_ref, m_ref, v_ref, *rest):
        g_refs, (go_ref, d_ref, mo_ref, vo_ref) = rest[:len(gs)], rest[len(gs):]
        if two:
            g = jnp.where(pl.program_id(0) < nb, g_refs[0][...], g_refs[1][...])
        else:
            g = g_refs[0][...]
        mn = ADAM_B1 * m_ref[...] + (1.0 - ADAM_B1) * g
        vn = ADAM_B2 * v_ref[...] + (1.0 - ADAM_B2) * jnp.square(g)
        m_hat = mn / (1.0 - ADAM_B1 ** ADAM_STEP)
        v_hat = vn / (1.0 - ADAM_B2 ** ADAM_STEP)
        go_ref[...] = g
        d_ref[...] = -ADAM_LR * (m_hat / (jnp.sqrt(v_hat) + ADAM_EPS) + ADAM_WD * w_ref[...])
        mo_ref[...] = mn
        vo_ref[...] = vn

    full = pl.BlockSpec((tr, cols), lambda i: (i, 0))
    if two:
        g_specs = [pl.BlockSpec((tr, cols), lambda i: (jnp.minimum(i, nb - 1), 0)),
                   pl.BlockSpec((tr, cols), lambda i: (jnp.maximum(i - nb, 0), 0))]
    else:
        g_specs = [full]
    return pl.pallas_call(
        body, name=name, grid=(2 * nb if two else nb,), in_specs=[full, full, full] + g_specs, out_specs=[full] * 4,
        out_shape=[S((rows, cols), F32)] * 4, compiler_params=_cp(("parallel",)))(w, m, v, *gs)


WEIGHTS = ["ffn1_norm_pre", "ffn1_norm_post", "ffn1_w_in", "ffn1_w_out", "mix_norm_pre", "mix_norm_post", "mix_w_in",
           "gdn_conv_w", "gdn_a_log", "gdn_dt_bias", "gdn_norm_w", "gdn_w_o", "cnv_pw1_b", "cnv_dw_w", "cnv_dw_b",
           "cnv_ln_g", "cnv_ln_b", "cnv_w_o", "cnv_b_o", "mix_w_out", "ffn2_norm_pre", "ffn2_norm_post", "ffn2_w_in",
           "ffn2_w_out"]
BIG = {"ffn1_w_in": True, "ffn1_w_out": False, "mix_w_in": True, "gdn_conv_w": True, "gdn_w_o": False,
       "cnv_dw_w": True, "cnv_w_o": False, "mix_w_out": False, "ffn2_w_in": True, "ffn2_w_out": False}
TINY = {"gdn_conv_w": (32, LANES), "cnv_dw_w": (64, LANES)}
SMALL = [n for n in WEIGHTS if n not in BIG]
SUB = {"ffn1": ["ffn1_w_in", "ffn1_w_out"], "ffn2": ["ffn2_w_in", "ffn2_w_out"],
       "mix": ["mix_w_in", "gdn_conv_w", "gdn_w_o", "cnv_dw_w", "cnv_w_o", "mix_w_out"]}
GATHER_ON_FFN_MIX = [("ffn_in", ["mix_w_in"]), ("ffn_out", SUB["mix"][1:])]
RS_ON_FFN_MIX = [("ffn_in_bx", ["mix_w_in"]), ("ffn_out_bx", SUB["mix"][1:])]


def _whole(name, blocks):
    if BIG[name]:
        return jnp.transpose(blocks, (1, 0, 2)).reshape(blocks.shape[1], N_BLK * blocks.shape[2])
    return blocks.reshape(N_BLK * blocks.shape[1], blocks.shape[2])


def _blocks(name, whole):
    r, cfull = whole.shape
    if BIG[name]:
        blk = jnp.transpose(whole.reshape(r, N_BLK, cfull // N_BLK), (1, 0, 2))
    else:
        blk = whole.reshape(N_BLK, r // N_BLK, cfull)
    if name in TINY:
        tr, tc = TINY[name]
        flat = blk.reshape(N_BLK, -1)
        blk = jnp.pad(flat, ((0, 0), (0, tr * tc - flat.shape[1]))).reshape(N_BLK, tr, tc)
    return blk.astype(GRAD_DT)


def _pack(parts):
    rows = []
    for p in parts:
        flat = p.reshape(-1).astype(F32)
        rows.append(jnp.pad(flat, (0, (-flat.shape[0]) % LANES)).reshape(-1, LANES))
    out = jnp.concatenate(rows, axis=0)
    return jnp.pad(out, ((0, (-out.shape[0]) % 8), (0, 0)))


def _unpack(packed, shapes):
    out, r = [], 0
    for shp in shapes:
        size = math.prod(shp)
        nr = -(-size // LANES)
        out.append(packed[r:r + nr].reshape(-1)[:size].reshape(shp))
        r += nr
    return out


def kernel(x, ffn1_norm_pre, ffn1_norm_post, ffn1_w_in, ffn1_w_out, mix_norm_pre, mix_norm_post, mix_w_in, gdn_conv_w, gdn_a_log, gdn_dt_bias, gdn_norm_w, gdn_w_o, cnv_pw1_b, cnv_dw_w, cnv_dw_b, cnv_ln_g, cnv_ln_b, cnv_w_o, cnv_b_o, mix_w_out, ffn2_norm_pre, ffn2_norm_post, ffn2_w_in, ffn2_w_out, loss_target, m_ffn1_norm_pre, m_ffn1_norm_post, m_ffn1_w_in, m_ffn1_w_out, m_mix_norm_pre, m_mix_norm_post, m_mix_w_in, m_gdn_conv_w, m_gdn_a_log, m_gdn_dt_bias, m_gdn_norm_w, m_gdn_w_o, m_cnv_pw1_b, m_cnv_dw_w, m_cnv_dw_b, m_cnv_ln_g, m_cnv_ln_b, m_cnv_w_o, m_cnv_b_o, m_mix_w_out, m_ffn2_norm_pre, m_ffn2_norm_post, m_ffn2_w_in, m_ffn2_w_out, v_ffn1_norm_pre, v_ffn1_norm_post, v_ffn1_w_in, v_ffn1_w_out, v_mix_norm_pre, v_mix_norm_post, v_mix_w_in, v_gdn_conv_w, v_gdn_a_log, v_gdn_dt_bias, v_gdn_norm_w, v_gdn_w_o, v_cnv_pw1_b, v_cnv_dw_w, v_cnv_dw_b, v_cnv_ln_g, v_cnv_ln_b, v_cnv_w_o, v_cnv_b_o, v_mix_w_out, v_ffn2_norm_pre, v_ffn2_norm_post, v_ffn2_w_in, v_ffn2_w_out):
    args = locals()
    wts = {n: args[n] for n in WEIGHTS}
    mom = {n: args["m_" + n] for n in WEIGHTS}
    var = {n: args["v_" + n] for n in WEIGHTS}
    big = list(BIG)

    mx, my, mc = _pos()
    mb = 2 * mx + my
    cb_arr = jnp.stack([mc, mb]).astype(jnp.int32)

    own = {(n, l): wts[n][l].astype(BF16) for n in big for l in range(DEPTH)}

    def planned(make_job, arrays, plan):
        jobs = []
        for host, names in plan:
            jb = make_job([arrays[n] for n in names])
            jb.host, jb.names = host, names
            jobs.append(jb)
        return jobs

    def landed(jobs):
        res = {}
        for jb in jobs:
            if jb.results is None:
                run_job("comm_alone", jb)
            res.update(zip(jb.names, jb.results))
        return res

    stages = [(l, s) for l in range(DEPTH) for s in ("ffn1", "mix", "ffn2")]
    carried_by_ffn = lambda s: GATHER_ON_FFN_MIX if s == "mix" else [("ffn_in", [s + "_w_in"]), ("ffn_out", [s + "_w_out"])]
    carried_by_mix = lambda s: [("mix_in", [s + "_w_in", s + "_w_out"])]

    def gather_jobs(l, s, carrier):
        plan = carried_by_mix(s) if carrier == "mix" else carried_by_ffn(s)
        return planned(job_gather_ici, {n: own[(n, l)] for n in SUB[s]}, plan)

    def sub_weights(l, s, jobs):
        lands = landed(jobs)
        names = SUB[s]
        lands = dict(zip(names, run_job("gather_sib", job_gather_sibling([lands[n] for n in names]))))
        wl = {}
        for n in names:
            blocks = lax.dynamic_update_index_in_dim(lands[n], own[(n, l)], mb, 0)
            wl[n] = blocks if n in BLOCKED else _whole(n, blocks)
        wl.update({n: wts[n][l] for n in SMALL})
        return prep_mix(wl) if s == "mix" else prep_ffn(wl, s)

    act = x[0]
    saved, weights = {}, {}
    jobs = planned(job_gather_ici, {n: own[(n, 0)] for n in SUB["ffn1"]}, [(None, SUB["ffn1"])])
    for i, (l, s) in enumerate(stages):
        weights[(l, s)] = sub_weights(l, s, jobs)
        jobs = gather_jobs(*stages[i + 1], "mix" if s == "mix" else "ffn") if i + 1 < len(stages) else []
        pending = list(jobs)
        fwd = mix_fwd if s == "mix" else ffn_fwd
        act, saved[(l, s)] = fwd("mix" if s == "mix" else "ffn", act, weights[(l, s)], pending)
    dx, loss = loss_head("loss", act, loss_target[0], _tm(act.shape[0]))

    def rs_jobs(l, s, g, carrier):
        if s == "mix":
            gw_s = unprep_mix(g)
        else:
            gw_s = unprep_ffn(g, s)
        small_grads[l].update({n: gw_s[n] for n in gw_s if n in SMALL})
        names = SUB[s]
        blocks = [gw_s[n + "#blocks"] if n in BLOCKED else _blocks(n, gw_s[n]) for n in names]
        parts = [add_half("add_half", b_, r, cb_arr) for b_, r in zip(blocks, rs_sibling(blocks))]
        partial_of.update({(n, l): p for n, p in zip(names, parts)})
        if carrier is None:
            plan = [(None, names)]
        elif carrier == "mix":
            plan = [("gintra_b", names)]
        elif s == "mix":
            plan = RS_ON_FFN_MIX
        else:
            plan = [("ffn_in_bx", [s + "_w_in"]), ("ffn_out_bx", [s + "_w_out"])]
        return planned(job_rs_chips, dict(zip(names, parts)), plan)

    small_grads = [{} for _ in range(DEPTH)]
    partial_of, chip_of = {}, {}
    jobs, jobs_key = [], None
    for i, (l, s) in enumerate(reversed(stages)):
        pending = list(jobs)
        bwd = mix_bwd if s == "mix" else ffn_bwd
        dx, g = bwd("mix" if s == "mix" else "ffn", dx, saved[(l, s)], weights[(l, s)], pending)
        if jobs:
            chip_of.update({(n, jobs_key): r for n, r in landed(jobs).items()})
        nxt = list(reversed(stages))[i + 1][1] if i + 1 < len(stages) else None
        jobs, jobs_key = rs_jobs(l, s, g, None if nxt is None else ("mix" if nxt == "mix" else "ffn")), l
    chip_of.update({(n, jobs_key): r for n, r in landed(jobs).items()})
    gw = small_grads

    small_shapes = [wts[n].shape for n in SMALL]
    packed = _pack([jnp.stack([gw[l][n] for l in range(DEPTH)]) for n in SMALL] + [loss])
    total = allreduce_small(packed)
    small_g = dict(zip(SMALL, _unpack(total, small_shapes)))
    loss_sum = total[sum(-(-math.prod(s) // LANES) for s in small_shapes), 0]

    keys = [(n, l) for l in range(DEPTH) for n in big]
    halves = [sum_chips("sum_chips", chip_of[k], partial_of[k], cb_arr) for k in keys]
    summed = dict(zip(keys, ag_sibling(halves)))

    out_g, out_d, out_m, out_v = {}, {}, {}, {}
    for n in big:
        shp = wts[n].shape
        gs = [summed[(n, l)] for l in range(DEPTH)]
        if n in TINY:
            gs = [jnp.concatenate([g.reshape(-1)[:shp[1] * shp[2]].reshape(shp[1], shp[2]) for g in gs], axis=0)]
        two_d = lambda a: a.reshape(DEPTH * shp[1], shp[2])
        res = adamw("adamw", two_d(wts[n]), two_d(mom[n]), two_d(var[n]), gs)
        out_g[n], out_d[n], out_m[n], out_v[n] = [r.reshape(shp) for r in res]

    pk = lambda d: _pack([d[n] for n in SMALL])
    res = adamw("adamw_small", pk(wts), pk(mom), pk(var), [pk(small_g)])
    for d, r in zip((out_g, out_d, out_m, out_v), res):
        d.update(dict(zip(SMALL, _unpack(r, small_shapes))))

    return (loss_sum, dx[None], *[out_g[n] for n in WEIGHTS], *[out_d[n] for n in WEIGHTS],
            *[out_m[n] for n in WEIGHTS], *[out_v[n] for n in WEIGHTS])
```

```python
import functools
import math

import jax
import jax.numpy as jnp
from jax import lax
from jax.experimental import pallas as pl
from jax.experimental.pallas import tpu as pltpu

F32, BF16 = jnp.float32, jnp.bfloat16
S = jax.ShapeDtypeStruct

D_MODEL = 1024
D_FF = 2816
HEADS = 8
DK = 128
CHUNK = 64
GDN_CONV = 4
CNV_K = 31
W_QKV = 3 * HEADS * DK
W_Z = HEADS * DK
W_GLU = 2 * D_MODEL
W_GATE = 2 * D_MODEL
P_IN = W_QKV + W_Z + 2 * HEADS + W_GLU + W_GATE
LANES = 128
P_ALL = W_QKV + W_Z + W_GLU + W_GATE + LANES
COL_Z = W_QKV // LANES
COL_GLU = (W_QKV + W_Z) // LANES
COL_GATE = (W_QKV + W_Z + W_GLU) // LANES
COL_BA = (W_QKV + W_Z + W_GLU + W_GATE) // LANES
RMS_EPS = 1e-6
LN_EPS = 1e-5
DEPTH = 2
N_BLK = 4
VMEM_LIMIT = 56 * 1024 * 1024
GRAD_DT = BF16

ADAM_LR, ADAM_B1, ADAM_B2, ADAM_EPS, ADAM_WD, ADAM_STEP = 0.001, 0.9, 0.999, 1e-08, 0.01, 10


def _cp(sem):
    return pltpu.CompilerParams(dimension_semantics=sem, vmem_limit_bytes=VMEM_LIMIT)


MM_VMEM_BUDGET = 36 * 1024 * 1024


def _mm_tiles(m, n, k_bytes_a, k_bytes_b, out_bytes, tn_fixed=None):
    best = None
    for tm in (1024, 512, 256, 128):
        if m % tm:
            continue
        for tn in ((tn_fixed,) if tn_fixed else (1024, 512, 640, 256, 384, 128)):
            if n % tn:
                continue
            need = 2 * (tm * k_bytes_a + tn * k_bytes_b + tm * tn * out_bytes)
            if need <= MM_VMEM_BUDGET and (best is None or tm * tn > best[0] * best[1]):
                best = (tm, tn)
    if best is None:
        raise ValueError((m, n, k_bytes_a, k_bytes_b))
    return best


def mm_nt_sum(name, parts, b):
    m, n = parts[0][0].shape[0], b.shape[0]
    k_total = sum(a.shape[1] for a, _ in parts)
    tm, tn = _mm_tiles(m, n, k_total * 2, k_total * 2, 4)
    n_p = len(parts)

    def body(*refs):
        o_ref = refs[2 * n_p]
        acc = None
        for a_ref, b_ref in zip(refs[:n_p], refs[n_p:2 * n_p]):
            t = lax.dot_general(a_ref[...], b_ref[...], (((1,), (1,)), ((), ())), preferred_element_type=F32)
            acc = t if acc is None else acc + t
        o_ref[...] = acc

    a_specs = [pl.BlockSpec((tm, a.shape[1]), lambda i, j: (i, 0)) for a, _ in parts]
    b_specs = [pl.BlockSpec((tn, a.shape[1]), functools.partial(lambda i, j, c: (j, c), c=col)) for a, col in parts]
    return pl.pallas_call(
        body, name=name, grid=(m // tm, n // tn), in_specs=a_specs + b_specs,
        out_specs=pl.BlockSpec((tm, tn), lambda i, j: (i, j)), out_shape=S((m, n), F32),
        compiler_params=_cp(("parallel", "parallel")))(*[a for a, _ in parts], *([b] * n_p))


def _take(jobs, host):
    for jb in jobs or []:
        if jb.host == host:
            jobs.remove(jb)
            return jb
    return None


def _pcall(body, job, *, name, grid, in_specs, out_specs, out_shape, scratch_shapes=(), semantics):
    in_specs, out_specs, out_shape, scratch_shapes = list(in_specs), list(out_specs), list(out_shape), list(scratch_shapes)
    if job is None:
        return lambda *args: pl.pallas_call(
            body, name=name, grid=grid, in_specs=in_specs, out_specs=out_specs, out_shape=out_shape,
            scratch_shapes=scratch_shapes, compiler_params=_cp(semantics))(*args)
    n_in, n_out, n_sc = len(in_specs), len(out_specs), len(scratch_shapes)
    n_i, n_o = len(job.ins), len(job.outs)

    def wrapped(*refs):
        cut = [n_in, n_i, n_out, n_o, n_sc]
        parts, pos = [], 0
        for c in cut:
            parts.append(refs[pos:pos + c])
            pos += c
        ins, j_in, outs, j_out, own = parts
        sems = refs[pos:]
        ids = [pl.program_id(d) for d in range(len(grid))]
        first = functools.reduce(lambda p, q: p & q, [i == 0 for i in ids])
        last = functools.reduce(lambda p, q: p & q, [i == g - 1 for i, g in zip(ids, grid)])

        @pl.when(first)
        def _():
            job.start(j_in, j_out, sems)

        body(*ins, *outs, *own)

        @pl.when(last)
        def _():
            job.finish(j_in, j_out, sems)

    def call(*args):
        res = pl.pallas_call(
            wrapped, name=name + "_c", grid=grid, in_specs=in_specs + [ANY] * n_i, out_specs=out_specs + [ANY] * n_o,
            out_shape=out_shape + list(job.outs),
            input_output_aliases={n_in + ki: n_out + ko for ki, ko in job.aliases.items()},
            scratch_shapes=scratch_shapes + job.scratch(),
            compiler_params=_cp(("arbitrary",) * len(grid)))(*args, *job.ins)
        job.results = res[n_out:]
        return res[:n_out]

    return call


def mm(name, a, b, ta=False, tb=False, out_dtype=F32, job=None, out_blocked=False):
    k = a.shape[0] if ta else a.shape[1]
    m = a.shape[1] if ta else a.shape[0]
    blocked = b.ndim == 3
    cb = b.shape[2] if blocked else None
    osz = jnp.dtype(out_dtype).itemsize
    if blocked and not tb:
        n = N_BLK * cb
        tm, tn = _mm_tiles(m, n, k * 2, k * 2, osz, tn_fixed=cb)
        b_spec = pl.BlockSpec((None, k, cb), lambda i, j: (j, 0, 0))
    elif blocked:
        n = b.shape[1]
        assert k == N_BLK * cb, (name, a.shape, b.shape)
        tm, tn = _mm_tiles(m, n, k * 2, k * 2, osz)
        b_spec = pl.BlockSpec((N_BLK, tn, cb), lambda i, j: (0, j, 0))
    else:
        n = b.shape[0] if tb else b.shape[1]
        assert k == (b.shape[1] if tb else b.shape[0]), (name, a.shape, b.shape)
        tm, tn = _mm_tiles(m, n, k * 2, k * 2, osz, tn_fixed=n // N_BLK if out_blocked else None)
        b_spec = pl.BlockSpec((tn, k), lambda i, j: (j, 0)) if tb else pl.BlockSpec((k, tn), lambda i, j: (0, j))
    a_spec = pl.BlockSpec((k, tm), lambda i, j: (0, i)) if ta else pl.BlockSpec((tm, k), lambda i, j: (i, 0))
    if out_blocked:
        o_spec, o_shape = pl.BlockSpec((None, tm, tn), lambda i, j: (j, i, 0)), S((N_BLK, m, tn), out_dtype)
    else:
        o_spec, o_shape = pl.BlockSpec((tm, tn), lambda i, j: (i, j)), S((m, n), out_dtype)
    dims = (((0 if ta else 1,), (1 if tb else 0,)), ((), ()))
    gm, gn = m // tm, n // tn

    def product(a_ref, b_ref):
        if blocked and tb:
            acc = None
            for q in range(N_BLK):
                t = lax.dot_general(a_ref[:, q * cb:(q + 1) * cb], b_ref[q], (((1,), (1,)), ((), ())),
                                    preferred_element_type=F32)
                acc = t if acc is None else acc + t
            return acc
        return lax.dot_general(a_ref[...], b_ref[...], dims, preferred_element_type=F32)

    def body(a_ref, b_ref, o_ref):
        o_ref[...] = product(a_ref, b_ref).astype(o_ref.dtype)

    return _pcall(body, job, name=name, grid=(gm, gn), in_specs=[a_spec, b_spec], out_specs=[o_spec],
                  out_shape=[o_shape], semantics=("parallel", "parallel"))(a, b)[0]


def ew_fwd(name, fn, grid, ins, outs):
    n_in = len(ins)

    def body(*refs):
        vals = [r[...].astype(F32) for r in refs[:n_in]]
        res = fn(pl.program_id(0), *vals)
        for r, v in zip(refs[n_in:], res):
            r[...] = v.astype(r.dtype)

    out = pl.pallas_call(
        body, name=name, grid=grid, in_specs=[s for _, s in ins], out_specs=[s for _, s in outs],
        out_shape=[sd for sd, _ in outs], compiler_params=_cp(("parallel", "parallel")))(*[a for a, _ in ins])
    return out


def ew_bwd(name, fn, grid, ins, cts, wrt, acc, add=None):
    n_in, n_ct, n_wrt, n_acc = len(ins), len(cts), len(wrt), len(acc)
    has_add = add is not None

    def body(*refs):
        in_refs = refs[:n_in]
        ct_refs = refs[n_in:n_in + n_ct]
        pos = n_in + n_ct
        add_ref = refs[pos] if has_add else None
        pos += 1 if has_add else 0
        wrt_refs = refs[pos:pos + n_wrt]
        acc_refs = refs[pos + n_wrt:pos + n_wrt + n_acc]
        col, tok = pl.program_id(0), pl.program_id(1)
        vals = [r[...].astype(F32) for r in in_refs]
        _, vjp = jax.vjp(lambda *a: fn(col, *a), *vals)
        grads = vjp(tuple(c[...].astype(F32) for c in ct_refs))
        for pos_w, ((idx, _, _), r) in enumerate(zip(wrt, wrt_refs)):
            g = grads[idx]
            if has_add and pos_w == 0:
                g = g + add_ref[...]
            r[...] = g.astype(r.dtype)
        for (idx, _, _, over_cols), r in zip(acc, acc_refs):
            first = (tok == 0) & (col == 0) if over_cols else tok == 0

            @pl.when(first)
            def _():
                r[...] = jnp.zeros_like(r)

            r[...] += grads[idx]

    arrays = [a for a, _ in ins] + [a for a, _ in cts] + ([add[0]] if has_add else [])
    in_specs = [s for _, s in ins] + [s for _, s in cts] + ([add[1]] if has_add else [])
    over_any = any(o for *_, o in acc)
    out = pl.pallas_call(
        body, name=name, grid=grid, in_specs=in_specs,
        out_specs=[s for _, _, s in wrt] + [s for _, _, s, _ in acc],
        out_shape=[sd for _, sd, _ in wrt] + [sd for _, sd, _, _ in acc],
        compiler_params=_cp(("arbitrary" if over_any else "parallel", "arbitrary")))(*arrays)
    return out


def _tok(width, col=0):
    return lambda tm: pl.BlockSpec((tm, width), lambda j, i: (i, col))


def _tokcol(off=0):
    return lambda tm: pl.BlockSpec((tm, LANES), lambda j, i: (i, off + j))


def _par(width, col=0):
    return pl.BlockSpec((1, width), lambda j, i: (0, col))


def _parcol(off=0):
    return pl.BlockSpec((1, LANES), lambda j, i: (0, off + j))


def _rms(x, w, eps=RMS_EPS):
    return x * lax.rsqrt(jnp.mean(x * x, axis=-1, keepdims=True) + eps) * w


def _silu(x):
    return x * jax.nn.sigmoid(x)


def fn_rms(col, x, w):
    return (_rms(x, w),)


def fn_swiglu(col, gate, up):
    return (_silu(gate) * up,)


def swiglu_bwd(name, u, da, tm):
    n_tok, f2 = u.shape
    f = f2 // 2

    def body(g_ref, up_ref, da_ref, o_ref):
        g, d = g_ref[...].astype(F32), da_ref[...].astype(F32)
        s = jax.nn.sigmoid(g)
        o_ref[:, :f] = (d * up_ref[...].astype(F32) * (s * (1.0 + g * (1.0 - s)))).astype(o_ref.dtype)
        o_ref[:, f:] = (d * (g * s)).astype(o_ref.dtype)

    half = lambda c: pl.BlockSpec((tm, f), lambda i: (i, c))
    return pl.pallas_call(
        body, name=name, grid=(n_tok // tm,), in_specs=[half(0), half(1), half(0)],
        out_specs=pl.BlockSpec((tm, f2), lambda i: (i, 0)), out_shape=S((n_tok, f2), BF16),
        compiler_params=_cp(("parallel",)))(u, u, da)


def fn_gdnout(col, o, z, nw):
    return (_rms(o, nw) * _silu(z),)


def fn_glu(col, a, g, ba, bg):
    return ((a + ba) * jax.nn.sigmoid(g + bg),)


def fn_lnsilu(col, h, g, b):
    mu = jnp.mean(h, axis=-1, keepdims=True)
    var = jnp.mean(jnp.square(h - mu), axis=-1, keepdims=True)
    return (_silu((h - mu) * lax.rsqrt(var + LN_EPS) * g + b),)


def fn_merge(col, ya, yb, ga, gb, bo):
    return (jax.nn.sigmoid(ga) * ya + jax.nn.sigmoid(gb) * (yb + bo),)


HALO = 32


def conv_fwd(name, x, col_off, n_ch, w, bias, tb):
    n_tok = x.shape[0]
    k = w.shape[0]
    nt = n_tok // tb

    def body(xp_ref, xc_ref, w_ref, *rest):
        if bias is not None:
            b_ref, o_ref, xs = rest
        else:
            o_ref, xs = rest
        i = pl.program_id(1)
        xs[0:HALO, :] = jnp.where(i == 0, 0.0, xp_ref[tb - HALO:tb, :].astype(F32))
        xs[HALO:HALO + tb, :] = xc_ref[...].astype(F32)
        acc = jnp.zeros((tb, LANES), F32)
        for j in range(k):
            s = k - 1 - j
            acc = acc + w_ref[j:j + 1, :] * xs[HALO - s:HALO - s + tb, :]
        if bias is not None:
            acc = acc + b_ref[...]
        o_ref[...] = acc

    in_specs = [pl.BlockSpec((tb, LANES), lambda j, i: (jnp.maximum(i - 1, 0), col_off + j)),
                pl.BlockSpec((tb, LANES), lambda j, i: (i, col_off + j)),
                pl.BlockSpec((k, LANES), lambda j, i: (0, j))]
    args = [x, x, w]
    if bias is not None:
        in_specs.append(pl.BlockSpec((1, LANES), lambda j, i: (0, j)))
        args.append(bias)
    return pl.pallas_call(
        body, name=name, grid=(n_ch // LANES, nt), in_specs=in_specs,
        out_specs=pl.BlockSpec((tb, LANES), lambda j, i: (i, j)), out_shape=S((n_tok, n_ch), F32),
        scratch_shapes=[pltpu.VMEM((HALO + tb, LANES), F32)],
        compiler_params=_cp(("parallel", "parallel")))(*args)


def conv_bwd(name, x, col_off, n_ch, w, dy, dx_dtype, tb):
    n_tok = x.shape[0]
    k = w.shape[0]
    nt = n_tok // tb

    def body(xp_ref, xc_ref, w_ref, dyc_ref, dyn_ref, dx_ref, dw_ref, db_ref, xs, dys):
        i = pl.program_id(1)
        xs[0:HALO, :] = jnp.where(i == 0, 0.0, xp_ref[tb - HALO:tb, :].astype(F32))
        xs[HALO:HALO + tb, :] = xc_ref[...].astype(F32)
        dyc = dyc_ref[...]
        dys[0:tb, :] = dyc
        dys[tb:tb + HALO, :] = jnp.where(i == nt - 1, 0.0, dyn_ref[0:HALO, :])

        @pl.when(i == 0)
        def _():
            dw_ref[...] = jnp.zeros_like(dw_ref)
            db_ref[...] = jnp.zeros_like(db_ref)

        acc = jnp.zeros((tb, LANES), F32)
        for j in range(k):
            s = k - 1 - j
            acc = acc + w_ref[j:j + 1, :] * dys[s:s + tb, :]
            dw_ref[j:j + 1, :] += jnp.sum(dyc * xs[HALO - s:HALO - s + tb, :], axis=0, keepdims=True)
        dx_ref[...] = acc.astype(dx_ref.dtype)
        db_ref[...] += jnp.sum(dyc, axis=0, keepdims=True)

    in_specs = [pl.BlockSpec((tb, LANES), lambda j, i: (jnp.maximum(i - 1, 0), col_off + j)),
                pl.BlockSpec((tb, LANES), lambda j, i: (i, col_off + j)),
                pl.BlockSpec((k, LANES), lambda j, i: (0, j)),
                pl.BlockSpec((tb, LANES), lambda j, i: (i, j)),
                pl.BlockSpec((tb, LANES), lambda j, i: (jnp.minimum(i + 1, nt - 1), j))]
    return pl.pallas_call(
        body, name=name, grid=(n_ch // LANES, nt), in_specs=in_specs,
        out_specs=[pl.BlockSpec((tb, LANES), lambda j, i: (i, j)),
                   pl.BlockSpec((k, LANES), lambda j, i: (0, j)),
                   pl.BlockSpec((1, LANES), lambda j, i: (0, j))],
        out_shape=[S((n_tok, n_ch), dx_dtype), S((k, n_ch), F32), S((1, n_ch), F32)],
        scratch_shapes=[pltpu.VMEM((HALO + tb, LANES), F32), pltpu.VMEM((tb + HALO, LANES), F32)],
        compiler_params=_cp(("parallel", "arbitrary")))(x, x, w, dy, dy)


def gdnconv_fwd(name, p, w, tb, job=None):
    n_tok = p.shape[0]
    k = w.shape[0]
    nt = n_tok // tb

    def body(xp_ref, xc_ref, w_ref, o_ref, xs):
        j, i = pl.program_id(0), pl.program_id(1)
        xs[0:HALO, :] = jnp.where(i == 0, 0.0, xp_ref[tb - HALO:tb, :].astype(F32))
        xs[HALO:HALO + tb, :] = xc_ref[...].astype(F32)
        c = jnp.zeros((tb, LANES), F32)
        for t in range(k):
            s = k - 1 - t
            c = c + w_ref[t:t + 1, :] * xs[HALO - s:HALO - s + tb, :]
        y = c * jax.nn.sigmoid(c)
        r = lax.rsqrt(jnp.sum(y * y, axis=-1, keepdims=True) + 1e-6) * jnp.where(j < HEADS, DK ** -0.5, 1.0)
        o_ref[...] = jnp.where(j < 2 * HEADS, y * r, y)

    return _pcall(
        body, job, name=name, grid=(W_QKV // LANES, nt),
        in_specs=[pl.BlockSpec((tb, LANES), lambda j, i: (jnp.maximum(i - 1, 0), j)),
                  pl.BlockSpec((tb, LANES), lambda j, i: (i, j)),
                  pl.BlockSpec((k, LANES), lambda j, i: (0, j))],
        out_specs=[pl.BlockSpec((tb, LANES), lambda j, i: (i, j))], out_shape=[S((n_tok, W_QKV), F32)],
        scratch_shapes=[pltpu.VMEM((HALO + tb, LANES), F32)], semantics=("parallel", "parallel"))(p, p, w)[0]


def gdnconv_bwd(name, p, w, dn, tb, job=None):
    n_tok = p.shape[0]
    k = w.shape[0]
    nt = n_tok // tb
    ext = tb + HALO

    def body(xp_ref, xc_ref, xn_ref, w_ref, dnc_ref, dnn_ref, dx_ref, dw_ref, xs, dns, dcs):
        j, i = pl.program_id(0), pl.program_id(1)
        xs[0:HALO, :] = jnp.where(i == 0, 0.0, xp_ref[tb - HALO:tb, :].astype(F32))
        xs[HALO:HALO + tb, :] = xc_ref[...].astype(F32)
        xs[HALO + tb:HALO + ext, :] = jnp.where(i == nt - 1, 0.0, xn_ref[0:HALO, :].astype(F32))
        dns[0:tb, :] = dnc_ref[...]
        dns[tb:ext, :] = jnp.where(i == nt - 1, 0.0, dnn_ref[0:HALO, :])

        @pl.when(i == 0)
        def _():
            dw_ref[...] = jnp.zeros_like(dw_ref)

        c = jnp.zeros((ext, LANES), F32)
        for t in range(k):
            s = k - 1 - t
            c = c + w_ref[t:t + 1, :] * xs[HALO - s:HALO - s + ext, :]
        d = dns[...]
        sg = jax.nn.sigmoid(c)
        y = c * sg
        r = lax.rsqrt(jnp.sum(y * y, axis=-1, keepdims=True) + 1e-6)
        scale = jnp.where(j < HEADS, DK ** -0.5, 1.0)
        dy_norm = scale * (d * r - y * (r * r * r) * jnp.sum(d * y, axis=-1, keepdims=True))
        dy = jnp.where(j < 2 * HEADS, dy_norm, d)
        dc = dy * (sg * (1.0 + c * (1.0 - sg)))
        dcs[...] = dc
        acc = jnp.zeros((tb, LANES), F32)
        for t in range(k):
            s = k - 1 - t
            acc = acc + w_ref[t:t + 1, :] * dcs[s:s + tb, :]
            dw_ref[t:t + 1, :] += jnp.sum(dcs[0:tb, :] * xs[HALO - s:HALO - s + tb, :], axis=0, keepdims=True)
        dx_ref[...] = acc.astype(dx_ref.dtype)

    cur = lambda j, i: (i, j)
    nxt = lambda j, i: (jnp.minimum(i + 1, nt - 1), j)
    return _pcall(
        body, job, name=name, grid=(W_QKV // LANES, nt),
        in_specs=[pl.BlockSpec((tb, LANES), lambda j, i: (jnp.maximum(i - 1, 0), j)),
                  pl.BlockSpec((tb, LANES), cur), pl.BlockSpec((tb, LANES), nxt),
                  pl.BlockSpec((k, LANES), lambda j, i: (0, j)),
                  pl.BlockSpec((tb, LANES), cur), pl.BlockSpec((tb, LANES), nxt)],
        out_specs=[pl.BlockSpec((tb, LANES), cur), pl.BlockSpec((k, LANES), lambda j, i: (0, j))],
        out_shape=[S((n_tok, W_QKV), BF16), S((k, W_QKV), F32)],
        scratch_shapes=[pltpu.VMEM((HALO + ext, LANES), F32), pltpu.VMEM((ext, LANES), F32),
                        pltpu.VMEM((ext, LANES), F32)],
        semantics=("parallel", "arbitrary"))(p, p, p, w, dn, dn)


GDN_GROUP = 4


def _dotb(a, b, ca, cb):
    return lax.dot_general(a.astype(BF16), b.astype(BF16), (((ca,), (cb,)), ((), ())), preferred_element_type=F32)


def _dot32(a, b, ca, cb):
    return lax.dot_general(a, b, (((ca,), (cb,)), ((), ())), preferred_element_type=F32,
                           precision=lax.Precision.HIGHEST)


def _dot3_many(xs, ys, ca, cb):
    xh = [x.astype(BF16) for x in xs]
    xl = [(x - h.astype(F32)).astype(BF16) for x, h in zip(xs, xh)]
    yh = [y.astype(BF16) for y in ys]
    yl = [(y - h.astype(F32)).astype(BF16) for y, h in zip(ys, yh)]
    dg = lambda p, q: lax.dot_general(p, q, (((ca,), (cb,)), ((), ())), preferred_element_type=F32)
    hh = [dg(p, q) for p, q in zip(xh, yh)]
    hl = [dg(p, q) for p, q in zip(xh, yl)]
    lh = [dg(p, q) for p, q in zip(xl, yh)]
    return [a + (b + c) for a, b, c in zip(hh, hl, lh)]


@jax.custom_vjp
def _mm3_many(xs, ys):
    return _dot3_many(xs, ys, 1, 0)


def _mm3_fwd(xs, ys):
    return _dot3_many(xs, ys, 1, 0), (xs, ys)


def _mm3_bwd(res, cts):
    xs, ys = res
    return _dot3_many(cts, ys, 1, 1), _dot3_many(xs, cts, 0, 0)


_mm3_many.defvjp(_mm3_fwd, _mm3_bwd)


@jax.custom_vjp
def _inv_unit_lower_many(mats):
    n = mats[0].shape[0]
    eye = (lax.broadcasted_iota(jnp.int32, (n, n), 0) == lax.broadcasted_iota(jnp.int32, (n, n), 1)).astype(F32)
    inv = [eye - a for a in mats]
    p = list(mats)
    for _ in range(int(math.log2(n)) - 1):
        p = _dot3_many(p, p, 1, 0)
        upd = _dot3_many(inv, p, 1, 0)
        inv = [i + u for i, u in zip(inv, upd)]
    return inv


def _inv_fwd(mats):
    t = _inv_unit_lower_many(mats)
    return t, t


def _inv_bwd(t, dt):
    x = _dot3_many(t, dt, 0, 0)
    return ([-y for y in _dot3_many(x, t, 1, 1)],)


_inv_unit_lower_many.defvjp(_inv_fwd, _inv_bwd)


@jax.custom_vjp
def _inv_given(mats, saved):
    return list(saved)


def _inv_given_fwd(mats, saved):
    return list(saved), list(saved)


def _inv_given_bwd(t, dt):
    return _inv_bwd(t, dt)[0], [jnp.zeros_like(s) for s in t]


_inv_given.defvjp(_inv_given_fwd, _inv_given_bwd)


def _softplus(x):
    return jnp.maximum(x, 0.0) + jnp.log(1.0 + jnp.exp(-jnp.abs(x)))


def _gdn_intra(qs, ks, vs, pbas, alog, dtb, tinv_saved=None):
    c = pbas[0].shape[0]
    row = lax.broadcasted_iota(jnp.int32, (c, c), 0)
    colm = lax.broadcasted_iota(jnp.int32, (c, c), 1)
    causal, strict = row >= colm, row > colm
    tril = causal.astype(F32)
    lane = lax.broadcasted_iota(jnp.int32, (1, LANES), 1)
    sub = lax.broadcasted_iota(jnp.int32, (LANES, 1), 0)
    last = (lax.broadcasted_iota(jnp.int32, (c, 1), 0) == c - 1).astype(F32)
    beta_all = [jax.nn.sigmoid(pb) for pb in pbas]
    g_all = [-jnp.exp(alog) * _softplus(pb + dtb) for pb in pbas]
    gc_all = [_dot32(tril, ga, 1, 0) for ga in g_all]
    gr_all = [_dot32(ga, tril, 0, 1) for ga in g_all]
    idx = [(g, h) for g in range(len(pbas)) for h in range(HEADS)]
    beta = [jnp.sum(beta_all[g] * (lane == h).astype(F32), axis=1, keepdims=True) for g, h in idx]
    gc = [jnp.sum(gc_all[g] * (lane == HEADS + h).astype(F32), axis=1, keepdims=True) for g, h in idx]
    gr = [jnp.sum(gr_all[g] * (sub == HEADS + h).astype(F32), axis=0, keepdims=True) for g, h in idx]
    decay = [jnp.where(causal, jnp.exp(jnp.where(causal, a - b, 0.0)), 0.0) for a, b in zip(gc, gr)]
    kk = [_dotb(k, k, 1, 1) for k in ks]
    a_mats = [jnp.where(strict, x * d * b, 0.0) for x, d, b in zip(kk, decay, beta)]
    tinv = _inv_unit_lower_many(a_mats) if tinv_saved is None else _inv_given(a_mats, tinv_saved)
    eg = [jnp.exp(a) for a in gc]
    g_last = [jnp.sum(a * last, axis=0, keepdims=True) for a in gc]
    us = _mm3_many(tinv, [v * b for v, b in zip(vs, beta)])
    ws = _mm3_many(tinv, [k * (b * e) for k, b, e in zip(ks, beta, eg)])
    qds = [q * e for q, e in zip(qs, eg)]
    kds = [k * jnp.exp(gl - a) for k, gl, a in zip(ks, g_last, gc)]
    qks = [_dotb(q, k, 1, 1) * d for q, k, d in zip(qs, ks, decay)]
    decs = [jnp.exp(gl) for gl in g_last]
    return us, ws, qds, kds, qks, decs, tinv


def _gdn_seq(us, ws, qds, kds, qks, decs, states):
    corr = [_dotb(w, st, 1, 0) for w, st in zip(ws, states)]
    from_state = [_dotb(qd, st, 1, 0) for qd, st in zip(qds, states)]
    v_new = [u - x for u, x in zip(us, corr)]
    intra = [_dotb(qk, vn, 1, 0) for qk, vn in zip(qks, v_new)]
    upd = [_dotb(kd, vn, 0, 0) for kd, vn in zip(kds, v_new)]
    outs = [a + b for a, b in zip(from_state, intra)]
    news = [st * d + x for st, d, x in zip(states, decs, upd)]
    return outs, news


def _heads(ref, rows=slice(None), base=0):
    return [ref[rows, (base + h) * DK:(base + h + 1) * DK].astype(F32) for h in range(HEADS)]


def _qk_heads(ref, rows=slice(None)):
    return [ref[rows, h * DK:h * DK + CHUNK].astype(F32) for h in range(HEADS)]


def _put_heads(ref, vals, rows=slice(None), base=0):
    for h in range(HEADS):
        ref[rows, (base + h) * DK:(base + h + 1) * DK] = vals[h].astype(ref.dtype)


def _put_qk(ref, vals, rows=slice(None)):
    for h in range(HEADS):
        ref[rows, h * DK:h * DK + CHUNK] = vals[h].astype(ref.dtype)
        ref[rows, h * DK + CHUNK:(h + 1) * DK] = jnp.zeros(vals[h].shape, ref.dtype)


def _group(n_chunks):
    return GDN_GROUP if n_chunks % GDN_GROUP == 0 else 1


def gdn_intra_fwd(name, qkvn, p, alog, dtb, job=None):
    n_tok = qkvn.shape[0]
    n = n_tok // CHUNK
    grp = _group(n)
    hd = HEADS * DK
    rb = grp * CHUNK

    def body(q_ref, k_ref, v_ref, pba_ref, al_ref, dt_ref, u_ref, w_ref, qd_ref, kd_ref, qk_ref, ti_ref, dec_ref):
        rows = [slice(g * CHUNK, (g + 1) * CHUNK) for g in range(grp)]
        cat = lambda ref: [t for r in rows for t in _heads(ref, r)]
        us, ws, qds, kds, qks, decs, tinv = _gdn_intra(cat(q_ref), cat(k_ref), cat(v_ref),
                                                       [pba_ref[r, :].astype(F32) for r in rows], al_ref[...], dt_ref[...])
        for g, r in enumerate(rows):
            part = slice(g * HEADS, (g + 1) * HEADS)
            _put_heads(u_ref, us[part], r)
            _put_heads(w_ref, ws[part], r)
            _put_heads(qd_ref, qds[part], r)
            _put_heads(kd_ref, kds[part], r)
            _put_qk(qk_ref, qks[part], r)
            _put_qk(ti_ref, tinv[part], r)
            for h in range(HEADS):
                dec_ref[g * HEADS + h:g * HEADS + h + 1, :] = jnp.broadcast_to(decs[g * HEADS + h], (1, LANES))

    blk = lambda c: pl.BlockSpec((rb, hd), lambda i: (i, c))
    par = pl.BlockSpec((1, LANES), lambda i: (0, 0))
    return _pcall(
        body, job, name=name, grid=(n // grp,),
        in_specs=[blk(0), blk(1), blk(2), pl.BlockSpec((rb, LANES), lambda i: (i, COL_BA)), par, par],
        out_specs=[blk(0)] * 6 + [pl.BlockSpec((grp * HEADS, LANES), lambda i: (i, 0))],
        out_shape=[S((n_tok, hd), F32)] + [S((n_tok, hd), BF16)] * 4 + [S((n_tok, hd), F32), S((n * HEADS, LANES), F32)],
        semantics=("parallel",))(qkvn, qkvn, qkvn, p, alog, dtb)


SEQ_GROUP = 4


def _decs_at(ref, g):
    return [ref[g * HEADS + h:g * HEADS + h + 1, 0:1] for h in range(HEADS)]


def gdn_seq_fwd(name, u, w, qd, kd, qk, dec):
    n_tok = u.shape[0]
    n = n_tok // CHUNK
    grp = SEQ_GROUP if n % SEQ_GROUP == 0 else 1
    hd = HEADS * DK

    def body(u_ref, w_ref, qd_ref, kd_ref, qk_ref, dec_ref, o_ref, s_ref, st):
        @pl.when(pl.program_id(0) == 0)
        def _():
            st[...] = jnp.zeros_like(st)

        states = [st[h * DK:(h + 1) * DK, :] for h in range(HEADS)]
        for g in range(grp):
            r = slice(g * CHUNK, (g + 1) * CHUNK)
            for h in range(HEADS):
                s_ref[g, h * DK:(h + 1) * DK, :] = states[h].astype(s_ref.dtype)
            outs, states = _gdn_seq(_heads(u_ref, r), _heads(w_ref, r), _heads(qd_ref, r), _heads(kd_ref, r),
                                    _qk_heads(qk_ref, r), _decs_at(dec_ref, g), states)
            _put_heads(o_ref, outs, r)
        for h in range(HEADS):
            st[h * DK:(h + 1) * DK, :] = states[h]

    blk = pl.BlockSpec((grp * CHUNK, hd), lambda i: (i, 0))
    return pl.pallas_call(
        body, name=name, grid=(n // grp,),
        in_specs=[blk] * 5 + [pl.BlockSpec((grp * HEADS, LANES), lambda i: (i, 0))],
        out_specs=[blk, pl.BlockSpec((grp, hd, DK), lambda i: (i, 0, 0))],
        out_shape=[S((n_tok, hd), F32), S((n, hd, DK), BF16)],
        scratch_shapes=[pltpu.VMEM((hd, DK), F32)],
        compiler_params=_cp(("arbitrary",)))(u, w, qd, kd, qk, dec)


def gdn_seq_bwd(name, u, w, qd, kd, qk, dec, states, do):
    n_tok = u.shape[0]
    n = n_tok // CHUNK
    grp = SEQ_GROUP if n % SEQ_GROUP == 0 else 1
    ns = n // grp
    hd = HEADS * DK

    def body(u_ref, w_ref, qd_ref, kd_ref, qk_ref, dec_ref, s_ref, do_ref,
             du_ref, dw_ref, dqd_ref, dkd_ref, dqk_ref, ddec_ref, dst):
        @pl.when(pl.program_id(0) == 0)
        def _():
            dst[...] = jnp.zeros_like(dst)

        d_news = [dst[h * DK:(h + 1) * DK, :] for h in range(HEADS)]
        for g in reversed(range(grp)):
            r = slice(g * CHUNK, (g + 1) * CHUNK)
            states = [s_ref[g, h * DK:(h + 1) * DK, :].astype(F32) for h in range(HEADS)]
            _, vjp = jax.vjp(_gdn_seq, _heads(u_ref, r), _heads(w_ref, r), _heads(qd_ref, r), _heads(kd_ref, r),
                             _qk_heads(qk_ref, r), _decs_at(dec_ref, g), states)
            du, dw, dqd, dkd, dqk, ddec, d_news = vjp((_heads(do_ref, r), d_news))
            _put_heads(du_ref, du, r)
            _put_heads(dw_ref, dw, r)
            _put_heads(dqd_ref, dqd, r)
            _put_heads(dkd_ref, dkd, r)
            _put_qk(dqk_ref, dqk, r)
            for h in range(HEADS):
                ddec_ref[g * HEADS + h:g * HEADS + h + 1, :] = jnp.broadcast_to(ddec[h], (1, LANES))
        for h in range(HEADS):
            dst[h * DK:(h + 1) * DK, :] = d_news[h]

    blk = pl.BlockSpec((grp * CHUNK, hd), lambda i: (ns - 1 - i, 0))
    dspec = pl.BlockSpec((grp * HEADS, LANES), lambda i: (ns - 1 - i, 0))
    return pl.pallas_call(
        body, name=name, grid=(ns,),
        in_specs=[blk] * 5 + [dspec, pl.BlockSpec((grp, hd, DK), lambda i: (ns - 1 - i, 0, 0)), blk],
        out_specs=[blk] * 5 + [dspec],
        out_shape=[S((n_tok, hd), F32)] * 5 + [S((n * HEADS, LANES), F32)],
        scratch_shapes=[pltpu.VMEM((hd, DK), F32)],
        compiler_params=_cp(("arbitrary",)))(u, w, qd, kd, qk, dec, states, do)


def gdn_intra_bwd(name, qkvn, p, alog, dtb, du, dw, dqd, dkd, dqk, ddec, tinv, job=None):
    n_tok = qkvn.shape[0]
    n = n_tok // CHUNK
    grp = _group(n)
    hd = HEADS * DK
    rb = grp * CHUNK

    def body(q_ref, k_ref, v_ref, pba_ref, al_ref, dt_ref, du_ref, dw_ref, dqd_ref, dkd_ref, dqk_ref, ddec_ref, ti_ref,
             dqkv_ref, dpba_ref, dal_ref, ddt_ref):
        @pl.when(pl.program_id(0) == 0)
        def _():
            dal_ref[...] = jnp.zeros_like(dal_ref)
            ddt_ref[...] = jnp.zeros_like(ddt_ref)

        rows = [slice(g * CHUNK, (g + 1) * CHUNK) for g in range(grp)]
        cat = lambda ref: [t for r in rows for t in _heads(ref, r)]
        kept = [t for r in rows for t in _qk_heads(ti_ref, r)]
        _, vjp = jax.vjp(lambda *a: _gdn_intra(*a, tinv_saved=kept)[:6], cat(q_ref), cat(k_ref), cat(v_ref),
                         [pba_ref[r, :].astype(F32) for r in rows], al_ref[...], dt_ref[...])
        cts = (cat(du_ref), cat(dw_ref), cat(dqd_ref), cat(dkd_ref), [t for r in rows for t in _qk_heads(dqk_ref, r)],
               [ddec_ref[i:i + 1, 0:1] for i in range(grp * HEADS)])
        dq, dk, dv, dpba, dal, ddt = vjp(cts)
        for g, r in enumerate(rows):
            part = slice(g * HEADS, (g + 1) * HEADS)
            _put_heads(dqkv_ref, dq[part], r, 0)
            _put_heads(dqkv_ref, dk[part], r, HEADS)
            _put_heads(dqkv_ref, dv[part], r, 2 * HEADS)
            dpba_ref[r, :] = dpba[g].astype(dpba_ref.dtype)
        dal_ref[...] += dal
        ddt_ref[...] += ddt

    blk = lambda c: pl.BlockSpec((rb, hd), lambda i: (i, c))
    par = pl.BlockSpec((1, LANES), lambda i: (0, 0))
    return _pcall(
        body, job, name=name, grid=(n // grp,),
        in_specs=[blk(0), blk(1), blk(2), pl.BlockSpec((rb, LANES), lambda i: (i, COL_BA)), par, par]
        + [blk(0)] * 5 + [pl.BlockSpec((grp * HEADS, LANES), lambda i: (i, 0)), blk(0)],
        out_specs=[pl.BlockSpec((rb, 3 * hd), lambda i: (i, 0)), pl.BlockSpec((rb, LANES), lambda i: (i, 0)), par, par],
        out_shape=[S((n_tok, 3 * hd), F32), S((n_tok, LANES), BF16), S((1, LANES), F32), S((1, LANES), F32)],
        semantics=("arbitrary",))(qkvn, qkvn, qkvn, p, alog, dtb, du, dw, dqd, dkd, dqk, ddec, tinv)


def loss_head(name, y, target, tm):
    n_tok, d = y.shape

    def body(y_ref, t_ref, dy_ref, l_ref):
        @pl.when(pl.program_id(0) == 0)
        def _():
            l_ref[...] = jnp.zeros_like(l_ref)

        e = y_ref[...] - t_ref[...]
        dy_ref[...] = e * (1.0 / d)
        l_ref[...] += jnp.sum(e * e, keepdims=True) * (0.5 / d)

    spec = pl.BlockSpec((tm, d), lambda i: (i, 0))
    return pl.pallas_call(
        body, name=name, grid=(n_tok // tm,), in_specs=[spec, spec],
        out_specs=[spec, pl.BlockSpec((1, 1), lambda i: (0, 0))], out_shape=[S((n_tok, d), F32), S((1, 1), F32)],
        compiler_params=_cp(("arbitrary",)))(y, target)


def _tm(n_tok):
    return min(512, n_tok)


def _gconv_tb(n_tok):
    return 1024 if n_tok % 1024 == 0 else _tm(n_tok)


def ffn_fwd(tag, x, w, jobs=None):
    n_tok = x.shape[0]
    tm = _tm(n_tok)
    g1 = (1, n_tok // tm)
    tD = _tok(D_MODEL)(tm)
    (h,) = ew_fwd(tag + "_rms", fn_rms, g1, [(x, tD), (w["norm_pre"], _par(D_MODEL))], [(S((n_tok, D_MODEL), BF16), tD)])
    u = mm(tag + "_in", h, w["w_in"], out_dtype=BF16, job=_take(jobs, "ffn_in"))
    tF = lambda c: _tok(D_FF, c)(tm)
    (a,) = ew_fwd(tag + "_swiglu", fn_swiglu, g1, [(u, tF(0)), (u, tF(1))], [(S((n_tok, D_FF), BF16), tF(0))])
    f = mm(tag + "_out", a, w["w_out"], job=_take(jobs, "ffn_out"))
    fn_res = lambda col, x_, f_, w_: (x_ + 0.5 * _rms(f_, w_),)
    (xo,) = ew_fwd(tag + "_res", fn_res, g1, [(x, tD), (f, tD), (w["norm_post"], _par(D_MODEL))],
                   [(S((n_tok, D_MODEL), F32), tD)])
    return xo, dict(x=x, h=h, u=u, a=a, f=f)


def ffn_bwd(tag, dxo, sv, w, jobs=None):
    n_tok = dxo.shape[0]
    tm = _tm(n_tok)
    g1 = (1, n_tok // tm)
    tD = _tok(D_MODEL)(tm)
    pD = _par(D_MODEL)
    fn_post = lambda col, f_, w_: (0.5 * _rms(f_, w_),)
    df, d_post = ew_bwd(tag + "_res_b", fn_post, g1, [(sv["f"], tD), (w["norm_post"], pD)], [(dxo, tD)],
                        [(0, S((n_tok, D_MODEL), BF16), tD)], [(1, S((1, D_MODEL), F32), pD, False)])
    da = mm(tag + "_out_bx", df, w["w_out"], tb=True, out_dtype=BF16, job=_take(jobs, "ffn_out_bx"))
    d_wout = mm(tag + "_out_bw", sv["a"], df, ta=True, out_dtype=GRAD_DT, job=_take(jobs, "ffn_out_bw"))
    tF = lambda c: _tok(D_FF, c)(tm)
    du = swiglu_bwd(tag + "_swiglu_b", sv["u"], da, tm)
    dh = mm(tag + "_in_bx", du, w["w_in"], tb=True, job=_take(jobs, "ffn_in_bx"))
    d_win = mm(tag + "_in_bw", sv["h"], du, ta=True, out_dtype=GRAD_DT, job=_take(jobs, "ffn_in_bw"), out_blocked=True)
    dx, d_pre = ew_bwd(tag + "_rms_b", fn_rms, g1, [(sv["x"], tD), (w["norm_pre"], pD)], [(dh, tD)],
                       [(0, S((n_tok, D_MODEL), F32), tD)], [(1, S((1, D_MODEL), F32), pD, False)], add=(dxo, tD))
    return dx, dict(norm_pre=d_pre, norm_post=d_post, w_in=d_win, w_out=d_wout)


def mix_fwd(tag, x, w, jobs=None):
    n_tok = x.shape[0]
    tm = _tm(n_tok)
    nt = n_tok // tm
    g1 = (1, nt)
    tD = _tok(D_MODEL)(tm)
    pD = _par(D_MODEL)
    (h,) = ew_fwd(tag + "_rms", fn_rms, g1, [(x, tD), (w["norm_pre"], pD)], [(S((n_tok, D_MODEL), BF16), tD)])
    p = mm(tag + "_in", h, w["w_all"], out_dtype=BF16, job=_take(jobs, "mix_in"))
    qkvn = gdnconv_fwd(tag + "_gconv", p, w["conv_w"], _gconv_tb(n_tok), job=_take(jobs, "gconv"))
    tC = _tokcol()(tm)
    *intra, tinv, dec = gdn_intra_fwd(tag + "_gintra", qkvn, p, w["alog"], w["dtb"], job=_take(jobs, "gintra"))
    intra.append(dec)
    o, states = gdn_seq_fwd(tag + "_gseq", *intra)
    (on,) = ew_fwd(tag + "_gout", fn_gdnout, (HEADS, nt),
                   [(o, tC), (p, _tokcol(COL_Z)(tm)), (w["gdn_norm_w"], _par(LANES))],
                   [(S((n_tok, D_MODEL), BF16), tC)])
    ya = mm(tag + "_go", on, w["gdn_w_o"], job=_take(jobs, "mix_small"))
    (hglu,) = ew_fwd(tag + "_glu", fn_glu, (D_MODEL // LANES, nt),
                     [(p, _tokcol(COL_GLU)(tm)), (p, _tokcol(COL_GLU + D_MODEL // LANES)(tm)),
                      (w["pw1_b"], _parcol(0)), (w["pw1_b"], _parcol(D_MODEL // LANES))],
                     [(S((n_tok, D_MODEL), F32), tC)])
    hc = conv_fwd(tag + "_cconv", hglu, 0, D_MODEL, w["dw_w"], w["dw_b"], tm)
    (hs,) = ew_fwd(tag + "_ln", fn_lnsilu, g1, [(hc, tD), (w["ln_g"], pD), (w["ln_b"], pD)],
                   [(S((n_tok, D_MODEL), BF16), tD)])
    yb = mm(tag + "_co", hs, w["cnv_w_o"], job=_take(jobs, "mix_small"))
    tG = lambda cb: pl.BlockSpec((tm, D_MODEL), lambda j, i: (i, cb))
    gcol = (W_QKV + W_Z + W_GLU) // D_MODEL
    (ym,) = ew_fwd(tag + "_merge", fn_merge, g1, [(ya, tD), (yb, tD), (p, tG(gcol)), (p, tG(gcol + 1)), (w["b_o"], pD)],
                   [(S((n_tok, D_MODEL), BF16), tD)])
    y = mm(tag + "_wo", ym, w["w_out"], job=_take(jobs, "mix_small"))
    fn_res = lambda col, x_, f_, w_: (x_ + _rms(f_, w_),)
    (xo,) = ew_fwd(tag + "_res", fn_res, g1, [(x, tD), (y, tD), (w["norm_post"], pD)], [(S((n_tok, D_MODEL), F32), tD)])
    sv = dict(x=x, h=h, p=p, qkvn=qkvn, intra=intra, tinv=tinv, states=states, o=o, on=on, ya=ya, hglu=hglu, hc=hc, hs=hs, yb=yb, ym=ym, y=y)
    return xo, sv


def mix_bwd(tag, dxo, sv, w, jobs=None):
    n_tok = dxo.shape[0]
    tm = _tm(n_tok)
    nt = n_tok // tm
    g1 = (1, nt)
    tD = _tok(D_MODEL)(tm)
    pD = _par(D_MODEL)
    tC = _tokcol()(tm)
    p = sv["p"]
    sD = lambda dt: S((n_tok, D_MODEL), dt)
    fn_post = lambda col, f_, w_: (_rms(f_, w_),)
    dy, d_post = ew_bwd(tag + "_res_b", fn_post, g1, [(sv["y"], tD), (w["norm_post"], pD)], [(dxo, tD)],
                        [(0, sD(BF16), tD)], [(1, S((1, D_MODEL), F32), pD, False)])
    dym = mm(tag + "_wo_bx", dy, w["w_out"], tb=True, job=_take(jobs, "mix_small"))
    d_wout = mm(tag + "_wo_bw", sv["ym"], dy, ta=True, out_dtype=GRAD_DT, job=_take(jobs, "mix_small"))
    tG = lambda cb: pl.BlockSpec((tm, D_MODEL), lambda j, i: (i, cb))
    gcol = (W_QKV + W_Z + W_GLU) // D_MODEL
    dya, dyb, dga, dgb, d_bo = ew_bwd(
        tag + "_merge_b", fn_merge, g1, [(sv["ya"], tD), (sv["yb"], tD), (p, tG(gcol)), (p, tG(gcol + 1)), (w["b_o"], pD)],
        [(dym, tD)], [(0, sD(BF16), tD), (1, sD(BF16), tD), (2, sD(BF16), tD), (3, sD(BF16), tD)],
        [(4, S((1, D_MODEL), F32), pD, False)])
    dhs = mm(tag + "_co_bx", dyb, w["cnv_w_o"], tb=True, job=_take(jobs, "mix_small"))
    d_cwo = mm(tag + "_co_bw", sv["hs"], dyb, ta=True, out_dtype=GRAD_DT, job=_take(jobs, "mix_small"))
    dhc, d_lng, d_lnb = ew_bwd(tag + "_ln_b", fn_lnsilu, g1, [(sv["hc"], tD), (w["ln_g"], pD), (w["ln_b"], pD)], [(dhs, tD)],
                               [(0, sD(F32), tD)], [(1, S((1, D_MODEL), F32), pD, False), (2, S((1, D_MODEL), F32), pD, False)])
    dhglu, d_dww, d_dwb = conv_bwd(tag + "_cconv_b", sv["hglu"], 0, D_MODEL, w["dw_w"], dhc, F32, tm)
    nc = D_MODEL // LANES
    dpa, dpg, d_ba, d_bg = ew_bwd(
        tag + "_glu_b", fn_glu, (nc, nt),
        [(p, _tokcol(COL_GLU)(tm)), (p, _tokcol(COL_GLU + nc)(tm)), (w["pw1_b"], _parcol(0)), (w["pw1_b"], _parcol(nc))],
        [(dhglu, tC)], [(0, sD(BF16), tC), (1, sD(BF16), tC)],
        [(2, S((1, D_MODEL), F32), _parcol(0), False), (3, S((1, D_MODEL), F32), _parcol(0), False)])
    don = mm(tag + "_go_bx", dya, w["gdn_w_o"], tb=True, job=_take(jobs, "mix_small"))
    d_gwo = mm(tag + "_go_bw", sv["on"], dya, ta=True, out_dtype=GRAD_DT, job=_take(jobs, "mix_small"))
    do, dz, d_gnw = ew_bwd(tag + "_gout_b", fn_gdnout, (HEADS, nt),
                           [(sv["o"], tC), (p, _tokcol(COL_Z)(tm)), (w["gdn_norm_w"], _par(LANES))], [(don, tC)],
                           [(0, sD(F32), tC), (1, sD(BF16), tC)], [(2, S((1, LANES), F32), _par(LANES), True)])
    d_intra = gdn_seq_bwd(tag + "_gseq_b", *sv["intra"], sv["states"], do)
    dqkvn, dpba, d_alog, d_dtb = gdn_intra_bwd(tag + "_gintra_b", sv["qkvn"], p, w["alog"], w["dtb"], *d_intra,
                                               sv["tinv"], job=_take(jobs, "gintra_b"))
    dqkv, d_convw = gdnconv_bwd(tag + "_gconv_b", p, w["conv_w"], dqkvn, _gconv_tb(n_tok), job=_take(jobs, "gconv_b"))
    nd = D_MODEL // LANES
    pieces = [(dqkv, 0, 0), (dz, W_QKV // D_MODEL, COL_Z), (dpa, COL_GLU // nd, COL_GLU), (dpg, COL_GLU // nd + 1, COL_GLU + nd),
              (dga, COL_GATE // nd, COL_GATE), (dgb, COL_GATE // nd + 1, COL_GATE + nd), (dpba, COL_BA, COL_BA)]
    dh = mm_nt_sum(tag + "_in_bx", [(a, blk) for a, blk, _ in pieces], w["w_all"])
    d_wall = [mm(tag + "_in_bw", sv["h"], a, ta=True, out_dtype=GRAD_DT, job=_take(jobs, "mix_small"))
              for a, _, _ in pieces]
    dx, d_pre = ew_bwd(tag + "_rms_b", fn_rms, g1, [(sv["x"], tD), (w["norm_pre"], pD)], [(dh, tD)],
                       [(0, sD(F32), tD)], [(1, S((1, D_MODEL), F32), pD, False)], add=(dxo, tD))
    grads = dict(norm_pre=d_pre, norm_post=d_post, w_all=d_wall, conv_w=d_convw, alog=d_alog, dtb=d_dtb,
                 gdn_norm_w=d_gnw, gdn_w_o=d_gwo, pw1_b=jnp.concatenate([d_ba, d_bg], axis=1), dw_w=d_dww,
                 dw_b=d_dwb, ln_g=d_lng, ln_b=d_lnb, cnv_w_o=d_cwo, b_o=d_bo, w_out=d_wout)
    return dx, grads


def local_step(x, target, layers):
    saved = []
    for lw in layers:
        x, sv = layer_fwd(x, lw)
        saved.append(sv)
    dx, loss = loss_head("loss", x, target, _tm(x.shape[0]))
    grads = [None] * len(layers)
    for i in reversed(range(len(layers))):
        dx, grads[i] = layer_bwd(dx, saved[i], layers[i])
    return loss, dx, grads


def layer_fwd(x, lw, jobs=None):
    x, s1 = ffn_fwd("ffn", x, lw["ffn1"], jobs)
    x, s2 = mix_fwd("mix", x, lw["mix"], jobs)
    x, s3 = ffn_fwd("ffn", x, lw["ffn2"], jobs)
    return x, (s1, s2, s3)


def layer_bwd(dx, saved, lw, jobs=None):
    s1, s2, s3 = saved
    dx, g3 = ffn_bwd("ffn", dx, s3, lw["ffn2"], jobs)
    dx, g2 = mix_bwd("mix", dx, s2, lw["mix"], jobs)
    dx, g1 = ffn_bwd("ffn", dx, s1, lw["ffn1"], jobs)
    return dx, dict(ffn1=g1, mix=g2, ffn2=g3)


_O_BA = W_QKV + W_Z
_O_GLU = _O_BA + 2 * HEADS


_MIX_BLK = P_IN // N_BLK
_MIX_B1 = _O_BA - _MIX_BLK
assert _O_BA + HEADS == 2 * _MIX_BLK
BLOCKED = ("ffn1_w_in", "ffn2_w_in", "mix_w_in")


def _row(v):
    return v.reshape(1, -1).astype(F32)


def prep_ffn(wl, k):
    w_in = wl[k + "_w_in"].astype(BF16)
    if w_in.ndim == 2:
        w_in = jnp.transpose(w_in.reshape(w_in.shape[0], N_BLK, -1), (1, 0, 2))
    return dict(norm_pre=_row(wl[k + "_norm_pre"]), norm_post=_row(wl[k + "_norm_post"]), w_in=w_in,
                w_out=wl[k + "_w_out"].astype(BF16))


def prep_layer(wl):
    return dict(ffn1=prep_ffn(wl, "ffn1"), mix=prep_mix(wl), ffn2=prep_ffn(wl, "ffn2"))


def prep_mix(wl):
    row = _row
    bf = lambda v: v.astype(BF16)
    lanes8 = lambda v: jnp.zeros((1, LANES), F32).at[0, HEADS:2 * HEADS].set(v.astype(F32))
    mw = bf(wl["mix_w_in"])
    pad = jnp.zeros((D_MODEL, LANES - 2 * HEADS), BF16)
    if mw.ndim == 3:
        w_all = jnp.concatenate([mw[0], mw[1][:, :_MIX_B1], mw[2][:, HEADS:], mw[3], mw[1][:, _MIX_B1:],
                                 mw[2][:, :HEADS], pad], axis=1)
    else:
        w_all = jnp.concatenate([mw[:, :_O_BA], mw[:, _O_GLU:], mw[:, _O_BA:_O_GLU], pad], axis=1)
    return dict(norm_pre=row(wl["mix_norm_pre"]), norm_post=row(wl["mix_norm_post"]), w_all=w_all,
               conv_w=wl["gdn_conv_w"].astype(F32), alog=lanes8(wl["gdn_a_log"]), dtb=lanes8(wl["gdn_dt_bias"]),
               gdn_norm_w=row(wl["gdn_norm_w"]), gdn_w_o=bf(wl["gdn_w_o"]), pw1_b=row(wl["cnv_pw1_b"]),
               dw_w=wl["cnv_dw_w"].astype(F32), dw_b=row(wl["cnv_dw_b"]), ln_g=row(wl["cnv_ln_g"]),
               ln_b=row(wl["cnv_ln_b"]), cnv_w_o=bf(wl["cnv_w_o"]), b_o=row(wl["cnv_b_o"]), w_out=bf(wl["mix_w_out"]))


def unprep_grads(g):
    return {**unprep_ffn(g["ffn1"], "ffn1"), **unprep_mix(g["mix"]), **unprep_ffn(g["ffn2"], "ffn2")}


def unprep_ffn(g, k):
    blk = g["w_in"]
    return {k + "_norm_pre": g["norm_pre"][0], k + "_norm_post": g["norm_post"][0], k + "_w_in#blocks": blk,
            k + "_w_in": jnp.transpose(blk, (1, 0, 2)).reshape(blk.shape[1], N_BLK * blk.shape[2]),
            k + "_w_out": g["w_out"]}


def unprep_mix(m):
    dqkv, dz, dpa, dpg, dga, dgb, dba = m["w_all"]
    out = {}
    out["mix_w_in#blocks"] = jnp.stack([
        dqkv[:, :_MIX_BLK], jnp.concatenate([dqkv[:, _MIX_BLK:], dz, dba[:, :HEADS]], axis=1),
        jnp.concatenate([dba[:, HEADS:2 * HEADS], dpa, dpg[:, :_MIX_B1 - D_MODEL]], axis=1),
        jnp.concatenate([dpg[:, _MIX_B1 - D_MODEL:], dga, dgb], axis=1)])
    out.update(
        mix_norm_pre=m["norm_pre"][0], mix_norm_post=m["norm_post"][0],
        mix_w_in=jnp.concatenate([dqkv, dz, dba[:, :2 * HEADS], dpa, dpg, dga, dgb], axis=1),
        gdn_conv_w=m["conv_w"], gdn_a_log=m["alog"][0, HEADS:2 * HEADS], gdn_dt_bias=m["dtb"][0, HEADS:2 * HEADS],
        gdn_norm_w=m["gdn_norm_w"][0], gdn_w_o=m["gdn_w_o"], cnv_pw1_b=m["pw1_b"][0], cnv_dw_w=m["dw_w"],
        cnv_dw_b=m["dw_b"][0], cnv_ln_g=m["ln_g"][0], cnv_ln_b=m["ln_b"][0], cnv_w_o=m["cnv_w_o"], cnv_b_o=m["b_o"][0],
        mix_w_out=m["w_out"])
    return out


MESH = pl.DeviceIdType.MESH
ANY = pl.BlockSpec(memory_space=pl.ANY)
N_DEV = 8


def _pos():
    return lax.axis_index("x"), lax.axis_index("y"), lax.axis_index("c")


def _other_chips(x, y):
    return [(1 - x, y), (x, 1 - y), (1 - x, 1 - y)]


class Job:
    def __init__(self, ins, outs, n_sems, copies, aliases=None):
        self.ins, self.outs, self.n_sems, self.copies = ins, outs, n_sems, copies
        self.aliases = aliases or {}
        self.results = None
        self.host = None

    def scratch(self):
        return [pltpu.SemaphoreType.DMA((self.n_sems,)), pltpu.SemaphoreType.DMA((self.n_sems,))]

    def start(self, in_refs, out_refs, sems):
        for cp in self.copies(in_refs, out_refs, sems, False):
            cp.start()

    def finish(self, in_refs, out_refs, sems):
        for cp in self.copies(in_refs, out_refs, sems, True):
            cp.wait_recv()
        for cp in self.copies(in_refs, out_refs, sems, False):
            cp.wait_send()


def run_job(name, job):
    n_i, n_o = len(job.ins), len(job.outs)

    def body(*refs):
        in_refs, out_refs, sems = refs[:n_i], refs[n_i:n_i + n_o], refs[n_i + n_o:]
        job.start(in_refs, out_refs, sems)
        job.finish(in_refs, out_refs, sems)

    job.results = pl.pallas_call(
        body, name=name, in_specs=[ANY] * n_i, out_specs=[ANY] * n_o, out_shape=job.outs,
        input_output_aliases=job.aliases, scratch_shapes=job.scratch())(*job.ins)
    return job.results


def _halved(rows):
    return rows % 32 == 0


def job_gather_ici(shards):
    n = len(shards)

    def copies(in_refs, out_refs, sems, recv):
        x, y, c = _pos()
        b = 2 * x + y
        chips = _other_chips(x, y)
        cps = []
        for a in range(n):
            hr = shards[a].shape[0] // 2
            for j in range(3):
                blk = 2 * chips[j][0] + chips[j][1] if recv else b
                if _halved(shards[a].shape[0]):
                    src, dst = in_refs[a].at[pl.ds(c * hr, hr)], out_refs[a].at[blk, pl.ds(c * hr, hr)]
                else:
                    src, dst = in_refs[a], out_refs[a].at[blk]
                cps.append(pltpu.make_async_remote_copy(
                    src_ref=src, dst_ref=dst, send_sem=sems[0].at[3 * a + j], recv_sem=sems[1].at[3 * a + j],
                    device_id=(chips[j][0], chips[j][1], c), device_id_type=MESH))
        return cps

    return Job(list(shards), [S((N_BLK,) + w.shape, w.dtype) for w in shards], 3 * n, copies)


def job_gather_sibling(lands):
    idx = [a for a, w in enumerate(lands) if _halved(w.shape[1])]

    def copies(in_refs, out_refs, sems, recv):
        x, y, c = _pos()
        chips = _other_chips(x, y)
        half = 1 - c if recv else c
        cps = []
        for pos, a in enumerate(idx):
            hr = lands[a].shape[1] // 2
            for j in range(3):
                rows = out_refs[a].at[2 * chips[j][0] + chips[j][1], pl.ds(half * hr, hr)]
                cps.append(pltpu.make_async_remote_copy(
                    src_ref=rows, dst_ref=rows, send_sem=sems[0].at[3 * pos + j], recv_sem=sems[1].at[3 * pos + j],
                    device_id=(x, y, 1 - c), device_id_type=MESH))
        return cps

    return Job(list(lands), [S(w.shape, w.dtype) for w in lands], 3 * len(idx), copies,
               aliases={a: a for a in range(len(lands))})


def job_rs_chips(ps):
    n = len(ps)

    def copies(in_refs, out_refs, sems, recv):
        x, y, c = _pos()
        b = 2 * x + y
        chips = _other_chips(x, y)
        cps = []
        for k in range(n):
            for j in range(3):
                other = 2 * chips[j][0] + chips[j][1]
                src_blk, dst_slot = (b, other) if recv else (other, b)
                cps.append(pltpu.make_async_remote_copy(
                    src_ref=in_refs[k].at[src_blk], dst_ref=out_refs[k].at[dst_slot], send_sem=sems[0].at[3 * k + j],
                    recv_sem=sems[1].at[3 * k + j], device_id=(chips[j][0], chips[j][1], c), device_id_type=MESH))
        return cps

    return Job(list(ps), [S(p.shape, p.dtype) for p in ps], 3 * n, copies)


def rs_sibling(gs):
    n = len(gs)

    def body(*refs):
        g_refs, r_refs = refs[:n], refs[n:2 * n]
        send_sems, recv_sems = refs[2 * n:]
        x, y, c = _pos()

        def cp(k):
            hr = gs[k].shape[1] // 2
            return pltpu.make_async_remote_copy(
                src_ref=g_refs[k].at[:, pl.ds((1 - c) * hr, hr)], dst_ref=r_refs[k], send_sem=send_sems.at[k],
                recv_sem=recv_sems.at[k], device_id=(x, y, 1 - c), device_id_type=MESH)

        cps = [cp(k) for k in range(n)]
        for d in cps:
            d.start()
        for d in cps:
            d.wait_recv()
        for d in cps:
            d.wait_send()

    return pl.pallas_call(
        body, name="rs_sibling", in_specs=[ANY] * n, out_specs=[ANY] * n,
        out_shape=[S((N_BLK, g.shape[1] // 2, g.shape[2]), g.dtype) for g in gs],
        scratch_shapes=[pltpu.SemaphoreType.DMA((n,)), pltpu.SemaphoreType.DMA((n,))])(*gs)


def ag_sibling(fs):
    n = len(fs)

    def body(*refs):
        o_refs = refs[n:2 * n]
        send_sems, recv_sems = refs[2 * n:]
        x, y, c = _pos()

        def cp(k, half):
            hr = fs[k].shape[0] // 2
            rows = o_refs[k].at[pl.ds(half * hr, hr)]
            return pltpu.make_async_remote_copy(
                src_ref=rows, dst_ref=rows, send_sem=send_sems.at[k], recv_sem=recv_sems.at[k],
                device_id=(x, y, 1 - c), device_id_type=MESH)

        cps = [cp(k, c) for k in range(n)]
        for d in cps:
            d.start()
        for k in range(n):
            cp(k, 1 - c).wait_recv()
        for d in cps:
            d.wait_send()

    return pl.pallas_call(
        body, name="ag_sibling", in_specs=[ANY] * n, out_specs=[ANY] * n,
        out_shape=[S(f.shape, f.dtype) for f in fs], input_output_aliases={k: k for k in range(n)},
        scratch_shapes=[pltpu.SemaphoreType.DMA((n,)), pltpu.SemaphoreType.DMA((n,))])(*fs)


def allreduce_small(v):
    rows = v.shape[0]

    def body(v_ref, o_ref, buf, send_sems, recv_sems):
        x, y, c = _pos()
        me = 4 * x + 2 * y + c
        buf[me] = v_ref[...]

        def cp(d, slot):
            dx, dy, dc = (d >> 2) & 1, (d >> 1) & 1, d & 1
            peer = (1 - x if dx else x, 1 - y if dy else y, 1 - c if dc else c)
            return pltpu.make_async_remote_copy(
                src_ref=v_ref, dst_ref=buf.at[slot], send_sem=send_sems.at[d - 1], recv_sem=recv_sems.at[d - 1],
                device_id=peer, device_id_type=MESH)

        cps = [cp(d, me) for d in range(1, N_DEV)]
        for d in cps:
            d.start()
        for d in range(1, N_DEV):
            dx, dy, dc = (d >> 2) & 1, (d >> 1) & 1, d & 1
            src = 4 * (1 - x if dx else x) + 2 * (1 - y if dy else y) + (1 - c if dc else c)
            cp(d, src).wait_recv()
        for d in cps:
            d.wait_send()
        acc = buf[0]
        for s in range(1, N_DEV):
            acc = acc + buf[s]
        o_ref[...] = acc

    vm = pl.BlockSpec(memory_space=pltpu.VMEM)
    return pl.pallas_call(
        body, name="allreduce_small", in_specs=[vm], out_specs=vm, out_shape=S(v.shape, v.dtype),
        scratch_shapes=[pltpu.VMEM((N_DEV, rows, LANES), F32), pltpu.SemaphoreType.DMA((N_DEV - 1,)),
                        pltpu.SemaphoreType.DMA((N_DEV - 1,))])(v)


def _rows_tile(rows, cols, cap_bytes=1 << 20, mult=8):
    best = None
    for t in range(mult, rows + 1, mult):
        if rows % t == 0 and t * cols * 4 <= cap_bytes:
            best = t
    return best if best is not None else rows


def add_half(name, g, r, c_arr):
    _, hr, cols = r.shape
    tr = _rows_tile(hr, cols, mult=16)
    nb = hr // tr

    def body(c_ref, g_ref, r_ref, o_ref):
        o_ref[...] = (g_ref[...].astype(F32) + r_ref[...].astype(F32)).astype(o_ref.dtype)

    gs = pltpu.PrefetchScalarGridSpec(
        num_scalar_prefetch=1, grid=(N_BLK, nb),
        in_specs=[pl.BlockSpec((None, tr, cols), lambda b, i, cr: (b, cr[0] * nb + i, 0)),
                  pl.BlockSpec((None, tr, cols), lambda b, i, cr: (b, i, 0))],
        out_specs=pl.BlockSpec((None, tr, cols), lambda b, i, cr: (b, i, 0)))
    return pl.pallas_call(body, name=name, grid_spec=gs, out_shape=S(r.shape, BF16),
                          compiler_params=_cp(("parallel", "parallel")))(c_arr, g, r)


def sum_chips(name, r, own, cb_arr):
    _, hr, cols = r.shape
    tr = _rows_tile(hr, cols, mult=16)
    nb = hr // tr

    def body(cb_ref, *refs):
        o_ref = refs[N_BLK + 1]
        b = cb_ref[1]
        acc = None
        for s in range(N_BLK):
            term = jnp.where(b == s, refs[N_BLK][...], refs[s][...]).astype(F32)
            acc = term if acc is None else acc + term
        o_ref[...] = acc

    slot = lambda s: pl.BlockSpec((None, tr, cols), lambda i, cb: (jnp.where(cb[1] == s, (s + 1) % N_BLK, s), i, 0))
    gs = pltpu.PrefetchScalarGridSpec(
        num_scalar_prefetch=1, grid=(nb,),
        in_specs=[slot(s) for s in range(N_BLK)] + [pl.BlockSpec((None, tr, cols), lambda i, cb: (cb[1], i, 0))],
        out_specs=pl.BlockSpec((tr, cols), lambda i, cb: (cb[0] * nb + i, 0)))
    return pl.pallas_call(body, name=name, grid_spec=gs, out_shape=S((2 * hr, cols), F32),
                          compiler_params=_cp(("parallel",)))(cb_arr, *([r] * N_BLK), own)


def adamw(name, w, m, v, gs):
    rows, cols = w.shape
    two = len(gs) == 2
    span = rows // 2 if two else rows
    tr = _rows_tile(span, cols, 1 << 21)
    nb = span // tr

    def body(w_ref, m_ref, v_ref, *rest):
        g_refs, (go_ref, d_ref, mo_ref, vo_ref) = rest[:len(gs)], rest[len(gs):]
        if two:
            g = jnp.where(pl.program_id(0) < nb, g_refs[0][...], g_refs[1][...])
        else:
            g = g_refs[0][...]
        mn = ADAM_B1 * m_ref[...] + (1.0 - ADAM_B1) * g
        vn = ADAM_B2 * v_ref[...] + (1.0 - ADAM_B2) * jnp.square(g)
        m_hat = mn / (1.0 - ADAM_B1 ** ADAM_STEP)
        v_hat = vn / (1.0 - ADAM_B2 ** ADAM_STEP)
        go_ref[...] = g
        d_ref[...] = -ADAM_LR * (m_hat / (jnp.sqrt(v_hat) + ADAM_EPS) + ADAM_WD * w_ref[...])
        mo_ref[...] = mn
        vo_ref[...] = vn

    full = pl.BlockSpec((tr, cols), lambda i: (i, 0))
    if two:
        g_specs = [pl.BlockSpec((tr, cols), lambda i: (jnp.minimum(i, nb - 1), 0)),
                   pl.BlockSpec((tr, cols), lambda i: (jnp.maximum(i - nb, 0), 0))]
    else:
        g_specs = [full]
    return pl.pallas_call(
        body, name=name, grid=(2 * nb if two else nb,), in_specs=[full, full, full] + g_specs, out_specs=[full] * 4,
        out_shape=[S((rows, cols), F32)] * 4, compiler_params=_cp(("parallel",)))(w, m, v, *gs)


WEIGHTS = ["ffn1_norm_pre", "ffn1_norm_post", "ffn1_w_in", "ffn1_w_out", "mix_norm_pre", "mix_norm_post", "mix_w_in",
           "gdn_conv_w", "gdn_a_log", "gdn_dt_bias", "gdn_norm_w", "gdn_w_o", "cnv_pw1_b", "cnv_dw_w", "cnv_dw_b",
           "cnv_ln_g", "cnv_ln_b", "cnv_w_o", "cnv_b_o", "mix_w_out", "ffn2_norm_pre", "ffn2_norm_post", "ffn2_w_in",
           "ffn2_w_out"]
BIG = {"ffn1_w_in": True, "ffn1_w_out": False, "mix_w_in": True, "gdn_conv_w": True, "gdn_w_o": False,
       "cnv_dw_w": True, "cnv_w_o": False, "mix_w_out": False, "ffn2_w_in": True, "ffn2_w_out": False}
TINY = {"gdn_conv_w": (32, LANES), "cnv_dw_w": (64, LANES)}
SMALL = [n for n in WEIGHTS if n not in BIG]
SUB = {"ffn1": ["ffn1_w_in", "ffn1_w_out"], "ffn2": ["ffn2_w_in", "ffn2_w_out"],
       "mix": ["mix_w_in", "gdn_conv_w", "gdn_w_o", "cnv_dw_w", "cnv_w_o", "mix_w_out"]}
GATHER_ON_FFN_MIX = [("ffn_in", ["mix_w_in"]), ("ffn_out", SUB["mix"][1:])]
RS_ON_FFN_MIX = [("ffn_in_bx", ["mix_w_in"]), ("ffn_out_bx", SUB["mix"][1:])]


def _whole(name, blocks):
    if BIG[name]:
        return jnp.transpose(blocks, (1, 0, 2)).reshape(blocks.shape[1], N_BLK * blocks.shape[2])
    return blocks.reshape(N_BLK * blocks.shape[1], blocks.shape[2])


def _blocks(name, whole):
    r, cfull = whole.shape
    if BIG[name]:
        blk = jnp.transpose(whole.reshape(r, N_BLK, cfull // N_BLK), (1, 0, 2))
    else:
        blk = whole.reshape(N_BLK, r // N_BLK, cfull)
    if name in TINY:
        tr, tc = TINY[name]
        flat = blk.reshape(N_BLK, -1)
        blk = jnp.pad(flat, ((0, 0), (0, tr * tc - flat.shape[1]))).reshape(N_BLK, tr, tc)
    return blk.astype(GRAD_DT)


def _pack(parts):
    rows = []
    for p in parts:
        flat = p.reshape(-1).astype(F32)
        rows.append(jnp.pad(flat, (0, (-flat.shape[0]) % LANES)).reshape(-1, LANES))
    out = jnp.concatenate(rows, axis=0)
    return jnp.pad(out, ((0, (-out.shape[0]) % 8), (0, 0)))


def _unpack(packed, shapes):
    out, r = [], 0
    for shp in shapes:
        size = math.prod(shp)
        nr = -(-size // LANES)
        out.append(packed[r:r + nr].reshape(-1)[:size].reshape(shp))
        r += nr
    return out


def kernel(x, ffn1_norm_pre, ffn1_norm_post, ffn1_w_in, ffn1_w_out, mix_norm_pre, mix_norm_post, mix_w_in, gdn_conv_w, gdn_a_log, gdn_dt_bias, gdn_norm_w, gdn_w_o, cnv_pw1_b, cnv_dw_w, cnv_dw_b, cnv_ln_g, cnv_ln_b, cnv_w_o, cnv_b_o, mix_w_out, ffn2_norm_pre, ffn2_norm_post, ffn2_w_in, ffn2_w_out, loss_target, m_ffn1_norm_pre, m_ffn1_norm_post, m_ffn1_w_in, m_ffn1_w_out, m_mix_norm_pre, m_mix_norm_post, m_mix_w_in, m_gdn_conv_w, m_gdn_a_log, m_gdn_dt_bias, m_gdn_norm_w, m_gdn_w_o, m_cnv_pw1_b, m_cnv_dw_w, m_cnv_dw_b, m_cnv_ln_g, m_cnv_ln_b, m_cnv_w_o, m_cnv_b_o, m_mix_w_out, m_ffn2_norm_pre, m_ffn2_norm_post, m_ffn2_w_in, m_ffn2_w_out, v_ffn1_norm_pre, v_ffn1_norm_post, v_ffn1_w_in, v_ffn1_w_out, v_mix_norm_pre, v_mix_norm_post, v_mix_w_in, v_gdn_conv_w, v_gdn_a_log, v_gdn_dt_bias, v_gdn_norm_w, v_gdn_w_o, v_cnv_pw1_b, v_cnv_dw_w, v_cnv_dw_b, v_cnv_ln_g, v_cnv_ln_b, v_cnv_w_o, v_cnv_b_o, v_mix_w_out, v_ffn2_norm_pre, v_ffn2_norm_post, v_ffn2_w_in, v_ffn2_w_out):
    args = locals()
    wts = {n: args[n] for n in WEIGHTS}
    mom = {n: args["m_" + n] for n in WEIGHTS}
    var = {n: args["v_" + n] for n in WEIGHTS}
    big = list(BIG)

    mx, my, mc = _pos()
    mb = 2 * mx + my
    cb_arr = jnp.stack([mc, mb]).astype(jnp.int32)

    own = {(n, l): wts[n][l].astype(BF16) for n in big for l in range(DEPTH)}

    def planned(make_job, arrays, plan):
        jobs = []
        for host, names in plan:
            jb = make_job([arrays[n] for n in names])
            jb.host, jb.names = host, names
            jobs.append(jb)
        return jobs

    def landed(jobs):
        res = {}
        for jb in jobs:
            if jb.results is None:
                run_job("comm_alone", jb)
            res.update(zip(jb.names, jb.results))
        return res

    stages = [(l, s) for l in range(DEPTH) for s in ("ffn1", "mix", "ffn2")]
    carried_by_ffn = lambda s: GATHER_ON_FFN_MIX if s == "mix" else [("ffn_in", [s + "_w_in"]), ("ffn_out", [s + "_w_out"])]
    carried_by_mix = lambda s: [("mix_in", [s + "_w_in", s + "_w_out"])]

    def gather_jobs(l, s, carrier):
        plan = carried_by_mix(s) if carrier == "mix" else carried_by_ffn(s)
        return planned(job_gather_ici, {n: own[(n, l)] for n in SUB[s]}, plan)

    def sub_weights(l, s, jobs):
        lands = landed(jobs)
        names = SUB[s]
        lands = dict(zip(names, run_job("gather_sib", job_gather_sibling([lands[n] for n in names]))))
        wl = {}
        for n in names:
            blocks = lax.dynamic_update_index_in_dim(lands[n], own[(n, l)], mb, 0)
            wl[n] = blocks if n in BLOCKED else _whole(n, blocks)
        wl.update({n: wts[n][l] for n in SMALL})
        return prep_mix(wl) if s == "mix" else prep_ffn(wl, s)

    act = x[0]
    saved, weights = {}, {}
    jobs = planned(job_gather_ici, {n: own[(n, 0)] for n in SUB["ffn1"]}, [(None, SUB["ffn1"])])
    for i, (l, s) in enumerate(stages):
        weights[(l, s)] = sub_weights(l, s, jobs)
        jobs = gather_jobs(*stages[i + 1], "mix" if s == "mix" else "ffn") if i + 1 < len(stages) else []
        pending = list(jobs)
        fwd = mix_fwd if s == "mix" else ffn_fwd
        act, saved[(l, s)] = fwd("mix" if s == "mix" else "ffn", act, weights[(l, s)], pending)
    dx, loss = loss_head("loss", act, loss_target[0], _tm(act.shape[0]))

    def rs_jobs(l, s, g, carrier):
        if s == "mix":
            gw_s = unprep_mix(g)
        else:
            gw_s = unprep_ffn(g, s)
        small_grads[l].update({n: gw_s[n] for n in gw_s if n in SMALL})
        names = SUB[s]
        blocks = [gw_s[n + "#blocks"] if n in BLOCKED else _blocks(n, gw_s[n]) for n in names]
        parts = [add_half("add_half", b_, r, cb_arr) for b_, r in zip(blocks, rs_sibling(blocks))]
        partial_of.update({(n, l): p for n, p in zip(names, parts)})
        if carrier is None:
            plan = [(None, names)]
        elif carrier == "mix":
            plan = [("gintra_b", names)]
        elif s == "mix":
            plan = RS_ON_FFN_MIX
        else:
            plan = [("ffn_in_bx", [s + "_w_in"]), ("ffn_out_bx", [s + "_w_out"])]
        return planned(job_rs_chips, dict(zip(names, parts)), plan)

    small_grads = [{} for _ in range(DEPTH)]
    partial_of, chip_of = {}, {}
    jobs, jobs_key = [], None
    for i, (l, s) in enumerate(reversed(stages)):
        pending = list(jobs)
        bwd = mix_bwd if s == "mix" else ffn_bwd
        dx, g = bwd("mix" if s == "mix" else "ffn", dx, saved[(l, s)], weights[(l, s)], pending)
        if jobs:
            chip_of.update({(n, jobs_key): r for n, r in landed(jobs).items()})
        nxt = list(reversed(stages))[i + 1][1] if i + 1 < len(stages) else None
        jobs, jobs_key = rs_jobs(l, s, g, None if nxt is None else ("mix" if nxt == "mix" else "ffn")), l
    chip_of.update({(n, jobs_key): r for n, r in landed(jobs).items()})
    gw = small_grads

    small_shapes = [wts[n].shape for n in SMALL]
    packed = _pack([jnp.stack([gw[l][n] for l in range(DEPTH)]) for n in SMALL] + [loss])
    total = allreduce_small(packed)
    small_g = dict(zip(SMALL, _unpack(total, small_shapes)))
    loss_sum = total[sum(-(-math.prod(s) // LANES) for s in small_shapes), 0]

    keys = [(n, l) for l in range(DEPTH) for n in big]
    halves = [sum_chips("sum_chips", chip_of[k], partial_of[k], cb_arr) for k in keys]
    summed = dict(zip(keys, ag_sibling(halves)))

    out_g, out_d, out_m, out_v = {}, {}, {}, {}
    for n in big:
        shp = wts[n].shape
        gs = [summed[(n, l)] for l in range(DEPTH)]
        if n in TINY:
            gs = [jnp.concatenate([g.reshape(-1)[:shp[1] * shp[2]].reshape(shp[1], shp[2]) for g in gs], axis=0)]
        two_d = lambda a: a.reshape(DEPTH * shp[1], shp[2])
        res = adamw("adamw", two_d(wts[n]), two_d(mom[n]), two_d(var[n]), gs)
        out_g[n], out_d[n], out_m[n], out_v[n] = [r.reshape(shp) for r in res]

    pk = lambda d: _pack([d[n] for n in SMALL])
    res = adamw("adamw_small", pk(wts), pk(mom), pk(var), [pk(small_g)])
    for d, r in zip((out_g, out_d, out_m, out_v), res):
        d.update(dict(zip(SMALL, _unpack(r, small_shapes))))

    return (loss_sum, dx[None], *[out_g[n] for n in WEIGHTS], *[out_d[n] for n in WEIGHTS],
            *[out_m[n] for n in WEIGHTS], *[out_v[n] for n in WEIGHTS])
```

```python
import functools
import math

import jax
import jax.numpy as jnp
from jax import lax
from jax.experimental import pallas as pl
from jax.experimental.pallas import tpu as pltpu

F32, BF16 = jnp.float32, jnp.bfloat16
S = jax.ShapeDtypeStruct

D_MODEL = 1024
D_FF = 2816
HEADS = 8
DK = 128
CHUNK = 64
GDN_CONV = 4
CNV_K = 31
W_QKV = 3 * HEADS * DK
W_Z = HEADS * DK
W_GLU = 2 * D_MODEL
W_GATE = 2 * D_MODEL
P_IN = W_QKV + W_Z + 2 * HEADS + W_GLU + W_GATE
LANES = 128
P_ALL = W_QKV + W_Z + W_GLU + W_GATE + LANES
COL_Z = W_QKV // LANES
COL_GLU = (W_QKV + W_Z) // LANES
COL_GATE = (W_QKV + W_Z + W_GLU) // LANES
COL_BA = (W_QKV + W_Z + W_GLU + W_GATE) // LANES
RMS_EPS = 1e-6
LN_EPS = 1e-5
DEPTH = 2
N_BLK = 4
VMEM_LIMIT = 56 * 1024 * 1024
GRAD_DT = BF16

ADAM_LR, ADAM_B1, ADAM_B2, ADAM_EPS, ADAM_WD, ADAM_STEP = 0.001, 0.9, 0.999, 1e-08, 0.01, 10


def _cp(sem):
    return pltpu.CompilerParams(dimension_semantics=sem, vmem_limit_bytes=VMEM_LIMIT)


MM_VMEM_BUDGET = 36 * 1024 * 1024


def _mm_tiles(m, n, k_bytes_a, k_bytes_b, out_bytes, tn_fixed=None):
    best = None
    for tm in (1024, 512, 256, 128):
        if m % tm:
            continue
        for tn in ((tn_fixed,) if tn_fixed else (1024, 512, 640, 256, 384, 128)):
            if n % tn:
                continue
            need = 2 * (tm * k_bytes_a + tn * k_bytes_b + tm * tn * out_bytes)
            if need <= MM_VMEM_BUDGET and (best is None or tm * tn > best[0] * best[1]):
                best = (tm, tn)
    if best is None:
        raise ValueError((m, n, k_bytes_a, k_bytes_b))
    return best


def mm_nt_sum(name, parts, b):
    m, n = parts[0][0].shape[0], b.shape[0]
    k_total = sum(a.shape[1] for a, _ in parts)
    tm, tn = _mm_tiles(m, n, k_total * 2, k_total * 2, 4)
    n_p = len(parts)

    def body(*refs):
        o_ref = refs[2 * n_p]
        acc = None
        for a_ref, b_ref in zip(refs[:n_p], refs[n_p:2 * n_p]):
            t = lax.dot_general(a_ref[...], b_ref[...], (((1,), (1,)), ((), ())), preferred_element_type=F32)
            acc = t if acc is None else acc + t
        o_ref[...] = acc

    a_specs = [pl.BlockSpec((tm, a.shape[1]), lambda i, j: (i, 0)) for a, _ in parts]
    b_specs = [pl.BlockSpec((tn, a.shape[1]), functools.partial(lambda i, j, c: (j, c), c=col)) for a, col in parts]
    return pl.pallas_call(
        body, name=name, grid=(m // tm, n // tn), in_specs=a_specs + b_specs,
        out_specs=pl.BlockSpec((tm, tn), lambda i, j: (i, j)), out_shape=S((m, n), F32),
        compiler_params=_cp(("parallel", "parallel")))(*[a for a, _ in parts], *([b] * n_p))


def _take(jobs, host):
    for jb in jobs or []:
        if jb.host == host:
            jobs.remove(jb)
            return jb
    return None


def _pcall(body, job, *, name, grid, in_specs, out_specs, out_shape, scratch_shapes=(), semantics):
    in_specs, out_specs, out_shape, scratch_shapes = list(in_specs), list(out_specs), list(out_shape), list(scratch_shapes)
    if job is None:
        return lambda *args: pl.pallas_call(
            body, name=name, grid=grid, in_specs=in_specs, out_specs=out_specs, out_shape=out_shape,
            scratch_shapes=scratch_shapes, compiler_params=_cp(semantics))(*args)
    n_in, n_out, n_sc = len(in_specs), len(out_specs), len(scratch_shapes)
    n_i, n_o = len(job.ins), len(job.outs)

    def wrapped(*refs):
        cut = [n_in, n_i, n_out, n_o, n_sc]
        parts, pos = [], 0
        for c in cut:
            parts.append(refs[pos:pos + c])
            pos += c
        ins, j_in, outs, j_out, own = parts
        sems = refs[pos:]
        ids = [pl.program_id(d) for d in range(len(grid))]
        first = functools.reduce(lambda p, q: p & q, [i == 0 for i in ids])
        last = functools.reduce(lambda p, q: p & q, [i == g - 1 for i, g in zip(ids, grid)])

        @pl.when(first)
        def _():
            job.start(j_in, j_out, sems)

        body(*ins, *outs, *own)

        @pl.when(last)
        def _():
            job.finish(j_in, j_out, sems)

    def call(*args):
        res = pl.pallas_call(
            wrapped, name=name + "_c", grid=grid, in_specs=in_specs + [ANY] * n_i, out_specs=out_specs + [ANY] * n_o,
            out_shape=out_shape + list(job.outs),
            input_output_aliases={n_in + ki: n_out + ko for ki, ko in job.aliases.items()},
            scratch_shapes=scratch_shapes + job.scratch(),
            compiler_params=_cp(("arbitrary",) * len(grid)))(*args, *job.ins)
        job.results = res[n_out:]
        return res[:n_out]

    return call


def mm(name, a, b, ta=False, tb=False, out_dtype=F32, job=None, out_blocked=False):
    k = a.shape[0] if ta else a.shape[1]
    m = a.shape[1] if ta else a.shape[0]
    blocked = b.ndim == 3
    cb = b.shape[2] if blocked else None
    osz = jnp.dtype(out_dtype).itemsize
    if blocked and not tb:
        n = N_BLK * cb
        tm, tn = _mm_tiles(m, n, k * 2, k * 2, osz, tn_fixed=cb)
        b_spec = pl.BlockSpec((None, k, cb), lambda i, j: (j, 0, 0))
    elif blocked:
        n = b.shape[1]
        assert k == N_BLK * cb, (name, a.shape, b.shape)
        tm, tn = _mm_tiles(m, n, k * 2, k * 2, osz)
        b_spec = pl.BlockSpec((N_BLK, tn, cb), lambda i, j: (0, j, 0))
    else:
        n = b.shape[0] if tb else b.shape[1]
        assert k == (b.shape[1] if tb else b.shape[0]), (name, a.shape, b.shape)
        tm, tn = _mm_tiles(m, n, k * 2, k * 2, osz, tn_fixed=n // N_BLK if out_blocked else None)
        b_spec = pl.BlockSpec((tn, k), lambda i, j: (j, 0)) if tb else pl.BlockSpec((k, tn), lambda i, j: (0, j))
    a_spec = pl.BlockSpec((k, tm), lambda i, j: (0, i)) if ta else pl.BlockSpec((tm, k), lambda i, j: (i, 0))
    if out_blocked:
        o_spec, o_shape = pl.BlockSpec((None, tm, tn), lambda i, j: (j, i, 0)), S((N_BLK, m, tn), out_dtype)
    else:
        o_spec, o_shape = pl.BlockSpec((tm, tn), lambda i, j: (i, j)), S((m, n), out_dtype)
    dims = (((0 if ta else 1,), (1 if tb else 0,)), ((), ()))
    gm, gn = m // tm, n // tn

    def product(a_ref, b_ref):
        if blocked and tb:
            acc = None
            for q in range(N_BLK):
                t = lax.dot_general(a_ref[:, q * cb:(q + 1) * cb], b_ref[q], (((1,), (1,)), ((), ())),
                                    preferred_element_type=F32)
                acc = t if acc is None else acc + t
            return acc
        return lax.dot_general(a_ref[...], b_ref[...], dims, preferred_element_type=F32)

    def body(a_ref, b_ref, o_ref):
        o_ref[...] = product(a_ref, b_ref).astype(o_ref.dtype)

    return _pcall(body, job, name=name, grid=(gm, gn), in_specs=[a_spec, b_spec], out_specs=[o_spec],
                  out_shape=[o_shape], semantics=("parallel", "parallel"))(a, b)[0]


def ew_fwd(name, fn, grid, ins, outs):
    n_in = len(ins)

    def body(*refs):
        vals = [r[...].astype(F32) for r in refs[:n_in]]
        res = fn(pl.program_id(0), *vals)
        for r, v in zip(refs[n_in:], res):
            r[...] = v.astype(r.dtype)

    out = pl.pallas_call(
        body, name=name, grid=grid, in_specs=[s for _, s in ins], out_specs=[s for _, s in outs],
        out_shape=[sd for sd, _ in outs], compiler_params=_cp(("parallel", "parallel")))(*[a for a, _ in ins])
    return out


def ew_bwd(name, fn, grid, ins, cts, wrt, acc, add=None):
    n_in, n_ct, n_wrt, n_acc = len(ins), len(cts), len(wrt), len(acc)
    has_add = add is not None

    def body(*refs):
        in_refs = refs[:n_in]
        ct_refs = refs[n_in:n_in + n_ct]
        pos = n_in + n_ct
        add_ref = refs[pos] if has_add else None
        pos += 1 if has_add else 0
        wrt_refs = refs[pos:pos + n_wrt]
        acc_refs = refs[pos + n_wrt:pos + n_wrt + n_acc]
        col, tok = pl.program_id(0), pl.program_id(1)
        vals = [r[...].astype(F32) for r in in_refs]
        _, vjp = jax.vjp(lambda *a: fn(col, *a), *vals)
        grads = vjp(tuple(c[...].astype(F32) for c in ct_refs))
        for pos_w, ((idx, _, _), r) in enumerate(zip(wrt, wrt_refs)):
            g = grads[idx]
            if has_add and pos_w == 0:
                g = g + add_ref[...]
            r[...] = g.astype(r.dtype)
        for (idx, _, _, over_cols), r in zip(acc, acc_refs):
            first = (tok == 0) & (col == 0) if over_cols else tok == 0

            @pl.when(first)
            def _():
                r[...] = jnp.zeros_like(r)

            r[...] += grads[idx]

    arrays = [a for a, _ in ins] + [a for a, _ in cts] + ([add[0]] if has_add else [])
    in_specs = [s for _, s in ins] + [s for _, s in cts] + ([add[1]] if has_add else [])
    over_any = any(o for *_, o in acc)
    out = pl.pallas_call(
        body, name=name, grid=grid, in_specs=in_specs,
        out_specs=[s for _, _, s in wrt] + [s for _, _, s, _ in acc],
        out_shape=[sd for _, sd, _ in wrt] + [sd for _, sd, _, _ in acc],
        compiler_params=_cp(("arbitrary" if over_any else "parallel", "arbitrary")))(*arrays)
    return out


def _tok(width, col=0):
    return lambda tm: pl.BlockSpec((tm, width), lambda j, i: (i, col))


def _tokcol(off=0):
    return lambda tm: pl.BlockSpec((tm, LANES), lambda j, i: (i, off + j))


def _par(width, col=0):
    return pl.BlockSpec((1, width), lambda j, i: (0, col))


def _parcol(off=0):
    return pl.BlockSpec((1, LANES), lambda j, i: (0, off + j))


def _rms(x, w, eps=RMS_EPS):
    return x * lax.rsqrt(jnp.mean(x * x, axis=-1, keepdims=True) + eps) * w


def _silu(x):
    return x * jax.nn.sigmoid(x)


def fn_rms(col, x, w):
    return (_rms(x, w),)


def fn_swiglu(col, gate, up):
    return (_silu(gate) * up,)


def swiglu_bwd(name, u, da, tm):
    n_tok, f2 = u.shape
    f = f2 // 2

    def body(g_ref, up_ref, da_ref, o_ref):
        g, d = g_ref[...].astype(F32), da_ref[...].astype(F32)
        s = jax.nn.sigmoid(g)
        o_ref[:, :f] = (d * up_ref[...].astype(F32) * (s * (1.0 + g * (1.0 - s)))).astype(o_ref.dtype)
        o_ref[:, f:] = (d * (g * s)).astype(o_ref.dtype)

    half = lambda c: pl.BlockSpec((tm, f), lambda i: (i, c))
    return pl.pallas_call(
        body, name=name, grid=(n_tok // tm,), in_specs=[half(0), half(1), half(0)],
        out_specs=pl.BlockSpec((tm, f2), lambda i: (i, 0)), out_shape=S((n_tok, f2), BF16),
        compiler_params=_cp(("parallel",)))(u, u, da)


def fn_gdnout(col, o, z, nw):
    return (_rms(o, nw) * _silu(z),)


def fn_glu(col, a, g, ba, bg):
    return ((a + ba) * jax.nn.sigmoid(g + bg),)


def fn_lnsilu(col, h, g, b):
    mu = jnp.mean(h, axis=-1, keepdims=True)
    var = jnp.mean(jnp.square(h - mu), axis=-1, keepdims=True)
    return (_silu((h - mu) * lax.rsqrt(var + LN_EPS) * g + b),)


def fn_merge(col, ya, yb, ga, gb, bo):
    return (jax.nn.sigmoid(ga) * ya + jax.nn.sigmoid(gb) * (yb + bo),)


HALO = 32


def conv_fwd(name, x, col_off, n_ch, w, bias, tb):
    n_tok = x.shape[0]
    k = w.shape[0]
    nt = n_tok // tb

    def body(xp_ref, xc_ref, w_ref, *rest):
        if bias is not None:
            b_ref, o_ref, xs = rest
        else:
            o_ref, xs = rest
        i = pl.program_id(1)
        xs[0:HALO, :] = jnp.where(i == 0, 0.0, xp_ref[tb - HALO:tb, :].astype(F32))
        xs[HALO:HALO + tb, :] = xc_ref[...].astype(F32)
        acc = jnp.zeros((tb, LANES), F32)
        for j in range(k):
            s = k - 1 - j
            acc = acc + w_ref[j:j + 1, :] * xs[HALO - s:HALO - s + tb, :]
        if bias is not None:
            acc = acc + b_ref[...]
        o_ref[...] = acc

    in_specs = [pl.BlockSpec((tb, LANES), lambda j, i: (jnp.maximum(i - 1, 0), col_off + j)),
                pl.BlockSpec((tb, LANES), lambda j, i: (i, col_off + j)),
                pl.BlockSpec((k, LANES), lambda j, i: (0, j))]
    args = [x, x, w]
    if bias is not None:
        in_specs.append(pl.BlockSpec((1, LANES), lambda j, i: (0, j)))
        args.append(bias)
    return pl.pallas_call(
        body, name=name, grid=(n_ch // LANES, nt), in_specs=in_specs,
        out_specs=pl.BlockSpec((tb, LANES), lambda j, i: (i, j)), out_shape=S((n_tok, n_ch), F32),
        scratch_shapes=[pltpu.VMEM((HALO + tb, LANES), F32)],
        compiler_params=_cp(("parallel", "parallel")))(*args)


def conv_bwd(name, x, col_off, n_ch, w, dy, dx_dtype, tb):
    n_tok = x.shape[0]
    k = w.shape[0]
    nt = n_tok // tb

    def body(xp_ref, xc_ref, w_ref, dyc_ref, dyn_ref, dx_ref, dw_ref, db_ref, xs, dys):
        i = pl.program_id(1)
        xs[0:HALO, :] = jnp.where(i == 0, 0.0, xp_ref[tb - HALO:tb, :].astype(F32))
        xs[HALO:HALO + tb, :] = xc_ref[...].astype(F32)
        dyc = dyc_ref[...]
        dys[0:tb, :] = dyc
        dys[tb:tb + HALO, :] = jnp.where(i == nt - 1, 0.0, dyn_ref[0:HALO, :])

        @pl.when(i == 0)
        def _():
            dw_ref[...] = jnp.zeros_like(dw_ref)
            db_ref[...] = jnp.zeros_like(db_ref)

        acc = jnp.zeros((tb, LANES), F32)
        for j in range(k):
            s = k - 1 - j
            acc = acc + w_ref[j:j + 1, :] * dys[s:s + tb, :]
            dw_ref[j:j + 1, :] += jnp.sum(dyc * xs[HALO - s:HALO - s + tb, :], axis=0, keepdims=True)
        dx_ref[...] = acc.astype(dx_ref.dtype)
        db_ref[...] += jnp.sum(dyc, axis=0, keepdims=True)

    in_specs = [pl.BlockSpec((tb, LANES), lambda j, i: (jnp.maximum(i - 1, 0), col_off + j)),
                pl.BlockSpec((tb, LANES), lambda j, i: (i, col_off + j)),
                pl.BlockSpec((k, LANES), lambda j, i: (0, j)),
                pl.BlockSpec((tb, LANES), lambda j, i: (i, j)),
                pl.BlockSpec((tb, LANES), lambda j, i: (jnp.minimum(i + 1, nt - 1), j))]
    return pl.pallas_call(
        body, name=name, grid=(n_ch // LANES, nt), in_specs=in_specs,
        out_specs=[pl.BlockSpec((tb, LANES), lambda j, i: (i, j)),
                   pl.BlockSpec((k, LANES), lambda j, i: (0, j)),
                   pl.BlockSpec((1, LANES), lambda j, i: (0, j))],
        out_shape=[S((n_tok, n_ch), dx_dtype), S((k, n_ch), F32), S((1, n_ch), F32)],
        scratch_shapes=[pltpu.VMEM((HALO + tb, LANES), F32), pltpu.VMEM((tb + HALO, LANES), F32)],
        compiler_params=_cp(("parallel", "arbitrary")))(x, x, w, dy, dy)


def gdnconv_fwd(name, p, w, tb, job=None):
    n_tok = p.shape[0]
    k = w.shape[0]
    nt = n_tok // tb

    def body(xp_ref, xc_ref, w_ref, o_ref, xs):
        j, i = pl.program_id(0), pl.program_id(1)
        xs[0:HALO, :] = jnp.where(i == 0, 0.0, xp_ref[tb - HALO:tb, :].astype(F32))
        xs[HALO:HALO + tb, :] = xc_ref[...].astype(F32)
        c = jnp.zeros((tb, LANES), F32)
        for t in range(k):
            s = k - 1 - t
            c = c + w_ref[t:t + 1, :] * xs[HALO - s:HALO - s + tb, :]
        y = c * jax.nn.sigmoid(c)
        r = lax.rsqrt(jnp.sum(y * y, axis=-1, keepdims=True) + 1e-6) * jnp.where(j < HEADS, DK ** -0.5, 1.0)
        o_ref[...] = jnp.where(j < 2 * HEADS, y * r, y)

    return _pcall(
        body, job, name=name, grid=(W_QKV // LANES, nt),
        in_specs=[pl.BlockSpec((tb, LANES), lambda j, i: (jnp.maximum(i - 1, 0), j)),
                  pl.BlockSpec((tb, LANES), lambda j, i: (i, j)),
                  pl.BlockSpec((k, LANES), lambda j, i: (0, j))],
        out_specs=[pl.BlockSpec((tb, LANES), lambda j, i: (i, j))], out_shape=[S((n_tok, W_QKV), F32)],
        scratch_shapes=[pltpu.VMEM((HALO + tb, LANES), F32)], semantics=("parallel", "parallel"))(p, p, w)[0]


def gdnconv_bwd(name, p, w, dn, tb, job=None):
    n_tok = p.shape[0]
    k = w.shape[0]
    nt = n_tok // tb
    ext = tb + HALO

    def body(xp_ref, xc_ref, xn_ref, w_ref, dnc_ref, dnn_ref, dx_ref, dw_ref, xs, dns, dcs):
        j, i = pl.program_id(0), pl.program_id(1)
        xs[0:HALO, :] = jnp.where(i == 0, 0.0, xp_ref[tb - HALO:tb, :].astype(F32))
        xs[HALO:HALO + tb, :] = xc_ref[...].astype(F32)
        xs[HALO + tb:HALO + ext, :] = jnp.where(i == nt - 1, 0.0, xn_ref[0:HALO, :].astype(F32))
        dns[0:tb, :] = dnc_ref[...]
        dns[tb:ext, :] = jnp.where(i == nt - 1, 0.0, dnn_ref[0:HALO, :])

        @pl.when(i == 0)
        def _():
            dw_ref[...] = jnp.zeros_like(dw_ref)

        c = jnp.zeros((ext, LANES), F32)
        for t in range(k):
            s = k - 1 - t
            c = c + w_ref[t:t + 1, :] * xs[HALO - s:HALO - s + ext, :]
        d = dns[...]
        sg = jax.nn.sigmoid(c)
        y = c * sg
        r = lax.rsqrt(jnp.sum(y * y, axis=-1, keepdims=True) + 1e-6)
        scale = jnp.where(j < HEADS, DK ** -0.5, 1.0)
        dy_norm = scale * (d * r - y * (r * r * r) * jnp.sum(d * y, axis=-1, keepdims=True))
        dy = jnp.where(j < 2 * HEADS, dy_norm, d)
        dc = dy * (sg * (1.0 + c * (1.0 - sg)))
        dcs[...] = dc
        acc = jnp.zeros((tb, LANES), F32)
        for t in range(k):
            s = k - 1 - t
            acc = acc + w_ref[t:t + 1, :] * dcs[s:s + tb, :]
            dw_ref[t:t + 1, :] += jnp.sum(dcs[0:tb, :] * xs[HALO - s:HALO - s + tb, :], axis=0, keepdims=True)
        dx_ref[...] = acc.astype(dx_ref.dtype)

    cur = lambda j, i: (i, j)
    nxt = lambda j, i: (jnp.minimum(i + 1, nt - 1), j)
    return _pcall(
        body, job, name=name, grid=(W_QKV // LANES, nt),
        in_specs=[pl.BlockSpec((tb, LANES), lambda j, i: (jnp.maximum(i - 1, 0), j)),
                  pl.BlockSpec((tb, LANES), cur), pl.BlockSpec((tb, LANES), nxt),
                  pl.BlockSpec((k, LANES), lambda j, i: (0, j)),
                  pl.BlockSpec((tb, LANES), cur), pl.BlockSpec((tb, LANES), nxt)],
        out_specs=[pl.BlockSpec((tb, LANES), cur), pl.BlockSpec((k, LANES), lambda j, i: (0, j))],
        out_shape=[S((n_tok, W_QKV), BF16), S((k, W_QKV), F32)],
        scratch_shapes=[pltpu.VMEM((HALO + ext, LANES), F32), pltpu.VMEM((ext, LANES), F32),
                        pltpu.VMEM((ext, LANES), F32)],
        semantics=("parallel", "arbitrary"))(p, p, p, w, dn, dn)


GDN_GROUP = 4


def _dotb(a, b, ca, cb):
    return lax.dot_general(a.astype(BF16), b.astype(BF16), (((ca,), (cb,)), ((), ())), preferred_element_type=F32)


def _dot32(a, b, ca, cb):
    return lax.dot_general(a, b, (((ca,), (cb,)), ((), ())), preferred_element_type=F32,
                           precision=lax.Precision.HIGHEST)


def _dot3_many(xs, ys, ca, cb):
    xh = [x.astype(BF16) for x in xs]
    xl = [(x - h.astype(F32)).astype(BF16) for x, h in zip(xs, xh)]
    yh = [y.astype(BF16) for y in ys]
    yl = [(y - h.astype(F32)).astype(BF16) for y, h in zip(ys, yh)]
    dg = lambda p, q: lax.dot_general(p, q, (((ca,), (cb,)), ((), ())), preferred_element_type=F32)
    hh = [dg(p, q) for p, q in zip(xh, yh)]
    hl = [dg(p, q) for p, q in zip(xh, yl)]
    lh = [dg(p, q) for p, q in zip(xl, yh)]
    return [a + (b + c) for a, b, c in zip(hh, hl, lh)]


@jax.custom_vjp
def _mm3_many(xs, ys):
    return _dot3_many(xs, ys, 1, 0)


def _mm3_fwd(xs, ys):
    return _dot3_many(xs, ys, 1, 0), (xs, ys)


def _mm3_bwd(res, cts):
    xs, ys = res
    return _dot3_many(cts, ys, 1, 1), _dot3_many(xs, cts, 0, 0)


_mm3_many.defvjp(_mm3_fwd, _mm3_bwd)


@jax.custom_vjp
def _inv_unit_lower_many(mats):
    n = mats[0].shape[0]
    eye = (lax.broadcasted_iota(jnp.int32, (n, n), 0) == lax.broadcasted_iota(jnp.int32, (n, n), 1)).astype(F32)
    inv = [eye - a for a in mats]
    p = list(mats)
    for _ in range(int(math.log2(n)) - 1):
        p = _dot3_many(p, p, 1, 0)
        upd = _dot3_many(inv, p, 1, 0)
        inv = [i + u for i, u in zip(inv, upd)]
    return inv


def _inv_fwd(mats):
    t = _inv_unit_lower_many(mats)
    return t, t


def _inv_bwd(t, dt):
    x = _dot3_many(t, dt, 0, 0)
    return ([-y for y in _dot3_many(x, t, 1, 1)],)


_inv_unit_lower_many.defvjp(_inv_fwd, _inv_bwd)


@jax.custom_vjp
def _inv_given(mats, saved):
    return list(saved)


def _inv_given_fwd(mats, saved):
    return list(saved), list(saved)


def _inv_given_bwd(t, dt):
    return _inv_bwd(t, dt)[0], [jnp.zeros_like(s) for s in t]


_inv_given.defvjp(_inv_given_fwd, _inv_given_bwd)


def _softplus(x):
    return jnp.maximum(x, 0.0) + jnp.log(1.0 + jnp.exp(-jnp.abs(x)))


def _gdn_intra(qs, ks, vs, pbas, alog, dtb, tinv_saved=None):
    c = pbas[0].shape[0]
    row = lax.broadcasted_iota(jnp.int32, (c, c), 0)
    colm = lax.broadcasted_iota(jnp.int32, (c, c), 1)
    causal, strict = row >= colm, row > colm
    tril = causal.astype(F32)
    lane = lax.broadcasted_iota(jnp.int32, (1, LANES), 1)
    sub = lax.broadcasted_iota(jnp.int32, (LANES, 1), 0)
    last = (lax.broadcasted_iota(jnp.int32, (c, 1), 0) == c - 1).astype(F32)
    beta_all = [jax.nn.sigmoid(pb) for pb in pbas]
    g_all = [-jnp.exp(alog) * _softplus(pb + dtb) for pb in pbas]
    gc_all = [_dot32(tril, ga, 1, 0) for ga in g_all]
    gr_all = [_dot32(ga, tril, 0, 1) for ga in g_all]
    idx = [(g, h) for g in range(len(pbas)) for h in range(HEADS)]
    beta = [jnp.sum(beta_all[g] * (lane == h).astype(F32), axis=1, keepdims=True) for g, h in idx]
    gc = [jnp.sum(gc_all[g] * (lane == HEADS + h).astype(F32), axis=1, keepdims=True) for g, h in idx]
    gr = [jnp.sum(gr_all[g] * (sub == HEADS + h).astype(F32), axis=0, keepdims=True) for g, h in idx]
    decay = [jnp.where(causal, jnp.exp(jnp.where(causal, a - b, 0.0)), 0.0) for a, b in zip(gc, gr)]
    kk = [_dotb(k, k, 1, 1) for k in ks]
    a_mats = [jnp.where(strict, x * d * b, 0.0) for x, d, b in zip(kk, decay, beta)]
    tinv = _inv_unit_lower_many(a_mats) if tinv_saved is None else _inv_given(a_mats, tinv_saved)
    eg = [jnp.exp(a) for a in gc]
    g_last = [jnp.sum(a * last, axis=0, keepdims=True) for a in gc]
    us = _mm3_many(tinv, [v * b for v, b in zip(vs, beta)])
    ws = _mm3_many(tinv, [k * (b * e) for k, b, e in zip(ks, beta, eg)])
    qds = [q * e for q, e in zip(qs, eg)]
    kds = [k * jnp.exp(gl - a) for k, gl, a in zip(ks, g_last, gc)]
    qks = [_dotb(q, k, 1, 1) * d for q, k, d in zip(qs, ks, decay)]
    decs = [jnp.exp(gl) for gl in g_last]
    return us, ws, qds, kds, qks, decs, tinv


def _gdn_seq(us, ws, qds, kds, qks, decs, states):
    corr = [_dotb(w, st, 1, 0) for w, st in zip(ws, states)]
    from_state = [_dotb(qd, st, 1, 0) for qd, st in zip(qds, states)]
    v_new = [u - x for u, x in zip(us, corr)]
    intra = [_dotb(qk, vn, 1, 0) for qk, vn in zip(qks, v_new)]
    upd = [_dotb(kd, vn, 0, 0) for kd, vn in zip(kds, v_new)]
    outs = [a + b for a, b in zip(from_state, intra)]
    news = [st * d + x for st, d, x in zip(states, decs, upd)]
    return outs, news


def _heads(ref, rows=slice(None), base=0):
    return [ref[rows, (base + h) * DK:(base + h + 1) * DK].astype(F32) for h in range(HEADS)]


def _qk_heads(ref, rows=slice(None)):
    return [ref[rows, h * DK:h * DK + CHUNK].astype(F32) for h in range(HEADS)]


def _put_heads(ref, vals, rows=slice(None), base=0):
    for h in range(HEADS):
        ref[rows, (base + h) * DK:(base + h + 1) * DK] = vals[h].astype(ref.dtype)


def _put_qk(ref, vals, rows=slice(None)):
    for h in range(HEADS):
        ref[rows, h * DK:h * DK + CHUNK] = vals[h].astype(ref.dtype)
        ref[rows, h * DK + CHUNK:(h + 1) * DK] = jnp.zeros(vals[h].shape, ref.dtype)


def _group(n_chunks):
    return GDN_GROUP if n_chunks % GDN_GROUP == 0 else 1


def gdn_intra_fwd(name, qkvn, p, alog, dtb, job=None):
    n_tok = qkvn.shape[0]
    n = n_tok // CHUNK
    grp = _group(n)
    hd = HEADS * DK
    rb = grp * CHUNK

    def body(q_ref, k_ref, v_ref, pba_ref, al_ref, dt_ref, u_ref, w_ref, qd_ref, kd_ref, qk_ref, ti_ref, dec_ref):
        rows = [slice(g * CHUNK, (g + 1) * CHUNK) for g in range(grp)]
        cat = lambda ref: [t for r in rows for t in _heads(ref, r)]
        us, ws, qds, kds, qks, decs, tinv = _gdn_intra(cat(q_ref), cat(k_ref), cat(v_ref),
                                                       [pba_ref[r, :].astype(F32) for r in rows], al_ref[...], dt_ref[...])
        for g, r in enumerate(rows):
            part = slice(g * HEADS, (g + 1) * HEADS)
            _put_heads(u_ref, us[part], r)
            _put_heads(w_ref, ws[part], r)
            _put_heads(qd_ref, qds[part], r)
            _put_heads(kd_ref, kds[part], r)
            _put_qk(qk_ref, qks[part], r)
            _put_qk(ti_ref, tinv[part], r)
            for h in range(HEADS):
                dec_ref[g * HEADS + h:g * HEADS + h + 1, :] = jnp.broadcast_to(decs[g * HEADS + h], (1, LANES))

    blk = lambda c: pl.BlockSpec((rb, hd), lambda i: (i, c))
    par = pl.BlockSpec((1, LANES), lambda i: (0, 0))
    return _pcall(
        body, job, name=name, grid=(n // grp,),
        in_specs=[blk(0), blk(1), blk(2), pl.BlockSpec((rb, LANES), lambda i: (i, COL_BA)), par, par],
        out_specs=[blk(0)] * 6 + [pl.BlockSpec((grp * HEADS, LANES), lambda i: (i, 0))],
        out_shape=[S((n_tok, hd), F32)] + [S((n_tok, hd), BF16)] * 4 + [S((n_tok, hd), F32), S((n * HEADS, LANES), F32)],
        semantics=("parallel",))(qkvn, qkvn, qkvn, p, alog, dtb)


SEQ_GROUP = 4


def _decs_at(ref, g):
    return [ref[g * HEADS + h:g * HEADS + h + 1, 0:1] for h in range(HEADS)]


def gdn_seq_fwd(name, u, w, qd, kd, qk, dec):
    n_tok = u.shape[0]
    n = n_tok // CHUNK
    grp = SEQ_GROUP if n % SEQ_GROUP == 0 else 1
    hd = HEADS * DK

    def body(u_ref, w_ref, qd_ref, kd_ref, qk_ref, dec_ref, o_ref, s_ref, st):
        @pl.when(pl.program_id(0) == 0)
        def _():
            st[...] = jnp.zeros_like(st)

        states = [st[h * DK:(h + 1) * DK, :] for h in range(HEADS)]
        for g in range(grp):
            r = slice(g * CHUNK, (g + 1) * CHUNK)
            for h in range(HEADS):
                s_ref[g, h * DK:(h + 1) * DK, :] = states[h].astype(s_ref.dtype)
            outs, states = _gdn_seq(_heads(u_ref, r), _heads(w_ref, r), _heads(qd_ref, r), _heads(kd_ref, r),
                                    _qk_heads(qk_ref, r), _decs_at(dec_ref, g), states)
            _put_heads(o_ref, outs, r)
        for h in range(HEADS):
            st[h * DK:(h + 1) * DK, :] = states[h]

    blk = pl.BlockSpec((grp * CHUNK, hd), lambda i: (i, 0))
    return pl.pallas_call(
        body, name=name, grid=(n // grp,),
        in_specs=[blk] * 5 + [pl.BlockSpec((grp * HEADS, LANES), lambda i: (i, 0))],
        out_specs=[blk, pl.BlockSpec((grp, hd, DK), lambda i: (i, 0, 0))],
        out_shape=[S((n_tok, hd), F32), S((n, hd, DK), BF16)],
        scratch_shapes=[pltpu.VMEM((hd, DK), F32)],
        compiler_params=_cp(("arbitrary",)))(u, w, qd, kd, qk, dec)


def gdn_seq_bwd(name, u, w, qd, kd, qk, dec, states, do):
    n_tok = u.shape[0]
    n = n_tok // CHUNK
    grp = SEQ_GROUP if n % SEQ_GROUP == 0 else 1
    ns = n // grp
    hd = HEADS * DK

    def body(u_ref, w_ref, qd_ref, kd_ref, qk_ref, dec_ref, s_ref, do_ref,
             du_ref, dw_ref, dqd_ref, dkd_ref, dqk_ref, ddec_ref, dst):
        @pl.when(pl.program_id(0) == 0)
        def _():
            dst[...] = jnp.zeros_like(dst)

        d_news = [dst[h * DK:(h + 1) * DK, :] for h in range(HEADS)]
        for g in reversed(range(grp)):
            r = slice(g * CHUNK, (g + 1) * CHUNK)
            states = [s_ref[g, h * DK:(h + 1) * DK, :].astype(F32) for h in range(HEADS)]
            _, vjp = jax.vjp(_gdn_seq, _heads(u_ref, r), _heads(w_ref, r), _heads(qd_ref, r), _heads(kd_ref, r),
                             _qk_heads(qk_ref, r), _decs_at(dec_ref, g), states)
            du, dw, dqd, dkd, dqk, ddec, d_news = vjp((_heads(do_ref, r), d_news))
            _put_heads(du_ref, du, r)
            _put_heads(dw_ref, dw, r)
            _put_heads(dqd_ref, dqd, r)
            _put_heads(dkd_ref, dkd, r)
            _put_qk(dqk_ref, dqk, r)
            for h in range(HEADS):
                ddec_ref[g * HEADS + h:g * HEADS + h + 1, :] = jnp.broadcast_to(ddec[h], (1, LANES))
        for h in range(HEADS):
            dst[h * DK:(h + 1) * DK, :] = d_news[h]

    blk = pl.BlockSpec((grp * CHUNK, hd), lambda i: (ns - 1 - i, 0))
    dspec = pl.BlockSpec((grp * HEADS, LANES), lambda i: (ns - 1 - i, 0))
    return pl.pallas_call(
        body, name=name, grid=(ns,),
        in_specs=[blk] * 5 + [dspec, pl.BlockSpec((grp, hd, DK), lambda i: (ns - 1 - i, 0, 0)), blk],
        out_specs=[blk] * 5 + [dspec],
        out_shape=[S((n_tok, hd), F32)] * 5 + [S((n * HEADS, LANES), F32)],
        scratch_shapes=[pltpu.VMEM((hd, DK), F32)],
        compiler_params=_cp(("arbitrary",)))(u, w, qd, kd, qk, dec, states, do)


def gdn_intra_bwd(name, qkvn, p, alog, dtb, du, dw, dqd, dkd, dqk, ddec, tinv, job=None):
    n_tok = qkvn.shape[0]
    n = n_tok // CHUNK
    grp = _group(n)
    hd = HEADS * DK
    rb = grp * CHUNK

    def body(q_ref, k_ref, v_ref, pba_ref, al_ref, dt_ref, du_ref, dw_ref, dqd_ref, dkd_ref, dqk_ref, ddec_ref, ti_ref,
             dqkv_ref, dpba_ref, dal_ref, ddt_ref):
        @pl.when(pl.program_id(0) == 0)
        def _():
            dal_ref[...] = jnp.zeros_like(dal_ref)
            ddt_ref[...] = jnp.zeros_like(ddt_ref)

        rows = [slice(g * CHUNK, (g + 1) * CHUNK) for g in range(grp)]
        cat = lambda ref: [t for r in rows for t in _heads(ref, r)]
        kept = [t for r in rows for t in _qk_heads(ti_ref, r)]
        _, vjp = jax.vjp(lambda *a: _gdn_intra(*a, tinv_saved=kept)[:6], cat(q_ref), cat(k_ref), cat(v_ref),
                         [pba_ref[r, :].astype(F32) for r in rows], al_ref[...], dt_ref[...])
        cts = (cat(du_ref), cat(dw_ref), cat(dqd_ref), cat(dkd_ref), [t for r in rows for t in _qk_heads(dqk_ref, r)],
               [ddec_ref[i:i + 1, 0:1] for i in range(grp * HEADS)])
        dq, dk, dv, dpba, dal, ddt = vjp(cts)
        for g, r in enumerate(rows):
            part = slice(g * HEADS, (g + 1) * HEADS)
            _put_heads(dqkv_ref, dq[part], r, 0)
            _put_heads(dqkv_ref, dk[part], r, HEADS)
            _put_heads(dqkv_ref, dv[part], r, 2 * HEADS)
            dpba_ref[r, :] = dpba[g].astype(dpba_ref.dtype)
        dal_ref[...] += dal
        ddt_ref[...] += ddt

    blk = lambda c: pl.BlockSpec((rb, hd), lambda i: (i, c))
    par = pl.BlockSpec((1, LANES), lambda i: (0, 0))
    return _pcall(
        body, job, name=name, grid=(n // grp,),
        in_specs=[blk(0), blk(1), blk(2), pl.BlockSpec((rb, LANES), lambda i: (i, COL_BA)), par, par]
        + [blk(0)] * 5 + [pl.BlockSpec((grp * HEADS, LANES), lambda i: (i, 0)), blk(0)],
        out_specs=[pl.BlockSpec((rb, 3 * hd), lambda i: (i, 0)), pl.BlockSpec((rb, LANES), lambda i: (i, 0)), par, par],
        out_shape=[S((n_tok, 3 * hd), F32), S((n_tok, LANES), BF16), S((1, LANES), F32), S((1, LANES), F32)],
        semantics=("arbitrary",))(qkvn, qkvn, qkvn, p, alog, dtb, du, dw, dqd, dkd, dqk, ddec, tinv)


def loss_head(name, y, target, tm):
    n_tok, d = y.shape

    def body(y_ref, t_ref, dy_ref, l_ref):
        @pl.when(pl.program_id(0) == 0)
        def _():
            l_ref[...] = jnp.zeros_like(l_ref)

        e = y_ref[...] - t_ref[...]
        dy_ref[...] = e * (1.0 / d)
        l_ref[...] += jnp.sum(e * e, keepdims=True) * (0.5 / d)

    spec = pl.BlockSpec((tm, d), lambda i: (i, 0))
    return pl.pallas_call(
        body, name=name, grid=(n_tok // tm,), in_specs=[spec, spec],
        out_specs=[spec, pl.BlockSpec((1, 1), lambda i: (0, 0))], out_shape=[S((n_tok, d), F32), S((1, 1), F32)],
        compiler_params=_cp(("arbitrary",)))(y, target)


def _tm(n_tok):
    return min(512, n_tok)


def _gconv_tb(n_tok):
    return 1024 if n_tok % 1024 == 0 else _tm(n_tok)


def ffn_fwd(tag, x, w, jobs=None):
    n_tok = x.shape[0]
    tm = _tm(n_tok)
    g1 = (1, n_tok // tm)
    tD = _tok(D_MODEL)(tm)
    (h,) = ew_fwd(tag + "_rms", fn_rms, g1, [(x, tD), (w["norm_pre"], _par(D_MODEL))], [(S((n_tok, D_MODEL), BF16), tD)])
    u = mm(tag + "_in", h, w["w_in"], out_dtype=BF16, job=_take(jobs, "ffn_in"))
    tF = lambda c: _tok(D_FF, c)(tm)
    (a,) = ew_fwd(tag + "_swiglu", fn_swiglu, g1, [(u, tF(0)), (u, tF(1))], [(S((n_tok, D_FF), BF16), tF(0))])
    f = mm(tag + "_out", a, w["w_out"], job=_take(jobs, "ffn_out"))
    fn_res = lambda col, x_, f_, w_: (x_ + 0.5 * _rms(f_, w_),)
    (xo,) = ew_fwd(tag + "_res", fn_res, g1, [(x, tD), (f, tD), (w["norm_post"], _par(D_MODEL))],
                   [(S((n_tok, D_MODEL), F32), tD)])
    return xo, dict(x=x, h=h, u=u, a=a, f=f)


def ffn_bwd(tag, dxo, sv, w, jobs=None):
    n_tok = dxo.shape[0]
    tm = _tm(n_tok)
    g1 = (1, n_tok // tm)
    tD = _tok(D_MODEL)(tm)
    pD = _par(D_MODEL)
    fn_post = lambda col, f_, w_: (0.5 * _rms(f_, w_),)
    df, d_post = ew_bwd(tag + "_res_b", fn_post, g1, [(sv["f"], tD), (w["norm_post"], pD)], [(dxo, tD)],
                        [(0, S((n_tok, D_MODEL), BF16), tD)], [(1, S((1, D_MODEL), F32), pD, False)])
    da = mm(tag + "_out_bx", df, w["w_out"], tb=True, out_dtype=BF16, job=_take(jobs, "ffn_out_bx"))
    d_wout = mm(tag + "_out_bw", sv["a"], df, ta=True, out_dtype=GRAD_DT, job=_take(jobs, "ffn_out_bw"))
    tF = lambda c: _tok(D_FF, c)(tm)
    du = swiglu_bwd(tag + "_swiglu_b", sv["u"], da, tm)
    dh = mm(tag + "_in_bx", du, w["w_in"], tb=True, job=_take(jobs, "ffn_in_bx"))
    d_win = mm(tag + "_in_bw", sv["h"], du, ta=True, out_dtype=GRAD_DT, job=_take(jobs, "ffn_in_bw"), out_blocked=True)
    dx, d_pre = ew_bwd(tag + "_rms_b", fn_rms, g1, [(sv["x"], tD), (w["norm_pre"], pD)], [(dh, tD)],
                       [(0, S((n_tok, D_MODEL), F32), tD)], [(1, S((1, D_MODEL), F32), pD, False)], add=(dxo, tD))
    return dx, dict(norm_pre=d_pre, norm_post=d_post, w_in=d_win, w_out=d_wout)


def mix_fwd(tag, x, w, jobs=None):
    n_tok = x.shape[0]
    tm = _tm(n_tok)
    nt = n_tok // tm
    g1 = (1, nt)
    tD = _tok(D_MODEL)(tm)
    pD = _par(D_MODEL)
    (h,) = ew_fwd(tag + "_rms", fn_rms, g1, [(x, tD), (w["norm_pre"], pD)], [(S((n_tok, D_MODEL), BF16), tD)])
    p = mm(tag + "_in", h, w["w_all"], out_dtype=BF16, job=_take(jobs, "mix_in"))
    qkvn = gdnconv_fwd(tag + "_gconv", p, w["conv_w"], _gconv_tb(n_tok), job=_take(jobs, "gconv"))
    tC = _tokcol()(tm)
    *intra, tinv, dec = gdn_intra_fwd(tag + "_gintra", qkvn, p, w["alog"], w["dtb"], job=_take(jobs, "gintra"))
    intra.append(dec)
    o, states = gdn_seq_fwd(tag + "_gseq", *intra)
    (on,) = ew_fwd(tag + "_gout", fn_gdnout, (HEADS, nt),
                   [(o, tC), (p, _tokcol(COL_Z)(tm)), (w["gdn_norm_w"], _par(LANES))],
                   [(S((n_tok, D_MODEL), BF16), tC)])
    ya = mm(tag + "_go", on, w["gdn_w_o"], job=_take(jobs, "mix_small"))
    (hglu,) = ew_fwd(tag + "_glu", fn_glu, (D_MODEL // LANES, nt),
                     [(p, _tokcol(COL_GLU)(tm)), (p, _tokcol(COL_GLU + D_MODEL // LANES)(tm)),
                      (w["pw1_b"], _parcol(0)), (w["pw1_b"], _parcol(D_MODEL // LANES))],
                     [(S((n_tok, D_MODEL), F32), tC)])
    hc = conv_fwd(tag + "_cconv", hglu, 0, D_MODEL, w["dw_w"], w["dw_b"], tm)
    (hs,) = ew_fwd(tag + "_ln", fn_lnsilu, g1, [(hc, tD), (w["ln_g"], pD), (w["ln_b"], pD)],
                   [(S((n_tok, D_MODEL), BF16), tD)])
    yb = mm(tag + "_co", hs, w["cnv_w_o"], job=_take(jobs, "mix_small"))
    tG = lambda cb: pl.BlockSpec((tm, D_MODEL), lambda j, i: (i, cb))
    gcol = (W_QKV + W_Z + W_GLU) // D_MODEL
    (ym,) = ew_fwd(tag + "_merge", fn_merge, g1, [(ya, tD), (yb, tD), (p, tG(gcol)), (p, tG(gcol + 1)), (w["b_o"], pD)],
                   [(S((n_tok, D_MODEL), BF16), tD)])
    y = mm(tag + "_wo", ym, w["w_out"], job=_take(jobs, "mix_small"))
    fn_res = lambda col, x_, f_, w_: (x_ + _rms(f_, w_),)
    (xo,) = ew_fwd(tag + "_res", fn_res, g1, [(x, tD), (y, tD), (w["norm_post"], pD)], [(S((n_tok, D_MODEL), F32), tD)])
    sv = dict(x=x, h=h, p=p, qkvn=qkvn, intra=intra, tinv=tinv, states=states, o=o, on=on, ya=ya, hglu=hglu, hc=hc, hs=hs, yb=yb, ym=ym, y=y)
    return xo, sv


def mix_bwd(tag, dxo, sv, w, jobs=None):
    n_tok = dxo.shape[0]
    tm = _tm(n_tok)
    nt = n_tok // tm
    g1 = (1, nt)
    tD = _tok(D_MODEL)(tm)
    pD = _par(D_MODEL)
    tC = _tokcol()(tm)
    p = sv["p"]
    sD = lambda dt: S((n_tok, D_MODEL), dt)
    fn_post = lambda col, f_, w_: (_rms(f_, w_),)
    dy, d_post = ew_bwd(tag + "_res_b", fn_post, g1, [(sv["y"], tD), (w["norm_post"], pD)], [(dxo, tD)],
                        [(0, sD(BF16), tD)], [(1, S((1, D_MODEL), F32), pD, False)])
    dym = mm(tag + "_wo_bx", dy, w["w_out"], tb=True, job=_take(jobs, "mix_small"))
    d_wout = mm(tag + "_wo_bw", sv["ym"], dy, ta=True, out_dtype=GRAD_DT, job=_take(jobs, "mix_small"))
    tG = lambda cb: pl.BlockSpec((tm, D_MODEL), lambda j, i: (i, cb))
    gcol = (W_QKV + W_Z + W_GLU) // D_MODEL
    dya, dyb, dga, dgb, d_bo = ew_bwd(
        tag + "_merge_b", fn_merge, g1, [(sv["ya"], tD), (sv["yb"], tD), (p, tG(gcol)), (p, tG(gcol + 1)), (w["b_o"], pD)],
        [(dym, tD)], [(0, sD(BF16), tD), (1, sD(BF16), tD), (2, sD(BF16), tD), (3, sD(BF16), tD)],
        [(4, S((1, D_MODEL), F32), pD, False)])
    dhs = mm(tag + "_co_bx", dyb, w["cnv_w_o"], tb=True, job=_take(jobs, "mix_small"))
    d_cwo = mm(tag + "_co_bw", sv["hs"], dyb, ta=True, out_dtype=GRAD_DT, job=_take(jobs, "mix_small"))
    dhc, d_lng, d_lnb = ew_bwd(tag + "_ln_b", fn_lnsilu, g1, [(sv["hc"], tD), (w["ln_g"], pD), (w["ln_b"], pD)], [(dhs, tD)],
                               [(0, sD(F32), tD)], [(1, S((1, D_MODEL), F32), pD, False), (2, S((1, D_MODEL), F32), pD, False)])
    dhglu, d_dww, d_dwb = conv_bwd(tag + "_cconv_b", sv["hglu"], 0, D_MODEL, w["dw_w"], dhc, F32, tm)
    nc = D_MODEL // LANES
    dpa, dpg, d_ba, d_bg = ew_bwd(
        tag + "_glu_b", fn_glu, (nc, nt),
        [(p, _tokcol(COL_GLU)(tm)), (p, _tokcol(COL_GLU + nc)(tm)), (w["pw1_b"], _parcol(0)), (w["pw1_b"], _parcol(nc))],
        [(dhglu, tC)], [(0, sD(BF16), tC), (1, sD(BF16), tC)],
        [(2, S((1, D_MODEL), F32), _parcol(0), False), (3, S((1, D_MODEL), F32), _parcol(0), False)])
    don = mm(tag + "_go_bx", dya, w["gdn_w_o"], tb=True, job=_take(jobs, "mix_small"))
    d_gwo = mm(tag + "_go_bw", sv["on"], dya, ta=True, out_dtype=GRAD_DT, job=_take(jobs, "mix_small"))
    do, dz, d_gnw = ew_bwd(tag + "_gout_b", fn_gdnout, (HEADS, nt),
                           [(sv["o"], tC), (p, _tokcol(COL_Z)(tm)), (w["gdn_norm_w"], _par(LANES))], [(don, tC)],
                           [(0, sD(F32), tC), (1, sD(BF16), tC)], [(2, S((1, LANES), F32), _par(LANES), True)])
    d_intra = gdn_seq_bwd(tag + "_gseq_b", *sv["intra"], sv["states"], do)
    dqkvn, dpba, d_alog, d_dtb = gdn_intra_bwd(tag + "_gintra_b", sv["qkvn"], p, w["alog"], w["dtb"], *d_intra,
                                               sv["tinv"], job=_take(jobs, "gintra_b"))
    dqkv, d_convw = gdnconv_bwd(tag + "_gconv_b", p, w["conv_w"], dqkvn, _gconv_tb(n_tok), job=_take(jobs, "gconv_b"))
    nd = D_MODEL // LANES
    pieces = [(dqkv, 0, 0), (dz, W_QKV // D_MODEL, COL_Z), (dpa, COL_GLU // nd, COL_GLU), (dpg, COL_GLU // nd + 1, COL_GLU + nd),
              (dga, COL_GATE // nd, COL_GATE), (dgb, COL_GATE // nd + 1, COL_GATE + nd), (dpba, COL_BA, COL_BA)]
    dh = mm_nt_sum(tag + "_in_bx", [(a, blk) for a, blk, _ in pieces], w["w_all"])
    d_wall = [mm(tag + "_in_bw", sv["h"], a, ta=True, out_dtype=GRAD_DT, job=_take(jobs, "mix_small"))
              for a, _, _ in pieces]
    dx, d_pre = ew_bwd(tag + "_rms_b", fn_rms, g1, [(sv["x"], tD), (w["norm_pre"], pD)], [(dh, tD)],
                       [(0, sD(F32), tD)], [(1, S((1, D_MODEL), F32), pD, False)], add=(dxo, tD))
    grads = dict(norm_pre=d_pre, norm_post=d_post, w_all=d_wall, conv_w=d_convw, alog=d_alog, dtb=d_dtb,
                 gdn_norm_w=d_gnw, gdn_w_o=d_gwo, pw1_b=jnp.concatenate([d_ba, d_bg], axis=1), dw_w=d_dww,
                 dw_b=d_dwb, ln_g=d_lng, ln_b=d_lnb, cnv_w_o=d_cwo, b_o=d_bo, w_out=d_wout)
    return dx, grads


def local_step(x, target, layers):
    saved = []
    for lw in layers:
        x, sv = layer_fwd(x, lw)
        saved.append(sv)
    dx, loss = loss_head("loss", x, target, _tm(x.shape[0]))
    grads = [None] * len(layers)
    for i in reversed(range(len(layers))):
        dx, grads[i] = layer_bwd(dx, saved[i], layers[i])
    return loss, dx, grads


def layer_fwd(x, lw, jobs=None):
    x, s1 = ffn_fwd("ffn", x, lw["ffn1"], jobs)
    x, s2 = mix_fwd("mix", x, lw["mix"], jobs)
    x, s3 = ffn_fwd("ffn", x, lw["ffn2"], jobs)
    return x, (s1, s2, s3)


def layer_bwd(dx, saved, lw, jobs=None):
    s1, s2, s3 = saved
    dx, g3 = ffn_bwd("ffn", dx, s3, lw["ffn2"], jobs)
    dx, g2 = mix_bwd("mix", dx, s2, lw["mix"], jobs)
    dx, g1 = ffn_bwd("ffn", dx, s1, lw["ffn1"], jobs)
    return dx, dict(ffn1=g1, mix=g2, ffn2=g3)


_O_BA = W_QKV + W_Z
_O_GLU = _O_BA + 2 * HEADS


_MIX_BLK = P_IN // N_BLK
_MIX_B1 = _O_BA - _MIX_BLK
assert _O_BA + HEADS == 2 * _MIX_BLK
BLOCKED = ("ffn1_w_in", "ffn2_w_in", "mix_w_in")


def _row(v):
    return v.reshape(1, -1).astype(F32)


def prep_ffn(wl, k):
    w_in = wl[k + "_w_in"].astype(BF16)
    if w_in.ndim == 2:
        w_in = jnp.transpose(w_in.reshape(w_in.shape[0], N_BLK, -1), (1, 0, 2))
    return dict(norm_pre=_row(wl[k + "_norm_pre"]), norm_post=_row(wl[k + "_norm_post"]), w_in=w_in,
                w_out=wl[k + "_w_out"].astype(BF16))


def prep_layer(wl):
    return dict(ffn1=prep_ffn(wl, "ffn1"), mix=prep_mix(wl), ffn2=prep_ffn(wl, "ffn2"))


def prep_mix(wl):
    row = _row
    bf = lambda v: v.astype(BF16)
    lanes8 = lambda v: jnp.zeros((1, LANES), F32).at[0, HEADS:2 * HEADS].set(v.astype(F32))
    mw = bf(wl["mix_w_in"])
    pad = jnp.zeros((D_MODEL, LANES - 2 * HEADS), BF16)
    if mw.ndim == 3:
        w_all = jnp.concatenate([mw[0], mw[1][:, :_MIX_B1], mw[2][:, HEADS:], mw[3], mw[1][:, _MIX_B1:],
                                 mw[2][:, :HEADS], pad], axis=1)
    else:
        w_all = jnp.concatenate([mw[:, :_O_BA], mw[:, _O_GLU:], mw[:, _O_BA:_O_GLU], pad], axis=1)
    return dict(norm_pre=row(wl["mix_norm_pre"]), norm_post=row(wl["mix_norm_post"]), w_all=w_all,
               conv_w=wl["gdn_conv_w"].astype(F32), alog=lanes8(wl["gdn_a_log"]), dtb=lanes8(wl["gdn_dt_bias"]),
               gdn_norm_w=row(wl["gdn_norm_w"]), gdn_w_o=bf(wl["gdn_w_o"]), pw1_b=row(wl["cnv_pw1_b"]),
               dw_w=wl["cnv_dw_w"].astype(F32), dw_b=row(wl["cnv_dw_b"]), ln_g=row(wl["cnv_ln_g"]),
               ln_b=row(wl["cnv_ln_b"]), cnv_w_o=bf(wl["cnv_w_o"]), b_o=row(wl["cnv_b_o"]), w_out=bf(wl["mix_w_out"]))


def unprep_grads(g):
    return {**unprep_ffn(g["ffn1"], "ffn1"), **unprep_mix(g["mix"]), **unprep_ffn(g["ffn2"], "ffn2")}


def unprep_ffn(g, k):
    blk = g["w_in"]
    return {k + "_norm_pre": g["norm_pre"][0], k + "_norm_post": g["norm_post"][0], k + "_w_in#blocks": blk,
            k + "_w_in": jnp.transpose(blk, (1, 0, 2)).reshape(blk.shape[1], N_BLK * blk.shape[2]),
            k + "_w_out": g["w_out"]}


def unprep_mix(m):
    dqkv, dz, dpa, dpg, dga, dgb, dba = m["w_all"]
    out = {}
    out["mix_w_in#blocks"] = jnp.stack([
        dqkv[:, :_MIX_BLK], jnp.concatenate([dqkv[:, _MIX_BLK:], dz, dba[:, :HEADS]], axis=1),
        jnp.concatenate([dba[:, HEADS:2 * HEADS], dpa, dpg[:, :_MIX_B1 - D_MODEL]], axis=1),
        jnp.concatenate([dpg[:, _MIX_B1 - D_MODEL:], dga, dgb], axis=1)])
    out.update(
        mix_norm_pre=m["norm_pre"][0], mix_norm_post=m["norm_post"][0],
        mix_w_in=jnp.concatenate([dqkv, dz, dba[:, :2 * HEADS], dpa, dpg, dga, dgb], axis=1),
        gdn_conv_w=m["conv_w"], gdn_a_log=m["alog"][0, HEADS:2 * HEADS], gdn_dt_bias=m["dtb"][0, HEADS:2 * HEADS],
        gdn_norm_w=m["gdn_norm_w"][0], gdn_w_o=m["gdn_w_o"], cnv_pw1_b=m["pw1_b"][0], cnv_dw_w=m["dw_w"],
        cnv_dw_b=m["dw_b"][0], cnv_ln_g=m["ln_g"][0], cnv_ln_b=m["ln_b"][0], cnv_w_o=m["cnv_w_o"], cnv_b_o=m["b_o"][0],
        mix_w_out=m["w_out"])
    return out


MESH = pl.DeviceIdType.MESH
ANY = pl.BlockSpec(memory_space=pl.ANY)
N_DEV = 8


def _pos():
    return lax.axis_index("x"), lax.axis_index("y"), lax.axis_index("c")


def _other_chips(x, y):
    return [(1 - x, y), (x, 1 - y), (1 - x, 1 - y)]


class Job:
    def __init__(self, ins, outs, n_sems, copies, aliases=None):
        self.ins, self.outs, self.n_sems, self.copies = ins, outs, n_sems, copies
        self.aliases = aliases or {}
        self.results = None
        self.host = None

    def scratch(self):
        return [pltpu.SemaphoreType.DMA((self.n_sems,)), pltpu.SemaphoreType.DMA((self.n_sems,))]

    def start(self, in_refs, out_refs, sems):
        for cp in self.copies(in_refs, out_refs, sems, False):
            cp.start()

    def finish(self, in_refs, out_refs, sems):
        for cp in self.copies(in_refs, out_refs, sems, True):
            cp.wait_recv()
        for cp in self.copies(in_refs, out_refs, sems, False):
            cp.wait_send()


def run_job(name, job):
    n_i, n_o = len(job.ins), len(job.outs)

    def body(*refs):
        in_refs, out_refs, sems = refs[:n_i], refs[n_i:n_i + n_o], refs[n_i + n_o:]
        job.start(in_refs, out_refs, sems)
        job.finish(in_refs, out_refs, sems)

    job.results = pl.pallas_call(
        body, name=name, in_specs=[ANY] * n_i, out_specs=[ANY] * n_o, out_shape=job.outs,
        input_output_aliases=job.aliases, scratch_shapes=job.scratch())(*job.ins)
    return job.results


def _halved(rows):
    return rows % 32 == 0


def job_gather_ici(shards):
    n = len(shards)

    def copies(in_refs, out_refs, sems, recv):
        x, y, c = _pos()
        b = 2 * x + y
        chips = _other_chips(x, y)
        cps = []
        for a in range(n):
            hr = shards[a].shape[0] // 2
            for j in range(3):
                blk = 2 * chips[j][0] + chips[j][1] if recv else b
                if _halved(shards[a].shape[0]):
                    src, dst = in_refs[a].at[pl.ds(c * hr, hr)], out_refs[a].at[blk, pl.ds(c * hr, hr)]
                else:
                    src, dst = in_refs[a], out_refs[a].at[blk]
                cps.append(pltpu.make_async_remote_copy(
                    src_ref=src, dst_ref=dst, send_sem=sems[0].at[3 * a + j], recv_sem=sems[1].at[3 * a + j],
                    device_id=(chips[j][0], chips[j][1], c), device_id_type=MESH))
        return cps

    return Job(list(shards), [S((N_BLK,) + w.shape, w.dtype) for w in shards], 3 * n, copies)


def job_gather_sibling(lands):
    idx = [a for a, w in enumerate(lands) if _halved(w.shape[1])]

    def copies(in_refs, out_refs, sems, recv):
        x, y, c = _pos()
        chips = _other_chips(x, y)
        half = 1 - c if recv else c
        cps = []
        for pos, a in enumerate(idx):
            hr = lands[a].shape[1] // 2
            for j in range(3):
                rows = out_refs[a].at[2 * chips[j][0] + chips[j][1], pl.ds(half * hr, hr)]
                cps.append(pltpu.make_async_remote_copy(
                    src_ref=rows, dst_ref=rows, send_sem=sems[0].at[3 * pos + j], recv_sem=sems[1].at[3 * pos + j],
                    device_id=(x, y, 1 - c), device_id_type=MESH))
        return cps

    return Job(list(lands), [S(w.shape, w.dtype) for w in lands], 3 * len(idx), copies,
               aliases={a: a for a in range(len(lands))})


def job_rs_chips(ps):
    n = len(ps)

    def copies(in_refs, out_refs, sems, recv):
        x, y, c = _pos()
        b = 2 * x + y
        chips = _other_chips(x, y)
        cps = []
        for k in range(n):
            for j in range(3):
                other = 2 * chips[j][0] + chips[j][1]
                src_blk, dst_slot = (b, other) if recv else (other, b)
                cps.append(pltpu.make_async_remote_copy(
                    src_ref=in_refs[k].at[src_blk], dst_ref=out_refs[k].at[dst_slot], send_sem=sems[0].at[3 * k + j],
                    recv_sem=sems[1].at[3 * k + j], device_id=(chips[j][0], chips[j][1], c), device_id_type=MESH))
        return cps

    return Job(list(ps), [S(p.shape, p.dtype) for p in ps], 3 * n, copies)


def rs_sibling(gs):
    n = len(gs)

    def body(*refs):
        g_refs, r_refs = refs[:n], refs[n:2 * n]
        send_sems, recv_sems = refs[2 * n:]
        x, y, c = _pos()

        def cp(k):
            hr = gs[k].shape[1] // 2
            return pltpu.make_async_remote_copy(
                src_ref=g_refs[k].at[:, pl.ds((1 - c) * hr, hr)], dst_ref=r_refs[k], send_sem=send_sems.at[k],
                recv_sem=recv_sems.at[k], device_id=(x, y, 1 - c), device_id_type=MESH)

        cps = [cp(k) for k in range(n)]
        for d in cps:
            d.start()
        for d in cps:
            d.wait_recv()
        for d in cps:
            d.wait_send()

    return pl.pallas_call(
        body, name="rs_sibling", in_specs=[ANY] * n, out_specs=[ANY] * n,
        out_shape=[S((N_BLK, g.shape[1] // 2, g.shape[2]), g.dtype) for g in gs],
        scratch_shapes=[pltpu.SemaphoreType.DMA((n,)), pltpu.SemaphoreType.DMA((n,))])(*gs)


def ag_sibling(fs):
    n = len(fs)

    def body(*refs):
        o_refs = refs[n:2 * n]
        send_sems, recv_sems = refs[2 * n:]
        x, y, c = _pos()

        def cp(k, half):
            hr = fs[k].shape[0] // 2
            rows = o_refs[k].at[pl.ds(half * hr, hr)]
            return pltpu.make_async_remote_copy(
                src_ref=rows, dst_ref=rows, send_sem=send_sems.at[k], recv_sem=recv_sems.at[k],
                device_id=(x, y, 1 - c), device_id_type=MESH)

        cps = [cp(k, c) for k in range(n)]
        for d in cps:
            d.start()
        for k in range(n):
            cp(k, 1 - c).wait_recv()
        for d in cps:
            d.wait_send()

    return pl.pallas_call(
        body, name="ag_sibling", in_specs=[ANY] * n, out_specs=[ANY] * n,
        out_shape=[S(f.shape, f.dtype) for f in fs], input_output_aliases={k: k for k in range(n)},
        scratch_shapes=[pltpu.SemaphoreType.DMA((n,)), pltpu.SemaphoreType.DMA((n,))])(*fs)


def allreduce_small(v):
    rows = v.shape[0]

    def body(v_ref, o_ref, buf, send_sems, recv_sems):
        x, y, c = _pos()
        me = 4 * x + 2 * y + c
        buf[me] = v_ref[...]

        def cp(d, slot):
            dx, dy, dc = (d >> 2) & 1, (d >> 1) & 1, d & 1
            peer = (1 - x if dx else x, 1 - y if dy else y, 1 - c if dc else c)
            return pltpu.make_async_remote_copy(
                src_ref=v_ref, dst_ref=buf.at[slot], send_sem=send_sems.at[d - 1], recv_sem=recv_sems.at[d - 1],
                device_id=peer, device_id_type=MESH)

        cps = [cp(d, me) for d in range(1, N_DEV)]
        for d in cps:
            d.start()
        for d in range(1, N_DEV):
            dx, dy, dc = (d >> 2) & 1, (d >> 1) & 1, d & 1
            src = 4 * (1 - x if dx else x) + 2 * (1 - y if dy else y) + (1 - c if dc else c)
            cp(d, src).wait_recv()
        for d in cps:
            d.wait_send()
        acc = buf[0]
        for s in range(1, N_DEV):
            acc = acc + buf[s]
        o_ref[...] = acc

    vm = pl.BlockSpec(memory_space=pltpu.VMEM)
    return pl.pallas_call(
        body, name="allreduce_small", in_specs=[vm], out_specs=vm, out_shape=S(v.shape, v.dtype),
        scratch_shapes=[pltpu.VMEM((N_DEV, rows, LANES), F32), pltpu.SemaphoreType.DMA((N_DEV - 1,)),
                        pltpu.SemaphoreType.DMA((N_DEV - 1,))])(v)


def _rows_tile(rows, cols, cap_bytes=1 << 20, mult=8):
    best = None
    for t in range(mult, rows + 1, mult):
        if rows % t == 0 and t * cols * 4 <= cap_bytes:
            best = t
    return best if best is not None else rows


def add_half(name, g, r, c_arr):
    _, hr, cols = r.shape
    tr = _rows_tile(hr, cols, mult=16)
    nb = hr // tr

    def body(c_ref, g_ref, r_ref, o_ref):
        o_ref[...] = (g_ref[...].astype(F32) + r_ref[...].astype(F32)).astype(o_ref.dtype)

    gs = pltpu.PrefetchScalarGridSpec(
        num_scalar_prefetch=1, grid=(N_BLK, nb),
        in_specs=[pl.BlockSpec((None, tr, cols), lambda b, i, cr: (b, cr[0] * nb + i, 0)),
                  pl.BlockSpec((None, tr, cols), lambda b, i, cr: (b, i, 0))],
        out_specs=pl.BlockSpec((None, tr, cols), lambda b, i, cr: (b, i, 0)))
    return pl.pallas_call(body, name=name, grid_spec=gs, out_shape=S(r.shape, BF16),
                          compiler_params=_cp(("parallel", "parallel")))(c_arr, g, r)


def sum_chips(name, r, own, cb_arr):
    _, hr, cols = r.shape
    tr = _rows_tile(hr, cols, mult=16)
    nb = hr // tr

    def body(cb_ref, *refs):
        o_ref = refs[N_BLK + 1]
        b = cb_ref[1]
        acc = None
        for s in range(N_BLK):
            term = jnp.where(b == s, refs[N_BLK][...], refs[s][...]).astype(F32)
            acc = term if acc is None else acc + term
        o_ref[...] = acc

    slot = lambda s: pl.BlockSpec((None, tr, cols), lambda i, cb: (jnp.where(cb[1] == s, (s + 1) % N_BLK, s), i, 0))
    gs = pltpu.PrefetchScalarGridSpec(
        num_scalar_prefetch=1, grid=(nb,),
        in_specs=[slot(s) for s in range(N_BLK)] + [pl.BlockSpec((None, tr, cols), lambda i, cb: (cb[1], i, 0))],
        out_specs=pl.BlockSpec((tr, cols), lambda i, cb: (cb[0] * nb + i, 0)))
    return pl.pallas_call(body, name=name, grid_spec=gs, out_shape=S((2 * hr, cols), F32),
                          compiler_params=_cp(("parallel",)))(cb_arr, *([r] * N_BLK), own)


def adamw(name, w, m, v, gs):
    rows, cols = w.shape
    two = len(gs) == 2
    span = rows // 2 if two else rows
    tr = _rows_tile(span, cols, 1 << 21)
    nb = span // tr

    def body(w_ref, m_ref, v_ref, *rest):
        g_refs, (go_ref, d_ref, mo_ref, vo_ref) = rest[:len(gs)], rest[len(gs):]
        if two:
            g = jnp.where(pl.program_id(0) < nb, g_refs[0][...], g_refs[1][...])
        else:
            g = g_refs[0][...]
        mn = ADAM_B1 * m_ref[...] + (1.0 - ADAM_B1) * g
        vn = ADAM_B2 * v_ref[...] + (1.0 - ADAM_B2) * jnp.square(g)
        m_hat = mn / (1.0 - ADAM_B1 ** ADAM_STEP)
        v_hat = vn / (1.0 - ADAM_B2 ** ADAM_STEP)
        go_ref[...] = g
        d_ref[...] = -ADAM_LR * (m_hat / (jnp.sqrt(v_hat) + ADAM_EPS) + ADAM_WD * w_ref[...])
        mo_ref[...] = mn
        vo_ref[...] = vn

    full = pl.BlockSpec((tr, cols), lambda i: (i, 0))
    if two:
        g_specs = [pl.BlockSpec((tr, cols), lambda i: (jnp.minimum(i, nb - 1), 0)),
                   pl.BlockSpec((tr, cols), lambda i: (jnp.maximum(i - nb, 0), 0))]
    else:
        g_specs = [full]
    return pl.pallas_call(
        body, name=name, grid=(2 * nb if two else nb,), in_specs=[full, full, full] + g_specs, out_specs=[full] * 4,
        out_shape=[S((rows, cols), F32)] * 4, compiler_params=_cp(("parallel",)))(w, m, v, *gs)


WEIGHTS = ["ffn1_norm_pre", "ffn1_norm_post", "ffn1_w_in", "ffn1_w_out", "mix_norm_pre", "mix_norm_post", "mix_w_in",
           "gdn_conv_w", "gdn_a_log", "gdn_dt_bias", "gdn_norm_w", "gdn_w_o", "cnv_pw1_b", "cnv_dw_w", "cnv_dw_b",
           "cnv_ln_g", "cnv_ln_b", "cnv_w_o", "cnv_b_o", "mix_w_out", "ffn2_norm_pre", "ffn2_norm_post", "ffn2_w_in",
           "ffn2_w_out"]
BIG = {"ffn1_w_in": True, "ffn1_w_out": False, "mix_w_in": True, "gdn_conv_w": True, "gdn_w_o": False,
       "cnv_dw_w": True, "cnv_w_o": False, "mix_w_out": False, "ffn2_w_in": True, "ffn2_w_out": False}
TINY = {"gdn_conv_w": (32, LANES), "cnv_dw_w": (64, LANES)}
SMALL = [n for n in WEIGHTS if n not in BIG]
SUB = {"ffn1": ["ffn1_w_in", "ffn1_w_out"], "ffn2": ["ffn2_w_in", "ffn2_w_out"],
       "mix": ["mix_w_in", "gdn_conv_w", "gdn_w_o", "cnv_dw_w", "cnv_w_o", "mix_w_out"]}
GATHER_ON_FFN_MIX = [("ffn_in", ["mix_w_in"]), ("ffn_out", SUB["mix"][1:])]
RS_ON_FFN_MIX = [("ffn_in_bx", ["mix_w_in"]), ("ffn_out_bx", SUB["mix"][1:])]


def _whole(name, blocks):
    if BIG[name]:
        return jnp.transpose(blocks, (1, 0, 2)).reshape(blocks.shape[1], N_BLK * blocks.shape[2])
    return blocks.reshape(N_BLK * blocks.shape[1], blocks.shape[2])


def _blocks(name, whole):
    r, cfull = whole.shape
    if BIG[name]:
        blk = jnp.transpose(whole.reshape(r, N_BLK, cfull // N_BLK), (1, 0, 2))
    else:
        blk = whole.reshape(N_BLK, r // N_BLK, cfull)
    if name in TINY:
        tr, tc = TINY[name]
        flat = blk.reshape(N_BLK, -1)
        blk = jnp.pad(flat, ((0, 0), (0, tr * tc - flat.shape[1]))).reshape(N_BLK, tr, tc)
    return blk.astype(GRAD_DT)


def _pack(parts):
    rows = []
    for p in parts:
        flat = p.reshape(-1).astype(F32)
        rows.append(jnp.pad(flat, (0, (-flat.shape[0]) % LANES)).reshape(-1, LANES))
    out = jnp.concatenate(rows, axis=0)
    return jnp.pad(out, ((0, (-out.shape[0]) % 8), (0, 0)))


def _unpack(packed, shapes):
    out, r = [], 0
    for shp in shapes:
        size = math.prod(shp)
        nr = -(-size // LANES)
        out.append(packed[r:r + nr].reshape(-1)[:size].reshape(shp))
        r += nr
    return out


def kernel(x, ffn1_norm_pre, ffn1_norm_post, ffn1_w_in, ffn1_w_out, mix_norm_pre, mix_norm_post, mix_w_in, gdn_conv_w, gdn_a_log, gdn_dt_bias, gdn_norm_w, gdn_w_o, cnv_pw1_b, cnv_dw_w, cnv_dw_b, cnv_ln_g, cnv_ln_b, cnv_w_o, cnv_b_o, mix_w_out, ffn2_norm_pre, ffn2_norm_post, ffn2_w_in, ffn2_w_out, loss_target, m_ffn1_norm_pre, m_ffn1_norm_post, m_ffn1_w_in, m_ffn1_w_out, m_mix_norm_pre, m_mix_norm_post, m_mix_w_in, m_gdn_conv_w, m_gdn_a_log, m_gdn_dt_bias, m_gdn_norm_w, m_gdn_w_o, m_cnv_pw1_b, m_cnv_dw_w, m_cnv_dw_b, m_cnv_ln_g, m_cnv_ln_b, m_cnv_w_o, m_cnv_b_o, m_mix_w_out, m_ffn2_norm_pre, m_ffn2_norm_post, m_ffn2_w_in, m_ffn2_w_out, v_ffn1_norm_pre, v_ffn1_norm_post, v_ffn1_w_in, v_ffn1_w_out, v_mix_norm_pre, v_mix_norm_post, v_mix_w_in, v_gdn_conv_w, v_gdn_a_log, v_gdn_dt_bias, v_gdn_norm_w, v_gdn_w_o, v_cnv_pw1_b, v_cnv_dw_w, v_cnv_dw_b, v_cnv_ln_g, v_cnv_ln_b, v_cnv_w_o, v_cnv_b_o, v_mix_w_out, v_ffn2_norm_pre, v_ffn2_norm_post, v_ffn2_w_in, v_ffn2_w_out):
    args = locals()
    wts = {n: args[n] for n in WEIGHTS}
    mom = {n: args["m_" + n] for n in WEIGHTS}
    var = {n: args["v_" + n] for n in WEIGHTS}
    big = list(BIG)

    mx, my, mc = _pos()
    mb = 2 * mx + my
    cb_arr = jnp.stack([mc, mb]).astype(jnp.int32)

    own = {(n, l): wts[n][l].astype(BF16) for n in big for l in range(DEPTH)}

    def planned(make_job, arrays, plan):
        jobs = []
        for host, names in plan:
            jb = make_job([arrays[n] for n in names])
            jb.host, jb.names = host, names
            jobs.append(jb)
        return jobs

    def landed(jobs):
        res = {}
        for jb in jobs:
            if jb.results is None:
                run_job("comm_alone", jb)
            res.update(zip(jb.names, jb.results))
        return res

    stages = [(l, s) for l in range(DEPTH) for s in ("ffn1", "mix", "ffn2")]
    carried_by_ffn = lambda s: GATHER_ON_FFN_MIX if s == "mix" else [("ffn_in", [s + "_w_in"]), ("ffn_out", [s + "_w_out"])]
    carried_by_mix = lambda s: [("mix_in", [s + "_w_in", s + "_w_out"])]

    def gather_jobs(l, s, carrier):
        plan = carried_by_mix(s) if carrier == "mix" else carried_by_ffn(s)
        return planned(job_gather_ici, {n: own[(n, l)] for n in SUB[s]}, plan)

    def sub_weights(l, s, jobs):
        lands = landed(jobs)
        names = SUB[s]
        lands = dict(zip(names, run_job("gather_sib", job_gather_sibling([lands[n] for n in names]))))
        wl = {}
        for n in names:
            blocks = lax.dynamic_update_index_in_dim(lands[n], own[(n, l)], mb, 0)
            wl[n] = blocks if n in BLOCKED else _whole(n, blocks)
        wl.update({n: wts[n][l] for n in SMALL})
        return prep_mix(wl) if s == "mix" else prep_ffn(wl, s)

    act = x[0]
    saved, weights = {}, {}
    jobs = planned(job_gather_ici, {n: own[(n, 0)] for n in SUB["ffn1"]}, [(None, SUB["ffn1"])])
    for i, (l, s) in enumerate(stages):
        weights[(l, s)] = sub_weights(l, s, jobs)
        jobs = gather_jobs(*stages[i + 1], "mix" if s == "mix" else "ffn") if i + 1 < len(stages) else []
        pending = list(jobs)
        fwd = mix_fwd if s == "mix" else ffn_fwd
        act, saved[(l, s)] = fwd("mix" if s == "mix" else "ffn", act, weights[(l, s)], pending)
    dx, loss = loss_head("loss", act, loss_target[0], _tm(act.shape[0]))

    def rs_jobs(l, s, g, carrier):
        if s == "mix":
            gw_s = unprep_mix(g)
        else:
            gw_s = unprep_ffn(g, s)
        small_grads[l].update({n: gw_s[n] for n in gw_s if n in SMALL})
        names = SUB[s]
        blocks = [gw_s[n + "#blocks"] if n in BLOCKED else _blocks(n, gw_s[n]) for n in names]
        parts = [add_half("add_half", b_, r, cb_arr) for b_, r in zip(blocks, rs_sibling(blocks))]
        partial_of.update({(n, l): p for n, p in zip(names, parts)})
        if carrier is None:
            plan = [(None, names)]
        elif carrier == "mix":
            plan = [("gintra_b", names)]
        elif s == "mix":
            plan = RS_ON_FFN_MIX if l == 0 else [("gconv_b", ["mix_w_in"]), ("ffn_out_bx", SUB["mix"][1:])]
        else:
            plan = [("ffn_in_bx", [s + "_w_in"]), ("ffn_out_bx", [s + "_w_out"])]
        return planned(job_rs_chips, dict(zip(names, parts)), plan)

    small_grads = [{} for _ in range(DEPTH)]
    partial_of, chip_of = {}, {}
    jobs, jobs_key, late = [], None, []
    for i, (l, s) in enumerate(reversed(stages)):
        pending = list(jobs) + [jb for jb, _ in late]
        bwd = mix_bwd if s == "mix" else ffn_bwd
        dx, g = bwd("mix" if s == "mix" else "ffn", dx, saved[(l, s)], weights[(l, s)], pending)
        late += [(jb, jobs_key) for jb in jobs if jb.host == "gconv_b"]
        chip_of.update({(n, jobs_key): r for n, r in landed([jb for jb in jobs if jb.host != "gconv_b"]).items()})
        for jb, key in late:
            if jb.results is not None:
                chip_of.update({(n, key): r for n, r in zip(jb.names, jb.results)})
        late = [e for e in late if e[0].results is None]
        nxt = list(reversed(stages))[i + 1][1] if i + 1 < len(stages) else None
        jobs, jobs_key = rs_jobs(l, s, g, None if nxt is None else ("mix" if nxt == "mix" else "ffn")), l
    for jb, key in late + [(jb, jobs_key) for jb in jobs]:
        chip_of.update({(n, key): r for n, r in landed([jb]).items()})
    gw = small_grads

    small_shapes = [wts[n].shape for n in SMALL]
    packed = _pack([jnp.stack([gw[l][n] for l in range(DEPTH)]) for n in SMALL] + [loss])
    total = allreduce_small(packed)
    small_g = dict(zip(SMALL, _unpack(total, small_shapes)))
    loss_sum = total[sum(-(-math.prod(s) // LANES) for s in small_shapes), 0]

    keys = [(n, l) for l in range(DEPTH) for n in big]
    halves = [sum_chips("sum_chips", chip_of[k], partial_of[k], cb_arr) for k in keys]
    summed = dict(zip(keys, ag_sibling(halves)))

    out_g, out_d, out_m, out_v = {}, {}, {}, {}
    for n in big:
        shp = wts[n].shape
        gs = [summed[(n, l)] for l in range(DEPTH)]
        if n in TINY:
            gs = [jnp.concatenate([g.reshape(-1)[:shp[1] * shp[2]].reshape(shp[1], shp[2]) for g in gs], axis=0)]
        two_d = lambda a: a.reshape(DEPTH * shp[1], shp[2])
        res = adamw("adamw", two_d(wts[n]), two_d(mom[n]), two_d(var[n]), gs)
        out_g[n], out_d[n], out_m[n], out_v[n] = [r.reshape(shp) for r in res]

    pk = lambda d: _pack([d[n] for n in SMALL])
    res = adamw("adamw_small", pk(wts), pk(mom), pk(var), [pk(small_g)])
    for d, r in zip((out_g, out_d, out_m, out_v), res):
        d.update(dict(zip(SMALL, _unpack(r, small_shapes))))

    return (loss_sum, dx[None], *[out_g[n] for n in WEIGHTS], *[out_d[n] for n in WEIGHTS],
            *[out_m[n] for n in WEIGHTS], *[out_v[n] for n in WEIGHTS])
```
